```python
import jax, jax.numpy as jnp
from jax import lax
import numpy as np

D_MODEL = 1024
BATCH = 8
SEQ = 2048
DEPTH = 1
DEC_BATCH = 128
DEC_SEQ = 4
PAST_LEN = 16384
PAGE_SIZE = 128

MIX_WIDTH = D_MODEL
GDN_HEADS = 4
GDN_DK = 128
GDN_DV = MIX_WIDTH // 2 // GDN_HEADS
GLA_HEADS = 4
GLA_DK = 64
GLA_DV = (MIX_WIDTH - GDN_HEADS * GDN_DV) // GLA_HEADS
GLA_GATE_RANK = 16
GLA_GATE_NORMALIZER = 16.0
CONV_WIDTH = 4
CHUNK = 64
N_EXPERTS = 32
TOP_K = 4
D_FF = D_MODEL
SWIGLU_LIMIT = 7.0
SWIGLU_ALPHA = 1.702
MOE_BLOCK = 128
RMS_EPS = 1e-6
L2_EPS = 1e-6

GDN_QK_W = GDN_HEADS * GDN_DK
GDN_V_W = GDN_HEADS * GDN_DV
GDN_CONV_CH = 2 * GDN_QK_W + GDN_V_W
GLA_QK_W = GLA_HEADS * GLA_DK
GLA_V_W = GLA_HEADS * GLA_DV
IN_SPLITS = (GDN_CONV_CH, GDN_V_W, GDN_HEADS, GDN_HEADS, GLA_QK_W, GLA_QK_W, GLA_V_W, GLA_V_W, GLA_GATE_RANK)
IN_DIM = sum(IN_SPLITS)

kernel_name = "hymba_gdn_gla_moe_step"


def rmsnorm(x, w):
    xf = x.astype(jnp.float32)
    y = xf * lax.rsqrt(jnp.mean(xf * xf, axis=-1, keepdims=True) + RMS_EPS)
    return (y * w.astype(jnp.float32)).astype(x.dtype)


def gated_rmsnorm(o, w, z):
    of = o.astype(jnp.float32)
    y = of * lax.rsqrt(jnp.mean(of * of, axis=-1, keepdims=True) + RMS_EPS)
    return y * w.astype(jnp.float32) * jax.nn.silu(z.astype(jnp.float32))


def l2norm(x):
    return x * lax.rsqrt(jnp.sum(x * x, axis=-1, keepdims=True) + L2_EPS)


def split_cols(t, sizes):
    out, off = [], 0
    for s in sizes:
        out.append(t[..., off:off + s])
        off += s
    return out


def to_heads(t, n_heads):
    b, s, w = t.shape
    return t.reshape(b, s, n_heads, w // n_heads).transpose(0, 2, 1, 3)


def pad_time(t, pad):
    widths = [(0, 0)] * t.ndim
    widths[2] = (0, pad)
    return jnp.pad(t, widths)


def causal_conv(u, buf, w):
    t_len = u.shape[1]
    xc = jnp.concatenate([buf.astype(u.dtype), u], axis=1)
    out = xc[:, 0:t_len] * w[0]
    for i in range(1, CONV_WIDTH):
        out = out + xc[:, i:i + t_len] * w[i]
    return jax.nn.silu(out), xc[:, -(CONV_WIDTH - 1):]


def gated_delta_rule(q, k, v, g, beta, s0):
    b, h, t_len, dk = q.shape
    dv = v.shape[-1]
    c = min(CHUNK, t_len)
    pad = (-t_len) % c
    if pad:
        q, k, v, g, beta = (pad_time(a, pad) for a in (q, k, v, g, beta))
    n = (t_len + pad) // c
    q = q.reshape(b, h, n, c, dk)
    k = k.reshape(b, h, n, c, dk)
    v = v.reshape(b, h, n, c, dv)
    g_cum = jnp.cumsum(g.reshape(b, h, n, c), axis=-1)
    beta = beta.reshape(b, h, n, c)
    tril = jnp.tril(jnp.ones((c, c), bool))
    strict = jnp.tril(jnp.ones((c, c), bool), -1)
    decay = jnp.exp(jnp.where(tril, g_cum[..., :, None] - g_cum[..., None, :], -jnp.inf))
    kb = k * beta[..., None]
    vb = v * beta[..., None]
    m = jnp.where(strict, jnp.einsum('bhnid,bhnjd->bhnij', kb, k) * decay, 0.0)
    eye = jnp.eye(c, dtype=m.dtype)
    t_inv = lax.linalg.triangular_solve(eye + m, jnp.broadcast_to(eye, m.shape), left_side=True,
                                        lower=True, unit_diagonal=True)
    u = jnp.einsum('bhnij,bhnjd->bhnid', t_inv, vb)
    w = jnp.einsum('bhnij,bhnjd->bhnid', t_inv, kb * jnp.exp(g_cum)[..., None])
    a_intra = jnp.einsum('bhnid,bhnjd->bhnij', q, k) * decay

    def step(s, inp):
        q_c, k_c, u_c, w_c, gc_c, a_c = inp
        v_new = u_c - jnp.einsum('bhcd,bhde->bhce', w_c, s)
        o = jnp.einsum('bhcd,bhde->bhce', q_c * jnp.exp(gc_c)[..., None], s) + \
            jnp.einsum('bhij,bhje->bhie', a_c, v_new)
        g_last = gc_c[..., -1]
        s = s * jnp.exp(g_last)[..., None, None] + \
            jnp.einsum('bhcd,bhce->bhde', k_c * jnp.exp(g_last[..., None] - gc_c)[..., None], v_new)
        return s, o

    xs = tuple(jnp.moveaxis(a, 2, 0) for a in (q, k, u, w, g_cum, a_intra))
    s_fin, o = lax.scan(step, s0, xs)
    o = jnp.moveaxis(o, 0, 2).reshape(b, h, n * c, dv)[:, :, :t_len]
    return o, s_fin


def gla_chunked(q, k, v, gk, s0):
    b, h, t_len, dk = q.shape
    dv = v.shape[-1]
    c = min(CHUNK, t_len)
    pad = (-t_len) % c
    if pad:
        q, k, v, gk = (pad_time(a, pad) for a in (q, k, v, gk))
    n = (t_len + pad) // c
    q = q.reshape(b, h, n, c, dk)
    k = k.reshape(b, h, n, c, dk)
    v = v.reshape(b, h, n, c, dv)
    b_cum = jnp.cumsum(gk.reshape(b, h, n, c, dk), axis=-2)
    tril = jnp.tril(jnp.ones((c, c), bool))[:, :, None]

    def step(s, inp):
        q_c, k_c, v_c, b_c = inp
        o_inter = jnp.einsum('bhcd,bhde->bhce', q_c * jnp.exp(b_c), s)
        dec = jnp.exp(jnp.where(tril, b_c[:, :, :, None, :] - b_c[:, :, None, :, :], -jnp.inf))
        a = jnp.einsum('bhid,bhjd,bhijd->bhij', q_c, k_c, dec)
        o = o_inter + jnp.einsum('bhij,bhje->bhie', a, v_c)
        b_last = b_c[:, :, -1]
        s = jnp.exp(b_last)[..., None] * s + \
            jnp.einsum('bhcd,bhce->bhde', k_c * jnp.exp(b_last[:, :, None] - b_c), v_c)
        return s, o

    xs = tuple(jnp.moveaxis(a, 2, 0) for a in (q, k, v, b_cum))
    s_fin, o = lax.scan(step, s0, xs)
    o = jnp.moveaxis(o, 0, 2).reshape(b, h, n * c, dv)[:, :, :t_len]
    return o, s_fin


def token_mixers(h, conv_buf, s_gdn, s_gla, w_in, conv_w, gdn_a_log, gdn_dt_bias, gdn_norm_w,
                 gla_gk_w, gla_gk_b, gla_norm_w, w_out):
    f32 = jnp.float32
    b, t_len, _ = h.shape
    proj = h @ w_in
    qkv_raw, gdn_z, gdn_a, gdn_b, gla_q, gla_k, gla_v, gla_g, gla_lr = split_cols(proj, IN_SPLITS)

    qkv, conv_new = causal_conv(qkv_raw, conv_buf, conv_w)
    q, k, v = split_cols(qkv.astype(f32), (GDN_QK_W, GDN_QK_W, GDN_V_W))
    q = l2norm(to_heads(q, GDN_HEADS)) * (GDN_DK ** -0.5)
    k = l2norm(to_heads(k, GDN_HEADS))
    v = to_heads(v, GDN_HEADS)
    beta = jax.nn.sigmoid(gdn_b.astype(f32)).transpose(0, 2, 1)
    g = (-jnp.exp(gdn_a_log.astype(f32)) *
         jax.nn.softplus(gdn_a.astype(f32) + gdn_dt_bias.astype(f32))).transpose(0, 2, 1)
    o_gdn, s_gdn_new = gated_delta_rule(q, k, v, g, beta, s_gdn.astype(f32))
    o_gdn = gated_rmsnorm(o_gdn.transpose(0, 2, 1, 3), gdn_norm_w,
                          gdn_z.reshape(b, t_len, GDN_HEADS, GDN_DV)).reshape(b, t_len, GDN_V_W)

    gk = jax.nn.log_sigmoid((gla_lr @ gla_gk_w + gla_gk_b).astype(f32)) / GLA_GATE_NORMALIZER
    o_gla, s_gla_new = gla_chunked(to_heads(gla_q.astype(f32), GLA_HEADS) * (GLA_DK ** -0.5),
                                   to_heads(gla_k.astype(f32), GLA_HEADS),
                                   to_heads(gla_v.astype(f32), GLA_HEADS),
                                   to_heads(gk, GLA_HEADS), s_gla.astype(f32))
    o_gla = gated_rmsnorm(o_gla.transpose(0, 2, 1, 3), gla_norm_w,
                          gla_g.reshape(b, t_len, GLA_HEADS, GLA_DV)).reshape(b, t_len, GLA_V_W)

    out = jnp.concatenate([o_gdn, o_gla], axis=-1).astype(h.dtype) @ w_out
    return (out, conv_new.astype(conv_buf.dtype), s_gdn_new.astype(s_gdn.dtype),
            s_gla_new.astype(s_gla.dtype))


def moe(x2d, w_router, b_router, w_up, b_up, w_down, b_down):
    f32 = jnp.float32
    n_tok, d = x2d.shape
    logits = x2d.astype(f32) @ w_router.astype(f32) + b_router.astype(f32)
    top_v, top_i = lax.top_k(logits, TOP_K)
    gates = jax.nn.softmax(top_v, axis=-1)
    n_assign = n_tok * TOP_K
    flat_e = top_i.reshape(-1)
    flat_tok = jnp.arange(n_assign, dtype=jnp.int32) // TOP_K
    flat_g = gates.reshape(-1)
    order = jnp.argsort(flat_e)
    se = flat_e[order]
    counts = jnp.bincount(flat_e, length=N_EXPERTS)
    padded = (counts + MOE_BLOCK - 1) // MOE_BLOCK * MOE_BLOCK
    pend = jnp.cumsum(padded)
    pstart = pend - padded
    ustart = jnp.cumsum(counts) - counts
    dest = pstart[se] + (jnp.arange(n_assign, dtype=jnp.int32) - ustart[se])
    n_rows = (n_assign + MOE_BLOCK - 1) // MOE_BLOCK * MOE_BLOCK + N_EXPERTS * MOE_BLOCK
    n_blocks = n_rows // MOE_BLOCK
    row_tok = jnp.full((n_rows,), n_tok, jnp.int32).at[dest].set(flat_tok[order])
    row_gate = jnp.zeros((n_rows,), f32).at[dest].set(flat_g[order])
    block_e = jnp.minimum(jnp.searchsorted(pend, jnp.arange(n_blocks) * MOE_BLOCK, side='right'),
                          N_EXPERTS - 1)
    x_pad = jnp.concatenate([x2d, jnp.zeros((1, d), x2d.dtype)], axis=0)
    xb = x_pad[row_tok].reshape(n_blocks, MOE_BLOCK, d)

    def expert_block(args):
        xe, e = args
        gu = xe @ w_up[e] + b_up[e]
        gate = jnp.minimum(gu[:, :D_FF], SWIGLU_LIMIT)
        up = jnp.clip(gu[:, D_FF:], -SWIGLU_LIMIT, SWIGLU_LIMIT)
        act = (up + 1.0) * gate * jax.nn.sigmoid(SWIGLU_ALPHA * gate)
        return act @ w_down[e] + b_down[e]

    yb = lax.map(expert_block, (xb, block_e)).reshape(n_rows, d)
    out = jnp.zeros((n_tok + 1, d), f32).at[row_tok].add(yb.astype(f32) * row_gate[:, None])
    return out[:n_tok].astype(x2d.dtype)


def trunk(x, conv_st, gdn_st, gla_st, rms_mix_w, w_in, conv_w, gdn_a_log, gdn_dt_bias, gdn_norm_w,
          gla_gk_w, gla_gk_b, gla_norm_w, w_out, rms_ffn_w, w_router, b_router, w_up, b_up,
          w_down, b_down, rms_final_w):
    convs, gdns, glas = [], [], []
    for l in range(DEPTH):
        mix, c_new, g_new, a_new = token_mixers(
            rmsnorm(x, rms_mix_w[l]), conv_st[l], gdn_st[l], gla_st[l], w_in[l], conv_w[l],
            gdn_a_log[l], gdn_dt_bias[l], gdn_norm_w[l], gla_gk_w[l], gla_gk_b[l], gla_norm_w[l], w_out[l])
        x = x + mix
        b, t_len, d = x.shape
        x = x + moe(rmsnorm(x, rms_ffn_w[l]).reshape(b * t_len, d), w_router[l], b_router[l],
                    w_up[l], b_up[l], w_down[l], b_down[l]).reshape(b, t_len, d)
        convs.append(c_new)
        gdns.append(g_new)
        glas.append(a_new)
    return rmsnorm(x, rms_final_w), jnp.stack(convs), jnp.stack(gdns), jnp.stack(glas)


def setup_inputs(seed: int = 0) -> dict:
    key = jax.random.key(seed)
    ks = jax.random.split(key, 24)
    f32 = jnp.float32
    L, D = DEPTH, D_MODEL

    def nrm(k, shape, scale):
        return jax.random.normal(k, shape, f32) * scale

    dt = jnp.exp(jax.random.uniform(ks[9], (L, GDN_HEADS), f32, np.log(1e-3), np.log(1e-1)))
    return {
        "x_prompt": nrm(ks[0], (BATCH, SEQ, D), 1.0),
        "x_sample": nrm(ks[1], (DEC_BATCH, DEC_SEQ, D), 1.0),
        "state_gdn_conv": nrm(ks[2], (L, DEC_BATCH, CONV_WIDTH - 1, GDN_CONV_CH), 1.0),
        "state_gdn": nrm(ks[3], (L, DEC_BATCH, GDN_HEADS, GDN_DK, GDN_DV), 0.1),
        "state_gla": nrm(ks[4], (L, DEC_BATCH, GLA_HEADS, GLA_DK, GLA_DV), 0.1),
        "rms_mix_w": 1.0 + nrm(ks[5], (L, D), 0.02),
        "w_in": nrm(ks[6], (L, D, IN_DIM), D ** -0.5),
        "conv_w": nrm(ks[7], (L, CONV_WIDTH, GDN_CONV_CH), CONV_WIDTH ** -0.5),
        "gdn_a_log": jnp.log(jax.random.uniform(ks[8], (L, GDN_HEADS), f32, 1.0, 16.0)),
        "gdn_dt_bias": dt + jnp.log(-jnp.expm1(-dt)),
        "gdn_norm_w": 1.0 + nrm(ks[10], (L, GDN_DV), 0.02),
        "gla_gk_w": nrm(ks[11], (L, GLA_GATE_RANK, GLA_QK_W), GLA_GATE_RANK ** -0.5),
        "gla_gk_b": nrm(ks[12], (L, GLA_QK_W), 0.01),
        "gla_norm_w": 1.0 + nrm(ks[13], (L, GLA_DV), 0.02),
        "w_out": nrm(ks[14], (L, MIX_WIDTH, D), MIX_WIDTH ** -0.5),
        "rms_ffn_w": 1.0 + nrm(ks[15], (L, D), 0.02),
        "w_router": nrm(ks[16], (L, D, N_EXPERTS), D ** -0.5),
        "b_router": nrm(ks[17], (L, N_EXPERTS), 0.01),
        "w_up": nrm(ks[18], (L, N_EXPERTS, D, 2 * D_FF), D ** -0.5),
        "b_up": nrm(ks[19], (L, N_EXPERTS, 2 * D_FF), 0.01),
        "w_down": nrm(ks[20], (L, N_EXPERTS, D_FF, D), D_FF ** -0.5),
        "b_down": nrm(ks[21], (L, N_EXPERTS, D), 0.01),
        "rms_final_w": 1.0 + nrm(ks[22], (D,), 0.02),
    }


def reference(x_prompt, x_sample, state_gdn_conv, state_gdn, state_gla, rms_mix_w, w_in, conv_w,
              gdn_a_log, gdn_dt_bias, gdn_norm_w, gla_gk_w, gla_gk_b, gla_norm_w, w_out, rms_ffn_w,
              w_router, b_router, w_up, b_up, w_down, b_down, rms_final_w):
    bp = x_prompt.shape[0]
    conv0 = jnp.zeros((DEPTH, bp) + state_gdn_conv.shape[2:], state_gdn_conv.dtype)
    gdn0 = jnp.zeros((DEPTH, bp) + state_gdn.shape[2:], state_gdn.dtype)
    gla0 = jnp.zeros((DEPTH, bp) + state_gla.shape[2:], state_gla.dtype)
    y_prompt, conv_prompt, gdn_prompt, gla_prompt = trunk(
        x_prompt, conv0, gdn0, gla0, rms_mix_w, w_in, conv_w, gdn_a_log, gdn_dt_bias, gdn_norm_w,
        gla_gk_w, gla_gk_b, gla_norm_w, w_out, rms_ffn_w, w_router, b_router, w_up, b_up, w_down,
        b_down, rms_final_w)
    y_sample, conv_sample, gdn_sample, gla_sample = trunk(
        x_sample, state_gdn_conv, state_gdn, state_gla, rms_mix_w, w_in, conv_w, gdn_a_log,
        gdn_dt_bias, gdn_norm_w, gla_gk_w, gla_gk_b, gla_norm_w, w_out, rms_ffn_w, w_router,
        b_router, w_up, b_up, w_down, b_down, rms_final_w)
    return (y_prompt, y_sample, conv_prompt, gdn_prompt, gla_prompt, conv_sample, gdn_sample, gla_sample)
```

```python
import functools

import jax
import jax.numpy as jnp
from jax import lax
from jax.experimental import pallas as pl
from jax.experimental.pallas import tpu as pltpu

F32 = jnp.float32
BF16 = jnp.bfloat16
HI = lax.Precision.HIGHEST

D_MODEL = 1024
GDN_HEADS = 4
GDN_DK = 128
GDN_DV = 128
GLA_HEADS = 4
GLA_DK = 64
GLA_DV = 128
GLA_GATE_RANK = 16
GLA_GATE_NORMALIZER = 16.0
CONV_WIDTH = 4
CHUNK = 64
N_EXPERTS = 32
TOP_K = 4
D_FF = 1024
SWIGLU_LIMIT = 7.0
SWIGLU_ALPHA = 1.702
RMS_EPS = 1e-6
L2_EPS = 1e-6

GDN_QK_W = GDN_HEADS * GDN_DK
GDN_V_W = GDN_HEADS * GDN_DV
GDN_CONV_CH = 2 * GDN_QK_W + GDN_V_W
GLA_QK_W = GLA_HEADS * GLA_DK
GLA_V_W = GLA_HEADS * GLA_DV

COL_QKV = 0
COL_Z = 1536
COL_GQ = 2048
COL_GK = 2304
COL_GV = 2560
COL_GG = 3072
COL_SM = 3584
SM_W = 128
PROJ_W = COL_SM + SM_W
SM_A, SM_B, SM_LR = 0, 4, 8

LANE = 128
ROW_TILE = 256
EXPERT_ROWS = 256
COMBINE_ROWS = 128
VMEM_LIMIT = 56 * 1024 * 1024


def _dot(a, b):
    return jnp.dot(a.astype(BF16), b.astype(BF16), preferred_element_type=F32)


def _dot_nt(a, b):
    return lax.dot_general(a.astype(BF16), b.astype(BF16), (((1,), (1,)), ((), ())),
                           preferred_element_type=F32)


def _dot_tn(a, b):
    return lax.dot_general(a.astype(BF16), b.astype(BF16), (((0,), (0,)), ((), ())),
                           preferred_element_type=F32)


def _dot_hi(a, b):
    return jnp.dot(a, b, precision=HI, preferred_element_type=F32)


def _rms(x, w):
    return x * lax.rsqrt(jnp.mean(x * x, axis=-1, keepdims=True) + RMS_EPS) * w


def _silu(x):
    return x * jax.nn.sigmoid(x)


def _inproj_body(x_ref, g_ref, w_ref, o_ref):
    h = _rms(x_ref[...], g_ref[...])
    o_ref[...] = jnp.dot(h.astype(BF16), w_ref[...], preferred_element_type=F32)


def _inproj(x, g, w):
    n = x.shape[0]
    return pl.pallas_call(
        _inproj_body,
        grid=(n // ROW_TILE,),
        in_specs=[
            pl.BlockSpec((ROW_TILE, D_MODEL), lambda i: (i, 0)),
            pl.BlockSpec((1, D_MODEL), lambda i: (0, 0)),
            pl.BlockSpec((D_MODEL, PROJ_W), lambda i: (0, 0)),
        ],
        out_specs=pl.BlockSpec((ROW_TILE, PROJ_W), lambda i: (i, 0)),
        out_shape=jax.ShapeDtypeStruct((n, PROJ_W), F32),
        compiler_params=pltpu.CompilerParams(dimension_semantics=("arbitrary",),
                                             vmem_limit_bytes=VMEM_LIMIT),
        name="in_proj",
    )(x, g, w)


def _log2(n):
    assert n & (n - 1) == 0
    return n.bit_length() - 1


def _tri_inv(m, c, ii, jj):
    eye = (ii == jj).astype(F32)
    base = min(c, 8)
    sh = _log2(base)
    n = jnp.where((ii >> sh) == (jj >> sh), m, 0.0)
    x = eye - n
    p = _dot_hi(n, n)
    x = x + _dot_hi(x, p)
    p = _dot_hi(p, p)
    x = x + _dot_hi(x, p)
    s = base
    while s < c:
        sh_s, sh_b = _log2(s), _log2(2 * s)
        off = ((ii >> sh_b) == (jj >> sh_b)) & ((ii >> sh_s) != (jj >> sh_s))
        cm = jnp.where(off, m, 0.0)
        x = x - _dot_hi(_dot_hi(x, cm), x)
        s *= 2
    return x


def _gated_norm(o, w, z):
    return o * lax.rsqrt(jnp.mean(o * o, axis=-1, keepdims=True) + RMS_EPS) * w * _silu(z)


def _gdn_body(qkv_ref, z_ref, sm_ref, cbuf_ref, s0_ref, cw_ref, alog_ref, dtb_ref, nw_ref,
              o_ref, sout_ref, st, xc, act, *, tb_rows, chunk, valid, n_tb):
    tb = pl.program_id(1)
    c = chunk
    tail = CONV_WIDTH - 1
    pad = 8

    @pl.when(tb == 0)
    def _():
        st[...] = s0_ref[...]
        xc[pad - tail:pad, :] = cbuf_ref[...]

    if n_tb > 1:
        @pl.when(tb > 0)
        def _():
            xc[pad - tail:pad, :] = xc[tb_rows + pad - tail:tb_rows + pad, :]

    xc[pad:pad + tb_rows, :] = qkv_ref[...]

    slab = min(tb_rows, 64)
    for s in range(tb_rows // slab):
        for cb in range(GDN_CONV_CH // 512):
            cs = slice(cb * 512, (cb + 1) * 512)
            acc = xc[pad - 3 + s * slab:pad - 3 + (s + 1) * slab, cs] * cw_ref[0:1, cs]
            for i in range(1, CONV_WIDTH):
                acc = acc + xc[pad - 3 + i + s * slab:pad - 3 + i + (s + 1) * slab, cs] * cw_ref[i:i + 1, cs]
            act[s * slab:(s + 1) * slab, cs] = _silu(acc)

    ii = lax.broadcasted_iota(jnp.int32, (c, c), 0)
    jj = lax.broadcasted_iota(jnp.int32, (c, c), 1)
    lower = (ii >= jj)
    lower_f = lower.astype(F32)
    strict = (ii > jj)
    rowmask = None
    if valid < c:
        rowmask = lax.broadcasted_iota(jnp.int32, (c, 1), 0) < valid

    def chunk_step(ci, carry):
        r = pl.multiple_of(ci * c, c)
        rows = pl.ds(r, c)
        sm = sm_ref[rows, :]
        g_t = -jnp.exp(alog_ref[...]) * jax.nn.softplus(sm + dtb_ref[...])
        b_t = jax.nn.sigmoid(sm)
        if rowmask is not None:
            g_t = jnp.where(rowmask, g_t, 0.0)
            b_t = jnp.where(rowmask, b_t, 0.0)
        gc_t = _dot_hi(lower_f, g_t)
        for h in range(GDN_HEADS):
            q = act[rows, h * GDN_DK:(h + 1) * GDN_DK]
            k = act[rows, GDN_QK_W + h * GDN_DK:GDN_QK_W + (h + 1) * GDN_DK]
            v = act[rows, 2 * GDN_QK_W + h * GDN_DV:2 * GDN_QK_W + (h + 1) * GDN_DV]
            if rowmask is not None:
                q = jnp.where(rowmask, q, 0.0)
                k = jnp.where(rowmask, k, 0.0)
                v = jnp.where(rowmask, v, 0.0)
            qn = q * lax.rsqrt(jnp.sum(q * q, axis=-1, keepdims=True) + L2_EPS) * (GDN_DK ** -0.5)
            kn = k * lax.rsqrt(jnp.sum(k * k, axis=-1, keepdims=True) + L2_EPS)
            beta = b_t[:, SM_B + h:SM_B + h + 1]
            gcol = gc_t[:, SM_A + h:SM_A + h + 1]
            kb = kn * beta
            vb = v * beta
            s1 = _dot_nt(jnp.concatenate([kb, qn], axis=0), kn)
            gb = jnp.broadcast_to(g_t[:, SM_A + h:SM_A + h + 1], (c, c))
            diff = _dot_hi(lower_f, jnp.where(strict, gb, 0.0))
            dec = jnp.exp(diff)
            mm = jnp.where(strict, s1[:c] * dec, 0.0)
            aq = jnp.where(lower, s1[c:] * dec, 0.0)
            tm = _tri_inv(mm, c, ii, jj)
            eg = jnp.exp(gcol)
            uw = _dot(tm, jnp.concatenate([vb, kb * eg], axis=1))
            u = uw[:, :GDN_DV]
            w = uw[:, GDN_DV:]
            s_h = st[h]
            ws = _dot(jnp.concatenate([w, qn * eg], axis=0), s_h)
            v_new = u - ws[:c]
            o = ws[c:] + _dot(aq, v_new)
            g_last = gcol[c - 1:c, :]
            kd = kn * jnp.exp(g_last - gcol)
            st[h] = s_h * jnp.exp(g_last) + _dot_tn(kd, v_new)
            zz = z_ref[rows, h * GDN_DV:(h + 1) * GDN_DV]
            o_ref[rows, h * GDN_DV:(h + 1) * GDN_DV] = _gated_norm(o, nw_ref[...], zz)
        return carry

    n_chunks = tb_rows // c
    if n_chunks == 1:
        chunk_step(0, 0)
    else:
        lax.fori_loop(0, n_chunks, chunk_step, 0)

    @pl.when(tb == n_tb - 1)
    def _():
        sout_ref[...] = st[...]


def _gdn(proj, row0, batch, t_len, tb_rows, chunk, valid, conv_buf, s0, conv_w, alog, dtb, nw):
    n_tb = t_len // tb_rows
    blk0 = row0 // tb_rows

    def rowblk(b, t):
        return blk0 + b * n_tb + t

    body = functools.partial(_gdn_body, tb_rows=tb_rows, chunk=chunk, valid=valid, n_tb=n_tb)
    return pl.pallas_call(
        body,
        grid=(batch, n_tb),
        in_specs=[
            pl.BlockSpec((tb_rows, GDN_CONV_CH), lambda b, t: (rowblk(b, t), COL_QKV // GDN_CONV_CH)),
            pl.BlockSpec((tb_rows, GDN_V_W), lambda b, t: (rowblk(b, t), COL_Z // GDN_V_W)),
            pl.BlockSpec((tb_rows, SM_W), lambda b, t: (rowblk(b, t), COL_SM // SM_W)),
            pl.BlockSpec((None, CONV_WIDTH - 1, GDN_CONV_CH), lambda b, t: (b, 0, 0)),
            pl.BlockSpec((None, GDN_HEADS, GDN_DK, GDN_DV), lambda b, t: (b, 0, 0, 0)),
            pl.BlockSpec((CONV_WIDTH, GDN_CONV_CH), lambda b, t: (0, 0)),
            pl.BlockSpec((1, SM_W), lambda b, t: (0, 0)),
            pl.BlockSpec((1, SM_W), lambda b, t: (0, 0)),
            pl.BlockSpec((1, GDN_DV), lambda b, t: (0, 0)),
        ],
        out_specs=[
            pl.BlockSpec((tb_rows, GDN_V_W), lambda b, t: (b * n_tb + t, 0)),
            pl.BlockSpec((None, GDN_HEADS, GDN_DK, GDN_DV), lambda b, t: (b, 0, 0, 0)),
        ],
        out_shape=[
            jax.ShapeDtypeStruct((batch * t_len, GDN_V_W), F32),
            jax.ShapeDtypeStruct((batch, GDN_HEADS, GDN_DK, GDN_DV), F32),
        ],
        scratch_shapes=[
            pltpu.VMEM((GDN_HEADS, GDN_DK, GDN_DV), F32),
            pltpu.VMEM((tb_rows + 8, GDN_CONV_CH), F32),
            pltpu.VMEM((tb_rows, GDN_CONV_CH), F32),
        ],
        compiler_params=pltpu.CompilerParams(dimension_semantics=("arbitrary", "arbitrary"),
                                             vmem_limit_bytes=VMEM_LIMIT),
        name="gdn_mixer",
    )(proj, proj, proj, conv_buf, s0, conv_w, alog, dtb, nw)


def _gla_body(q_ref, k_ref, v_ref, go_ref, sm_ref, s0_ref, wgk_ref, bgk_ref, nw_ref,
              o_ref, sout_ref, st, *, tb_rows, chunk, valid, n_tb):
    tb = pl.program_id(1)
    c = chunk

    @pl.when(tb == 0)
    def _():
        st[...] = s0_ref[...]

    ii = lax.broadcasted_iota(jnp.int32, (c, c), 0)
    jj = lax.broadcasted_iota(jnp.int32, (c, c), 1)
    lower = (ii >= jj)
    lower_f = lower.astype(F32)
    rid = lax.broadcasted_iota(jnp.int32, (c, 1), 0)
    rowmask = (rid < valid) if valid < c else None
    n_sub = max(c // 16, 1)
    sub = c // n_sub
    ones_cv = jnp.ones((c, GLA_DV), F32)

    def chunk_step(ci, carry):
        r = pl.multiple_of(ci * c, c)
        rows = pl.ds(r, c)
        sm = sm_ref[rows, :]
        gk = jax.nn.log_sigmoid(_dot(sm, wgk_ref[...]) + bgk_ref[...]) / GLA_GATE_NORMALIZER
        if rowmask is not None:
            gk = jnp.where(rowmask, gk, 0.0)
        bc = _dot_hi(lower_f, gk)
        for h in range(GLA_HEADS):
            ks = slice(h * GLA_DK, (h + 1) * GLA_DK)
            vs = slice(h * GLA_DV, (h + 1) * GLA_DV)
            q = q_ref[rows, ks] * (GLA_DK ** -0.5)
            k = k_ref[rows, ks]
            v = v_ref[rows, vs]
            if rowmask is not None:
                k = jnp.where(rowmask, k, 0.0)
                v = jnp.where(rowmask, v, 0.0)
            bch = bc[:, ks]
            s_h = st[h]
            o = _dot(q * jnp.exp(bch), s_h)
            q_parts, k_parts = [], []
            for sb in range(n_sub):
                ref_row = bch[sb * sub:sb * sub + 1, :]
                in_blk = (rid >= sb * sub) & (rid < (sb + 1) * sub)
                q_parts.append(jnp.where(in_blk, q * jnp.exp(jnp.where(in_blk, bch - ref_row, 0.0)), 0.0))
                k_parts.append(k * jnp.exp(jnp.where(rid < (sb + 1) * sub, ref_row - bch, 0.0)))
            if n_sub > 1:
                q_hat = jnp.concatenate(q_parts, axis=1)
                k_hat = jnp.concatenate(k_parts, axis=1)
            else:
                q_hat, k_hat = q_parts[0], k_parts[0]
            a = jnp.where(lower, _dot_nt(q_hat, k_hat), 0.0)
            o = o + _dot(a, v)
            b_last = bch[c - 1:c, :]
            kd = k * jnp.exp(b_last - bch)
            dec_s = jnp.exp(lax.dot_general(gk[:, ks], ones_cv, (((0,), (0,)), ((), ())),
                                            precision=HI, preferred_element_type=F32))
            st[h] = dec_s * s_h + _dot_tn(kd, v)
            o_ref[rows, vs] = _gated_norm(o, nw_ref[...], go_ref[rows, vs])
        return carry

    n_chunks = tb_rows // c
    if n_chunks == 1:
        chunk_step(0, 0)
    else:
        lax.fori_loop(0, n_chunks, chunk_step, 0)

    @pl.when(tb == n_tb - 1)
    def _():
        sout_ref[...] = st[...]


def _gla(proj, row0, batch, t_len, tb_rows, chunk, valid, s0, wgk, bgk, nw):
    n_tb = t_len // tb_rows
    blk0 = row0 // tb_rows

    def rowblk(b, t):
        return blk0 + b * n_tb + t

    body = functools.partial(_gla_body, tb_rows=tb_rows, chunk=chunk, valid=valid, n_tb=n_tb)
    return pl.pallas_call(
        body,
        grid=(batch, n_tb),
        in_specs=[
            pl.BlockSpec((tb_rows, GLA_QK_W), lambda b, t: (rowblk(b, t), COL_GQ // GLA_QK_W)),
            pl.BlockSpec((tb_rows, GLA_QK_W), lambda b, t: (rowblk(b, t), COL_GK // GLA_QK_W)),
            pl.BlockSpec((tb_rows, GLA_V_W), lambda b, t: (rowblk(b, t), COL_GV // GLA_V_W)),
            pl.BlockSpec((tb_rows, GLA_V_W), lambda b, t: (rowblk(b, t), COL_GG // GLA_V_W)),
            pl.BlockSpec((tb_rows, SM_W), lambda b, t: (rowblk(b, t), COL_SM // SM_W)),
            pl.BlockSpec((None, GLA_HEADS, GLA_DK, GLA_DV), lambda b, t: (b, 0, 0, 0)),
            pl.BlockSpec((SM_W, GLA_QK_W), lambda b, t: (0, 0)),
            pl.BlockSpec((1, GLA_QK_W), lambda b, t: (0, 0)),
            pl.BlockSpec((1, GLA_DV), lambda b, t: (0, 0)),
        ],
        out_specs=[
            pl.BlockSpec((tb_rows, GLA_V_W), lambda b, t: (b * n_tb + t, 0)),
            pl.BlockSpec((None, GLA_HEADS, GLA_DK, GLA_DV), lambda b, t: (b, 0, 0, 0)),
        ],
        out_shape=[
            jax.ShapeDtypeStruct((batch * t_len, GLA_V_W), F32),
            jax.ShapeDtypeStruct((batch, GLA_HEADS, GLA_DK, GLA_DV), F32),
        ],
        scratch_shapes=[pltpu.VMEM((GLA_HEADS, GLA_DK, GLA_DV), F32)],
        compiler_params=pltpu.CompilerParams(dimension_semantics=("arbitrary", "arbitrary"),
                                             vmem_limit_bytes=VMEM_LIMIT),
        name="gla_mixer",
    )(proj, proj, proj, proj, proj, s0, wgk, bgk, nw)


def _outproj_body(og_ref, ol_ref, x_ref, wo_ref, g_ref, wr_ref, br_ref, x1_ref, h2_ref, lg_ref):
    o = jnp.concatenate([og_ref[...], ol_ref[...]], axis=1)
    x1 = x_ref[...] + jnp.dot(o.astype(BF16), wo_ref[...], preferred_element_type=F32)
    x1_ref[...] = x1
    h = _rms(x1, g_ref[...])
    h2_ref[...] = h
    lg_ref[...] = _dot_hi(h, wr_ref[...]) + br_ref[...]


def _outproj(og, ol, x, wo, g, wr, br):
    n = x.shape[0]
    return pl.pallas_call(
        _outproj_body,
        grid=(n // ROW_TILE,),
        in_specs=[
            pl.BlockSpec((ROW_TILE, GDN_V_W), lambda i: (i, 0)),
            pl.BlockSpec((ROW_TILE, GLA_V_W), lambda i: (i, 0)),
            pl.BlockSpec((ROW_TILE, D_MODEL), lambda i: (i, 0)),
            pl.BlockSpec((D_MODEL, D_MODEL), lambda i: (0, 0)),
            pl.BlockSpec((1, D_MODEL), lambda i: (0, 0)),
            pl.BlockSpec((D_MODEL, LANE), lambda i: (0, 0)),
            pl.BlockSpec((1, LANE), lambda i: (0, 0)),
        ],
        out_specs=[
            pl.BlockSpec((ROW_TILE, D_MODEL), lambda i: (i, 0)),
            pl.BlockSpec((ROW_TILE, D_MODEL), lambda i: (i, 0)),
            pl.BlockSpec((ROW_TILE, LANE), lambda i: (i, 0)),
        ],
        out_shape=[
            jax.ShapeDtypeStruct((n, D_MODEL), F32),
            jax.ShapeDtypeStruct((n, D_MODEL), F32),
            jax.ShapeDtypeStruct((n, LANE), F32),
        ],
        compiler_params=pltpu.CompilerParams(dimension_semantics=("arbitrary",),
                                             vmem_limit_bytes=VMEM_LIMIT),
        name="out_proj",
    )(og, ol, x, wo, g, wr, br)


def _row_copy(src_hbm, row, dst, slot, sem):
    return pltpu.make_async_copy(src_hbm.at[pl.ds(row, 1), :], dst.at[pl.ds(slot, 1), :], sem)


def _expert_body(be_ref, nu_ref, idx_ref, h_hbm, wup_ref, bup_ref, wdn_ref, bdn_ref, y_ref, xbuf, sem):
    i = pl.program_id(0)

    @pl.when(i < nu_ref[0])
    def _():
        def issue(r, carry):
            _row_copy(h_hbm, idx_ref[0, r], xbuf, r, sem).start()
            return carry

        lax.fori_loop(0, EXPERT_ROWS, issue, 0)
        pltpu.make_async_copy(h_hbm.at[pl.ds(0, EXPERT_ROWS), :], xbuf, sem).wait()
        gu = _dot(xbuf[...], wup_ref[...]) + bup_ref[...]
        gate = jnp.minimum(gu[:, :D_FF], SWIGLU_LIMIT)
        up = jnp.clip(gu[:, D_FF:], -SWIGLU_LIMIT, SWIGLU_LIMIT)
        a = (up + 1.0) * gate * jax.nn.sigmoid(SWIGLU_ALPHA * gate)
        y_ref[...] = _dot(a, wdn_ref[...]) + bdn_ref[...]

    @pl.when(i >= nu_ref[0])
    def _():
        y_ref[...] = jnp.zeros_like(y_ref)


def _experts(block_e, n_used, src_tok, h2, w_up, b_up, w_down, b_down):
    n_blocks = block_e.shape[0]
    grid_spec = pltpu.PrefetchScalarGridSpec(
        num_scalar_prefetch=2,
        grid=(n_blocks,),
        in_specs=[
            pl.BlockSpec((None, 1, EXPERT_ROWS), lambda i, be, nu: (i, 0, 0), memory_space=pltpu.SMEM),
            pl.BlockSpec(memory_space=pl.ANY),
            pl.BlockSpec((None, D_MODEL, 2 * D_FF), lambda i, be, nu: (be[i], 0, 0)),
            pl.BlockSpec((None, 1, 2 * D_FF), lambda i, be, nu: (be[i], 0, 0)),
            pl.BlockSpec((None, D_FF, D_MODEL), lambda i, be, nu: (be[i], 0, 0)),
            pl.BlockSpec((None, 1, D_MODEL), lambda i, be, nu: (be[i], 0, 0)),
        ],
        out_specs=pl.BlockSpec((EXPERT_ROWS, D_MODEL), lambda i, be, nu: (i, 0)),
        scratch_shapes=[pltpu.VMEM((EXPERT_ROWS, D_MODEL), F32), pltpu.SemaphoreType.DMA(())],
    )
    return pl.pallas_call(
        _expert_body,
        grid_spec=grid_spec,
        out_shape=jax.ShapeDtypeStruct((n_blocks * EXPERT_ROWS, D_MODEL), F32),
        compiler_params=pltpu.CompilerParams(dimension_semantics=("arbitrary",),
                                             vmem_limit_bytes=VMEM_LIMIT),
        name="experts",
    )(block_e, n_used, src_tok.reshape(n_blocks, 1, EXPERT_ROWS), h2, w_up,
      b_up.reshape(N_EXPERTS, 1, 2 * D_FF), w_down, b_down.reshape(N_EXPERTS, 1, D_MODEL))


def _combine_body(dest_ref, gates_ref, y_hbm, x1_ref, g_ref, o_ref, ybuf, sem):
    def issue(r, carry):
        for k in range(TOP_K):
            _row_copy(y_hbm, dest_ref[0, r * TOP_K + k], ybuf.at[k], r, sem).start()
        return carry

    lax.fori_loop(0, COMBINE_ROWS, issue, 0)
    for k in range(TOP_K):
        pltpu.make_async_copy(y_hbm.at[pl.ds(0, COMBINE_ROWS), :], ybuf.at[k], sem).wait()
    acc = x1_ref[...]
    moe = ybuf[0] * gates_ref[:, 0:1]
    for k in range(1, TOP_K):
        moe = moe + ybuf[k] * gates_ref[:, k:k + 1]
    o_ref[...] = _rms(acc + moe, g_ref[...])


def _combine(dest, gates, y, x1, g):
    n = x1.shape[0]
    n_blk = n // COMBINE_ROWS
    return pl.pallas_call(
        _combine_body,
        grid=(n_blk,),
        in_specs=[
            pl.BlockSpec((None, 1, COMBINE_ROWS * TOP_K), lambda i: (i, 0, 0), memory_space=pltpu.SMEM),
            pl.BlockSpec((COMBINE_ROWS, TOP_K), lambda i: (i, 0)),
            pl.BlockSpec(memory_space=pl.ANY),
            pl.BlockSpec((COMBINE_ROWS, D_MODEL), lambda i: (i, 0)),
            pl.BlockSpec((1, D_MODEL), lambda i: (0, 0)),
        ],
        out_specs=pl.BlockSpec((COMBINE_ROWS, D_MODEL), lambda i: (i, 0)),
        out_shape=jax.ShapeDtypeStruct((n, D_MODEL), F32),
        scratch_shapes=[pltpu.VMEM((TOP_K, COMBINE_ROWS, D_MODEL), F32), pltpu.SemaphoreType.DMA(())],
        compiler_params=pltpu.CompilerParams(dimension_semantics=("arbitrary",),
                                             vmem_limit_bytes=VMEM_LIMIT),
        name="combine",
    )(dest.reshape(n_blk, 1, COMBINE_ROWS * TOP_K), gates, y, x1, g)


def _route(logits):
    n = logits.shape[0]
    top_v, top_i = lax.top_k(logits[:, :N_EXPERTS], TOP_K)
    gates = jax.nn.softmax(top_v, axis=-1)
    flat_e = top_i.reshape(-1).astype(jnp.int32)
    n_assign = n * TOP_K
    onehot = (flat_e[:, None] == jnp.arange(N_EXPERTS, dtype=jnp.int32)[None, :]).astype(jnp.int32)
    csum = jnp.cumsum(onehot, axis=0)
    rank = jnp.take_along_axis(csum, flat_e[:, None], axis=1)[:, 0] - 1
    counts = csum[-1]
    padded = (counts + EXPERT_ROWS - 1) // EXPERT_ROWS * EXPERT_ROWS
    pend = jnp.cumsum(padded)
    pstart = pend - padded
    dest = (pstart[flat_e] + rank).astype(jnp.int32)
    n_rows = n_assign + N_EXPERTS * EXPERT_ROWS
    n_blocks = n_rows // EXPERT_ROWS
    src_tok = jnp.zeros((n_rows,), jnp.int32).at[dest].set(jnp.arange(n_assign, dtype=jnp.int32) // TOP_K)
    block_e = jnp.minimum(jnp.searchsorted(pend, jnp.arange(n_blocks, dtype=jnp.int32) * EXPERT_ROWS, side="right"),
                          N_EXPERTS - 1).astype(jnp.int32)
    n_used = (pend[-1] // EXPERT_ROWS).astype(jnp.int32).reshape(1)
    return gates, dest, src_tok, block_e, n_used


def _pad_lanes(v, width):
    return jnp.zeros((1, width), F32).at[0, :v.shape[0]].set(v.astype(F32))


def kernel(x_prompt, x_sample, state_gdn_conv, state_gdn, state_gla, rms_mix_w, w_in, conv_w, gdn_a_log,
           gdn_dt_bias, gdn_norm_w, gla_gk_w, gla_gk_b, gla_norm_w, w_out, rms_ffn_w, w_router, b_router,
           w_up, b_up, w_down, b_down, rms_final_w):
    bp, tp, d = x_prompt.shape
    bs, ts, _ = x_sample.shape
    n_p, n_s = bp * tp, bs * ts
    assert d == D_MODEL and state_gdn.shape[0] == 1, "single-layer kernel"
    l = 0

    wi = w_in[l]
    a0 = GDN_CONV_CH + GDN_V_W
    g0 = a0 + 2 * GDN_HEADS
    lr0 = g0 + 2 * GLA_QK_W + 2 * GLA_V_W
    small = jnp.concatenate([wi[:, a0:a0 + 2 * GDN_HEADS], wi[:, lr0:lr0 + GLA_GATE_RANK],
                             jnp.zeros((d, SM_W - 2 * GDN_HEADS - GLA_GATE_RANK), F32)], axis=1)
    w_big = jnp.concatenate([wi[:, :a0], wi[:, g0:lr0], small], axis=1).astype(BF16)
    alog = _pad_lanes(gdn_a_log[l], SM_W)
    dtb = _pad_lanes(gdn_dt_bias[l], SM_W)
    wgk = jnp.zeros((SM_W, GLA_QK_W), F32).at[SM_LR:SM_LR + GLA_GATE_RANK].set(gla_gk_w[l])
    wr = jnp.zeros((d, LANE), F32).at[:, :N_EXPERTS].set(w_router[l])
    br = jnp.full((1, LANE), -1e30, F32).at[0, :N_EXPERTS].set(b_router[l])

    x_all = jnp.concatenate([x_prompt.reshape(n_p, d), x_sample.reshape(n_s, d)], axis=0)
    proj = _inproj(x_all, rms_mix_w[l][None, :], w_big)

    tb_p = 512
    zeros_conv = jnp.zeros((bp, CONV_WIDTH - 1, GDN_CONV_CH), F32)
    og_p, gdn_p = _gdn(proj, 0, bp, tp, tb_p, CHUNK, CHUNK, zeros_conv,
                       jnp.zeros((bp, GDN_HEADS, GDN_DK, GDN_DV), F32), conv_w[l], alog, dtb, gdn_norm_w[l][None, :])
    ol_p, gla_p = _gla(proj, 0, bp, tp, tb_p, CHUNK, CHUNK, jnp.zeros((bp, GLA_HEADS, GLA_DK, GLA_DV), F32),
                       wgk, gla_gk_b[l][None, :], gla_norm_w[l][None, :])

    ts_pad = 8
    proj_s = proj[n_p:].reshape(bs, ts, PROJ_W)
    proj_sp = jnp.pad(proj_s, ((0, 0), (0, ts_pad - ts), (0, 0))).reshape(bs * ts_pad, PROJ_W)
    og_s, gdn_s = _gdn(proj_sp, 0, bs, ts_pad, ts_pad, ts_pad, ts, state_gdn_conv[l], state_gdn[l], conv_w[l],
                       alog, dtb, gdn_norm_w[l][None, :])
    ol_s, gla_s = _gla(proj_sp, 0, bs, ts_pad, ts_pad, ts_pad, ts, state_gla[l], wgk, gla_gk_b[l][None, :],
                       gla_norm_w[l][None, :])
    og_s = og_s.reshape(bs, ts_pad, GDN_V_W)[:, :ts].reshape(n_s, GDN_V_W)
    ol_s = ol_s.reshape(bs, ts_pad, GLA_V_W)[:, :ts].reshape(n_s, GLA_V_W)

    og = jnp.concatenate([og_p, og_s], axis=0)
    ol = jnp.concatenate([ol_p, ol_s], axis=0)
    x1, h2, logits = _outproj(og, ol, x_all, w_out[l].astype(BF16), rms_ffn_w[l][None, :], wr, br)

    gates, dest, src_tok, block_e, n_used = _route(logits)
    y_rows = _experts(block_e, n_used, src_tok, h2, w_up[l], b_up[l], w_down[l], b_down[l])
    y_all = _combine(dest, gates, y_rows, x1, rms_final_w[None, :])

    y_prompt = y_all[:n_p].reshape(bp, tp, d)
    y_sample = y_all[n_p:].reshape(bs, ts, d)
    assert tp >= CONV_WIDTH - 1 and ts >= CONV_WIDTH - 1
    conv_p = proj[:n_p].reshape(bp, tp, PROJ_W)[:, tp - (CONV_WIDTH - 1):, :GDN_CONV_CH]
    conv_s = proj_s[:, ts - (CONV_WIDTH - 1):, :GDN_CONV_CH]
    return (y_prompt, y_sample, conv_p[None], gdn_p[None], gla_p[None], conv_s[None], gdn_s[None], gla_s[None])
```

```python
import functools

import jax
import jax.numpy as jnp
from jax import lax
from jax.experimental import pallas as pl
from jax.experimental.pallas import tpu as pltpu

F32 = jnp.float32
BF16 = jnp.bfloat16
HI = lax.Precision.HIGHEST

D_MODEL = 1024
GDN_HEADS = 4
GDN_DK = 128
GDN_DV = 128
GLA_HEADS = 4
GLA_DK = 64
GLA_DV = 128
GLA_GATE_RANK = 16
GLA_GATE_NORMALIZER = 16.0
CONV_WIDTH = 4
CHUNK = 64
N_EXPERTS = 32
TOP_K = 4
D_FF = 1024
SWIGLU_LIMIT = 7.0
SWIGLU_ALPHA = 1.702
RMS_EPS = 1e-6
L2_EPS = 1e-6

GDN_QK_W = GDN_HEADS * GDN_DK
GDN_V_W = GDN_HEADS * GDN_DV
GDN_CONV_CH = 2 * GDN_QK_W + GDN_V_W
GLA_QK_W = GLA_HEADS * GLA_DK
GLA_V_W = GLA_HEADS * GLA_DV

COL_QKV = 0
COL_Z = 1536
COL_GQ = 2048
COL_GK = 2304
COL_GV = 2560
COL_GG = 3072
COL_SM = 3584
SM_W = 128
PROJ_W = COL_SM + SM_W
SM_A, SM_B, SM_LR = 0, 4, 8

LANE = 128
SUBLANE = 8
ROW_TILE = 256
EXPERT_ROWS = 256
COMBINE_ROWS = 128
PROMPT_TIME_BLOCK = 512
SAMPLE_SEQS_PER_STEP = 8
VMEM_LIMIT = 56 * 1024 * 1024


def _dot(a, b):
    return jnp.dot(a.astype(BF16), b.astype(BF16), preferred_element_type=F32)


def _dot_nt(a, b):
    return lax.dot_general(a.astype(BF16), b.astype(BF16), (((1,), (1,)), ((), ())),
                           preferred_element_type=F32)


def _dot_tn(a, b):
    return lax.dot_general(a.astype(BF16), b.astype(BF16), (((0,), (0,)), ((), ())),
                           preferred_element_type=F32)


def _dot_hi(a, b):
    return jnp.dot(a, b, precision=HI, preferred_element_type=F32)


def _rms(x, w):
    return x * lax.rsqrt(jnp.mean(x * x, axis=-1, keepdims=True) + RMS_EPS) * w


def _silu(x):
    return x * jax.nn.sigmoid(x)


def _inproj_body(x_ref, g_ref, w_ref, o_ref):
    h = _rms(x_ref[...], g_ref[...])
    o_ref[...] = jnp.dot(h.astype(BF16), w_ref[...], preferred_element_type=F32)


def _inproj(x, g, w):
    n = x.shape[0]
    return pl.pallas_call(
        _inproj_body,
        grid=(n // ROW_TILE,),
        in_specs=[
            pl.BlockSpec((ROW_TILE, D_MODEL), lambda i: (i, 0)),
            pl.BlockSpec((1, D_MODEL), lambda i: (0, 0)),
            pl.BlockSpec((D_MODEL, PROJ_W), lambda i: (0, 0)),
        ],
        out_specs=pl.BlockSpec((ROW_TILE, PROJ_W), lambda i: (i, 0)),
        out_shape=jax.ShapeDtypeStruct((n, PROJ_W), F32),
        compiler_params=pltpu.CompilerParams(dimension_semantics=("arbitrary",),
                                             vmem_limit_bytes=VMEM_LIMIT),
        name="in_proj",
    )(x, g, w)


def _log2(n):
    assert n & (n - 1) == 0
    return n.bit_length() - 1


def _tri_inv_all(ms, c, ii, jj):
    eye = (ii == jj).astype(F32)
    base = min(c, 8)
    sh = _log2(base)
    blk = (ii >> sh) == (jj >> sh)
    ns = [jnp.where(blk, m, 0.0) for m in ms]
    xs = [eye - n for n in ns]
    ps = [_dot(n, n) for n in ns]
    ts = [_dot(jnp.concatenate([x, p], axis=0), p) for x, p in zip(xs, ps)]
    xs = [x + t[:c] for x, t in zip(xs, ts)]
    ps = [t[c:] for t in ts]
    xs = [x + _dot(x, p) for x, p in zip(xs, ps)]
    s = base
    while s < c:
        sh_s, sh_b = _log2(s), _log2(2 * s)
        off = ((ii >> sh_b) == (jj >> sh_b)) & ((ii >> sh_s) != (jj >> sh_s))
        ys = [_dot(x, jnp.where(off, m, 0.0)) for x, m in zip(xs, ms)]
        xs = [x - _dot(y, x) for x, y in zip(xs, ys)]
        s *= 2
    return xs


def _gated_norm(o, w, z):
    return o * lax.rsqrt(jnp.mean(o * o, axis=-1, keepdims=True) + RMS_EPS) * w * _silu(z)


def _chunk_rows(s, tb_rows, ci, c):
    r = s * tb_rows + ci * c
    if not isinstance(r, int):
        r = pl.multiple_of(r, c)
    return r


def _for_chunks(n_chunks, step):
    if n_chunks == 1:
        step(0, 0)
    else:
        lax.fori_loop(0, n_chunks, step, 0)


def _gdn_body(qkv_ref, z_ref, sm_ref, cbuf_ref, s0_ref, cw_ref, alog_ref, dtb_ref, nw_ref,
              o_ref, sout_ref, st, xc, act, gcs, us, wss, qgs, kds, aqs,
              *, nb, tb_rows, chunk, valid, n_tb):
    tb = pl.program_id(1)
    c = chunk
    n_heads = GDN_HEADS
    tail = CONV_WIDTH - 1
    pad = SUBLANE
    units = [(s, h) for s in range(nb) for h in range(n_heads)]

    @pl.when(tb == 0)
    def _():
        st[...] = s0_ref[...]
        for s in range(nb):
            xc[s, pad - tail:pad, :] = cbuf_ref[s]

    if n_tb > 1:
        @pl.when(tb > 0)
        def _():
            for s in range(nb):
                xc[s, pad - tail:pad, :] = xc[s, tb_rows + pad - tail:tb_rows + pad, :]

    for s in range(nb):
        xc[s, pad:pad + tb_rows, :] = qkv_ref[s * tb_rows:(s + 1) * tb_rows, :]

    slab = min(tb_rows, 64)
    for s in range(nb):
        for sl in range(tb_rows // slab):
            for cb in range(GDN_CONV_CH // 512):
                cs = slice(cb * 512, (cb + 1) * 512)
                lo = pad - tail + sl * slab
                acc = xc[s, lo:lo + slab, cs] * cw_ref[0:1, cs]
                for i in range(1, CONV_WIDTH):
                    acc = acc + xc[s, lo + i:lo + i + slab, cs] * cw_ref[i:i + 1, cs]
                act[s * tb_rows + sl * slab:s * tb_rows + (sl + 1) * slab, cs] = _silu(acc)

    ii = lax.broadcasted_iota(jnp.int32, (c, c), 0)
    jj = lax.broadcasted_iota(jnp.int32, (c, c), 1)
    lower = (ii >= jj)
    lower_f = lower.astype(F32)
    strict = (ii > jj)
    rowmask = None
    if valid < c:
        rowmask = lax.broadcasted_iota(jnp.int32, (c, 1), 0) < valid

    def hs(h, w):
        return slice(h * w, (h + 1) * w)

    def phase1(ci, carry):
        rows, b_ts, gc_ts, gc_tts = [], [], [], []
        for s in range(nb):
            rr = pl.ds(_chunk_rows(s, tb_rows, ci, c), c)
            sm = sm_ref[rr, :]
            g_t = -jnp.exp(alog_ref[...]) * jax.nn.softplus(sm + dtb_ref[...])
            b_t = jax.nn.sigmoid(sm)
            if rowmask is not None:
                g_t = jnp.where(rowmask, g_t, 0.0)
                b_t = jnp.where(rowmask, b_t, 0.0)
            gc_t = _dot_hi(lower_f, g_t)
            gcs[rr, :] = gc_t
            rows.append(rr)
            b_ts.append(b_t)
            gc_ts.append(gc_t)
            gc_tts.append(gc_t.T)
        qn, kn, kb, vb = {}, {}, {}, {}
        for (s, h) in units:
            q = act[rows[s], hs(h, GDN_DK)]
            k = act[rows[s], slice(GDN_QK_W + h * GDN_DK, GDN_QK_W + (h + 1) * GDN_DK)]
            v = act[rows[s], slice(2 * GDN_QK_W + h * GDN_DV, 2 * GDN_QK_W + (h + 1) * GDN_DV)]
            if rowmask is not None:
                q = jnp.where(rowmask, q, 0.0)
                k = jnp.where(rowmask, k, 0.0)
                v = jnp.where(rowmask, v, 0.0)
            qn[s, h] = q * lax.rsqrt(jnp.sum(q * q, axis=-1, keepdims=True) + L2_EPS) * (GDN_DK ** -0.5)
            kn[s, h] = k * lax.rsqrt(jnp.sum(k * k, axis=-1, keepdims=True) + L2_EPS)
            beta = b_ts[s][:, SM_B + h:SM_B + h + 1]
            kb[s, h] = kn[s, h] * beta
            vb[s, h] = v * beta
        s1 = {u: _dot_nt(jnp.concatenate([kb[u], qn[u]], axis=0), kn[u]) for u in units}
        mm = []
        for (s, h) in units:
            gcol = gc_ts[s][:, SM_A + h:SM_A + h + 1]
            grow = gc_tts[s][SM_A + h:SM_A + h + 1, :]
            dec = jnp.exp(jnp.where(lower, gcol - grow, -jnp.inf))
            mm.append(jnp.where(strict, s1[s, h][:c] * dec, 0.0))
            aqs[h, rows[s], :] = s1[s, h][c:] * dec
        tms = _tri_inv_all(mm, c, ii, jj)
        for (s, h), tm in zip(units, tms):
            gcol = gc_ts[s][:, SM_A + h:SM_A + h + 1]
            eg = jnp.exp(gcol)
            uw = _dot(tm, jnp.concatenate([vb[s, h], kb[s, h] * eg], axis=1))
            us[rows[s], hs(h, GDN_DV)] = uw[:, :GDN_DV]
            wss[rows[s], hs(h, GDN_DV)] = uw[:, GDN_DV:]
            qgs[rows[s], hs(h, GDN_DK)] = qn[s, h] * eg
            kds[rows[s], hs(h, GDN_DK)] = kn[s, h] * jnp.exp(gcol[c - 1:c, :] - gcol)
        return carry

    def phase2(ci, carry):
        r0 = [_chunk_rows(s, tb_rows, ci, c) for s in range(nb)]
        rows = [pl.ds(r, c) for r in r0]
        ws = {(s, h): _dot(jnp.concatenate([wss[rows[s], hs(h, GDN_DV)], qgs[rows[s], hs(h, GDN_DK)]], axis=0),
                           st[s, h]) for (s, h) in units}
        v_new = {(s, h): us[rows[s], hs(h, GDN_DV)] - ws[s, h][:c] for (s, h) in units}
        o = {(s, h): ws[s, h][c:] + _dot(aqs[h, rows[s], :], v_new[s, h]) for (s, h) in units}
        upd = {(s, h): _dot_tn(kds[rows[s], hs(h, GDN_DK)], v_new[s, h]) for (s, h) in units}
        for (s, h) in units:
            g_last = gcs[pl.ds(r0[s] + c - 1, 1), SM_A + h:SM_A + h + 1]
            st[s, h] = st[s, h] * jnp.exp(g_last) + upd[s, h]
        for s in range(nb):
            o_ref[rows[s], :] = jnp.concatenate(
                [_gated_norm(o[s, h], nw_ref[...], z_ref[rows[s], hs(h, GDN_DV)]) for h in range(n_heads)], axis=1)
        return carry

    n_chunks = tb_rows // c
    _for_chunks(n_chunks, phase1)
    _for_chunks(n_chunks, phase2)

    @pl.when(tb == n_tb - 1)
    def _():
        sout_ref[...] = st[...]


def _gdn(proj, n_seq, nb, t_len, tb_rows, chunk, valid, conv_buf, s0, conv_w, alog, dtb, nw):
    n_tb = t_len // tb_rows
    assert nb == 1 or n_tb == 1
    rows = nb * tb_rows

    def rowblk(b, t):
        return b * n_tb + t

    body = functools.partial(_gdn_body, nb=nb, tb_rows=tb_rows, chunk=chunk, valid=valid, n_tb=n_tb)
    return pl.pallas_call(
        body,
        grid=(n_seq // nb, n_tb),
        in_specs=[
            pl.BlockSpec((rows, GDN_CONV_CH), lambda b, t: (rowblk(b, t), COL_QKV // GDN_CONV_CH)),
            pl.BlockSpec((rows, GDN_V_W), lambda b, t: (rowblk(b, t), COL_Z // GDN_V_W)),
            pl.BlockSpec((rows, SM_W), lambda b, t: (rowblk(b, t), COL_SM // SM_W)),
            pl.BlockSpec((nb, CONV_WIDTH - 1, GDN_CONV_CH), lambda b, t: (b, 0, 0)),
            pl.BlockSpec((nb, GDN_HEADS, GDN_DK, GDN_DV), lambda b, t: (b, 0, 0, 0)),
            pl.BlockSpec((CONV_WIDTH, GDN_CONV_CH), lambda b, t: (0, 0)),
            pl.BlockSpec((1, SM_W), lambda b, t: (0, 0)),
            pl.BlockSpec((1, SM_W), lambda b, t: (0, 0)),
            pl.BlockSpec((1, GDN_DV), lambda b, t: (0, 0)),
        ],
        out_specs=[
            pl.BlockSpec((rows, GDN_V_W), lambda b, t: (rowblk(b, t), 0)),
            pl.BlockSpec((nb, GDN_HEADS, GDN_DK, GDN_DV), lambda b, t: (b, 0, 0, 0)),
        ],
        out_shape=[
            jax.ShapeDtypeStruct((n_seq * t_len, GDN_V_W), F32),
            jax.ShapeDtypeStruct((n_seq, GDN_HEADS, GDN_DK, GDN_DV), F32),
        ],
        scratch_shapes=[
            pltpu.VMEM((nb, GDN_HEADS, GDN_DK, GDN_DV), F32),
            pltpu.VMEM((nb, tb_rows + SUBLANE, GDN_CONV_CH), F32),
            pltpu.VMEM((rows, GDN_CONV_CH), F32),
            pltpu.VMEM((rows, SM_W), F32),
            pltpu.VMEM((rows, GDN_V_W), F32),
            pltpu.VMEM((rows, GDN_V_W), F32),
            pltpu.VMEM((rows, GDN_QK_W), F32),
            pltpu.VMEM((rows, GDN_QK_W), F32),
            pltpu.VMEM((GDN_HEADS, rows, chunk), F32),
        ],
        compiler_params=pltpu.CompilerParams(dimension_semantics=("arbitrary", "arbitrary"),
                                             vmem_limit_bytes=VMEM_LIMIT),
        name="gdn_mixer",
    )(proj, proj, proj, conv_buf, s0, conv_w, alog, dtb, nw)


def _gla_body(q_ref, k_ref, v_ref, go_ref, sm_ref, s0_ref, wgk_ref, bgk_ref, nw_ref,
              o_ref, sout_ref, st, *, nb, tb_rows, chunk, valid, n_tb):
    tb = pl.program_id(1)
    c = chunk
    n_heads = GLA_HEADS
    units = [(s, h) for s in range(nb) for h in range(n_heads)]

    @pl.when(tb == 0)
    def _():
        st[...] = s0_ref[...]

    ii = lax.broadcasted_iota(jnp.int32, (c, c), 0)
    jj = lax.broadcasted_iota(jnp.int32, (c, c), 1)
    lower = (ii >= jj)
    lower_f = lower.astype(F32)
    rid = lax.broadcasted_iota(jnp.int32, (c, 1), 0)
    rowmask = (rid < valid) if valid < c else None
    n_sub = max(c // 16, 1)
    sub = c // n_sub

    def chunk_step(ci, carry):
        rows, bcs, bc_ts = [], [], []
        for s in range(nb):
            rr = pl.ds(_chunk_rows(s, tb_rows, ci, c), c)
            gk = jax.nn.log_sigmoid(_dot(sm_ref[rr, :], wgk_ref[...]) + bgk_ref[...]) / GLA_GATE_NORMALIZER
            if rowmask is not None:
                gk = jnp.where(rowmask, gk, 0.0)
            bc = _dot_hi(lower_f, gk)
            rows.append(rr)
            bcs.append(bc)
            bc_ts.append(bc.T)
        q, k, v, bch = {}, {}, {}, {}
        for (s, h) in units:
            ks = slice(h * GLA_DK, (h + 1) * GLA_DK)
            vs = slice(h * GLA_DV, (h + 1) * GLA_DV)
            q[s, h] = q_ref[rows[s], ks] * (GLA_DK ** -0.5)
            kk = k_ref[rows[s], ks]
            vv = v_ref[rows[s], vs]
            if rowmask is not None:
                kk = jnp.where(rowmask, kk, 0.0)
                vv = jnp.where(rowmask, vv, 0.0)
            k[s, h], v[s, h] = kk, vv
            bch[s, h] = bcs[s][:, ks]
        o_inter = {u: _dot(q[u] * jnp.exp(bch[u]), st[u[0], u[1]]) for u in units}
        a = {}
        for u in units:
            q_parts, k_parts = [], []
            for sb in range(n_sub):
                ref_row = bch[u][sb * sub:sb * sub + 1, :]
                in_blk = (rid >= sb * sub) & (rid < (sb + 1) * sub)
                q_parts.append(jnp.where(in_blk, q[u] * jnp.exp(jnp.where(in_blk, bch[u] - ref_row, 0.0)), 0.0))
                k_parts.append(k[u] * jnp.exp(jnp.where(rid < (sb + 1) * sub, ref_row - bch[u], 0.0)))
            q_hat = jnp.concatenate(q_parts, axis=1) if n_sub > 1 else q_parts[0]
            k_hat = jnp.concatenate(k_parts, axis=1) if n_sub > 1 else k_parts[0]
            a[u] = jnp.where(lower, _dot_nt(q_hat, k_hat), 0.0)
        upd = {u: _dot_tn(k[u] * jnp.exp(bch[u][c - 1:c, :] - bch[u]), v[u]) for u in units}
        o = {u: o_inter[u] + _dot(a[u], v[u]) for u in units}
        for (s, h) in units:
            dec_col = bc_ts[s][h * GLA_DK:(h + 1) * GLA_DK, c - 1:c]
            st[s, h] = jnp.exp(dec_col) * st[s, h] + upd[s, h]
        for s in range(nb):
            o_ref[rows[s], :] = jnp.concatenate(
                [_gated_norm(o[s, h], nw_ref[...], go_ref[rows[s], h * GLA_DV:(h + 1) * GLA_DV])
                 for h in range(n_heads)], axis=1)
        return carry

    _for_chunks(tb_rows // c, chunk_step)

    @pl.when(tb == n_tb - 1)
    def _():
        sout_ref[...] = st[...]


def _gla(proj, n_seq, nb, t_len, tb_rows, chunk, valid, s0, wgk, bgk, nw):
    n_tb = t_len // tb_rows
    assert nb == 1 or n_tb == 1
    rows = nb * tb_rows

    def rowblk(b, t):
        return b * n_tb + t

    body = functools.partial(_gla_body, nb=nb, tb_rows=tb_rows, chunk=chunk, valid=valid, n_tb=n_tb)
    return pl.pallas_call(
        body,
        grid=(n_seq // nb, n_tb),
        in_specs=[
            pl.BlockSpec((rows, GLA_QK_W), lambda b, t: (rowblk(b, t), COL_GQ // GLA_QK_W)),
            pl.BlockSpec((rows, GLA_QK_W), lambda b, t: (rowblk(b, t), COL_GK // GLA_QK_W)),
            pl.BlockSpec((rows, GLA_V_W), lambda b, t: (rowblk(b, t), COL_GV // GLA_V_W)),
            pl.BlockSpec((rows, GLA_V_W), lambda b, t: (rowblk(b, t), COL_GG // GLA_V_W)),
            pl.BlockSpec((rows, SM_W), lambda b, t: (rowblk(b, t), COL_SM // SM_W)),
            pl.BlockSpec((nb, GLA_HEADS, GLA_DK, GLA_DV), lambda b, t: (b, 0, 0, 0)),
            pl.BlockSpec((SM_W, GLA_QK_W), lambda b, t: (0, 0)),
            pl.BlockSpec((1, GLA_QK_W), lambda b, t: (0, 0)),
            pl.BlockSpec((1, GLA_DV), lambda b, t: (0, 0)),
        ],
        out_specs=[
            pl.BlockSpec((rows, GLA_V_W), lambda b, t: (rowblk(b, t), 0)),
            pl.BlockSpec((nb, GLA_HEADS, GLA_DK, GLA_DV), lambda b, t: (b, 0, 0, 0)),
        ],
        out_shape=[
            jax.ShapeDtypeStruct((n_seq * t_len, GLA_V_W), F32),
            jax.ShapeDtypeStruct((n_seq, GLA_HEADS, GLA_DK, GLA_DV), F32),
        ],
        scratch_shapes=[pltpu.VMEM((nb, GLA_HEADS, GLA_DK, GLA_DV), F32)],
        compiler_params=pltpu.CompilerParams(dimension_semantics=("arbitrary", "arbitrary"),
                                             vmem_limit_bytes=VMEM_LIMIT),
        name="gla_mixer",
    )(proj, proj, proj, proj, proj, s0, wgk, bgk, nw)


def _outproj_body(og_ref, ol_ref, x_ref, wo_ref, g_ref, wr_ref, br_ref, x1_ref, h2_ref, lg_ref):
    o = jnp.concatenate([og_ref[...], ol_ref[...]], axis=1)
    x1 = x_ref[...] + jnp.dot(o.astype(BF16), wo_ref[...], preferred_element_type=F32)
    x1_ref[...] = x1
    h = _rms(x1, g_ref[...])
    h2_ref[...] = h
    lg_ref[...] = _dot_hi(h, wr_ref[...]) + br_ref[...]


def _outproj(og, ol, x, wo, g, wr, br):
    n = x.shape[0]
    return pl.pallas_call(
        _outproj_body,
        grid=(n // ROW_TILE,),
        in_specs=[
            pl.BlockSpec((ROW_TILE, GDN_V_W), lambda i: (i, 0)),
            pl.BlockSpec((ROW_TILE, GLA_V_W), lambda i: (i, 0)),
            pl.BlockSpec((ROW_TILE, D_MODEL), lambda i: (i, 0)),
            pl.BlockSpec((D_MODEL, D_MODEL), lambda i: (0, 0)),
            pl.BlockSpec((1, D_MODEL), lambda i: (0, 0)),
            pl.BlockSpec((D_MODEL, LANE), lambda i: (0, 0)),
            pl.BlockSpec((1, LANE), lambda i: (0, 0)),
        ],
        out_specs=[
            pl.BlockSpec((ROW_TILE, D_MODEL), lambda i: (i, 0)),
            pl.BlockSpec((ROW_TILE, D_MODEL), lambda i: (i, 0)),
            pl.BlockSpec((ROW_TILE, LANE), lambda i: (i, 0)),
        ],
        out_shape=[
            jax.ShapeDtypeStruct((n, D_MODEL), F32),
            jax.ShapeDtypeStruct((n, D_MODEL), F32),
            jax.ShapeDtypeStruct((n, LANE), F32),
        ],
        compiler_params=pltpu.CompilerParams(dimension_semantics=("arbitrary",),
                                             vmem_limit_bytes=VMEM_LIMIT),
        name="out_proj",
    )(og, ol, x, wo, g, wr, br)


def _row_copy(src_hbm, row, dst, slot, sem):
    return pltpu.make_async_copy(src_hbm.at[pl.ds(row, 1), :], dst.at[pl.ds(slot, 1), :], sem)


def _expert_body(be_ref, nu_ref, idx_ref, h_hbm, wup_ref, bup_ref, wdn_ref, bdn_ref, y_ref, xbuf, sem):
    i = pl.program_id(0)

    @pl.when(i < nu_ref[0])
    def _():
        def issue(r, carry):
            _row_copy(h_hbm, idx_ref[0, r], xbuf, r, sem).start()
            return carry

        lax.fori_loop(0, EXPERT_ROWS, issue, 0)
        pltpu.make_async_copy(h_hbm.at[pl.ds(0, EXPERT_ROWS), :], xbuf, sem).wait()
        gu = _dot(xbuf[...], wup_ref[...]) + bup_ref[...]
        gate = jnp.minimum(gu[:, :D_FF], SWIGLU_LIMIT)
        up = jnp.clip(gu[:, D_FF:], -SWIGLU_LIMIT, SWIGLU_LIMIT)
        a = (up + 1.0) * gate * jax.nn.sigmoid(SWIGLU_ALPHA * gate)
        y_ref[...] = _dot(a, wdn_ref[...]) + bdn_ref[...]

    @pl.when(i >= nu_ref[0])
    def _():
        y_ref[...] = jnp.zeros_like(y_ref)


def _experts(block_e, n_used, src_tok, h2, w_up, b_up, w_down, b_down):
    n_blocks = block_e.shape[0]
    grid_spec = pltpu.PrefetchScalarGridSpec(
        num_scalar_prefetch=2,
        grid=(n_blocks,),
        in_specs=[
            pl.BlockSpec((None, 1, EXPERT_ROWS), lambda i, be, nu: (i, 0, 0), memory_space=pltpu.SMEM),
            pl.BlockSpec(memory_space=pl.ANY),
            pl.BlockSpec((None, D_MODEL, 2 * D_FF), lambda i, be, nu: (be[i], 0, 0)),
            pl.BlockSpec((None, 1, 2 * D_FF), lambda i, be, nu: (be[i], 0, 0)),
            pl.BlockSpec((None, D_FF, D_MODEL), lambda i, be, nu: (be[i], 0, 0)),
            pl.BlockSpec((None, 1, D_MODEL), lambda i, be, nu: (be[i], 0, 0)),
        ],
        out_specs=pl.BlockSpec((EXPERT_ROWS, D_MODEL), lambda i, be, nu: (i, 0)),
        scratch_shapes=[pltpu.VMEM((EXPERT_ROWS, D_MODEL), F32), pltpu.SemaphoreType.DMA(())],
    )
    return pl.pallas_call(
        _expert_body,
        grid_spec=grid_spec,
        out_shape=jax.ShapeDtypeStruct((n_blocks * EXPERT_ROWS, D_MODEL), F32),
        compiler_params=pltpu.CompilerParams(dimension_semantics=("arbitrary",),
                                             vmem_limit_bytes=VMEM_LIMIT),
        name="experts",
    )(block_e, n_used, src_tok.reshape(n_blocks, 1, EXPERT_ROWS), h2, w_up,
      b_up.reshape(N_EXPERTS, 1, 2 * D_FF), w_down, b_down.reshape(N_EXPERTS, 1, D_MODEL))


def _combine_body(dest_ref, gates_ref, y_hbm, x1_ref, g_ref, o_ref, ybuf, sem):
    def issue(r, carry):
        for k in range(TOP_K):
            _row_copy(y_hbm, dest_ref[0, r * TOP_K + k], ybuf.at[k], r, sem).start()
        return carry

    lax.fori_loop(0, COMBINE_ROWS, issue, 0)
    for k in range(TOP_K):
        pltpu.make_async_copy(y_hbm.at[pl.ds(0, COMBINE_ROWS), :], ybuf.at[k], sem).wait()
    acc = x1_ref[...]
    moe = ybuf[0] * gates_ref[:, 0:1]
    for k in range(1, TOP_K):
        moe = moe + ybuf[k] * gates_ref[:, k:k + 1]
    o_ref[...] = _rms(acc + moe, g_ref[...])


def _combine(dest, gates, y, x1, g):
    n = x1.shape[0]
    n_blk = n // COMBINE_ROWS
    return pl.pallas_call(
        _combine_body,
        grid=(n_blk,),
        in_specs=[
            pl.BlockSpec((None, 1, COMBINE_ROWS * TOP_K), lambda i: (i, 0, 0), memory_space=pltpu.SMEM),
            pl.BlockSpec((COMBINE_ROWS, TOP_K), lambda i: (i, 0)),
            pl.BlockSpec(memory_space=pl.ANY),
            pl.BlockSpec((COMBINE_ROWS, D_MODEL), lambda i: (i, 0)),
            pl.BlockSpec((1, D_MODEL), lambda i: (0, 0)),
        ],
        out_specs=pl.BlockSpec((COMBINE_ROWS, D_MODEL), lambda i: (i, 0)),
        out_shape=jax.ShapeDtypeStruct((n, D_MODEL), F32),
        scratch_shapes=[pltpu.VMEM((TOP_K, COMBINE_ROWS, D_MODEL), F32), pltpu.SemaphoreType.DMA(())],
        compiler_params=pltpu.CompilerParams(dimension_semantics=("arbitrary",),
                                             vmem_limit_bytes=VMEM_LIMIT),
        name="combine",
    )(dest.reshape(n_blk, 1, COMBINE_ROWS * TOP_K), gates, y, x1, g)


def _route(logits):
    n = logits.shape[0]
    top_v, top_i = lax.top_k(logits[:, :N_EXPERTS], TOP_K)
    gates = jax.nn.softmax(top_v, axis=-1)
    flat_e = top_i.reshape(-1).astype(jnp.int32)
    n_assign = n * TOP_K
    onehot = (flat_e[:, None] == jnp.arange(N_EXPERTS, dtype=jnp.int32)[None, :]).astype(jnp.int32)
    csum = jnp.cumsum(onehot, axis=0)
    rank = jnp.take_along_axis(csum, flat_e[:, None], axis=1)[:, 0] - 1
    counts = csum[-1]
    padded = (counts + EXPERT_ROWS - 1) // EXPERT_ROWS * EXPERT_ROWS
    pend = jnp.cumsum(padded)
    pstart = pend - padded
    dest = (pstart[flat_e] + rank).astype(jnp.int32)
    n_rows = n_assign + N_EXPERTS * EXPERT_ROWS
    n_blocks = n_rows // EXPERT_ROWS
    src_tok = jnp.zeros((n_rows,), jnp.int32).at[dest].set(jnp.arange(n_assign, dtype=jnp.int32) // TOP_K)
    block_e = jnp.minimum(jnp.searchsorted(pend, jnp.arange(n_blocks, dtype=jnp.int32) * EXPERT_ROWS, side="right"),
                          N_EXPERTS - 1).astype(jnp.int32)
    n_used = (pend[-1] // EXPERT_ROWS).astype(jnp.int32).reshape(1)
    return gates, dest, src_tok, block_e, n_used


def _pad_lanes(v, width):
    return jnp.zeros((1, width), F32).at[0, :v.shape[0]].set(v.astype(F32))


def kernel(x_prompt, x_sample, state_gdn_conv, state_gdn, state_gla, rms_mix_w, w_in, conv_w, gdn_a_log,
           gdn_dt_bias, gdn_norm_w, gla_gk_w, gla_gk_b, gla_norm_w, w_out, rms_ffn_w, w_router, b_router,
           w_up, b_up, w_down, b_down, rms_final_w):
    bp, tp, d = x_prompt.shape
    bs, ts, _ = x_sample.shape
    n_p, n_s = bp * tp, bs * ts
    assert d == D_MODEL and state_gdn.shape[0] == 1, "single-layer kernel"
    l = 0

    wi = w_in[l]
    a0 = GDN_CONV_CH + GDN_V_W
    g0 = a0 + 2 * GDN_HEADS
    lr0 = g0 + 2 * GLA_QK_W + 2 * GLA_V_W
    small = jnp.concatenate([wi[:, a0:a0 + 2 * GDN_HEADS], wi[:, lr0:lr0 + GLA_GATE_RANK],
                             jnp.zeros((d, SM_W - 2 * GDN_HEADS - GLA_GATE_RANK), F32)], axis=1)
    w_big = jnp.concatenate([wi[:, :a0], wi[:, g0:lr0], small], axis=1).astype(BF16)
    alog = _pad_lanes(gdn_a_log[l], SM_W)
    dtb = _pad_lanes(gdn_dt_bias[l], SM_W)
    wgk = jnp.zeros((SM_W, GLA_QK_W), F32).at[SM_LR:SM_LR + GLA_GATE_RANK].set(gla_gk_w[l])
    wr = jnp.zeros((d, LANE), F32).at[:, :N_EXPERTS].set(w_router[l])
    br = jnp.full((1, LANE), -1e30, F32).at[0, :N_EXPERTS].set(b_router[l])

    x_all = jnp.concatenate([x_prompt.reshape(n_p, d), x_sample.reshape(n_s, d)], axis=0)
    proj = _inproj(x_all, rms_mix_w[l][None, :], w_big)

    tb_p = PROMPT_TIME_BLOCK
    zeros_conv = jnp.zeros((bp, CONV_WIDTH - 1, GDN_CONV_CH), F32)
    og_p, gdn_p = _gdn(proj, bp, 1, tp, tb_p, CHUNK, CHUNK, zeros_conv,
                       jnp.zeros((bp, GDN_HEADS, GDN_DK, GDN_DV), F32), conv_w[l], alog, dtb, gdn_norm_w[l][None, :])
    ol_p, gla_p = _gla(proj, bp, 1, tp, tb_p, CHUNK, CHUNK, jnp.zeros((bp, GLA_HEADS, GLA_DK, GLA_DV), F32),
                       wgk, gla_gk_b[l][None, :], gla_norm_w[l][None, :])

    ts_pad = SUBLANE
    nb_s = SAMPLE_SEQS_PER_STEP
    proj_s = proj[n_p:].reshape(bs, ts, PROJ_W)
    proj_sp = jnp.pad(proj_s, ((0, 0), (0, ts_pad - ts), (0, 0))).reshape(bs * ts_pad, PROJ_W)
    og_s, gdn_s = _gdn(proj_sp, bs, nb_s, ts_pad, ts_pad, ts_pad, ts, state_gdn_conv[l], state_gdn[l], conv_w[l],
                       alog, dtb, gdn_norm_w[l][None, :])
    ol_s, gla_s = _gla(proj_sp, bs, nb_s, ts_pad, ts_pad, ts_pad, ts, state_gla[l], wgk, gla_gk_b[l][None, :],
                       gla_norm_w[l][None, :])
    og_s = og_s.reshape(bs, ts_pad, GDN_V_W)[:, :ts].reshape(n_s, GDN_V_W)
    ol_s = ol_s.reshape(bs, ts_pad, GLA_V_W)[:, :ts].reshape(n_s, GLA_V_W)

    og = jnp.concatenate([og_p, og_s], axis=0)
    ol = jnp.concatenate([ol_p, ol_s], axis=0)
    x1, h2, logits = _outproj(og, ol, x_all, w_out[l].astype(BF16), rms_ffn_w[l][None, :], wr, br)

    gates, dest, src_tok, block_e, n_used = _route(logits)
    y_rows = _experts(block_e, n_used, src_tok, h2, w_up[l], b_up[l], w_down[l], b_down[l])
    y_all = _combine(dest, gates, y_rows, x1, rms_final_w[None, :])

    y_prompt = y_all[:n_p].reshape(bp, tp, d)
    y_sample = y_all[n_p:].reshape(bs, ts, d)
    assert tp >= CONV_WIDTH - 1 and ts >= CONV_WIDTH - 1
    conv_p = proj[:n_p].reshape(bp, tp, PROJ_W)[:, tp - (CONV_WIDTH - 1):, :GDN_CONV_CH]
    conv_s = proj_s[:, ts - (CONV_WIDTH - 1):, :GDN_CONV_CH]
    return (y_prompt, y_sample, conv_p[None], gdn_p[None], gla_p[None], conv_s[None], gdn_s[None], gla_s[None])
```

```python
import functools

import jax
import jax.numpy as jnp
from jax import lax
from jax.experimental import pallas as pl
from jax.experimental.pallas import tpu as pltpu

F32 = jnp.float32
BF16 = jnp.bfloat16
HI = lax.Precision.HIGHEST

D_MODEL = 1024
GDN_HEADS = 4
GDN_DK = 128
GDN_DV = 128
GLA_HEADS = 4
GLA_DK = 64
GLA_DV = 128
GLA_GATE_RANK = 16
GLA_GATE_NORMALIZER = 16.0
CONV_WIDTH = 4
CHUNK = 64
N_EXPERTS = 32
TOP_K = 4
D_FF = 1024
SWIGLU_LIMIT = 7.0
SWIGLU_ALPHA = 1.702
RMS_EPS = 1e-6
L2_EPS = 1e-6

GDN_QK_W = GDN_HEADS * GDN_DK
GDN_V_W = GDN_HEADS * GDN_DV
GDN_CONV_CH = 2 * GDN_QK_W + GDN_V_W
GLA_QK_W = GLA_HEADS * GLA_DK
GLA_V_W = GLA_HEADS * GLA_DV

COL_QKV = 0
COL_Z = 1536
COL_GQ = 2048
COL_GK = 2304
COL_GV = 2560
COL_GG = 3072
COL_SM = 3584
SM_W = 128
PROJ_W = COL_SM + SM_W
SM_A, SM_B, SM_LR = 0, 4, 8

LANE = 128
SUBLANE = 8
ROW_TILE = 256
EXPERT_ROWS = 256
COMBINE_ROWS = 128
DMA_ISSUE_UNROLL = 8
PROMPT_TIME_BLOCK = 512
SAMPLE_SEQS_PER_STEP = 8
VMEM_LIMIT = 56 * 1024 * 1024


def _dot(a, b):
    return jnp.dot(a.astype(BF16), b.astype(BF16), preferred_element_type=F32)


def _dot_nt(a, b):
    return lax.dot_general(a.astype(BF16), b.astype(BF16), (((1,), (1,)), ((), ())),
                           preferred_element_type=F32)


def _dot_tn(a, b):
    return lax.dot_general(a.astype(BF16), b.astype(BF16), (((0,), (0,)), ((), ())),
                           preferred_element_type=F32)


def _dot_hi(a, b):
    return jnp.dot(a, b, precision=HI, preferred_element_type=F32)


def _rms(x, w):
    return x * lax.rsqrt(jnp.mean(x * x, axis=-1, keepdims=True) + RMS_EPS) * w


def _silu(x):
    return x * jax.nn.sigmoid(x)


def _inproj_body(x_ref, g_ref, w_ref, o_ref):
    h = _rms(x_ref[...], g_ref[...])
    o_ref[...] = jnp.dot(h.astype(BF16), w_ref[...], preferred_element_type=F32)


def _inproj(x, g, w):
    n = x.shape[0]
    return pl.pallas_call(
        _inproj_body,
        grid=(n // ROW_TILE,),
        in_specs=[
            pl.BlockSpec((ROW_TILE, D_MODEL), lambda i: (i, 0)),
            pl.BlockSpec((1, D_MODEL), lambda i: (0, 0)),
            pl.BlockSpec((D_MODEL, PROJ_W), lambda i: (0, 0)),
        ],
        out_specs=pl.BlockSpec((ROW_TILE, PROJ_W), lambda i: (i, 0)),
        out_shape=jax.ShapeDtypeStruct((n, PROJ_W), F32),
        compiler_params=pltpu.CompilerParams(dimension_semantics=("arbitrary",),
                                             vmem_limit_bytes=VMEM_LIMIT),
        name="in_proj",
    )(x, g, w)


def _log2(n):
    assert n & (n - 1) == 0
    return n.bit_length() - 1


def _tri_inv_all(ms, c, ii, jj):
    eye = (ii == jj).astype(F32)
    base = min(c, 8)
    sh = _log2(base)
    blk = (ii >> sh) == (jj >> sh)
    ns = [jnp.where(blk, m, 0.0) for m in ms]
    xs = [eye - n for n in ns]
    ps = [_dot(n, n) for n in ns]
    ts = [_dot(jnp.concatenate([x, p], axis=0), p) for x, p in zip(xs, ps)]
    xs = [x + t[:c] for x, t in zip(xs, ts)]
    ps = [t[c:] for t in ts]
    xs = [x + _dot(x, p) for x, p in zip(xs, ps)]
    s = base
    while s < c:
        sh_s, sh_b = _log2(s), _log2(2 * s)
        off = ((ii >> sh_b) == (jj >> sh_b)) & ((ii >> sh_s) != (jj >> sh_s))
        ys = [_dot(x, jnp.where(off, m, 0.0)) for x, m in zip(xs, ms)]
        xs = [x - _dot(y, x) for x, y in zip(xs, ys)]
        s *= 2
    return xs


def _gated_norm(o, w, z):
    return o * lax.rsqrt(jnp.mean(o * o, axis=-1, keepdims=True) + RMS_EPS) * w * _silu(z)


def _chunk_rows(s, tb_rows, ci, c):
    r = s * tb_rows + ci * c
    if not isinstance(r, int):
        r = pl.multiple_of(r, c)
    return r


def _for_chunks(n_chunks, step):
    if n_chunks == 1:
        step(0, 0)
    else:
        lax.fori_loop(0, n_chunks, step, 0)


def _gdn_body(qkv_ref, z_ref, sm_ref, cbuf_ref, s0_ref, cw_ref, alog_ref, dtb_ref, nw_ref,
              o_ref, sout_ref, cout_ref, st, xc, act, gcs, us, wss, qgs, kds, aqs,
              *, nb, tb_rows, chunk, valid, n_tb):
    tb = pl.program_id(1)
    c = chunk
    n_heads = GDN_HEADS
    tail = CONV_WIDTH - 1
    pad = SUBLANE
    units = [(s, h) for s in range(nb) for h in range(n_heads)]

    @pl.when(tb == 0)
    def _():
        st[...] = s0_ref[...]
        for s in range(nb):
            xc[s, pad - tail:pad, :] = cbuf_ref[s]

    if n_tb > 1:
        @pl.when(tb > 0)
        def _():
            for s in range(nb):
                xc[s, pad - tail:pad, :] = xc[s, tb_rows + pad - tail:tb_rows + pad, :]

    for s in range(nb):
        xc[s, pad:pad + tb_rows, :] = qkv_ref[s * tb_rows:(s + 1) * tb_rows, :]

    slab = min(tb_rows, 64)
    for s in range(nb):
        for sl in range(tb_rows // slab):
            for cb in range(GDN_CONV_CH // 512):
                cs = slice(cb * 512, (cb + 1) * 512)
                lo = pad - tail + sl * slab
                acc = xc[s, lo:lo + slab, cs] * cw_ref[0:1, cs]
                for i in range(1, CONV_WIDTH):
                    acc = acc + xc[s, lo + i:lo + i + slab, cs] * cw_ref[i:i + 1, cs]
                act[s * tb_rows + sl * slab:s * tb_rows + (sl + 1) * slab, cs] = _silu(acc)

    ii = lax.broadcasted_iota(jnp.int32, (c, c), 0)
    jj = lax.broadcasted_iota(jnp.int32, (c, c), 1)
    lower = (ii >= jj)
    lower_f = lower.astype(F32)
    strict = (ii > jj)
    rowmask = None
    if valid < c:
        rowmask = lax.broadcasted_iota(jnp.int32, (c, 1), 0) < valid

    def hs(h, w):
        return slice(h * w, (h + 1) * w)

    def phase1(ci, carry):
        rows, b_ts, gc_ts, gc_tts = [], [], [], []
        for s in range(nb):
            rr = pl.ds(_chunk_rows(s, tb_rows, ci, c), c)
            sm = sm_ref[rr, :]
            g_t = -jnp.exp(alog_ref[...]) * jax.nn.softplus(sm + dtb_ref[...])
            b_t = jax.nn.sigmoid(sm)
            if rowmask is not None:
                g_t = jnp.where(rowmask, g_t, 0.0)
                b_t = jnp.where(rowmask, b_t, 0.0)
            gc_t = _dot_hi(lower_f, g_t)
            gcs[rr, :] = gc_t
            rows.append(rr)
            b_ts.append(b_t)
            gc_ts.append(gc_t)
            gc_tts.append(gc_t.T)
        qn, kn, kb, vb = {}, {}, {}, {}
        for (s, h) in units:
            q = act[rows[s], hs(h, GDN_DK)]
            k = act[rows[s], slice(GDN_QK_W + h * GDN_DK, GDN_QK_W + (h + 1) * GDN_DK)]
            v = act[rows[s], slice(2 * GDN_QK_W + h * GDN_DV, 2 * GDN_QK_W + (h + 1) * GDN_DV)]
            if rowmask is not None:
                q = jnp.where(rowmask, q, 0.0)
                k = jnp.where(rowmask, k, 0.0)
                v = jnp.where(rowmask, v, 0.0)
            qn[s, h] = q * lax.rsqrt(jnp.sum(q * q, axis=-1, keepdims=True) + L2_EPS) * (GDN_DK ** -0.5)
            kn[s, h] = k * lax.rsqrt(jnp.sum(k * k, axis=-1, keepdims=True) + L2_EPS)
            beta = b_ts[s][:, SM_B + h:SM_B + h + 1]
            kb[s, h] = kn[s, h] * beta
            vb[s, h] = v * beta
        s1 = {u: _dot_nt(jnp.concatenate([kb[u], qn[u]], axis=0), kn[u]) for u in units}
        mm = []
        for (s, h) in units:
            gcol = gc_ts[s][:, SM_A + h:SM_A + h + 1]
            grow = gc_tts[s][SM_A + h:SM_A + h + 1, :]
            dec = jnp.exp(jnp.where(lower, gcol - grow, -jnp.inf))
            mm.append(jnp.where(strict, s1[s, h][:c] * dec, 0.0))
            aqs[h, rows[s], :] = s1[s, h][c:] * dec
        tms = _tri_inv_all(mm, c, ii, jj)
        for (s, h), tm in zip(units, tms):
            gcol = gc_ts[s][:, SM_A + h:SM_A + h + 1]
            eg = jnp.exp(gcol)
            uw = _dot(tm, jnp.concatenate([vb[s, h], kb[s, h] * eg], axis=1))
            us[rows[s], hs(h, GDN_DV)] = uw[:, :GDN_DV]
            wss[rows[s], hs(h, GDN_DV)] = uw[:, GDN_DV:]
            qgs[rows[s], hs(h, GDN_DK)] = qn[s, h] * eg
            kds[rows[s], hs(h, GDN_DK)] = kn[s, h] * jnp.exp(gcol[c - 1:c, :] - gcol)
        return carry

    def phase2(ci, carry):
        r0 = [_chunk_rows(s, tb_rows, ci, c) for s in range(nb)]
        rows = [pl.ds(r, c) for r in r0]
        ws = {(s, h): _dot(jnp.concatenate([wss[rows[s], hs(h, GDN_DV)], qgs[rows[s], hs(h, GDN_DK)]], axis=0),
                           st[s, h]) for (s, h) in units}
        v_new = {(s, h): us[rows[s], hs(h, GDN_DV)] - ws[s, h][:c] for (s, h) in units}
        o = {(s, h): ws[s, h][c:] + _dot(aqs[h, rows[s], :], v_new[s, h]) for (s, h) in units}
        upd = {(s, h): _dot_tn(kds[rows[s], hs(h, GDN_DK)], v_new[s, h]) for (s, h) in units}
        for (s, h) in units:
            g_last = gcs[pl.ds(r0[s] + c - 1, 1), SM_A + h:SM_A + h + 1]
            st[s, h] = st[s, h] * jnp.exp(g_last) + upd[s, h]
        for s in range(nb):
            o_ref[rows[s], :] = jnp.concatenate(
                [_gated_norm(o[s, h], nw_ref[...], z_ref[rows[s], hs(h, GDN_DV)]) for h in range(n_heads)], axis=1)
        return carry

    n_chunks = tb_rows // c
    _for_chunks(n_chunks, phase1)
    _for_chunks(n_chunks, phase2)

    @pl.when(tb == n_tb - 1)
    def _():
        sout_ref[...] = st[...]
        last = tb_rows if valid == c else valid
        for s in range(nb):
            cout_ref[s] = xc[s, pad + last - tail:pad + last, :]


def _gdn(proj, n_seq, nb, t_len, tb_rows, chunk, valid, conv_buf, s0, conv_w, alog, dtb, nw):
    n_tb = t_len // tb_rows
    assert nb == 1 or n_tb == 1
    rows = nb * tb_rows

    def rowblk(b, t):
        return b * n_tb + t

    body = functools.partial(_gdn_body, nb=nb, tb_rows=tb_rows, chunk=chunk, valid=valid, n_tb=n_tb)
    return pl.pallas_call(
        body,
        grid=(n_seq // nb, n_tb),
        in_specs=[
            pl.BlockSpec((rows, GDN_CONV_CH), lambda b, t: (rowblk(b, t), COL_QKV // GDN_CONV_CH)),
            pl.BlockSpec((rows, GDN_V_W), lambda b, t: (rowblk(b, t), COL_Z // GDN_V_W)),
            pl.BlockSpec((rows, SM_W), lambda b, t: (rowblk(b, t), COL_SM // SM_W)),
            pl.BlockSpec((nb, CONV_WIDTH - 1, GDN_CONV_CH), lambda b, t: (b, 0, 0)),
            pl.BlockSpec((nb, GDN_HEADS, GDN_DK, GDN_DV), lambda b, t: (b, 0, 0, 0)),
            pl.BlockSpec((CONV_WIDTH, GDN_CONV_CH), lambda b, t: (0, 0)),
            pl.BlockSpec((1, SM_W), lambda b, t: (0, 0)),
            pl.BlockSpec((1, SM_W), lambda b, t: (0, 0)),
            pl.BlockSpec((1, GDN_DV), lambda b, t: (0, 0)),
        ],
        out_specs=[
            pl.BlockSpec((rows, GDN_V_W), lambda b, t: (rowblk(b, t), 0)),
            pl.BlockSpec((nb, GDN_HEADS, GDN_DK, GDN_DV), lambda b, t: (b, 0, 0, 0)),
            pl.BlockSpec((nb, CONV_WIDTH - 1, GDN_CONV_CH), lambda b, t: (b, 0, 0)),
        ],
        out_shape=[
            jax.ShapeDtypeStruct((n_seq * t_len, GDN_V_W), F32),
            jax.ShapeDtypeStruct((n_seq, GDN_HEADS, GDN_DK, GDN_DV), F32),
            jax.ShapeDtypeStruct((n_seq, CONV_WIDTH - 1, GDN_CONV_CH), F32),
        ],
        scratch_shapes=[
            pltpu.VMEM((nb, GDN_HEADS, GDN_DK, GDN_DV), F32),
            pltpu.VMEM((nb, tb_rows + SUBLANE, GDN_CONV_CH), F32),
            pltpu.VMEM((rows, GDN_CONV_CH), F32),
            pltpu.VMEM((rows, SM_W), F32),
            pltpu.VMEM((rows, GDN_V_W), F32),
            pltpu.VMEM((rows, GDN_V_W), F32),
            pltpu.VMEM((rows, GDN_QK_W), F32),
            pltpu.VMEM((rows, GDN_QK_W), F32),
            pltpu.VMEM((GDN_HEADS, rows, chunk), F32),
        ],
        compiler_params=pltpu.CompilerParams(dimension_semantics=("arbitrary", "arbitrary"),
                                             vmem_limit_bytes=VMEM_LIMIT),
        name="gdn_mixer",
    )(proj, proj, proj, conv_buf, s0, conv_w, alog, dtb, nw)


def _gla_body(q_ref, k_ref, v_ref, go_ref, sm_ref, s0_ref, wgk_ref, bgk_ref, nw_ref,
              o_ref, sout_ref, st, *, nb, tb_rows, chunk, valid, n_tb):
    tb = pl.program_id(1)
    c = chunk
    n_heads = GLA_HEADS
    units = [(s, h) for s in range(nb) for h in range(n_heads)]

    @pl.when(tb == 0)
    def _():
        st[...] = s0_ref[...]

    ii = lax.broadcasted_iota(jnp.int32, (c, c), 0)
    jj = lax.broadcasted_iota(jnp.int32, (c, c), 1)
    lower = (ii >= jj)
    lower_f = lower.astype(F32)
    rid = lax.broadcasted_iota(jnp.int32, (c, 1), 0)
    rowmask = (rid < valid) if valid < c else None
    n_sub = max(c // 16, 1)
    sub = c // n_sub

    def chunk_step(ci, carry):
        rows, bcs, bc_ts = [], [], []
        for s in range(nb):
            rr = pl.ds(_chunk_rows(s, tb_rows, ci, c), c)
            gk = jax.nn.log_sigmoid(_dot(sm_ref[rr, :], wgk_ref[...]) + bgk_ref[...]) / GLA_GATE_NORMALIZER
            if rowmask is not None:
                gk = jnp.where(rowmask, gk, 0.0)
            bc = _dot_hi(lower_f, gk)
            rows.append(rr)
            bcs.append(bc)
            bc_ts.append(bc.T)
        q, k, v, bch = {}, {}, {}, {}
        for (s, h) in units:
            ks = slice(h * GLA_DK, (h + 1) * GLA_DK)
            vs = slice(h * GLA_DV, (h + 1) * GLA_DV)
            q[s, h] = q_ref[rows[s], ks] * (GLA_DK ** -0.5)
            kk = k_ref[rows[s], ks]
            vv = v_ref[rows[s], vs]
            if rowmask is not None:
                kk = jnp.where(rowmask, kk, 0.0)
                vv = jnp.where(rowmask, vv, 0.0)
            k[s, h], v[s, h] = kk, vv
            bch[s, h] = bcs[s][:, ks]
        o_inter = {u: _dot(q[u] * jnp.exp(bch[u]), st[u[0], u[1]]) for u in units}
        a = {}
        for u in units:
            q_parts, k_parts = [], []
            for sb in range(n_sub):
                ref_row = bch[u][sb * sub:sb * sub + 1, :]
                in_blk = (rid >= sb * sub) & (rid < (sb + 1) * sub)
                q_parts.append(jnp.where(in_blk, q[u] * jnp.exp(jnp.where(in_blk, bch[u] - ref_row, 0.0)), 0.0))
                k_parts.append(k[u] * jnp.exp(jnp.where(rid < (sb + 1) * sub, ref_row - bch[u], 0.0)))
            q_hat = jnp.concatenate(q_parts, axis=1) if n_sub > 1 else q_parts[0]
            k_hat = jnp.concatenate(k_parts, axis=1) if n_sub > 1 else k_parts[0]
            a[u] = jnp.where(lower, _dot_nt(q_hat, k_hat), 0.0)
        upd = {u: _dot_tn(k[u] * jnp.exp(bch[u][c - 1:c, :] - bch[u]), v[u]) for u in units}
        o = {u: o_inter[u] + _dot(a[u], v[u]) for u in units}
        for (s, h) in units:
            dec_col = bc_ts[s][h * GLA_DK:(h + 1) * GLA_DK, c - 1:c]
            st[s, h] = jnp.exp(dec_col) * st[s, h] + upd[s, h]
        for s in range(nb):
            o_ref[rows[s], :] = jnp.concatenate(
                [_gated_norm(o[s, h], nw_ref[...], go_ref[rows[s], h * GLA_DV:(h + 1) * GLA_DV])
                 for h in range(n_heads)], axis=1)
        return carry

    _for_chunks(tb_rows // c, chunk_step)

    @pl.when(tb == n_tb - 1)
    def _():
        sout_ref[...] = st[...]


def _gla(proj, n_seq, nb, t_len, tb_rows, chunk, valid, s0, wgk, bgk, nw):
    n_tb = t_len // tb_rows
    assert nb == 1 or n_tb == 1
    rows = nb * tb_rows

    def rowblk(b, t):
        return b * n_tb + t

    body = functools.partial(_gla_body, nb=nb, tb_rows=tb_rows, chunk=chunk, valid=valid, n_tb=n_tb)
    return pl.pallas_call(
        body,
        grid=(n_seq // nb, n_tb),
        in_specs=[
            pl.BlockSpec((rows, GLA_QK_W), lambda b, t: (rowblk(b, t), COL_GQ // GLA_QK_W)),
            pl.BlockSpec((rows, GLA_QK_W), lambda b, t: (rowblk(b, t), COL_GK // GLA_QK_W)),
            pl.BlockSpec((rows, GLA_V_W), lambda b, t: (rowblk(b, t), COL_GV // GLA_V_W)),
            pl.BlockSpec((rows, GLA_V_W), lambda b, t: (rowblk(b, t), COL_GG // GLA_V_W)),
            pl.BlockSpec((rows, SM_W), lambda b, t: (rowblk(b, t), COL_SM // SM_W)),
            pl.BlockSpec((nb, GLA_HEADS, GLA_DK, GLA_DV), lambda b, t: (b, 0, 0, 0)),
            pl.BlockSpec((SM_W, GLA_QK_W), lambda b, t: (0, 0)),
            pl.BlockSpec((1, GLA_QK_W), lambda b, t: (0, 0)),
            pl.BlockSpec((1, GLA_DV), lambda b, t: (0, 0)),
        ],
        out_specs=[
            pl.BlockSpec((rows, GLA_V_W), lambda b, t: (rowblk(b, t), 0)),
            pl.BlockSpec((nb, GLA_HEADS, GLA_DK, GLA_DV), lambda b, t: (b, 0, 0, 0)),
        ],
        out_shape=[
            jax.ShapeDtypeStruct((n_seq * t_len, GLA_V_W), F32),
            jax.ShapeDtypeStruct((n_seq, GLA_HEADS, GLA_DK, GLA_DV), F32),
        ],
        scratch_shapes=[pltpu.VMEM((nb, GLA_HEADS, GLA_DK, GLA_DV), F32)],
        compiler_params=pltpu.CompilerParams(dimension_semantics=("arbitrary", "arbitrary"),
                                             vmem_limit_bytes=VMEM_LIMIT),
        name="gla_mixer",
    )(proj, proj, proj, proj, proj, s0, wgk, bgk, nw)


def _outproj_body(og_ref, ol_ref, x_ref, wo_ref, g_ref, wr_ref, br_ref,
                  x1_ref, h2_ref, rt_ref, cnt_ref, base):
    i = pl.program_id(0)

    @pl.when(i == 0)
    def _():
        base[...] = jnp.zeros_like(base)

    o = jnp.concatenate([og_ref[...], ol_ref[...]], axis=1)
    x1 = x_ref[...] + jnp.dot(o.astype(BF16), wo_ref[...], preferred_element_type=F32)
    x1_ref[...] = x1
    h = _rms(x1, g_ref[...])
    h2_ref[...] = h
    logits = _dot_hi(h, wr_ref[...]) + br_ref[...]

    tm = logits.shape[0]
    lane = lax.broadcasted_iota(jnp.int32, (tm, LANE), 1)
    work = logits
    sel = jnp.zeros((tm, LANE), F32)
    ids, vals = [], []
    for _ in range(TOP_K):
        m = jnp.max(work, axis=-1, keepdims=True)
        idx = jnp.min(jnp.where(work == m, lane, LANE), axis=-1, keepdims=True)
        hit = lane == idx
        ids.append(idx)
        vals.append(m)
        work = jnp.where(hit, -jnp.inf, work)
        sel = sel + hit.astype(F32)
    exps = [jnp.exp(v - vals[0]) for v in vals]
    den = exps[0]
    for e in exps[1:]:
        den = den + e
    gates = [e / den for e in exps]

    ri = lax.broadcasted_iota(jnp.int32, (tm, tm), 0)
    ci = lax.broadcasted_iota(jnp.int32, (tm, tm), 1)
    before = _dot((ri > ci).astype(F32), sel) + base[...]
    ranks = [jnp.sum(jnp.where(lane == idx, before, 0.0), axis=-1, keepdims=True) for idx in ids]
    base[...] = base[...] + jnp.sum(sel, axis=0, keepdims=True)
    cnt_ref[...] = base[...]

    rec = jnp.zeros((tm, LANE), F32)
    for k in range(TOP_K):
        rec = jnp.where(lane == k, ids[k].astype(F32), rec)
        rec = jnp.where(lane == TOP_K + k, ranks[k], rec)
        rec = jnp.where(lane == 2 * TOP_K + k, gates[k], rec)
    rt_ref[...] = rec


def _outproj(og, ol, x, wo, g, wr, br):
    n = x.shape[0]
    return pl.pallas_call(
        _outproj_body,
        grid=(n // ROW_TILE,),
        in_specs=[
            pl.BlockSpec((ROW_TILE, GDN_V_W), lambda i: (i, 0)),
            pl.BlockSpec((ROW_TILE, GLA_V_W), lambda i: (i, 0)),
            pl.BlockSpec((ROW_TILE, D_MODEL), lambda i: (i, 0)),
            pl.BlockSpec((D_MODEL, D_MODEL), lambda i: (0, 0)),
            pl.BlockSpec((1, D_MODEL), lambda i: (0, 0)),
            pl.BlockSpec((D_MODEL, LANE), lambda i: (0, 0)),
            pl.BlockSpec((1, LANE), lambda i: (0, 0)),
        ],
        out_specs=[
            pl.BlockSpec((ROW_TILE, D_MODEL), lambda i: (i, 0)),
            pl.BlockSpec((ROW_TILE, D_MODEL), lambda i: (i, 0)),
            pl.BlockSpec((ROW_TILE, LANE), lambda i: (i, 0)),
            pl.BlockSpec((1, LANE), lambda i: (0, 0)),
        ],
        out_shape=[
            jax.ShapeDtypeStruct((n, D_MODEL), F32),
            jax.ShapeDtypeStruct((n, D_MODEL), F32),
            jax.ShapeDtypeStruct((n, LANE), F32),
            jax.ShapeDtypeStruct((1, LANE), F32),
        ],
        scratch_shapes=[pltpu.VMEM((1, LANE), F32)],
        compiler_params=pltpu.CompilerParams(dimension_semantics=("arbitrary",),
                                             vmem_limit_bytes=VMEM_LIMIT),
        name="out_proj",
    )(og, ol, x, wo, g, wr, br)


def _row_copy(src_hbm, row, dst, slot, sem):
    return pltpu.make_async_copy(src_hbm.at[pl.ds(row, 1), :], dst.at[pl.ds(slot, 1), :], sem)


def _gather_rows(src_hbm, idx_ref, n_rows, dst, sem):
    def issue(r, carry):
        _row_copy(src_hbm, idx_ref[0, r], dst, r, sem).start()
        return carry

    lax.fori_loop(0, n_rows, issue, 0, unroll=DMA_ISSUE_UNROLL)


def _wait_rows(src_hbm, n_rows, dst, sem):
    pltpu.make_async_copy(src_hbm.at[pl.ds(0, n_rows), :], dst, sem).wait()


def _expert_body(be_ref, nu_ref, idx_ref, idx_next_ref, h_hbm, wup_ref, bup_ref, wdn_ref, bdn_ref, y_ref,
                 xbuf, sems):
    i = pl.program_id(0)
    n_used = nu_ref[0]
    slot = i % 2

    @pl.when((i == 0) & (n_used > 0))
    def _():
        _gather_rows(h_hbm, idx_ref, EXPERT_ROWS, xbuf.at[0], sems.at[0])

    @pl.when(i < n_used)
    def _():
        _wait_rows(h_hbm, EXPERT_ROWS, xbuf.at[slot], sems.at[slot])

        @pl.when(i + 1 < n_used)
        def _():
            _gather_rows(h_hbm, idx_next_ref, EXPERT_ROWS, xbuf.at[1 - slot], sems.at[1 - slot])

        gu = _dot(xbuf[slot], wup_ref[...]) + bup_ref[...]
        gate = jnp.minimum(gu[:, :D_FF], SWIGLU_LIMIT)
        up = jnp.clip(gu[:, D_FF:], -SWIGLU_LIMIT, SWIGLU_LIMIT)
        a = (up + 1.0) * gate * jax.nn.sigmoid(SWIGLU_ALPHA * gate)
        y_ref[...] = _dot(a, wdn_ref[...]) + bdn_ref[...]

    @pl.when(i >= n_used)
    def _():
        y_ref[...] = jnp.zeros_like(y_ref)


def _experts(block_e, n_used, src_tok, h2, w_up, b_up, w_down, b_down):
    n_blocks = block_e.shape[0]
    src_blocks = src_tok.reshape(n_blocks, 1, EXPERT_ROWS)
    grid_spec = pltpu.PrefetchScalarGridSpec(
        num_scalar_prefetch=2,
        grid=(n_blocks,),
        in_specs=[
            pl.BlockSpec((None, 1, EXPERT_ROWS), lambda i, be, nu: (i, 0, 0), memory_space=pltpu.SMEM),
            pl.BlockSpec((None, 1, EXPERT_ROWS), lambda i, be, nu: (jnp.minimum(i + 1, n_blocks - 1), 0, 0),
                         memory_space=pltpu.SMEM),
            pl.BlockSpec(memory_space=pl.ANY),
            pl.BlockSpec((None, D_MODEL, 2 * D_FF), lambda i, be, nu: (be[i], 0, 0)),
            pl.BlockSpec((None, 1, 2 * D_FF), lambda i, be, nu: (be[i], 0, 0)),
            pl.BlockSpec((None, D_FF, D_MODEL), lambda i, be, nu: (be[i], 0, 0)),
            pl.BlockSpec((None, 1, D_MODEL), lambda i, be, nu: (be[i], 0, 0)),
        ],
        out_specs=pl.BlockSpec((EXPERT_ROWS, D_MODEL), lambda i, be, nu: (i, 0)),
        scratch_shapes=[pltpu.VMEM((2, EXPERT_ROWS, D_MODEL), F32), pltpu.SemaphoreType.DMA((2,))],
    )
    return pl.pallas_call(
        _expert_body,
        grid_spec=grid_spec,
        out_shape=jax.ShapeDtypeStruct((n_blocks * EXPERT_ROWS, D_MODEL), F32),
        compiler_params=pltpu.CompilerParams(dimension_semantics=("arbitrary",),
                                             vmem_limit_bytes=VMEM_LIMIT),
        name="experts",
    )(block_e, n_used, src_blocks, src_blocks, h2, w_up,
      b_up.reshape(N_EXPERTS, 1, 2 * D_FF), w_down, b_down.reshape(N_EXPERTS, 1, D_MODEL))


def _combine_body(dest_ref, dest_next_ref, rt_ref, y_hbm, x1_ref, g_ref, o_ref, ybuf, sems):
    i = pl.program_id(0)
    slot = i % 2
    n_rows = TOP_K * COMBINE_ROWS

    @pl.when(i == 0)
    def _():
        _gather_rows(y_hbm, dest_ref, n_rows, ybuf.at[0], sems.at[0])

    _wait_rows(y_hbm, n_rows, ybuf.at[slot], sems.at[slot])

    @pl.when(i + 1 < pl.num_programs(0))
    def _():
        _gather_rows(y_hbm, dest_next_ref, n_rows, ybuf.at[1 - slot], sems.at[1 - slot])

    moe = ybuf[slot, 0:COMBINE_ROWS, :] * rt_ref[:, 2 * TOP_K:2 * TOP_K + 1]
    for k in range(1, TOP_K):
        moe = moe + ybuf[slot, k * COMBINE_ROWS:(k + 1) * COMBINE_ROWS, :] * rt_ref[:, 2 * TOP_K + k:2 * TOP_K + k + 1]
    o_ref[...] = _rms(x1_ref[...] + moe, g_ref[...])


def _combine(dest, rt, y, x1, g):
    n = x1.shape[0]
    n_blk = n // COMBINE_ROWS
    dest_blocks = dest.reshape(n_blk, COMBINE_ROWS, TOP_K).transpose(0, 2, 1).reshape(n_blk, 1, TOP_K * COMBINE_ROWS)
    return pl.pallas_call(
        _combine_body,
        grid=(n_blk,),
        in_specs=[
            pl.BlockSpec((None, 1, COMBINE_ROWS * TOP_K), lambda i: (i, 0, 0), memory_space=pltpu.SMEM),
            pl.BlockSpec((None, 1, COMBINE_ROWS * TOP_K), lambda i: (jnp.minimum(i + 1, n_blk - 1), 0, 0),
                         memory_space=pltpu.SMEM),
            pl.BlockSpec((COMBINE_ROWS, LANE), lambda i: (i, 0)),
            pl.BlockSpec(memory_space=pl.ANY),
            pl.BlockSpec((COMBINE_ROWS, D_MODEL), lambda i: (i, 0)),
            pl.BlockSpec((1, D_MODEL), lambda i: (0, 0)),
        ],
        out_specs=pl.BlockSpec((COMBINE_ROWS, D_MODEL), lambda i: (i, 0)),
        out_shape=jax.ShapeDtypeStruct((n, D_MODEL), F32),
        scratch_shapes=[pltpu.VMEM((2, TOP_K * COMBINE_ROWS, D_MODEL), F32), pltpu.SemaphoreType.DMA((2,))],
        compiler_params=pltpu.CompilerParams(dimension_semantics=("arbitrary",),
                                             vmem_limit_bytes=VMEM_LIMIT),
        name="combine",
    )(dest_blocks, dest_blocks, rt, y, x1, g)


def _route(rt, cnt):
    n = rt.shape[0]
    n_assign = n * TOP_K
    flat_e = rt[:, 0:TOP_K].astype(jnp.int32).reshape(-1)
    rank = rt[:, TOP_K:2 * TOP_K].astype(jnp.int32).reshape(-1)
    counts = cnt[0, :N_EXPERTS].astype(jnp.int32)
    padded = (counts + EXPERT_ROWS - 1) // EXPERT_ROWS * EXPERT_ROWS
    pend = jnp.cumsum(padded)
    pstart = pend - padded
    dest = (pstart[flat_e] + rank).astype(jnp.int32)
    n_rows = n_assign + N_EXPERTS * EXPERT_ROWS
    n_blocks = n_rows // EXPERT_ROWS
    src_tok = jnp.zeros((n_rows,), jnp.int32).at[dest].set(jnp.arange(n_assign, dtype=jnp.int32) // TOP_K)
    block_first = jnp.arange(n_blocks, dtype=jnp.int32) * EXPERT_ROWS
    block_e = jnp.minimum(jnp.sum((pend[None, :] <= block_first[:, None]).astype(jnp.int32), axis=1),
                          N_EXPERTS - 1).astype(jnp.int32)
    n_used = (pend[-1] // EXPERT_ROWS).astype(jnp.int32).reshape(1)
    return dest, src_tok, block_e, n_used


def _pad_lanes(v, width):
    return jnp.zeros((1, width), F32).at[0, :v.shape[0]].set(v.astype(F32))


def kernel(x_prompt, x_sample, state_gdn_conv, state_gdn, state_gla, rms_mix_w, w_in, conv_w, gdn_a_log,
           gdn_dt_bias, gdn_norm_w, gla_gk_w, gla_gk_b, gla_norm_w, w_out, rms_ffn_w, w_router, b_router,
           w_up, b_up, w_down, b_down, rms_final_w):
    bp, tp, d = x_prompt.shape
    bs, ts, _ = x_sample.shape
    n_p, n_s = bp * tp, bs * ts
    assert d == D_MODEL and state_gdn.shape[0] == 1, "single-layer kernel"
    assert tp >= CONV_WIDTH - 1 and ts >= CONV_WIDTH - 1, "new conv state is taken from the new tokens only"
    l = 0

    wi = w_in[l]
    a0 = GDN_CONV_CH + GDN_V_W
    g0 = a0 + 2 * GDN_HEADS
    lr0 = g0 + 2 * GLA_QK_W + 2 * GLA_V_W
    small = jnp.concatenate([wi[:, a0:a0 + 2 * GDN_HEADS], wi[:, lr0:lr0 + GLA_GATE_RANK],
                             jnp.zeros((d, SM_W - 2 * GDN_HEADS - GLA_GATE_RANK), F32)], axis=1)
    w_big = jnp.concatenate([wi[:, :a0], wi[:, g0:lr0], small], axis=1).astype(BF16)
    alog = _pad_lanes(gdn_a_log[l], SM_W)
    dtb = _pad_lanes(gdn_dt_bias[l], SM_W)
    wgk = jnp.zeros((SM_W, GLA_QK_W), F32).at[SM_LR:SM_LR + GLA_GATE_RANK].set(gla_gk_w[l])
    wr = jnp.zeros((d, LANE), F32).at[:, :N_EXPERTS].set(w_router[l])
    br = jnp.full((1, LANE), -1e30, F32).at[0, :N_EXPERTS].set(b_router[l])

    x_all = jnp.concatenate([x_prompt.reshape(n_p, d), x_sample.reshape(n_s, d)], axis=0)
    proj = _inproj(x_all, rms_mix_w[l][None, :], w_big)

    tb_p = PROMPT_TIME_BLOCK
    zeros_conv = jnp.zeros((bp, CONV_WIDTH - 1, GDN_CONV_CH), F32)
    og_p, gdn_p, conv_p = _gdn(proj, bp, 1, tp, tb_p, CHUNK, CHUNK, zeros_conv,
                               jnp.zeros((bp, GDN_HEADS, GDN_DK, GDN_DV), F32), conv_w[l], alog, dtb,
                               gdn_norm_w[l][None, :])
    ol_p, gla_p = _gla(proj, bp, 1, tp, tb_p, CHUNK, CHUNK, jnp.zeros((bp, GLA_HEADS, GLA_DK, GLA_DV), F32),
                       wgk, gla_gk_b[l][None, :], gla_norm_w[l][None, :])

    ts_pad = SUBLANE
    nb_s = SAMPLE_SEQS_PER_STEP
    proj_s = proj[n_p:].reshape(bs, ts, PROJ_W)
    proj_sp = jnp.pad(proj_s, ((0, 0), (0, ts_pad - ts), (0, 0))).reshape(bs * ts_pad, PROJ_W)
    og_s, gdn_s, conv_s = _gdn(proj_sp, bs, nb_s, ts_pad, ts_pad, ts_pad, ts, state_gdn_conv[l], state_gdn[l],
                               conv_w[l], alog, dtb, gdn_norm_w[l][None, :])
    ol_s, gla_s = _gla(proj_sp, bs, nb_s, ts_pad, ts_pad, ts_pad, ts, state_gla[l], wgk, gla_gk_b[l][None, :],
                       gla_norm_w[l][None, :])
    og_s = og_s.reshape(bs, ts_pad, GDN_V_W)[:, :ts].reshape(n_s, GDN_V_W)
    ol_s = ol_s.reshape(bs, ts_pad, GLA_V_W)[:, :ts].reshape(n_s, GLA_V_W)

    og = jnp.concatenate([og_p, og_s], axis=0)
    ol = jnp.concatenate([ol_p, ol_s], axis=0)
    x1, h2, rt, cnt = _outproj(og, ol, x_all, w_out[l].astype(BF16), rms_ffn_w[l][None, :], wr, br)

    dest, src_tok, block_e, n_used = _route(rt, cnt)
    y_rows = _experts(block_e, n_used, src_tok, h2, w_up[l], b_up[l], w_down[l], b_down[l])
    y_all = _combine(dest, rt, y_rows, x1, rms_final_w[None, :])

    y_prompt = y_all[:n_p].reshape(bp, tp, d)
    y_sample = y_all[n_p:].reshape(bs, ts, d)
    return (y_prompt, y_sample, conv_p[None], gdn_p[None], gla_p[None], conv_s[None], gdn_s[None], gla_s[None])
```

```python
import functools

import jax
import jax.numpy as jnp
from jax import lax
from jax.experimental import pallas as pl
from jax.experimental.pallas import tpu as pltpu

F32 = jnp.float32
BF16 = jnp.bfloat16
HI = lax.Precision.HIGHEST

D_MODEL = 1024
GDN_HEADS = 4
GDN_DK = 128
GDN_DV = 128
GLA_HEADS = 4
GLA_DK = 64
GLA_DV = 128
GLA_GATE_RANK = 16
GLA_GATE_NORMALIZER = 16.0
CONV_WIDTH = 4
CHUNK = 64
N_EXPERTS = 32
TOP_K = 4
D_FF = 1024
SWIGLU_LIMIT = 7.0
SWIGLU_ALPHA = 1.702
RMS_EPS = 1e-6
L2_EPS = 1e-6

GDN_QK_W = GDN_HEADS * GDN_DK
GDN_V_W = GDN_HEADS * GDN_DV
GDN_CONV_CH = 2 * GDN_QK_W + GDN_V_W
GLA_QK_W = GLA_HEADS * GLA_DK
GLA_V_W = GLA_HEADS * GLA_DV

COL_QKV = 0
COL_Z = 1536
COL_GQ = 2048
COL_GK = 2304
COL_GV = 2560
COL_GG = 3072
COL_SM = 3584
SM_W = 128
PROJ_W = COL_SM + SM_W
SM_A, SM_B, SM_LR = 0, 4, 8

LANE = 128
SUBLANE = 8
TOK_TILES = D_MODEL // LANE
ROW_TILE = 256
EXPERT_ROWS = 256
COMBINE_ROWS = 128
DMA_ISSUE_UNROLL = 8
PROMPT_TIME_BLOCK = 512
SAMPLE_SEQS_PER_STEP = 8
VMEM_LIMIT = 56 * 1024 * 1024


def _dot(a, b):
    return jnp.dot(a.astype(BF16), b.astype(BF16), preferred_element_type=F32)


def _dot_nt(a, b):
    return lax.dot_general(a.astype(BF16), b.astype(BF16), (((1,), (1,)), ((), ())),
                           preferred_element_type=F32)


def _dot_tn(a, b):
    return lax.dot_general(a.astype(BF16), b.astype(BF16), (((0,), (0,)), ((), ())),
                           preferred_element_type=F32)


def _dot_hi(a, b):
    return jnp.dot(a, b, precision=HI, preferred_element_type=F32)


def _rms(x, w):
    return x * lax.rsqrt(jnp.mean(x * x, axis=-1, keepdims=True) + RMS_EPS) * w


def _silu(x):
    return x * jax.nn.sigmoid(x)


def _inproj_body(x_ref, g_ref, w_ref, o_ref):
    h = _rms(x_ref[...], g_ref[...])
    o_ref[...] = jnp.dot(h.astype(BF16), w_ref[...], preferred_element_type=F32)


def _inproj(x, g, w):
    n = x.shape[0]
    return pl.pallas_call(
        _inproj_body,
        grid=(n // ROW_TILE,),
        in_specs=[
            pl.BlockSpec((ROW_TILE, D_MODEL), lambda i: (i, 0)),
            pl.BlockSpec((1, D_MODEL), lambda i: (0, 0)),
            pl.BlockSpec((D_MODEL, PROJ_W), lambda i: (0, 0)),
        ],
        out_specs=pl.BlockSpec((ROW_TILE, PROJ_W), lambda i: (i, 0)),
        out_shape=jax.ShapeDtypeStruct((n, PROJ_W), F32),
        compiler_params=pltpu.CompilerParams(dimension_semantics=("arbitrary",),
                                             vmem_limit_bytes=VMEM_LIMIT),
        name="in_proj",
    )(x, g, w)


def _log2(n):
    assert n & (n - 1) == 0
    return n.bit_length() - 1


def _tri_inv_all(ms, c, ii, jj):
    eye = (ii == jj).astype(F32)
    base = min(c, 8)
    sh = _log2(base)
    blk = (ii >> sh) == (jj >> sh)
    ns = [jnp.where(blk, m, 0.0) for m in ms]
    xs = [eye - n for n in ns]
    ps = [_dot(n, n) for n in ns]
    ts = [_dot(jnp.concatenate([x, p], axis=0), p) for x, p in zip(xs, ps)]
    xs = [x + t[:c] for x, t in zip(xs, ts)]
    ps = [t[c:] for t in ts]
    xs = [x + _dot(x, p) for x, p in zip(xs, ps)]
    s = base
    while s < c:
        sh_s, sh_b = _log2(s), _log2(2 * s)
        off = ((ii >> sh_b) == (jj >> sh_b)) & ((ii >> sh_s) != (jj >> sh_s))
        ys = [_dot(x, jnp.where(off, m, 0.0)) for x, m in zip(xs, ms)]
        xs = [x - _dot(y, x) for x, y in zip(xs, ys)]
        s *= 2
    return xs


def _gated_norm(o, w, z):
    return o * lax.rsqrt(jnp.mean(o * o, axis=-1, keepdims=True) + RMS_EPS) * w * _silu(z)


def _chunk_rows(s, tb_rows, ci, c):
    r = s * tb_rows + ci * c
    if not isinstance(r, int):
        r = pl.multiple_of(r, c)
    return r


def _for_chunks(n_chunks, step):
    if n_chunks == 1:
        step(0, 0)
    else:
        lax.fori_loop(0, n_chunks, step, 0)


def _gdn_body(qkv_ref, z_ref, sm_ref, cbuf_ref, s0_ref, cw_ref, alog_ref, dtb_ref, nw_ref,
              o_ref, sout_ref, cout_ref, st, xc, act, gcs, us, wss, qgs, kds, aqs,
              *, nb, tb_rows, chunk, valid, n_tb):
    tb = pl.program_id(1)
    c = chunk
    n_heads = GDN_HEADS
    tail = CONV_WIDTH - 1
    pad = SUBLANE
    units = [(s, h) for s in range(nb) for h in range(n_heads)]

    @pl.when(tb == 0)
    def _():
        st[...] = s0_ref[...]
        for s in range(nb):
            xc[s, pad - tail:pad, :] = cbuf_ref[s]

    if n_tb > 1:
        @pl.when(tb > 0)
        def _():
            for s in range(nb):
                xc[s, pad - tail:pad, :] = xc[s, tb_rows + pad - tail:tb_rows + pad, :]

    for s in range(nb):
        xc[s, pad:pad + tb_rows, :] = qkv_ref[s * tb_rows:(s + 1) * tb_rows, :]

    slab = min(tb_rows, 64)
    for s in range(nb):
        for sl in range(tb_rows // slab):
            for cb in range(GDN_CONV_CH // 512):
                cs = slice(cb * 512, (cb + 1) * 512)
                lo = pad - tail + sl * slab
                acc = xc[s, lo:lo + slab, cs] * cw_ref[0:1, cs]
                for i in range(1, CONV_WIDTH):
                    acc = acc + xc[s, lo + i:lo + i + slab, cs] * cw_ref[i:i + 1, cs]
                act[s * tb_rows + sl * slab:s * tb_rows + (sl + 1) * slab, cs] = _silu(acc)

    ii = lax.broadcasted_iota(jnp.int32, (c, c), 0)
    jj = lax.broadcasted_iota(jnp.int32, (c, c), 1)
    lower = (ii >= jj)
    lower_f = lower.astype(F32)
    strict = (ii > jj)
    rowmask = None
    if valid < c:
        rowmask = lax.broadcasted_iota(jnp.int32, (c, 1), 0) < valid

    def hs(h, w):
        return slice(h * w, (h + 1) * w)

    def phase1(ci, carry):
        rows, b_ts, gc_ts, gc_tts = [], [], [], []
        for s in range(nb):
            rr = pl.ds(_chunk_rows(s, tb_rows, ci, c), c)
            sm = sm_ref[rr, :]
            g_t = -jnp.exp(alog_ref[...]) * jax.nn.softplus(sm + dtb_ref[...])
            b_t = jax.nn.sigmoid(sm)
            if rowmask is not None:
                g_t = jnp.where(rowmask, g_t, 0.0)
                b_t = jnp.where(rowmask, b_t, 0.0)
            gc_t = _dot_hi(lower_f, g_t)
            gcs[rr, :] = gc_t
            rows.append(rr)
            b_ts.append(b_t)
            gc_ts.append(gc_t)
            gc_tts.append(gc_t.T)
        qn, kn, kb, vb = {}, {}, {}, {}
        for (s, h) in units:
            q = act[rows[s], hs(h, GDN_DK)]
            k = act[rows[s], slice(GDN_QK_W + h * GDN_DK, GDN_QK_W + (h + 1) * GDN_DK)]
            v = act[rows[s], slice(2 * GDN_QK_W + h * GDN_DV, 2 * GDN_QK_W + (h + 1) * GDN_DV)]
            if rowmask is not None:
                q = jnp.where(rowmask, q, 0.0)
                k = jnp.where(rowmask, k, 0.0)
                v = jnp.where(rowmask, v, 0.0)
            qn[s, h] = q * lax.rsqrt(jnp.sum(q * q, axis=-1, keepdims=True) + L2_EPS) * (GDN_DK ** -0.5)
            kn[s, h] = k * lax.rsqrt(jnp.sum(k * k, axis=-1, keepdims=True) + L2_EPS)
            beta = b_ts[s][:, SM_B + h:SM_B + h + 1]
            kb[s, h] = kn[s, h] * beta
            vb[s, h] = v * beta
        s1 = {u: _dot_nt(jnp.concatenate([kb[u], qn[u]], axis=0), kn[u]) for u in units}
        mm = []
        for (s, h) in units:
            gcol = gc_ts[s][:, SM_A + h:SM_A + h + 1]
            grow = gc_tts[s][SM_A + h:SM_A + h + 1, :]
            dec = jnp.exp(jnp.where(lower, gcol - grow, -jnp.inf))
            mm.append(jnp.where(strict, s1[s, h][:c] * dec, 0.0))
            aqs[h, rows[s], :] = s1[s, h][c:] * dec
        tms = _tri_inv_all(mm, c, ii, jj)
        for (s, h), tm in zip(units, tms):
            gcol = gc_ts[s][:, SM_A + h:SM_A + h + 1]
            eg = jnp.exp(gcol)
            uw = _dot(tm, jnp.concatenate([vb[s, h], kb[s, h] * eg], axis=1))
            us[rows[s], hs(h, GDN_DV)] = uw[:, :GDN_DV]
            wss[rows[s], hs(h, GDN_DV)] = uw[:, GDN_DV:]
            qgs[rows[s], hs(h, GDN_DK)] = qn[s, h] * eg
            kds[rows[s], hs(h, GDN_DK)] = kn[s, h] * jnp.exp(gcol[c - 1:c, :] - gcol)
        return carry

    def phase2(ci, carry):
        r0 = [_chunk_rows(s, tb_rows, ci, c) for s in range(nb)]
        rows = [pl.ds(r, c) for r in r0]
        ws = {(s, h): _dot(jnp.concatenate([wss[rows[s], hs(h, GDN_DV)], qgs[rows[s], hs(h, GDN_DK)]], axis=0),
                           st[s, h]) for (s, h) in units}
        v_new = {(s, h): us[rows[s], hs(h, GDN_DV)] - ws[s, h][:c] for (s, h) in units}
        o = {(s, h): ws[s, h][c:] + _dot(aqs[h, rows[s], :], v_new[s, h]) for (s, h) in units}
        upd = {(s, h): _dot_tn(kds[rows[s], hs(h, GDN_DK)], v_new[s, h]) for (s, h) in units}
        for (s, h) in units:
            g_last = gcs[pl.ds(r0[s] + c - 1, 1), SM_A + h:SM_A + h + 1]
            st[s, h] = st[s, h] * jnp.exp(g_last) + upd[s, h]
        for s in range(nb):
            o_ref[rows[s], :] = jnp.concatenate(
                [_gated_norm(o[s, h], nw_ref[...], z_ref[rows[s], hs(h, GDN_DV)]) for h in range(n_heads)], axis=1)
        return carry

    n_chunks = tb_rows // c
    _for_chunks(n_chunks, phase1)
    _for_chunks(n_chunks, phase2)

    @pl.when(tb == n_tb - 1)
    def _():
        sout_ref[...] = st[...]
        last = tb_rows if valid == c else valid
        for s in range(nb):
            cout_ref[s] = xc[s, pad + last - tail:pad + last, :]


def _gdn(proj, n_seq, nb, t_len, tb_rows, chunk, valid, conv_buf, s0, conv_w, alog, dtb, nw):
    n_tb = t_len // tb_rows
    assert nb == 1 or n_tb == 1
    rows = nb * tb_rows

    def rowblk(b, t):
        return b * n_tb + t

    body = functools.partial(_gdn_body, nb=nb, tb_rows=tb_rows, chunk=chunk, valid=valid, n_tb=n_tb)
    return pl.pallas_call(
        body,
        grid=(n_seq // nb, n_tb),
        in_specs=[
            pl.BlockSpec((rows, GDN_CONV_CH), lambda b, t: (rowblk(b, t), COL_QKV // GDN_CONV_CH)),
            pl.BlockSpec((rows, GDN_V_W), lambda b, t: (rowblk(b, t), COL_Z // GDN_V_W)),
            pl.BlockSpec((rows, SM_W), lambda b, t: (rowblk(b, t), COL_SM // SM_W)),
            pl.BlockSpec((nb, CONV_WIDTH - 1, GDN_CONV_CH), lambda b, t: (b, 0, 0)),
            pl.BlockSpec((nb, GDN_HEADS, GDN_DK, GDN_DV), lambda b, t: (b, 0, 0, 0)),
            pl.BlockSpec((CONV_WIDTH, GDN_CONV_CH), lambda b, t: (0, 0)),
            pl.BlockSpec((1, SM_W), lambda b, t: (0, 0)),
            pl.BlockSpec((1, SM_W), lambda b, t: (0, 0)),
            pl.BlockSpec((1, GDN_DV), lambda b, t: (0, 0)),
        ],
        out_specs=[
            pl.BlockSpec((rows, GDN_V_W), lambda b, t: (rowblk(b, t), 0)),
            pl.BlockSpec((nb, GDN_HEADS, GDN_DK, GDN_DV), lambda b, t: (b, 0, 0, 0)),
            pl.BlockSpec((nb, CONV_WIDTH - 1, GDN_CONV_CH), lambda b, t: (b, 0, 0)),
        ],
        out_shape=[
            jax.ShapeDtypeStruct((n_seq * t_len, GDN_V_W), F32),
            jax.ShapeDtypeStruct((n_seq, GDN_HEADS, GDN_DK, GDN_DV), F32),
            jax.ShapeDtypeStruct((n_seq, CONV_WIDTH - 1, GDN_CONV_CH), F32),
        ],
        scratch_shapes=[
            pltpu.VMEM((nb, GDN_HEADS, GDN_DK, GDN_DV), F32),
            pltpu.VMEM((nb, tb_rows + SUBLANE, GDN_CONV_CH), F32),
            pltpu.VMEM((rows, GDN_CONV_CH), F32),
            pltpu.VMEM((rows, SM_W), F32),
            pltpu.VMEM((rows, GDN_V_W), F32),
            pltpu.VMEM((rows, GDN_V_W), F32),
            pltpu.VMEM((rows, GDN_QK_W), F32),
            pltpu.VMEM((rows, GDN_QK_W), F32),
            pltpu.VMEM((GDN_HEADS, rows, chunk), F32),
        ],
        compiler_params=pltpu.CompilerParams(dimension_semantics=("arbitrary", "arbitrary"),
                                             vmem_limit_bytes=VMEM_LIMIT),
        name="gdn_mixer",
    )(proj, proj, proj, conv_buf, s0, conv_w, alog, dtb, nw)


def _gla_body(q_ref, k_ref, v_ref, go_ref, sm_ref, s0_ref, wgk_ref, bgk_ref, nw_ref,
              o_ref, sout_ref, st, *, nb, tb_rows, chunk, valid, n_tb):
    tb = pl.program_id(1)
    c = chunk
    n_heads = GLA_HEADS
    units = [(s, h) for s in range(nb) for h in range(n_heads)]

    @pl.when(tb == 0)
    def _():
        st[...] = s0_ref[...]

    ii = lax.broadcasted_iota(jnp.int32, (c, c), 0)
    jj = lax.broadcasted_iota(jnp.int32, (c, c), 1)
    lower = (ii >= jj)
    lower_f = lower.astype(F32)
    rid = lax.broadcasted_iota(jnp.int32, (c, 1), 0)
    rowmask = (rid < valid) if valid < c else None
    n_sub = max(c // 16, 1)
    sub = c // n_sub

    def chunk_step(ci, carry):
        rows, bcs, bc_ts = [], [], []
        for s in range(nb):
            rr = pl.ds(_chunk_rows(s, tb_rows, ci, c), c)
            gk = jax.nn.log_sigmoid(_dot(sm_ref[rr, :], wgk_ref[...]) + bgk_ref[...]) / GLA_GATE_NORMALIZER
            if rowmask is not None:
                gk = jnp.where(rowmask, gk, 0.0)
            bc = _dot_hi(lower_f, gk)
            rows.append(rr)
            bcs.append(bc)
            bc_ts.append(bc.T)
        q, k, v, bch = {}, {}, {}, {}
        for (s, h) in units:
            ks = slice(h * GLA_DK, (h + 1) * GLA_DK)
            vs = slice(h * GLA_DV, (h + 1) * GLA_DV)
            q[s, h] = q_ref[rows[s], ks] * (GLA_DK ** -0.5)
            kk = k_ref[rows[s], ks]
            vv = v_ref[rows[s], vs]
            if rowmask is not None:
                kk = jnp.where(rowmask, kk, 0.0)
                vv = jnp.where(rowmask, vv, 0.0)
            k[s, h], v[s, h] = kk, vv
            bch[s, h] = bcs[s][:, ks]
        o_inter = {u: _dot(q[u] * jnp.exp(bch[u]), st[u[0], u[1]]) for u in units}
        a = {}
        for u in units:
            q_parts, k_parts = [], []
            for sb in range(n_sub):
                ref_row = bch[u][sb * sub:sb * sub + 1, :]
                in_blk = (rid >= sb * sub) & (rid < (sb + 1) * sub)
                q_parts.append(jnp.where(in_blk, q[u] * jnp.exp(jnp.where(in_blk, bch[u] - ref_row, 0.0)), 0.0))
                k_parts.append(k[u] * jnp.exp(jnp.where(rid < (sb + 1) * sub, ref_row - bch[u], 0.0)))
            q_hat = jnp.concatenate(q_parts, axis=1) if n_sub > 1 else q_parts[0]
            k_hat = jnp.concatenate(k_parts, axis=1) if n_sub > 1 else k_parts[0]
            a[u] = jnp.where(lower, _dot_nt(q_hat, k_hat), 0.0)
        upd = {u: _dot_tn(k[u] * jnp.exp(bch[u][c - 1:c, :] - bch[u]), v[u]) for u in units}
        o = {u: o_inter[u] + _dot(a[u], v[u]) for u in units}
        for (s, h) in units:
            dec_col = bc_ts[s][h * GLA_DK:(h + 1) * GLA_DK, c - 1:c]
            st[s, h] = jnp.exp(dec_col) * st[s, h] + upd[s, h]
        for s in range(nb):
            o_ref[rows[s], :] = jnp.concatenate(
                [_gated_norm(o[s, h], nw_ref[...], go_ref[rows[s], h * GLA_DV:(h + 1) * GLA_DV])
                 for h in range(n_heads)], axis=1)
        return carry

    _for_chunks(tb_rows // c, chunk_step)

    @pl.when(tb == n_tb - 1)
    def _():
        sout_ref[...] = st[...]


def _gla(proj, n_seq, nb, t_len, tb_rows, chunk, valid, s0, wgk, bgk, nw):
    n_tb = t_len // tb_rows
    assert nb == 1 or n_tb == 1
    rows = nb * tb_rows

    def rowblk(b, t):
        return b * n_tb + t

    body = functools.partial(_gla_body, nb=nb, tb_rows=tb_rows, chunk=chunk, valid=valid, n_tb=n_tb)
    return pl.pallas_call(
        body,
        grid=(n_seq // nb, n_tb),
        in_specs=[
            pl.BlockSpec((rows, GLA_QK_W), lambda b, t: (rowblk(b, t), COL_GQ // GLA_QK_W)),
            pl.BlockSpec((rows, GLA_QK_W), lambda b, t: (rowblk(b, t), COL_GK // GLA_QK_W)),
            pl.BlockSpec((rows, GLA_V_W), lambda b, t: (rowblk(b, t), COL_GV // GLA_V_W)),
            pl.BlockSpec((rows, GLA_V_W), lambda b, t: (rowblk(b, t), COL_GG // GLA_V_W)),
            pl.BlockSpec((rows, SM_W), lambda b, t: (rowblk(b, t), COL_SM // SM_W)),
            pl.BlockSpec((nb, GLA_HEADS, GLA_DK, GLA_DV), lambda b, t: (b, 0, 0, 0)),
            pl.BlockSpec((SM_W, GLA_QK_W), lambda b, t: (0, 0)),
            pl.BlockSpec((1, GLA_QK_W), lambda b, t: (0, 0)),
            pl.BlockSpec((1, GLA_DV), lambda b, t: (0, 0)),
        ],
        out_specs=[
            pl.BlockSpec((rows, GLA_V_W), lambda b, t: (rowblk(b, t), 0)),
            pl.BlockSpec((nb, GLA_HEADS, GLA_DK, GLA_DV), lambda b, t: (b, 0, 0, 0)),
        ],
        out_shape=[
            jax.ShapeDtypeStruct((n_seq * t_len, GLA_V_W), F32),
            jax.ShapeDtypeStruct((n_seq, GLA_HEADS, GLA_DK, GLA_DV), F32),
        ],
        scratch_shapes=[pltpu.VMEM((nb, GLA_HEADS, GLA_DK, GLA_DV), F32)],
        compiler_params=pltpu.CompilerParams(dimension_semantics=("arbitrary", "arbitrary"),
                                             vmem_limit_bytes=VMEM_LIMIT),
        name="gla_mixer",
    )(proj, proj, proj, proj, proj, s0, wgk, bgk, nw)


def _outproj_body(og_ref, ol_ref, x_ref, wo_ref, g_ref, wr_ref, br_ref,
                  x1_ref, h2_ref, rt_ref, cnt_ref, base):
    i = pl.program_id(0)

    @pl.when(i == 0)
    def _():
        base[...] = jnp.zeros_like(base)

    o = jnp.concatenate([og_ref[...], ol_ref[...]], axis=1)
    x1 = x_ref[...] + jnp.dot(o.astype(BF16), wo_ref[...], preferred_element_type=F32)
    x1_ref[...] = x1
    h = _rms(x1, g_ref[...])
    _store_token_tiles(h2_ref, h)
    logits =_dot_hi(h, wr_ref[...]) + br_ref[...]

    tm = logits.shape[0]
    lane = lax.broadcasted_iota(jnp.int32, (tm, LANE), 1)
    work = logits
    sel = jnp.zeros((tm, LANE), F32)
    ids, vals = [], []
    for _ in range(TOP_K):
        m = jnp.max(work, axis=-1, keepdims=True)
        idx = jnp.min(jnp.where(work == m, lane, LANE), axis=-1, keepdims=True)
        hit = lane == idx
        ids.append(idx)
        vals.append(m)
        work = jnp.where(hit, -jnp.inf, work)
        sel = sel + hit.astype(F32)
    exps = [jnp.exp(v - vals[0]) for v in vals]
    den = exps[0]
    for e in exps[1:]:
        den = den + e
    gates = [e / den for e in exps]

    ri = lax.broadcasted_iota(jnp.int32, (tm, tm), 0)
    ci = lax.broadcasted_iota(jnp.int32, (tm, tm), 1)
    before = _dot((ri > ci).astype(F32), sel) + base[...]
    ranks = [jnp.sum(jnp.where(lane == idx, before, 0.0), axis=-1, keepdims=True) for idx in ids]
    base[...] = base[...] + jnp.sum(sel, axis=0, keepdims=True)
    cnt_ref[...] = base[...]

    rec = jnp.zeros((tm, LANE), F32)
    for k in range(TOP_K):
        rec = jnp.where(lane == k, ids[k].astype(F32), rec)
        rec = jnp.where(lane == TOP_K + k, ranks[k], rec)
        rec = jnp.where(lane == 2 * TOP_K + k, gates[k], rec)
    rt_ref[...] = rec


def _outproj(og, ol, x, wo, g, wr, br):
    n = x.shape[0]
    return pl.pallas_call(
        _outproj_body,
        grid=(n // ROW_TILE,),
        in_specs=[
            pl.BlockSpec((ROW_TILE, GDN_V_W), lambda i: (i, 0)),
            pl.BlockSpec((ROW_TILE, GLA_V_W), lambda i: (i, 0)),
            pl.BlockSpec((ROW_TILE, D_MODEL), lambda i: (i, 0)),
            pl.BlockSpec((D_MODEL, D_MODEL), lambda i: (0, 0)),
            pl.BlockSpec((1, D_MODEL), lambda i: (0, 0)),
            pl.BlockSpec((D_MODEL, LANE), lambda i: (0, 0)),
            pl.BlockSpec((1, LANE), lambda i: (0, 0)),
        ],
        out_specs=[
            pl.BlockSpec((ROW_TILE, D_MODEL), lambda i: (i, 0)),
            pl.BlockSpec((ROW_TILE * TOK_TILES, LANE), lambda i: (i, 0)),
            pl.BlockSpec((ROW_TILE, LANE), lambda i: (i, 0)),
            pl.BlockSpec((1, LANE), lambda i: (0, 0)),
        ],
        out_shape=[
            jax.ShapeDtypeStruct((n, D_MODEL), F32),
            jax.ShapeDtypeStruct((n * TOK_TILES, LANE), F32),
            jax.ShapeDtypeStruct((n, LANE), F32),
            jax.ShapeDtypeStruct((1, LANE), F32),
        ],
        scratch_shapes=[pltpu.VMEM((1, LANE), F32)],
        compiler_params=pltpu.CompilerParams(dimension_semantics=("arbitrary",),
                                             vmem_limit_bytes=VMEM_LIMIT),
        name="out_proj",
    )(og, ol, x, wo, g, wr, br)


def _store_token_tiles(ref2d, val):
    rows = val.shape[0]
    for c in range(TOK_TILES):
        ref2d[pl.ds(c, rows, stride=TOK_TILES), :] = val[:, c * LANE:(c + 1) * LANE]


def _load_token_tiles(ref2d, first_row, rows):
    return jnp.concatenate(
        [ref2d[pl.ds(first_row * TOK_TILES + c, rows, stride=TOK_TILES), :] for c in range(TOK_TILES)], axis=1)


def _gather_rows(src_tiles, idx_ref, n_rows, dst2d, sem):
    def issue(r, carry):
        dst = dst2d.at[pl.ds(pl.multiple_of(r * TOK_TILES, TOK_TILES), TOK_TILES), :]
        pltpu.make_async_copy(src_tiles.at[idx_ref[0, r]], dst, sem).start()
        return carry

    lax.fori_loop(0, n_rows, issue, 0, unroll=DMA_ISSUE_UNROLL)


def _wait_rows(src2d, n_rows, dst2d, sem):
    pltpu.make_async_copy(src2d.at[pl.ds(0, n_rows * TOK_TILES), :], dst2d, sem).wait()


def _expert_body(be_ref, nu_ref, idx_ref, idx_next_ref, h_tiles, h_2d, wup_ref, bup_ref, wdn_ref, bdn_ref, y_ref,
                 xbuf, sems):
    i = pl.program_id(0)
    n_used = nu_ref[0]
    slot = i % 2

    @pl.when((i == 0) & (n_used > 0))
    def _():
        _gather_rows(h_tiles, idx_ref, EXPERT_ROWS, xbuf.at[0], sems.at[0])

    @pl.when(i < n_used)
    def _():
        _wait_rows(h_2d, EXPERT_ROWS, xbuf.at[slot], sems.at[slot])

        @pl.when(i + 1 < n_used)
        def _():
            _gather_rows(h_tiles, idx_next_ref, EXPERT_ROWS, xbuf.at[1 - slot], sems.at[1 - slot])

        gu = _dot(_load_token_tiles(xbuf.at[slot], 0, EXPERT_ROWS), wup_ref[...]) + bup_ref[...]
        gate = jnp.minimum(gu[:, :D_FF], SWIGLU_LIMIT)
        up = jnp.clip(gu[:, D_FF:], -SWIGLU_LIMIT, SWIGLU_LIMIT)
        a = (up + 1.0) * gate * jax.nn.sigmoid(SWIGLU_ALPHA * gate)
        _store_token_tiles(y_ref, _dot(a, wdn_ref[...]) + bdn_ref[...])

    @pl.when(i >= n_used)
    def _():
        y_ref[...] = jnp.zeros_like(y_ref)


def _experts(block_e, n_used, src_tok, h2_2d, w_up, b_up, w_down, b_down):
    n_blocks = block_e.shape[0]
    src_blocks = src_tok.reshape(n_blocks, 1, EXPERT_ROWS)
    grid_spec = pltpu.PrefetchScalarGridSpec(
        num_scalar_prefetch=2,
        grid=(n_blocks,),
        in_specs=[
            pl.BlockSpec((None, 1, EXPERT_ROWS), lambda i, be, nu: (i, 0, 0), memory_space=pltpu.SMEM),
            pl.BlockSpec((None, 1, EXPERT_ROWS), lambda i, be, nu: (jnp.minimum(i + 1, n_blocks - 1), 0, 0),
                         memory_space=pltpu.SMEM),
            pl.BlockSpec(memory_space=pl.ANY),
            pl.BlockSpec(memory_space=pl.ANY),
            pl.BlockSpec((None, D_MODEL, 2 * D_FF), lambda i, be, nu: (be[i], 0, 0)),
            pl.BlockSpec((None, 1, 2 * D_FF), lambda i, be, nu: (be[i], 0, 0)),
            pl.BlockSpec((None, D_FF, D_MODEL), lambda i, be, nu: (be[i], 0, 0)),
            pl.BlockSpec((None, 1, D_MODEL), lambda i, be, nu: (be[i], 0, 0)),
        ],
        out_specs=pl.BlockSpec((EXPERT_ROWS * TOK_TILES, LANE), lambda i, be, nu: (i, 0)),
        scratch_shapes=[pltpu.VMEM((2, EXPERT_ROWS * TOK_TILES, LANE), F32), pltpu.SemaphoreType.DMA((2,))],
    )
    return pl.pallas_call(
        _expert_body,
        grid_spec=grid_spec,
        out_shape=jax.ShapeDtypeStruct((n_blocks * EXPERT_ROWS * TOK_TILES, LANE), F32),
        compiler_params=pltpu.CompilerParams(dimension_semantics=("arbitrary",),
                                             vmem_limit_bytes=VMEM_LIMIT),
        name="experts",
    )(block_e, n_used, src_blocks, src_blocks, h2_2d.reshape(-1, TOK_TILES, LANE), h2_2d, w_up,
      b_up.reshape(N_EXPERTS, 1, 2 * D_FF), w_down, b_down.reshape(N_EXPERTS, 1, D_MODEL))


def _combine_body(dest_ref, dest_next_ref, rt_ref, y_tiles, y_2d, x1_ref, g_ref, o_ref, ybuf, sems):
    i = pl.program_id(0)
    slot = i % 2
    n_rows = TOP_K * COMBINE_ROWS

    @pl.when(i == 0)
    def _():
        _gather_rows(y_tiles, dest_ref, n_rows, ybuf.at[0], sems.at[0])

    _wait_rows(y_2d, n_rows, ybuf.at[slot], sems.at[slot])

    @pl.when(i + 1 < pl.num_programs(0))
    def _():
        _gather_rows(y_tiles, dest_next_ref, n_rows, ybuf.at[1 - slot], sems.at[1 - slot])

    buf = ybuf.at[slot]
    moe = _load_token_tiles(buf, 0, COMBINE_ROWS) * rt_ref[:, 2 * TOP_K:2 * TOP_K + 1]
    for k in range(1, TOP_K):
        moe = moe + _load_token_tiles(buf, k * COMBINE_ROWS, COMBINE_ROWS) * rt_ref[:, 2 * TOP_K + k:2 * TOP_K + k + 1]
    o_ref[...] = _rms(x1_ref[...] + moe, g_ref[...])


def _combine(dest, rt, y_2d, x1, g):
    n = x1.shape[0]
    n_blk = n // COMBINE_ROWS
    dest_blocks = dest.reshape(n_blk, COMBINE_ROWS, TOP_K).transpose(0, 2, 1).reshape(n_blk, 1, TOP_K * COMBINE_ROWS)
    return pl.pallas_call(
        _combine_body,
        grid=(n_blk,),
        in_specs=[
            pl.BlockSpec((None, 1, COMBINE_ROWS * TOP_K), lambda i: (i, 0, 0), memory_space=pltpu.SMEM),
            pl.BlockSpec((None, 1, COMBINE_ROWS * TOP_K), lambda i: (jnp.minimum(i + 1, n_blk - 1), 0, 0),
                         memory_space=pltpu.SMEM),
            pl.BlockSpec((COMBINE_ROWS, LANE), lambda i: (i, 0)),
            pl.BlockSpec(memory_space=pl.ANY),
            pl.BlockSpec(memory_space=pl.ANY),
            pl.BlockSpec((COMBINE_ROWS, D_MODEL), lambda i: (i, 0)),
            pl.BlockSpec((1, D_MODEL), lambda i: (0, 0)),
        ],
        out_specs=pl.BlockSpec((COMBINE_ROWS, D_MODEL), lambda i: (i, 0)),
        out_shape=jax.ShapeDtypeStruct((n, D_MODEL), F32),
        scratch_shapes=[pltpu.VMEM((2, TOP_K * COMBINE_ROWS * TOK_TILES, LANE), F32),
                        pltpu.SemaphoreType.DMA((2,))],
        compiler_params=pltpu.CompilerParams(dimension_semantics=("arbitrary",),
                                             vmem_limit_bytes=VMEM_LIMIT),
        name="combine",
    )(dest_blocks, dest_blocks, rt, y_2d.reshape(-1, TOK_TILES, LANE), y_2d, x1, g)


def _route(rt, cnt):
    n = rt.shape[0]
    n_assign = n * TOP_K
    flat_e = rt[:, 0:TOP_K].astype(jnp.int32).reshape(-1)
    rank = rt[:, TOP_K:2 * TOP_K].astype(jnp.int32).reshape(-1)
    counts = cnt[0, :N_EXPERTS].astype(jnp.int32)
    padded = (counts + EXPERT_ROWS - 1) // EXPERT_ROWS * EXPERT_ROWS
    pend = jnp.cumsum(padded)
    pstart = pend - padded
    dest = (pstart[flat_e] + rank).astype(jnp.int32)
    n_rows = n_assign + N_EXPERTS * EXPERT_ROWS
    n_blocks = n_rows // EXPERT_ROWS
    src_tok = jnp.zeros((n_rows,), jnp.int32).at[dest].set(jnp.arange(n_assign, dtype=jnp.int32) // TOP_K)
    block_first = jnp.arange(n_blocks, dtype=jnp.int32) * EXPERT_ROWS
    block_e = jnp.minimum(jnp.sum((pend[None, :] <= block_first[:, None]).astype(jnp.int32), axis=1),
                          N_EXPERTS - 1).astype(jnp.int32)
    n_used = (pend[-1] // EXPERT_ROWS).astype(jnp.int32).reshape(1)
    return dest, src_tok, block_e, n_used


def _pad_lanes(v, width):
    return jnp.zeros((1, width), F32).at[0, :v.shape[0]].set(v.astype(F32))


def kernel(x_prompt, x_sample, state_gdn_conv, state_gdn, state_gla, rms_mix_w, w_in, conv_w, gdn_a_log,
           gdn_dt_bias, gdn_norm_w, gla_gk_w, gla_gk_b, gla_norm_w, w_out, rms_ffn_w, w_router, b_router,
           w_up, b_up, w_down, b_down, rms_final_w):
    bp, tp, d = x_prompt.shape
    bs, ts, _ = x_sample.shape
    n_p, n_s = bp * tp, bs * ts
    assert d == D_MODEL and state_gdn.shape[0] == 1, "single-layer kernel"
    assert tp >= CONV_WIDTH - 1 and ts >= CONV_WIDTH - 1, "new conv state is taken from the new tokens only"
    l = 0

    wi = w_in[l]
    a0 = GDN_CONV_CH + GDN_V_W
    g0 = a0 + 2 * GDN_HEADS
    lr0 = g0 + 2 * GLA_QK_W + 2 * GLA_V_W
    small = jnp.concatenate([wi[:, a0:a0 + 2 * GDN_HEADS], wi[:, lr0:lr0 + GLA_GATE_RANK],
                             jnp.zeros((d, SM_W - 2 * GDN_HEADS - GLA_GATE_RANK), F32)], axis=1)
    w_big = jnp.concatenate([wi[:, :a0], wi[:, g0:lr0], small], axis=1).astype(BF16)
    alog = _pad_lanes(gdn_a_log[l], SM_W)
    dtb = _pad_lanes(gdn_dt_bias[l], SM_W)
    wgk = jnp.zeros((SM_W, GLA_QK_W), F32).at[SM_LR:SM_LR + GLA_GATE_RANK].set(gla_gk_w[l])
    wr = jnp.zeros((d, LANE), F32).at[:, :N_EXPERTS].set(w_router[l])
    br = jnp.full((1, LANE), -1e30, F32).at[0, :N_EXPERTS].set(b_router[l])

    x_all = jnp.concatenate([x_prompt.reshape(n_p, d), x_sample.reshape(n_s, d)], axis=0)
    proj = _inproj(x_all, rms_mix_w[l][None, :], w_big)

    tb_p = PROMPT_TIME_BLOCK
    zeros_conv = jnp.zeros((bp, CONV_WIDTH - 1, GDN_CONV_CH), F32)
    og_p, gdn_p, conv_p = _gdn(proj, bp, 1, tp, tb_p, CHUNK, CHUNK, zeros_conv,
                               jnp.zeros((bp, GDN_HEADS, GDN_DK, GDN_DV), F32), conv_w[l], alog, dtb,
                               gdn_norm_w[l][None, :])
    ol_p, gla_p = _gla(proj, bp, 1, tp, tb_p, CHUNK, CHUNK, jnp.zeros((bp, GLA_HEADS, GLA_DK, GLA_DV), F32),
                       wgk, gla_gk_b[l][None, :], gla_norm_w[l][None, :])

    ts_pad = SUBLANE
    nb_s = SAMPLE_SEQS_PER_STEP
    proj_s = proj[n_p:].reshape(bs, ts, PROJ_W)
    proj_sp = jnp.pad(proj_s, ((0, 0), (0, ts_pad - ts), (0, 0))).reshape(bs * ts_pad, PROJ_W)
    og_s, gdn_s, conv_s = _gdn(proj_sp, bs, nb_s, ts_pad, ts_pad, ts_pad, ts, state_gdn_conv[l], state_gdn[l],
                               conv_w[l], alog, dtb, gdn_norm_w[l][None, :])
    ol_s, gla_s = _gla(proj_sp, bs, nb_s, ts_pad, ts_pad, ts_pad, ts, state_gla[l], wgk, gla_gk_b[l][None, :],
                       gla_norm_w[l][None, :])
    og_s = og_s.reshape(bs, ts_pad, GDN_V_W)[:, :ts].reshape(n_s, GDN_V_W)
    ol_s = ol_s.reshape(bs, ts_pad, GLA_V_W)[:, :ts].reshape(n_s, GLA_V_W)

    og = jnp.concatenate([og_p, og_s], axis=0)
    ol = jnp.concatenate([ol_p, ol_s], axis=0)
    x1, h2, rt, cnt = _outproj(og, ol, x_all, w_out[l].astype(BF16), rms_ffn_w[l][None, :], wr, br)

    dest, src_tok, block_e, n_used = _route(rt, cnt)
    y_rows = _experts(block_e, n_used, src_tok, h2, w_up[l], b_up[l], w_down[l], b_down[l])
    y_all = _combine(dest, rt, y_rows, x1, rms_final_w[None, :])

    y_prompt = y_all[:n_p].reshape(bp, tp, d)
    y_sample = y_all[n_p:].reshape(bs, ts, d)
    return (y_prompt, y_sample, conv_p[None], gdn_p[None], gla_p[None], conv_s[None], gdn_s[None], gla_s[None])
```

```python
import functools

import jax
import jax.numpy as jnp
from jax import lax
from jax.experimental import pallas as pl
from jax.experimental.pallas import tpu as pltpu

F32 = jnp.float32
BF16 = jnp.bfloat16
HI = lax.Precision.HIGHEST

D_MODEL = 1024
GDN_HEADS = 4
GDN_DK = 128
GDN_DV = 128
GLA_HEADS = 4
GLA_DK = 64
GLA_DV = 128
GLA_GATE_RANK = 16
GLA_GATE_NORMALIZER = 16.0
CONV_WIDTH = 4
CHUNK = 64
N_EXPERTS = 32
TOP_K = 4
D_FF = 1024
SWIGLU_LIMIT = 7.0
SWIGLU_ALPHA = 1.702
RMS_EPS = 1e-6
L2_EPS = 1e-6

GDN_QK_W = GDN_HEADS * GDN_DK
GDN_V_W = GDN_HEADS * GDN_DV
GDN_CONV_CH = 2 * GDN_QK_W + GDN_V_W
GLA_QK_W = GLA_HEADS * GLA_DK
GLA_V_W = GLA_HEADS * GLA_DV

COL_QKV = 0
COL_Z = 1536
COL_GQ = 2048
COL_GK = 2304
COL_GV = 2560
COL_GG = 3072
COL_SM = 3584
SM_W = 128
PROJ_W = COL_SM + SM_W
SM_A, SM_B, SM_LR = 0, 4, 8

LANE = 128
SUBLANE = 8
TOK_TILES = D_MODEL // LANE
ROW_TILE = 256
EXPERT_ROWS = 256
COMBINE_ROWS = 128
DMA_ISSUE_UNROLL = 8
EXPERT_GATHER_QUEUES = (1,)
COMBINE_GATHER_QUEUES = (0, 1)
PROMPT_TIME_BLOCK = 512
SAMPLE_SEQS_PER_STEP = 8
VMEM_LIMIT = 56 * 1024 * 1024


def _dot(a, b):
    return jnp.dot(a.astype(BF16), b.astype(BF16), preferred_element_type=F32)


def _dot_nt(a, b):
    return lax.dot_general(a.astype(BF16), b.astype(BF16), (((1,), (1,)), ((), ())),
                           preferred_element_type=F32)


def _dot_tn(a, b):
    return lax.dot_general(a.astype(BF16), b.astype(BF16), (((0,), (0,)), ((), ())),
                           preferred_element_type=F32)


def _dot_hi(a, b):
    return jnp.dot(a, b, precision=HI, preferred_element_type=F32)


def _rms(x, w):
    return x * lax.rsqrt(jnp.mean(x * x, axis=-1, keepdims=True) + RMS_EPS) * w


def _silu(x):
    return x * jax.nn.sigmoid(x)


def _inproj_body(x_ref, g_ref, w_ref, o_ref):
    h = _rms(x_ref[...], g_ref[...])
    o_ref[...] = jnp.dot(h.astype(BF16), w_ref[...], preferred_element_type=F32)


def _inproj(x, g, w):
    n = x.shape[0]
    return pl.pallas_call(
        _inproj_body,
        grid=(n // ROW_TILE,),
        in_specs=[
            pl.BlockSpec((ROW_TILE, D_MODEL), lambda i: (i, 0)),
            pl.BlockSpec((1, D_MODEL), lambda i: (0, 0)),
            pl.BlockSpec((D_MODEL, PROJ_W), lambda i: (0, 0)),
        ],
        out_specs=pl.BlockSpec((ROW_TILE, PROJ_W), lambda i: (i, 0)),
        out_shape=jax.ShapeDtypeStruct((n, PROJ_W), F32),
        compiler_params=pltpu.CompilerParams(dimension_semantics=("arbitrary",),
                                             vmem_limit_bytes=VMEM_LIMIT),
        name="in_proj",
    )(x, g, w)


def _log2(n):
    assert n & (n - 1) == 0
    return n.bit_length() - 1


def _tri_inv_all(ms, c, ii, jj):
    eye = (ii == jj).astype(F32)
    base = min(c, 8)
    sh = _log2(base)
    blk = (ii >> sh) == (jj >> sh)
    ns = [jnp.where(blk, m, 0.0) for m in ms]
    xs = [eye - n for n in ns]
    ps = [_dot(n, n) for n in ns]
    ts = [_dot(jnp.concatenate([x, p], axis=0), p) for x, p in zip(xs, ps)]
    xs = [x + t[:c] for x, t in zip(xs, ts)]
    ps = [t[c:] for t in ts]
    xs = [x + _dot(x, p) for x, p in zip(xs, ps)]
    s = base
    while s < c:
        sh_s, sh_b = _log2(s), _log2(2 * s)
        off = ((ii >> sh_b) == (jj >> sh_b)) & ((ii >> sh_s) != (jj >> sh_s))
        ys = [_dot(x, jnp.where(off, m, 0.0)) for x, m in zip(xs, ms)]
        xs = [x - _dot(y, x) for x, y in zip(xs, ys)]
        s *= 2
    return xs


def _gated_norm(o, w, z):
    return o * lax.rsqrt(jnp.mean(o * o, axis=-1, keepdims=True) + RMS_EPS) * w * _silu(z)


def _chunk_rows(s, tb_rows, ci, c):
    r = s * tb_rows + ci * c
    if not isinstance(r, int):
        r = pl.multiple_of(r, c)
    return r


def _for_chunks(n_chunks, step):
    if n_chunks == 1:
        step(0, 0)
    else:
        lax.fori_loop(0, n_chunks, step, 0)


def _gdn_body(qkv_ref, z_ref, sm_ref, cbuf_ref, s0_ref, cw_ref, alog_ref, dtb_ref, nw_ref,
              o_ref, sout_ref, cout_ref, st, xc, act, gcs, us, wss, qgs, kds, aqs,
              *, nb, tb_rows, chunk, valid, n_tb):
    tb = pl.program_id(1)
    c = chunk
    n_heads = GDN_HEADS
    tail = CONV_WIDTH - 1
    pad = SUBLANE
    units = [(s, h) for s in range(nb) for h in range(n_heads)]

    @pl.when(tb == 0)
    def _():
        st[...] = s0_ref[...]
        for s in range(nb):
            xc[s, pad - tail:pad, :] = cbuf_ref[s]

    if n_tb > 1:
        @pl.when(tb > 0)
        def _():
            for s in range(nb):
                xc[s, pad - tail:pad, :] = xc[s, tb_rows + pad - tail:tb_rows + pad, :]

    for s in range(nb):
        xc[s, pad:pad + tb_rows, :] = qkv_ref[s * tb_rows:(s + 1) * tb_rows, :]

    slab = min(tb_rows, 64)
    for s in range(nb):
        for sl in range(tb_rows // slab):
            for cb in range(GDN_CONV_CH // 512):
                cs = slice(cb * 512, (cb + 1) * 512)
                lo = pad - tail + sl * slab
                acc = xc[s, lo:lo + slab, cs] * cw_ref[0:1, cs]
                for i in range(1, CONV_WIDTH):
                    acc = acc + xc[s, lo + i:lo + i + slab, cs] * cw_ref[i:i + 1, cs]
                act[s * tb_rows + sl * slab:s * tb_rows + (sl + 1) * slab, cs] = _silu(acc)

    ii = lax.broadcasted_iota(jnp.int32, (c, c), 0)
    jj = lax.broadcasted_iota(jnp.int32, (c, c), 1)
    lower = (ii >= jj)
    lower_f = lower.astype(F32)
    strict = (ii > jj)
    rowmask = None
    if valid < c:
        rowmask = lax.broadcasted_iota(jnp.int32, (c, 1), 0) < valid

    def hs(h, w):
        return slice(h * w, (h + 1) * w)

    def phase1(ci, carry):
        rows, b_ts, gc_ts, gc_tts = [], [], [], []
        for s in range(nb):
            rr = pl.ds(_chunk_rows(s, tb_rows, ci, c), c)
            sm = sm_ref[rr, :]
            g_t = -jnp.exp(alog_ref[...]) * jax.nn.softplus(sm + dtb_ref[...])
            b_t = jax.nn.sigmoid(sm)
            if rowmask is not None:
                g_t = jnp.where(rowmask, g_t, 0.0)
                b_t = jnp.where(rowmask, b_t, 0.0)
            gc_t = _dot_hi(lower_f, g_t)
            gcs[rr, :] = gc_t
            rows.append(rr)
            b_ts.append(b_t)
            gc_ts.append(gc_t)
            gc_tts.append(gc_t.T)
        qn, kn, kb, vb = {}, {}, {}, {}
        for (s, h) in units:
            q = act[rows[s], hs(h, GDN_DK)]
            k = act[rows[s], slice(GDN_QK_W + h * GDN_DK, GDN_QK_W + (h + 1) * GDN_DK)]
            v = act[rows[s], slice(2 * GDN_QK_W + h * GDN_DV, 2 * GDN_QK_W + (h + 1) * GDN_DV)]
            if rowmask is not None:
                q = jnp.where(rowmask, q, 0.0)
                k = jnp.where(rowmask, k, 0.0)
                v = jnp.where(rowmask, v, 0.0)
            qn[s, h] = q * lax.rsqrt(jnp.sum(q * q, axis=-1, keepdims=True) + L2_EPS) * (GDN_DK ** -0.5)
            kn[s, h] = k * lax.rsqrt(jnp.sum(k * k, axis=-1, keepdims=True) + L2_EPS)
            beta = b_ts[s][:, SM_B + h:SM_B + h + 1]
            kb[s, h] = kn[s, h] * beta
            vb[s, h] = v * beta
        s1 = {u: _dot_nt(jnp.concatenate([kb[u], qn[u]], axis=0), kn[u]) for u in units}
        mm = []
        for (s, h) in units:
            gcol = gc_ts[s][:, SM_A + h:SM_A + h + 1]
            grow = gc_tts[s][SM_A + h:SM_A + h + 1, :]
            dec = jnp.exp(jnp.where(lower, gcol - grow, -jnp.inf))
            mm.append(jnp.where(strict, s1[s, h][:c] * dec, 0.0))
            aqs[h, rows[s], :] = s1[s, h][c:] * dec
        tms = _tri_inv_all(mm, c, ii, jj)
        for (s, h), tm in zip(units, tms):
            gcol = gc_ts[s][:, SM_A + h:SM_A + h + 1]
            eg = jnp.exp(gcol)
            uw = _dot(tm, jnp.concatenate([vb[s, h], kb[s, h] * eg], axis=1))
            us[rows[s], hs(h, GDN_DV)] = uw[:, :GDN_DV]
            wss[rows[s], hs(h, GDN_DV)] = uw[:, GDN_DV:]
            qgs[rows[s], hs(h, GDN_DK)] = qn[s, h] * eg
            kds[rows[s], hs(h, GDN_DK)] = kn[s, h] * jnp.exp(gcol[c - 1:c, :] - gcol)
        return carry

    def phase2(ci, carry):
        r0 = [_chunk_rows(s, tb_rows, ci, c) for s in range(nb)]
        rows = [pl.ds(r, c) for r in r0]
        ws = {(s, h): _dot(jnp.concatenate([wss[rows[s], hs(h, GDN_DV)], qgs[rows[s], hs(h, GDN_DK)]], axis=0),
                           st[s, h]) for (s, h) in units}
        v_new = {(s, h): us[rows[s], hs(h, GDN_DV)] - ws[s, h][:c] for (s, h) in units}
        o = {(s, h): ws[s, h][c:] + _dot(aqs[h, rows[s], :], v_new[s, h]) for (s, h) in units}
        upd = {(s, h): _dot_tn(kds[rows[s], hs(h, GDN_DK)], v_new[s, h]) for (s, h) in units}
        for (s, h) in units:
            g_last = gcs[pl.ds(r0[s] + c - 1, 1), SM_A + h:SM_A + h + 1]
            st[s, h] = st[s, h] * jnp.exp(g_last) + upd[s, h]
        for s in range(nb):
            o_ref[rows[s], :] = jnp.concatenate(
                [_gated_norm(o[s, h], nw_ref[...], z_ref[rows[s], hs(h, GDN_DV)]) for h in range(n_heads)], axis=1)
        return carry

    n_chunks = tb_rows // c
    _for_chunks(n_chunks, phase1)
    _for_chunks(n_chunks, phase2)

    @pl.when(tb == n_tb - 1)
    def _():
        sout_ref[...] = st[...]
        last = tb_rows if valid == c else valid
        for s in range(nb):
            cout_ref[s] = xc[s, pad + last - tail:pad + last, :]


def _gdn(proj, n_seq, nb, t_len, tb_rows, chunk, valid, conv_buf, s0, conv_w, alog, dtb, nw):
    n_tb = t_len // tb_rows
    assert nb == 1 or n_tb == 1
    rows = nb * tb_rows

    def rowblk(b, t):
        return b * n_tb + t

    body = functools.partial(_gdn_body, nb=nb, tb_rows=tb_rows, chunk=chunk, valid=valid, n_tb=n_tb)
    return pl.pallas_call(
        body,
        grid=(n_seq // nb, n_tb),
        in_specs=[
            pl.BlockSpec((rows, GDN_CONV_CH), lambda b, t: (rowblk(b, t), COL_QKV // GDN_CONV_CH)),
            pl.BlockSpec((rows, GDN_V_W), lambda b, t: (rowblk(b, t), COL_Z // GDN_V_W)),
            pl.BlockSpec((rows, SM_W), lambda b, t: (rowblk(b, t), COL_SM // SM_W)),
            pl.BlockSpec((nb, CONV_WIDTH - 1, GDN_CONV_CH), lambda b, t: (b, 0, 0)),
            pl.BlockSpec((nb, GDN_HEADS, GDN_DK, GDN_DV), lambda b, t: (b, 0, 0, 0)),
            pl.BlockSpec((CONV_WIDTH, GDN_CONV_CH), lambda b, t: (0, 0)),
            pl.BlockSpec((1, SM_W), lambda b, t: (0, 0)),
            pl.BlockSpec((1, SM_W), lambda b, t: (0, 0)),
            pl.BlockSpec((1, GDN_DV), lambda b, t: (0, 0)),
        ],
        out_specs=[
            pl.BlockSpec((rows, GDN_V_W), lambda b, t: (rowblk(b, t), 0)),
            pl.BlockSpec((nb, GDN_HEADS, GDN_DK, GDN_DV), lambda b, t: (b, 0, 0, 0)),
            pl.BlockSpec((nb, CONV_WIDTH - 1, GDN_CONV_CH), lambda b, t: (b, 0, 0)),
        ],
        out_shape=[
            jax.ShapeDtypeStruct((n_seq * t_len, GDN_V_W), F32),
            jax.ShapeDtypeStruct((n_seq, GDN_HEADS, GDN_DK, GDN_DV), F32),
            jax.ShapeDtypeStruct((n_seq, CONV_WIDTH - 1, GDN_CONV_CH), F32),
        ],
        scratch_shapes=[
            pltpu.VMEM((nb, GDN_HEADS, GDN_DK, GDN_DV), F32),
            pltpu.VMEM((nb, tb_rows + SUBLANE, GDN_CONV_CH), F32),
            pltpu.VMEM((rows, GDN_CONV_CH), F32),
            pltpu.VMEM((rows, SM_W), F32),
            pltpu.VMEM((rows, GDN_V_W), F32),
            pltpu.VMEM((rows, GDN_V_W), F32),
            pltpu.VMEM((rows, GDN_QK_W), F32),
            pltpu.VMEM((rows, GDN_QK_W), F32),
            pltpu.VMEM((GDN_HEADS, rows, chunk), F32),
        ],
        compiler_params=pltpu.CompilerParams(dimension_semantics=("arbitrary", "arbitrary"),
                                             vmem_limit_bytes=VMEM_LIMIT),
        name="gdn_mixer",
    )(proj, proj, proj, conv_buf, s0, conv_w, alog, dtb, nw)


def _gla_body(q_ref, k_ref, v_ref, go_ref, sm_ref, s0_ref, wgk_ref, bgk_ref, nw_ref,
              o_ref, sout_ref, st, *, nb, tb_rows, chunk, valid, n_tb):
    tb = pl.program_id(1)
    c = chunk
    n_heads = GLA_HEADS
    units = [(s, h) for s in range(nb) for h in range(n_heads)]

    @pl.when(tb == 0)
    def _():
        st[...] = s0_ref[...]

    ii = lax.broadcasted_iota(jnp.int32, (c, c), 0)
    jj = lax.broadcasted_iota(jnp.int32, (c, c), 1)
    lower = (ii >= jj)
    lower_f = lower.astype(F32)
    rid = lax.broadcasted_iota(jnp.int32, (c, 1), 0)
    rowmask = (rid < valid) if valid < c else None
    n_sub = max(c // 16, 1)
    sub = c // n_sub

    def chunk_step(ci, carry):
        rows, bcs, bc_ts = [], [], []
        for s in range(nb):
            rr = pl.ds(_chunk_rows(s, tb_rows, ci, c), c)
            gk = jax.nn.log_sigmoid(_dot(sm_ref[rr, :], wgk_ref[...]) + bgk_ref[...]) / GLA_GATE_NORMALIZER
            if rowmask is not None:
                gk = jnp.where(rowmask, gk, 0.0)
            bc = _dot_hi(lower_f, gk)
            rows.append(rr)
            bcs.append(bc)
            bc_ts.append(bc.T)
        q, k, v, bch = {}, {}, {}, {}
        for (s, h) in units:
            ks = slice(h * GLA_DK, (h + 1) * GLA_DK)
            vs = slice(h * GLA_DV, (h + 1) * GLA_DV)
            q[s, h] = q_ref[rows[s], ks] * (GLA_DK ** -0.5)
            kk = k_ref[rows[s], ks]
            vv = v_ref[rows[s], vs]
            if rowmask is not None:
                kk = jnp.where(rowmask, kk, 0.0)
                vv = jnp.where(rowmask, vv, 0.0)
            k[s, h], v[s, h] = kk, vv
            bch[s, h] = bcs[s][:, ks]
        o_inter = {u: _dot(q[u] * jnp.exp(bch[u]), st[u[0], u[1]]) for u in units}
        a = {}
        for u in units:
            q_parts, k_parts = [], []
            for sb in range(n_sub):
                ref_row = bch[u][sb * sub:sb * sub + 1, :]
                in_blk = (rid >= sb * sub) & (rid < (sb + 1) * sub)
                q_parts.append(jnp.where(in_blk, q[u] * jnp.exp(jnp.where(in_blk, bch[u] - ref_row, 0.0)), 0.0))
                k_parts.append(k[u] * jnp.exp(jnp.where(rid < (sb + 1) * sub, ref_row - bch[u], 0.0)))
            q_hat = jnp.concatenate(q_parts, axis=1) if n_sub > 1 else q_parts[0]
            k_hat = jnp.concatenate(k_parts, axis=1) if n_sub > 1 else k_parts[0]
            a[u] = jnp.where(lower, _dot_nt(q_hat, k_hat), 0.0)
        upd = {u: _dot_tn(k[u] * jnp.exp(bch[u][c - 1:c, :] - bch[u]), v[u]) for u in units}
        o = {u: o_inter[u] + _dot(a[u], v[u]) for u in units}
        for (s, h) in units:
            dec_col = bc_ts[s][h * GLA_DK:(h + 1) * GLA_DK, c - 1:c]
            st[s, h] = jnp.exp(dec_col) * st[s, h] + upd[s, h]
        for s in range(nb):
            o_ref[rows[s], :] = jnp.concatenate(
                [_gated_norm(o[s, h], nw_ref[...], go_ref[rows[s], h * GLA_DV:(h + 1) * GLA_DV])
                 for h in range(n_heads)], axis=1)
        return carry

    _for_chunks(tb_rows // c, chunk_step)

    @pl.when(tb == n_tb - 1)
    def _():
        sout_ref[...] = st[...]


def _gla(proj, n_seq, nb, t_len, tb_rows, chunk, valid, s0, wgk, bgk, nw):
    n_tb = t_len // tb_rows
    assert nb == 1 or n_tb == 1
    rows = nb * tb_rows

    def rowblk(b, t):
        return b * n_tb + t

    body = functools.partial(_gla_body, nb=nb, tb_rows=tb_rows, chunk=chunk, valid=valid, n_tb=n_tb)
    return pl.pallas_call(
        body,
        grid=(n_seq // nb, n_tb),
        in_specs=[
            pl.BlockSpec((rows, GLA_QK_W), lambda b, t: (rowblk(b, t), COL_GQ // GLA_QK_W)),
            pl.BlockSpec((rows, GLA_QK_W), lambda b, t: (rowblk(b, t), COL_GK // GLA_QK_W)),
            pl.BlockSpec((rows, GLA_V_W), lambda b, t: (rowblk(b, t), COL_GV // GLA_V_W)),
            pl.BlockSpec((rows, GLA_V_W), lambda b, t: (rowblk(b, t), COL_GG // GLA_V_W)),
            pl.BlockSpec((rows, SM_W), lambda b, t: (rowblk(b, t), COL_SM // SM_W)),
            pl.BlockSpec((nb, GLA_HEADS, GLA_DK, GLA_DV), lambda b, t: (b, 0, 0, 0)),
            pl.BlockSpec((SM_W, GLA_QK_W), lambda b, t: (0, 0)),
            pl.BlockSpec((1, GLA_QK_W), lambda b, t: (0, 0)),
            pl.BlockSpec((1, GLA_DV), lambda b, t: (0, 0)),
        ],
        out_specs=[
            pl.BlockSpec((rows, GLA_V_W), lambda b, t: (rowblk(b, t), 0)),
            pl.BlockSpec((nb, GLA_HEADS, GLA_DK, GLA_DV), lambda b, t: (b, 0, 0, 0)),
        ],
        out_shape=[
            jax.ShapeDtypeStruct((n_seq * t_len, GLA_V_W), F32),
            jax.ShapeDtypeStruct((n_seq, GLA_HEADS, GLA_DK, GLA_DV), F32),
        ],
        scratch_shapes=[pltpu.VMEM((nb, GLA_HEADS, GLA_DK, GLA_DV), F32)],
        compiler_params=pltpu.CompilerParams(dimension_semantics=("arbitrary", "arbitrary"),
                                             vmem_limit_bytes=VMEM_LIMIT),
        name="gla_mixer",
    )(proj, proj, proj, proj, proj, s0, wgk, bgk, nw)


def _outproj_body(og_ref, ol_ref, x_ref, wo_ref, g_ref, wr_ref, br_ref,
                  x1_ref, h2_ref, rt_ref, cnt_ref, base):
    i = pl.program_id(0)

    @pl.when(i == 0)
    def _():
        base[...] = jnp.zeros_like(base)

    o = jnp.concatenate([og_ref[...], ol_ref[...]], axis=1)
    x1 = x_ref[...] + jnp.dot(o.astype(BF16), wo_ref[...], preferred_element_type=F32)
    x1_ref[...] = x1
    h = _rms(x1, g_ref[...])
    _store_token_tiles(h2_ref, h)
    logits =_dot_hi(h, wr_ref[...]) + br_ref[...]

    tm = logits.shape[0]
    lane = lax.broadcasted_iota(jnp.int32, (tm, LANE), 1)
    work = logits
    sel = jnp.zeros((tm, LANE), F32)
    ids, vals = [], []
    for _ in range(TOP_K):
        m = jnp.max(work, axis=-1, keepdims=True)
        idx = jnp.min(jnp.where(work == m, lane, LANE), axis=-1, keepdims=True)
        hit = lane == idx
        ids.append(idx)
        vals.append(m)
        work = jnp.where(hit, -jnp.inf, work)
        sel = sel + hit.astype(F32)
    exps = [jnp.exp(v - vals[0]) for v in vals]
    den = exps[0]
    for e in exps[1:]:
        den = den + e
    gates = [e / den for e in exps]

    ri = lax.broadcasted_iota(jnp.int32, (tm, tm), 0)
    ci = lax.broadcasted_iota(jnp.int32, (tm, tm), 1)
    before = _dot((ri > ci).astype(F32), sel) + base[...]
    ranks = [jnp.sum(jnp.where(lane == idx, before, 0.0), axis=-1, keepdims=True) for idx in ids]
    base[...] = base[...] + jnp.sum(sel, axis=0, keepdims=True)
    cnt_ref[...] = base[...]

    rec = jnp.zeros((tm, LANE), F32)
    for k in range(TOP_K):
        rec = jnp.where(lane == k, ids[k].astype(F32), rec)
        rec = jnp.where(lane == TOP_K + k, ranks[k], rec)
        rec = jnp.where(lane == 2 * TOP_K + k, gates[k], rec)
    rt_ref[...] = rec


def _outproj(og, ol, x, wo, g, wr, br):
    n = x.shape[0]
    return pl.pallas_call(
        _outproj_body,
        grid=(n // ROW_TILE,),
        in_specs=[
            pl.BlockSpec((ROW_TILE, GDN_V_W), lambda i: (i, 0)),
            pl.BlockSpec((ROW_TILE, GLA_V_W), lambda i: (i, 0)),
            pl.BlockSpec((ROW_TILE, D_MODEL), lambda i: (i, 0)),
            pl.BlockSpec((D_MODEL, D_MODEL), lambda i: (0, 0)),
            pl.BlockSpec((1, D_MODEL), lambda i: (0, 0)),
            pl.BlockSpec((D_MODEL, LANE), lambda i: (0, 0)),
            pl.BlockSpec((1, LANE), lambda i: (0, 0)),
        ],
        out_specs=[
            pl.BlockSpec((ROW_TILE, D_MODEL), lambda i: (i, 0)),
            pl.BlockSpec((ROW_TILE * TOK_TILES, LANE), lambda i: (i, 0)),
            pl.BlockSpec((ROW_TILE, LANE), lambda i: (i, 0)),
            pl.BlockSpec((1, LANE), lambda i: (0, 0)),
        ],
        out_shape=[
            jax.ShapeDtypeStruct((n, D_MODEL), F32),
            jax.ShapeDtypeStruct((n * TOK_TILES, LANE), F32),
            jax.ShapeDtypeStruct((n, LANE), F32),
            jax.ShapeDtypeStruct((1, LANE), F32),
        ],
        scratch_shapes=[pltpu.VMEM((1, LANE), F32)],
        compiler_params=pltpu.CompilerParams(dimension_semantics=("arbitrary",),
                                             vmem_limit_bytes=VMEM_LIMIT),
        name="out_proj",
    )(og, ol, x, wo, g, wr, br)


def _store_token_tiles(ref2d, val):
    rows = val.shape[0]
    for c in range(TOK_TILES):
        ref2d[pl.ds(c, rows, stride=TOK_TILES), :] = val[:, c * LANE:(c + 1) * LANE]


def _load_token_tiles(ref2d, first_row, rows):
    return jnp.concatenate(
        [ref2d[pl.ds(first_row * TOK_TILES + c, rows, stride=TOK_TILES), :] for c in range(TOK_TILES)], axis=1)


def _gather_rows(src_tiles, idx_ref, n_rows, dst2d, sem, priorities):
    def issue(j, carry):
        for u in range(DMA_ISSUE_UNROLL):
            r = j * DMA_ISSUE_UNROLL + u
            dst = dst2d.at[pl.ds(pl.multiple_of(r * TOK_TILES, TOK_TILES), TOK_TILES), :]
            pltpu.make_async_copy(src_tiles.at[idx_ref[0, r]], dst, sem).start(
                priority=priorities[u % len(priorities)])
        return carry

    lax.fori_loop(0, n_rows // DMA_ISSUE_UNROLL, issue, 0)


def _wait_rows(src2d, n_rows, dst2d, sem):
    pltpu.make_async_copy(src2d.at[pl.ds(0, n_rows * TOK_TILES), :], dst2d, sem).wait()


def _expert_body(be_ref, nu_ref, idx_ref, idx_next_ref, h_tiles, h_2d, wup_ref, bup_ref, wdn_ref, bdn_ref, y_ref,
                 xbuf, sems):
    i = pl.program_id(0)
    n_used = nu_ref[0]
    slot = i % 2

    @pl.when((i == 0) & (n_used > 0))
    def _():
        _gather_rows(h_tiles, idx_ref, EXPERT_ROWS, xbuf.at[0], sems.at[0], EXPERT_GATHER_QUEUES)

    @pl.when(i < n_used)
    def _():
        _wait_rows(h_2d, EXPERT_ROWS, xbuf.at[slot], sems.at[slot])

        @pl.when(i + 1 < n_used)
        def _():
            _gather_rows(h_tiles, idx_next_ref, EXPERT_ROWS, xbuf.at[1 - slot], sems.at[1 - slot],
                         EXPERT_GATHER_QUEUES)

        gu = _dot(_load_token_tiles(xbuf.at[slot], 0, EXPERT_ROWS), wup_ref[...]) + bup_ref[...]
        gate = jnp.minimum(gu[:, :D_FF], SWIGLU_LIMIT)
        up = jnp.clip(gu[:, D_FF:], -SWIGLU_LIMIT, SWIGLU_LIMIT)
        a = (up + 1.0) * gate * jax.nn.sigmoid(SWIGLU_ALPHA * gate)
        _store_token_tiles(y_ref, _dot(a, wdn_ref[...]) + bdn_ref[...])

    @pl.when(i >= n_used)
    def _():
        y_ref[...] = jnp.zeros_like(y_ref)


def _experts(block_e, n_used, src_tok, h2_2d, w_up, b_up, w_down, b_down):
    n_blocks = block_e.shape[0]
    src_blocks = src_tok.reshape(n_blocks, 1, EXPERT_ROWS)
    grid_spec = pltpu.PrefetchScalarGridSpec(
        num_scalar_prefetch=2,
        grid=(n_blocks,),
        in_specs=[
            pl.BlockSpec((None, 1, EXPERT_ROWS), lambda i, be, nu: (i, 0, 0), memory_space=pltpu.SMEM),
            pl.BlockSpec((None, 1, EXPERT_ROWS), lambda i, be, nu: (jnp.minimum(i + 1, n_blocks - 1), 0, 0),
                         memory_space=pltpu.SMEM),
            pl.BlockSpec(memory_space=pl.ANY),
            pl.BlockSpec(memory_space=pl.ANY),
            pl.BlockSpec((None, D_MODEL, 2 * D_FF), lambda i, be, nu: (be[i], 0, 0)),
            pl.BlockSpec((None, 1, 2 * D_FF), lambda i, be, nu: (be[i], 0, 0)),
            pl.BlockSpec((None, D_FF, D_MODEL), lambda i, be, nu: (be[i], 0, 0)),
            pl.BlockSpec((None, 1, D_MODEL), lambda i, be, nu: (be[i], 0, 0)),
        ],
        out_specs=pl.BlockSpec((EXPERT_ROWS * TOK_TILES, LANE), lambda i, be, nu: (i, 0)),
        scratch_shapes=[pltpu.VMEM((2, EXPERT_ROWS * TOK_TILES, LANE), F32), pltpu.SemaphoreType.DMA((2,))],
    )
    return pl.pallas_call(
        _expert_body,
        grid_spec=grid_spec,
        out_shape=jax.ShapeDtypeStruct((n_blocks * EXPERT_ROWS * TOK_TILES, LANE), F32),
        compiler_params=pltpu.CompilerParams(dimension_semantics=("arbitrary",),
                                             vmem_limit_bytes=VMEM_LIMIT),
        name="experts",
    )(block_e, n_used, src_blocks, src_blocks, h2_2d.reshape(-1, TOK_TILES, LANE), h2_2d, w_up,
      b_up.reshape(N_EXPERTS, 1, 2 * D_FF), w_down, b_down.reshape(N_EXPERTS, 1, D_MODEL))


def _combine_body(dest_ref, dest_next_ref, rt_ref, y_tiles, y_2d, x1_ref, g_ref, o_ref, ybuf, sems):
    i = pl.program_id(0)
    slot = i % 2
    n_rows = TOP_K * COMBINE_ROWS

    @pl.when(i == 0)
    def _():
        _gather_rows(y_tiles, dest_ref, n_rows, ybuf.at[0], sems.at[0], COMBINE_GATHER_QUEUES)

    _wait_rows(y_2d, n_rows, ybuf.at[slot], sems.at[slot])

    @pl.when(i + 1 < pl.num_programs(0))
    def _():
        _gather_rows(y_tiles, dest_next_ref, n_rows, ybuf.at[1 - slot], sems.at[1 - slot], COMBINE_GATHER_QUEUES)

    buf = ybuf.at[slot]
    moe = _load_token_tiles(buf, 0, COMBINE_ROWS) * rt_ref[:, 2 * TOP_K:2 * TOP_K + 1]
    for k in range(1, TOP_K):
        moe = moe + _load_token_tiles(buf, k * COMBINE_ROWS, COMBINE_ROWS) * rt_ref[:, 2 * TOP_K + k:2 * TOP_K + k + 1]
    o_ref[...] = _rms(x1_ref[...] + moe, g_ref[...])


def _combine(dest, rt, y_2d, x1, g):
    n = x1.shape[0]
    n_blk = n // COMBINE_ROWS
    dest_blocks = dest.reshape(n_blk, COMBINE_ROWS, TOP_K).transpose(0, 2, 1).reshape(n_blk, 1, TOP_K * COMBINE_ROWS)
    return pl.pallas_call(
        _combine_body,
        grid=(n_blk,),
        in_specs=[
            pl.BlockSpec((None, 1, COMBINE_ROWS * TOP_K), lambda i: (i, 0, 0), memory_space=pltpu.SMEM),
            pl.BlockSpec((None, 1, COMBINE_ROWS * TOP_K), lambda i: (jnp.minimum(i + 1, n_blk - 1), 0, 0),
                         memory_space=pltpu.SMEM),
            pl.BlockSpec((COMBINE_ROWS, LANE), lambda i: (i, 0)),
            pl.BlockSpec(memory_space=pl.ANY),
            pl.BlockSpec(memory_space=pl.ANY),
            pl.BlockSpec((COMBINE_ROWS, D_MODEL), lambda i: (i, 0)),
            pl.BlockSpec((1, D_MODEL), lambda i: (0, 0)),
        ],
        out_specs=pl.BlockSpec((COMBINE_ROWS, D_MODEL), lambda i: (i, 0)),
        out_shape=jax.ShapeDtypeStruct((n, D_MODEL), F32),
        scratch_shapes=[pltpu.VMEM((2, TOP_K * COMBINE_ROWS * TOK_TILES, LANE), F32),
                        pltpu.SemaphoreType.DMA((2,))],
        compiler_params=pltpu.CompilerParams(dimension_semantics=("arbitrary",),
                                             vmem_limit_bytes=VMEM_LIMIT),
        name="combine",
    )(dest_blocks, dest_blocks, rt, y_2d.reshape(-1, TOK_TILES, LANE), y_2d, x1, g)


def _route(rt, cnt):
    n = rt.shape[0]
    n_assign = n * TOP_K
    flat_e = rt[:, 0:TOP_K].astype(jnp.int32).reshape(-1)
    rank = rt[:, TOP_K:2 * TOP_K].astype(jnp.int32).reshape(-1)
    counts = cnt[0, :N_EXPERTS].astype(jnp.int32)
    padded = (counts + EXPERT_ROWS - 1) // EXPERT_ROWS * EXPERT_ROWS
    pend = jnp.cumsum(padded)
    pstart = pend - padded
    dest = (pstart[flat_e] + rank).astype(jnp.int32)
    n_rows = n_assign + N_EXPERTS * EXPERT_ROWS
    n_blocks = n_rows // EXPERT_ROWS
    src_tok = jnp.zeros((n_rows,), jnp.int32).at[dest].set(jnp.arange(n_assign, dtype=jnp.int32) // TOP_K)
    block_first = jnp.arange(n_blocks, dtype=jnp.int32) * EXPERT_ROWS
    block_e = jnp.minimum(jnp.sum((pend[None, :] <= block_first[:, None]).astype(jnp.int32), axis=1),
                          N_EXPERTS - 1).astype(jnp.int32)
    n_used = (pend[-1] // EXPERT_ROWS).astype(jnp.int32).reshape(1)
    return dest, src_tok, block_e, n_used


def _pad_lanes(v, width):
    return jnp.zeros((1, width), F32).at[0, :v.shape[0]].set(v.astype(F32))


def kernel(x_prompt, x_sample, state_gdn_conv, state_gdn, state_gla, rms_mix_w, w_in, conv_w, gdn_a_log,
           gdn_dt_bias, gdn_norm_w, gla_gk_w, gla_gk_b, gla_norm_w, w_out, rms_ffn_w, w_router, b_router,
           w_up, b_up, w_down, b_down, rms_final_w):
    bp, tp, d = x_prompt.shape
    bs, ts, _ = x_sample.shape
    n_p, n_s = bp * tp, bs * ts
    assert d == D_MODEL and state_gdn.shape[0] == 1, "single-layer kernel"
    assert tp >= CONV_WIDTH - 1 and ts >= CONV_WIDTH - 1, "new conv state is taken from the new tokens only"
    l = 0

    wi = w_in[l]
    a0 = GDN_CONV_CH + GDN_V_W
    g0 = a0 + 2 * GDN_HEADS
    lr0 = g0 + 2 * GLA_QK_W + 2 * GLA_V_W
    small = jnp.concatenate([wi[:, a0:a0 + 2 * GDN_HEADS], wi[:, lr0:lr0 + GLA_GATE_RANK],
                             jnp.zeros((d, SM_W - 2 * GDN_HEADS - GLA_GATE_RANK), F32)], axis=1)
    w_big = jnp.concatenate([wi[:, :a0], wi[:, g0:lr0], small], axis=1).astype(BF16)
    alog = _pad_lanes(gdn_a_log[l], SM_W)
    dtb = _pad_lanes(gdn_dt_bias[l], SM_W)
    wgk = jnp.zeros((SM_W, GLA_QK_W), F32).at[SM_LR:SM_LR + GLA_GATE_RANK].set(gla_gk_w[l])
    wr = jnp.zeros((d, LANE), F32).at[:, :N_EXPERTS].set(w_router[l])
    br = jnp.full((1, LANE), -1e30, F32).at[0, :N_EXPERTS].set(b_router[l])

    x_all = jnp.concatenate([x_prompt.reshape(n_p, d), x_sample.reshape(n_s, d)], axis=0)
    proj = _inproj(x_all, rms_mix_w[l][None, :], w_big)

    tb_p = PROMPT_TIME_BLOCK
    zeros_conv = jnp.zeros((bp, CONV_WIDTH - 1, GDN_CONV_CH), F32)
    og_p, gdn_p, conv_p = _gdn(proj, bp, 1, tp, tb_p, CHUNK, CHUNK, zeros_conv,
                               jnp.zeros((bp, GDN_HEADS, GDN_DK, GDN_DV), F32), conv_w[l], alog, dtb,
                               gdn_norm_w[l][None, :])
    ol_p, gla_p = _gla(proj, bp, 1, tp, tb_p, CHUNK, CHUNK, jnp.zeros((bp, GLA_HEADS, GLA_DK, GLA_DV), F32),
                       wgk, gla_gk_b[l][None, :], gla_norm_w[l][None, :])

    ts_pad = SUBLANE
    nb_s = SAMPLE_SEQS_PER_STEP
    proj_s = proj[n_p:].reshape(bs, ts, PROJ_W)
    proj_sp = jnp.pad(proj_s, ((0, 0), (0, ts_pad - ts), (0, 0))).reshape(bs * ts_pad, PROJ_W)
    og_s, gdn_s, conv_s = _gdn(proj_sp, bs, nb_s, ts_pad, ts_pad, ts_pad, ts, state_gdn_conv[l], state_gdn[l],
                               conv_w[l], alog, dtb, gdn_norm_w[l][None, :])
    ol_s, gla_s = _gla(proj_sp, bs, nb_s, ts_pad, ts_pad, ts_pad, ts, state_gla[l], wgk, gla_gk_b[l][None, :],
                       gla_norm_w[l][None, :])
    og_s = og_s.reshape(bs, ts_pad, GDN_V_W)[:, :ts].reshape(n_s, GDN_V_W)
    ol_s = ol_s.reshape(bs, ts_pad, GLA_V_W)[:, :ts].reshape(n_s, GLA_V_W)

    og = jnp.concatenate([og_p, og_s], axis=0)
    ol = jnp.concatenate([ol_p, ol_s], axis=0)
    x1, h2, rt, cnt = _outproj(og, ol, x_all, w_out[l].astype(BF16), rms_ffn_w[l][None, :], wr, br)

    dest, src_tok, block_e, n_used = _route(rt, cnt)
    y_rows = _experts(block_e, n_used, src_tok, h2, w_up[l], b_up[l], w_down[l], b_down[l])
    y_all = _combine(dest, rt, y_rows, x1, rms_final_w[None, :])

    y_prompt = y_all[:n_p].reshape(bp, tp, d)
    y_sample = y_all[n_p:].reshape(bs, ts, d)
    return (y_prompt, y_sample, conv_p[None], gdn_p[None], gla_p[None], conv_s[None], gdn_s[None], gla_s[None])
```

```python
import functools

import jax
import jax.numpy as jnp
from jax import lax
from jax.experimental import pallas as pl
from jax.experimental.pallas import tpu as pltpu

F32 = jnp.float32
BF16 = jnp.bfloat16
HI = lax.Precision.HIGHEST

D_MODEL = 1024
GDN_HEADS = 4
GDN_DK = 128
GDN_DV = 128
GLA_HEADS = 4
GLA_DK = 64
GLA_DV = 128
GLA_GATE_RANK = 16
GLA_GATE_NORMALIZER = 16.0
CONV_WIDTH = 4
CHUNK = 64
N_EXPERTS = 32
TOP_K = 4
D_FF = 1024
SWIGLU_LIMIT = 7.0
SWIGLU_ALPHA = 1.702
RMS_EPS = 1e-6
L2_EPS = 1e-6

GDN_QK_W = GDN_HEADS * GDN_DK
GDN_V_W = GDN_HEADS * GDN_DV
GDN_CONV_CH = 2 * GDN_QK_W + GDN_V_W
GLA_QK_W = GLA_HEADS * GLA_DK
GLA_V_W = GLA_HEADS * GLA_DV

COL_QKV = 0
COL_Z = 1536
COL_GQ = 2048
COL_GK = 2304
COL_GV = 2560
COL_GG = 3072
COL_SM = 3584
SM_W = 128
PROJ_W = COL_SM + SM_W
SM_A, SM_B, SM_LR = 0, 4, 8

LANE = 128
SUBLANE = 8
TOK_TILES = D_MODEL // LANE
ROW_TILE = 256
EXPERT_ROWS = 256
COMBINE_ROWS = 128
DMA_ISSUE_UNROLL = 8
INVERT_BLOCK = 1024
EXPERT_GATHER_QUEUES = (0, 1)
COMBINE_GATHER_QUEUES = (0, 1)
PROMPT_TIME_BLOCK = 512
SAMPLE_SEQS_PER_STEP = 8
VMEM_LIMIT = 56 * 1024 * 1024


def _dot(a, b):
    return jnp.dot(a.astype(BF16), b.astype(BF16), preferred_element_type=F32)


def _dot_nt(a, b):
    return lax.dot_general(a.astype(BF16), b.astype(BF16), (((1,), (1,)), ((), ())),
                           preferred_element_type=F32)


def _dot_tn(a, b):
    return lax.dot_general(a.astype(BF16), b.astype(BF16), (((0,), (0,)), ((), ())),
                           preferred_element_type=F32)


def _dot_hi(a, b):
    return jnp.dot(a, b, precision=HI, preferred_element_type=F32)


def _dot_3pass(a, b):
    a_hi = a.astype(BF16)
    b_hi = b.astype(BF16)
    a_lo = (a - a_hi.astype(F32)).astype(BF16)
    b_lo = (b - b_hi.astype(F32)).astype(BF16)

    def mm(x, y):
        return jnp.dot(x, y, preferred_element_type=F32)

    return (mm(a_lo, b_hi) + mm(a_hi, b_lo)) + mm(a_hi, b_hi)


def _rms(x, w):
    return x * lax.rsqrt(jnp.mean(x * x, axis=-1, keepdims=True) + RMS_EPS) * w


def _silu(x):
    return x * jax.nn.sigmoid(x)


def _group_specs(rows, width, n_p_blocks):
    return [pl.BlockSpec((rows, width), lambda i: (jnp.minimum(i, n_p_blocks - 1), 0)),
            pl.BlockSpec((rows, width), lambda i: (jnp.maximum(i - n_p_blocks, 0), 0))]


def _group_pick(i, n_p_blocks, p_ref, s_ref):
    return jnp.where(i < n_p_blocks, p_ref[...], s_ref[...])


def _inproj_body(xp_ref, xs_ref, g_ref, w_ref, o_ref, *, n_p_blocks):
    x = _group_pick(pl.program_id(0), n_p_blocks, xp_ref, xs_ref)
    h = _rms(x, g_ref[...])
    o_ref[...] = jnp.dot(h.astype(BF16), w_ref[...], preferred_element_type=F32)


def _inproj(x_p, x_s, g, w):
    n_p_blocks, n_s_blocks = x_p.shape[0] // ROW_TILE, x_s.shape[0] // ROW_TILE
    n = x_p.shape[0] + x_s.shape[0]
    return pl.pallas_call(
        functools.partial(_inproj_body, n_p_blocks=n_p_blocks),
        grid=(n_p_blocks + n_s_blocks,),
        in_specs=_group_specs(ROW_TILE, D_MODEL, n_p_blocks) + [
            pl.BlockSpec((1, D_MODEL), lambda i: (0, 0)),
            pl.BlockSpec((D_MODEL, PROJ_W), lambda i: (0, 0)),
        ],
        out_specs=pl.BlockSpec((ROW_TILE, PROJ_W), lambda i: (i, 0)),
        out_shape=jax.ShapeDtypeStruct((n, PROJ_W), F32),
        compiler_params=pltpu.CompilerParams(dimension_semantics=("arbitrary",),
                                             vmem_limit_bytes=VMEM_LIMIT),
        name="in_proj",
    )(x_p, x_s, g, w)


def _log2(n):
    assert n & (n - 1) == 0
    return n.bit_length() - 1


def _tri_inv_all(ms, c, ii, jj):
    eye = (ii == jj).astype(F32)
    base = min(c, 8)
    sh = _log2(base)
    blk = (ii >> sh) == (jj >> sh)
    ns = [jnp.where(blk, m, 0.0) for m in ms]
    xs = [eye - n for n in ns]
    ps = [_dot(n, n) for n in ns]
    ts = [_dot(jnp.concatenate([x, p], axis=0), p) for x, p in zip(xs, ps)]
    xs = [x + t[:c] for x, t in zip(xs, ts)]
    ps = [t[c:] for t in ts]
    xs = [x + _dot(x, p) for x, p in zip(xs, ps)]
    s = base
    while s < c:
        sh_s, sh_b = _log2(s), _log2(2 * s)
        off = ((ii >> sh_b) == (jj >> sh_b)) & ((ii >> sh_s) != (jj >> sh_s))
        ys = [_dot(x, jnp.where(off, m, 0.0)) for x, m in zip(xs, ms)]
        xs = [x - _dot(y, x) for x, y in zip(xs, ys)]
        s *= 2
    return xs


def _gated_norm(o, w, z):
    return o * lax.rsqrt(jnp.mean(o * o, axis=-1, keepdims=True) + RMS_EPS) * w * _silu(z)


def _chunk_rows(s, tb_rows, ci, c):
    r = s * tb_rows + ci * c
    if not isinstance(r, int):
        r = pl.multiple_of(r, c)
    return r


def _for_chunks(n_chunks, step):
    if n_chunks == 1:
        step(0, 0)
    else:
        lax.fori_loop(0, n_chunks, step, 0)


def _gdn_body(qkv_ref, z_ref, sm_ref, cbuf_ref, s0_ref, cw_ref, alog_ref, dtb_ref, nw_ref,
              o_ref, sout_ref, cout_ref, st, xc, act, gcs, us, wss, qgs, kds, aqs,
              *, nb, tb_rows, chunk, valid, n_tb):
    tb = pl.program_id(1)
    c = chunk
    n_heads = GDN_HEADS
    tail = CONV_WIDTH - 1
    pad = SUBLANE
    units = [(s, h) for s in range(nb) for h in range(n_heads)]

    @pl.when(tb == 0)
    def _():
        st[...] = s0_ref[...]
        for s in range(nb):
            xc[s, pad - tail:pad, :] = cbuf_ref[s]

    if n_tb > 1:
        @pl.when(tb > 0)
        def _():
            for s in range(nb):
                xc[s, pad - tail:pad, :] = xc[s, tb_rows + pad - tail:tb_rows + pad, :]

    for s in range(nb):
        xc[s, pad:pad + tb_rows, :] = qkv_ref[s * tb_rows:(s + 1) * tb_rows, :]

    slab = min(tb_rows, 64)
    for s in range(nb):
        for sl in range(tb_rows // slab):
            for cb in range(GDN_CONV_CH // 512):
                cs = slice(cb * 512, (cb + 1) * 512)
                lo = pad - tail + sl * slab
                acc = xc[s, lo:lo + slab, cs] * cw_ref[0:1, cs]
                for i in range(1, CONV_WIDTH):
                    acc = acc + xc[s, lo + i:lo + i + slab, cs] * cw_ref[i:i + 1, cs]
                act[s * tb_rows + sl * slab:s * tb_rows + (sl + 1) * slab, cs] = _silu(acc)

    ii = lax.broadcasted_iota(jnp.int32, (c, c), 0)
    jj = lax.broadcasted_iota(jnp.int32, (c, c), 1)
    lower = (ii >= jj)
    lower_f = lower.astype(F32)
    strict = (ii > jj)
    rowmask = None
    if valid < c:
        rowmask = lax.broadcasted_iota(jnp.int32, (c, 1), 0) < valid

    def hs(h, w):
        return slice(h * w, (h + 1) * w)

    n_chunks = tb_rows // c
    cpi = 2 if n_chunks % 2 == 0 else 1
    p1_units = [(g, h) for g in range(nb * cpi) for h in range(n_heads)]

    def phase1(ci, carry):
        rows, b_ts, gc_ts, gc_tts = [], [], [], []
        for g in range(nb * cpi):
            rr = pl.ds(_chunk_rows(g // cpi, tb_rows, ci * cpi + g % cpi, c), c)
            sm = sm_ref[rr, :]
            g_t = -jnp.exp(alog_ref[...]) * jax.nn.softplus(sm + dtb_ref[...])
            b_t = jax.nn.sigmoid(sm)
            if rowmask is not None:
                g_t = jnp.where(rowmask, g_t, 0.0)
                b_t = jnp.where(rowmask, b_t, 0.0)
            gc_t = _dot_hi(lower_f, g_t)
            gcs[rr, :] = gc_t
            rows.append(rr)
            b_ts.append(b_t)
            gc_ts.append(gc_t)
            gc_tts.append(gc_t.T)
        qn, kn, kb, vb = {}, {}, {}, {}
        for (s, h) in p1_units:
            q = act[rows[s], hs(h, GDN_DK)]
            k = act[rows[s], slice(GDN_QK_W + h * GDN_DK, GDN_QK_W + (h + 1) * GDN_DK)]
            v = act[rows[s], slice(2 * GDN_QK_W + h * GDN_DV, 2 * GDN_QK_W + (h + 1) * GDN_DV)]
            if rowmask is not None:
                q = jnp.where(rowmask, q, 0.0)
                k = jnp.where(rowmask, k, 0.0)
                v = jnp.where(rowmask, v, 0.0)
            qn[s, h] = q * lax.rsqrt(jnp.sum(q * q, axis=-1, keepdims=True) + L2_EPS) * (GDN_DK ** -0.5)
            kn[s, h] = k * lax.rsqrt(jnp.sum(k * k, axis=-1, keepdims=True) + L2_EPS)
            beta = b_ts[s][:, SM_B + h:SM_B + h + 1]
            kb[s, h] = kn[s, h] * beta
            vb[s, h] = v * beta
        s1 = {u: _dot_nt(jnp.concatenate([kb[u], qn[u]], axis=0), kn[u]) for u in p1_units}
        mm = []
        for (s, h) in p1_units:
            gcol = gc_ts[s][:, SM_A + h:SM_A + h + 1]
            grow = gc_tts[s][SM_A + h:SM_A + h + 1, :]
            dec = jnp.exp(jnp.where(lower, gcol - grow, -jnp.inf))
            mm.append(jnp.where(strict, s1[s, h][:c] * dec, 0.0))
            aqs[h, rows[s], :] = s1[s, h][c:] * dec
        tms = _tri_inv_all(mm, c, ii, jj)
        for (s, h), tm in zip(p1_units, tms):
            gcol = gc_ts[s][:, SM_A + h:SM_A + h + 1]
            eg = jnp.exp(gcol)
            uw = _dot(tm, jnp.concatenate([vb[s, h], kb[s, h] * eg], axis=1))
            us[rows[s], hs(h, GDN_DV)] = uw[:, :GDN_DV]
            wss[rows[s], hs(h, GDN_DV)] = uw[:, GDN_DV:]
            qgs[rows[s], hs(h, GDN_DK)] = qn[s, h] * eg
            kds[rows[s], hs(h, GDN_DK)] = kn[s, h] * jnp.exp(gcol[c - 1:c, :] - gcol)
        return carry

    def phase2(ci, carry):
        r0 = [_chunk_rows(s, tb_rows, ci, c) for s in range(nb)]
        rows = [pl.ds(r, c) for r in r0]
        ws = {(s, h): _dot(jnp.concatenate([wss[rows[s], hs(h, GDN_DV)], qgs[rows[s], hs(h, GDN_DK)]], axis=0),
                           st[s, h]) for (s, h) in units}
        v_new = {(s, h): us[rows[s], hs(h, GDN_DV)] - ws[s, h][:c] for (s, h) in units}
        o = {(s, h): ws[s, h][c:] + _dot(aqs[h, rows[s], :], v_new[s, h]) for (s, h) in units}
        upd = {(s, h): _dot_tn(kds[rows[s], hs(h, GDN_DK)], v_new[s, h]) for (s, h) in units}
        for (s, h) in units:
            g_last = gcs[pl.ds(r0[s] + c - 1, 1), SM_A + h:SM_A + h + 1]
            st[s, h] = st[s, h] * jnp.exp(g_last) + upd[s, h]
        for s in range(nb):
            o_ref[rows[s], :] = jnp.concatenate(
                [_gated_norm(o[s, h], nw_ref[...], z_ref[rows[s], hs(h, GDN_DV)]) for h in range(n_heads)], axis=1)
        return carry

    _for_chunks(n_chunks // cpi, phase1)
    _for_chunks(n_chunks, phase2)

    @pl.when(tb == n_tb - 1)
    def _():
        sout_ref[...] = st[...]
        last = tb_rows if valid == c else valid
        for s in range(nb):
            cout_ref[s] = xc[s, pad + last - tail:pad + last, :]


def _gdn(proj, n_seq, nb, t_len, tb_rows, chunk, valid, conv_buf, s0, conv_w, alog, dtb, nw):
    n_tb = t_len // tb_rows
    assert nb == 1 or n_tb == 1
    rows = nb * tb_rows

    def rowblk(b, t):
        return b * n_tb + t

    body = functools.partial(_gdn_body, nb=nb, tb_rows=tb_rows, chunk=chunk, valid=valid, n_tb=n_tb)
    return pl.pallas_call(
        body,
        grid=(n_seq // nb, n_tb),
        in_specs=[
            pl.BlockSpec((rows, GDN_CONV_CH), lambda b, t: (rowblk(b, t), COL_QKV // GDN_CONV_CH)),
            pl.BlockSpec((rows, GDN_V_W), lambda b, t: (rowblk(b, t), COL_Z // GDN_V_W)),
            pl.BlockSpec((rows, SM_W), lambda b, t: (rowblk(b, t), COL_SM // SM_W)),
            pl.BlockSpec((nb, CONV_WIDTH - 1, GDN_CONV_CH), lambda b, t: (b, 0, 0)),
            pl.BlockSpec((nb, GDN_HEADS, GDN_DK, GDN_DV), lambda b, t: (b, 0, 0, 0)),
            pl.BlockSpec((CONV_WIDTH, GDN_CONV_CH), lambda b, t: (0, 0)),
            pl.BlockSpec((1, SM_W), lambda b, t: (0, 0)),
            pl.BlockSpec((1, SM_W), lambda b, t: (0, 0)),
            pl.BlockSpec((1, GDN_DV), lambda b, t: (0, 0)),
        ],
        out_specs=[
            pl.BlockSpec((rows, GDN_V_W), lambda b, t: (rowblk(b, t), 0)),
            pl.BlockSpec((nb, GDN_HEADS, GDN_DK, GDN_DV), lambda b, t: (b, 0, 0, 0)),
            pl.BlockSpec((nb, CONV_WIDTH - 1, GDN_CONV_CH), lambda b, t: (b, 0, 0)),
        ],
        out_shape=[
            jax.ShapeDtypeStruct((n_seq * t_len, GDN_V_W), F32),
            jax.ShapeDtypeStruct((n_seq, GDN_HEADS, GDN_DK, GDN_DV), F32),
            jax.ShapeDtypeStruct((n_seq, CONV_WIDTH - 1, GDN_CONV_CH), F32),
        ],
        scratch_shapes=[
            pltpu.VMEM((nb, GDN_HEADS, GDN_DK, GDN_DV), F32),
            pltpu.VMEM((nb, tb_rows + SUBLANE, GDN_CONV_CH), F32),
            pltpu.VMEM((rows, GDN_CONV_CH), F32),
            pltpu.VMEM((rows, SM_W), F32),
            pltpu.VMEM((rows, GDN_V_W), F32),
            pltpu.VMEM((rows, GDN_V_W), F32),
            pltpu.VMEM((rows, GDN_QK_W), F32),
            pltpu.VMEM((rows, GDN_QK_W), F32),
            pltpu.VMEM((GDN_HEADS, rows, chunk), F32),
        ],
        compiler_params=pltpu.CompilerParams(dimension_semantics=("arbitrary", "arbitrary"),
                                             vmem_limit_bytes=VMEM_LIMIT),
        name="gdn_mixer",
    )(proj, proj, proj, conv_buf, s0, conv_w, alog, dtb, nw)


def _gla_body(q_ref, k_ref, v_ref, go_ref, sm_ref, s0_ref, wgk_ref, bgk_ref, nw_ref,
              o_ref, sout_ref, st, *, nb, tb_rows, chunk, valid, n_tb):
    tb = pl.program_id(1)
    c = chunk
    n_heads = GLA_HEADS
    units = [(s, h) for s in range(nb) for h in range(n_heads)]

    @pl.when(tb == 0)
    def _():
        st[...] = s0_ref[...]

    ii = lax.broadcasted_iota(jnp.int32, (c, c), 0)
    jj = lax.broadcasted_iota(jnp.int32, (c, c), 1)
    lower = (ii >= jj)
    lower_f = lower.astype(F32)
    rid = lax.broadcasted_iota(jnp.int32, (c, 1), 0)
    rowmask = (rid < valid) if valid < c else None
    n_sub = max(c // 16, 1)
    sub = c // n_sub

    def chunk_step(ci, carry):
        rows, bcs, bc_ts = [], [], []
        for s in range(nb):
            rr = pl.ds(_chunk_rows(s, tb_rows, ci, c), c)
            gk = jax.nn.log_sigmoid(_dot(sm_ref[rr, :], wgk_ref[...]) + bgk_ref[...]) / GLA_GATE_NORMALIZER
            if rowmask is not None:
                gk = jnp.where(rowmask, gk, 0.0)
            bc = _dot_hi(lower_f, gk)
            rows.append(rr)
            bcs.append(bc)
            bc_ts.append(bc.T)
        q, k, v, bch = {}, {}, {}, {}
        for (s, h) in units:
            ks = slice(h * GLA_DK, (h + 1) * GLA_DK)
            vs = slice(h * GLA_DV, (h + 1) * GLA_DV)
            q[s, h] = q_ref[rows[s], ks] * (GLA_DK ** -0.5)
            kk = k_ref[rows[s], ks]
            vv = v_ref[rows[s], vs]
            if rowmask is not None:
                kk = jnp.where(rowmask, kk, 0.0)
                vv = jnp.where(rowmask, vv, 0.0)
            k[s, h], v[s, h] = kk, vv
            bch[s, h] = bcs[s][:, ks]
        o_inter = {u: _dot(q[u] * jnp.exp(bch[u]), st[u[0], u[1]]) for u in units}
        a = {}
        for u in units:
            q_parts, k_parts = [], []
            for sb in range(n_sub):
                ref_row = bch[u][sb * sub:sb * sub + 1, :]
                in_blk = (rid >= sb * sub) & (rid < (sb + 1) * sub)
                q_parts.append(jnp.where(in_blk, q[u] * jnp.exp(jnp.where(in_blk, bch[u] - ref_row, 0.0)), 0.0))
                k_parts.append(k[u] * jnp.exp(jnp.where(rid < (sb + 1) * sub, ref_row - bch[u], 0.0)))
            q_hat = jnp.concatenate(q_parts, axis=1) if n_sub > 1 else q_parts[0]
            k_hat = jnp.concatenate(k_parts, axis=1) if n_sub > 1 else k_parts[0]
            a[u] = jnp.where(lower, _dot_nt(q_hat, k_hat), 0.0)
        upd = {u: _dot_tn(k[u] * jnp.exp(bch[u][c - 1:c, :] - bch[u]), v[u]) for u in units}
        o = {u: o_inter[u] + _dot(a[u], v[u]) for u in units}
        for (s, h) in units:
            dec_col = bc_ts[s][h * GLA_DK:(h + 1) * GLA_DK, c - 1:c]
            st[s, h] = jnp.exp(dec_col) * st[s, h] + upd[s, h]
        for s in range(nb):
            o_ref[rows[s], :] = jnp.concatenate(
                [_gated_norm(o[s, h], nw_ref[...], go_ref[rows[s], h * GLA_DV:(h + 1) * GLA_DV])
                 for h in range(n_heads)], axis=1)
        return carry

    _for_chunks(tb_rows // c, chunk_step)

    @pl.when(tb == n_tb - 1)
    def _():
        sout_ref[...] = st[...]


def _gla(proj, n_seq, nb, t_len, tb_rows, chunk, valid, s0, wgk, bgk, nw):
    n_tb = t_len // tb_rows
    assert nb == 1 or n_tb == 1
    rows = nb * tb_rows

    def rowblk(b, t):
        return b * n_tb + t

    body = functools.partial(_gla_body, nb=nb, tb_rows=tb_rows, chunk=chunk, valid=valid, n_tb=n_tb)
    return pl.pallas_call(
        body,
        grid=(n_seq // nb, n_tb),
        in_specs=[
            pl.BlockSpec((rows, GLA_QK_W), lambda b, t: (rowblk(b, t), COL_GQ // GLA_QK_W)),
            pl.BlockSpec((rows, GLA_QK_W), lambda b, t: (rowblk(b, t), COL_GK // GLA_QK_W)),
            pl.BlockSpec((rows, GLA_V_W), lambda b, t: (rowblk(b, t), COL_GV // GLA_V_W)),
            pl.BlockSpec((rows, GLA_V_W), lambda b, t: (rowblk(b, t), COL_GG // GLA_V_W)),
            pl.BlockSpec((rows, SM_W), lambda b, t: (rowblk(b, t), COL_SM // SM_W)),
            pl.BlockSpec((nb, GLA_HEADS, GLA_DK, GLA_DV), lambda b, t: (b, 0, 0, 0)),
            pl.BlockSpec((SM_W, GLA_QK_W), lambda b, t: (0, 0)),
            pl.BlockSpec((1, GLA_QK_W), lambda b, t: (0, 0)),
            pl.BlockSpec((1, GLA_DV), lambda b, t: (0, 0)),
        ],
        out_specs=[
            pl.BlockSpec((rows, GLA_V_W), lambda b, t: (rowblk(b, t), 0)),
            pl.BlockSpec((nb, GLA_HEADS, GLA_DK, GLA_DV), lambda b, t: (b, 0, 0, 0)),
        ],
        out_shape=[
            jax.ShapeDtypeStruct((n_seq * t_len, GLA_V_W), F32),
            jax.ShapeDtypeStruct((n_seq, GLA_HEADS, GLA_DK, GLA_DV), F32),
        ],
        scratch_shapes=[pltpu.VMEM((nb, GLA_HEADS, GLA_DK, GLA_DV), F32)],
        compiler_params=pltpu.CompilerParams(dimension_semantics=("arbitrary", "arbitrary"),
                                             vmem_limit_bytes=VMEM_LIMIT),
        name="gla_mixer",
    )(proj, proj, proj, proj, proj, s0, wgk, bgk, nw)


def _outproj_body(ogp_ref, ogs_ref, olp_ref, ols_ref, xp_ref, xs_ref, wo_ref, g_ref, wr_ref, br_ref,
                  x1_ref, h2_ref, rt_ref, cnt_ref, base, *, n_p_blocks):
    i = pl.program_id(0)

    @pl.when(i == 0)
    def _():
        base[...] = jnp.zeros_like(base)

    o = jnp.concatenate([_group_pick(i, n_p_blocks, ogp_ref, ogs_ref),
                         _group_pick(i, n_p_blocks, olp_ref, ols_ref)], axis=1)
    x1 = _group_pick(i, n_p_blocks, xp_ref, xs_ref) + jnp.dot(o.astype(BF16), wo_ref[...],
                                                               preferred_element_type=F32)
    x1_ref[...] = x1
    h = _rms(x1, g_ref[...])
    _store_token_tiles(h2_ref, h)
    logits = _dot_3pass(h, wr_ref[...]) + br_ref[...]

    tm = logits.shape[0]
    lane = lax.broadcasted_iota(jnp.int32, (tm, LANE), 1)
    work = logits
    sel = jnp.zeros((tm, LANE), F32)
    ids, vals = [], []
    for _ in range(TOP_K):
        m = jnp.max(work, axis=-1, keepdims=True)
        idx = jnp.min(jnp.where(work == m, lane, LANE), axis=-1, keepdims=True)
        hit = lane == idx
        ids.append(idx)
        vals.append(m)
        work = jnp.where(hit, -jnp.inf, work)
        sel = sel + hit.astype(F32)
    exps = [jnp.exp(v - vals[0]) for v in vals]
    den = exps[0]
    for e in exps[1:]:
        den = den + e
    gates = [e / den for e in exps]

    ri = lax.broadcasted_iota(jnp.int32, (tm, tm), 0)
    ci = lax.broadcasted_iota(jnp.int32, (tm, tm), 1)
    before = _dot((ri > ci).astype(F32), sel) + base[...]
    ranks = [jnp.sum(jnp.where(lane == idx, before, 0.0), axis=-1, keepdims=True) for idx in ids]
    base[...] = base[...] + jnp.sum(sel, axis=0, keepdims=True)
    cnt_ref[...] = base[...]

    rec = jnp.zeros((tm, LANE), F32)
    for k in range(TOP_K):
        rec = jnp.where(lane == k, ids[k].astype(F32), rec)
        rec = jnp.where(lane == TOP_K + k, ranks[k], rec)
        rec = jnp.where(lane == 2 * TOP_K + k, gates[k], rec)
    rt_ref[...] = rec


def _outproj(og_p, og_s, ol_p, ol_s, x_p, x_s, wo, g, wr, br):
    n_p_blocks, n_s_blocks = x_p.shape[0] // ROW_TILE, x_s.shape[0] // ROW_TILE
    n = x_p.shape[0] + x_s.shape[0]
    return pl.pallas_call(
        functools.partial(_outproj_body, n_p_blocks=n_p_blocks),
        grid=(n_p_blocks + n_s_blocks,),
        in_specs=_group_specs(ROW_TILE, GDN_V_W, n_p_blocks) + _group_specs(ROW_TILE, GLA_V_W, n_p_blocks)
        + _group_specs(ROW_TILE, D_MODEL, n_p_blocks) + [
            pl.BlockSpec((D_MODEL, D_MODEL), lambda i: (0, 0)),
            pl.BlockSpec((1, D_MODEL), lambda i: (0, 0)),
            pl.BlockSpec((D_MODEL, LANE), lambda i: (0, 0)),
            pl.BlockSpec((1, LANE), lambda i: (0, 0)),
        ],
        out_specs=[
            pl.BlockSpec((ROW_TILE, D_MODEL), lambda i: (i, 0)),
            pl.BlockSpec((ROW_TILE * TOK_TILES, LANE), lambda i: (i, 0)),
            pl.BlockSpec((ROW_TILE, LANE), lambda i: (i, 0)),
            pl.BlockSpec((1, LANE), lambda i: (0, 0)),
        ],
        out_shape=[
            jax.ShapeDtypeStruct((n, D_MODEL), F32),
            jax.ShapeDtypeStruct((n * TOK_TILES, LANE), F32),
            jax.ShapeDtypeStruct((n, LANE), F32),
            jax.ShapeDtypeStruct((1, LANE), F32),
        ],
        scratch_shapes=[pltpu.VMEM((1, LANE), F32)],
        compiler_params=pltpu.CompilerParams(dimension_semantics=("arbitrary",),
                                             vmem_limit_bytes=VMEM_LIMIT),
        name="out_proj",
    )(og_p, og_s, ol_p, ol_s, x_p, x_s, wo, g, wr, br)


def _store_token_tiles(ref2d, val):
    rows = val.shape[0]
    for c in range(TOK_TILES):
        ref2d[pl.ds(c, rows, stride=TOK_TILES), :] = val[:, c * LANE:(c + 1) * LANE]


def _load_token_tiles(ref2d, first_row, rows):
    return jnp.concatenate(
        [ref2d[pl.ds(first_row * TOK_TILES + c, rows, stride=TOK_TILES), :] for c in range(TOK_TILES)], axis=1)


def _gather_rows(src_tiles, idx_ref, n_rows, dst2d, sem, priorities):
    def issue(j, carry):
        for u in range(DMA_ISSUE_UNROLL):
            r = j * DMA_ISSUE_UNROLL + u
            dst = dst2d.at[pl.ds(pl.multiple_of(r * TOK_TILES, TOK_TILES), TOK_TILES), :]
            pltpu.make_async_copy(src_tiles.at[idx_ref[0, r]], dst, sem).start(
                priority=priorities[u % len(priorities)])
        return carry

    lax.fori_loop(0, n_rows // DMA_ISSUE_UNROLL, issue, 0)


def _wait_rows(src2d, n_rows, dst2d, sem):
    pltpu.make_async_copy(src2d.at[pl.ds(0, n_rows * TOK_TILES), :], dst2d, sem).wait()


def _expert_body(be_ref, nu_ref, idx_ref, idx_next_ref, h_tiles, h_2d, wup_ref, bup_ref, wdn_ref, bdn_ref, y_ref,
                 xbuf, sems):
    i = pl.program_id(0)
    n_used = nu_ref[0]
    slot = i % 2

    @pl.when((i == 0) & (n_used > 0))
    def _():
        _gather_rows(h_tiles, idx_ref, EXPERT_ROWS, xbuf.at[0], sems.at[0], EXPERT_GATHER_QUEUES)

    @pl.when(i < n_used)
    def _():
        _wait_rows(h_2d, EXPERT_ROWS, xbuf.at[slot], sems.at[slot])

        @pl.when(i + 1 < n_used)
        def _():
            _gather_rows(h_tiles, idx_next_ref, EXPERT_ROWS, xbuf.at[1 - slot], sems.at[1 - slot],
                         EXPERT_GATHER_QUEUES)

        gu = _dot(_load_token_tiles(xbuf.at[slot], 0, EXPERT_ROWS), wup_ref[...]) + bup_ref[...]
        gate = jnp.minimum(gu[:, :D_FF], SWIGLU_LIMIT)
        up = jnp.clip(gu[:, D_FF:], -SWIGLU_LIMIT, SWIGLU_LIMIT)
        a = (up + 1.0) * gate * jax.nn.sigmoid(SWIGLU_ALPHA * gate)
        _store_token_tiles(y_ref, _dot(a, wdn_ref[...]) + bdn_ref[...])

    @pl.when(i >= n_used)
    def _():
        y_ref[...] = jnp.zeros_like(y_ref)


def _experts(block_e, n_used, src_tok, h2_2d, w_up, b_up, w_down, b_down):
    n_blocks = block_e.shape[0]
    src_blocks = src_tok.reshape(n_blocks, 1, EXPERT_ROWS)
    grid_spec = pltpu.PrefetchScalarGridSpec(
        num_scalar_prefetch=2,
        grid=(n_blocks,),
        in_specs=[
            pl.BlockSpec((None, 1, EXPERT_ROWS), lambda i, be, nu: (i, 0, 0), memory_space=pltpu.SMEM),
            pl.BlockSpec((None, 1, EXPERT_ROWS), lambda i, be, nu: (jnp.minimum(i + 1, n_blocks - 1), 0, 0),
                         memory_space=pltpu.SMEM),
            pl.BlockSpec(memory_space=pl.ANY),
            pl.BlockSpec(memory_space=pl.ANY),
            pl.BlockSpec((None, D_MODEL, 2 * D_FF), lambda i, be, nu: (be[i], 0, 0)),
            pl.BlockSpec((None, 1, 2 * D_FF), lambda i, be, nu: (be[i], 0, 0)),
            pl.BlockSpec((None, D_FF, D_MODEL), lambda i, be, nu: (be[i], 0, 0)),
            pl.BlockSpec((None, 1, D_MODEL), lambda i, be, nu: (be[i], 0, 0)),
        ],
        out_specs=pl.BlockSpec((EXPERT_ROWS * TOK_TILES, LANE), lambda i, be, nu: (i, 0)),
        scratch_shapes=[pltpu.VMEM((2, EXPERT_ROWS * TOK_TILES, LANE), F32), pltpu.SemaphoreType.DMA((2,))],
    )
    return pl.pallas_call(
        _expert_body,
        grid_spec=grid_spec,
        out_shape=jax.ShapeDtypeStruct((n_blocks * EXPERT_ROWS * TOK_TILES, LANE), F32),
        compiler_params=pltpu.CompilerParams(dimension_semantics=("arbitrary",),
                                             vmem_limit_bytes=VMEM_LIMIT),
        name="experts",
    )(block_e, n_used, src_blocks, src_blocks, h2_2d.reshape(-1, TOK_TILES, LANE), h2_2d, w_up,
      b_up.reshape(N_EXPERTS, 1, 2 * D_FF), w_down, b_down.reshape(N_EXPERTS, 1, D_MODEL))


def _combine_body(dest_ref, dest_next_ref, rt_ref, y_tiles, y_2d, x1_ref, g_ref, op_ref, os_ref, ybuf, sems,
                  *, n_p_blocks):
    i = pl.program_id(0)
    slot = i % 2
    n_rows = TOP_K * COMBINE_ROWS

    @pl.when(i == 0)
    def _():
        _gather_rows(y_tiles, dest_ref, n_rows, ybuf.at[0], sems.at[0], COMBINE_GATHER_QUEUES)

    _wait_rows(y_2d, n_rows, ybuf.at[slot], sems.at[slot])

    @pl.when(i + 1 < pl.num_programs(0))
    def _():
        _gather_rows(y_tiles, dest_next_ref, n_rows, ybuf.at[1 - slot], sems.at[1 - slot], COMBINE_GATHER_QUEUES)

    buf = ybuf.at[slot]
    moe = _load_token_tiles(buf, 0, COMBINE_ROWS) * rt_ref[:, 2 * TOP_K:2 * TOP_K + 1]
    for k in range(1, TOP_K):
        moe = moe + _load_token_tiles(buf, k * COMBINE_ROWS, COMBINE_ROWS) * rt_ref[:, 2 * TOP_K + k:2 * TOP_K + k + 1]
    res = _rms(x1_ref[...] + moe, g_ref[...])

    @pl.when(i < n_p_blocks)
    def _():
        op_ref[...] = res

    @pl.when(i >= n_p_blocks)
    def _():
        os_ref[...] = res


def _combine(dest, rt, y_2d, x1, g, n_p):
    n = x1.shape[0]
    n_blk = n // COMBINE_ROWS
    n_p_blocks = n_p // COMBINE_ROWS
    dest_blocks = dest.reshape(n_blk, COMBINE_ROWS, TOP_K).transpose(0, 2, 1).reshape(n_blk, 1, TOP_K * COMBINE_ROWS)
    return pl.pallas_call(
        functools.partial(_combine_body, n_p_blocks=n_p_blocks),
        grid=(n_blk,),
        in_specs=[
            pl.BlockSpec((None, 1, COMBINE_ROWS * TOP_K), lambda i: (i, 0, 0), memory_space=pltpu.SMEM),
            pl.BlockSpec((None, 1, COMBINE_ROWS * TOP_K), lambda i: (jnp.minimum(i + 1, n_blk - 1), 0, 0),
                         memory_space=pltpu.SMEM),
            pl.BlockSpec((COMBINE_ROWS, LANE), lambda i: (i, 0)),
            pl.BlockSpec(memory_space=pl.ANY),
            pl.BlockSpec(memory_space=pl.ANY),
            pl.BlockSpec((COMBINE_ROWS, D_MODEL), lambda i: (i, 0)),
            pl.BlockSpec((1, D_MODEL), lambda i: (0, 0)),
        ],
        out_specs=_group_specs(COMBINE_ROWS, D_MODEL, n_p_blocks),
        out_shape=[jax.ShapeDtypeStruct((n_p, D_MODEL), F32), jax.ShapeDtypeStruct((n - n_p, D_MODEL), F32)],
        scratch_shapes=[pltpu.VMEM((2, TOP_K * COMBINE_ROWS * TOK_TILES, LANE), F32),
                        pltpu.SemaphoreType.DMA((2,))],
        compiler_params=pltpu.CompilerParams(dimension_semantics=("arbitrary",),
                                             vmem_limit_bytes=VMEM_LIMIT),
        name="combine",
    )(dest_blocks, dest_blocks, rt, y_2d.reshape(-1, TOK_TILES, LANE), y_2d, x1, g)


def _invert_body(dest_ref, tbl_ref, zbuf, sem):
    i = pl.program_id(0)

    @pl.when(i == 0)
    def _():
        zbuf[...] = jnp.zeros_like(zbuf)
        fill = pltpu.make_async_copy(zbuf, tbl_ref, sem)
        fill.start()
        fill.wait()

    def body(j, carry):
        for u in range(DMA_ISSUE_UNROLL):
            a = j * DMA_ISSUE_UNROLL + u
            tbl_ref[dest_ref[0, a]] = lax.shift_right_logical(i * INVERT_BLOCK + a, _log2(TOP_K))
        return carry

    lax.fori_loop(0, INVERT_BLOCK // DMA_ISSUE_UNROLL, body, 0)


def _invert(dest, n_rows):
    n_assign = dest.shape[0]
    n_blk = n_assign // INVERT_BLOCK
    return pl.pallas_call(
        _invert_body,
        grid=(n_blk,),
        in_specs=[pl.BlockSpec((None, 1, INVERT_BLOCK), lambda i: (i, 0, 0), memory_space=pltpu.SMEM)],
        out_specs=pl.BlockSpec(memory_space=pltpu.SMEM),
        out_shape=jax.ShapeDtypeStruct((n_rows,), jnp.int32),
        scratch_shapes=[pltpu.VMEM((n_rows,), jnp.int32), pltpu.SemaphoreType.DMA(())],
        compiler_params=pltpu.CompilerParams(dimension_semantics=("arbitrary",)),
        name="invert",
    )(dest.reshape(n_blk, 1, INVERT_BLOCK))


def _route(rt, cnt):
    n = rt.shape[0]
    n_assign = n * TOP_K
    flat_e = rt[:, 0:TOP_K].astype(jnp.int32).reshape(-1)
    rank = rt[:, TOP_K:2 * TOP_K].astype(jnp.int32).reshape(-1)
    counts = cnt[0, :N_EXPERTS].astype(jnp.int32)
    padded = (counts + EXPERT_ROWS - 1) // EXPERT_ROWS * EXPERT_ROWS
    pend = jnp.cumsum(padded)
    pstart = pend - padded
    dest = (pstart[flat_e] + rank).astype(jnp.int32)
    n_rows = n_assign + N_EXPERTS * EXPERT_ROWS
    n_blocks = n_rows // EXPERT_ROWS
    src_tok = _invert(dest, n_rows)
    block_first = jnp.arange(n_blocks, dtype=jnp.int32) * EXPERT_ROWS
    block_e = jnp.minimum(jnp.sum((pend[None, :] <= block_first[:, None]).astype(jnp.int32), axis=1),
                          N_EXPERTS - 1).astype(jnp.int32)
    n_used = (pend[-1] // EXPERT_ROWS).astype(jnp.int32).reshape(1)
    return dest, src_tok, block_e, n_used


def _pad_lanes(v, width):
    return jnp.zeros((1, width), F32).at[0, :v.shape[0]].set(v.astype(F32))


def kernel(x_prompt, x_sample, state_gdn_conv, state_gdn, state_gla, rms_mix_w, w_in, conv_w, gdn_a_log,
           gdn_dt_bias, gdn_norm_w, gla_gk_w, gla_gk_b, gla_norm_w, w_out, rms_ffn_w, w_router, b_router,
           w_up, b_up, w_down, b_down, rms_final_w):
    bp, tp, d = x_prompt.shape
    bs, ts, _ = x_sample.shape
    n_p, n_s = bp * tp, bs * ts
    assert d == D_MODEL and state_gdn.shape[0] == 1, "single-layer kernel"
    assert tp >= CONV_WIDTH - 1 and ts >= CONV_WIDTH - 1, "new conv state is taken from the new tokens only"
    l = 0

    wi = w_in[l]
    a0 = GDN_CONV_CH + GDN_V_W
    g0 = a0 + 2 * GDN_HEADS
    lr0 = g0 + 2 * GLA_QK_W + 2 * GLA_V_W
    small = jnp.concatenate([wi[:, a0:a0 + 2 * GDN_HEADS], wi[:, lr0:lr0 + GLA_GATE_RANK],
                             jnp.zeros((d, SM_W - 2 * GDN_HEADS - GLA_GATE_RANK), F32)], axis=1)
    w_big = jnp.concatenate([wi[:, :a0], wi[:, g0:lr0], small], axis=1).astype(BF16)
    alog = _pad_lanes(gdn_a_log[l], SM_W)
    dtb = _pad_lanes(gdn_dt_bias[l], SM_W)
    wgk = jnp.zeros((SM_W, GLA_QK_W), F32).at[SM_LR:SM_LR + GLA_GATE_RANK].set(gla_gk_w[l])
    wr = jnp.zeros((d, LANE), F32).at[:, :N_EXPERTS].set(w_router[l])
    br = jnp.full((1, LANE), -1e30, F32).at[0, :N_EXPERTS].set(b_router[l])

    assert n_p % ROW_TILE == 0 and n_s % ROW_TILE == 0
    x_p, x_s = x_prompt.reshape(n_p, d), x_sample.reshape(n_s, d)
    proj = _inproj(x_p, x_s, rms_mix_w[l][None, :], w_big)

    tb_p = PROMPT_TIME_BLOCK
    zeros_conv = jnp.zeros((bp, CONV_WIDTH - 1, GDN_CONV_CH), F32)
    og_p, gdn_p, conv_p = _gdn(proj, bp, 1, tp, tb_p, CHUNK, CHUNK, zeros_conv,
                               jnp.zeros((bp, GDN_HEADS, GDN_DK, GDN_DV), F32), conv_w[l], alog, dtb,
                               gdn_norm_w[l][None, :])
    ol_p, gla_p = _gla(proj, bp, 1, tp, tb_p, CHUNK, CHUNK, jnp.zeros((bp, GLA_HEADS, GLA_DK, GLA_DV), F32),
                       wgk, gla_gk_b[l][None, :], gla_norm_w[l][None, :])

    ts_pad = SUBLANE
    nb_s = SAMPLE_SEQS_PER_STEP
    proj_s = proj[n_p:].reshape(bs, ts, PROJ_W)
    proj_sp = jnp.pad(proj_s, ((0, 0), (0, ts_pad - ts), (0, 0))).reshape(bs * ts_pad, PROJ_W)
    og_s, gdn_s, conv_s = _gdn(proj_sp, bs, nb_s, ts_pad, ts_pad, ts_pad, ts, state_gdn_conv[l], state_gdn[l],
                               conv_w[l], alog, dtb, gdn_norm_w[l][None, :])
    ol_s, gla_s = _gla(proj_sp, bs, nb_s, ts_pad, ts_pad, ts_pad, ts, state_gla[l], wgk, gla_gk_b[l][None, :],
                       gla_norm_w[l][None, :])
    og_s = og_s.reshape(bs, ts_pad, GDN_V_W)[:, :ts].reshape(n_s, GDN_V_W)
    ol_s = ol_s.reshape(bs, ts_pad, GLA_V_W)[:, :ts].reshape(n_s, GLA_V_W)

    x1, h2, rt, cnt = _outproj(og_p, og_s, ol_p, ol_s, x_p, x_s, w_out[l].astype(BF16), rms_ffn_w[l][None, :],
                               wr, br)

    dest, src_tok, block_e, n_used = _route(rt, cnt)
    y_rows = _experts(block_e, n_used, src_tok, h2, w_up[l], b_up[l], w_down[l], b_down[l])
    y_p, y_s = _combine(dest, rt, y_rows, x1, rms_final_w[None, :], n_p)
    y_prompt = y_p.reshape(bp, tp, d)
    y_sample = y_s.reshape(bs, ts, d)
    return (y_prompt, y_sample, conv_p[None], gdn_p[None], gla_p[None], conv_s[None], gdn_s[None], gla_s[None])
```

```python
import functools

import jax
import jax.numpy as jnp
from jax import lax
from jax.experimental import pallas as pl
from jax.experimental.pallas import tpu as pltpu

F32 = jnp.float32
BF16 = jnp.bfloat16
HI = lax.Precision.HIGHEST

D_MODEL = 1024
GDN_HEADS = 4
GDN_DK = 128
GDN_DV = 128
GLA_HEADS = 4
GLA_DK = 64
GLA_DV = 128
GLA_GATE_RANK = 16
GLA_GATE_NORMALIZER = 16.0
CONV_WIDTH = 4
CHUNK = 64
N_EXPERTS = 32
TOP_K = 4
D_FF = 1024
SWIGLU_LIMIT = 7.0
SWIGLU_ALPHA = 1.702
RMS_EPS = 1e-6
L2_EPS = 1e-6

GDN_QK_W = GDN_HEADS * GDN_DK
GDN_V_W = GDN_HEADS * GDN_DV
GDN_CONV_CH = 2 * GDN_QK_W + GDN_V_W
GLA_QK_W = GLA_HEADS * GLA_DK
GLA_V_W = GLA_HEADS * GLA_DV

COL_QKV = 0
COL_Z = 1536
COL_GQ = 2048
COL_GK = 2304
COL_GV = 2560
COL_GG = 3072
COL_SM = 3584
SM_W = 128
PROJ_W = COL_SM + SM_W
SM_A, SM_B, SM_LR = 0, 4, 8

LANE = 128
SUBLANE = 8
TOK_TILES = D_MODEL // LANE
ROW_TILE = 256
EXPERT_ROWS = 256
COMBINE_ROWS = 128
DMA_ISSUE_UNROLL = 8
INVERT_BLOCK = 1024
GDN_CHUNKS_PER_TRIP = 4
GLA_CHUNKS_PER_TRIP = 4
FF_CHUNKS = 4
EXPERT_GATHER_QUEUES = (0, 1)
COMBINE_GATHER_QUEUES = (0, 1)
PROMPT_TIME_BLOCK = 512
SAMPLE_SEQS_PER_STEP = 8
VMEM_LIMIT = 56 * 1024 * 1024


def _dot(a, b):
    return jnp.dot(a.astype(BF16), b.astype(BF16), preferred_element_type=F32)


def _dot_nt(a, b):
    return lax.dot_general(a.astype(BF16), b.astype(BF16), (((1,), (1,)), ((), ())),
                           preferred_element_type=F32)


def _dot_tn(a, b):
    return lax.dot_general(a.astype(BF16), b.astype(BF16), (((0,), (0,)), ((), ())),
                           preferred_element_type=F32)


def _dot_hi(a, b):
    return jnp.dot(a, b, precision=HI, preferred_element_type=F32)


def _dot_3pass(a, b):
    a_hi = a.astype(BF16)
    b_hi = b.astype(BF16)
    a_lo = (a - a_hi.astype(F32)).astype(BF16)
    b_lo = (b - b_hi.astype(F32)).astype(BF16)

    def mm(x, y):
        return jnp.dot(x, y, preferred_element_type=F32)

    return (mm(a_lo, b_hi) + mm(a_hi, b_lo)) + mm(a_hi, b_hi)


def _rms(x, w):
    return x * lax.rsqrt(jnp.mean(x * x, axis=-1, keepdims=True) + RMS_EPS) * w


def _silu(x):
    return x * jax.nn.sigmoid(x)


def _group_specs(rows, width, n_p_blocks):
    return [pl.BlockSpec((rows, width), lambda i: (jnp.minimum(i, n_p_blocks - 1), 0)),
            pl.BlockSpec((rows, width), lambda i: (jnp.maximum(i - n_p_blocks, 0), 0))]


def _group_pick(i, n_p_blocks, p_ref, s_ref):
    return jnp.where(i < n_p_blocks, p_ref[...], s_ref[...])


def _inproj_body(xp_ref, xs_ref, g_ref, w_ref, o_ref, *, n_p_blocks):
    x = _group_pick(pl.program_id(0), n_p_blocks, xp_ref, xs_ref)
    h = _rms(x, g_ref[...])
    o_ref[...] = jnp.dot(h.astype(BF16), w_ref[...], preferred_element_type=F32)


def _inproj(x_p, x_s, g, w):
    n_p_blocks, n_s_blocks = x_p.shape[0] // ROW_TILE, x_s.shape[0] // ROW_TILE
    n = x_p.shape[0] + x_s.shape[0]
    return pl.pallas_call(
        functools.partial(_inproj_body, n_p_blocks=n_p_blocks),
        grid=(n_p_blocks + n_s_blocks,),
        in_specs=_group_specs(ROW_TILE, D_MODEL, n_p_blocks) + [
            pl.BlockSpec((1, D_MODEL), lambda i: (0, 0)),
            pl.BlockSpec((D_MODEL, PROJ_W), lambda i: (0, 0)),
        ],
        out_specs=pl.BlockSpec((ROW_TILE, PROJ_W), lambda i: (i, 0)),
        out_shape=jax.ShapeDtypeStruct((n, PROJ_W), F32),
        compiler_params=pltpu.CompilerParams(dimension_semantics=("arbitrary",),
                                             vmem_limit_bytes=VMEM_LIMIT),
        name="in_proj",
    )(x_p, x_s, g, w)


def _log2(n):
    assert n & (n - 1) == 0
    return n.bit_length() - 1


def _tri_inv_all(ms, c, ii, jj):
    eye = (ii == jj).astype(F32)
    base = min(c, 8)
    sh = _log2(base)
    blk = (ii >> sh) == (jj >> sh)
    ns = [jnp.where(blk, m, 0.0) for m in ms]
    xs = [eye - n for n in ns]
    ps = [_dot(n, n) for n in ns]
    ts = [_dot(jnp.concatenate([x, p], axis=0), p) for x, p in zip(xs, ps)]
    xs = [x + t[:c] for x, t in zip(xs, ts)]
    ps = [t[c:] for t in ts]
    xs = [x + _dot(x, p) for x, p in zip(xs, ps)]
    s = base
    while s < c:
        sh_s, sh_b = _log2(s), _log2(2 * s)
        off = ((ii >> sh_b) == (jj >> sh_b)) & ((ii >> sh_s) != (jj >> sh_s))
        ys = [_dot(x, jnp.where(off, m, 0.0)) for x, m in zip(xs, ms)]
        xs = [x - _dot(y, x) for x, y in zip(xs, ys)]
        s *= 2
    return xs


def _gated_norm(o, w, z):
    return o * lax.rsqrt(jnp.mean(o * o, axis=-1, keepdims=True) + RMS_EPS) * w * _silu(z)


def _chunk_rows(s, tb_rows, ci, c):
    r = s * tb_rows + ci * c
    if not isinstance(r, int):
        r = pl.multiple_of(r, c)
    return r


def _for_chunks(n_chunks, step):
    if n_chunks == 1:
        step(0, 0)
    else:
        lax.fori_loop(0, n_chunks, step, 0)


def _gdn_body(qkv_ref, z_ref, sm_ref, cbuf_ref, s0_ref, cw_ref, alog_ref, dtb_ref, nw_ref,
              o_ref, sout_ref, cout_ref, st, xc, act, gcs, us, wss, qgs, kds, aqs,
              *, nb, tb_rows, chunk, valid, n_tb):
    tb = pl.program_id(1)
    c = chunk
    n_heads = GDN_HEADS
    tail = CONV_WIDTH - 1
    pad = SUBLANE
    units = [(s, h) for s in range(nb) for h in range(n_heads)]

    @pl.when(tb == 0)
    def _():
        st[...] = s0_ref[...]
        for s in range(nb):
            xc[s, pad - tail:pad, :] = cbuf_ref[s]

    if n_tb > 1:
        @pl.when(tb > 0)
        def _():
            for s in range(nb):
                xc[s, pad - tail:pad, :] = xc[s, tb_rows + pad - tail:tb_rows + pad, :]

    for s in range(nb):
        xc[s, pad:pad + tb_rows, :] = qkv_ref[s * tb_rows:(s + 1) * tb_rows, :]

    slab = min(tb_rows, 64)
    for s in range(nb):
        for sl in range(tb_rows // slab):
            for cb in range(GDN_CONV_CH // 512):
                cs = slice(cb * 512, (cb + 1) * 512)
                lo = pad - tail + sl * slab
                acc = xc[s, lo:lo + slab, cs] * cw_ref[0:1, cs]
                for i in range(1, CONV_WIDTH):
                    acc = acc + xc[s, lo + i:lo + i + slab, cs] * cw_ref[i:i + 1, cs]
                act[s * tb_rows + sl * slab:s * tb_rows + (sl + 1) * slab, cs] = _silu(acc)

    ii = lax.broadcasted_iota(jnp.int32, (c, c), 0)
    jj = lax.broadcasted_iota(jnp.int32, (c, c), 1)
    lower = (ii >= jj)
    lower_f = lower.astype(F32)
    strict = (ii > jj)
    rowmask = None
    if valid < c:
        rowmask = lax.broadcasted_iota(jnp.int32, (c, 1), 0) < valid

    def hs(h, w):
        return slice(h * w, (h + 1) * w)

    n_chunks = tb_rows // c
    cpi = next(k for k in (GDN_CHUNKS_PER_TRIP, 2, 1) if n_chunks % k == 0)
    p1_units = [(g, h) for g in range(nb * cpi) for h in range(n_heads)]

    def phase1(ci, carry):
        rows, b_ts, gc_ts, gc_tts = [], [], [], []
        for g in range(nb * cpi):
            rr = pl.ds(_chunk_rows(g // cpi, tb_rows, ci * cpi + g % cpi, c), c)
            sm = sm_ref[rr, :]
            g_t = -jnp.exp(alog_ref[...]) * jax.nn.softplus(sm + dtb_ref[...])
            b_t = jax.nn.sigmoid(sm)
            if rowmask is not None:
                g_t = jnp.where(rowmask, g_t, 0.0)
                b_t = jnp.where(rowmask, b_t, 0.0)
            gc_t = _dot_hi(lower_f, g_t)
            gcs[rr, :] = gc_t
            rows.append(rr)
            b_ts.append(b_t)
            gc_ts.append(gc_t)
            gc_tts.append(gc_t.T)
        qn, kn, kb, vb = {}, {}, {}, {}
        for (s, h) in p1_units:
            q = act[rows[s], hs(h, GDN_DK)]
            k = act[rows[s], slice(GDN_QK_W + h * GDN_DK, GDN_QK_W + (h + 1) * GDN_DK)]
            v = act[rows[s], slice(2 * GDN_QK_W + h * GDN_DV, 2 * GDN_QK_W + (h + 1) * GDN_DV)]
            if rowmask is not None:
                q = jnp.where(rowmask, q, 0.0)
                k = jnp.where(rowmask, k, 0.0)
                v = jnp.where(rowmask, v, 0.0)
            qn[s, h] = q * lax.rsqrt(jnp.sum(q * q, axis=-1, keepdims=True) + L2_EPS) * (GDN_DK ** -0.5)
            kn[s, h] = k * lax.rsqrt(jnp.sum(k * k, axis=-1, keepdims=True) + L2_EPS)
            beta = b_ts[s][:, SM_B + h:SM_B + h + 1]
            kb[s, h] = kn[s, h] * beta
            vb[s, h] = v * beta
        s1 = {u: _dot_nt(jnp.concatenate([kb[u], qn[u]], axis=0), kn[u]) for u in p1_units}
        mm = []
        for (s, h) in p1_units:
            gcol = gc_ts[s][:, SM_A + h:SM_A + h + 1]
            grow = gc_tts[s][SM_A + h:SM_A + h + 1, :]
            dec = jnp.exp(jnp.where(lower, gcol - grow, -jnp.inf))
            mm.append(jnp.where(strict, s1[s, h][:c] * dec, 0.0))
            aqs[h, rows[s], :] = s1[s, h][c:] * dec
        tms = _tri_inv_all(mm, c, ii, jj)
        for (s, h), tm in zip(p1_units, tms):
            gcol = gc_ts[s][:, SM_A + h:SM_A + h + 1]
            eg = jnp.exp(gcol)
            uw = _dot(tm, jnp.concatenate([vb[s, h], kb[s, h] * eg], axis=1))
            us[rows[s], hs(h, GDN_DV)] = uw[:, :GDN_DV]
            wss[rows[s], hs(h, GDN_DV)] = uw[:, GDN_DV:]
            qgs[rows[s], hs(h, GDN_DK)] = qn[s, h] * eg
            kds[rows[s], hs(h, GDN_DK)] = kn[s, h] * jnp.exp(gcol[c - 1:c, :] - gcol)
        return carry

    def phase2(ci, carry):
        r0 = [_chunk_rows(s, tb_rows, ci, c) for s in range(nb)]
        rows = [pl.ds(r, c) for r in r0]
        ws = {(s, h): _dot(jnp.concatenate([wss[rows[s], hs(h, GDN_DV)], qgs[rows[s], hs(h, GDN_DK)]], axis=0),
                           st[s, h]) for (s, h) in units}
        v_new = {(s, h): us[rows[s], hs(h, GDN_DV)] - ws[s, h][:c] for (s, h) in units}
        o = {(s, h): ws[s, h][c:] + _dot(aqs[h, rows[s], :], v_new[s, h]) for (s, h) in units}
        upd = {(s, h): _dot_tn(kds[rows[s], hs(h, GDN_DK)], v_new[s, h]) for (s, h) in units}
        for (s, h) in units:
            g_last = gcs[pl.ds(r0[s] + c - 1, 1), SM_A + h:SM_A + h + 1]
            st[s, h] = st[s, h] * jnp.exp(g_last) + upd[s, h]
        for s in range(nb):
            o_ref[rows[s], :] = jnp.concatenate(
                [_gated_norm(o[s, h], nw_ref[...], z_ref[rows[s], hs(h, GDN_DV)]) for h in range(n_heads)], axis=1)
        return carry

    _for_chunks(n_chunks // cpi, phase1)
    _for_chunks(n_chunks, phase2)

    @pl.when(tb == n_tb - 1)
    def _():
        sout_ref[...] = st[...]
        last = tb_rows if valid == c else valid
        for s in range(nb):
            cout_ref[s] = xc[s, pad + last - tail:pad + last, :]


def _gdn(proj, n_seq, nb, t_len, tb_rows, chunk, valid, conv_buf, s0, conv_w, alog, dtb, nw):
    n_tb = t_len // tb_rows
    assert nb == 1 or n_tb == 1
    rows = nb * tb_rows

    def rowblk(b, t):
        return b * n_tb + t

    body = functools.partial(_gdn_body, nb=nb, tb_rows=tb_rows, chunk=chunk, valid=valid, n_tb=n_tb)
    return pl.pallas_call(
        body,
        grid=(n_seq // nb, n_tb),
        in_specs=[
            pl.BlockSpec((rows, GDN_CONV_CH), lambda b, t: (rowblk(b, t), COL_QKV // GDN_CONV_CH)),
            pl.BlockSpec((rows, GDN_V_W), lambda b, t: (rowblk(b, t), COL_Z // GDN_V_W)),
            pl.BlockSpec((rows, SM_W), lambda b, t: (rowblk(b, t), COL_SM // SM_W)),
            pl.BlockSpec((nb, CONV_WIDTH - 1, GDN_CONV_CH), lambda b, t: (b, 0, 0)),
            pl.BlockSpec((nb, GDN_HEADS, GDN_DK, GDN_DV), lambda b, t: (b, 0, 0, 0)),
            pl.BlockSpec((CONV_WIDTH, GDN_CONV_CH), lambda b, t: (0, 0)),
            pl.BlockSpec((1, SM_W), lambda b, t: (0, 0)),
            pl.BlockSpec((1, SM_W), lambda b, t: (0, 0)),
            pl.BlockSpec((1, GDN_DV), lambda b, t: (0, 0)),
        ],
        out_specs=[
            pl.BlockSpec((rows, GDN_V_W), lambda b, t: (rowblk(b, t), 0)),
            pl.BlockSpec((nb, GDN_HEADS, GDN_DK, GDN_DV), lambda b, t: (b, 0, 0, 0)),
            pl.BlockSpec((nb, CONV_WIDTH - 1, GDN_CONV_CH), lambda b, t: (b, 0, 0)),
        ],
        out_shape=[
            jax.ShapeDtypeStruct((n_seq * t_len, GDN_V_W), F32),
            jax.ShapeDtypeStruct((n_seq, GDN_HEADS, GDN_DK, GDN_DV), F32),
            jax.ShapeDtypeStruct((n_seq, CONV_WIDTH - 1, GDN_CONV_CH), F32),
        ],
        scratch_shapes=[
            pltpu.VMEM((nb, GDN_HEADS, GDN_DK, GDN_DV), F32),
            pltpu.VMEM((nb, tb_rows + SUBLANE, GDN_CONV_CH), F32),
            pltpu.VMEM((rows, GDN_CONV_CH), F32),
            pltpu.VMEM((rows, SM_W), F32),
            pltpu.VMEM((rows, GDN_V_W), F32),
            pltpu.VMEM((rows, GDN_V_W), F32),
            pltpu.VMEM((rows, GDN_QK_W), F32),
            pltpu.VMEM((rows, GDN_QK_W), F32),
            pltpu.VMEM((GDN_HEADS, rows, chunk), F32),
        ],
        compiler_params=pltpu.CompilerParams(dimension_semantics=("arbitrary", "arbitrary"),
                                             vmem_limit_bytes=VMEM_LIMIT),
        name="gdn_mixer",
    )(proj, proj, proj, conv_buf, s0, conv_w, alog, dtb, nw)


def _gla_body(q_ref, k_ref, v_ref, go_ref, sm_ref, s0_ref, wgk_ref, bgk_ref, nw_ref,
              o_ref, sout_ref, st, qes, ois, upds, decs, *, nb, tb_rows, chunk, valid, n_tb):
    tb = pl.program_id(1)
    c = chunk
    n_heads = GLA_HEADS
    units = [(s, h) for s in range(nb) for h in range(n_heads)]

    @pl.when(tb == 0)
    def _():
        st[...] = s0_ref[...]

    ii = lax.broadcasted_iota(jnp.int32, (c, c), 0)
    jj = lax.broadcasted_iota(jnp.int32, (c, c), 1)
    lower = (ii >= jj)
    lower_f = lower.astype(F32)
    rid = lax.broadcasted_iota(jnp.int32, (c, 1), 0)
    rowmask = (rid < valid) if valid < c else None
    n_sub = max(c // 16, 1)
    sub = c // n_sub

    n_chunks = tb_rows // c
    cpi = next(k for k in (GLA_CHUNKS_PER_TRIP, 2, 1) if n_chunks % k == 0)
    p1_units = [(g, h) for g in range(nb * cpi) for h in range(n_heads)]

    def phase1(ci, carry):
        rows, slots, bcs, bc_ts = [], [], [], []
        for g in range(nb * cpi):
            chunk_idx = ci * cpi + g % cpi
            rr = pl.ds(_chunk_rows(g // cpi, tb_rows, chunk_idx, c), c)
            slots.append((g // cpi) * n_chunks + chunk_idx)
            gk = jax.nn.log_sigmoid(_dot(sm_ref[rr, :], wgk_ref[...]) + bgk_ref[...]) / GLA_GATE_NORMALIZER
            if rowmask is not None:
                gk = jnp.where(rowmask, gk, 0.0)
            bc = _dot_hi(lower_f, gk)
            rows.append(rr)
            bcs.append(bc)
            bc_ts.append(bc.T)
        q, k, v, bch = {}, {}, {}, {}
        for (s, h) in p1_units:
            ks = slice(h * GLA_DK, (h + 1) * GLA_DK)
            vs = slice(h * GLA_DV, (h + 1) * GLA_DV)
            q[s, h] = q_ref[rows[s], ks] * (GLA_DK ** -0.5)
            kk = k_ref[rows[s], ks]
            vv = v_ref[rows[s], vs]
            if rowmask is not None:
                kk = jnp.where(rowmask, kk, 0.0)
                vv = jnp.where(rowmask, vv, 0.0)
            k[s, h], v[s, h] = kk, vv
            bch[s, h] = bcs[s][:, ks]
        for (g, h) in p1_units:
            qes[h, rows[g], :] = q[g, h] * jnp.exp(bch[g, h])
        a = {}
        for u in p1_units:
            q_parts, k_parts = [], []
            for sb in range(n_sub):
                ref_row = bch[u][sb * sub:sb * sub + 1, :]
                in_blk = (rid >= sb * sub) & (rid < (sb + 1) * sub)
                q_parts.append(jnp.where(in_blk, q[u] * jnp.exp(jnp.where(in_blk, bch[u] - ref_row, 0.0)), 0.0))
                k_parts.append(k[u] * jnp.exp(jnp.where(rid < (sb + 1) * sub, ref_row - bch[u], 0.0)))
            q_hat = jnp.concatenate(q_parts, axis=1) if n_sub > 1 else q_parts[0]
            k_hat = jnp.concatenate(k_parts, axis=1) if n_sub > 1 else k_parts[0]
            a[u] = jnp.where(lower, _dot_nt(q_hat, k_hat), 0.0)
        upd = {u: _dot_tn(k[u] * jnp.exp(bch[u][c - 1:c, :] - bch[u]), v[u]) for u in p1_units}
        o_intra = {u: _dot(a[u], v[u]) for u in p1_units}
        for (g, h) in p1_units:
            dec_col = bc_ts[g][h * GLA_DK:(h + 1) * GLA_DK, c - 1:c]
            decs[slots[g], h] = jnp.broadcast_to(jnp.exp(dec_col), (GLA_DK, GLA_DV))
            upds[slots[g], h] = upd[g, h]
            ois[rows[g], h * GLA_DV:(h + 1) * GLA_DV] = o_intra[g, h]
        return carry

    def phase2(ci, carry):
        rows = [pl.ds(_chunk_rows(s, tb_rows, ci, c), c) for s in range(nb)]
        o = {(s, h): ois[rows[s], h * GLA_DV:(h + 1) * GLA_DV] + _dot(qes[h, rows[s], :], st[s, h])
             for (s, h) in units}
        for (s, h) in units:
            st[s, h] = decs[s * n_chunks + ci, h] * st[s, h] + upds[s * n_chunks + ci, h]
        for s in range(nb):
            o_ref[rows[s], :] = jnp.concatenate(
                [_gated_norm(o[s, h], nw_ref[...], go_ref[rows[s], h * GLA_DV:(h + 1) * GLA_DV])
                 for h in range(n_heads)], axis=1)
        return carry

    _for_chunks(n_chunks // cpi, phase1)
    _for_chunks(n_chunks, phase2)

    @pl.when(tb == n_tb - 1)
    def _():
        sout_ref[...] = st[...]


def _gla(proj, n_seq, nb, t_len, tb_rows, chunk, valid, s0, wgk, bgk, nw):
    n_tb = t_len // tb_rows
    assert nb == 1 or n_tb == 1
    rows = nb * tb_rows

    def rowblk(b, t):
        return b * n_tb + t

    body = functools.partial(_gla_body, nb=nb, tb_rows=tb_rows, chunk=chunk, valid=valid, n_tb=n_tb)
    return pl.pallas_call(
        body,
        grid=(n_seq // nb, n_tb),
        in_specs=[
            pl.BlockSpec((rows, GLA_QK_W), lambda b, t: (rowblk(b, t), COL_GQ // GLA_QK_W)),
            pl.BlockSpec((rows, GLA_QK_W), lambda b, t: (rowblk(b, t), COL_GK // GLA_QK_W)),
            pl.BlockSpec((rows, GLA_V_W), lambda b, t: (rowblk(b, t), COL_GV // GLA_V_W)),
            pl.BlockSpec((rows, GLA_V_W), lambda b, t: (rowblk(b, t), COL_GG // GLA_V_W)),
            pl.BlockSpec((rows, SM_W), lambda b, t: (rowblk(b, t), COL_SM // SM_W)),
            pl.BlockSpec((nb, GLA_HEADS, GLA_DK, GLA_DV), lambda b, t: (b, 0, 0, 0)),
            pl.BlockSpec((SM_W, GLA_QK_W), lambda b, t: (0, 0)),
            pl.BlockSpec((1, GLA_QK_W), lambda b, t: (0, 0)),
            pl.BlockSpec((1, GLA_DV), lambda b, t: (0, 0)),
        ],
        out_specs=[
            pl.BlockSpec((rows, GLA_V_W), lambda b, t: (rowblk(b, t), 0)),
            pl.BlockSpec((nb, GLA_HEADS, GLA_DK, GLA_DV), lambda b, t: (b, 0, 0, 0)),
        ],
        out_shape=[
            jax.ShapeDtypeStruct((n_seq * t_len, GLA_V_W), F32),
            jax.ShapeDtypeStruct((n_seq, GLA_HEADS, GLA_DK, GLA_DV), F32),
        ],
        scratch_shapes=[
            pltpu.VMEM((nb, GLA_HEADS, GLA_DK, GLA_DV), F32),
            pltpu.VMEM((GLA_HEADS, rows, GLA_DK), F32),
            pltpu.VMEM((rows, GLA_V_W), F32),
            pltpu.VMEM((rows // chunk, GLA_HEADS, GLA_DK, GLA_DV), F32),
            pltpu.VMEM((rows // chunk, GLA_HEADS, GLA_DK, GLA_DV), F32),
        ],
        compiler_params=pltpu.CompilerParams(dimension_semantics=("arbitrary", "arbitrary"),
                                             vmem_limit_bytes=VMEM_LIMIT),
        name="gla_mixer",
    )(proj, proj, proj, proj, proj, s0, wgk, bgk, nw)


def _outproj_body(ogp_ref, ogs_ref, olp_ref, ols_ref, xp_ref, xs_ref, wo_ref, g_ref, wr_ref, br_ref,
                  x1_ref, h2_ref, rt_ref, cnt_ref, base, *, n_p_blocks):
    i = pl.program_id(0)

    @pl.when(i == 0)
    def _():
        base[...] = jnp.zeros_like(base)

    o = jnp.concatenate([_group_pick(i, n_p_blocks, ogp_ref, ogs_ref),
                         _group_pick(i, n_p_blocks, olp_ref, ols_ref)], axis=1)
    x1 = _group_pick(i, n_p_blocks, xp_ref, xs_ref) + jnp.dot(o.astype(BF16), wo_ref[...],
                                                               preferred_element_type=F32)
    x1_ref[...] = x1
    h = _rms(x1, g_ref[...])
    _store_token_tiles(h2_ref, h)
    logits = _dot_3pass(h, wr_ref[...]) + br_ref[...]

    tm = logits.shape[0]
    lane = lax.broadcasted_iota(jnp.int32, (tm, LANE), 1)
    work = logits
    sel = jnp.zeros((tm, LANE), F32)
    ids, vals = [], []
    for _ in range(TOP_K):
        m = jnp.max(work, axis=-1, keepdims=True)
        idx = jnp.min(jnp.where(work == m, lane, LANE), axis=-1, keepdims=True)
        hit = lane == idx
        ids.append(idx)
        vals.append(m)
        work = jnp.where(hit, -jnp.inf, work)
        sel = sel + hit.astype(F32)
    exps = [jnp.exp(v - vals[0]) for v in vals]
    den = exps[0]
    for e in exps[1:]:
        den = den + e
    gates = [e / den for e in exps]

    ri = lax.broadcasted_iota(jnp.int32, (tm, tm), 0)
    ci = lax.broadcasted_iota(jnp.int32, (tm, tm), 1)
    before = _dot((ri > ci).astype(F32), sel) + base[...]
    ranks = [jnp.sum(jnp.where(lane == idx, before, 0.0), axis=-1, keepdims=True) for idx in ids]
    base[...] = base[...] + jnp.sum(sel, axis=0, keepdims=True)
    cnt_ref[...] = base[...]

    rec = jnp.zeros((tm, LANE), F32)
    for k in range(TOP_K):
        rec = jnp.where(lane == k, ids[k].astype(F32), rec)
        rec = jnp.where(lane == TOP_K + k, ranks[k], rec)
        rec = jnp.where(lane == 2 * TOP_K + k, gates[k], rec)
    rt_ref[...] = rec


def _outproj(og_p, og_s, ol_p, ol_s, x_p, x_s, wo, g, wr, br):
    n_p_blocks, n_s_blocks = x_p.shape[0] // ROW_TILE, x_s.shape[0] // ROW_TILE
    n = x_p.shape[0] + x_s.shape[0]
    return pl.pallas_call(
        functools.partial(_outproj_body, n_p_blocks=n_p_blocks),
        grid=(n_p_blocks + n_s_blocks,),
        in_specs=_group_specs(ROW_TILE, GDN_V_W, n_p_blocks) + _group_specs(ROW_TILE, GLA_V_W, n_p_blocks)
        + _group_specs(ROW_TILE, D_MODEL, n_p_blocks) + [
            pl.BlockSpec((D_MODEL, D_MODEL), lambda i: (0, 0)),
            pl.BlockSpec((1, D_MODEL), lambda i: (0, 0)),
            pl.BlockSpec((D_MODEL, LANE), lambda i: (0, 0)),
            pl.BlockSpec((1, LANE), lambda i: (0, 0)),
        ],
        out_specs=[
            pl.BlockSpec((ROW_TILE, D_MODEL), lambda i: (i, 0)),
            pl.BlockSpec((ROW_TILE * TOK_TILES, LANE), lambda i: (i, 0)),
            pl.BlockSpec((ROW_TILE, LANE), lambda i: (i, 0)),
            pl.BlockSpec((1, LANE), lambda i: (0, 0)),
        ],
        out_shape=[
            jax.ShapeDtypeStruct((n, D_MODEL), F32),
            jax.ShapeDtypeStruct((n * TOK_TILES, LANE), F32),
            jax.ShapeDtypeStruct((n, LANE), F32),
            jax.ShapeDtypeStruct((1, LANE), F32),
        ],
        scratch_shapes=[pltpu.VMEM((1, LANE), F32)],
        compiler_params=pltpu.CompilerParams(dimension_semantics=("arbitrary",),
                                             vmem_limit_bytes=VMEM_LIMIT),
        name="out_proj",
    )(og_p, og_s, ol_p, ol_s, x_p, x_s, wo, g, wr, br)


def _store_token_tiles(ref2d, val):
    rows = val.shape[0]
    for c in range(TOK_TILES):
        ref2d[pl.ds(c, rows, stride=TOK_TILES), :] = val[:, c * LANE:(c + 1) * LANE]


def _load_token_tiles(ref2d, first_row, rows):
    return jnp.concatenate(
        [ref2d[pl.ds(first_row * TOK_TILES + c, rows, stride=TOK_TILES), :] for c in range(TOK_TILES)], axis=1)


def _gather_rows(src_tiles, idx_ref, n_rows, dst2d, sem, priorities):
    def issue(j, carry):
        for u in range(DMA_ISSUE_UNROLL):
            r = j * DMA_ISSUE_UNROLL + u
            dst = dst2d.at[pl.ds(pl.multiple_of(r * TOK_TILES, TOK_TILES), TOK_TILES), :]
            pltpu.make_async_copy(src_tiles.at[idx_ref[0, r]], dst, sem).start(
                priority=priorities[u % len(priorities)])
        return carry

    lax.fori_loop(0, n_rows // DMA_ISSUE_UNROLL, issue, 0)


def _wait_rows(src2d, n_rows, dst2d, sem):
    pltpu.make_async_copy(src2d.at[pl.ds(0, n_rows * TOK_TILES), :], dst2d, sem).wait()


def _expert_body(be_ref, nu_ref, idx_ref, idx_next_ref, h_tiles, h_2d, wup_ref, bup_ref, wdn_ref, bdn_ref, y_ref,
                 xbuf, sems):
    i = pl.program_id(0)
    n_used = nu_ref[0]
    slot = i % 2

    @pl.when((i == 0) & (n_used > 0))
    def _():
        _gather_rows(h_tiles, idx_ref, EXPERT_ROWS, xbuf.at[0], sems.at[0], EXPERT_GATHER_QUEUES)

    @pl.when(i < n_used)
    def _():
        _wait_rows(h_2d, EXPERT_ROWS, xbuf.at[slot], sems.at[slot])

        @pl.when(i + 1 < n_used)
        def _():
            _gather_rows(h_tiles, idx_next_ref, EXPERT_ROWS, xbuf.at[1 - slot], sems.at[1 - slot],
                         EXPERT_GATHER_QUEUES)

        xb = _load_token_tiles(xbuf.at[slot], 0, EXPERT_ROWS).astype(BF16)
        w = D_FF // FF_CHUNKS
        acts = []
        for j in range(FF_CHUNKS):
            gs, us_ = slice(j * w, (j + 1) * w), slice(D_FF + j * w, D_FF + (j + 1) * w)
            gate = jnp.minimum(_dot(xb, wup_ref[:, gs]) + bup_ref[:, gs], SWIGLU_LIMIT)
            up = jnp.clip(_dot(xb, wup_ref[:, us_]) + bup_ref[:, us_], -SWIGLU_LIMIT, SWIGLU_LIMIT)
            acts.append(((up + 1.0) * gate * jax.nn.sigmoid(SWIGLU_ALPHA * gate)).astype(BF16))
        a = jnp.concatenate(acts, axis=1)
        wo = D_MODEL // FF_CHUNKS
        for j in range(FF_CHUNKS):
            ys = _dot(a, wdn_ref[:, j * wo:(j + 1) * wo]) + bdn_ref[:, j * wo:(j + 1) * wo]
            for t in range(wo // LANE):
                c = j * (wo // LANE) + t
                y_ref[pl.ds(c, EXPERT_ROWS, stride=TOK_TILES), :] = ys[:, t * LANE:(t + 1) * LANE]

    @pl.when(i >= n_used)
    def _():
        y_ref[...] = jnp.zeros_like(y_ref)


def _experts(block_e, n_used, src_tok, h2_2d, w_up, b_up, w_down, b_down):
    n_blocks = block_e.shape[0]
    src_blocks = src_tok.reshape(n_blocks, 1, EXPERT_ROWS)
    grid_spec = pltpu.PrefetchScalarGridSpec(
        num_scalar_prefetch=2,
        grid=(n_blocks,),
        in_specs=[
            pl.BlockSpec((None, 1, EXPERT_ROWS), lambda i, be, nu: (i, 0, 0), memory_space=pltpu.SMEM),
            pl.BlockSpec((None, 1, EXPERT_ROWS), lambda i, be, nu: (jnp.minimum(i + 1, n_blocks - 1), 0, 0),
                         memory_space=pltpu.SMEM),
            pl.BlockSpec(memory_space=pl.ANY),
            pl.BlockSpec(memory_space=pl.ANY),
            pl.BlockSpec((None, D_MODEL, 2 * D_FF), lambda i, be, nu: (be[i], 0, 0)),
            pl.BlockSpec((None, 1, 2 * D_FF), lambda i, be, nu: (be[i], 0, 0)),
            pl.BlockSpec((None, D_FF, D_MODEL), lambda i, be, nu: (be[i], 0, 0)),
            pl.BlockSpec((None, 1, D_MODEL), lambda i, be, nu: (be[i], 0, 0)),
        ],
        out_specs=pl.BlockSpec((EXPERT_ROWS * TOK_TILES, LANE), lambda i, be, nu: (i, 0)),
        scratch_shapes=[pltpu.VMEM((2, EXPERT_ROWS * TOK_TILES, LANE), F32), pltpu.SemaphoreType.DMA((2,))],
    )
    return pl.pallas_call(
        _expert_body,
        grid_spec=grid_spec,
        out_shape=jax.ShapeDtypeStruct((n_blocks * EXPERT_ROWS * TOK_TILES, LANE), F32),
        compiler_params=pltpu.CompilerParams(dimension_semantics=("arbitrary",),
                                             vmem_limit_bytes=VMEM_LIMIT),
        name="experts",
    )(block_e, n_used, src_blocks, src_blocks, h2_2d.reshape(-1, TOK_TILES, LANE), h2_2d, w_up,
      b_up.reshape(N_EXPERTS, 1, 2 * D_FF), w_down, b_down.reshape(N_EXPERTS, 1, D_MODEL))


def _combine_body(dest_ref, dest_next_ref, rt_ref, y_tiles, y_2d, x1_ref, g_ref, op_ref, os_ref, ybuf, sems,
                  *, n_p_blocks):
    i = pl.program_id(0)
    slot = i % 2
    n_rows = TOP_K * COMBINE_ROWS

    @pl.when(i == 0)
    def _():
        _gather_rows(y_tiles, dest_ref, n_rows, ybuf.at[0], sems.at[0], COMBINE_GATHER_QUEUES)

    _wait_rows(y_2d, n_rows, ybuf.at[slot], sems.at[slot])

    @pl.when(i + 1 < pl.num_programs(0))
    def _():
        _gather_rows(y_tiles, dest_next_ref, n_rows, ybuf.at[1 - slot], sems.at[1 - slot], COMBINE_GATHER_QUEUES)

    buf = ybuf.at[slot]
    moe = _load_token_tiles(buf, 0, COMBINE_ROWS) * rt_ref[:, 2 * TOP_K:2 * TOP_K + 1]
    for k in range(1, TOP_K):
        moe = moe + _load_token_tiles(buf, k * COMBINE_ROWS, COMBINE_ROWS) * rt_ref[:, 2 * TOP_K + k:2 * TOP_K + k + 1]
    res = _rms(x1_ref[...] + moe, g_ref[...])

    @pl.when(i < n_p_blocks)
    def _():
        op_ref[...] = res

    @pl.when(i >= n_p_blocks)
    def _():
        os_ref[...] = res


def _combine(dest, rt, y_2d, x1, g, n_p):
    n = x1.shape[0]
    n_blk = n // COMBINE_ROWS
    n_p_blocks = n_p // COMBINE_ROWS
    dest_blocks = dest.reshape(n_blk, COMBINE_ROWS, TOP_K).transpose(0, 2, 1).reshape(n_blk, 1, TOP_K * COMBINE_ROWS)
    return pl.pallas_call(
        functools.partial(_combine_body, n_p_blocks=n_p_blocks),
        grid=(n_blk,),
        in_specs=[
            pl.BlockSpec((None, 1, COMBINE_ROWS * TOP_K), lambda i: (i, 0, 0), memory_space=pltpu.SMEM),
            pl.BlockSpec((None, 1, COMBINE_ROWS * TOP_K), lambda i: (jnp.minimum(i + 1, n_blk - 1), 0, 0),
                         memory_space=pltpu.SMEM),
            pl.BlockSpec((COMBINE_ROWS, LANE), lambda i: (i, 0)),
            pl.BlockSpec(memory_space=pl.ANY),
            pl.BlockSpec(memory_space=pl.ANY),
            pl.BlockSpec((COMBINE_ROWS, D_MODEL), lambda i: (i, 0)),
            pl.BlockSpec((1, D_MODEL), lambda i: (0, 0)),
        ],
        out_specs=_group_specs(COMBINE_ROWS, D_MODEL, n_p_blocks),
        out_shape=[jax.ShapeDtypeStruct((n_p, D_MODEL), F32), jax.ShapeDtypeStruct((n - n_p, D_MODEL), F32)],
        scratch_shapes=[pltpu.VMEM((2, TOP_K * COMBINE_ROWS * TOK_TILES, LANE), F32),
                        pltpu.SemaphoreType.DMA((2,))],
        compiler_params=pltpu.CompilerParams(dimension_semantics=("arbitrary",),
                                             vmem_limit_bytes=VMEM_LIMIT),
        name="combine",
    )(dest_blocks, dest_blocks, rt, y_2d.reshape(-1, TOK_TILES, LANE), y_2d, x1, g)


def _invert_body(dest_ref, tbl_ref, zbuf, sem):
    i = pl.program_id(0)

    @pl.when(i == 0)
    def _():
        zbuf[...] = jnp.zeros_like(zbuf)
        fill = pltpu.make_async_copy(zbuf, tbl_ref, sem)
        fill.start()
        fill.wait()

    def body(j, carry):
        for u in range(DMA_ISSUE_UNROLL):
            a = j * DMA_ISSUE_UNROLL + u
            tbl_ref[dest_ref[0, a]] = lax.shift_right_logical(i * INVERT_BLOCK + a, _log2(TOP_K))
        return carry

    lax.fori_loop(0, INVERT_BLOCK // DMA_ISSUE_UNROLL, body, 0)


def _invert(dest, n_rows):
    n_assign = dest.shape[0]
    n_blk = n_assign // INVERT_BLOCK
    return pl.pallas_call(
        _invert_body,
        grid=(n_blk,),
        in_specs=[pl.BlockSpec((None, 1, INVERT_BLOCK), lambda i: (i, 0, 0), memory_space=pltpu.SMEM)],
        out_specs=pl.BlockSpec(memory_space=pltpu.SMEM),
        out_shape=jax.ShapeDtypeStruct((n_rows,), jnp.int32),
        scratch_shapes=[pltpu.VMEM((n_rows,), jnp.int32), pltpu.SemaphoreType.DMA(())],
        compiler_params=pltpu.CompilerParams(dimension_semantics=("arbitrary",)),
        name="invert",
    )(dest.reshape(n_blk, 1, INVERT_BLOCK))


def _route(rt, cnt):
    n = rt.shape[0]
    n_assign = n * TOP_K
    flat_e = rt[:, 0:TOP_K].astype(jnp.int32).reshape(-1)
    rank = rt[:, TOP_K:2 * TOP_K].astype(jnp.int32).reshape(-1)
    counts = cnt[0, :N_EXPERTS].astype(jnp.int32)
    padded = (counts + EXPERT_ROWS - 1) // EXPERT_ROWS * EXPERT_ROWS
    pend = jnp.cumsum(padded)
    pstart = pend - padded
    dest = (pstart[flat_e] + rank).astype(jnp.int32)
    n_rows = n_assign + N_EXPERTS * EXPERT_ROWS
    n_blocks = n_rows // EXPERT_ROWS
    src_tok = _invert(dest, n_rows)
    block_first = jnp.arange(n_blocks, dtype=jnp.int32) * EXPERT_ROWS
    block_e = jnp.minimum(jnp.sum((pend[None, :] <= block_first[:, None]).astype(jnp.int32), axis=1),
                          N_EXPERTS - 1).astype(jnp.int32)
    n_used = (pend[-1] // EXPERT_ROWS).astype(jnp.int32).reshape(1)
    return dest, src_tok, block_e, n_used


def _pad_lanes(v, width):
    return jnp.zeros((1, width), F32).at[0, :v.shape[0]].set(v.astype(F32))


def kernel(x_prompt, x_sample, state_gdn_conv, state_gdn, state_gla, rms_mix_w, w_in, conv_w, gdn_a_log,
           gdn_dt_bias, gdn_norm_w, gla_gk_w, gla_gk_b, gla_norm_w, w_out, rms_ffn_w, w_router, b_router,
           w_up, b_up, w_down, b_down, rms_final_w):
    bp, tp, d = x_prompt.shape
    bs, ts, _ = x_sample.shape
    n_p, n_s = bp * tp, bs * ts
    assert d == D_MODEL and state_gdn.shape[0] == 1, "single-layer kernel"
    assert tp >= CONV_WIDTH - 1 and ts >= CONV_WIDTH - 1, "new conv state is taken from the new tokens only"
    l = 0

    wi = w_in[l]
    a0 = GDN_CONV_CH + GDN_V_W
    g0 = a0 + 2 * GDN_HEADS
    lr0 = g0 + 2 * GLA_QK_W + 2 * GLA_V_W
    small = jnp.concatenate([wi[:, a0:a0 + 2 * GDN_HEADS], wi[:, lr0:lr0 + GLA_GATE_RANK],
                             jnp.zeros((d, SM_W - 2 * GDN_HEADS - GLA_GATE_RANK), F32)], axis=1)
    w_big = jnp.concatenate([wi[:, :a0], wi[:, g0:lr0], small], axis=1).astype(BF16)
    alog = _pad_lanes(gdn_a_log[l], SM_W)
    dtb = _pad_lanes(gdn_dt_bias[l], SM_W)
    wgk = jnp.zeros((SM_W, GLA_QK_W), F32).at[SM_LR:SM_LR + GLA_GATE_RANK].set(gla_gk_w[l])
    wr = jnp.zeros((d, LANE), F32).at[:, :N_EXPERTS].set(w_router[l])
    br = jnp.full((1, LANE), -1e30, F32).at[0, :N_EXPERTS].set(b_router[l])

    assert n_p % ROW_TILE == 0 and n_s % ROW_TILE == 0
    x_p, x_s = x_prompt.reshape(n_p, d), x_sample.reshape(n_s, d)
    proj = _inproj(x_p, x_s, rms_mix_w[l][None, :], w_big)

    tb_p = PROMPT_TIME_BLOCK
    zeros_conv = jnp.zeros((bp, CONV_WIDTH - 1, GDN_CONV_CH), F32)
    og_p, gdn_p, conv_p = _gdn(proj, bp, 1, tp, tb_p, CHUNK, CHUNK, zeros_conv,
                               jnp.zeros((bp, GDN_HEADS, GDN_DK, GDN_DV), F32), conv_w[l], alog, dtb,
                               gdn_norm_w[l][None, :])
    ol_p, gla_p = _gla(proj, bp, 1, tp, tb_p, CHUNK, CHUNK, jnp.zeros((bp, GLA_HEADS, GLA_DK, GLA_DV), F32),
                       wgk, gla_gk_b[l][None, :], gla_norm_w[l][None, :])

    ts_pad = SUBLANE
    nb_s = SAMPLE_SEQS_PER_STEP
    proj_s = proj[n_p:].reshape(bs, ts, PROJ_W)
    proj_sp = jnp.pad(proj_s, ((0, 0), (0, ts_pad - ts), (0, 0))).reshape(bs * ts_pad, PROJ_W)
    og_s, gdn_s, conv_s = _gdn(proj_sp, bs, nb_s, ts_pad, ts_pad, ts_pad, ts, state_gdn_conv[l], state_gdn[l],
                               conv_w[l], alog, dtb, gdn_norm_w[l][None, :])
    ol_s, gla_s = _gla(proj_sp, bs, nb_s, ts_pad, ts_pad, ts_pad, ts, state_gla[l], wgk, gla_gk_b[l][None, :],
                       gla_norm_w[l][None, :])
    og_s = og_s.reshape(bs, ts_pad, GDN_V_W)[:, :ts].reshape(n_s, GDN_V_W)
    ol_s = ol_s.reshape(bs, ts_pad, GLA_V_W)[:, :ts].reshape(n_s, GLA_V_W)

    x1, h2, rt, cnt = _outproj(og_p, og_s, ol_p, ol_s, x_p, x_s, w_out[l].astype(BF16), rms_ffn_w[l][None, :],
                               wr, br)

    dest, src_tok, block_e, n_used = _route(rt, cnt)
    y_rows = _experts(block_e, n_used, src_tok, h2, w_up[l], b_up[l], w_down[l], b_down[l])
    y_p, y_s = _combine(dest, rt, y_rows, x1, rms_final_w[None, :], n_p)
    y_prompt = y_p.reshape(bp, tp, d)
    y_sample = y_s.reshape(bs, ts, d)
    return (y_prompt, y_sample, conv_p[None], gdn_p[None], gla_p[None], conv_s[None], gdn_s[None], gla_s[None])
```

```python
import functools

import jax
import jax.numpy as jnp
from jax import lax
from jax.experimental import pallas as pl
from jax.experimental.pallas import tpu as pltpu

F32 = jnp.float32
BF16 = jnp.bfloat16
HI = lax.Precision.HIGHEST

D_MODEL = 1024
GDN_HEADS = 4
GDN_DK = 128
GDN_DV = 128
GLA_HEADS = 4
GLA_DK = 64
GLA_DV = 128
GLA_GATE_RANK = 16
GLA_GATE_NORMALIZER = 16.0
CONV_WIDTH = 4
CHUNK = 64
N_EXPERTS = 32
TOP_K = 4
D_FF = 1024
SWIGLU_LIMIT = 7.0
SWIGLU_ALPHA = 1.702
RMS_EPS = 1e-6
L2_EPS = 1e-6

GDN_QK_W = GDN_HEADS * GDN_DK
GDN_V_W = GDN_HEADS * GDN_DV
GDN_CONV_CH = 2 * GDN_QK_W + GDN_V_W
GLA_QK_W = GLA_HEADS * GLA_DK
GLA_V_W = GLA_HEADS * GLA_DV

COL_QKV = 0
COL_Z = 1536
COL_GQ = 2048
COL_GK = 2304
COL_GV = 2560
COL_GG = 3072
COL_SM = 3584
SM_W = 128
PROJ_W = COL_SM + SM_W
SM_A, SM_B, SM_LR = 0, 4, 8

LANE = 128
SUBLANE = 8
TOK_TILES = D_MODEL // LANE
ROW_TILE = 256
EXPERT_ROWS = 512
COMBINE_ROWS = 128
DMA_ISSUE_UNROLL = 8
INVERT_BLOCK = 1024
GDN_CHUNKS_PER_TRIP = 4
GLA_CHUNKS_PER_TRIP = 4
EXPERT_GATHER_QUEUES = (0, 1)
COMBINE_GATHER_QUEUES = (0, 1)
PROMPT_TIME_BLOCK = 512
SAMPLE_SEQS_PER_STEP = 8
VMEM_LIMIT = 56 * 1024 * 1024


def _dot(a, b):
    return jnp.dot(a.astype(BF16), b.astype(BF16), preferred_element_type=F32)


def _dot_nt(a, b):
    return lax.dot_general(a.astype(BF16), b.astype(BF16), (((1,), (1,)), ((), ())),
                           preferred_element_type=F32)


def _dot_tn(a, b):
    return lax.dot_general(a.astype(BF16), b.astype(BF16), (((0,), (0,)), ((), ())),
                           preferred_element_type=F32)


def _dot_hi(a, b):
    return jnp.dot(a, b, precision=HI, preferred_element_type=F32)


def _dot_3pass(a, b):
    a_hi = a.astype(BF16)
    b_hi = b.astype(BF16)
    a_lo = (a - a_hi.astype(F32)).astype(BF16)
    b_lo = (b - b_hi.astype(F32)).astype(BF16)

    def mm(x, y):
        return jnp.dot(x, y, preferred_element_type=F32)

    return (mm(a_lo, b_hi) + mm(a_hi, b_lo)) + mm(a_hi, b_hi)


def _rms(x, w):
    return x * lax.rsqrt(jnp.mean(x * x, axis=-1, keepdims=True) + RMS_EPS) * w


def _silu(x):
    return x * jax.nn.sigmoid(x)


def _group_specs(rows, width, n_p_blocks):
    return [pl.BlockSpec((rows, width), lambda i: (jnp.minimum(i, n_p_blocks - 1), 0)),
            pl.BlockSpec((rows, width), lambda i: (jnp.maximum(i - n_p_blocks, 0), 0))]


def _group_pick(i, n_p_blocks, p_ref, s_ref):
    return jnp.where(i < n_p_blocks, p_ref[...], s_ref[...])


def _inproj_body(xp_ref, xs_ref, g_ref, w_ref, o_ref, *, n_p_blocks):
    x = _group_pick(pl.program_id(0), n_p_blocks, xp_ref, xs_ref)
    h = _rms(x, g_ref[...])
    o_ref[...] = jnp.dot(h.astype(BF16), w_ref[...], preferred_element_type=F32)


def _inproj(x_p, x_s, g, w):
    n_p_blocks, n_s_blocks = x_p.shape[0] // ROW_TILE, x_s.shape[0] // ROW_TILE
    n = x_p.shape[0] + x_s.shape[0]
    return pl.pallas_call(
        functools.partial(_inproj_body, n_p_blocks=n_p_blocks),
        grid=(n_p_blocks + n_s_blocks,),
        in_specs=_group_specs(ROW_TILE, D_MODEL, n_p_blocks) + [
            pl.BlockSpec((1, D_MODEL), lambda i: (0, 0)),
            pl.BlockSpec((D_MODEL, PROJ_W), lambda i: (0, 0)),
        ],
        out_specs=pl.BlockSpec((ROW_TILE, PROJ_W), lambda i: (i, 0)),
        out_shape=jax.ShapeDtypeStruct((n, PROJ_W), F32),
        compiler_params=pltpu.CompilerParams(dimension_semantics=("arbitrary",),
                                             vmem_limit_bytes=VMEM_LIMIT),
        name="in_proj",
    )(x_p, x_s, g, w)


def _log2(n):
    assert n & (n - 1) == 0
    return n.bit_length() - 1


def _tri_inv_all(ms, c, ii, jj):
    eye = (ii == jj).astype(F32)
    base = min(c, 8)
    sh = _log2(base)
    blk = (ii >> sh) == (jj >> sh)
    ns = [jnp.where(blk, m, 0.0) for m in ms]
    xs = [eye - n for n in ns]
    ps = [_dot(n, n) for n in ns]
    ts = [_dot(jnp.concatenate([x, p], axis=0), p) for x, p in zip(xs, ps)]
    xs = [x + t[:c] for x, t in zip(xs, ts)]
    ps = [t[c:] for t in ts]
    xs = [x + _dot(x, p) for x, p in zip(xs, ps)]
    s = base
    while s < c:
        sh_s, sh_b = _log2(s), _log2(2 * s)
        off = ((ii >> sh_b) == (jj >> sh_b)) & ((ii >> sh_s) != (jj >> sh_s))
        ys = [_dot(x, jnp.where(off, m, 0.0)) for x, m in zip(xs, ms)]
        xs = [x - _dot(y, x) for x, y in zip(xs, ys)]
        s *= 2
    return xs


def _gated_norm(o, w, z):
    return o * lax.rsqrt(jnp.mean(o * o, axis=-1, keepdims=True) + RMS_EPS) * w * _silu(z)


def _chunk_rows(s, tb_rows, ci, c):
    r = s * tb_rows + ci * c
    if not isinstance(r, int):
        r = pl.multiple_of(r, c)
    return r


def _for_chunks(n_chunks, step):
    if n_chunks == 1:
        step(0, 0)
    else:
        lax.fori_loop(0, n_chunks, step, 0)


def _gdn_body(qkv_ref, z_ref, sm_ref, cbuf_ref, s0_ref, cw_ref, alog_ref, dtb_ref, nw_ref,
              o_ref, sout_ref, cout_ref, st, xc, act, gcs, us, wss, qgs, kds, aqs,
              *, nb, tb_rows, chunk, valid, n_tb):
    tb = pl.program_id(1)
    c = chunk
    n_heads = GDN_HEADS
    tail = CONV_WIDTH - 1
    pad = SUBLANE
    units = [(s, h) for s in range(nb) for h in range(n_heads)]

    @pl.when(tb == 0)
    def _():
        st[...] = s0_ref[...]
        for s in range(nb):
            xc[s, pad - tail:pad, :] = cbuf_ref[s]

    if n_tb > 1:
        @pl.when(tb > 0)
        def _():
            for s in range(nb):
                xc[s, pad - tail:pad, :] = xc[s, tb_rows + pad - tail:tb_rows + pad, :]

    for s in range(nb):
        xc[s, pad:pad + tb_rows, :] = qkv_ref[s * tb_rows:(s + 1) * tb_rows, :]

    slab = min(tb_rows, 64)
    for s in range(nb):
        for sl in range(tb_rows // slab):
            for cb in range(GDN_CONV_CH // 512):
                cs = slice(cb * 512, (cb + 1) * 512)
                lo = pad - tail + sl * slab
                acc = xc[s, lo:lo + slab, cs] * cw_ref[0:1, cs]
                for i in range(1, CONV_WIDTH):
                    acc = acc + xc[s, lo + i:lo + i + slab, cs] * cw_ref[i:i + 1, cs]
                act[s * tb_rows + sl * slab:s * tb_rows + (sl + 1) * slab, cs] = _silu(acc)

    ii = lax.broadcasted_iota(jnp.int32, (c, c), 0)
    jj = lax.broadcasted_iota(jnp.int32, (c, c), 1)
    lower = (ii >= jj)
    lower_f = lower.astype(F32)
    strict = (ii > jj)
    rowmask = None
    if valid < c:
        rowmask = lax.broadcasted_iota(jnp.int32, (c, 1), 0) < valid

    def hs(h, w):
        return slice(h * w, (h + 1) * w)

    n_chunks = tb_rows // c
    cpi = next(k for k in (GDN_CHUNKS_PER_TRIP, 2, 1) if n_chunks % k == 0)
    p1_units = [(g, h) for g in range(nb * cpi) for h in range(n_heads)]

    def phase1(ci, carry):
        rows, b_ts, gc_ts, gc_tts = [], [], [], []
        for g in range(nb * cpi):
            rr = pl.ds(_chunk_rows(g // cpi, tb_rows, ci * cpi + g % cpi, c), c)
            sm = sm_ref[rr, :]
            g_t = -jnp.exp(alog_ref[...]) * jax.nn.softplus(sm + dtb_ref[...])
            b_t = jax.nn.sigmoid(sm)
            if rowmask is not None:
                g_t = jnp.where(rowmask, g_t, 0.0)
                b_t = jnp.where(rowmask, b_t, 0.0)
            gc_t = _dot_hi(lower_f, g_t)
            gcs[rr, :] = gc_t
            rows.append(rr)
            b_ts.append(b_t)
            gc_ts.append(gc_t)
            gc_tts.append(gc_t.T)
        qn, kn, kb, vb = {}, {}, {}, {}
        for (s, h) in p1_units:
            q = act[rows[s], hs(h, GDN_DK)]
            k = act[rows[s], slice(GDN_QK_W + h * GDN_DK, GDN_QK_W + (h + 1) * GDN_DK)]
            v = act[rows[s], slice(2 * GDN_QK_W + h * GDN_DV, 2 * GDN_QK_W + (h + 1) * GDN_DV)]
            if rowmask is not None:
                q = jnp.where(rowmask, q, 0.0)
                k = jnp.where(rowmask, k, 0.0)
                v = jnp.where(rowmask, v, 0.0)
            qn[s, h] = q * lax.rsqrt(jnp.sum(q * q, axis=-1, keepdims=True) + L2_EPS) * (GDN_DK ** -0.5)
            kn[s, h] = k * lax.rsqrt(jnp.sum(k * k, axis=-1, keepdims=True) + L2_EPS)
            beta = b_ts[s][:, SM_B + h:SM_B + h + 1]
            kb[s, h] = kn[s, h] * beta
            vb[s, h] = v * beta
        s1 = {u: _dot_nt(jnp.concatenate([kb[u], qn[u]], axis=0), kn[u]) for u in p1_units}
        mm = []
        for (s, h) in p1_units:
            gcol = gc_ts[s][:, SM_A + h:SM_A + h + 1]
            grow = gc_tts[s][SM_A + h:SM_A + h + 1, :]
            dec = jnp.exp(jnp.where(lower, gcol - grow, -jnp.inf))
            mm.append(jnp.where(strict, s1[s, h][:c] * dec, 0.0))
            aqs[h, rows[s], :] = s1[s, h][c:] * dec
        tms = _tri_inv_all(mm, c, ii, jj)
        for (s, h), tm in zip(p1_units, tms):
            gcol = gc_ts[s][:, SM_A + h:SM_A + h + 1]
            eg = jnp.exp(gcol)
            uw = _dot(tm, jnp.concatenate([vb[s, h], kb[s, h] * eg], axis=1))
            us[rows[s], hs(h, GDN_DV)] = uw[:, :GDN_DV]
            wss[rows[s], hs(h, GDN_DV)] = uw[:, GDN_DV:]
            qgs[rows[s], hs(h, GDN_DK)] = qn[s, h] * eg
            kds[rows[s], hs(h, GDN_DK)] = kn[s, h] * jnp.exp(gcol[c - 1:c, :] - gcol)
        return carry

    def phase2(ci, carry):
        r0 = [_chunk_rows(s, tb_rows, ci, c) for s in range(nb)]
        rows = [pl.ds(r, c) for r in r0]
        ws = {(s, h): _dot(jnp.concatenate([wss[rows[s], hs(h, GDN_DV)], qgs[rows[s], hs(h, GDN_DK)]], axis=0),
                           st[s, h]) for (s, h) in units}
        v_new = {(s, h): us[rows[s], hs(h, GDN_DV)] - ws[s, h][:c] for (s, h) in units}
        o = {(s, h): ws[s, h][c:] + _dot(aqs[h, rows[s], :], v_new[s, h]) for (s, h) in units}
        upd = {(s, h): _dot_tn(kds[rows[s], hs(h, GDN_DK)], v_new[s, h]) for (s, h) in units}
        for (s, h) in units:
            g_last = gcs[pl.ds(r0[s] + c - 1, 1), SM_A + h:SM_A + h + 1]
            st[s, h] = st[s, h] * jnp.exp(g_last) + upd[s, h]
        for s in range(nb):
            o_ref[rows[s], :] = jnp.concatenate(
                [_gated_norm(o[s, h], nw_ref[...], z_ref[rows[s], hs(h, GDN_DV)]) for h in range(n_heads)], axis=1)
        return carry

    _for_chunks(n_chunks // cpi, phase1)
    _for_chunks(n_chunks, phase2)

    @pl.when(tb == n_tb - 1)
    def _():
        sout_ref[...] = st[...]
        last = tb_rows if valid == c else valid
        for s in range(nb):
            cout_ref[s] = xc[s, pad + last - tail:pad + last, :]


def _gdn(proj, n_seq, nb, t_len, tb_rows, chunk, valid, conv_buf, s0, conv_w, alog, dtb, nw):
    n_tb = t_len // tb_rows
    assert nb == 1 or n_tb == 1
    rows = nb * tb_rows

    def rowblk(b, t):
        return b * n_tb + t

    body = functools.partial(_gdn_body, nb=nb, tb_rows=tb_rows, chunk=chunk, valid=valid, n_tb=n_tb)
    return pl.pallas_call(
        body,
        grid=(n_seq // nb, n_tb),
        in_specs=[
            pl.BlockSpec((rows, GDN_CONV_CH), lambda b, t: (rowblk(b, t), COL_QKV // GDN_CONV_CH)),
            pl.BlockSpec((rows, GDN_V_W), lambda b, t: (rowblk(b, t), COL_Z // GDN_V_W)),
            pl.BlockSpec((rows, SM_W), lambda b, t: (rowblk(b, t), COL_SM // SM_W)),
            pl.BlockSpec((nb, CONV_WIDTH - 1, GDN_CONV_CH), lambda b, t: (b, 0, 0)),
            pl.BlockSpec((nb, GDN_HEADS, GDN_DK, GDN_DV), lambda b, t: (b, 0, 0, 0)),
            pl.BlockSpec((CONV_WIDTH, GDN_CONV_CH), lambda b, t: (0, 0)),
            pl.BlockSpec((1, SM_W), lambda b, t: (0, 0)),
            pl.BlockSpec((1, SM_W), lambda b, t: (0, 0)),
            pl.BlockSpec((1, GDN_DV), lambda b, t: (0, 0)),
        ],
        out_specs=[
            pl.BlockSpec((rows, GDN_V_W), lambda b, t: (rowblk(b, t), 0)),
            pl.BlockSpec((nb, GDN_HEADS, GDN_DK, GDN_DV), lambda b, t: (b, 0, 0, 0)),
            pl.BlockSpec((nb, CONV_WIDTH - 1, GDN_CONV_CH), lambda b, t: (b, 0, 0)),
        ],
        out_shape=[
            jax.ShapeDtypeStruct((n_seq * t_len, GDN_V_W), F32),
            jax.ShapeDtypeStruct((n_seq, GDN_HEADS, GDN_DK, GDN_DV), F32),
            jax.ShapeDtypeStruct((n_seq, CONV_WIDTH - 1, GDN_CONV_CH), F32),
        ],
        scratch_shapes=[
            pltpu.VMEM((nb, GDN_HEADS, GDN_DK, GDN_DV), F32),
            pltpu.VMEM((nb, tb_rows + SUBLANE, GDN_CONV_CH), F32),
            pltpu.VMEM((rows, GDN_CONV_CH), F32),
            pltpu.VMEM((rows, SM_W), F32),
            pltpu.VMEM((rows, GDN_V_W), F32),
            pltpu.VMEM((rows, GDN_V_W), F32),
            pltpu.VMEM((rows, GDN_QK_W), F32),
            pltpu.VMEM((rows, GDN_QK_W), F32),
            pltpu.VMEM((GDN_HEADS, rows, chunk), F32),
        ],
        compiler_params=pltpu.CompilerParams(dimension_semantics=("arbitrary", "arbitrary"),
                                             vmem_limit_bytes=VMEM_LIMIT),
        name="gdn_mixer",
    )(proj, proj, proj, conv_buf, s0, conv_w, alog, dtb, nw)


def _gla_body(q_ref, k_ref, v_ref, go_ref, sm_ref, s0_ref, wgk_ref, bgk_ref, nw_ref,
              o_ref, sout_ref, st, qes, ois, upds, decs, *, nb, tb_rows, chunk, valid, n_tb):
    tb = pl.program_id(1)
    c = chunk
    n_heads = GLA_HEADS
    units = [(s, h) for s in range(nb) for h in range(n_heads)]

    @pl.when(tb == 0)
    def _():
        st[...] = s0_ref[...]

    ii = lax.broadcasted_iota(jnp.int32, (c, c), 0)
    jj = lax.broadcasted_iota(jnp.int32, (c, c), 1)
    lower = (ii >= jj)
    lower_f = lower.astype(F32)
    rid = lax.broadcasted_iota(jnp.int32, (c, 1), 0)
    rowmask = (rid < valid) if valid < c else None
    n_sub = max(c // 16, 1)
    sub = c // n_sub

    n_chunks = tb_rows // c
    cpi = next(k for k in (GLA_CHUNKS_PER_TRIP, 2, 1) if n_chunks % k == 0)
    p1_units = [(g, h) for g in range(nb * cpi) for h in range(n_heads)]

    def phase1(ci, carry):
        rows, slots, bcs, bc_ts = [], [], [], []
        for g in range(nb * cpi):
            chunk_idx = ci * cpi + g % cpi
            rr = pl.ds(_chunk_rows(g // cpi, tb_rows, chunk_idx, c), c)
            slots.append((g // cpi) * n_chunks + chunk_idx)
            gk = jax.nn.log_sigmoid(_dot(sm_ref[rr, :], wgk_ref[...]) + bgk_ref[...]) / GLA_GATE_NORMALIZER
            if rowmask is not None:
                gk = jnp.where(rowmask, gk, 0.0)
            bc = _dot_hi(lower_f, gk)
            rows.append(rr)
            bcs.append(bc)
            bc_ts.append(bc.T)
        q, k, v, bch = {}, {}, {}, {}
        for (s, h) in p1_units:
            ks = slice(h * GLA_DK, (h + 1) * GLA_DK)
            vs = slice(h * GLA_DV, (h + 1) * GLA_DV)
            q[s, h] = q_ref[rows[s], ks] * (GLA_DK ** -0.5)
            kk = k_ref[rows[s], ks]
            vv = v_ref[rows[s], vs]
            if rowmask is not None:
                kk = jnp.where(rowmask, kk, 0.0)
                vv = jnp.where(rowmask, vv, 0.0)
            k[s, h], v[s, h] = kk, vv
            bch[s, h] = bcs[s][:, ks]
        for (g, h) in p1_units:
            qes[h, rows[g], :] = q[g, h] * jnp.exp(bch[g, h])
        a = {}
        for u in p1_units:
            q_parts, k_parts = [], []
            for sb in range(n_sub):
                ref_row = bch[u][sb * sub:sb * sub + 1, :]
                in_blk = (rid >= sb * sub) & (rid < (sb + 1) * sub)
                q_parts.append(jnp.where(in_blk, q[u] * jnp.exp(jnp.where(in_blk, bch[u] - ref_row, 0.0)), 0.0))
                k_parts.append(k[u] * jnp.exp(jnp.where(rid < (sb + 1) * sub, ref_row - bch[u], 0.0)))
            q_hat = jnp.concatenate(q_parts, axis=1) if n_sub > 1 else q_parts[0]
            k_hat = jnp.concatenate(k_parts, axis=1) if n_sub > 1 else k_parts[0]
            a[u] = jnp.where(lower, _dot_nt(q_hat, k_hat), 0.0)
        upd = {u: _dot_tn(k[u] * jnp.exp(bch[u][c - 1:c, :] - bch[u]), v[u]) for u in p1_units}
        o_intra = {u: _dot(a[u], v[u]) for u in p1_units}
        for (g, h) in p1_units:
            dec_col = bc_ts[g][h * GLA_DK:(h + 1) * GLA_DK, c - 1:c]
            decs[slots[g], h] = jnp.broadcast_to(jnp.exp(dec_col), (GLA_DK, GLA_DV))
            upds[slots[g], h] = upd[g, h]
            ois[rows[g], h * GLA_DV:(h + 1) * GLA_DV] = o_intra[g, h]
        return carry

    def phase2(ci, carry):
        rows = [pl.ds(_chunk_rows(s, tb_rows, ci, c), c) for s in range(nb)]
        o = {(s, h): ois[rows[s], h * GLA_DV:(h + 1) * GLA_DV] + _dot(qes[h, rows[s], :], st[s, h])
             for (s, h) in units}
        for (s, h) in units:
            st[s, h] = decs[s * n_chunks + ci, h] * st[s, h] + upds[s * n_chunks + ci, h]
        for s in range(nb):
            o_ref[rows[s], :] = jnp.concatenate(
                [_gated_norm(o[s, h], nw_ref[...], go_ref[rows[s], h * GLA_DV:(h + 1) * GLA_DV])
                 for h in range(n_heads)], axis=1)
        return carry

    _for_chunks(n_chunks // cpi, phase1)
    _for_chunks(n_chunks, phase2)

    @pl.when(tb == n_tb - 1)
    def _():
        sout_ref[...] = st[...]


def _gla(proj, n_seq, nb, t_len, tb_rows, chunk, valid, s0, wgk, bgk, nw):
    n_tb = t_len // tb_rows
    assert nb == 1 or n_tb == 1
    rows = nb * tb_rows

    def rowblk(b, t):
        return b * n_tb + t

    body = functools.partial(_gla_body, nb=nb, tb_rows=tb_rows, chunk=chunk, valid=valid, n_tb=n_tb)
    return pl.pallas_call(
        body,
        grid=(n_seq // nb, n_tb),
        in_specs=[
            pl.BlockSpec((rows, GLA_QK_W), lambda b, t: (rowblk(b, t), COL_GQ // GLA_QK_W)),
            pl.BlockSpec((rows, GLA_QK_W), lambda b, t: (rowblk(b, t), COL_GK // GLA_QK_W)),
            pl.BlockSpec((rows, GLA_V_W), lambda b, t: (rowblk(b, t), COL_GV // GLA_V_W)),
            pl.BlockSpec((rows, GLA_V_W), lambda b, t: (rowblk(b, t), COL_GG // GLA_V_W)),
            pl.BlockSpec((rows, SM_W), lambda b, t: (rowblk(b, t), COL_SM // SM_W)),
            pl.BlockSpec((nb, GLA_HEADS, GLA_DK, GLA_DV), lambda b, t: (b, 0, 0, 0)),
            pl.BlockSpec((SM_W, GLA_QK_W), lambda b, t: (0, 0)),
            pl.BlockSpec((1, GLA_QK_W), lambda b, t: (0, 0)),
            pl.BlockSpec((1, GLA_DV), lambda b, t: (0, 0)),
        ],
        out_specs=[
            pl.BlockSpec((rows, GLA_V_W), lambda b, t: (rowblk(b, t), 0)),
            pl.BlockSpec((nb, GLA_HEADS, GLA_DK, GLA_DV), lambda b, t: (b, 0, 0, 0)),
        ],
        out_shape=[
            jax.ShapeDtypeStruct((n_seq * t_len, GLA_V_W), F32),
            jax.ShapeDtypeStruct((n_seq, GLA_HEADS, GLA_DK, GLA_DV), F32),
        ],
        scratch_shapes=[
            pltpu.VMEM((nb, GLA_HEADS, GLA_DK, GLA_DV), F32),
            pltpu.VMEM((GLA_HEADS, rows, GLA_DK), F32),
            pltpu.VMEM((rows, GLA_V_W), F32),
            pltpu.VMEM((rows // chunk, GLA_HEADS, GLA_DK, GLA_DV), F32),
            pltpu.VMEM((rows // chunk, GLA_HEADS, GLA_DK, GLA_DV), F32),
        ],
        compiler_params=pltpu.CompilerParams(dimension_semantics=("arbitrary", "arbitrary"),
                                             vmem_limit_bytes=VMEM_LIMIT),
        name="gla_mixer",
    )(proj, proj, proj, proj, proj, s0, wgk, bgk, nw)


def _outproj_body(ogp_ref, ogs_ref, olp_ref, ols_ref, xp_ref, xs_ref, wo_ref, g_ref, wr_ref, br_ref,
                  x1_ref, h2_ref, rt_ref, cnt_ref, base, *, n_p_blocks):
    i = pl.program_id(0)

    @pl.when(i == 0)
    def _():
        base[...] = jnp.zeros_like(base)

    o = jnp.concatenate([_group_pick(i, n_p_blocks, ogp_ref, ogs_ref),
                         _group_pick(i, n_p_blocks, olp_ref, ols_ref)], axis=1)
    x1 = _group_pick(i, n_p_blocks, xp_ref, xs_ref) + jnp.dot(o.astype(BF16), wo_ref[...],
                                                               preferred_element_type=F32)
    x1_ref[...] = x1
    h = _rms(x1, g_ref[...])
    _store_token_tiles(h2_ref, h)
    logits = _dot_3pass(h, wr_ref[...]) + br_ref[...]

    tm = logits.shape[0]
    lt = logits.T[:N_EXPERTS]
    eid = lax.broadcasted_iota(jnp.int32, (N_EXPERTS, tm), 0)
    work = lt
    sel = jnp.zeros((N_EXPERTS, tm), F32)
    hits, ids, vals = [], [], []
    for _ in range(TOP_K):
        m = jnp.max(work, axis=0, keepdims=True)
        idx = jnp.min(jnp.where(work == m, eid, N_EXPERTS), axis=0, keepdims=True)
        hit = eid == idx
        hits.append(hit)
        ids.append(idx)
        vals.append(m)
        work = jnp.where(hit, -jnp.inf, work)
        sel = sel + hit.astype(F32)
    exps = [jnp.exp(v - vals[0]) for v in vals]
    den = exps[0]
    for e in exps[1:]:
        den = den + e
    gates = [e / den for e in exps]

    ri = lax.broadcasted_iota(jnp.int32, (tm, tm), 0)
    ci = lax.broadcasted_iota(jnp.int32, (tm, tm), 1)
    before = _dot(sel, (ri < ci).astype(F32)) + base[...]
    ranks = [jnp.sum(jnp.where(hit, before, 0.0), axis=0, keepdims=True) for hit in hits]
    base[...] = base[...] + jnp.sum(sel, axis=1, keepdims=True)
    cnt_ref[...] = base[...]

    row = lax.broadcasted_iota(jnp.int32, (LANE, tm), 0)
    rec = jnp.zeros((LANE, tm), F32)
    for k in range(TOP_K):
        rec = jnp.where(row == k, ids[k].astype(F32), rec)
        rec = jnp.where(row == TOP_K + k, ranks[k], rec)
        rec = jnp.where(row == 2 * TOP_K + k, gates[k], rec)
    rt_ref[...] = rec.T


def _outproj(og_p, og_s, ol_p, ol_s, x_p, x_s, wo, g, wr, br):
    n_p_blocks, n_s_blocks = x_p.shape[0] // ROW_TILE, x_s.shape[0] // ROW_TILE
    n = x_p.shape[0] + x_s.shape[0]
    return pl.pallas_call(
        functools.partial(_outproj_body, n_p_blocks=n_p_blocks),
        grid=(n_p_blocks + n_s_blocks,),
        in_specs=_group_specs(ROW_TILE, GDN_V_W, n_p_blocks) + _group_specs(ROW_TILE, GLA_V_W, n_p_blocks)
        + _group_specs(ROW_TILE, D_MODEL, n_p_blocks) + [
            pl.BlockSpec((D_MODEL, D_MODEL), lambda i: (0, 0)),
            pl.BlockSpec((1, D_MODEL), lambda i: (0, 0)),
            pl.BlockSpec((D_MODEL, LANE), lambda i: (0, 0)),
            pl.BlockSpec((1, LANE), lambda i: (0, 0)),
        ],
        out_specs=[
            pl.BlockSpec((ROW_TILE, D_MODEL), lambda i: (i, 0)),
            pl.BlockSpec((ROW_TILE * TOK_TILES, LANE), lambda i: (i, 0)),
            pl.BlockSpec((ROW_TILE, LANE), lambda i: (i, 0)),
            pl.BlockSpec((N_EXPERTS, 1), lambda i: (0, 0)),
        ],
        out_shape=[
            jax.ShapeDtypeStruct((n, D_MODEL), F32),
            jax.ShapeDtypeStruct((n * TOK_TILES, LANE), F32),
            jax.ShapeDtypeStruct((n, LANE), F32),
            jax.ShapeDtypeStruct((N_EXPERTS, 1), F32),
        ],
        scratch_shapes=[pltpu.VMEM((N_EXPERTS, 1), F32)],
        compiler_params=pltpu.CompilerParams(dimension_semantics=("arbitrary",),
                                             vmem_limit_bytes=VMEM_LIMIT),
        name="out_proj",
    )(og_p, og_s, ol_p, ol_s, x_p, x_s, wo, g, wr, br)


def _store_token_tiles(ref2d, val):
    rows = val.shape[0]
    for c in range(TOK_TILES):
        ref2d[pl.ds(c, rows, stride=TOK_TILES), :] = val[:, c * LANE:(c + 1) * LANE]


def _load_token_tiles(ref2d, first_row, rows):
    return jnp.concatenate(
        [ref2d[pl.ds(first_row * TOK_TILES + c, rows, stride=TOK_TILES), :] for c in range(TOK_TILES)], axis=1)


def _gather_rows(src_tiles, idx_ref, n_rows, dst2d, sem, priorities):
    def issue(j, carry):
        for u in range(DMA_ISSUE_UNROLL):
            r = j * DMA_ISSUE_UNROLL + u
            dst = dst2d.at[pl.ds(pl.multiple_of(r * TOK_TILES, TOK_TILES), TOK_TILES), :]
            pltpu.make_async_copy(src_tiles.at[idx_ref[0, r]], dst, sem).start(
                priority=priorities[u % len(priorities)])
        return carry

    lax.fori_loop(0, n_rows // DMA_ISSUE_UNROLL, issue, 0)


def _wait_rows(src2d, n_rows, dst2d, sem):
    pltpu.make_async_copy(src2d.at[pl.ds(0, n_rows * TOK_TILES), :], dst2d, sem).wait()


def _expert_body(be_ref, nu_ref, idx_ref, idx_next_ref, h_tiles, h_2d, wup_ref, bup_ref, wdn_ref, bdn_ref, y_ref,
                 xbuf, sems):
    i = pl.program_id(0)
    n_used = nu_ref[0]
    slot = i % 2

    @pl.when((i == 0) & (n_used > 0))
    def _():
        _gather_rows(h_tiles, idx_ref, EXPERT_ROWS, xbuf.at[0], sems.at[0], EXPERT_GATHER_QUEUES)

    @pl.when(i < n_used)
    def _():
        _wait_rows(h_2d, EXPERT_ROWS, xbuf.at[slot], sems.at[slot])

        @pl.when(i + 1 < n_used)
        def _():
            _gather_rows(h_tiles, idx_next_ref, EXPERT_ROWS, xbuf.at[1 - slot], sems.at[1 - slot],
                         EXPERT_GATHER_QUEUES)

        gu = _dot(_load_token_tiles(xbuf.at[slot], 0, EXPERT_ROWS), wup_ref[...]) + bup_ref[...]
        gate = jnp.minimum(gu[:, :D_FF], SWIGLU_LIMIT)
        up = jnp.clip(gu[:, D_FF:], -SWIGLU_LIMIT, SWIGLU_LIMIT)
        a = (up + 1.0) * gate * jax.nn.sigmoid(SWIGLU_ALPHA * gate)
        _store_token_tiles(y_ref, _dot(a, wdn_ref[...]) + bdn_ref[...])

    @pl.when(i >= n_used)
    def _():
        y_ref[...] = jnp.zeros_like(y_ref)


def _experts(block_e, n_used, src_tok, h2_2d, w_up, b_up, w_down, b_down):
    n_blocks = block_e.shape[0]
    src_blocks = src_tok.reshape(n_blocks, 1, EXPERT_ROWS)
    grid_spec = pltpu.PrefetchScalarGridSpec(
        num_scalar_prefetch=2,
        grid=(n_blocks,),
        in_specs=[
            pl.BlockSpec((None, 1, EXPERT_ROWS), lambda i, be, nu: (i, 0, 0), memory_space=pltpu.SMEM),
            pl.BlockSpec((None, 1, EXPERT_ROWS), lambda i, be, nu: (jnp.minimum(i + 1, n_blocks - 1), 0, 0),
                         memory_space=pltpu.SMEM),
            pl.BlockSpec(memory_space=pl.ANY),
            pl.BlockSpec(memory_space=pl.ANY),
            pl.BlockSpec((None, D_MODEL, 2 * D_FF), lambda i, be, nu: (be[i], 0, 0)),
            pl.BlockSpec((None, 1, 2 * D_FF), lambda i, be, nu: (be[i], 0, 0)),
            pl.BlockSpec((None, D_FF, D_MODEL), lambda i, be, nu: (be[i], 0, 0)),
            pl.BlockSpec((None, 1, D_MODEL), lambda i, be, nu: (be[i], 0, 0)),
        ],
        out_specs=pl.BlockSpec((EXPERT_ROWS * TOK_TILES, LANE), lambda i, be, nu: (i, 0)),
        scratch_shapes=[pltpu.VMEM((2, EXPERT_ROWS * TOK_TILES, LANE), F32), pltpu.SemaphoreType.DMA((2,))],
    )
    return pl.pallas_call(
        _expert_body,
        grid_spec=grid_spec,
        out_shape=jax.ShapeDtypeStruct((n_blocks * EXPERT_ROWS * TOK_TILES, LANE), F32),
        compiler_params=pltpu.CompilerParams(dimension_semantics=("arbitrary",),
                                             vmem_limit_bytes=VMEM_LIMIT),
        name="experts",
    )(block_e, n_used, src_blocks, src_blocks, h2_2d.reshape(-1, TOK_TILES, LANE), h2_2d, w_up,
      b_up.reshape(N_EXPERTS, 1, 2 * D_FF), w_down, b_down.reshape(N_EXPERTS, 1, D_MODEL))


def _combine_body(dest_ref, dest_next_ref, rt_ref, y_tiles, y_2d, x1_ref, g_ref, op_ref, os_ref, ybuf, sems,
                  *, n_p_blocks):
    i = pl.program_id(0)
    slot = i % 2
    n_rows = TOP_K * COMBINE_ROWS

    @pl.when(i == 0)
    def _():
        _gather_rows(y_tiles, dest_ref, n_rows, ybuf.at[0], sems.at[0], COMBINE_GATHER_QUEUES)

    _wait_rows(y_2d, n_rows, ybuf.at[slot], sems.at[slot])

    @pl.when(i + 1 < pl.num_programs(0))
    def _():
        _gather_rows(y_tiles, dest_next_ref, n_rows, ybuf.at[1 - slot], sems.at[1 - slot], COMBINE_GATHER_QUEUES)

    buf = ybuf.at[slot]
    moe = _load_token_tiles(buf, 0, COMBINE_ROWS) * rt_ref[:, 2 * TOP_K:2 * TOP_K + 1]
    for k in range(1, TOP_K):
        moe = moe + _load_token_tiles(buf, k * COMBINE_ROWS, COMBINE_ROWS) * rt_ref[:, 2 * TOP_K + k:2 * TOP_K + k + 1]
    res = _rms(x1_ref[...] + moe, g_ref[...])

    @pl.when(i < n_p_blocks)
    def _():
        op_ref[...] = res

    @pl.when(i >= n_p_blocks)
    def _():
        os_ref[...] = res


def _combine(dest, rt, y_2d, x1, g, n_p):
    n = x1.shape[0]
    n_blk = n // COMBINE_ROWS
    n_p_blocks = n_p // COMBINE_ROWS
    dest_blocks = dest.reshape(n_blk, COMBINE_ROWS, TOP_K).transpose(0, 2, 1).reshape(n_blk, 1, TOP_K * COMBINE_ROWS)
    return pl.pallas_call(
        functools.partial(_combine_body, n_p_blocks=n_p_blocks),
        grid=(n_blk,),
        in_specs=[
            pl.BlockSpec((None, 1, COMBINE_ROWS * TOP_K), lambda i: (i, 0, 0), memory_space=pltpu.SMEM),
            pl.BlockSpec((None, 1, COMBINE_ROWS * TOP_K), lambda i: (jnp.minimum(i + 1, n_blk - 1), 0, 0),
                         memory_space=pltpu.SMEM),
            pl.BlockSpec((COMBINE_ROWS, LANE), lambda i: (i, 0)),
            pl.BlockSpec(memory_space=pl.ANY),
            pl.BlockSpec(memory_space=pl.ANY),
            pl.BlockSpec((COMBINE_ROWS, D_MODEL), lambda i: (i, 0)),
            pl.BlockSpec((1, D_MODEL), lambda i: (0, 0)),
        ],
        out_specs=_group_specs(COMBINE_ROWS, D_MODEL, n_p_blocks),
        out_shape=[jax.ShapeDtypeStruct((n_p, D_MODEL), F32), jax.ShapeDtypeStruct((n - n_p, D_MODEL), F32)],
        scratch_shapes=[pltpu.VMEM((2, TOP_K * COMBINE_ROWS * TOK_TILES, LANE), F32),
                        pltpu.SemaphoreType.DMA((2,))],
        compiler_params=pltpu.CompilerParams(dimension_semantics=("arbitrary",),
                                             vmem_limit_bytes=VMEM_LIMIT),
        name="combine",
    )(dest_blocks, dest_blocks, rt, y_2d.reshape(-1, TOK_TILES, LANE), y_2d, x1, g)


def _invert_body(dest_ref, tbl_ref, zbuf, sem):
    i = pl.program_id(0)

    @pl.when(i == 0)
    def _():
        zbuf[...] = jnp.zeros_like(zbuf)
        fill = pltpu.make_async_copy(zbuf, tbl_ref, sem)
        fill.start()
        fill.wait()

    toks_per_trip = DMA_ISSUE_UNROLL // TOP_K

    def body(j, carry):
        for u in range(toks_per_trip):
            tok = i * (INVERT_BLOCK // TOP_K) + j * toks_per_trip + u
            for k in range(TOP_K):
                tbl_ref[dest_ref[0, (j * toks_per_trip + u) * TOP_K + k]] = tok
        return carry

    lax.fori_loop(0, INVERT_BLOCK // DMA_ISSUE_UNROLL, body, 0)


def _invert(dest, n_rows):
    n_assign = dest.shape[0]
    n_blk = n_assign // INVERT_BLOCK
    return pl.pallas_call(
        _invert_body,
        grid=(n_blk,),
        in_specs=[pl.BlockSpec((None, 1, INVERT_BLOCK), lambda i: (i, 0, 0), memory_space=pltpu.SMEM)],
        out_specs=pl.BlockSpec(memory_space=pltpu.SMEM),
        out_shape=jax.ShapeDtypeStruct((n_rows,), jnp.int32),
        scratch_shapes=[pltpu.VMEM((n_rows,), jnp.int32), pltpu.SemaphoreType.DMA(())],
        compiler_params=pltpu.CompilerParams(dimension_semantics=("arbitrary",)),
        name="invert",
    )(dest.reshape(n_blk, 1, INVERT_BLOCK))


def _route(rt, cnt):
    n = rt.shape[0]
    n_assign = n * TOP_K
    flat_e = rt[:, 0:TOP_K].astype(jnp.int32).reshape(-1)
    rank = rt[:, TOP_K:2 * TOP_K].astype(jnp.int32).reshape(-1)
    counts = cnt[:, 0].astype(jnp.int32)
    padded = (counts + EXPERT_ROWS - 1) // EXPERT_ROWS * EXPERT_ROWS
    pend = jnp.cumsum(padded)
    pstart = pend - padded
    dest = (pstart[flat_e] + rank).astype(jnp.int32)
    n_rows = n_assign + N_EXPERTS * EXPERT_ROWS
    n_blocks = n_rows // EXPERT_ROWS
    src_tok = _invert(dest, n_rows)
    block_first = jnp.arange(n_blocks, dtype=jnp.int32) * EXPERT_ROWS
    block_e = jnp.minimum(jnp.sum((pend[None, :] <= block_first[:, None]).astype(jnp.int32), axis=1),
                          N_EXPERTS - 1).astype(jnp.int32)
    n_used = (pend[-1] // EXPERT_ROWS).astype(jnp.int32).reshape(1)
    return dest, src_tok, block_e, n_used


def _pad_lanes(v, width):
    return jnp.zeros((1, width), F32).at[0, :v.shape[0]].set(v.astype(F32))


def kernel(x_prompt, x_sample, state_gdn_conv, state_gdn, state_gla, rms_mix_w, w_in, conv_w, gdn_a_log,
           gdn_dt_bias, gdn_norm_w, gla_gk_w, gla_gk_b, gla_norm_w, w_out, rms_ffn_w, w_router, b_router,
           w_up, b_up, w_down, b_down, rms_final_w):
    bp, tp, d = x_prompt.shape
    bs, ts, _ = x_sample.shape
    n_p, n_s = bp * tp, bs * ts
    assert d == D_MODEL and state_gdn.shape[0] == 1, "single-layer kernel"
    assert tp >= CONV_WIDTH - 1 and ts >= CONV_WIDTH - 1, "new conv state is taken from the new tokens only"
    l = 0

    wi = w_in[l]
    a0 = GDN_CONV_CH + GDN_V_W
    g0 = a0 + 2 * GDN_HEADS
    lr0 = g0 + 2 * GLA_QK_W + 2 * GLA_V_W
    small = jnp.concatenate([wi[:, a0:a0 + 2 * GDN_HEADS], wi[:, lr0:lr0 + GLA_GATE_RANK],
                             jnp.zeros((d, SM_W - 2 * GDN_HEADS - GLA_GATE_RANK), F32)], axis=1)
    w_big = jnp.concatenate([wi[:, :a0], wi[:, g0:lr0], small], axis=1).astype(BF16)
    alog = _pad_lanes(gdn_a_log[l], SM_W)
    dtb = _pad_lanes(gdn_dt_bias[l], SM_W)
    wgk = jnp.zeros((SM_W, GLA_QK_W), F32).at[SM_LR:SM_LR + GLA_GATE_RANK].set(gla_gk_w[l])
    wr = jnp.zeros((d, LANE), F32).at[:, :N_EXPERTS].set(w_router[l])
    br = jnp.full((1, LANE), -1e30, F32).at[0, :N_EXPERTS].set(b_router[l])

    assert n_p % ROW_TILE == 0 and n_s % ROW_TILE == 0
    x_p, x_s = x_prompt.reshape(n_p, d), x_sample.reshape(n_s, d)
    proj = _inproj(x_p, x_s, rms_mix_w[l][None, :], w_big)

    tb_p = PROMPT_TIME_BLOCK
    zeros_conv = jnp.zeros((bp, CONV_WIDTH - 1, GDN_CONV_CH), F32)
    og_p, gdn_p, conv_p = _gdn(proj, bp, 1, tp, tb_p, CHUNK, CHUNK, zeros_conv,
                               jnp.zeros((bp, GDN_HEADS, GDN_DK, GDN_DV), F32), conv_w[l], alog, dtb,
                               gdn_norm_w[l][None, :])
    ol_p, gla_p = _gla(proj, bp, 1, tp, tb_p, CHUNK, CHUNK, jnp.zeros((bp, GLA_HEADS, GLA_DK, GLA_DV), F32),
                       wgk, gla_gk_b[l][None, :], gla_norm_w[l][None, :])

    ts_pad = SUBLANE
    nb_s = SAMPLE_SEQS_PER_STEP
    proj_s = proj[n_p:].reshape(bs, ts, PROJ_W)
    proj_sp = jnp.pad(proj_s, ((0, 0), (0, ts_pad - ts), (0, 0))).reshape(bs * ts_pad, PROJ_W)
    og_s, gdn_s, conv_s = _gdn(proj_sp, bs, nb_s, ts_pad, ts_pad, ts_pad, ts, state_gdn_conv[l], state_gdn[l],
                               conv_w[l], alog, dtb, gdn_norm_w[l][None, :])
    ol_s, gla_s = _gla(proj_sp, bs, nb_s, ts_pad, ts_pad, ts_pad, ts, state_gla[l], wgk, gla_gk_b[l][None, :],
                       gla_norm_w[l][None, :])
    og_s = og_s.reshape(bs, ts_pad, GDN_V_W)[:, :ts].reshape(n_s, GDN_V_W)
    ol_s = ol_s.reshape(bs, ts_pad, GLA_V_W)[:, :ts].reshape(n_s, GLA_V_W)

    x1, h2, rt, cnt = _outproj(og_p, og_s, ol_p, ol_s, x_p, x_s, w_out[l].astype(BF16), rms_ffn_w[l][None, :],
                               wr, br)

    dest, src_tok, block_e, n_used = _route(rt, cnt)
    y_rows = _experts(block_e, n_used, src_tok, h2, w_up[l], b_up[l], w_down[l], b_down[l])
    y_p, y_s = _combine(dest, rt, y_rows, x1, rms_final_w[None, :], n_p)
    y_prompt = y_p.reshape(bp, tp, d)
    y_sample = y_s.reshape(bs, ts, d)
    return (y_prompt, y_sample, conv_p[None], gdn_p[None], gla_p[None], conv_s[None], gdn_s[None], gla_s[None])
```

```python
import functools

import jax
import jax.numpy as jnp
from jax import lax
from jax.experimental import pallas as pl
from jax.experimental.pallas import tpu as pltpu

F32 = jnp.float32
BF16 = jnp.bfloat16
HI = lax.Precision.HIGHEST

D_MODEL = 1024
GDN_HEADS = 4
GDN_DK = 128
GDN_DV = 128
GLA_HEADS = 4
GLA_DK = 64
GLA_DV = 128
GLA_GATE_RANK = 16
GLA_GATE_NORMALIZER = 16.0
CONV_WIDTH = 4
CHUNK = 64
N_EXPERTS = 32
TOP_K = 4
D_FF = 1024
SWIGLU_LIMIT = 7.0
SWIGLU_ALPHA = 1.702
RMS_EPS = 1e-6
L2_EPS = 1e-6

GDN_QK_W = GDN_HEADS * GDN_DK
GDN_V_W = GDN_HEADS * GDN_DV
GDN_CONV_CH = 2 * GDN_QK_W + GDN_V_W
GLA_QK_W = GLA_HEADS * GLA_DK
GLA_V_W = GLA_HEADS * GLA_DV

COL_QKV = 0
COL_Z = 1536
COL_GQ = 2048
COL_GK = 2304
COL_GV = 2560
COL_GG = 3072
COL_SM = 3584
SM_W = 128
PROJ_W = COL_SM + SM_W
SM_A, SM_B, SM_LR = 0, 4, 8

LANE = 128
SUBLANE = 8
TOK_TILES = D_MODEL // LANE
ROW_TILE = 256
EXPERT_ROWS = 256
EXPERT_WEIGHT_QUEUE = 1
COMBINE_ROWS = 128
DMA_ISSUE_UNROLL = 8
INVERT_BLOCK = 1024
GDN_CHUNKS_PER_TRIP = 4
GLA_CHUNKS_PER_TRIP = 4
EXPERT_GATHER_QUEUES = (0,)
COMBINE_GATHER_QUEUES = (0, 1)
PROMPT_TIME_BLOCK = 512
SAMPLE_SEQS_PER_STEP = 8
VMEM_LIMIT = 56 * 1024 * 1024


def _dot(a, b):
    return jnp.dot(a.astype(BF16), b.astype(BF16), preferred_element_type=F32)


def _dot_nt(a, b):
    return lax.dot_general(a.astype(BF16), b.astype(BF16), (((1,), (1,)), ((), ())),
                           preferred_element_type=F32)


def _dot_tn(a, b):
    return lax.dot_general(a.astype(BF16), b.astype(BF16), (((0,), (0,)), ((), ())),
                           preferred_element_type=F32)


def _dot_hi(a, b):
    return jnp.dot(a, b, precision=HI, preferred_element_type=F32)


def _dot_3pass(a, b):
    a_hi = a.astype(BF16)
    b_hi = b.astype(BF16)
    a_lo = (a - a_hi.astype(F32)).astype(BF16)
    b_lo = (b - b_hi.astype(F32)).astype(BF16)

    def mm(x, y):
        return jnp.dot(x, y, preferred_element_type=F32)

    return (mm(a_lo, b_hi) + mm(a_hi, b_lo)) + mm(a_hi, b_hi)


def _rms(x, w):
    return x * lax.rsqrt(jnp.mean(x * x, axis=-1, keepdims=True) + RMS_EPS) * w


def _silu(x):
    return x * jax.nn.sigmoid(x)


def _group_specs(rows, width, n_p_blocks):
    return [pl.BlockSpec((rows, width), lambda i: (jnp.minimum(i, n_p_blocks - 1), 0)),
            pl.BlockSpec((rows, width), lambda i: (jnp.maximum(i - n_p_blocks, 0), 0))]


def _group_pick(i, n_p_blocks, p_ref, s_ref):
    return jnp.where(i < n_p_blocks, p_ref[...], s_ref[...])


def _inproj_body(xp_ref, xs_ref, g_ref, w_ref, o_ref, *, n_p_blocks):
    x = _group_pick(pl.program_id(0), n_p_blocks, xp_ref, xs_ref)
    h = _rms(x, g_ref[...])
    o_ref[...] = jnp.dot(h.astype(BF16), w_ref[...], preferred_element_type=F32)


def _inproj(x_p, x_s, g, w):
    n_p_blocks, n_s_blocks = x_p.shape[0] // ROW_TILE, x_s.shape[0] // ROW_TILE
    n = x_p.shape[0] + x_s.shape[0]
    return pl.pallas_call(
        functools.partial(_inproj_body, n_p_blocks=n_p_blocks),
        grid=(n_p_blocks + n_s_blocks,),
        in_specs=_group_specs(ROW_TILE, D_MODEL, n_p_blocks) + [
            pl.BlockSpec((1, D_MODEL), lambda i: (0, 0)),
            pl.BlockSpec((D_MODEL, PROJ_W), lambda i: (0, 0)),
        ],
        out_specs=pl.BlockSpec((ROW_TILE, PROJ_W), lambda i: (i, 0)),
        out_shape=jax.ShapeDtypeStruct((n, PROJ_W), F32),
        compiler_params=pltpu.CompilerParams(dimension_semantics=("arbitrary",),
                                             vmem_limit_bytes=VMEM_LIMIT),
        name="in_proj",
    )(x_p, x_s, g, w)


def _log2(n):
    assert n & (n - 1) == 0
    return n.bit_length() - 1


def _tri_inv_all(ms, c, ii, jj):
    eye = (ii == jj).astype(F32)
    base = min(c, 8)
    sh = _log2(base)
    blk = (ii >> sh) == (jj >> sh)
    ns = [jnp.where(blk, m, 0.0) for m in ms]
    xs = [eye - n for n in ns]
    ps = [_dot(n, n) for n in ns]
    ts = [_dot(jnp.concatenate([x, p], axis=0), p) for x, p in zip(xs, ps)]
    xs = [x + t[:c] for x, t in zip(xs, ts)]
    ps = [t[c:] for t in ts]
    xs = [x + _dot(x, p) for x, p in zip(xs, ps)]
    s = base
    while s < c:
        sh_s, sh_b = _log2(s), _log2(2 * s)
        off = ((ii >> sh_b) == (jj >> sh_b)) & ((ii >> sh_s) != (jj >> sh_s))
        ys = [_dot(x, jnp.where(off, m, 0.0)) for x, m in zip(xs, ms)]
        xs = [x - _dot(y, x) for x, y in zip(xs, ys)]
        s *= 2
    return xs


def _gated_norm(o, w, z):
    return o * lax.rsqrt(jnp.mean(o * o, axis=-1, keepdims=True) + RMS_EPS) * w * _silu(z)


def _chunk_rows(s, tb_rows, ci, c):
    r = s * tb_rows + ci * c
    if not isinstance(r, int):
        r = pl.multiple_of(r, c)
    return r


def _for_chunks(n_chunks, step):
    if n_chunks == 1:
        step(0, 0)
    else:
        lax.fori_loop(0, n_chunks, step, 0)


def _gdn_body(qkv_ref, z_ref, sm_ref, cbuf_ref, s0_ref, cw_ref, alog_ref, dtb_ref, nw_ref,
              o_ref, sout_ref, cout_ref, st, xc, act, gcs, us, wss, qgs, kds, aqs,
              *, nb, tb_rows, chunk, valid, n_tb):
    tb = pl.program_id(1)
    c = chunk
    n_heads = GDN_HEADS
    tail = CONV_WIDTH - 1
    pad = SUBLANE
    units = [(s, h) for s in range(nb) for h in range(n_heads)]

    @pl.when(tb == 0)
    def _():
        st[...] = s0_ref[...]
        for s in range(nb):
            xc[s, pad - tail:pad, :] = cbuf_ref[s]

    if n_tb > 1:
        @pl.when(tb > 0)
        def _():
            for s in range(nb):
                xc[s, pad - tail:pad, :] = xc[s, tb_rows + pad - tail:tb_rows + pad, :]

    for s in range(nb):
        xc[s, pad:pad + tb_rows, :] = qkv_ref[s * tb_rows:(s + 1) * tb_rows, :]

    slab = min(tb_rows, 64)
    for s in range(nb):
        for sl in range(tb_rows // slab):
            for cb in range(GDN_CONV_CH // 512):
                cs = slice(cb * 512, (cb + 1) * 512)
                lo = pad - tail + sl * slab
                acc = xc[s, lo:lo + slab, cs] * cw_ref[0:1, cs]
                for i in range(1, CONV_WIDTH):
                    acc = acc + xc[s, lo + i:lo + i + slab, cs] * cw_ref[i:i + 1, cs]
                act[s * tb_rows + sl * slab:s * tb_rows + (sl + 1) * slab, cs] = _silu(acc)

    ii = lax.broadcasted_iota(jnp.int32, (c, c), 0)
    jj = lax.broadcasted_iota(jnp.int32, (c, c), 1)
    lower = (ii >= jj)
    lower_f = lower.astype(F32)
    strict = (ii > jj)
    rowmask = None
    if valid < c:
        rowmask = lax.broadcasted_iota(jnp.int32, (c, 1), 0) < valid

    def hs(h, w):
        return slice(h * w, (h + 1) * w)

    n_chunks = tb_rows // c
    cpi = next(k for k in (GDN_CHUNKS_PER_TRIP, 2, 1) if n_chunks % k == 0)
    p1_units = [(g, h) for g in range(nb * cpi) for h in range(n_heads)]

    def phase1(ci, carry):
        rows, b_ts, gc_ts, gc_tts = [], [], [], []
        for g in range(nb * cpi):
            rr = pl.ds(_chunk_rows(g // cpi, tb_rows, ci * cpi + g % cpi, c), c)
            sm = sm_ref[rr, :]
            g_t = -jnp.exp(alog_ref[...]) * jax.nn.softplus(sm + dtb_ref[...])
            b_t = jax.nn.sigmoid(sm)
            if rowmask is not None:
                g_t = jnp.where(rowmask, g_t, 0.0)
                b_t = jnp.where(rowmask, b_t, 0.0)
            gc_t = _dot_hi(lower_f, g_t)
            gcs[rr, :] = gc_t
            rows.append(rr)
            b_ts.append(b_t)
            gc_ts.append(gc_t)
            gc_tts.append(gc_t.T)
        qn, kn, kb, vb = {}, {}, {}, {}
        for (s, h) in p1_units:
            q = act[rows[s], hs(h, GDN_DK)]
            k = act[rows[s], slice(GDN_QK_W + h * GDN_DK, GDN_QK_W + (h + 1) * GDN_DK)]
            v = act[rows[s], slice(2 * GDN_QK_W + h * GDN_DV, 2 * GDN_QK_W + (h + 1) * GDN_DV)]
            if rowmask is not None:
                q = jnp.where(rowmask, q, 0.0)
                k = jnp.where(rowmask, k, 0.0)
                v = jnp.where(rowmask, v, 0.0)
            qn[s, h] = q * lax.rsqrt(jnp.sum(q * q, axis=-1, keepdims=True) + L2_EPS) * (GDN_DK ** -0.5)
            kn[s, h] = k * lax.rsqrt(jnp.sum(k * k, axis=-1, keepdims=True) + L2_EPS)
            beta = b_ts[s][:, SM_B + h:SM_B + h + 1]
            kb[s, h] = kn[s, h] * beta
            vb[s, h] = v * beta
        s1 = {u: _dot_nt(jnp.concatenate([kb[u], qn[u]], axis=0), kn[u]) for u in p1_units}
        mm = []
        for (s, h) in p1_units:
            gcol = gc_ts[s][:, SM_A + h:SM_A + h + 1]
            grow = gc_tts[s][SM_A + h:SM_A + h + 1, :]
            dec = jnp.exp(jnp.where(lower, gcol - grow, -jnp.inf))
            mm.append(jnp.where(strict, s1[s, h][:c] * dec, 0.0))
            aqs[h, rows[s], :] = s1[s, h][c:] * dec
        tms = _tri_inv_all(mm, c, ii, jj)
        for (s, h), tm in zip(p1_units, tms):
            gcol = gc_ts[s][:, SM_A + h:SM_A + h + 1]
            eg = jnp.exp(gcol)
            uw = _dot(tm, jnp.concatenate([vb[s, h], kb[s, h] * eg], axis=1))
            us[rows[s], hs(h, GDN_DV)] = uw[:, :GDN_DV]
            wss[rows[s], hs(h, GDN_DV)] = uw[:, GDN_DV:]
            qgs[rows[s], hs(h, GDN_DK)] = qn[s, h] * eg
            kds[rows[s], hs(h, GDN_DK)] = kn[s, h] * jnp.exp(gcol[c - 1:c, :] - gcol)
        return carry

    def phase2(ci, carry):
        r0 = [_chunk_rows(s, tb_rows, ci, c) for s in range(nb)]
        rows = [pl.ds(r, c) for r in r0]
        ws = {(s, h): _dot(jnp.concatenate([wss[rows[s], hs(h, GDN_DV)], qgs[rows[s], hs(h, GDN_DK)]], axis=0),
                           st[s, h]) for (s, h) in units}
        v_new = {(s, h): us[rows[s], hs(h, GDN_DV)] - ws[s, h][:c] for (s, h) in units}
        o = {(s, h): ws[s, h][c:] + _dot(aqs[h, rows[s], :], v_new[s, h]) for (s, h) in units}
        upd = {(s, h): _dot_tn(kds[rows[s], hs(h, GDN_DK)], v_new[s, h]) for (s, h) in units}
        for (s, h) in units:
            g_last = gcs[pl.ds(r0[s] + c - 1, 1), SM_A + h:SM_A + h + 1]
            st[s, h] = st[s, h] * jnp.exp(g_last) + upd[s, h]
        for s in range(nb):
            o_ref[rows[s], :] = jnp.concatenate(
                [_gated_norm(o[s, h], nw_ref[...], z_ref[rows[s], hs(h, GDN_DV)]) for h in range(n_heads)], axis=1)
        return carry

    _for_chunks(n_chunks // cpi, phase1)
    _for_chunks(n_chunks, phase2)

    @pl.when(tb == n_tb - 1)
    def _():
        sout_ref[...] = st[...]
        last = tb_rows if valid == c else valid
        for s in range(nb):
            cout_ref[s] = xc[s, pad + last - tail:pad + last, :]


def _gdn(proj, n_seq, nb, t_len, tb_rows, chunk, valid, conv_buf, s0, conv_w, alog, dtb, nw):
    n_tb = t_len // tb_rows
    assert nb == 1 or n_tb == 1
    rows = nb * tb_rows

    def rowblk(b, t):
        return b * n_tb + t

    body = functools.partial(_gdn_body, nb=nb, tb_rows=tb_rows, chunk=chunk, valid=valid, n_tb=n_tb)
    return pl.pallas_call(
        body,
        grid=(n_seq // nb, n_tb),
        in_specs=[
            pl.BlockSpec((rows, GDN_CONV_CH), lambda b, t: (rowblk(b, t), COL_QKV // GDN_CONV_CH)),
            pl.BlockSpec((rows, GDN_V_W), lambda b, t: (rowblk(b, t), COL_Z // GDN_V_W)),
            pl.BlockSpec((rows, SM_W), lambda b, t: (rowblk(b, t), COL_SM // SM_W)),
            pl.BlockSpec((nb, CONV_WIDTH - 1, GDN_CONV_CH), lambda b, t: (b, 0, 0)),
            pl.BlockSpec((nb, GDN_HEADS, GDN_DK, GDN_DV), lambda b, t: (b, 0, 0, 0)),
            pl.BlockSpec((CONV_WIDTH, GDN_CONV_CH), lambda b, t: (0, 0)),
            pl.BlockSpec((1, SM_W), lambda b, t: (0, 0)),
            pl.BlockSpec((1, SM_W), lambda b, t: (0, 0)),
            pl.BlockSpec((1, GDN_DV), lambda b, t: (0, 0)),
        ],
        out_specs=[
            pl.BlockSpec((rows, GDN_V_W), lambda b, t: (rowblk(b, t), 0)),
            pl.BlockSpec((nb, GDN_HEADS, GDN_DK, GDN_DV), lambda b, t: (b, 0, 0, 0)),
            pl.BlockSpec((nb, CONV_WIDTH - 1, GDN_CONV_CH), lambda b, t: (b, 0, 0)),
        ],
        out_shape=[
            jax.ShapeDtypeStruct((n_seq * t_len, GDN_V_W), F32),
            jax.ShapeDtypeStruct((n_seq, GDN_HEADS, GDN_DK, GDN_DV), F32),
            jax.ShapeDtypeStruct((n_seq, CONV_WIDTH - 1, GDN_CONV_CH), F32),
        ],
        scratch_shapes=[
            pltpu.VMEM((nb, GDN_HEADS, GDN_DK, GDN_DV), F32),
            pltpu.VMEM((nb, tb_rows + SUBLANE, GDN_CONV_CH), F32),
            pltpu.VMEM((rows, GDN_CONV_CH), F32),
            pltpu.VMEM((rows, SM_W), F32),
            pltpu.VMEM((rows, GDN_V_W), F32),
            pltpu.VMEM((rows, GDN_V_W), F32),
            pltpu.VMEM((rows, GDN_QK_W), F32),
            pltpu.VMEM((rows, GDN_QK_W), F32),
            pltpu.VMEM((GDN_HEADS, rows, chunk), F32),
        ],
        compiler_params=pltpu.CompilerParams(dimension_semantics=("arbitrary", "arbitrary"),
                                             vmem_limit_bytes=VMEM_LIMIT),
        name="gdn_mixer",
    )(proj, proj, proj, conv_buf, s0, conv_w, alog, dtb, nw)


def _gla_body(q_ref, k_ref, v_ref, go_ref, sm_ref, s0_ref, wgk_ref, bgk_ref, nw_ref,
              o_ref, sout_ref, st, qes, ois, upds, decs, *, nb, tb_rows, chunk, valid, n_tb):
    tb = pl.program_id(1)
    c = chunk
    n_heads = GLA_HEADS
    units = [(s, h) for s in range(nb) for h in range(n_heads)]

    @pl.when(tb == 0)
    def _():
        st[...] = s0_ref[...]

    ii = lax.broadcasted_iota(jnp.int32, (c, c), 0)
    jj = lax.broadcasted_iota(jnp.int32, (c, c), 1)
    lower = (ii >= jj)
    lower_f = lower.astype(F32)
    rid = lax.broadcasted_iota(jnp.int32, (c, 1), 0)
    rowmask = (rid < valid) if valid < c else None
    n_sub = max(c // 16, 1)
    sub = c // n_sub

    n_chunks = tb_rows // c
    cpi = next(k for k in (GLA_CHUNKS_PER_TRIP, 2, 1) if n_chunks % k == 0)
    p1_units = [(g, h) for g in range(nb * cpi) for h in range(n_heads)]

    def phase1(ci, carry):
        rows, slots, bcs, bc_ts = [], [], [], []
        for g in range(nb * cpi):
            chunk_idx = ci * cpi + g % cpi
            rr = pl.ds(_chunk_rows(g // cpi, tb_rows, chunk_idx, c), c)
            slots.append((g // cpi) * n_chunks + chunk_idx)
            gk = jax.nn.log_sigmoid(_dot(sm_ref[rr, :], wgk_ref[...]) + bgk_ref[...]) / GLA_GATE_NORMALIZER
            if rowmask is not None:
                gk = jnp.where(rowmask, gk, 0.0)
            bc = _dot_hi(lower_f, gk)
            rows.append(rr)
            bcs.append(bc)
            bc_ts.append(bc.T)
        q, k, v, bch = {}, {}, {}, {}
        for (s, h) in p1_units:
            ks = slice(h * GLA_DK, (h + 1) * GLA_DK)
            vs = slice(h * GLA_DV, (h + 1) * GLA_DV)
            q[s, h] = q_ref[rows[s], ks] * (GLA_DK ** -0.5)
            kk = k_ref[rows[s], ks]
            vv = v_ref[rows[s], vs]
            if rowmask is not None:
                kk = jnp.where(rowmask, kk, 0.0)
                vv = jnp.where(rowmask, vv, 0.0)
            k[s, h], v[s, h] = kk, vv
            bch[s, h] = bcs[s][:, ks]
        for (g, h) in p1_units:
            qes[h, rows[g], :] = q[g, h] * jnp.exp(bch[g, h])
        a = {}
        for u in p1_units:
            q_parts, k_parts = [], []
            for sb in range(n_sub):
                ref_row = bch[u][sb * sub:sb * sub + 1, :]
                in_blk = (rid >= sb * sub) & (rid < (sb + 1) * sub)
                q_parts.append(jnp.where(in_blk, q[u] * jnp.exp(jnp.where(in_blk, bch[u] - ref_row, 0.0)), 0.0))
                k_parts.append(k[u] * jnp.exp(jnp.where(rid < (sb + 1) * sub, ref_row - bch[u], 0.0)))
            q_hat = jnp.concatenate(q_parts, axis=1) if n_sub > 1 else q_parts[0]
            k_hat = jnp.concatenate(k_parts, axis=1) if n_sub > 1 else k_parts[0]
            a[u] = jnp.where(lower, _dot_nt(q_hat, k_hat), 0.0)
        upd = {u: _dot_tn(k[u] * jnp.exp(bch[u][c - 1:c, :] - bch[u]), v[u]) for u in p1_units}
        o_intra = {u: _dot(a[u], v[u]) for u in p1_units}
        for (g, h) in p1_units:
            dec_col = bc_ts[g][h * GLA_DK:(h + 1) * GLA_DK, c - 1:c]
            decs[slots[g], h] = jnp.broadcast_to(jnp.exp(dec_col), (GLA_DK, GLA_DV))
            upds[slots[g], h] = upd[g, h]
            ois[rows[g], h * GLA_DV:(h + 1) * GLA_DV] = o_intra[g, h]
        return carry

    def phase2(ci, carry):
        rows = [pl.ds(_chunk_rows(s, tb_rows, ci, c), c) for s in range(nb)]
        o = {(s, h): ois[rows[s], h * GLA_DV:(h + 1) * GLA_DV] + _dot(qes[h, rows[s], :], st[s, h])
             for (s, h) in units}
        for (s, h) in units:
            st[s, h] = decs[s * n_chunks + ci, h] * st[s, h] + upds[s * n_chunks + ci, h]
        for s in range(nb):
            o_ref[rows[s], :] = jnp.concatenate(
                [_gated_norm(o[s, h], nw_ref[...], go_ref[rows[s], h * GLA_DV:(h + 1) * GLA_DV])
                 for h in range(n_heads)], axis=1)
        return carry

    _for_chunks(n_chunks // cpi, phase1)
    _for_chunks(n_chunks, phase2)

    @pl.when(tb == n_tb - 1)
    def _():
        sout_ref[...] = st[...]


def _gla(proj, n_seq, nb, t_len, tb_rows, chunk, valid, s0, wgk, bgk, nw):
    n_tb = t_len // tb_rows
    assert nb == 1 or n_tb == 1
    rows = nb * tb_rows

    def rowblk(b, t):
        return b * n_tb + t

    body = functools.partial(_gla_body, nb=nb, tb_rows=tb_rows, chunk=chunk, valid=valid, n_tb=n_tb)
    return pl.pallas_call(
        body,
        grid=(n_seq // nb, n_tb),
        in_specs=[
            pl.BlockSpec((rows, GLA_QK_W), lambda b, t: (rowblk(b, t), COL_GQ // GLA_QK_W)),
            pl.BlockSpec((rows, GLA_QK_W), lambda b, t: (rowblk(b, t), COL_GK // GLA_QK_W)),
            pl.BlockSpec((rows, GLA_V_W), lambda b, t: (rowblk(b, t), COL_GV // GLA_V_W)),
            pl.BlockSpec((rows, GLA_V_W), lambda b, t: (rowblk(b, t), COL_GG // GLA_V_W)),
            pl.BlockSpec((rows, SM_W), lambda b, t: (rowblk(b, t), COL_SM // SM_W)),
            pl.BlockSpec((nb, GLA_HEADS, GLA_DK, GLA_DV), lambda b, t: (b, 0, 0, 0)),
            pl.BlockSpec((SM_W, GLA_QK_W), lambda b, t: (0, 0)),
            pl.BlockSpec((1, GLA_QK_W), lambda b, t: (0, 0)),
            pl.BlockSpec((1, GLA_DV), lambda b, t: (0, 0)),
        ],
        out_specs=[
            pl.BlockSpec((rows, GLA_V_W), lambda b, t: (rowblk(b, t), 0)),
            pl.BlockSpec((nb, GLA_HEADS, GLA_DK, GLA_DV), lambda b, t: (b, 0, 0, 0)),
        ],
        out_shape=[
            jax.ShapeDtypeStruct((n_seq * t_len, GLA_V_W), F32),
            jax.ShapeDtypeStruct((n_seq, GLA_HEADS, GLA_DK, GLA_DV), F32),
        ],
        scratch_shapes=[
            pltpu.VMEM((nb, GLA_HEADS, GLA_DK, GLA_DV), F32),
            pltpu.VMEM((GLA_HEADS, rows, GLA_DK), F32),
            pltpu.VMEM((rows, GLA_V_W), F32),
            pltpu.VMEM((rows // chunk, GLA_HEADS, GLA_DK, GLA_DV), F32),
            pltpu.VMEM((rows // chunk, GLA_HEADS, GLA_DK, GLA_DV), F32),
        ],
        compiler_params=pltpu.CompilerParams(dimension_semantics=("arbitrary", "arbitrary"),
                                             vmem_limit_bytes=VMEM_LIMIT),
        name="gla_mixer",
    )(proj, proj, proj, proj, proj, s0, wgk, bgk, nw)


def _outproj_body(ogp_ref, ogs_ref, olp_ref, ols_ref, xp_ref, xs_ref, wo_ref, g_ref, wr_ref, br_ref,
                  x1_ref, h2_ref, rt_ref, cnt_ref, base, *, n_p_blocks):
    i = pl.program_id(0)

    @pl.when(i == 0)
    def _():
        base[...] = jnp.zeros_like(base)

    o = jnp.concatenate([_group_pick(i, n_p_blocks, ogp_ref, ogs_ref),
                         _group_pick(i, n_p_blocks, olp_ref, ols_ref)], axis=1)
    x1 = _group_pick(i, n_p_blocks, xp_ref, xs_ref) + jnp.dot(o.astype(BF16), wo_ref[...],
                                                               preferred_element_type=F32)
    x1_ref[...] = x1
    h = _rms(x1, g_ref[...])
    _store_token_tiles(h2_ref, h)
    logits = _dot_3pass(h, wr_ref[...]) + br_ref[...]

    tm = logits.shape[0]
    lt = logits.T[:N_EXPERTS]
    eid = lax.broadcasted_iota(jnp.int32, (N_EXPERTS, tm), 0)
    work = lt
    sel = jnp.zeros((N_EXPERTS, tm), F32)
    hits, ids, vals = [], [], []
    for _ in range(TOP_K):
        m = jnp.max(work, axis=0, keepdims=True)
        idx = jnp.min(jnp.where(work == m, eid, N_EXPERTS), axis=0, keepdims=True)
        hit = eid == idx
        hits.append(hit)
        ids.append(idx)
        vals.append(m)
        work = jnp.where(hit, -jnp.inf, work)
        sel = sel + hit.astype(F32)
    exps = [jnp.exp(v - vals[0]) for v in vals]
    den = exps[0]
    for e in exps[1:]:
        den = den + e
    gates = [e / den for e in exps]

    ri = lax.broadcasted_iota(jnp.int32, (tm, tm), 0)
    ci = lax.broadcasted_iota(jnp.int32, (tm, tm), 1)
    before = _dot(sel, (ri < ci).astype(F32)) + base[...]
    ranks = [jnp.sum(jnp.where(hit, before, 0.0), axis=0, keepdims=True) for hit in hits]
    base[...] = base[...] + jnp.sum(sel, axis=1, keepdims=True)
    cnt_ref[...] = base[...]

    row = lax.broadcasted_iota(jnp.int32, (LANE, tm), 0)
    rec = jnp.zeros((LANE, tm), F32)
    for k in range(TOP_K):
        rec = jnp.where(row == k, ids[k].astype(F32), rec)
        rec = jnp.where(row == TOP_K + k, ranks[k], rec)
        rec = jnp.where(row == 2 * TOP_K + k, gates[k], rec)
    rt_ref[...] = rec.T


def _outproj(og_p, og_s, ol_p, ol_s, x_p, x_s, wo, g, wr, br):
    n_p_blocks, n_s_blocks = x_p.shape[0] // ROW_TILE, x_s.shape[0] // ROW_TILE
    n = x_p.shape[0] + x_s.shape[0]
    return pl.pallas_call(
        functools.partial(_outproj_body, n_p_blocks=n_p_blocks),
        grid=(n_p_blocks + n_s_blocks,),
        in_specs=_group_specs(ROW_TILE, GDN_V_W, n_p_blocks) + _group_specs(ROW_TILE, GLA_V_W, n_p_blocks)
        + _group_specs(ROW_TILE, D_MODEL, n_p_blocks) + [
            pl.BlockSpec((D_MODEL, D_MODEL), lambda i: (0, 0)),
            pl.BlockSpec((1, D_MODEL), lambda i: (0, 0)),
            pl.BlockSpec((D_MODEL, LANE), lambda i: (0, 0)),
            pl.BlockSpec((1, LANE), lambda i: (0, 0)),
        ],
        out_specs=[
            pl.BlockSpec((ROW_TILE, D_MODEL), lambda i: (i, 0)),
            pl.BlockSpec((ROW_TILE * TOK_TILES, LANE), lambda i: (i, 0)),
            pl.BlockSpec((ROW_TILE, LANE), lambda i: (i, 0)),
            pl.BlockSpec((N_EXPERTS, 1), lambda i: (0, 0)),
        ],
        out_shape=[
            jax.ShapeDtypeStruct((n, D_MODEL), F32),
            jax.ShapeDtypeStruct((n * TOK_TILES, LANE), F32),
            jax.ShapeDtypeStruct((n, LANE), F32),
            jax.ShapeDtypeStruct((N_EXPERTS, 1), F32),
        ],
        scratch_shapes=[pltpu.VMEM((N_EXPERTS, 1), F32)],
        compiler_params=pltpu.CompilerParams(dimension_semantics=("arbitrary",),
                                             vmem_limit_bytes=VMEM_LIMIT),
        name="out_proj",
    )(og_p, og_s, ol_p, ol_s, x_p, x_s, wo, g, wr, br)


def _store_token_tiles(ref2d, val):
    rows = val.shape[0]
    for c in range(TOK_TILES):
        ref2d[pl.ds(c, rows, stride=TOK_TILES), :] = val[:, c * LANE:(c + 1) * LANE]


def _load_token_tiles(ref2d, first_row, rows):
    return jnp.concatenate(
        [ref2d[pl.ds(first_row * TOK_TILES + c, rows, stride=TOK_TILES), :] for c in range(TOK_TILES)], axis=1)


def _gather_rows(src_tiles, idx_ref, n_rows, dst2d, sem, priorities):
    def issue(j, carry):
        for u in range(DMA_ISSUE_UNROLL):
            r = j * DMA_ISSUE_UNROLL + u
            dst = dst2d.at[pl.ds(pl.multiple_of(r * TOK_TILES, TOK_TILES), TOK_TILES), :]
            pltpu.make_async_copy(src_tiles.at[idx_ref[0, r]], dst, sem).start(
                priority=priorities[u % len(priorities)])
        return carry

    lax.fori_loop(0, n_rows // DMA_ISSUE_UNROLL, issue, 0)


def _wait_rows(src2d, n_rows, dst2d, sem):
    pltpu.make_async_copy(src2d.at[pl.ds(0, n_rows * TOK_TILES), :], dst2d, sem).wait()


def _expert_weight_copies(e, ws, wup_hbm, wdn_hbm, wup_buf, wdn_buf, wsems):
    return (pltpu.make_async_copy(wup_hbm.at[e], wup_buf.at[ws], wsems.at[ws]),
            pltpu.make_async_copy(wdn_hbm.at[e], wdn_buf.at[ws], wsems.at[ws]))


def _expert_body(be_ref, nu_ref, first_ref, wslot_ref, next_ref, idx_ref, idx_next_ref, h_tiles, h_2d,
                 wup_hbm, bup_ref, wdn_hbm, bdn_ref, y_ref, xbuf, sems, wup_buf, wdn_buf, wsems):
    i = pl.program_id(0)
    n_used = nu_ref[0]
    slot = i % 2
    ws = wslot_ref[i]
    weight_copies = functools.partial(_expert_weight_copies, wup_hbm=wup_hbm, wdn_hbm=wdn_hbm, wup_buf=wup_buf,
                                      wdn_buf=wdn_buf, wsems=wsems)

    @pl.when((i == 0) & (n_used > 0))
    def _():
        for cp in weight_copies(be_ref[0], ws):
            cp.start(priority=EXPERT_WEIGHT_QUEUE)
        _gather_rows(h_tiles, idx_ref, EXPERT_ROWS, xbuf.at[0], sems.at[0], EXPERT_GATHER_QUEUES)

    @pl.when(i < n_used)
    def _():
        @pl.when(first_ref[i] == 1)
        def _():
            for cp in weight_copies(be_ref[i], ws):
                cp.wait()

            @pl.when(next_ref[i] >= 0)
            def _():
                for cp in weight_copies(next_ref[i], 1 - ws):
                    cp.start(priority=EXPERT_WEIGHT_QUEUE)

        _wait_rows(h_2d, EXPERT_ROWS, xbuf.at[slot], sems.at[slot])

        @pl.when(i + 1 < n_used)
        def _():
            _gather_rows(h_tiles, idx_next_ref, EXPERT_ROWS, xbuf.at[1 - slot], sems.at[1 - slot],
                         EXPERT_GATHER_QUEUES)

        gu = _dot(_load_token_tiles(xbuf.at[slot], 0, EXPERT_ROWS), wup_buf[ws]) + bup_ref[...]
        gate = jnp.minimum(gu[:, :D_FF], SWIGLU_LIMIT)
        up = jnp.clip(gu[:, D_FF:], -SWIGLU_LIMIT, SWIGLU_LIMIT)
        a = (up + 1.0) * gate * jax.nn.sigmoid(SWIGLU_ALPHA * gate)
        _store_token_tiles(y_ref, _dot(a, wdn_buf[ws]) + bdn_ref[...])

    @pl.when(i >= n_used)
    def _():
        y_ref[...] = jnp.zeros_like(y_ref)


def _experts(block_meta, src_tok, h2_2d, w_up, b_up, w_down, b_down):
    n_blocks = block_meta[0].shape[0]
    src_blocks = src_tok.reshape(n_blocks, 1, EXPERT_ROWS)
    grid_spec = pltpu.PrefetchScalarGridSpec(
        num_scalar_prefetch=len(block_meta),
        grid=(n_blocks,),
        in_specs=[
            pl.BlockSpec((None, 1, EXPERT_ROWS), lambda i, *_: (i, 0, 0), memory_space=pltpu.SMEM),
            pl.BlockSpec((None, 1, EXPERT_ROWS), lambda i, *_: (jnp.minimum(i + 1, n_blocks - 1), 0, 0),
                         memory_space=pltpu.SMEM),
            pl.BlockSpec(memory_space=pl.ANY),
            pl.BlockSpec(memory_space=pl.ANY),
            pl.BlockSpec(memory_space=pl.ANY),
            pl.BlockSpec((None, 1, 2 * D_FF), lambda i, be, *_: (be[i], 0, 0)),
            pl.BlockSpec(memory_space=pl.ANY),
            pl.BlockSpec((None, 1, D_MODEL), lambda i, be, *_: (be[i], 0, 0)),
        ],
        out_specs=pl.BlockSpec((EXPERT_ROWS * TOK_TILES, LANE), lambda i, *_: (i, 0)),
        scratch_shapes=[
            pltpu.VMEM((2, EXPERT_ROWS * TOK_TILES, LANE), F32),
            pltpu.SemaphoreType.DMA((2,)),
            pltpu.VMEM((2, D_MODEL, 2 * D_FF), F32),
            pltpu.VMEM((2, D_FF, D_MODEL), F32),
            pltpu.SemaphoreType.DMA((2,)),
        ],
    )
    return pl.pallas_call(
        _expert_body,
        grid_spec=grid_spec,
        out_shape=jax.ShapeDtypeStruct((n_blocks * EXPERT_ROWS * TOK_TILES, LANE), F32),
        compiler_params=pltpu.CompilerParams(dimension_semantics=("arbitrary",),
                                             vmem_limit_bytes=VMEM_LIMIT),
        name="experts",
    )(*block_meta, src_blocks, src_blocks, h2_2d.reshape(-1, TOK_TILES, LANE), h2_2d, w_up,
      b_up.reshape(N_EXPERTS, 1, 2 * D_FF), w_down, b_down.reshape(N_EXPERTS, 1, D_MODEL))


def _combine_body(dest_ref, dest_next_ref, rt_ref, y_tiles, y_2d, x1_ref, g_ref, op_ref, os_ref, ybuf, sems,
                  *, n_p_blocks):
    i = pl.program_id(0)
    slot = i % 2
    n_rows = TOP_K * COMBINE_ROWS

    @pl.when(i == 0)
    def _():
        _gather_rows(y_tiles, dest_ref, n_rows, ybuf.at[0], sems.at[0], COMBINE_GATHER_QUEUES)

    _wait_rows(y_2d, n_rows, ybuf.at[slot], sems.at[slot])

    @pl.when(i + 1 < pl.num_programs(0))
    def _():
        _gather_rows(y_tiles, dest_next_ref, n_rows, ybuf.at[1 - slot], sems.at[1 - slot], COMBINE_GATHER_QUEUES)

    buf = ybuf.at[slot]
    moe = _load_token_tiles(buf, 0, COMBINE_ROWS) * rt_ref[:, 2 * TOP_K:2 * TOP_K + 1]
    for k in range(1, TOP_K):
        moe = moe + _load_token_tiles(buf, k * COMBINE_ROWS, COMBINE_ROWS) * rt_ref[:, 2 * TOP_K + k:2 * TOP_K + k + 1]
    res = _rms(x1_ref[...] + moe, g_ref[...])

    @pl.when(i < n_p_blocks)
    def _():
        op_ref[...] = res

    @pl.when(i >= n_p_blocks)
    def _():
        os_ref[...] = res


def _combine(dest, rt, y_2d, x1, g, n_p):
    n = x1.shape[0]
    n_blk = n // COMBINE_ROWS
    n_p_blocks = n_p // COMBINE_ROWS
    dest_blocks = dest.reshape(n_blk, COMBINE_ROWS, TOP_K).transpose(0, 2, 1).reshape(n_blk, 1, TOP_K * COMBINE_ROWS)
    return pl.pallas_call(
        functools.partial(_combine_body, n_p_blocks=n_p_blocks),
        grid=(n_blk,),
        in_specs=[
            pl.BlockSpec((None, 1, COMBINE_ROWS * TOP_K), lambda i: (i, 0, 0), memory_space=pltpu.SMEM),
            pl.BlockSpec((None, 1, COMBINE_ROWS * TOP_K), lambda i: (jnp.minimum(i + 1, n_blk - 1), 0, 0),
                         memory_space=pltpu.SMEM),
            pl.BlockSpec((COMBINE_ROWS, LANE), lambda i: (i, 0)),
            pl.BlockSpec(memory_space=pl.ANY),
            pl.BlockSpec(memory_space=pl.ANY),
            pl.BlockSpec((COMBINE_ROWS, D_MODEL), lambda i: (i, 0)),
            pl.BlockSpec((1, D_MODEL), lambda i: (0, 0)),
        ],
        out_specs=_group_specs(COMBINE_ROWS, D_MODEL, n_p_blocks),
        out_shape=[jax.ShapeDtypeStruct((n_p, D_MODEL), F32), jax.ShapeDtypeStruct((n - n_p, D_MODEL), F32)],
        scratch_shapes=[pltpu.VMEM((2, TOP_K * COMBINE_ROWS * TOK_TILES, LANE), F32),
                        pltpu.SemaphoreType.DMA((2,))],
        compiler_params=pltpu.CompilerParams(dimension_semantics=("arbitrary",),
                                             vmem_limit_bytes=VMEM_LIMIT),
        name="combine",
    )(dest_blocks, dest_blocks, rt, y_2d.reshape(-1, TOK_TILES, LANE), y_2d, x1, g)


def _invert_body(dest_ref, tbl_ref, zbuf, sem):
    i = pl.program_id(0)

    @pl.when(i == 0)
    def _():
        zbuf[...] = jnp.zeros_like(zbuf)
        fill = pltpu.make_async_copy(zbuf, tbl_ref, sem)
        fill.start()
        fill.wait()

    toks_per_trip = DMA_ISSUE_UNROLL // TOP_K

    def body(j, carry):
        for u in range(toks_per_trip):
            tok = i * (INVERT_BLOCK // TOP_K) + j * toks_per_trip + u
            for k in range(TOP_K):
                tbl_ref[dest_ref[0, (j * toks_per_trip + u) * TOP_K + k]] = tok
        return carry

    lax.fori_loop(0, INVERT_BLOCK // DMA_ISSUE_UNROLL, body, 0)


def _invert(dest, n_rows):
    n_assign = dest.shape[0]
    n_blk = n_assign // INVERT_BLOCK
    return pl.pallas_call(
        _invert_body,
        grid=(n_blk,),
        in_specs=[pl.BlockSpec((None, 1, INVERT_BLOCK), lambda i: (i, 0, 0), memory_space=pltpu.SMEM)],
        out_specs=pl.BlockSpec(memory_space=pltpu.SMEM),
        out_shape=jax.ShapeDtypeStruct((n_rows,), jnp.int32),
        scratch_shapes=[pltpu.VMEM((n_rows,), jnp.int32), pltpu.SemaphoreType.DMA(())],
        compiler_params=pltpu.CompilerParams(dimension_semantics=("arbitrary",)),
        name="invert",
    )(dest.reshape(n_blk, 1, INVERT_BLOCK))


def _route(rt, cnt):
    n = rt.shape[0]
    n_assign = n * TOP_K
    flat_e = rt[:, 0:TOP_K].astype(jnp.int32).reshape(-1)
    rank = rt[:, TOP_K:2 * TOP_K].astype(jnp.int32).reshape(-1)
    counts = cnt[:, 0].astype(jnp.int32)
    padded = (counts + EXPERT_ROWS - 1) // EXPERT_ROWS * EXPERT_ROWS
    pend = jnp.cumsum(padded)
    pstart = pend - padded
    dest = (pstart[flat_e] + rank).astype(jnp.int32)
    n_rows = n_assign + N_EXPERTS * EXPERT_ROWS
    n_blocks = n_rows // EXPERT_ROWS
    src_tok = _invert(dest, n_rows)
    block_first = jnp.arange(n_blocks, dtype=jnp.int32) * EXPERT_ROWS
    block_e = jnp.minimum(jnp.sum((pend[None, :] <= block_first[:, None]).astype(jnp.int32), axis=1),
                          N_EXPERTS - 1).astype(jnp.int32)
    n_used = (pend[-1] // EXPERT_ROWS).astype(jnp.int32).reshape(1)
    ids = jnp.arange(N_EXPERTS, dtype=jnp.int32)
    present = padded > 0
    group = jnp.cumsum(present.astype(jnp.int32)) - 1
    later = jnp.flip(lax.cummin(jnp.flip(jnp.where(present, ids, N_EXPERTS))))
    next_e = jnp.concatenate([later[1:], jnp.full((1,), N_EXPERTS, jnp.int32)])
    next_e = jnp.where(next_e >= N_EXPERTS, -1, next_e)
    prev_block_e = jnp.concatenate([jnp.full((1,), -1, jnp.int32), block_e[:-1]])
    first = ((block_e != prev_block_e) & (jnp.arange(n_blocks) < n_used[0])).astype(jnp.int32)
    wslot = (group[block_e] % 2).astype(jnp.int32)
    return dest, src_tok, (block_e, n_used, first, wslot, next_e[block_e].astype(jnp.int32))


def _pad_lanes(v, width):
    return jnp.zeros((1, width), F32).at[0, :v.shape[0]].set(v.astype(F32))


def kernel(x_prompt, x_sample, state_gdn_conv, state_gdn, state_gla, rms_mix_w, w_in, conv_w, gdn_a_log,
           gdn_dt_bias, gdn_norm_w, gla_gk_w, gla_gk_b, gla_norm_w, w_out, rms_ffn_w, w_router, b_router,
           w_up, b_up, w_down, b_down, rms_final_w):
    bp, tp, d = x_prompt.shape
    bs, ts, _ = x_sample.shape
    n_p, n_s = bp * tp, bs * ts
    assert d == D_MODEL and state_gdn.shape[0] == 1, "single-layer kernel"
    assert tp >= CONV_WIDTH - 1 and ts >= CONV_WIDTH - 1, "new conv state is taken from the new tokens only"
    l = 0

    wi = w_in[l]
    a0 = GDN_CONV_CH + GDN_V_W
    g0 = a0 + 2 * GDN_HEADS
    lr0 = g0 + 2 * GLA_QK_W + 2 * GLA_V_W
    small = jnp.concatenate([wi[:, a0:a0 + 2 * GDN_HEADS], wi[:, lr0:lr0 + GLA_GATE_RANK],
                             jnp.zeros((d, SM_W - 2 * GDN_HEADS - GLA_GATE_RANK), F32)], axis=1)
    w_big = jnp.concatenate([wi[:, :a0], wi[:, g0:lr0], small], axis=1).astype(BF16)
    alog = _pad_lanes(gdn_a_log[l], SM_W)
    dtb = _pad_lanes(gdn_dt_bias[l], SM_W)
    wgk = jnp.zeros((SM_W, GLA_QK_W), F32).at[SM_LR:SM_LR + GLA_GATE_RANK].set(gla_gk_w[l])
    wr = jnp.zeros((d, LANE), F32).at[:, :N_EXPERTS].set(w_router[l])
    br = jnp.full((1, LANE), -1e30, F32).at[0, :N_EXPERTS].set(b_router[l])

    assert n_p % ROW_TILE == 0 and n_s % ROW_TILE == 0
    x_p, x_s = x_prompt.reshape(n_p, d), x_sample.reshape(n_s, d)
    proj = _inproj(x_p, x_s, rms_mix_w[l][None, :], w_big)

    tb_p = PROMPT_TIME_BLOCK
    zeros_conv = jnp.zeros((bp, CONV_WIDTH - 1, GDN_CONV_CH), F32)
    og_p, gdn_p, conv_p = _gdn(proj, bp, 1, tp, tb_p, CHUNK, CHUNK, zeros_conv,
                               jnp.zeros((bp, GDN_HEADS, GDN_DK, GDN_DV), F32), conv_w[l], alog, dtb,
                               gdn_norm_w[l][None, :])
    ol_p, gla_p = _gla(proj, bp, 1, tp, tb_p, CHUNK, CHUNK, jnp.zeros((bp, GLA_HEADS, GLA_DK, GLA_DV), F32),
                       wgk, gla_gk_b[l][None, :], gla_norm_w[l][None, :])

    ts_pad = SUBLANE
    nb_s = SAMPLE_SEQS_PER_STEP
    proj_s = proj[n_p:].reshape(bs, ts, PROJ_W)
    proj_sp = jnp.pad(proj_s, ((0, 0), (0, ts_pad - ts), (0, 0))).reshape(bs * ts_pad, PROJ_W)
    og_s, gdn_s, conv_s = _gdn(proj_sp, bs, nb_s, ts_pad, ts_pad, ts_pad, ts, state_gdn_conv[l], state_gdn[l],
                               conv_w[l], alog, dtb, gdn_norm_w[l][None, :])
    ol_s, gla_s = _gla(proj_sp, bs, nb_s, ts_pad, ts_pad, ts_pad, ts, state_gla[l], wgk, gla_gk_b[l][None, :],
                       gla_norm_w[l][None, :])
    og_s = og_s.reshape(bs, ts_pad, GDN_V_W)[:, :ts].reshape(n_s, GDN_V_W)
    ol_s = ol_s.reshape(bs, ts_pad, GLA_V_W)[:, :ts].reshape(n_s, GLA_V_W)

    x1, h2, rt, cnt = _outproj(og_p, og_s, ol_p, ol_s, x_p, x_s, w_out[l].astype(BF16), rms_ffn_w[l][None, :],
                               wr, br)

    dest, src_tok, block_meta = _route(rt, cnt)
    y_rows = _experts(block_meta, src_tok, h2, w_up[l], b_up[l], w_down[l], b_down[l])
    y_p, y_s = _combine(dest, rt, y_rows, x1, rms_final_w[None, :], n_p)
    y_prompt = y_p.reshape(bp, tp, d)
    y_sample = y_s.reshape(bs, ts, d)
    return (y_prompt, y_sample, conv_p[None], gdn_p[None], gla_p[None], conv_s[None], gdn_s[None], gla_s[None])
```

```python
import functools

import jax
import jax.numpy as jnp
from jax import lax
from jax.experimental import pallas as pl
from jax.experimental.pallas import tpu as pltpu
from jax.experimental.pallas import tpu_sc as plsc

F32 = jnp.float32
BF16 = jnp.bfloat16
HI = lax.Precision.HIGHEST

D_MODEL = 1024
GDN_HEADS = 4
GDN_DK = 128
GDN_DV = 128
GLA_HEADS = 4
GLA_DK = 64
GLA_DV = 128
GLA_GATE_RANK = 16
GLA_GATE_NORMALIZER = 16.0
CONV_WIDTH = 4
CHUNK = 64
N_EXPERTS = 32
TOP_K = 4
D_FF = 1024
SWIGLU_LIMIT = 7.0
SWIGLU_ALPHA = 1.702
RMS_EPS = 1e-6
L2_EPS = 1e-6

GDN_QK_W = GDN_HEADS * GDN_DK
GDN_V_W = GDN_HEADS * GDN_DV
GDN_CONV_CH = 2 * GDN_QK_W + GDN_V_W
GLA_QK_W = GLA_HEADS * GLA_DK
GLA_V_W = GLA_HEADS * GLA_DV

COL_QKV = 0
COL_Z = 1536
COL_GQ = 2048
COL_GK = 2304
COL_GV = 2560
COL_GG = 3072
COL_SM = 3584
SM_W = 128
PROJ_W = COL_SM + SM_W
SM_A, SM_B, SM_LR = 0, 4, 8

LANE = 128
SUBLANE = 8
TOK_TILES = D_MODEL // LANE
ROW_TILE = 256
EXPERT_ROWS = 256
EXPERT_WEIGHT_QUEUE = 1
COMBINE_ROWS = 128
DMA_ISSUE_UNROLL = 8
GDN_CHUNKS_PER_TRIP = 4
GLA_CHUNKS_PER_TRIP = 4
COMBINE_GATHER_QUEUES = (0, 1)
PROMPT_TIME_BLOCK = 512
SAMPLE_SEQS_PER_STEP = 8
VMEM_LIMIT = 56 * 1024 * 1024


def _dot(a, b):
    return jnp.dot(a.astype(BF16), b.astype(BF16), preferred_element_type=F32)


def _dot_nt(a, b):
    return lax.dot_general(a.astype(BF16), b.astype(BF16), (((1,), (1,)), ((), ())),
                           preferred_element_type=F32)


def _dot_tn(a, b):
    return lax.dot_general(a.astype(BF16), b.astype(BF16), (((0,), (0,)), ((), ())),
                           preferred_element_type=F32)


def _dot_hi(a, b):
    return jnp.dot(a, b, precision=HI, preferred_element_type=F32)


def _dot_3pass(a, b):
    a_hi = a.astype(BF16)
    b_hi = b.astype(BF16)
    a_lo = (a - a_hi.astype(F32)).astype(BF16)
    b_lo = (b - b_hi.astype(F32)).astype(BF16)

    def mm(x, y):
        return jnp.dot(x, y, preferred_element_type=F32)

    return (mm(a_lo, b_hi) + mm(a_hi, b_lo)) + mm(a_hi, b_hi)


def _rms(x, w):
    return x * lax.rsqrt(jnp.mean(x * x, axis=-1, keepdims=True) + RMS_EPS) * w


def _silu(x):
    return x * jax.nn.sigmoid(x)


def _group_specs(rows, width, n_p_blocks):
    return [pl.BlockSpec((rows, width), lambda i: (jnp.minimum(i, n_p_blocks - 1), 0)),
            pl.BlockSpec((rows, width), lambda i: (jnp.maximum(i - n_p_blocks, 0), 0))]


def _group_pick(i, n_p_blocks, p_ref, s_ref):
    return jnp.where(i < n_p_blocks, p_ref[...], s_ref[...])


def _inproj_body(xp_ref, xs_ref, g_ref, w_ref, o_ref, *, n_p_blocks):
    x = _group_pick(pl.program_id(0), n_p_blocks, xp_ref, xs_ref)
    h = _rms(x, g_ref[...])
    o_ref[...] = jnp.dot(h.astype(BF16), w_ref[...], preferred_element_type=F32)


def _inproj(x_p, x_s, g, w):
    n_p_blocks, n_s_blocks = x_p.shape[0] // ROW_TILE, x_s.shape[0] // ROW_TILE
    n = x_p.shape[0] + x_s.shape[0]
    return pl.pallas_call(
        functools.partial(_inproj_body, n_p_blocks=n_p_blocks),
        grid=(n_p_blocks + n_s_blocks,),
        in_specs=_group_specs(ROW_TILE, D_MODEL, n_p_blocks) + [
            pl.BlockSpec((1, D_MODEL), lambda i: (0, 0)),
            pl.BlockSpec((D_MODEL, PROJ_W), lambda i: (0, 0)),
        ],
        out_specs=pl.BlockSpec((ROW_TILE, PROJ_W), lambda i: (i, 0)),
        out_shape=jax.ShapeDtypeStruct((n, PROJ_W), F32),
        compiler_params=pltpu.CompilerParams(dimension_semantics=("arbitrary",),
                                             vmem_limit_bytes=VMEM_LIMIT),
        name="in_proj",
    )(x_p, x_s, g, w)


def _log2(n):
    assert n & (n - 1) == 0
    return n.bit_length() - 1


def _tri_inv_all(ms, c, ii, jj):
    eye = (ii == jj).astype(F32)
    base = min(c, 8)
    sh = _log2(base)
    blk = (ii >> sh) == (jj >> sh)
    ns = [jnp.where(blk, m, 0.0) for m in ms]
    xs = [eye - n for n in ns]
    ps = [_dot(n, n) for n in ns]
    ts = [_dot(jnp.concatenate([x, p], axis=0), p) for x, p in zip(xs, ps)]
    xs = [x + t[:c] for x, t in zip(xs, ts)]
    ps = [t[c:] for t in ts]
    xs = [x + _dot(x, p) for x, p in zip(xs, ps)]
    s = base
    while s < c:
        sh_s, sh_b = _log2(s), _log2(2 * s)
        off = ((ii >> sh_b) == (jj >> sh_b)) & ((ii >> sh_s) != (jj >> sh_s))
        ys = [_dot(x, jnp.where(off, m, 0.0)) for x, m in zip(xs, ms)]
        xs = [x - _dot(y, x) for x, y in zip(xs, ys)]
        s *= 2
    return xs


def _gated_norm(o, w, z):
    return o * lax.rsqrt(jnp.mean(o * o, axis=-1, keepdims=True) + RMS_EPS) * w * _silu(z)


def _chunk_rows(s, tb_rows, ci, c):
    r = s * tb_rows + ci * c
    if not isinstance(r, int):
        r = pl.multiple_of(r, c)
    return r


def _for_chunks(n_chunks, step):
    if n_chunks == 1:
        step(0, 0)
    else:
        lax.fori_loop(0, n_chunks, step, 0)


def _gdn_body(qkv_ref, z_ref, sm_ref, cbuf_ref, s0_ref, cw_ref, alog_ref, dtb_ref, nw_ref,
              o_ref, sout_ref, cout_ref, st, xc, act, gcs, us, wss, qgs, kds, aqs,
              *, nb, tb_rows, chunk, valid, n_tb):
    tb = pl.program_id(1)
    c = chunk
    n_heads = GDN_HEADS
    tail = CONV_WIDTH - 1
    pad = SUBLANE
    units = [(s, h) for s in range(nb) for h in range(n_heads)]

    @pl.when(tb == 0)
    def _():
        st[...] = s0_ref[...]
        for s in range(nb):
            xc[s, pad - tail:pad, :] = cbuf_ref[s]

    if n_tb > 1:
        @pl.when(tb > 0)
        def _():
            for s in range(nb):
                xc[s, pad - tail:pad, :] = xc[s, tb_rows + pad - tail:tb_rows + pad, :]

    for s in range(nb):
        xc[s, pad:pad + tb_rows, :] = qkv_ref[s * tb_rows:(s + 1) * tb_rows, :]

    slab = min(tb_rows, 64)
    for s in range(nb):
        for sl in range(tb_rows // slab):
            for cb in range(GDN_CONV_CH // 512):
                cs = slice(cb * 512, (cb + 1) * 512)
                lo = pad - tail + sl * slab
                acc = xc[s, lo:lo + slab, cs] * cw_ref[0:1, cs]
                for i in range(1, CONV_WIDTH):
                    acc = acc + xc[s, lo + i:lo + i + slab, cs] * cw_ref[i:i + 1, cs]
                act[s * tb_rows + sl * slab:s * tb_rows + (sl + 1) * slab, cs] = _silu(acc)

    ii = lax.broadcasted_iota(jnp.int32, (c, c), 0)
    jj = lax.broadcasted_iota(jnp.int32, (c, c), 1)
    lower = (ii >= jj)
    lower_f = lower.astype(F32)
    strict = (ii > jj)
    rowmask = None
    if valid < c:
        rowmask = lax.broadcasted_iota(jnp.int32, (c, 1), 0) < valid

    def hs(h, w):
        return slice(h * w, (h + 1) * w)

    n_chunks = tb_rows // c
    cpi = next(k for k in (GDN_CHUNKS_PER_TRIP, 2, 1) if n_chunks % k == 0)
    p1_units = [(g, h) for g in range(nb * cpi) for h in range(n_heads)]

    def phase1(ci, carry):
        rows, b_ts, gc_ts, gc_tts = [], [], [], []
        for g in range(nb * cpi):
            rr = pl.ds(_chunk_rows(g // cpi, tb_rows, ci * cpi + g % cpi, c), c)
            sm = sm_ref[rr, :]
            g_t = -jnp.exp(alog_ref[...]) * jax.nn.softplus(sm + dtb_ref[...])
            b_t = jax.nn.sigmoid(sm)
            if rowmask is not None:
                g_t = jnp.where(rowmask, g_t, 0.0)
                b_t = jnp.where(rowmask, b_t, 0.0)
            gc_t = _dot_hi(lower_f, g_t)
            gcs[rr, :] = gc_t
            rows.append(rr)
            b_ts.append(b_t)
            gc_ts.append(gc_t)
            gc_tts.append(gc_t.T)
        qn, kn, kb, vb = {}, {}, {}, {}
        for (s, h) in p1_units:
            q = act[rows[s], hs(h, GDN_DK)]
            k = act[rows[s], slice(GDN_QK_W + h * GDN_DK, GDN_QK_W + (h + 1) * GDN_DK)]
            v = act[rows[s], slice(2 * GDN_QK_W + h * GDN_DV, 2 * GDN_QK_W + (h + 1) * GDN_DV)]
            if rowmask is not None:
                q = jnp.where(rowmask, q, 0.0)
                k = jnp.where(rowmask, k, 0.0)
                v = jnp.where(rowmask, v, 0.0)
            qn[s, h] = q * lax.rsqrt(jnp.sum(q * q, axis=-1, keepdims=True) + L2_EPS) * (GDN_DK ** -0.5)
            kn[s, h] = k * lax.rsqrt(jnp.sum(k * k, axis=-1, keepdims=True) + L2_EPS)
            beta = b_ts[s][:, SM_B + h:SM_B + h + 1]
            kb[s, h] = kn[s, h] * beta
            vb[s, h] = v * beta
        s1 = {u: _dot_nt(jnp.concatenate([kb[u], qn[u]], axis=0), kn[u]) for u in p1_units}
        mm = []
        for (s, h) in p1_units:
            gcol = gc_ts[s][:, SM_A + h:SM_A + h + 1]
            grow = gc_tts[s][SM_A + h:SM_A + h + 1, :]
            dec = jnp.exp(jnp.where(lower, gcol - grow, -jnp.inf))
            mm.append(jnp.where(strict, s1[s, h][:c] * dec, 0.0))
            aqs[h, rows[s], :] = s1[s, h][c:] * dec
        tms = _tri_inv_all(mm, c, ii, jj)
        for (s, h), tm in zip(p1_units, tms):
            gcol = gc_ts[s][:, SM_A + h:SM_A + h + 1]
            eg = jnp.exp(gcol)
            uw = _dot(tm, jnp.concatenate([vb[s, h], kb[s, h] * eg], axis=1))
            us[rows[s], hs(h, GDN_DV)] = uw[:, :GDN_DV]
            wss[rows[s], hs(h, GDN_DV)] = uw[:, GDN_DV:]
            qgs[rows[s], hs(h, GDN_DK)] = qn[s, h] * eg
            kds[rows[s], hs(h, GDN_DK)] = kn[s, h] * jnp.exp(gcol[c - 1:c, :] - gcol)
        return carry

    def phase2(ci, carry):
        r0 = [_chunk_rows(s, tb_rows, ci, c) for s in range(nb)]
        rows = [pl.ds(r, c) for r in r0]
        ws = {(s, h): _dot(jnp.concatenate([wss[rows[s], hs(h, GDN_DV)], qgs[rows[s], hs(h, GDN_DK)]], axis=0),
                           st[s, h]) for (s, h) in units}
        v_new = {(s, h): us[rows[s], hs(h, GDN_DV)] - ws[s, h][:c] for (s, h) in units}
        o = {(s, h): ws[s, h][c:] + _dot(aqs[h, rows[s], :], v_new[s, h]) for (s, h) in units}
        upd = {(s, h): _dot_tn(kds[rows[s], hs(h, GDN_DK)], v_new[s, h]) for (s, h) in units}
        for (s, h) in units:
            g_last = gcs[pl.ds(r0[s] + c - 1, 1), SM_A + h:SM_A + h + 1]
            st[s, h] = st[s, h] * jnp.exp(g_last) + upd[s, h]
        for s in range(nb):
            o_ref[rows[s], :] = jnp.concatenate(
                [_gated_norm(o[s, h], nw_ref[...], z_ref[rows[s], hs(h, GDN_DV)]) for h in range(n_heads)], axis=1)
        return carry

    _for_chunks(n_chunks // cpi, phase1)
    _for_chunks(n_chunks, phase2)

    @pl.when(tb == n_tb - 1)
    def _():
        sout_ref[...] = st[...]
        last = tb_rows if valid == c else valid
        for s in range(nb):
            cout_ref[s] = xc[s, pad + last - tail:pad + last, :]


def _gdn(proj, n_seq, nb, t_len, tb_rows, chunk, valid, conv_buf, s0, conv_w, alog, dtb, nw):
    n_tb = t_len // tb_rows
    assert nb == 1 or n_tb == 1
    rows = nb * tb_rows

    def rowblk(b, t):
        return b * n_tb + t

    body = functools.partial(_gdn_body, nb=nb, tb_rows=tb_rows, chunk=chunk, valid=valid, n_tb=n_tb)
    return pl.pallas_call(
        body,
        grid=(n_seq // nb, n_tb),
        in_specs=[
            pl.BlockSpec((rows, GDN_CONV_CH), lambda b, t: (rowblk(b, t), COL_QKV // GDN_CONV_CH)),
            pl.BlockSpec((rows, GDN_V_W), lambda b, t: (rowblk(b, t), COL_Z // GDN_V_W)),
            pl.BlockSpec((rows, SM_W), lambda b, t: (rowblk(b, t), COL_SM // SM_W)),
            pl.BlockSpec((nb, CONV_WIDTH - 1, GDN_CONV_CH), lambda b, t: (b, 0, 0)),
            pl.BlockSpec((nb, GDN_HEADS, GDN_DK, GDN_DV), lambda b, t: (b, 0, 0, 0)),
            pl.BlockSpec((CONV_WIDTH, GDN_CONV_CH), lambda b, t: (0, 0)),
            pl.BlockSpec((1, SM_W), lambda b, t: (0, 0)),
            pl.BlockSpec((1, SM_W), lambda b, t: (0, 0)),
            pl.BlockSpec((1, GDN_DV), lambda b, t: (0, 0)),
        ],
        out_specs=[
            pl.BlockSpec((rows, GDN_V_W), lambda b, t: (rowblk(b, t), 0)),
            pl.BlockSpec((nb, GDN_HEADS, GDN_DK, GDN_DV), lambda b, t: (b, 0, 0, 0)),
            pl.BlockSpec((nb, CONV_WIDTH - 1, GDN_CONV_CH), lambda b, t: (b, 0, 0)),
        ],
        out_shape=[
            jax.ShapeDtypeStruct((n_seq * t_len, GDN_V_W), F32),
            jax.ShapeDtypeStruct((n_seq, GDN_HEADS, GDN_DK, GDN_DV), F32),
            jax.ShapeDtypeStruct((n_seq, CONV_WIDTH - 1, GDN_CONV_CH), F32),
        ],
        scratch_shapes=[
            pltpu.VMEM((nb, GDN_HEADS, GDN_DK, GDN_DV), F32),
            pltpu.VMEM((nb, tb_rows + SUBLANE, GDN_CONV_CH), F32),
            pltpu.VMEM((rows, GDN_CONV_CH), F32),
            pltpu.VMEM((rows, SM_W), F32),
            pltpu.VMEM((rows, GDN_V_W), F32),
            pltpu.VMEM((rows, GDN_V_W), F32),
            pltpu.VMEM((rows, GDN_QK_W), F32),
            pltpu.VMEM((rows, GDN_QK_W), F32),
            pltpu.VMEM((GDN_HEADS, rows, chunk), F32),
        ],
        compiler_params=pltpu.CompilerParams(dimension_semantics=("arbitrary", "arbitrary"),
                                             vmem_limit_bytes=VMEM_LIMIT),
        name="gdn_mixer",
    )(proj, proj, proj, conv_buf, s0, conv_w, alog, dtb, nw)


def _gla_body(q_ref, k_ref, v_ref, go_ref, sm_ref, s0_ref, wgk_ref, bgk_ref, nw_ref,
              o_ref, sout_ref, st, qes, ois, upds, decs, *, nb, tb_rows, chunk, valid, n_tb):
    tb = pl.program_id(1)
    c = chunk
    n_heads = GLA_HEADS
    units = [(s, h) for s in range(nb) for h in range(n_heads)]

    @pl.when(tb == 0)
    def _():
        st[...] = s0_ref[...]

    ii = lax.broadcasted_iota(jnp.int32, (c, c), 0)
    jj = lax.broadcasted_iota(jnp.int32, (c, c), 1)
    lower = (ii >= jj)
    lower_f = lower.astype(F32)
    rid = lax.broadcasted_iota(jnp.int32, (c, 1), 0)
    rowmask = (rid < valid) if valid < c else None
    n_sub = max(c // 16, 1)
    sub = c // n_sub

    n_chunks = tb_rows // c
    cpi = next(k for k in (GLA_CHUNKS_PER_TRIP, 2, 1) if n_chunks % k == 0)
    p1_units = [(g, h) for g in range(nb * cpi) for h in range(n_heads)]

    def phase1(ci, carry):
        rows, slots, bcs, bc_ts = [], [], [], []
        for g in range(nb * cpi):
            chunk_idx = ci * cpi + g % cpi
            rr = pl.ds(_chunk_rows(g // cpi, tb_rows, chunk_idx, c), c)
            slots.append((g // cpi) * n_chunks + chunk_idx)
            gk = jax.nn.log_sigmoid(_dot(sm_ref[rr, :], wgk_ref[...]) + bgk_ref[...]) / GLA_GATE_NORMALIZER
            if rowmask is not None:
                gk = jnp.where(rowmask, gk, 0.0)
            bc = _dot_hi(lower_f, gk)
            rows.append(rr)
            bcs.append(bc)
            bc_ts.append(bc.T)
        q, k, v, bch = {}, {}, {}, {}
        for (s, h) in p1_units:
            ks = slice(h * GLA_DK, (h + 1) * GLA_DK)
            vs = slice(h * GLA_DV, (h + 1) * GLA_DV)
            q[s, h] = q_ref[rows[s], ks] * (GLA_DK ** -0.5)
            kk = k_ref[rows[s], ks]
            vv = v_ref[rows[s], vs]
            if rowmask is not None:
                kk = jnp.where(rowmask, kk, 0.0)
                vv = jnp.where(rowmask, vv, 0.0)
            k[s, h], v[s, h] = kk, vv
            bch[s, h] = bcs[s][:, ks]
        for (g, h) in p1_units:
            qes[h, rows[g], :] = q[g, h] * jnp.exp(bch[g, h])
        a = {}
        for u in p1_units:
            q_parts, k_parts = [], []
            for sb in range(n_sub):
                ref_row = bch[u][sb * sub:sb * sub + 1, :]
                in_blk = (rid >= sb * sub) & (rid < (sb + 1) * sub)
                q_parts.append(jnp.where(in_blk, q[u] * jnp.exp(jnp.where(in_blk, bch[u] - ref_row, 0.0)), 0.0))
                k_parts.append(k[u] * jnp.exp(jnp.where(rid < (sb + 1) * sub, ref_row - bch[u], 0.0)))
            q_hat = jnp.concatenate(q_parts, axis=1) if n_sub > 1 else q_parts[0]
            k_hat = jnp.concatenate(k_parts, axis=1) if n_sub > 1 else k_parts[0]
            a[u] = jnp.where(lower, _dot_nt(q_hat, k_hat), 0.0)
        upd = {u: _dot_tn(k[u] * jnp.exp(bch[u][c - 1:c, :] - bch[u]), v[u]) for u in p1_units}
        o_intra = {u: _dot(a[u], v[u]) for u in p1_units}
        for (g, h) in p1_units:
            dec_col = bc_ts[g][h * GLA_DK:(h + 1) * GLA_DK, c - 1:c]
            decs[slots[g], h] = jnp.broadcast_to(jnp.exp(dec_col), (GLA_DK, GLA_DV))
            upds[slots[g], h] = upd[g, h]
            ois[rows[g], h * GLA_DV:(h + 1) * GLA_DV] = o_intra[g, h]
        return carry

    def phase2(ci, carry):
        rows = [pl.ds(_chunk_rows(s, tb_rows, ci, c), c) for s in range(nb)]
        o = {(s, h): ois[rows[s], h * GLA_DV:(h + 1) * GLA_DV] + _dot(qes[h, rows[s], :], st[s, h])
             for (s, h) in units}
        for (s, h) in units:
            st[s, h] = decs[s * n_chunks + ci, h] * st[s, h] + upds[s * n_chunks + ci, h]
        for s in range(nb):
            o_ref[rows[s], :] = jnp.concatenate(
                [_gated_norm(o[s, h], nw_ref[...], go_ref[rows[s], h * GLA_DV:(h + 1) * GLA_DV])
                 for h in range(n_heads)], axis=1)
        return carry

    _for_chunks(n_chunks // cpi, phase1)
    _for_chunks(n_chunks, phase2)

    @pl.when(tb == n_tb - 1)
    def _():
        sout_ref[...] = st[...]


def _gla(proj, n_seq, nb, t_len, tb_rows, chunk, valid, s0, wgk, bgk, nw):
    n_tb = t_len // tb_rows
    assert nb == 1 or n_tb == 1
    rows = nb * tb_rows

    def rowblk(b, t):
        return b * n_tb + t

    body = functools.partial(_gla_body, nb=nb, tb_rows=tb_rows, chunk=chunk, valid=valid, n_tb=n_tb)
    return pl.pallas_call(
        body,
        grid=(n_seq // nb, n_tb),
        in_specs=[
            pl.BlockSpec((rows, GLA_QK_W), lambda b, t: (rowblk(b, t), COL_GQ // GLA_QK_W)),
            pl.BlockSpec((rows, GLA_QK_W), lambda b, t: (rowblk(b, t), COL_GK // GLA_QK_W)),
            pl.BlockSpec((rows, GLA_V_W), lambda b, t: (rowblk(b, t), COL_GV // GLA_V_W)),
            pl.BlockSpec((rows, GLA_V_W), lambda b, t: (rowblk(b, t), COL_GG // GLA_V_W)),
            pl.BlockSpec((rows, SM_W), lambda b, t: (rowblk(b, t), COL_SM // SM_W)),
            pl.BlockSpec((nb, GLA_HEADS, GLA_DK, GLA_DV), lambda b, t: (b, 0, 0, 0)),
            pl.BlockSpec((SM_W, GLA_QK_W), lambda b, t: (0, 0)),
            pl.BlockSpec((1, GLA_QK_W), lambda b, t: (0, 0)),
            pl.BlockSpec((1, GLA_DV), lambda b, t: (0, 0)),
        ],
        out_specs=[
            pl.BlockSpec((rows, GLA_V_W), lambda b, t: (rowblk(b, t), 0)),
            pl.BlockSpec((nb, GLA_HEADS, GLA_DK, GLA_DV), lambda b, t: (b, 0, 0, 0)),
        ],
        out_shape=[
            jax.ShapeDtypeStruct((n_seq * t_len, GLA_V_W), F32),
            jax.ShapeDtypeStruct((n_seq, GLA_HEADS, GLA_DK, GLA_DV), F32),
        ],
        scratch_shapes=[
            pltpu.VMEM((nb, GLA_HEADS, GLA_DK, GLA_DV), F32),
            pltpu.VMEM((GLA_HEADS, rows, GLA_DK), F32),
            pltpu.VMEM((rows, GLA_V_W), F32),
            pltpu.VMEM((rows // chunk, GLA_HEADS, GLA_DK, GLA_DV), F32),
            pltpu.VMEM((rows // chunk, GLA_HEADS, GLA_DK, GLA_DV), F32),
        ],
        compiler_params=pltpu.CompilerParams(dimension_semantics=("arbitrary", "arbitrary"),
                                             vmem_limit_bytes=VMEM_LIMIT),
        name="gla_mixer",
    )(proj, proj, proj, proj, proj, s0, wgk, bgk, nw)


def _outproj_body(ogp_ref, ogs_ref, olp_ref, ols_ref, xp_ref, xs_ref, wo_ref, g_ref, wr_ref, br_ref,
                  x1_ref, h2_ref, rt_ref, cnt_ref, base, *, n_p_blocks):
    i = pl.program_id(0)

    @pl.when(i == 0)
    def _():
        base[...] = jnp.zeros_like(base)

    o = jnp.concatenate([_group_pick(i, n_p_blocks, ogp_ref, ogs_ref),
                         _group_pick(i, n_p_blocks, olp_ref, ols_ref)], axis=1)
    x1 = _group_pick(i, n_p_blocks, xp_ref, xs_ref) + jnp.dot(o.astype(BF16), wo_ref[...],
                                                               preferred_element_type=F32)
    x1_ref[...] = x1
    h = _rms(x1, g_ref[...])
    _store_token_tiles(h2_ref, h)
    logits = _dot_3pass(h, wr_ref[...]) + br_ref[...]

    tm = logits.shape[0]
    lt = logits.T[:N_EXPERTS]
    eid = lax.broadcasted_iota(jnp.int32, (N_EXPERTS, tm), 0)
    work = lt
    sel = jnp.zeros((N_EXPERTS, tm), F32)
    hits, ids, vals = [], [], []
    for _ in range(TOP_K):
        m = jnp.max(work, axis=0, keepdims=True)
        idx = jnp.min(jnp.where(work == m, eid, N_EXPERTS), axis=0, keepdims=True)
        hit = eid == idx
        hits.append(hit)
        ids.append(idx)
        vals.append(m)
        work = jnp.where(hit, -jnp.inf, work)
        sel = sel + hit.astype(F32)
    exps = [jnp.exp(v - vals[0]) for v in vals]
    den = exps[0]
    for e in exps[1:]:
        den = den + e
    gates = [e / den for e in exps]

    ri = lax.broadcasted_iota(jnp.int32, (tm, tm), 0)
    ci = lax.broadcasted_iota(jnp.int32, (tm, tm), 1)
    before = _dot(sel, (ri < ci).astype(F32)) + base[...]
    ranks = [jnp.sum(jnp.where(hit, before, 0.0), axis=0, keepdims=True) for hit in hits]
    base[...] = base[...] + jnp.sum(sel, axis=1, keepdims=True)
    cnt_ref[...] = base[...]

    row = lax.broadcasted_iota(jnp.int32, (LANE, tm), 0)
    rec = jnp.zeros((LANE, tm), F32)
    for k in range(TOP_K):
        rec = jnp.where(row == k, ids[k].astype(F32), rec)
        rec = jnp.where(row == TOP_K + k, ranks[k], rec)
        rec = jnp.where(row == 2 * TOP_K + k, gates[k], rec)
    rt_ref[...] = rec.T


def _outproj(og_p, og_s, ol_p, ol_s, x_p, x_s, wo, g, wr, br):
    n_p_blocks, n_s_blocks = x_p.shape[0] // ROW_TILE, x_s.shape[0] // ROW_TILE
    n = x_p.shape[0] + x_s.shape[0]
    return pl.pallas_call(
        functools.partial(_outproj_body, n_p_blocks=n_p_blocks),
        grid=(n_p_blocks + n_s_blocks,),
        in_specs=_group_specs(ROW_TILE, GDN_V_W, n_p_blocks) + _group_specs(ROW_TILE, GLA_V_W, n_p_blocks)
        + _group_specs(ROW_TILE, D_MODEL, n_p_blocks) + [
            pl.BlockSpec((D_MODEL, D_MODEL), lambda i: (0, 0)),
            pl.BlockSpec((1, D_MODEL), lambda i: (0, 0)),
            pl.BlockSpec((D_MODEL, LANE), lambda i: (0, 0)),
            pl.BlockSpec((1, LANE), lambda i: (0, 0)),
        ],
        out_specs=[
            pl.BlockSpec((ROW_TILE, D_MODEL), lambda i: (i, 0)),
            pl.BlockSpec((ROW_TILE * TOK_TILES, LANE), lambda i: (i, 0)),
            pl.BlockSpec((ROW_TILE, LANE), lambda i: (i, 0)),
            pl.BlockSpec((N_EXPERTS, 1), lambda i: (0, 0)),
        ],
        out_shape=[
            jax.ShapeDtypeStruct((n, D_MODEL), F32),
            jax.ShapeDtypeStruct((n * TOK_TILES, LANE), F32),
            jax.ShapeDtypeStruct((n, LANE), F32),
            jax.ShapeDtypeStruct((N_EXPERTS, 1), F32),
        ],
        scratch_shapes=[pltpu.VMEM((N_EXPERTS, 1), F32)],
        compiler_params=pltpu.CompilerParams(dimension_semantics=("arbitrary",),
                                             vmem_limit_bytes=VMEM_LIMIT),
        name="out_proj",
    )(og_p, og_s, ol_p, ol_s, x_p, x_s, wo, g, wr, br)


def _store_token_tiles(ref2d, val):
    rows = val.shape[0]
    for c in range(TOK_TILES):
        ref2d[pl.ds(c, rows, stride=TOK_TILES), :] = val[:, c * LANE:(c + 1) * LANE]


def _load_token_tiles(ref2d, first_row, rows):
    return jnp.concatenate(
        [ref2d[pl.ds(first_row * TOK_TILES + c, rows, stride=TOK_TILES), :] for c in range(TOK_TILES)], axis=1)


def _gather_rows(src_tiles, idx_ref, n_rows, dst2d, sem, priorities):
    def issue(j, carry):
        for u in range(DMA_ISSUE_UNROLL):
            r = j * DMA_ISSUE_UNROLL + u
            dst = dst2d.at[pl.ds(pl.multiple_of(r * TOK_TILES, TOK_TILES), TOK_TILES), :]
            pltpu.make_async_copy(src_tiles.at[idx_ref[0, r]], dst, sem).start(
                priority=priorities[u % len(priorities)])
        return carry

    lax.fori_loop(0, n_rows // DMA_ISSUE_UNROLL, issue, 0)


def _wait_rows(src2d, n_rows, dst2d, sem):
    pltpu.make_async_copy(src2d.at[pl.ds(0, n_rows * TOK_TILES), :], dst2d, sem).wait()


def _expert_weight_copies(e, ws, wup_hbm, wdn_hbm, wup_buf, wdn_buf, wsems):
    return (pltpu.make_async_copy(wup_hbm.at[e], wup_buf.at[ws], wsems.at[ws]),
            pltpu.make_async_copy(wdn_hbm.at[e], wdn_buf.at[ws], wsems.at[ws]))


def _expert_body(be_ref, nu_ref, first_ref, wslot_ref, next_ref, x_ref,
                 wup_hbm, bup_ref, wdn_hbm, bdn_ref, y_ref, wup_buf, wdn_buf, wsems):
    i = pl.program_id(0)
    n_used = nu_ref[0]
    ws = wslot_ref[i]
    weight_copies = functools.partial(_expert_weight_copies, wup_hbm=wup_hbm, wdn_hbm=wdn_hbm, wup_buf=wup_buf,
                                      wdn_buf=wdn_buf, wsems=wsems)

    @pl.when((i == 0) & (n_used > 0))
    def _():
        for cp in weight_copies(be_ref[0], ws):
            cp.start(priority=EXPERT_WEIGHT_QUEUE)

    @pl.when(i < n_used)
    def _():
        @pl.when(first_ref[i] == 1)
        def _():
            for cp in weight_copies(be_ref[i], ws):
                cp.wait()

            @pl.when(next_ref[i] >= 0)
            def _():
                for cp in weight_copies(next_ref[i], 1 - ws):
                    cp.start(priority=EXPERT_WEIGHT_QUEUE)

        gu = _dot(_load_token_tiles(x_ref, 0, EXPERT_ROWS), wup_buf[ws]) + bup_ref[...]
        gate = jnp.minimum(gu[:, :D_FF], SWIGLU_LIMIT)
        up = jnp.clip(gu[:, D_FF:], -SWIGLU_LIMIT, SWIGLU_LIMIT)
        a = (up + 1.0) * gate * jax.nn.sigmoid(SWIGLU_ALPHA * gate)
        _store_token_tiles(y_ref, _dot(a, wdn_buf[ws]) + bdn_ref[...])

    @pl.when(i >= n_used)
    def _():
        y_ref[...] = jnp.zeros_like(y_ref)


def _experts(block_meta, xs_2d, w_up, b_up, w_down, b_down):
    n_blocks = block_meta[0].shape[0]
    grid_spec = pltpu.PrefetchScalarGridSpec(
        num_scalar_prefetch=len(block_meta),
        grid=(n_blocks,),
        in_specs=[
            pl.BlockSpec((EXPERT_ROWS * TOK_TILES, LANE), lambda i, *_: (i, 0)),
            pl.BlockSpec(memory_space=pl.ANY),
            pl.BlockSpec((None, 1, 2 * D_FF), lambda i, be, *_: (be[i], 0, 0)),
            pl.BlockSpec(memory_space=pl.ANY),
            pl.BlockSpec((None, 1, D_MODEL), lambda i, be, *_: (be[i], 0, 0)),
        ],
        out_specs=pl.BlockSpec((EXPERT_ROWS * TOK_TILES, LANE), lambda i, *_: (i, 0)),
        scratch_shapes=[
            pltpu.VMEM((2, D_MODEL, 2 * D_FF), F32),
            pltpu.VMEM((2, D_FF, D_MODEL), F32),
            pltpu.SemaphoreType.DMA((2,)),
        ],
    )
    return pl.pallas_call(
        _expert_body,
        grid_spec=grid_spec,
        out_shape=jax.ShapeDtypeStruct((n_blocks * EXPERT_ROWS * TOK_TILES, LANE), F32),
        compiler_params=pltpu.CompilerParams(dimension_semantics=("arbitrary",),
                                             vmem_limit_bytes=VMEM_LIMIT),
        name="experts",
    )(*block_meta, xs_2d, w_up, b_up.reshape(N_EXPERTS, 1, 2 * D_FF), w_down,
      b_down.reshape(N_EXPERTS, 1, D_MODEL))


def _dispatch(h_tiles, dest_kmajor, n_rows):
    n_tok = h_tiles.shape[0]
    info = plsc.get_sparse_core_info()
    n_workers = info.num_cores * info.num_subcores
    per_worker = n_tok // n_workers
    chunk = next(c for c in (64, 48, 32, 16, 8) if per_worker % c == 0)
    assert n_tok % n_workers == 0 and per_worker % SUBLANE == 0
    mesh = plsc.VectorSubcoreMesh(core_axis_name="c", subcore_axis_name="s")

    @functools.partial(
        pl.kernel, mesh=mesh,
        out_type=jax.ShapeDtypeStruct((n_rows, TOK_TILES, LANE), F32),
        scratch_types=[pltpu.VMEM((TOP_K, chunk), jnp.int32), pltpu.VMEM((chunk, TOK_TILES, LANE), F32),
                       pltpu.SemaphoreType.DMA],
    )
    def dispatch(h_hbm, dest_hbm, out_hbm, idx_v, rows_v, sem):
        wid = lax.axis_index("s") * info.num_cores + lax.axis_index("c")

        def step(j, carry):
            t0 = pl.multiple_of(wid * per_worker + j * chunk, SUBLANE)
            pltpu.sync_copy(h_hbm.at[pl.ds(t0, chunk)], rows_v)
            for k in range(TOP_K):
                pltpu.sync_copy(dest_hbm.at[pl.ds(pl.multiple_of(k * n_tok + t0, SUBLANE), chunk)], idx_v.at[k])
            for k in range(TOP_K):
                pltpu.async_copy(rows_v, out_hbm.at[idx_v.at[k]], sem).wait()
            return carry

        lax.fori_loop(0, per_worker // chunk, step, 0)

    return dispatch(h_tiles, dest_kmajor)


def _combine_body(dest_ref, dest_next_ref, rt_ref, y_tiles, y_2d, x1_ref, g_ref, op_ref, os_ref, ybuf, sems,
                  *, n_p_blocks):
    i = pl.program_id(0)
    slot = i % 2
    n_rows = TOP_K * COMBINE_ROWS

    @pl.when(i == 0)
    def _():
        _gather_rows(y_tiles, dest_ref, n_rows, ybuf.at[0], sems.at[0], COMBINE_GATHER_QUEUES)

    _wait_rows(y_2d, n_rows, ybuf.at[slot], sems.at[slot])

    @pl.when(i + 1 < pl.num_programs(0))
    def _():
        _gather_rows(y_tiles, dest_next_ref, n_rows, ybuf.at[1 - slot], sems.at[1 - slot], COMBINE_GATHER_QUEUES)

    buf = ybuf.at[slot]
    moe = _load_token_tiles(buf, 0, COMBINE_ROWS) * rt_ref[:, 2 * TOP_K:2 * TOP_K + 1]
    for k in range(1, TOP_K):
        moe = moe + _load_token_tiles(buf, k * COMBINE_ROWS, COMBINE_ROWS) * rt_ref[:, 2 * TOP_K + k:2 * TOP_K + k + 1]
    res = _rms(x1_ref[...] + moe, g_ref[...])

    @pl.when(i < n_p_blocks)
    def _():
        op_ref[...] = res

    @pl.when(i >= n_p_blocks)
    def _():
        os_ref[...] = res


def _combine(dest, rt, y_2d, x1, g, n_p):
    n = x1.shape[0]
    n_blk = n // COMBINE_ROWS
    n_p_blocks = n_p // COMBINE_ROWS
    dest_blocks = dest.reshape(n_blk, COMBINE_ROWS, TOP_K).transpose(0, 2, 1).reshape(n_blk, 1, TOP_K * COMBINE_ROWS)
    return pl.pallas_call(
        functools.partial(_combine_body, n_p_blocks=n_p_blocks),
        grid=(n_blk,),
        in_specs=[
            pl.BlockSpec((None, 1, COMBINE_ROWS * TOP_K), lambda i: (i, 0, 0), memory_space=pltpu.SMEM),
            pl.BlockSpec((None, 1, COMBINE_ROWS * TOP_K), lambda i: (jnp.minimum(i + 1, n_blk - 1), 0, 0),
                         memory_space=pltpu.SMEM),
            pl.BlockSpec((COMBINE_ROWS, LANE), lambda i: (i, 0)),
            pl.BlockSpec(memory_space=pl.ANY),
            pl.BlockSpec(memory_space=pl.ANY),
            pl.BlockSpec((COMBINE_ROWS, D_MODEL), lambda i: (i, 0)),
            pl.BlockSpec((1, D_MODEL), lambda i: (0, 0)),
        ],
        out_specs=_group_specs(COMBINE_ROWS, D_MODEL, n_p_blocks),
        out_shape=[jax.ShapeDtypeStruct((n_p, D_MODEL), F32), jax.ShapeDtypeStruct((n - n_p, D_MODEL), F32)],
        scratch_shapes=[pltpu.VMEM((2, TOP_K * COMBINE_ROWS * TOK_TILES, LANE), F32),
                        pltpu.SemaphoreType.DMA((2,))],
        compiler_params=pltpu.CompilerParams(dimension_semantics=("arbitrary",),
                                             vmem_limit_bytes=VMEM_LIMIT),
        name="combine",
    )(dest_blocks, dest_blocks, rt, y_2d.reshape(-1, TOK_TILES, LANE), y_2d, x1, g)


def _route(rt, cnt):
    n = rt.shape[0]
    n_assign = n * TOP_K
    flat_e = rt[:, 0:TOP_K].astype(jnp.int32).reshape(-1)
    rank = rt[:, TOP_K:2 * TOP_K].astype(jnp.int32).reshape(-1)
    counts = cnt[:, 0].astype(jnp.int32)
    padded = (counts + EXPERT_ROWS - 1) // EXPERT_ROWS * EXPERT_ROWS
    pend = jnp.cumsum(padded)
    pstart = pend - padded
    dest = (pstart[flat_e] + rank).astype(jnp.int32)
    n_rows = n_assign + N_EXPERTS * EXPERT_ROWS
    n_blocks = n_rows // EXPERT_ROWS
    block_first = jnp.arange(n_blocks, dtype=jnp.int32) * EXPERT_ROWS
    block_e = jnp.minimum(jnp.sum((pend[None, :] <= block_first[:, None]).astype(jnp.int32), axis=1),
                          N_EXPERTS - 1).astype(jnp.int32)
    n_used = (pend[-1] // EXPERT_ROWS).astype(jnp.int32).reshape(1)
    ids = jnp.arange(N_EXPERTS, dtype=jnp.int32)
    present = padded > 0
    group = jnp.cumsum(present.astype(jnp.int32)) - 1
    later = jnp.flip(lax.cummin(jnp.flip(jnp.where(present, ids, N_EXPERTS))))
    next_e = jnp.concatenate([later[1:], jnp.full((1,), N_EXPERTS, jnp.int32)])
    next_e = jnp.where(next_e >= N_EXPERTS, -1, next_e)
    prev_block_e = jnp.concatenate([jnp.full((1,), -1, jnp.int32), block_e[:-1]])
    first = ((block_e != prev_block_e) & (jnp.arange(n_blocks) < n_used[0])).astype(jnp.int32)
    wslot = (group[block_e] % 2).astype(jnp.int32)
    return dest, n_rows, (block_e, n_used, first, wslot, next_e[block_e].astype(jnp.int32))


def _pad_lanes(v, width):
    return jnp.zeros((1, width), F32).at[0, :v.shape[0]].set(v.astype(F32))


def kernel(x_prompt, x_sample, state_gdn_conv, state_gdn, state_gla, rms_mix_w, w_in, conv_w, gdn_a_log,
           gdn_dt_bias, gdn_norm_w, gla_gk_w, gla_gk_b, gla_norm_w, w_out, rms_ffn_w, w_router, b_router,
           w_up, b_up, w_down, b_down, rms_final_w):
    bp, tp, d = x_prompt.shape
    bs, ts, _ = x_sample.shape
    n_p, n_s = bp * tp, bs * ts
    assert d == D_MODEL and state_gdn.shape[0] == 1, "single-layer kernel"
    assert tp >= CONV_WIDTH - 1 and ts >= CONV_WIDTH - 1, "new conv state is taken from the new tokens only"
    l = 0

    wi = w_in[l]
    a0 = GDN_CONV_CH + GDN_V_W
    g0 = a0 + 2 * GDN_HEADS
    lr0 = g0 + 2 * GLA_QK_W + 2 * GLA_V_W
    small = jnp.concatenate([wi[:, a0:a0 + 2 * GDN_HEADS], wi[:, lr0:lr0 + GLA_GATE_RANK],
                             jnp.zeros((d, SM_W - 2 * GDN_HEADS - GLA_GATE_RANK), F32)], axis=1)
    w_big = jnp.concatenate([wi[:, :a0], wi[:, g0:lr0], small], axis=1).astype(BF16)
    alog = _pad_lanes(gdn_a_log[l], SM_W)
    dtb = _pad_lanes(gdn_dt_bias[l], SM_W)
    wgk = jnp.zeros((SM_W, GLA_QK_W), F32).at[SM_LR:SM_LR + GLA_GATE_RANK].set(gla_gk_w[l])
    wr = jnp.zeros((d, LANE), F32).at[:, :N_EXPERTS].set(w_router[l])
    br = jnp.full((1, LANE), -1e30, F32).at[0, :N_EXPERTS].set(b_router[l])

    assert n_p % ROW_TILE == 0 and n_s % ROW_TILE == 0
    x_p, x_s = x_prompt.reshape(n_p, d), x_sample.reshape(n_s, d)
    proj = _inproj(x_p, x_s, rms_mix_w[l][None, :], w_big)

    tb_p = PROMPT_TIME_BLOCK
    zeros_conv = jnp.zeros((bp, CONV_WIDTH - 1, GDN_CONV_CH), F32)
    og_p, gdn_p, conv_p = _gdn(proj, bp, 1, tp, tb_p, CHUNK, CHUNK, zeros_conv,
                               jnp.zeros((bp, GDN_HEADS, GDN_DK, GDN_DV), F32), conv_w[l], alog, dtb,
                               gdn_norm_w[l][None, :])
    ol_p, gla_p = _gla(proj, bp, 1, tp, tb_p, CHUNK, CHUNK, jnp.zeros((bp, GLA_HEADS, GLA_DK, GLA_DV), F32),
                       wgk, gla_gk_b[l][None, :], gla_norm_w[l][None, :])

    ts_pad = SUBLANE
    nb_s = SAMPLE_SEQS_PER_STEP
    proj_s = proj[n_p:].reshape(bs, ts, PROJ_W)
    proj_sp = jnp.pad(proj_s, ((0, 0), (0, ts_pad - ts), (0, 0))).reshape(bs * ts_pad, PROJ_W)
    og_s, gdn_s, conv_s = _gdn(proj_sp, bs, nb_s, ts_pad, ts_pad, ts_pad, ts, state_gdn_conv[l], state_gdn[l],
                               conv_w[l], alog, dtb, gdn_norm_w[l][None, :])
    ol_s, gla_s = _gla(proj_sp, bs, nb_s, ts_pad, ts_pad, ts_pad, ts, state_gla[l], wgk, gla_gk_b[l][None, :],
                       gla_norm_w[l][None, :])
    og_s = og_s.reshape(bs, ts_pad, GDN_V_W)[:, :ts].reshape(n_s, GDN_V_W)
    ol_s = ol_s.reshape(bs, ts_pad, GLA_V_W)[:, :ts].reshape(n_s, GLA_V_W)

    x1, h2, rt, cnt = _outproj(og_p, og_s, ol_p, ol_s, x_p, x_s, w_out[l].astype(BF16), rms_ffn_w[l][None, :],
                               wr, br)

    dest, n_rows, block_meta = _route(rt, cnt)
    dest_kmajor = dest.reshape(-1, TOP_K).T.reshape(-1)
    xs = _dispatch(h2.reshape(-1, TOK_TILES, LANE), dest_kmajor, n_rows)
    y_rows = _experts(block_meta, xs.reshape(-1, LANE), w_up[l], b_up[l], w_down[l], b_down[l])
    y_p, y_s = _combine(dest, rt, y_rows, x1, rms_final_w[None, :], n_p)
    y_prompt = y_p.reshape(bp, tp, d)
    y_sample = y_s.reshape(bs, ts, d)
    return (y_prompt, y_sample, conv_p[None], gdn_p[None], gla_p[None], conv_s[None], gdn_s[None], gla_s[None])
```

```python
import functools

import jax
import jax.numpy as jnp
from jax import lax
from jax.experimental import pallas as pl
from jax.experimental.pallas import tpu as pltpu
from jax.experimental.pallas import tpu_sc as plsc

F32 = jnp.float32
BF16 = jnp.bfloat16
HI = lax.Precision.HIGHEST

D_MODEL = 1024
GDN_HEADS = 4
GDN_DK = 128
GDN_DV = 128
GLA_HEADS = 4
GLA_DK = 64
GLA_DV = 128
GLA_GATE_RANK = 16
GLA_GATE_NORMALIZER = 16.0
CONV_WIDTH = 4
CHUNK = 64
N_EXPERTS = 32
TOP_K = 4
D_FF = 1024
SWIGLU_LIMIT = 7.0
SWIGLU_ALPHA = 1.702
RMS_EPS = 1e-6
L2_EPS = 1e-6

GDN_QK_W = GDN_HEADS * GDN_DK
GDN_V_W = GDN_HEADS * GDN_DV
GDN_CONV_CH = 2 * GDN_QK_W + GDN_V_W
GLA_QK_W = GLA_HEADS * GLA_DK
GLA_V_W = GLA_HEADS * GLA_DV

COL_QKV = 0
COL_Z = 1536
COL_GQ = 2048
COL_GK = 2304
COL_GV = 2560
COL_GG = 3072
COL_SM = 3584
SM_W = 128
PROJ_W = COL_SM + SM_W
SM_A, SM_B, SM_LR = 0, 4, 8

LANE = 128
SUBLANE = 8
TOK_TILES = D_MODEL // LANE
ROW_TILE = 256
EXPERT_ROWS = 256
EXPERT_WEIGHT_QUEUE = 1
COMBINE_ROWS = 256
GDN_CHUNKS_PER_TRIP = 4
GLA_CHUNKS_PER_TRIP = 4
PROMPT_TIME_BLOCK = 512
SAMPLE_SEQS_PER_STEP = 8
VMEM_LIMIT = 56 * 1024 * 1024


def _dot(a, b):
    return jnp.dot(a.astype(BF16), b.astype(BF16), preferred_element_type=F32)


def _dot_nt(a, b):
    return lax.dot_general(a.astype(BF16), b.astype(BF16), (((1,), (1,)), ((), ())),
                           preferred_element_type=F32)


def _dot_tn(a, b):
    return lax.dot_general(a.astype(BF16), b.astype(BF16), (((0,), (0,)), ((), ())),
                           preferred_element_type=F32)


def _dot_hi(a, b):
    return jnp.dot(a, b, precision=HI, preferred_element_type=F32)


def _dot_3pass(a, b):
    a_hi = a.astype(BF16)
    b_hi = b.astype(BF16)
    a_lo = (a - a_hi.astype(F32)).astype(BF16)
    b_lo = (b - b_hi.astype(F32)).astype(BF16)

    def mm(x, y):
        return jnp.dot(x, y, preferred_element_type=F32)

    return (mm(a_lo, b_hi) + mm(a_hi, b_lo)) + mm(a_hi, b_hi)


def _rms(x, w):
    return x * lax.rsqrt(jnp.mean(x * x, axis=-1, keepdims=True) + RMS_EPS) * w


def _silu(x):
    return x * jax.nn.sigmoid(x)


def _group_specs(rows, width, n_p_blocks):
    return [pl.BlockSpec((rows, width), lambda i: (jnp.minimum(i, n_p_blocks - 1), 0)),
            pl.BlockSpec((rows, width), lambda i: (jnp.maximum(i - n_p_blocks, 0), 0))]


def _group_pick(i, n_p_blocks, p_ref, s_ref):
    return jnp.where(i < n_p_blocks, p_ref[...], s_ref[...])


def _inproj_body(xp_ref, xs_ref, g_ref, w_ref, o_ref, *, n_p_blocks):
    x = _group_pick(pl.program_id(0), n_p_blocks, xp_ref, xs_ref)
    h = _rms(x, g_ref[...])
    o_ref[...] = jnp.dot(h.astype(BF16), w_ref[...], preferred_element_type=F32)


def _inproj(x_p, x_s, g, w):
    n_p_blocks, n_s_blocks = x_p.shape[0] // ROW_TILE, x_s.shape[0] // ROW_TILE
    n = x_p.shape[0] + x_s.shape[0]
    return pl.pallas_call(
        functools.partial(_inproj_body, n_p_blocks=n_p_blocks),
        grid=(n_p_blocks + n_s_blocks,),
        in_specs=_group_specs(ROW_TILE, D_MODEL, n_p_blocks) + [
            pl.BlockSpec((1, D_MODEL), lambda i: (0, 0)),
            pl.BlockSpec((D_MODEL, PROJ_W), lambda i: (0, 0)),
        ],
        out_specs=pl.BlockSpec((ROW_TILE, PROJ_W), lambda i: (i, 0)),
        out_shape=jax.ShapeDtypeStruct((n, PROJ_W), F32),
        compiler_params=pltpu.CompilerParams(dimension_semantics=("arbitrary",),
                                             vmem_limit_bytes=VMEM_LIMIT),
        name="in_proj",
    )(x_p, x_s, g, w)


def _log2(n):
    assert n & (n - 1) == 0
    return n.bit_length() - 1


def _tri_inv_all(ms, c, ii, jj):
    eye = (ii == jj).astype(F32)
    base = min(c, 8)
    sh = _log2(base)
    blk = (ii >> sh) == (jj >> sh)
    ns = [jnp.where(blk, m, 0.0) for m in ms]
    xs = [eye - n for n in ns]
    ps = [_dot(n, n) for n in ns]
    ts = [_dot(jnp.concatenate([x, p], axis=0), p) for x, p in zip(xs, ps)]
    xs = [x + t[:c] for x, t in zip(xs, ts)]
    ps = [t[c:] for t in ts]
    xs = [x + _dot(x, p) for x, p in zip(xs, ps)]
    s = base
    while s < c:
        sh_s, sh_b = _log2(s), _log2(2 * s)
        off = ((ii >> sh_b) == (jj >> sh_b)) & ((ii >> sh_s) != (jj >> sh_s))
        ys = [_dot(x, jnp.where(off, m, 0.0)) for x, m in zip(xs, ms)]
        xs = [x - _dot(y, x) for x, y in zip(xs, ys)]
        s *= 2
    return xs


def _gated_norm(o, w, z):
    return o * lax.rsqrt(jnp.mean(o * o, axis=-1, keepdims=True) + RMS_EPS) * w * _silu(z)


def _chunk_rows(s, tb_rows, ci, c):
    r = s * tb_rows + ci * c
    if not isinstance(r, int):
        r = pl.multiple_of(r, c)
    return r


def _for_chunks(n_chunks, step):
    if n_chunks == 1:
        step(0, 0)
    else:
        lax.fori_loop(0, n_chunks, step, 0)


def _gdn_body(qkv_ref, z_ref, sm_ref, cbuf_ref, s0_ref, cw_ref, alog_ref, dtb_ref, nw_ref,
              o_ref, sout_ref, cout_ref, st, xc, act, gcs, us, wss, qgs, kds, aqs,
              *, nb, tb_rows, chunk, valid, n_tb):
    tb = pl.program_id(1)
    c = chunk
    n_heads = GDN_HEADS
    tail = CONV_WIDTH - 1
    pad = SUBLANE
    units = [(s, h) for s in range(nb) for h in range(n_heads)]

    @pl.when(tb == 0)
    def _():
        st[...] = s0_ref[...]
        for s in range(nb):
            xc[s, pad - tail:pad, :] = cbuf_ref[s]

    if n_tb > 1:
        @pl.when(tb > 0)
        def _():
            for s in range(nb):
                xc[s, pad - tail:pad, :] = xc[s, tb_rows + pad - tail:tb_rows + pad, :]

    for s in range(nb):
        xc[s, pad:pad + tb_rows, :] = qkv_ref[s * tb_rows:(s + 1) * tb_rows, :]

    slab = min(tb_rows, 64)
    for s in range(nb):
        for sl in range(tb_rows // slab):
            for cb in range(GDN_CONV_CH // 512):
                cs = slice(cb * 512, (cb + 1) * 512)
                lo = pad - tail + sl * slab
                acc = xc[s, lo:lo + slab, cs] * cw_ref[0:1, cs]
                for i in range(1, CONV_WIDTH):
                    acc = acc + xc[s, lo + i:lo + i + slab, cs] * cw_ref[i:i + 1, cs]
                act[s * tb_rows + sl * slab:s * tb_rows + (sl + 1) * slab, cs] = _silu(acc)

    ii = lax.broadcasted_iota(jnp.int32, (c, c), 0)
    jj = lax.broadcasted_iota(jnp.int32, (c, c), 1)
    lower = (ii >= jj)
    lower_f = lower.astype(F32)
    strict = (ii > jj)
    rowmask = None
    if valid < c:
        rowmask = lax.broadcasted_iota(jnp.int32, (c, 1), 0) < valid

    def hs(h, w):
        return slice(h * w, (h + 1) * w)

    n_chunks = tb_rows // c
    cpi = next(k for k in (GDN_CHUNKS_PER_TRIP, 2, 1) if n_chunks % k == 0)
    p1_units = [(g, h) for g in range(nb * cpi) for h in range(n_heads)]

    def phase1(ci, carry):
        rows, b_ts, gc_ts, gc_tts = [], [], [], []
        for g in range(nb * cpi):
            rr = pl.ds(_chunk_rows(g // cpi, tb_rows, ci * cpi + g % cpi, c), c)
            sm = sm_ref[rr, :]
            g_t = -jnp.exp(alog_ref[...]) * jax.nn.softplus(sm + dtb_ref[...])
            b_t = jax.nn.sigmoid(sm)
            if rowmask is not None:
                g_t = jnp.where(rowmask, g_t, 0.0)
                b_t = jnp.where(rowmask, b_t, 0.0)
            gc_t = _dot_hi(lower_f, g_t)
            gcs[rr, :] = gc_t
            rows.append(rr)
            b_ts.append(b_t)
            gc_ts.append(gc_t)
            gc_tts.append(gc_t.T)
        qn, kn, kb, vb = {}, {}, {}, {}
        for (s, h) in p1_units:
            q = act[rows[s], hs(h, GDN_DK)]
            k = act[rows[s], slice(GDN_QK_W + h * GDN_DK, GDN_QK_W + (h + 1) * GDN_DK)]
            v = act[rows[s], slice(2 * GDN_QK_W + h * GDN_DV, 2 * GDN_QK_W + (h + 1) * GDN_DV)]
            if rowmask is not None:
                q = jnp.where(rowmask, q, 0.0)
                k = jnp.where(rowmask, k, 0.0)
                v = jnp.where(rowmask, v, 0.0)
            qn[s, h] = q * lax.rsqrt(jnp.sum(q * q, axis=-1, keepdims=True) + L2_EPS) * (GDN_DK ** -0.5)
            kn[s, h] = k * lax.rsqrt(jnp.sum(k * k, axis=-1, keepdims=True) + L2_EPS)
            beta = b_ts[s][:, SM_B + h:SM_B + h + 1]
            kb[s, h] = kn[s, h] * beta
            vb[s, h] = v * beta
        s1 = {u: _dot_nt(jnp.concatenate([kb[u], qn[u]], axis=0), kn[u]) for u in p1_units}
        mm = []
        for (s, h) in p1_units:
            gcol = gc_ts[s][:, SM_A + h:SM_A + h + 1]
            grow = gc_tts[s][SM_A + h:SM_A + h + 1, :]
            dec = jnp.exp(jnp.where(lower, gcol - grow, -jnp.inf))
            mm.append(jnp.where(strict, s1[s, h][:c] * dec, 0.0))
            aqs[h, rows[s], :] = s1[s, h][c:] * dec
        tms = _tri_inv_all(mm, c, ii, jj)
        for (s, h), tm in zip(p1_units, tms):
            gcol = gc_ts[s][:, SM_A + h:SM_A + h + 1]
            eg = jnp.exp(gcol)
            uw = _dot(tm, jnp.concatenate([vb[s, h], kb[s, h] * eg], axis=1))
            us[rows[s], hs(h, GDN_DV)] = uw[:, :GDN_DV]
            wss[rows[s], hs(h, GDN_DV)] = uw[:, GDN_DV:]
            qgs[rows[s], hs(h, GDN_DK)] = qn[s, h] * eg
            kds[rows[s], hs(h, GDN_DK)] = kn[s, h] * jnp.exp(gcol[c - 1:c, :] - gcol)
        return carry

    def phase2(ci, carry):
        r0 = [_chunk_rows(s, tb_rows, ci, c) for s in range(nb)]
        rows = [pl.ds(r, c) for r in r0]
        ws = {(s, h): _dot(jnp.concatenate([wss[rows[s], hs(h, GDN_DV)], qgs[rows[s], hs(h, GDN_DK)]], axis=0),
                           st[s, h]) for (s, h) in units}
        v_new = {(s, h): us[rows[s], hs(h, GDN_DV)] - ws[s, h][:c] for (s, h) in units}
        o = {(s, h): ws[s, h][c:] + _dot(aqs[h, rows[s], :], v_new[s, h]) for (s, h) in units}
        upd = {(s, h): _dot_tn(kds[rows[s], hs(h, GDN_DK)], v_new[s, h]) for (s, h) in units}
        for (s, h) in units:
            g_last = gcs[pl.ds(r0[s] + c - 1, 1), SM_A + h:SM_A + h + 1]
            st[s, h] = st[s, h] * jnp.exp(g_last) + upd[s, h]
        for s in range(nb):
            o_ref[rows[s], :] = jnp.concatenate(
                [_gated_norm(o[s, h], nw_ref[...], z_ref[rows[s], hs(h, GDN_DV)]) for h in range(n_heads)], axis=1)
        return carry

    _for_chunks(n_chunks // cpi, phase1)
    _for_chunks(n_chunks, phase2)

    @pl.when(tb == n_tb - 1)
    def _():
        sout_ref[...] = st[...]
        last = tb_rows if valid == c else valid
        for s in range(nb):
            cout_ref[s] = xc[s, pad + last - tail:pad + last, :]


def _gdn(proj, n_seq, nb, t_len, tb_rows, chunk, valid, conv_buf, s0, conv_w, alog, dtb, nw):
    n_tb = t_len // tb_rows
    assert nb == 1 or n_tb == 1
    rows = nb * tb_rows

    def rowblk(b, t):
        return b * n_tb + t

    body = functools.partial(_gdn_body, nb=nb, tb_rows=tb_rows, chunk=chunk, valid=valid, n_tb=n_tb)
    return pl.pallas_call(
        body,
        grid=(n_seq // nb, n_tb),
        in_specs=[
            pl.BlockSpec((rows, GDN_CONV_CH), lambda b, t: (rowblk(b, t), COL_QKV // GDN_CONV_CH)),
            pl.BlockSpec((rows, GDN_V_W), lambda b, t: (rowblk(b, t), COL_Z // GDN_V_W)),
            pl.BlockSpec((rows, SM_W), lambda b, t: (rowblk(b, t), COL_SM // SM_W)),
            pl.BlockSpec((nb, CONV_WIDTH - 1, GDN_CONV_CH), lambda b, t: (b, 0, 0)),
            pl.BlockSpec((nb, GDN_HEADS, GDN_DK, GDN_DV), lambda b, t: (b, 0, 0, 0)),
            pl.BlockSpec((CONV_WIDTH, GDN_CONV_CH), lambda b, t: (0, 0)),
            pl.BlockSpec((1, SM_W), lambda b, t: (0, 0)),
            pl.BlockSpec((1, SM_W), lambda b, t: (0, 0)),
            pl.BlockSpec((1, GDN_DV), lambda b, t: (0, 0)),
        ],
        out_specs=[
            pl.BlockSpec((rows, GDN_V_W), lambda b, t: (rowblk(b, t), 0)),
            pl.BlockSpec((nb, GDN_HEADS, GDN_DK, GDN_DV), lambda b, t: (b, 0, 0, 0)),
            pl.BlockSpec((nb, CONV_WIDTH - 1, GDN_CONV_CH), lambda b, t: (b, 0, 0)),
        ],
        out_shape=[
            jax.ShapeDtypeStruct((n_seq * t_len, GDN_V_W), F32),
            jax.ShapeDtypeStruct((n_seq, GDN_HEADS, GDN_DK, GDN_DV), F32),
            jax.ShapeDtypeStruct((n_seq, CONV_WIDTH - 1, GDN_CONV_CH), F32),
        ],
        scratch_shapes=[
            pltpu.VMEM((nb, GDN_HEADS, GDN_DK, GDN_DV), F32),
            pltpu.VMEM((nb, tb_rows + SUBLANE, GDN_CONV_CH), F32),
            pltpu.VMEM((rows, GDN_CONV_CH), F32),
            pltpu.VMEM((rows, SM_W), F32),
            pltpu.VMEM((rows, GDN_V_W), F32),
            pltpu.VMEM((rows, GDN_V_W), F32),
            pltpu.VMEM((rows, GDN_QK_W), F32),
            pltpu.VMEM((rows, GDN_QK_W), F32),
            pltpu.VMEM((GDN_HEADS, rows, chunk), F32),
        ],
        compiler_params=pltpu.CompilerParams(dimension_semantics=("arbitrary", "arbitrary"),
                                             vmem_limit_bytes=VMEM_LIMIT),
        name="gdn_mixer",
    )(proj, proj, proj, conv_buf, s0, conv_w, alog, dtb, nw)


def _gla_body(q_ref, k_ref, v_ref, go_ref, sm_ref, s0_ref, wgk_ref, bgk_ref, nw_ref,
              o_ref, sout_ref, st, qes, ois, upds, decs, *, nb, tb_rows, chunk, valid, n_tb):
    tb = pl.program_id(1)
    c = chunk
    n_heads = GLA_HEADS
    units = [(s, h) for s in range(nb) for h in range(n_heads)]

    @pl.when(tb == 0)
    def _():
        st[...] = s0_ref[...]

    ii = lax.broadcasted_iota(jnp.int32, (c, c), 0)
    jj = lax.broadcasted_iota(jnp.int32, (c, c), 1)
    lower = (ii >= jj)
    lower_f = lower.astype(F32)
    rid = lax.broadcasted_iota(jnp.int32, (c, 1), 0)
    rowmask = (rid < valid) if valid < c else None
    n_sub = max(c // 16, 1)
    sub = c // n_sub

    n_chunks = tb_rows // c
    cpi = next(k for k in (GLA_CHUNKS_PER_TRIP, 2, 1) if n_chunks % k == 0)
    p1_units = [(g, h) for g in range(nb * cpi) for h in range(n_heads)]

    def phase1(ci, carry):
        rows, slots, bcs, bc_ts = [], [], [], []
        for g in range(nb * cpi):
            chunk_idx = ci * cpi + g % cpi
            rr = pl.ds(_chunk_rows(g // cpi, tb_rows, chunk_idx, c), c)
            slots.append((g // cpi) * n_chunks + chunk_idx)
            gk = jax.nn.log_sigmoid(_dot(sm_ref[rr, :], wgk_ref[...]) + bgk_ref[...]) / GLA_GATE_NORMALIZER
            if rowmask is not None:
                gk = jnp.where(rowmask, gk, 0.0)
            bc = _dot_hi(lower_f, gk)
            rows.append(rr)
            bcs.append(bc)
            bc_ts.append(bc.T)
        q, k, v, bch = {}, {}, {}, {}
        for (s, h) in p1_units:
            ks = slice(h * GLA_DK, (h + 1) * GLA_DK)
            vs = slice(h * GLA_DV, (h + 1) * GLA_DV)
            q[s, h] = q_ref[rows[s], ks] * (GLA_DK ** -0.5)
            kk = k_ref[rows[s], ks]
            vv = v_ref[rows[s], vs]
            if rowmask is not None:
                kk = jnp.where(rowmask, kk, 0.0)
                vv = jnp.where(rowmask, vv, 0.0)
            k[s, h], v[s, h] = kk, vv
            bch[s, h] = bcs[s][:, ks]
        for (g, h) in p1_units:
            qes[h, rows[g], :] = q[g, h] * jnp.exp(bch[g, h])
        a = {}
        for u in p1_units:
            q_parts, k_parts = [], []
            for sb in range(n_sub):
                ref_row = bch[u][sb * sub:sb * sub + 1, :]
                in_blk = (rid >= sb * sub) & (rid < (sb + 1) * sub)
                q_parts.append(jnp.where(in_blk, q[u] * jnp.exp(jnp.where(in_blk, bch[u] - ref_row, 0.0)), 0.0))
                k_parts.append(k[u] * jnp.exp(jnp.where(rid < (sb + 1) * sub, ref_row - bch[u], 0.0)))
            q_hat = jnp.concatenate(q_parts, axis=1) if n_sub > 1 else q_parts[0]
            k_hat = jnp.concatenate(k_parts, axis=1) if n_sub > 1 else k_parts[0]
            a[u] = jnp.where(lower, _dot_nt(q_hat, k_hat), 0.0)
        upd = {u: _dot_tn(k[u] * jnp.exp(bch[u][c - 1:c, :] - bch[u]), v[u]) for u in p1_units}
        o_intra = {u: _dot(a[u], v[u]) for u in p1_units}
        for (g, h) in p1_units:
            dec_col = bc_ts[g][h * GLA_DK:(h + 1) * GLA_DK, c - 1:c]
            decs[slots[g], h] = jnp.broadcast_to(jnp.exp(dec_col), (GLA_DK, GLA_DV))
            upds[slots[g], h] = upd[g, h]
            ois[rows[g], h * GLA_DV:(h + 1) * GLA_DV] = o_intra[g, h]
        return carry

    def phase2(ci, carry):
        rows = [pl.ds(_chunk_rows(s, tb_rows, ci, c), c) for s in range(nb)]
        o = {(s, h): ois[rows[s], h * GLA_DV:(h + 1) * GLA_DV] + _dot(qes[h, rows[s], :], st[s, h])
             for (s, h) in units}
        for (s, h) in units:
            st[s, h] = decs[s * n_chunks + ci, h] * st[s, h] + upds[s * n_chunks + ci, h]
        for s in range(nb):
            o_ref[rows[s], :] = jnp.concatenate(
                [_gated_norm(o[s, h], nw_ref[...], go_ref[rows[s], h * GLA_DV:(h + 1) * GLA_DV])
                 for h in range(n_heads)], axis=1)
        return carry

    _for_chunks(n_chunks // cpi, phase1)
    _for_chunks(n_chunks, phase2)

    @pl.when(tb == n_tb - 1)
    def _():
        sout_ref[...] = st[...]


def _gla(proj, n_seq, nb, t_len, tb_rows, chunk, valid, s0, wgk, bgk, nw):
    n_tb = t_len // tb_rows
    assert nb == 1 or n_tb == 1
    rows = nb * tb_rows

    def rowblk(b, t):
        return b * n_tb + t

    body = functools.partial(_gla_body, nb=nb, tb_rows=tb_rows, chunk=chunk, valid=valid, n_tb=n_tb)
    return pl.pallas_call(
        body,
        grid=(n_seq // nb, n_tb),
        in_specs=[
            pl.BlockSpec((rows, GLA_QK_W), lambda b, t: (rowblk(b, t), COL_GQ // GLA_QK_W)),
            pl.BlockSpec((rows, GLA_QK_W), lambda b, t: (rowblk(b, t), COL_GK // GLA_QK_W)),
            pl.BlockSpec((rows, GLA_V_W), lambda b, t: (rowblk(b, t), COL_GV // GLA_V_W)),
            pl.BlockSpec((rows, GLA_V_W), lambda b, t: (rowblk(b, t), COL_GG // GLA_V_W)),
            pl.BlockSpec((rows, SM_W), lambda b, t: (rowblk(b, t), COL_SM // SM_W)),
            pl.BlockSpec((nb, GLA_HEADS, GLA_DK, GLA_DV), lambda b, t: (b, 0, 0, 0)),
            pl.BlockSpec((SM_W, GLA_QK_W), lambda b, t: (0, 0)),
            pl.BlockSpec((1, GLA_QK_W), lambda b, t: (0, 0)),
            pl.BlockSpec((1, GLA_DV), lambda b, t: (0, 0)),
        ],
        out_specs=[
            pl.BlockSpec((rows, GLA_V_W), lambda b, t: (rowblk(b, t), 0)),
            pl.BlockSpec((nb, GLA_HEADS, GLA_DK, GLA_DV), lambda b, t: (b, 0, 0, 0)),
        ],
        out_shape=[
            jax.ShapeDtypeStruct((n_seq * t_len, GLA_V_W), F32),
            jax.ShapeDtypeStruct((n_seq, GLA_HEADS, GLA_DK, GLA_DV), F32),
        ],
        scratch_shapes=[
            pltpu.VMEM((nb, GLA_HEADS, GLA_DK, GLA_DV), F32),
            pltpu.VMEM((GLA_HEADS, rows, GLA_DK), F32),
            pltpu.VMEM((rows, GLA_V_W), F32),
            pltpu.VMEM((rows // chunk, GLA_HEADS, GLA_DK, GLA_DV), F32),
            pltpu.VMEM((rows // chunk, GLA_HEADS, GLA_DK, GLA_DV), F32),
        ],
        compiler_params=pltpu.CompilerParams(dimension_semantics=("arbitrary", "arbitrary"),
                                             vmem_limit_bytes=VMEM_LIMIT),
        name="gla_mixer",
    )(proj, proj, proj, proj, proj, s0, wgk, bgk, nw)


def _outproj_body(ogp_ref, ogs_ref, olp_ref, ols_ref, xp_ref, xs_ref, wo_ref, g_ref, wr_ref, br_ref,
                  x1_ref, h2_ref, rt_ref, cnt_ref, base, *, n_p_blocks):
    i = pl.program_id(0)

    @pl.when(i == 0)
    def _():
        base[...] = jnp.zeros_like(base)

    o = jnp.concatenate([_group_pick(i, n_p_blocks, ogp_ref, ogs_ref),
                         _group_pick(i, n_p_blocks, olp_ref, ols_ref)], axis=1)
    x1 = _group_pick(i, n_p_blocks, xp_ref, xs_ref) + jnp.dot(o.astype(BF16), wo_ref[...],
                                                               preferred_element_type=F32)
    x1_ref[...] = x1
    h = _rms(x1, g_ref[...])
    _store_token_tiles(h2_ref, h)
    logits = _dot_3pass(h, wr_ref[...]) + br_ref[...]

    tm = logits.shape[0]
    lt = logits.T[:N_EXPERTS]
    eid = lax.broadcasted_iota(jnp.int32, (N_EXPERTS, tm), 0)
    work = lt
    sel = jnp.zeros((N_EXPERTS, tm), F32)
    hits, ids, vals = [], [], []
    for _ in range(TOP_K):
        m = jnp.max(work, axis=0, keepdims=True)
        idx = jnp.min(jnp.where(work == m, eid, N_EXPERTS), axis=0, keepdims=True)
        hit = eid == idx
        hits.append(hit)
        ids.append(idx)
        vals.append(m)
        work = jnp.where(hit, -jnp.inf, work)
        sel = sel + hit.astype(F32)
    exps = [jnp.exp(v - vals[0]) for v in vals]
    den = exps[0]
    for e in exps[1:]:
        den = den + e
    gates = [e / den for e in exps]

    ri = lax.broadcasted_iota(jnp.int32, (tm, tm), 0)
    ci = lax.broadcasted_iota(jnp.int32, (tm, tm), 1)
    before = _dot(sel, (ri < ci).astype(F32)) + base[...]
    ranks = [jnp.sum(jnp.where(hit, before, 0.0), axis=0, keepdims=True) for hit in hits]
    base[...] = base[...] + jnp.sum(sel, axis=1, keepdims=True)
    cnt_ref[...] = base[...]

    row = lax.broadcasted_iota(jnp.int32, (LANE, tm), 0)
    rec = jnp.zeros((LANE, tm), F32)
    for k in range(TOP_K):
        rec = jnp.where(row == k, ids[k].astype(F32), rec)
        rec = jnp.where(row == TOP_K + k, ranks[k], rec)
        rec = jnp.where(row == 2 * TOP_K + k, gates[k], rec)
    rt_ref[...] = rec.T


def _outproj(og_p, og_s, ol_p, ol_s, x_p, x_s, wo, g, wr, br):
    n_p_blocks, n_s_blocks = x_p.shape[0] // ROW_TILE, x_s.shape[0] // ROW_TILE
    n = x_p.shape[0] + x_s.shape[0]
    return pl.pallas_call(
        functools.partial(_outproj_body, n_p_blocks=n_p_blocks),
        grid=(n_p_blocks + n_s_blocks,),
        in_specs=_group_specs(ROW_TILE, GDN_V_W, n_p_blocks) + _group_specs(ROW_TILE, GLA_V_W, n_p_blocks)
        + _group_specs(ROW_TILE, D_MODEL, n_p_blocks) + [
            pl.BlockSpec((D_MODEL, D_MODEL), lambda i: (0, 0)),
            pl.BlockSpec((1, D_MODEL), lambda i: (0, 0)),
            pl.BlockSpec((D_MODEL, LANE), lambda i: (0, 0)),
            pl.BlockSpec((1, LANE), lambda i: (0, 0)),
        ],
        out_specs=[
            pl.BlockSpec((ROW_TILE, D_MODEL), lambda i: (i, 0)),
            pl.BlockSpec((ROW_TILE * TOK_TILES, LANE), lambda i: (i, 0)),
            pl.BlockSpec((ROW_TILE, LANE), lambda i: (i, 0)),
            pl.BlockSpec((N_EXPERTS, 1), lambda i: (0, 0)),
        ],
        out_shape=[
            jax.ShapeDtypeStruct((n, D_MODEL), F32),
            jax.ShapeDtypeStruct((n * TOK_TILES, LANE), F32),
            jax.ShapeDtypeStruct((n, LANE), F32),
            jax.ShapeDtypeStruct((N_EXPERTS, 1), F32),
        ],
        scratch_shapes=[pltpu.VMEM((N_EXPERTS, 1), F32)],
        compiler_params=pltpu.CompilerParams(dimension_semantics=("arbitrary",),
                                             vmem_limit_bytes=VMEM_LIMIT),
        name="out_proj",
    )(og_p, og_s, ol_p, ol_s, x_p, x_s, wo, g, wr, br)


def _store_token_tiles(ref2d, val):
    rows = val.shape[0]
    for c in range(TOK_TILES):
        ref2d[pl.ds(c, rows, stride=TOK_TILES), :] = val[:, c * LANE:(c + 1) * LANE]


def _load_token_tiles(ref2d, first_row, rows):
    return jnp.concatenate(
        [ref2d[pl.ds(first_row * TOK_TILES + c, rows, stride=TOK_TILES), :] for c in range(TOK_TILES)], axis=1)


def _expert_weight_copies(e, ws, wup_hbm, wdn_hbm, wup_buf, wdn_buf, wsems):
    return (pltpu.make_async_copy(wup_hbm.at[e], wup_buf.at[ws], wsems.at[ws]),
            pltpu.make_async_copy(wdn_hbm.at[e], wdn_buf.at[ws], wsems.at[ws]))


def _expert_body(be_ref, nu_ref, first_ref, wslot_ref, next_ref, x_ref,
                 wup_hbm, bup_ref, wdn_hbm, bdn_ref, y_ref, wup_buf, wdn_buf, wsems):
    i = pl.program_id(0)
    n_used = nu_ref[0]
    ws = wslot_ref[i]
    weight_copies = functools.partial(_expert_weight_copies, wup_hbm=wup_hbm, wdn_hbm=wdn_hbm, wup_buf=wup_buf,
                                      wdn_buf=wdn_buf, wsems=wsems)

    @pl.when((i == 0) & (n_used > 0))
    def _():
        for cp in weight_copies(be_ref[0], ws):
            cp.start(priority=EXPERT_WEIGHT_QUEUE)

    @pl.when(i < n_used)
    def _():
        @pl.when(first_ref[i] == 1)
        def _():
            for cp in weight_copies(be_ref[i], ws):
                cp.wait()

            @pl.when(next_ref[i] >= 0)
            def _():
                for cp in weight_copies(next_ref[i], 1 - ws):
                    cp.start(priority=EXPERT_WEIGHT_QUEUE)

        gu = _dot(_load_token_tiles(x_ref, 0, EXPERT_ROWS), wup_buf[ws]) + bup_ref[...]
        gate = jnp.minimum(gu[:, :D_FF], SWIGLU_LIMIT)
        up = jnp.clip(gu[:, D_FF:], -SWIGLU_LIMIT, SWIGLU_LIMIT)
        a = (up + 1.0) * gate * jax.nn.sigmoid(SWIGLU_ALPHA * gate)
        _store_token_tiles(y_ref, _dot(a, wdn_buf[ws]) + bdn_ref[...])

    @pl.when(i >= n_used)
    def _():
        y_ref[...] = jnp.zeros_like(y_ref)


def _experts(block_meta, xs_2d, w_up, b_up, w_down, b_down):
    n_blocks = block_meta[0].shape[0]
    grid_spec = pltpu.PrefetchScalarGridSpec(
        num_scalar_prefetch=len(block_meta),
        grid=(n_blocks,),
        in_specs=[
            pl.BlockSpec((EXPERT_ROWS * TOK_TILES, LANE), lambda i, *_: (i, 0)),
            pl.BlockSpec(memory_space=pl.ANY),
            pl.BlockSpec((None, 1, 2 * D_FF), lambda i, be, *_: (be[i], 0, 0)),
            pl.BlockSpec(memory_space=pl.ANY),
            pl.BlockSpec((None, 1, D_MODEL), lambda i, be, *_: (be[i], 0, 0)),
        ],
        out_specs=pl.BlockSpec((EXPERT_ROWS * TOK_TILES, LANE), lambda i, *_: (i, 0)),
        scratch_shapes=[
            pltpu.VMEM((2, D_MODEL, 2 * D_FF), F32),
            pltpu.VMEM((2, D_FF, D_MODEL), F32),
            pltpu.SemaphoreType.DMA((2,)),
        ],
    )
    return pl.pallas_call(
        _expert_body,
        grid_spec=grid_spec,
        out_shape=jax.ShapeDtypeStruct((n_blocks * EXPERT_ROWS * TOK_TILES, LANE), F32),
        compiler_params=pltpu.CompilerParams(dimension_semantics=("arbitrary",),
                                             vmem_limit_bytes=VMEM_LIMIT),
        name="experts",
    )(*block_meta, xs_2d, w_up, b_up.reshape(N_EXPERTS, 1, 2 * D_FF), w_down,
      b_down.reshape(N_EXPERTS, 1, D_MODEL))


def _dispatch(h_tiles, dest_kmajor, n_rows):
    n_tok = h_tiles.shape[0]
    info = plsc.get_sparse_core_info()
    n_workers = info.num_cores * info.num_subcores
    per_worker = n_tok // n_workers
    chunk = next(c for c in (64, 48, 32, 16, 8) if per_worker % c == 0)
    assert n_tok % n_workers == 0 and per_worker % SUBLANE == 0
    mesh = plsc.VectorSubcoreMesh(core_axis_name="c", subcore_axis_name="s")

    @functools.partial(
        pl.kernel, mesh=mesh,
        out_type=jax.ShapeDtypeStruct((n_rows, TOK_TILES, LANE), F32),
        scratch_types=[pltpu.VMEM((TOP_K, chunk), jnp.int32), pltpu.VMEM((chunk, TOK_TILES, LANE), F32),
                       pltpu.SemaphoreType.DMA],
    )
    def dispatch(h_hbm, dest_hbm, out_hbm, idx_v, rows_v, sem):
        wid = lax.axis_index("s") * info.num_cores + lax.axis_index("c")

        def step(j, carry):
            t0 = pl.multiple_of(wid * per_worker + j * chunk, SUBLANE)
            pltpu.sync_copy(h_hbm.at[pl.ds(t0, chunk)], rows_v)
            for k in range(TOP_K):
                pltpu.sync_copy(dest_hbm.at[pl.ds(pl.multiple_of(k * n_tok + t0, SUBLANE), chunk)], idx_v.at[k])
            for k in range(TOP_K):
                pltpu.async_copy(rows_v, out_hbm.at[idx_v.at[k]], sem).wait()
            return carry

        lax.fori_loop(0, per_worker // chunk, step, 0)

    return dispatch(h_tiles, dest_kmajor)


def _combine_body(rt_ref, *refs, n_p_blocks):
    y_refs, (x1_ref, g_ref, op_ref, os_ref) = refs[:TOP_K], refs[TOP_K:]
    i = pl.program_id(0)
    moe = _load_token_tiles(y_refs[0], 0, COMBINE_ROWS) * rt_ref[:, 2 * TOP_K:2 * TOP_K + 1]
    for k in range(1, TOP_K):
        moe = moe + _load_token_tiles(y_refs[k], 0, COMBINE_ROWS) * rt_ref[:, 2 * TOP_K + k:2 * TOP_K + k + 1]
    res = _rms(x1_ref[...] + moe, g_ref[...])

    @pl.when(i < n_p_blocks)
    def _():
        op_ref[...] = res

    @pl.when(i >= n_p_blocks)
    def _():
        os_ref[...] = res


def _combine(rt, y4_2d, x1, g, n_p):
    n = x1.shape[0]
    n_blk = n // COMBINE_ROWS
    n_p_blocks = n_p // COMBINE_ROWS
    y_specs = [pl.BlockSpec((COMBINE_ROWS * TOK_TILES, LANE), lambda i, k=k: (k * n_blk + i, 0))
               for k in range(TOP_K)]
    return pl.pallas_call(
        functools.partial(_combine_body, n_p_blocks=n_p_blocks),
        grid=(n_blk,),
        in_specs=[pl.BlockSpec((COMBINE_ROWS, LANE), lambda i: (i, 0))] + y_specs + [
            pl.BlockSpec((COMBINE_ROWS, D_MODEL), lambda i: (i, 0)),
            pl.BlockSpec((1, D_MODEL), lambda i: (0, 0)),
        ],
        out_specs=_group_specs(COMBINE_ROWS, D_MODEL, n_p_blocks),
        out_shape=[jax.ShapeDtypeStruct((n_p, D_MODEL), F32), jax.ShapeDtypeStruct((n - n_p, D_MODEL), F32)],
        compiler_params=pltpu.CompilerParams(dimension_semantics=("arbitrary",),
                                             vmem_limit_bytes=VMEM_LIMIT),
        name="combine",
    )(rt, *([y4_2d] * TOP_K), x1, g)


def _collect(y_tiles, dest_kmajor, n_tok):
    info = plsc.get_sparse_core_info()
    n_workers = info.num_cores * info.num_subcores
    per_worker = n_tok // n_workers
    chunk = next(c for c in (64, 48, 32, 16, 8) if per_worker % c == 0)
    assert n_tok % n_workers == 0 and per_worker % SUBLANE == 0
    mesh = plsc.VectorSubcoreMesh(core_axis_name="c", subcore_axis_name="s")

    @functools.partial(
        pl.kernel, mesh=mesh,
        out_type=jax.ShapeDtypeStruct((TOP_K * n_tok, TOK_TILES, LANE), F32),
        scratch_types=[pltpu.VMEM((TOP_K, chunk), jnp.int32), pltpu.VMEM((chunk, TOK_TILES, LANE), F32),
                       pltpu.SemaphoreType.DMA],
    )
    def collect(y_hbm, dest_hbm, out_hbm, idx_v, rows_v, sem):
        wid = lax.axis_index("s") * info.num_cores + lax.axis_index("c")

        def step(j, carry):
            t0 = pl.multiple_of(wid * per_worker + j * chunk, SUBLANE)
            for k in range(TOP_K):
                a0 = pl.multiple_of(k * n_tok + t0, SUBLANE)
                pltpu.sync_copy(dest_hbm.at[pl.ds(a0, chunk)], idx_v.at[k])
                pltpu.async_copy(y_hbm.at[idx_v.at[k]], rows_v, sem).wait()
                pltpu.sync_copy(rows_v, out_hbm.at[pl.ds(a0, chunk)])
            return carry

        lax.fori_loop(0, per_worker // chunk, step, 0)

    return collect(y_tiles, dest_kmajor)


def _route(rt, cnt):
    n = rt.shape[0]
    n_assign = n * TOP_K
    flat_e = rt[:, 0:TOP_K].astype(jnp.int32).reshape(-1)
    rank = rt[:, TOP_K:2 * TOP_K].astype(jnp.int32).reshape(-1)
    counts = cnt[:, 0].astype(jnp.int32)
    padded = (counts + EXPERT_ROWS - 1) // EXPERT_ROWS * EXPERT_ROWS
    pend = jnp.cumsum(padded)
    pstart = pend - padded
    dest = (pstart[flat_e] + rank).astype(jnp.int32)
    n_rows = n_assign + N_EXPERTS * EXPERT_ROWS
    n_blocks = n_rows // EXPERT_ROWS
    block_first = jnp.arange(n_blocks, dtype=jnp.int32) * EXPERT_ROWS
    block_e = jnp.minimum(jnp.sum((pend[None, :] <= block_first[:, None]).astype(jnp.int32), axis=1),
                          N_EXPERTS - 1).astype(jnp.int32)
    n_used = (pend[-1] // EXPERT_ROWS).astype(jnp.int32).reshape(1)
    ids = jnp.arange(N_EXPERTS, dtype=jnp.int32)
    present = padded > 0
    group = jnp.cumsum(present.astype(jnp.int32)) - 1
    later = jnp.flip(lax.cummin(jnp.flip(jnp.where(present, ids, N_EXPERTS))))
    next_e = jnp.concatenate([later[1:], jnp.full((1,), N_EXPERTS, jnp.int32)])
    next_e = jnp.where(next_e >= N_EXPERTS, -1, next_e)
    prev_block_e = jnp.concatenate([jnp.full((1,), -1, jnp.int32), block_e[:-1]])
    first = ((block_e != prev_block_e) & (jnp.arange(n_blocks) < n_used[0])).astype(jnp.int32)
    wslot = (group[block_e] % 2).astype(jnp.int32)
    return dest, n_rows, (block_e, n_used, first, wslot, next_e[block_e].astype(jnp.int32))


def _pad_lanes(v, width):
    return jnp.zeros((1, width), F32).at[0, :v.shape[0]].set(v.astype(F32))


def kernel(x_prompt, x_sample, state_gdn_conv, state_gdn, state_gla, rms_mix_w, w_in, conv_w, gdn_a_log,
           gdn_dt_bias, gdn_norm_w, gla_gk_w, gla_gk_b, gla_norm_w, w_out, rms_ffn_w, w_router, b_router,
           w_up, b_up, w_down, b_down, rms_final_w):
    bp, tp, d = x_prompt.shape
    bs, ts, _ = x_sample.shape
    n_p, n_s = bp * tp, bs * ts
    assert d == D_MODEL and state_gdn.shape[0] == 1, "single-layer kernel"
    assert tp >= CONV_WIDTH - 1 and ts >= CONV_WIDTH - 1, "new conv state is taken from the new tokens only"
    l = 0

    wi = w_in[l]
    a0 = GDN_CONV_CH + GDN_V_W
    g0 = a0 + 2 * GDN_HEADS
    lr0 = g0 + 2 * GLA_QK_W + 2 * GLA_V_W
    small = jnp.concatenate([wi[:, a0:a0 + 2 * GDN_HEADS], wi[:, lr0:lr0 + GLA_GATE_RANK],
                             jnp.zeros((d, SM_W - 2 * GDN_HEADS - GLA_GATE_RANK), F32)], axis=1)
    w_big = jnp.concatenate([wi[:, :a0], wi[:, g0:lr0], small], axis=1).astype(BF16)
    alog = _pad_lanes(gdn_a_log[l], SM_W)
    dtb = _pad_lanes(gdn_dt_bias[l], SM_W)
    wgk = jnp.zeros((SM_W, GLA_QK_W), F32).at[SM_LR:SM_LR + GLA_GATE_RANK].set(gla_gk_w[l])
    wr = jnp.zeros((d, LANE), F32).at[:, :N_EXPERTS].set(w_router[l])
    br = jnp.full((1, LANE), -1e30, F32).at[0, :N_EXPERTS].set(b_router[l])

    assert n_p % ROW_TILE == 0 and n_s % ROW_TILE == 0
    x_p, x_s = x_prompt.reshape(n_p, d), x_sample.reshape(n_s, d)
    proj = _inproj(x_p, x_s, rms_mix_w[l][None, :], w_big)

    tb_p = PROMPT_TIME_BLOCK
    zeros_conv = jnp.zeros((bp, CONV_WIDTH - 1, GDN_CONV_CH), F32)
    og_p, gdn_p, conv_p = _gdn(proj, bp, 1, tp, tb_p, CHUNK, CHUNK, zeros_conv,
                               jnp.zeros((bp, GDN_HEADS, GDN_DK, GDN_DV), F32), conv_w[l], alog, dtb,
                               gdn_norm_w[l][None, :])
    ol_p, gla_p = _gla(proj, bp, 1, tp, tb_p, CHUNK, CHUNK, jnp.zeros((bp, GLA_HEADS, GLA_DK, GLA_DV), F32),
                       wgk, gla_gk_b[l][None, :], gla_norm_w[l][None, :])

    ts_pad = SUBLANE
    nb_s = SAMPLE_SEQS_PER_STEP
    proj_s = proj[n_p:].reshape(bs, ts, PROJ_W)
    proj_sp = jnp.pad(proj_s, ((0, 0), (0, ts_pad - ts), (0, 0))).reshape(bs * ts_pad, PROJ_W)
    og_s, gdn_s, conv_s = _gdn(proj_sp, bs, nb_s, ts_pad, ts_pad, ts_pad, ts, state_gdn_conv[l], state_gdn[l],
                               conv_w[l], alog, dtb, gdn_norm_w[l][None, :])
    ol_s, gla_s = _gla(proj_sp, bs, nb_s, ts_pad, ts_pad, ts_pad, ts, state_gla[l], wgk, gla_gk_b[l][None, :],
                       gla_norm_w[l][None, :])
    og_s = og_s.reshape(bs, ts_pad, GDN_V_W)[:, :ts].reshape(n_s, GDN_V_W)
    ol_s = ol_s.reshape(bs, ts_pad, GLA_V_W)[:, :ts].reshape(n_s, GLA_V_W)

    x1, h2, rt, cnt = _outproj(og_p, og_s, ol_p, ol_s, x_p, x_s, w_out[l].astype(BF16), rms_ffn_w[l][None, :],
                               wr, br)

    dest, n_rows, block_meta = _route(rt, cnt)
    dest_kmajor = dest.reshape(-1, TOP_K).T.reshape(-1)
    xs = _dispatch(h2.reshape(-1, TOK_TILES, LANE), dest_kmajor, n_rows)
    y_rows = _experts(block_meta, xs.reshape(-1, LANE), w_up[l], b_up[l], w_down[l], b_down[l])
    y4 = _collect(y_rows.reshape(-1, TOK_TILES, LANE), dest_kmajor, n_p + n_s)
    y_p, y_s = _combine(rt, y4.reshape(-1, LANE), x1, rms_final_w[None, :], n_p)
    y_prompt = y_p.reshape(bp, tp, d)
    y_sample = y_s.reshape(bs, ts, d)
    return (y_prompt, y_sample, conv_p[None], gdn_p[None], gla_p[None], conv_s[None], gdn_s[None], gla_s[None])
```

```python
import functools

import jax
import jax.numpy as jnp
from jax import lax
from jax.experimental import pallas as pl
from jax.experimental.pallas import tpu as pltpu
from jax.experimental.pallas import tpu_sc as plsc

F32 = jnp.float32
BF16 = jnp.bfloat16
HI = lax.Precision.HIGHEST

D_MODEL = 1024
GDN_HEADS = 4
GDN_DK = 128
GDN_DV = 128
GLA_HEADS = 4
GLA_DK = 64
GLA_DV = 128
GLA_GATE_RANK = 16
GLA_GATE_NORMALIZER = 16.0
CONV_WIDTH = 4
CHUNK = 64
N_EXPERTS = 32
TOP_K = 4
D_FF = 1024
SWIGLU_LIMIT = 7.0
SWIGLU_ALPHA = 1.702
RMS_EPS = 1e-6
L2_EPS = 1e-6

GDN_QK_W = GDN_HEADS * GDN_DK
GDN_V_W = GDN_HEADS * GDN_DV
GDN_CONV_CH = 2 * GDN_QK_W + GDN_V_W
GLA_QK_W = GLA_HEADS * GLA_DK
GLA_V_W = GLA_HEADS * GLA_DV

COL_QKV = 0
COL_Z = 1536
COL_GQ = 2048
COL_GK = 2304
COL_GV = 2560
COL_GG = 3072
COL_SM = 3584
SM_W = 128
PROJ_W = COL_SM + SM_W
SM_A, SM_B, SM_LR = 0, 4, 8

LANE = 128
SUBLANE = 8
TOK_TILES = D_MODEL // LANE
ROW_TILE = 256
EXPERT_ROWS = 512
EXPERT_WEIGHT_QUEUE = 1
COMBINE_ROWS = 128
DMA_ISSUE_UNROLL = 8
COMBINE_GATHER_QUEUES = (0, 1)
GDN_CHUNKS_PER_TRIP = 4
GLA_CHUNKS_PER_TRIP = 4
PROMPT_TIME_BLOCK = 512
SAMPLE_SEQS_PER_STEP = 8
VMEM_LIMIT = 56 * 1024 * 1024


def _dot(a, b):
    return jnp.dot(a.astype(BF16), b.astype(BF16), preferred_element_type=F32)


def _dot_nt(a, b):
    return lax.dot_general(a.astype(BF16), b.astype(BF16), (((1,), (1,)), ((), ())),
                           preferred_element_type=F32)


def _dot_tn(a, b):
    return lax.dot_general(a.astype(BF16), b.astype(BF16), (((0,), (0,)), ((), ())),
                           preferred_element_type=F32)


def _dot_hi(a, b):
    return jnp.dot(a, b, precision=HI, preferred_element_type=F32)


def _dot_3pass(a, b):
    a_hi = a.astype(BF16)
    b_hi = b.astype(BF16)
    a_lo = (a - a_hi.astype(F32)).astype(BF16)
    b_lo = (b - b_hi.astype(F32)).astype(BF16)

    def mm(x, y):
        return jnp.dot(x, y, preferred_element_type=F32)

    return (mm(a_lo, b_hi) + mm(a_hi, b_lo)) + mm(a_hi, b_hi)


def _rms(x, w):
    return x * lax.rsqrt(jnp.mean(x * x, axis=-1, keepdims=True) + RMS_EPS) * w


def _silu(x):
    return x * jax.nn.sigmoid(x)


def _group_specs(rows, width, n_p_blocks):
    return [pl.BlockSpec((rows, width), lambda i: (jnp.minimum(i, n_p_blocks - 1), 0)),
            pl.BlockSpec((rows, width), lambda i: (jnp.maximum(i - n_p_blocks, 0), 0))]


def _group_pick(i, n_p_blocks, p_ref, s_ref):
    return jnp.where(i < n_p_blocks, p_ref[...], s_ref[...])


def _inproj_body(xp_ref, xs_ref, g_ref, w_ref, o_ref, *, n_p_blocks):
    x = _group_pick(pl.program_id(0), n_p_blocks, xp_ref, xs_ref)
    h = _rms(x, g_ref[...])
    o_ref[...] = jnp.dot(h.astype(BF16), w_ref[...], preferred_element_type=F32)


def _inproj(x_p, x_s, g, w):
    n_p_blocks, n_s_blocks = x_p.shape[0] // ROW_TILE, x_s.shape[0] // ROW_TILE
    n = x_p.shape[0] + x_s.shape[0]
    return pl.pallas_call(
        functools.partial(_inproj_body, n_p_blocks=n_p_blocks),
        grid=(n_p_blocks + n_s_blocks,),
        in_specs=_group_specs(ROW_TILE, D_MODEL, n_p_blocks) + [
            pl.BlockSpec((1, D_MODEL), lambda i: (0, 0)),
            pl.BlockSpec((D_MODEL, PROJ_W), lambda i: (0, 0)),
        ],
        out_specs=pl.BlockSpec((ROW_TILE, PROJ_W), lambda i: (i, 0)),
        out_shape=jax.ShapeDtypeStruct((n, PROJ_W), F32),
        compiler_params=pltpu.CompilerParams(dimension_semantics=("arbitrary",),
                                             vmem_limit_bytes=VMEM_LIMIT),
        name="in_proj",
    )(x_p, x_s, g, w)


def _log2(n):
    assert n & (n - 1) == 0
    return n.bit_length() - 1


def _tri_inv_all(ms, c, ii, jj):
    eye = (ii == jj).astype(F32)
    base = min(c, 8)
    sh = _log2(base)
    blk = (ii >> sh) == (jj >> sh)
    ns = [jnp.where(blk, m, 0.0) for m in ms]
    xs = [eye - n for n in ns]
    ps = [_dot(n, n) for n in ns]
    ts = [_dot(jnp.concatenate([x, p], axis=0), p) for x, p in zip(xs, ps)]
    xs = [x + t[:c] for x, t in zip(xs, ts)]
    ps = [t[c:] for t in ts]
    xs = [x + _dot(x, p) for x, p in zip(xs, ps)]
    s = base
    while s < c:
        sh_s, sh_b = _log2(s), _log2(2 * s)
        off = ((ii >> sh_b) == (jj >> sh_b)) & ((ii >> sh_s) != (jj >> sh_s))
        ys = [_dot(x, jnp.where(off, m, 0.0)) for x, m in zip(xs, ms)]
        xs = [x - _dot(y, x) for x, y in zip(xs, ys)]
        s *= 2
    return xs


def _gated_norm(o, w, z):
    return o * lax.rsqrt(jnp.mean(o * o, axis=-1, keepdims=True) + RMS_EPS) * w * _silu(z)


def _chunk_rows(s, tb_rows, ci, c):
    r = s * tb_rows + ci * c
    if not isinstance(r, int):
        r = pl.multiple_of(r, c)
    return r


def _for_chunks(n_chunks, step):
    if n_chunks == 1:
        step(0, 0)
    else:
        lax.fori_loop(0, n_chunks, step, 0)


def _gdn_body(qkv_ref, z_ref, sm_ref, cbuf_ref, s0_ref, cw_ref, alog_ref, dtb_ref, nw_ref,
              o_ref, sout_ref, cout_ref, st, xc, act, gcs, us, wss, qgs, kds, aqs,
              *, nb, tb_rows, chunk, valid, n_tb):
    tb = pl.program_id(1)
    c = chunk
    n_heads = GDN_HEADS
    tail = CONV_WIDTH - 1
    pad = SUBLANE
    units = [(s, h) for s in range(nb) for h in range(n_heads)]

    @pl.when(tb == 0)
    def _():
        st[...] = s0_ref[...]
        for s in range(nb):
            xc[s, pad - tail:pad, :] = cbuf_ref[s]

    if n_tb > 1:
        @pl.when(tb > 0)
        def _():
            for s in range(nb):
                xc[s, pad - tail:pad, :] = xc[s, tb_rows + pad - tail:tb_rows + pad, :]

    for s in range(nb):
        xc[s, pad:pad + tb_rows, :] = qkv_ref[s * tb_rows:(s + 1) * tb_rows, :]

    slab = min(tb_rows, 64)
    for s in range(nb):
        for sl in range(tb_rows // slab):
            for cb in range(GDN_CONV_CH // 512):
                cs = slice(cb * 512, (cb + 1) * 512)
                lo = pad - tail + sl * slab
                acc = xc[s, lo:lo + slab, cs] * cw_ref[0:1, cs]
                for i in range(1, CONV_WIDTH):
                    acc = acc + xc[s, lo + i:lo + i + slab, cs] * cw_ref[i:i + 1, cs]
                act[s * tb_rows + sl * slab:s * tb_rows + (sl + 1) * slab, cs] = _silu(acc)

    ii = lax.broadcasted_iota(jnp.int32, (c, c), 0)
    jj = lax.broadcasted_iota(jnp.int32, (c, c), 1)
    lower = (ii >= jj)
    lower_f = lower.astype(F32)
    strict = (ii > jj)
    rowmask = None
    if valid < c:
        rowmask = lax.broadcasted_iota(jnp.int32, (c, 1), 0) < valid

    def hs(h, w):
        return slice(h * w, (h + 1) * w)

    n_chunks = tb_rows // c
    cpi = next(k for k in (GDN_CHUNKS_PER_TRIP, 2, 1) if n_chunks % k == 0)
    p1_units = [(g, h) for g in range(nb * cpi) for h in range(n_heads)]

    def phase1(ci, carry):
        rows, b_ts, gc_ts, gc_tts = [], [], [], []
        for g in range(nb * cpi):
            rr = pl.ds(_chunk_rows(g // cpi, tb_rows, ci * cpi + g % cpi, c), c)
            sm = sm_ref[rr, :]
            g_t = -jnp.exp(alog_ref[...]) * jax.nn.softplus(sm + dtb_ref[...])
            b_t = jax.nn.sigmoid(sm)
            if rowmask is not None:
                g_t = jnp.where(rowmask, g_t, 0.0)
                b_t = jnp.where(rowmask, b_t, 0.0)
            gc_t = _dot_hi(lower_f, g_t)
            gcs[rr, :] = gc_t
            rows.append(rr)
            b_ts.append(b_t)
            gc_ts.append(gc_t)
            gc_tts.append(gc_t.T)
        qn, kn, kb, vb = {}, {}, {}, {}
        for (s, h) in p1_units:
            q = act[rows[s], hs(h, GDN_DK)]
            k = act[rows[s], slice(GDN_QK_W + h * GDN_DK, GDN_QK_W + (h + 1) * GDN_DK)]
            v = act[rows[s], slice(2 * GDN_QK_W + h * GDN_DV, 2 * GDN_QK_W + (h + 1) * GDN_DV)]
            if rowmask is not None:
                q = jnp.where(rowmask, q, 0.0)
                k = jnp.where(rowmask, k, 0.0)
                v = jnp.where(rowmask, v, 0.0)
            qn[s, h] = q * lax.rsqrt(jnp.sum(q * q, axis=-1, keepdims=True) + L2_EPS) * (GDN_DK ** -0.5)
            kn[s, h] = k * lax.rsqrt(jnp.sum(k * k, axis=-1, keepdims=True) + L2_EPS)
            beta = b_ts[s][:, SM_B + h:SM_B + h + 1]
            kb[s, h] = kn[s, h] * beta
            vb[s, h] = v * beta
        s1 = {u: _dot_nt(jnp.concatenate([kb[u], qn[u]], axis=0), kn[u]) for u in p1_units}
        mm = []
        for (s, h) in p1_units:
            gcol = gc_ts[s][:, SM_A + h:SM_A + h + 1]
            grow = gc_tts[s][SM_A + h:SM_A + h + 1, :]
            dec = jnp.exp(jnp.where(lower, gcol - grow, -jnp.inf))
            mm.append(jnp.where(strict, s1[s, h][:c] * dec, 0.0))
            aqs[h, rows[s], :] = s1[s, h][c:] * dec
        tms = _tri_inv_all(mm, c, ii, jj)
        for (s, h), tm in zip(p1_units, tms):
            gcol = gc_ts[s][:, SM_A + h:SM_A + h + 1]
            eg = jnp.exp(gcol)
            uw = _dot(tm, jnp.concatenate([vb[s, h], kb[s, h] * eg], axis=1))
            us[rows[s], hs(h, GDN_DV)] = uw[:, :GDN_DV]
            wss[rows[s], hs(h, GDN_DV)] = uw[:, GDN_DV:]
            qgs[rows[s], hs(h, GDN_DK)] = qn[s, h] * eg
            kds[rows[s], hs(h, GDN_DK)] = kn[s, h] * jnp.exp(gcol[c - 1:c, :] - gcol)
        return carry

    def phase2(ci, carry):
        r0 = [_chunk_rows(s, tb_rows, ci, c) for s in range(nb)]
        rows = [pl.ds(r, c) for r in r0]
        ws = {(s, h): _dot(jnp.concatenate([wss[rows[s], hs(h, GDN_DV)], qgs[rows[s], hs(h, GDN_DK)]], axis=0),
                           st[s, h]) for (s, h) in units}
        v_new = {(s, h): us[rows[s], hs(h, GDN_DV)] - ws[s, h][:c] for (s, h) in units}
        o = {(s, h): ws[s, h][c:] + _dot(aqs[h, rows[s], :], v_new[s, h]) for (s, h) in units}
        upd = {(s, h): _dot_tn(kds[rows[s], hs(h, GDN_DK)], v_new[s, h]) for (s, h) in units}
        for (s, h) in units:
            g_last = gcs[pl.ds(r0[s] + c - 1, 1), SM_A + h:SM_A + h + 1]
            st[s, h] = st[s, h] * jnp.exp(g_last) + upd[s, h]
        for s in range(nb):
            o_ref[rows[s], :] = jnp.concatenate(
                [_gated_norm(o[s, h], nw_ref[...], z_ref[rows[s], hs(h, GDN_DV)]) for h in range(n_heads)], axis=1)
        return carry

    _for_chunks(n_chunks // cpi, phase1)
    _for_chunks(n_chunks, phase2)

    @pl.when(tb == n_tb - 1)
    def _():
        sout_ref[...] = st[...]
        last = tb_rows if valid == c else valid
        for s in range(nb):
            cout_ref[s] = xc[s, pad + last - tail:pad + last, :]


def _gdn(proj, n_seq, nb, t_len, tb_rows, chunk, valid, conv_buf, s0, conv_w, alog, dtb, nw):
    n_tb = t_len // tb_rows
    assert nb == 1 or n_tb == 1
    rows = nb * tb_rows

    def rowblk(b, t):
        return b * n_tb + t

    body = functools.partial(_gdn_body, nb=nb, tb_rows=tb_rows, chunk=chunk, valid=valid, n_tb=n_tb)
    return pl.pallas_call(
        body,
        grid=(n_seq // nb, n_tb),
        in_specs=[
            pl.BlockSpec((rows, GDN_CONV_CH), lambda b, t: (rowblk(b, t), COL_QKV // GDN_CONV_CH)),
            pl.BlockSpec((rows, GDN_V_W), lambda b, t: (rowblk(b, t), COL_Z // GDN_V_W)),
            pl.BlockSpec((rows, SM_W), lambda b, t: (rowblk(b, t), COL_SM // SM_W)),
            pl.BlockSpec((nb, CONV_WIDTH - 1, GDN_CONV_CH), lambda b, t: (b, 0, 0)),
            pl.BlockSpec((nb, GDN_HEADS, GDN_DK, GDN_DV), lambda b, t: (b, 0, 0, 0)),
            pl.BlockSpec((CONV_WIDTH, GDN_CONV_CH), lambda b, t: (0, 0)),
            pl.BlockSpec((1, SM_W), lambda b, t: (0, 0)),
            pl.BlockSpec((1, SM_W), lambda b, t: (0, 0)),
            pl.BlockSpec((1, GDN_DV), lambda b, t: (0, 0)),
        ],
        out_specs=[
            pl.BlockSpec((rows, GDN_V_W), lambda b, t: (rowblk(b, t), 0)),
            pl.BlockSpec((nb, GDN_HEADS, GDN_DK, GDN_DV), lambda b, t: (b, 0, 0, 0)),
            pl.BlockSpec((nb, CONV_WIDTH - 1, GDN_CONV_CH), lambda b, t: (b, 0, 0)),
        ],
        out_shape=[
            jax.ShapeDtypeStruct((n_seq * t_len, GDN_V_W), F32),
            jax.ShapeDtypeStruct((n_seq, GDN_HEADS, GDN_DK, GDN_DV), F32),
            jax.ShapeDtypeStruct((n_seq, CONV_WIDTH - 1, GDN_CONV_CH), F32),
        ],
        scratch_shapes=[
            pltpu.VMEM((nb, GDN_HEADS, GDN_DK, GDN_DV), F32),
            pltpu.VMEM((nb, tb_rows + SUBLANE, GDN_CONV_CH), F32),
            pltpu.VMEM((rows, GDN_CONV_CH), F32),
            pltpu.VMEM((rows, SM_W), F32),
            pltpu.VMEM((rows, GDN_V_W), F32),
            pltpu.VMEM((rows, GDN_V_W), F32),
            pltpu.VMEM((rows, GDN_QK_W), F32),
            pltpu.VMEM((rows, GDN_QK_W), F32),
            pltpu.VMEM((GDN_HEADS, rows, chunk), F32),
        ],
        compiler_params=pltpu.CompilerParams(dimension_semantics=("arbitrary", "arbitrary"),
                                             vmem_limit_bytes=VMEM_LIMIT),
        name="gdn_mixer",
    )(proj, proj, proj, conv_buf, s0, conv_w, alog, dtb, nw)


def _gla_body(q_ref, k_ref, v_ref, go_ref, sm_ref, s0_ref, wgk_ref, bgk_ref, nw_ref,
              o_ref, sout_ref, st, qes, ois, upds, decs, *, nb, tb_rows, chunk, valid, n_tb):
    tb = pl.program_id(1)
    c = chunk
    n_heads = GLA_HEADS
    units = [(s, h) for s in range(nb) for h in range(n_heads)]

    @pl.when(tb == 0)
    def _():
        st[...] = s0_ref[...]

    ii = lax.broadcasted_iota(jnp.int32, (c, c), 0)
    jj = lax.broadcasted_iota(jnp.int32, (c, c), 1)
    lower = (ii >= jj)
    lower_f = lower.astype(F32)
    rid = lax.broadcasted_iota(jnp.int32, (c, 1), 0)
    rowmask = (rid < valid) if valid < c else None
    n_sub = max(c // 16, 1)
    sub = c // n_sub

    n_chunks = tb_rows // c
    cpi = next(k for k in (GLA_CHUNKS_PER_TRIP, 2, 1) if n_chunks % k == 0)
    p1_units = [(g, h) for g in range(nb * cpi) for h in range(n_heads)]

    def phase1(ci, carry):
        rows, slots, bcs, bc_ts = [], [], [], []
        for g in range(nb * cpi):
            chunk_idx = ci * cpi + g % cpi
            rr = pl.ds(_chunk_rows(g // cpi, tb_rows, chunk_idx, c), c)
            slots.append((g // cpi) * n_chunks + chunk_idx)
            gk = jax.nn.log_sigmoid(_dot(sm_ref[rr, :], wgk_ref[...]) + bgk_ref[...]) / GLA_GATE_NORMALIZER
            if rowmask is not None:
                gk = jnp.where(rowmask, gk, 0.0)
            bc = _dot_hi(lower_f, gk)
            rows.append(rr)
            bcs.append(bc)
            bc_ts.append(bc.T)
        q, k, v, bch = {}, {}, {}, {}
        for (s, h) in p1_units:
            ks = slice(h * GLA_DK, (h + 1) * GLA_DK)
            vs = slice(h * GLA_DV, (h + 1) * GLA_DV)
            q[s, h] = q_ref[rows[s], ks] * (GLA_DK ** -0.5)
            kk = k_ref[rows[s], ks]
            vv = v_ref[rows[s], vs]
            if rowmask is not None:
                kk = jnp.where(rowmask, kk, 0.0)
                vv = jnp.where(rowmask, vv, 0.0)
            k[s, h], v[s, h] = kk, vv
            bch[s, h] = bcs[s][:, ks]
        for (g, h) in p1_units:
            qes[h, rows[g], :] = q[g, h] * jnp.exp(bch[g, h])
        a = {}
        for u in p1_units:
            q_parts, k_parts = [], []
            for sb in range(n_sub):
                ref_row = bch[u][sb * sub:sb * sub + 1, :]
                in_blk = (rid >= sb * sub) & (rid < (sb + 1) * sub)
                q_parts.append(jnp.where(in_blk, q[u] * jnp.exp(jnp.where(in_blk, bch[u] - ref_row, 0.0)), 0.0))
                k_parts.append(k[u] * jnp.exp(jnp.where(rid < (sb + 1) * sub, ref_row - bch[u], 0.0)))
            q_hat = jnp.concatenate(q_parts, axis=1) if n_sub > 1 else q_parts[0]
            k_hat = jnp.concatenate(k_parts, axis=1) if n_sub > 1 else k_parts[0]
            a[u] = jnp.where(lower, _dot_nt(q_hat, k_hat), 0.0)
        upd = {u: _dot_tn(k[u] * jnp.exp(bch[u][c - 1:c, :] - bch[u]), v[u]) for u in p1_units}
        o_intra = {u: _dot(a[u], v[u]) for u in p1_units}
        for (g, h) in p1_units:
            dec_col = bc_ts[g][h * GLA_DK:(h + 1) * GLA_DK, c - 1:c]
            decs[slots[g], h] = jnp.broadcast_to(jnp.exp(dec_col), (GLA_DK, GLA_DV))
            upds[slots[g], h] = upd[g, h]
            ois[rows[g], h * GLA_DV:(h + 1) * GLA_DV] = o_intra[g, h]
        return carry

    def phase2(ci, carry):
        rows = [pl.ds(_chunk_rows(s, tb_rows, ci, c), c) for s in range(nb)]
        o = {(s, h): ois[rows[s], h * GLA_DV:(h + 1) * GLA_DV] + _dot(qes[h, rows[s], :], st[s, h])
             for (s, h) in units}
        for (s, h) in units:
            st[s, h] = decs[s * n_chunks + ci, h] * st[s, h] + upds[s * n_chunks + ci, h]
        for s in range(nb):
            o_ref[rows[s], :] = jnp.concatenate(
                [_gated_norm(o[s, h], nw_ref[...], go_ref[rows[s], h * GLA_DV:(h + 1) * GLA_DV])
                 for h in range(n_heads)], axis=1)
        return carry

    _for_chunks(n_chunks // cpi, phase1)
    _for_chunks(n_chunks, phase2)

    @pl.when(tb == n_tb - 1)
    def _():
        sout_ref[...] = st[...]


def _gla(proj, n_seq, nb, t_len, tb_rows, chunk, valid, s0, wgk, bgk, nw):
    n_tb = t_len // tb_rows
    assert nb == 1 or n_tb == 1
    rows = nb * tb_rows

    def rowblk(b, t):
        return b * n_tb + t

    body = functools.partial(_gla_body, nb=nb, tb_rows=tb_rows, chunk=chunk, valid=valid, n_tb=n_tb)
    return pl.pallas_call(
        body,
        grid=(n_seq // nb, n_tb),
        in_specs=[
            pl.BlockSpec((rows, GLA_QK_W), lambda b, t: (rowblk(b, t), COL_GQ // GLA_QK_W)),
            pl.BlockSpec((rows, GLA_QK_W), lambda b, t: (rowblk(b, t), COL_GK // GLA_QK_W)),
            pl.BlockSpec((rows, GLA_V_W), lambda b, t: (rowblk(b, t), COL_GV // GLA_V_W)),
            pl.BlockSpec((rows, GLA_V_W), lambda b, t: (rowblk(b, t), COL_GG // GLA_V_W)),
            pl.BlockSpec((rows, SM_W), lambda b, t: (rowblk(b, t), COL_SM // SM_W)),
            pl.BlockSpec((nb, GLA_HEADS, GLA_DK, GLA_DV), lambda b, t: (b, 0, 0, 0)),
            pl.BlockSpec((SM_W, GLA_QK_W), lambda b, t: (0, 0)),
            pl.BlockSpec((1, GLA_QK_W), lambda b, t: (0, 0)),
            pl.BlockSpec((1, GLA_DV), lambda b, t: (0, 0)),
        ],
        out_specs=[
            pl.BlockSpec((rows, GLA_V_W), lambda b, t: (rowblk(b, t), 0)),
            pl.BlockSpec((nb, GLA_HEADS, GLA_DK, GLA_DV), lambda b, t: (b, 0, 0, 0)),
        ],
        out_shape=[
            jax.ShapeDtypeStruct((n_seq * t_len, GLA_V_W), F32),
            jax.ShapeDtypeStruct((n_seq, GLA_HEADS, GLA_DK, GLA_DV), F32),
        ],
        scratch_shapes=[
            pltpu.VMEM((nb, GLA_HEADS, GLA_DK, GLA_DV), F32),
            pltpu.VMEM((GLA_HEADS, rows, GLA_DK), F32),
            pltpu.VMEM((rows, GLA_V_W), F32),
            pltpu.VMEM((rows // chunk, GLA_HEADS, GLA_DK, GLA_DV), F32),
            pltpu.VMEM((rows // chunk, GLA_HEADS, GLA_DK, GLA_DV), F32),
        ],
        compiler_params=pltpu.CompilerParams(dimension_semantics=("arbitrary", "arbitrary"),
                                             vmem_limit_bytes=VMEM_LIMIT),
        name="gla_mixer",
    )(proj, proj, proj, proj, proj, s0, wgk, bgk, nw)


def _outproj_body(ogp_ref, ogs_ref, olp_ref, ols_ref, xp_ref, xs_ref, wo_ref, g_ref, wr_ref, br_ref,
                  x1_ref, h2_ref, rt_ref, cnt_ref, base, *, n_p_blocks):
    i = pl.program_id(0)

    @pl.when(i == 0)
    def _():
        base[...] = jnp.zeros_like(base)

    o = jnp.concatenate([_group_pick(i, n_p_blocks, ogp_ref, ogs_ref),
                         _group_pick(i, n_p_blocks, olp_ref, ols_ref)], axis=1)
    x1 = _group_pick(i, n_p_blocks, xp_ref, xs_ref) + jnp.dot(o.astype(BF16), wo_ref[...],
                                                               preferred_element_type=F32)
    x1_ref[...] = x1
    h = _rms(x1, g_ref[...])
    _store_token_tiles(h2_ref, h)
    logits = _dot_3pass(h, wr_ref[...]) + br_ref[...]

    tm = logits.shape[0]
    lt = logits.T[:N_EXPERTS]
    eid = lax.broadcasted_iota(jnp.int32, (N_EXPERTS, tm), 0)
    work = lt
    sel = jnp.zeros((N_EXPERTS, tm), F32)
    hits, ids, vals = [], [], []
    for _ in range(TOP_K):
        m = jnp.max(work, axis=0, keepdims=True)
        idx = jnp.min(jnp.where(work == m, eid, N_EXPERTS), axis=0, keepdims=True)
        hit = eid == idx
        hits.append(hit)
        ids.append(idx)
        vals.append(m)
        work = jnp.where(hit, -jnp.inf, work)
        sel = sel + hit.astype(F32)
    exps = [jnp.exp(v - vals[0]) for v in vals]
    den = exps[0]
    for e in exps[1:]:
        den = den + e
    gates = [e / den for e in exps]

    ri = lax.broadcasted_iota(jnp.int32, (tm, tm), 0)
    ci = lax.broadcasted_iota(jnp.int32, (tm, tm), 1)
    before = _dot(sel, (ri < ci).astype(F32)) + base[...]
    ranks = [jnp.sum(jnp.where(hit, before, 0.0), axis=0, keepdims=True) for hit in hits]
    base[...] = base[...] + jnp.sum(sel, axis=1, keepdims=True)
    cnt_ref[...] = base[...]

    row = lax.broadcasted_iota(jnp.int32, (LANE, tm), 0)
    rec = jnp.zeros((LANE, tm), F32)
    for k in range(TOP_K):
        rec = jnp.where(row == k, ids[k].astype(F32), rec)
        rec = jnp.where(row == TOP_K + k, ranks[k], rec)
        rec = jnp.where(row == 2 * TOP_K + k, gates[k], rec)
    rt_ref[...] = rec.T


def _outproj(og_p, og_s, ol_p, ol_s, x_p, x_s, wo, g, wr, br):
    n_p_blocks, n_s_blocks = x_p.shape[0] // ROW_TILE, x_s.shape[0] // ROW_TILE
    n = x_p.shape[0] + x_s.shape[0]
    return pl.pallas_call(
        functools.partial(_outproj_body, n_p_blocks=n_p_blocks),
        grid=(n_p_blocks + n_s_blocks,),
        in_specs=_group_specs(ROW_TILE, GDN_V_W, n_p_blocks) + _group_specs(ROW_TILE, GLA_V_W, n_p_blocks)
        + _group_specs(ROW_TILE, D_MODEL, n_p_blocks) + [
            pl.BlockSpec((D_MODEL, D_MODEL), lambda i: (0, 0)),
            pl.BlockSpec((1, D_MODEL), lambda i: (0, 0)),
            pl.BlockSpec((D_MODEL, LANE), lambda i: (0, 0)),
            pl.BlockSpec((1, LANE), lambda i: (0, 0)),
        ],
        out_specs=[
            pl.BlockSpec((ROW_TILE, D_MODEL), lambda i: (i, 0)),
            pl.BlockSpec((ROW_TILE * TOK_TILES, LANE), lambda i: (i, 0)),
            pl.BlockSpec((ROW_TILE, LANE), lambda i: (i, 0)),
            pl.BlockSpec((N_EXPERTS, 1), lambda i: (0, 0)),
        ],
        out_shape=[
            jax.ShapeDtypeStruct((n, D_MODEL), F32),
            jax.ShapeDtypeStruct((n * TOK_TILES, LANE), F32),
            jax.ShapeDtypeStruct((n, LANE), F32),
            jax.ShapeDtypeStruct((N_EXPERTS, 1), F32),
        ],
        scratch_shapes=[pltpu.VMEM((N_EXPERTS, 1), F32)],
        compiler_params=pltpu.CompilerParams(dimension_semantics=("arbitrary",),
                                             vmem_limit_bytes=VMEM_LIMIT),
        name="out_proj",
    )(og_p, og_s, ol_p, ol_s, x_p, x_s, wo, g, wr, br)


def _store_token_tiles(ref2d, val):
    rows = val.shape[0]
    for c in range(TOK_TILES):
        ref2d[pl.ds(c, rows, stride=TOK_TILES), :] = val[:, c * LANE:(c + 1) * LANE]


def _load_token_tiles(ref2d, first_row, rows):
    return jnp.concatenate(
        [ref2d[pl.ds(first_row * TOK_TILES + c, rows, stride=TOK_TILES), :] for c in range(TOK_TILES)], axis=1)


def _expert_weight_copies(e, ws, wup_hbm, wdn_hbm, wup_buf, wdn_buf, wsems):
    return (pltpu.make_async_copy(wup_hbm.at[e], wup_buf.at[ws], wsems.at[ws]),
            pltpu.make_async_copy(wdn_hbm.at[e], wdn_buf.at[ws], wsems.at[ws]))


def _expert_body(be_ref, nu_ref, first_ref, wslot_ref, next_ref, x_ref,
                 wup_hbm, bup_ref, wdn_hbm, bdn_ref, y_ref, wup_buf, wdn_buf, wsems):
    i = pl.program_id(0)
    n_used = nu_ref[0]
    ws = wslot_ref[i]
    weight_copies = functools.partial(_expert_weight_copies, wup_hbm=wup_hbm, wdn_hbm=wdn_hbm, wup_buf=wup_buf,
                                      wdn_buf=wdn_buf, wsems=wsems)

    @pl.when((i == 0) & (n_used > 0))
    def _():
        for cp in weight_copies(be_ref[0], ws):
            cp.start(priority=EXPERT_WEIGHT_QUEUE)

    @pl.when(i < n_used)
    def _():
        @pl.when(first_ref[i] == 1)
        def _():
            for cp in weight_copies(be_ref[i], ws):
                cp.wait()

            @pl.when(next_ref[i] >= 0)
            def _():
                for cp in weight_copies(next_ref[i], 1 - ws):
                    cp.start(priority=EXPERT_WEIGHT_QUEUE)

        gu = _dot(_load_token_tiles(x_ref, 0, EXPERT_ROWS), wup_buf[ws]) + bup_ref[...]
        gate = jnp.minimum(gu[:, :D_FF], SWIGLU_LIMIT)
        up = jnp.clip(gu[:, D_FF:], -SWIGLU_LIMIT, SWIGLU_LIMIT)
        a = (up + 1.0) * gate * jax.nn.sigmoid(SWIGLU_ALPHA * gate)
        _store_token_tiles(y_ref, _dot(a, wdn_buf[ws]) + bdn_ref[...])

    @pl.when(i >= n_used)
    def _():
        y_ref[...] = jnp.zeros_like(y_ref)


def _experts(block_meta, xs_2d, w_up, b_up, w_down, b_down):
    n_blocks = block_meta[0].shape[0]
    grid_spec = pltpu.PrefetchScalarGridSpec(
        num_scalar_prefetch=len(block_meta),
        grid=(n_blocks,),
        in_specs=[
            pl.BlockSpec((EXPERT_ROWS * TOK_TILES, LANE), lambda i, *_: (i, 0)),
            pl.BlockSpec(memory_space=pl.ANY),
            pl.BlockSpec((None, 1, 2 * D_FF), lambda i, be, *_: (be[i], 0, 0)),
            pl.BlockSpec(memory_space=pl.ANY),
            pl.BlockSpec((None, 1, D_MODEL), lambda i, be, *_: (be[i], 0, 0)),
        ],
        out_specs=pl.BlockSpec((EXPERT_ROWS * TOK_TILES, LANE), lambda i, *_: (i, 0)),
        scratch_shapes=[
            pltpu.VMEM((2, D_MODEL, 2 * D_FF), F32),
            pltpu.VMEM((2, D_FF, D_MODEL), F32),
            pltpu.SemaphoreType.DMA((2,)),
        ],
    )
    return pl.pallas_call(
        _expert_body,
        grid_spec=grid_spec,
        out_shape=jax.ShapeDtypeStruct((n_blocks * EXPERT_ROWS * TOK_TILES, LANE), F32),
        compiler_params=pltpu.CompilerParams(dimension_semantics=("arbitrary",),
                                             vmem_limit_bytes=VMEM_LIMIT),
        name="experts",
    )(*block_meta, xs_2d, w_up, b_up.reshape(N_EXPERTS, 1, 2 * D_FF), w_down,
      b_down.reshape(N_EXPERTS, 1, D_MODEL))


def _dispatch(h_tiles, dest_kmajor, n_rows):
    n_tok = h_tiles.shape[0]
    info = plsc.get_sparse_core_info()
    n_workers = info.num_cores * info.num_subcores
    per_worker = n_tok // n_workers
    chunk = next(c for c in (64, 48, 32, 16, 8) if per_worker % c == 0)
    assert n_tok % n_workers == 0 and per_worker % SUBLANE == 0
    mesh = plsc.VectorSubcoreMesh(core_axis_name="c", subcore_axis_name="s")

    @functools.partial(
        pl.kernel, mesh=mesh,
        out_type=jax.ShapeDtypeStruct((n_rows, TOK_TILES, LANE), F32),
        scratch_types=[pltpu.VMEM((TOP_K, chunk), jnp.int32), pltpu.VMEM((chunk, TOK_TILES, LANE), F32),
                       pltpu.SemaphoreType.DMA],
    )
    def dispatch(h_hbm, dest_hbm, out_hbm, idx_v, rows_v, sem):
        wid = lax.axis_index("s") * info.num_cores + lax.axis_index("c")

        def step(j, carry):
            t0 = pl.multiple_of(wid * per_worker + j * chunk, SUBLANE)
            loads = [pltpu.async_copy(h_hbm.at[pl.ds(t0, chunk)], rows_v, sem)]
            for k in range(TOP_K):
                loads.append(pltpu.async_copy(
                    dest_hbm.at[pl.ds(pl.multiple_of(k * n_tok + t0, SUBLANE), chunk)], idx_v.at[k], sem))
            for cp in loads:
                cp.wait()
            stores = [pltpu.async_copy(rows_v, out_hbm.at[idx_v.at[k]], sem) for k in range(TOP_K)]
            for cp in stores:
                cp.wait()
            return carry

        lax.fori_loop(0, per_worker // chunk, step, 0)

    return dispatch(h_tiles, dest_kmajor)


def _gather_rows(src_tiles, idx_ref, n_rows, dst2d, sem, priorities):
    def issue(j, carry):
        for u in range(DMA_ISSUE_UNROLL):
            r = j * DMA_ISSUE_UNROLL + u
            dst = dst2d.at[pl.ds(pl.multiple_of(r * TOK_TILES, TOK_TILES), TOK_TILES), :]
            pltpu.make_async_copy(src_tiles.at[idx_ref[0, r]], dst, sem).start(
                priority=priorities[u % len(priorities)])
        return carry

    lax.fori_loop(0, n_rows // DMA_ISSUE_UNROLL, issue, 0)


def _wait_rows(src2d, n_rows, dst2d, sem):
    pltpu.make_async_copy(src2d.at[pl.ds(0, n_rows * TOK_TILES), :], dst2d, sem).wait()


def _combine_body(dest_ref, dest_next_ref, rt_ref, y_tiles, y_2d, x1_ref, g_ref, op_ref, os_ref, ybuf, sems,
                  *, n_p_blocks):
    i = pl.program_id(0)
    slot = i % 2
    n_rows = TOP_K * COMBINE_ROWS

    @pl.when(i == 0)
    def _():
        _gather_rows(y_tiles, dest_ref, n_rows, ybuf.at[0], sems.at[0], COMBINE_GATHER_QUEUES)

    _wait_rows(y_2d, n_rows, ybuf.at[slot], sems.at[slot])

    @pl.when(i + 1 < pl.num_programs(0))
    def _():
        _gather_rows(y_tiles, dest_next_ref, n_rows, ybuf.at[1 - slot], sems.at[1 - slot], COMBINE_GATHER_QUEUES)

    buf = ybuf.at[slot]
    moe = _load_token_tiles(buf, 0, COMBINE_ROWS) * rt_ref[:, 2 * TOP_K:2 * TOP_K + 1]
    for k in range(1, TOP_K):
        moe = moe + _load_token_tiles(buf, k * COMBINE_ROWS, COMBINE_ROWS) * rt_ref[:, 2 * TOP_K + k:2 * TOP_K + k + 1]
    res = _rms(x1_ref[...] + moe, g_ref[...])

    @pl.when(i < n_p_blocks)
    def _():
        op_ref[...] = res

    @pl.when(i >= n_p_blocks)
    def _():
        os_ref[...] = res


def _combine(dest, rt, y_2d, x1, g, n_p):
    n = x1.shape[0]
    n_blk = n // COMBINE_ROWS
    n_p_blocks = n_p // COMBINE_ROWS
    dest_blocks = dest.reshape(n_blk, COMBINE_ROWS, TOP_K).transpose(0, 2, 1).reshape(n_blk, 1, TOP_K * COMBINE_ROWS)
    return pl.pallas_call(
        functools.partial(_combine_body, n_p_blocks=n_p_blocks),
        grid=(n_blk,),
        in_specs=[
            pl.BlockSpec((None, 1, COMBINE_ROWS * TOP_K), lambda i: (i, 0, 0), memory_space=pltpu.SMEM),
            pl.BlockSpec((None, 1, COMBINE_ROWS * TOP_K), lambda i: (jnp.minimum(i + 1, n_blk - 1), 0, 0),
                         memory_space=pltpu.SMEM),
            pl.BlockSpec((COMBINE_ROWS, LANE), lambda i: (i, 0)),
            pl.BlockSpec(memory_space=pl.ANY),
            pl.BlockSpec(memory_space=pl.ANY),
            pl.BlockSpec((COMBINE_ROWS, D_MODEL), lambda i: (i, 0)),
            pl.BlockSpec((1, D_MODEL), lambda i: (0, 0)),
        ],
        out_specs=_group_specs(COMBINE_ROWS, D_MODEL, n_p_blocks),
        out_shape=[jax.ShapeDtypeStruct((n_p, D_MODEL), F32), jax.ShapeDtypeStruct((n - n_p, D_MODEL), F32)],
        scratch_shapes=[pltpu.VMEM((2, TOP_K * COMBINE_ROWS * TOK_TILES, LANE), F32),
                        pltpu.SemaphoreType.DMA((2,))],
        compiler_params=pltpu.CompilerParams(dimension_semantics=("arbitrary",),
                                             vmem_limit_bytes=VMEM_LIMIT),
        name="combine",
    )(dest_blocks, dest_blocks, rt, y_2d.reshape(-1, TOK_TILES, LANE), y_2d, x1, g)


def _route(rt, cnt):
    n = rt.shape[0]
    n_assign = n * TOP_K
    flat_e = rt[:, 0:TOP_K].astype(jnp.int32).reshape(-1)
    rank = rt[:, TOP_K:2 * TOP_K].astype(jnp.int32).reshape(-1)
    counts = cnt[:, 0].astype(jnp.int32)
    padded = (counts + EXPERT_ROWS - 1) // EXPERT_ROWS * EXPERT_ROWS
    pend = jnp.cumsum(padded)
    pstart = pend - padded
    dest = (pstart[flat_e] + rank).astype(jnp.int32)
    n_rows = n_assign + N_EXPERTS * EXPERT_ROWS
    n_blocks = n_rows // EXPERT_ROWS
    block_first = jnp.arange(n_blocks, dtype=jnp.int32) * EXPERT_ROWS
    block_e = jnp.minimum(jnp.sum((pend[None, :] <= block_first[:, None]).astype(jnp.int32), axis=1),
                          N_EXPERTS - 1).astype(jnp.int32)
    n_used = (pend[-1] // EXPERT_ROWS).astype(jnp.int32).reshape(1)
    ids = jnp.arange(N_EXPERTS, dtype=jnp.int32)
    present = padded > 0
    group = jnp.cumsum(present.astype(jnp.int32)) - 1
    later = jnp.flip(lax.cummin(jnp.flip(jnp.where(present, ids, N_EXPERTS))))
    next_e = jnp.concatenate([later[1:], jnp.full((1,), N_EXPERTS, jnp.int32)])
    next_e = jnp.where(next_e >= N_EXPERTS, -1, next_e)
    prev_block_e = jnp.concatenate([jnp.full((1,), -1, jnp.int32), block_e[:-1]])
    first = ((block_e != prev_block_e) & (jnp.arange(n_blocks) < n_used[0])).astype(jnp.int32)
    wslot = (group[block_e] % 2).astype(jnp.int32)
    return dest, n_rows, (block_e, n_used, first, wslot, next_e[block_e].astype(jnp.int32))


def _pad_lanes(v, width):
    return jnp.zeros((1, width), F32).at[0, :v.shape[0]].set(v.astype(F32))


def kernel(x_prompt, x_sample, state_gdn_conv, state_gdn, state_gla, rms_mix_w, w_in, conv_w, gdn_a_log,
           gdn_dt_bias, gdn_norm_w, gla_gk_w, gla_gk_b, gla_norm_w, w_out, rms_ffn_w, w_router, b_router,
           w_up, b_up, w_down, b_down, rms_final_w):
    bp, tp, d = x_prompt.shape
    bs, ts, _ = x_sample.shape
    n_p, n_s = bp * tp, bs * ts
    assert d == D_MODEL and state_gdn.shape[0] == 1, "single-layer kernel"
    assert tp >= CONV_WIDTH - 1 and ts >= CONV_WIDTH - 1, "new conv state is taken from the new tokens only"
    l = 0

    wi = w_in[l]
    a0 = GDN_CONV_CH + GDN_V_W
    g0 = a0 + 2 * GDN_HEADS
    lr0 = g0 + 2 * GLA_QK_W + 2 * GLA_V_W
    small = jnp.concatenate([wi[:, a0:a0 + 2 * GDN_HEADS], wi[:, lr0:lr0 + GLA_GATE_RANK],
                             jnp.zeros((d, SM_W - 2 * GDN_HEADS - GLA_GATE_RANK), F32)], axis=1)
    w_big = jnp.concatenate([wi[:, :a0], wi[:, g0:lr0], small], axis=1).astype(BF16)
    alog = _pad_lanes(gdn_a_log[l], SM_W)
    dtb = _pad_lanes(gdn_dt_bias[l], SM_W)
    wgk = jnp.zeros((SM_W, GLA_QK_W), F32).at[SM_LR:SM_LR + GLA_GATE_RANK].set(gla_gk_w[l])
    wr = jnp.zeros((d, LANE), F32).at[:, :N_EXPERTS].set(w_router[l])
    br = jnp.full((1, LANE), -1e30, F32).at[0, :N_EXPERTS].set(b_router[l])

    assert n_p % ROW_TILE == 0 and n_s % ROW_TILE == 0
    x_p, x_s = x_prompt.reshape(n_p, d), x_sample.reshape(n_s, d)
    proj = _inproj(x_p, x_s, rms_mix_w[l][None, :], w_big)

    tb_p = PROMPT_TIME_BLOCK
    zeros_conv = jnp.zeros((bp, CONV_WIDTH - 1, GDN_CONV_CH), F32)
    og_p, gdn_p, conv_p = _gdn(proj, bp, 1, tp, tb_p, CHUNK, CHUNK, zeros_conv,
                               jnp.zeros((bp, GDN_HEADS, GDN_DK, GDN_DV), F32), conv_w[l], alog, dtb,
                               gdn_norm_w[l][None, :])
    ol_p, gla_p = _gla(proj, bp, 1, tp, tb_p, CHUNK, CHUNK, jnp.zeros((bp, GLA_HEADS, GLA_DK, GLA_DV), F32),
                       wgk, gla_gk_b[l][None, :], gla_norm_w[l][None, :])

    ts_pad = SUBLANE
    nb_s = SAMPLE_SEQS_PER_STEP
    proj_s = proj[n_p:].reshape(bs, ts, PROJ_W)
    proj_sp = jnp.pad(proj_s, ((0, 0), (0, ts_pad - ts), (0, 0))).reshape(bs * ts_pad, PROJ_W)
    og_s, gdn_s, conv_s = _gdn(proj_sp, bs, nb_s, ts_pad, ts_pad, ts_pad, ts, state_gdn_conv[l], state_gdn[l],
                               conv_w[l], alog, dtb, gdn_norm_w[l][None, :])
    ol_s, gla_s = _gla(proj_sp, bs, nb_s, ts_pad, ts_pad, ts_pad, ts, state_gla[l], wgk, gla_gk_b[l][None, :],
                       gla_norm_w[l][None, :])
    og_s = og_s.reshape(bs, ts_pad, GDN_V_W)[:, :ts].reshape(n_s, GDN_V_W)
    ol_s = ol_s.reshape(bs, ts_pad, GLA_V_W)[:, :ts].reshape(n_s, GLA_V_W)

    x1, h2, rt, cnt = _outproj(og_p, og_s, ol_p, ol_s, x_p, x_s, w_out[l].astype(BF16), rms_ffn_w[l][None, :],
                               wr, br)

    dest, n_rows, block_meta = _route(rt, cnt)
    dest_kmajor = dest.reshape(-1, TOP_K).T.reshape(-1)
    xs = _dispatch(h2.reshape(-1, TOK_TILES, LANE), dest_kmajor, n_rows)
    y_rows = _experts(block_meta, xs.reshape(-1, LANE), w_up[l], b_up[l], w_down[l], b_down[l])
    y_p, y_s = _combine(dest, rt, y_rows, x1, rms_final_w[None, :], n_p)
    y_prompt = y_p.reshape(bp, tp, d)
    y_sample = y_s.reshape(bs, ts, d)
    return (y_prompt, y_sample, conv_p[None], gdn_p[None], gla_p[None], conv_s[None], gdn_s[None], gla_s[None])
```

```python
import functools

import jax
import jax.numpy as jnp
from jax import lax
from jax.experimental import pallas as pl
from jax.experimental.pallas import tpu as pltpu
from jax.experimental.pallas import tpu_sc as plsc

F32 = jnp.float32
BF16 = jnp.bfloat16
HI = lax.Precision.HIGHEST

D_MODEL = 1024
GDN_HEADS = 4
GDN_DK = 128
GDN_DV = 128
GLA_HEADS = 4
GLA_DK = 64
GLA_DV = 128
GLA_GATE_RANK = 16
GLA_GATE_NORMALIZER = 16.0
CONV_WIDTH = 4
CHUNK = 64
N_EXPERTS = 32
TOP_K = 4
D_FF = 1024
SWIGLU_LIMIT = 7.0
SWIGLU_ALPHA = 1.702
RMS_EPS = 1e-6
L2_EPS = 1e-6

GDN_QK_W = GDN_HEADS * GDN_DK
GDN_V_W = GDN_HEADS * GDN_DV
GDN_CONV_CH = 2 * GDN_QK_W + GDN_V_W
GLA_QK_W = GLA_HEADS * GLA_DK
GLA_V_W = GLA_HEADS * GLA_DV

COL_QKV = 0
COL_Z = 1536
COL_GQ = 2048
COL_GK = 2304
COL_GV = 2560
COL_GG = 3072
COL_SM = 3584
SM_W = 128
PROJ_W = COL_SM + SM_W
SM_A, SM_B, SM_LR = 0, 4, 8

LANE = 128
SUBLANE = 8
TOK_TILES = D_MODEL // LANE
ROW_TILE = 256
EXPERT_ROWS = 512
EXPERT_WEIGHT_QUEUE = 1
COMBINE_ROWS = 128
DMA_ISSUE_UNROLL = 8
COMBINE_GATHER_QUEUES = (0, 1)
GDN_CHUNKS_PER_TRIP = 4
GLA_CHUNKS_PER_TRIP = 4
PROMPT_TIME_BLOCK = 512
SAMPLE_SEQS_PER_STEP = 8
VMEM_LIMIT = 56 * 1024 * 1024


def _dot(a, b):
    return jnp.dot(a.astype(BF16), b.astype(BF16), preferred_element_type=F32)


def _dot_nt(a, b):
    return lax.dot_general(a.astype(BF16), b.astype(BF16), (((1,), (1,)), ((), ())),
                           preferred_element_type=F32)


def _dot_tn(a, b):
    return lax.dot_general(a.astype(BF16), b.astype(BF16), (((0,), (0,)), ((), ())),
                           preferred_element_type=F32)


def _dot_hi(a, b):
    return jnp.dot(a, b, precision=HI, preferred_element_type=F32)


def _dot_3pass(a, b):
    a_hi = a.astype(BF16)
    b_hi = b.astype(BF16)
    a_lo = (a - a_hi.astype(F32)).astype(BF16)
    b_lo = (b - b_hi.astype(F32)).astype(BF16)

    def mm(x, y):
        return jnp.dot(x, y, preferred_element_type=F32)

    return (mm(a_lo, b_hi) + mm(a_hi, b_lo)) + mm(a_hi, b_hi)


def _rms(x, w):
    return x * lax.rsqrt(jnp.mean(x * x, axis=-1, keepdims=True) + RMS_EPS) * w


def _silu(x):
    return x * jax.nn.sigmoid(x)


def _group_specs(rows, width, n_p_blocks):
    return [pl.BlockSpec((rows, width), lambda i: (jnp.minimum(i, n_p_blocks - 1), 0)),
            pl.BlockSpec((rows, width), lambda i: (jnp.maximum(i - n_p_blocks, 0), 0))]


def _group_pick(i, n_p_blocks, p_ref, s_ref):
    return jnp.where(i < n_p_blocks, p_ref[...], s_ref[...])


def _inproj_body(xp_ref, xs_ref, g_ref, w_ref, o_ref, *, n_p_blocks):
    x = _group_pick(pl.program_id(0), n_p_blocks, xp_ref, xs_ref)
    h = _rms(x, g_ref[...])
    o_ref[...] = jnp.dot(h.astype(BF16), w_ref[...], preferred_element_type=F32)


def _inproj(x_p, x_s, g, w):
    n_p_blocks, n_s_blocks = x_p.shape[0] // ROW_TILE, x_s.shape[0] // ROW_TILE
    n = x_p.shape[0] + x_s.shape[0]
    return pl.pallas_call(
        functools.partial(_inproj_body, n_p_blocks=n_p_blocks),
        grid=(n_p_blocks + n_s_blocks,),
        in_specs=_group_specs(ROW_TILE, D_MODEL, n_p_blocks) + [
            pl.BlockSpec((1, D_MODEL), lambda i: (0, 0)),
            pl.BlockSpec((D_MODEL, PROJ_W), lambda i: (0, 0)),
        ],
        out_specs=pl.BlockSpec((ROW_TILE, PROJ_W), lambda i: (i, 0)),
        out_shape=jax.ShapeDtypeStruct((n, PROJ_W), F32),
        compiler_params=pltpu.CompilerParams(dimension_semantics=("arbitrary",),
                                             vmem_limit_bytes=VMEM_LIMIT),
        name="in_proj",
    )(x_p, x_s, g, w)


def _log2(n):
    assert n & (n - 1) == 0
    return n.bit_length() - 1


def _tri_inv_all(ms, c, ii, jj):
    eye = (ii == jj).astype(F32)
    base = min(c, 8)
    sh = _log2(base)
    blk = (ii >> sh) == (jj >> sh)
    ns = [jnp.where(blk, m, 0.0) for m in ms]
    xs = [eye - n for n in ns]
    ps = [_dot(n, n) for n in ns]
    ts = [_dot(jnp.concatenate([x, p], axis=0), p) for x, p in zip(xs, ps)]
    xs = [x + t[:c] for x, t in zip(xs, ts)]
    ps = [t[c:] for t in ts]
    xs = [x + _dot(x, p) for x, p in zip(xs, ps)]
    s = base
    while s < c:
        sh_s, sh_b = _log2(s), _log2(2 * s)
        off = ((ii >> sh_b) == (jj >> sh_b)) & ((ii >> sh_s) != (jj >> sh_s))
        ys = [_dot(x, jnp.where(off, m, 0.0)) for x, m in zip(xs, ms)]
        xs = [x - _dot(y, x) for x, y in zip(xs, ys)]
        s *= 2
    return xs


def _gated_norm(o, w, z):
    return o * lax.rsqrt(jnp.mean(o * o, axis=-1, keepdims=True) + RMS_EPS) * w * _silu(z)


def _chunk_rows(s, tb_rows, ci, c):
    r = s * tb_rows + ci * c
    if not isinstance(r, int):
        r = pl.multiple_of(r, c)
    return r


def _for_chunks(n_chunks, step):
    if n_chunks == 1:
        step(0, 0)
    else:
        lax.fori_loop(0, n_chunks, step, 0)


def _gdn_body(qkv_ref, z_ref, sm_ref, cbuf_ref, s0_ref, cw_ref, alog_ref, dtb_ref, nw_ref,
              o_ref, sout_ref, cout_ref, st, xc, act, gcs, us, wss, qgs, kds, aqs,
              *, nb, tb_rows, chunk, valid, n_tb):
    tb = pl.program_id(1)
    c = chunk
    n_heads = GDN_HEADS
    tail = CONV_WIDTH - 1
    pad = SUBLANE
    units = [(s, h) for s in range(nb) for h in range(n_heads)]

    @pl.when(tb == 0)
    def _():
        st[...] = s0_ref[...]
        for s in range(nb):
            xc[s, pad - tail:pad, :] = cbuf_ref[s]

    if n_tb > 1:
        @pl.when(tb > 0)
        def _():
            for s in range(nb):
                xc[s, pad - tail:pad, :] = xc[s, tb_rows + pad - tail:tb_rows + pad, :]

    for s in range(nb):
        xc[s, pad:pad + tb_rows, :] = qkv_ref[s * tb_rows:(s + 1) * tb_rows, :]

    slab = min(tb_rows, 64)
    for s in range(nb):
        for sl in range(tb_rows // slab):
            for cb in range(GDN_CONV_CH // 512):
                cs = slice(cb * 512, (cb + 1) * 512)
                lo = pad - tail + sl * slab
                acc = xc[s, lo:lo + slab, cs] * cw_ref[0:1, cs]
                for i in range(1, CONV_WIDTH):
                    acc = acc + xc[s, lo + i:lo + i + slab, cs] * cw_ref[i:i + 1, cs]
                act[s * tb_rows + sl * slab:s * tb_rows + (sl + 1) * slab, cs] = _silu(acc)

    ii = lax.broadcasted_iota(jnp.int32, (c, c), 0)
    jj = lax.broadcasted_iota(jnp.int32, (c, c), 1)
    lower = (ii >= jj)
    lower_f = lower.astype(F32)
    strict = (ii > jj)
    rowmask = None
    if valid < c:
        rowmask = lax.broadcasted_iota(jnp.int32, (c, 1), 0) < valid

    def hs(h, w):
        return slice(h * w, (h + 1) * w)

    n_chunks = tb_rows // c
    cpi = next(k for k in (GDN_CHUNKS_PER_TRIP, 2, 1) if n_chunks % k == 0)
    p1_units = [(g, h) for g in range(nb * cpi) for h in range(n_heads)]

    def phase1(ci, carry):
        rows, b_ts, gc_ts, gc_tts = [], [], [], []
        for g in range(nb * cpi):
            rr = pl.ds(_chunk_rows(g // cpi, tb_rows, ci * cpi + g % cpi, c), c)
            sm = sm_ref[rr, :]
            g_t = -jnp.exp(alog_ref[...]) * jax.nn.softplus(sm + dtb_ref[...])
            b_t = jax.nn.sigmoid(sm)
            if rowmask is not None:
                g_t = jnp.where(rowmask, g_t, 0.0)
                b_t = jnp.where(rowmask, b_t, 0.0)
            gc_t = _dot_hi(lower_f, g_t)
            gcs[rr, :] = gc_t
            rows.append(rr)
            b_ts.append(b_t)
            gc_ts.append(gc_t)
            gc_tts.append(gc_t.T)
        qn, kn, kb, vb = {}, {}, {}, {}
        for (s, h) in p1_units:
            q = act[rows[s], hs(h, GDN_DK)]
            k = act[rows[s], slice(GDN_QK_W + h * GDN_DK, GDN_QK_W + (h + 1) * GDN_DK)]
            v = act[rows[s], slice(2 * GDN_QK_W + h * GDN_DV, 2 * GDN_QK_W + (h + 1) * GDN_DV)]
            if rowmask is not None:
                q = jnp.where(rowmask, q, 0.0)
                k = jnp.where(rowmask, k, 0.0)
                v = jnp.where(rowmask, v, 0.0)
            qn[s, h] = q * lax.rsqrt(jnp.sum(q * q, axis=-1, keepdims=True) + L2_EPS) * (GDN_DK ** -0.5)
            kn[s, h] = k * lax.rsqrt(jnp.sum(k * k, axis=-1, keepdims=True) + L2_EPS)
            beta = b_ts[s][:, SM_B + h:SM_B + h + 1]
            kb[s, h] = kn[s, h] * beta
            vb[s, h] = v * beta
        s1 = {u: _dot_nt(jnp.concatenate([kb[u], qn[u]], axis=0), kn[u]) for u in p1_units}
        mm = []
        for (s, h) in p1_units:
            gcol = gc_ts[s][:, SM_A + h:SM_A + h + 1]
            grow = gc_tts[s][SM_A + h:SM_A + h + 1, :]
            dec = jnp.exp(jnp.where(lower, gcol - grow, -jnp.inf))
            mm.append(jnp.where(strict, s1[s, h][:c] * dec, 0.0))
            aqs[h, rows[s], :] = s1[s, h][c:] * dec
        tms = _tri_inv_all(mm, c, ii, jj)
        for (s, h), tm in zip(p1_units, tms):
            gcol = gc_ts[s][:, SM_A + h:SM_A + h + 1]
            eg = jnp.exp(gcol)
            uw = _dot(tm, jnp.concatenate([vb[s, h], kb[s, h] * eg], axis=1))
            us[rows[s], hs(h, GDN_DV)] = uw[:, :GDN_DV]
            wss[rows[s], hs(h, GDN_DV)] = uw[:, GDN_DV:]
            qgs[rows[s], hs(h, GDN_DK)] = qn[s, h] * eg
            kds[rows[s], hs(h, GDN_DK)] = kn[s, h] * jnp.exp(gcol[c - 1:c, :] - gcol)
        return carry

    def phase2(ci, carry):
        r0 = [_chunk_rows(s, tb_rows, ci, c) for s in range(nb)]
        rows = [pl.ds(r, c) for r in r0]
        ws = {(s, h): _dot(jnp.concatenate([wss[rows[s], hs(h, GDN_DV)], qgs[rows[s], hs(h, GDN_DK)]], axis=0),
                           st[s, h]) for (s, h) in units}
        v_new = {(s, h): us[rows[s], hs(h, GDN_DV)] - ws[s, h][:c] for (s, h) in units}
        o = {(s, h): ws[s, h][c:] + _dot(aqs[h, rows[s], :], v_new[s, h]) for (s, h) in units}
        upd = {(s, h): _dot_tn(kds[rows[s], hs(h, GDN_DK)], v_new[s, h]) for (s, h) in units}
        for (s, h) in units:
            g_last = gcs[pl.ds(r0[s] + c - 1, 1), SM_A + h:SM_A + h + 1]
            st[s, h] = st[s, h] * jnp.exp(g_last) + upd[s, h]
        for s in range(nb):
            o_ref[rows[s], :] = jnp.concatenate(
                [_gated_norm(o[s, h], nw_ref[...], z_ref[rows[s], hs(h, GDN_DV)]) for h in range(n_heads)], axis=1)
        return carry

    _for_chunks(n_chunks // cpi, phase1)
    _for_chunks(n_chunks, phase2)

    @pl.when(tb == n_tb - 1)
    def _():
        sout_ref[...] = st[...]
        last = tb_rows if valid == c else valid
        for s in range(nb):
            cout_ref[s] = xc[s, pad + last - tail:pad + last, :]


def _gdn(proj, n_seq, nb, t_len, tb_rows, chunk, valid, conv_buf, s0, conv_w, alog, dtb, nw):
    n_tb = t_len // tb_rows
    assert nb == 1 or n_tb == 1
    rows = nb * tb_rows

    def rowblk(b, t):
        return b * n_tb + t

    body = functools.partial(_gdn_body, nb=nb, tb_rows=tb_rows, chunk=chunk, valid=valid, n_tb=n_tb)
    return pl.pallas_call(
        body,
        grid=(n_seq // nb, n_tb),
        in_specs=[
            pl.BlockSpec((rows, GDN_CONV_CH), lambda b, t: (rowblk(b, t), COL_QKV // GDN_CONV_CH)),
            pl.BlockSpec((rows, GDN_V_W), lambda b, t: (rowblk(b, t), COL_Z // GDN_V_W)),
            pl.BlockSpec((rows, SM_W), lambda b, t: (rowblk(b, t), COL_SM // SM_W)),
            pl.BlockSpec((nb, CONV_WIDTH - 1, GDN_CONV_CH), lambda b, t: (b, 0, 0)),
            pl.BlockSpec((nb, GDN_HEADS, GDN_DK, GDN_DV), lambda b, t: (b, 0, 0, 0)),
            pl.BlockSpec((CONV_WIDTH, GDN_CONV_CH), lambda b, t: (0, 0)),
            pl.BlockSpec((1, SM_W), lambda b, t: (0, 0)),
            pl.BlockSpec((1, SM_W), lambda b, t: (0, 0)),
            pl.BlockSpec((1, GDN_DV), lambda b, t: (0, 0)),
        ],
        out_specs=[
            pl.BlockSpec((rows, GDN_V_W), lambda b, t: (rowblk(b, t), 0)),
            pl.BlockSpec((nb, GDN_HEADS, GDN_DK, GDN_DV), lambda b, t: (b, 0, 0, 0)),
            pl.BlockSpec((nb, CONV_WIDTH - 1, GDN_CONV_CH), lambda b, t: (b, 0, 0)),
        ],
        out_shape=[
            jax.ShapeDtypeStruct((n_seq * t_len, GDN_V_W), F32),
            jax.ShapeDtypeStruct((n_seq, GDN_HEADS, GDN_DK, GDN_DV), F32),
            jax.ShapeDtypeStruct((n_seq, CONV_WIDTH - 1, GDN_CONV_CH), F32),
        ],
        scratch_shapes=[
            pltpu.VMEM((nb, GDN_HEADS, GDN_DK, GDN_DV), F32),
            pltpu.VMEM((nb, tb_rows + SUBLANE, GDN_CONV_CH), F32),
            pltpu.VMEM((rows, GDN_CONV_CH), F32),
            pltpu.VMEM((rows, SM_W), F32),
            pltpu.VMEM((rows, GDN_V_W), F32),
            pltpu.VMEM((rows, GDN_V_W), F32),
            pltpu.VMEM((rows, GDN_QK_W), F32),
            pltpu.VMEM((rows, GDN_QK_W), F32),
            pltpu.VMEM((GDN_HEADS, rows, chunk), F32),
        ],
        compiler_params=pltpu.CompilerParams(dimension_semantics=("arbitrary", "arbitrary"),
                                             vmem_limit_bytes=VMEM_LIMIT),
        name="gdn_mixer",
    )(proj, proj, proj, conv_buf, s0, conv_w, alog, dtb, nw)


def _gla_body(q_ref, k_ref, v_ref, go_ref, sm_ref, s0_ref, wgk_ref, bgk_ref, nw_ref,
              o_ref, sout_ref, st, qes, ois, upds, decs, *, nb, tb_rows, chunk, valid, n_tb):
    tb = pl.program_id(1)
    c = chunk
    n_heads = GLA_HEADS
    units = [(s, h) for s in range(nb) for h in range(n_heads)]

    @pl.when(tb == 0)
    def _():
        st[...] = s0_ref[...]

    ii = lax.broadcasted_iota(jnp.int32, (c, c), 0)
    jj = lax.broadcasted_iota(jnp.int32, (c, c), 1)
    lower = (ii >= jj)
    lower_f = lower.astype(F32)
    rid = lax.broadcasted_iota(jnp.int32, (c, 1), 0)
    rowmask = (rid < valid) if valid < c else None
    n_sub = max(c // 16, 1)
    sub = c // n_sub

    n_chunks = tb_rows // c
    cpi = next(k for k in (GLA_CHUNKS_PER_TRIP, 2, 1) if n_chunks % k == 0)
    p1_units = [(g, h) for g in range(nb * cpi) for h in range(n_heads)]

    def phase1(ci, carry):
        rows, slots, bcs, bc_ts = [], [], [], []
        for g in range(nb * cpi):
            chunk_idx = ci * cpi + g % cpi
            rr = pl.ds(_chunk_rows(g // cpi, tb_rows, chunk_idx, c), c)
            slots.append((g // cpi) * n_chunks + chunk_idx)
            gk = jax.nn.log_sigmoid(_dot(sm_ref[rr, :], wgk_ref[...]) + bgk_ref[...]) / GLA_GATE_NORMALIZER
            if rowmask is not None:
                gk = jnp.where(rowmask, gk, 0.0)
            bc = _dot_hi(lower_f, gk)
            rows.append(rr)
            bcs.append(bc)
            bc_ts.append(bc.T)
        q, k, v, bch = {}, {}, {}, {}
        for (s, h) in p1_units:
            ks = slice(h * GLA_DK, (h + 1) * GLA_DK)
            vs = slice(h * GLA_DV, (h + 1) * GLA_DV)
            q[s, h] = q_ref[rows[s], ks] * (GLA_DK ** -0.5)
            kk = k_ref[rows[s], ks]
            vv = v_ref[rows[s], vs]
            if rowmask is not None:
                kk = jnp.where(rowmask, kk, 0.0)
                vv = jnp.where(rowmask, vv, 0.0)
            k[s, h], v[s, h] = kk, vv
            bch[s, h] = bcs[s][:, ks]
        for (g, h) in p1_units:
            qes[h, rows[g], :] = q[g, h] * jnp.exp(bch[g, h])
        a = {}
        for u in p1_units:
            q_parts, k_parts = [], []
            for sb in range(n_sub):
                ref_row = bch[u][sb * sub:sb * sub + 1, :]
                in_blk = (rid >= sb * sub) & (rid < (sb + 1) * sub)
                q_parts.append(jnp.where(in_blk, q[u] * jnp.exp(jnp.where(in_blk, bch[u] - ref_row, 0.0)), 0.0))
                k_parts.append(k[u] * jnp.exp(jnp.where(rid < (sb + 1) * sub, ref_row - bch[u], 0.0)))
            q_hat = jnp.concatenate(q_parts, axis=1) if n_sub > 1 else q_parts[0]
            k_hat = jnp.concatenate(k_parts, axis=1) if n_sub > 1 else k_parts[0]
            a[u] = jnp.where(lower, _dot_nt(q_hat, k_hat), 0.0)
        upd = {u: _dot_tn(k[u] * jnp.exp(bch[u][c - 1:c, :] - bch[u]), v[u]) for u in p1_units}
        o_intra = {u: _dot(a[u], v[u]) for u in p1_units}
        for (g, h) in p1_units:
            dec_col = bc_ts[g][h * GLA_DK:(h + 1) * GLA_DK, c - 1:c]
            decs[slots[g], h] = jnp.broadcast_to(jnp.exp(dec_col), (GLA_DK, GLA_DV))
            upds[slots[g], h] = upd[g, h]
            ois[rows[g], h * GLA_DV:(h + 1) * GLA_DV] = o_intra[g, h]
        return carry

    def phase2(ci, carry):
        rows = [pl.ds(_chunk_rows(s, tb_rows, ci, c), c) for s in range(nb)]
        o = {(s, h): ois[rows[s], h * GLA_DV:(h + 1) * GLA_DV] + _dot(qes[h, rows[s], :], st[s, h])
             for (s, h) in units}
        for (s, h) in units:
            st[s, h] = decs[s * n_chunks + ci, h] * st[s, h] + upds[s * n_chunks + ci, h]
        for s in range(nb):
            o_ref[rows[s], :] = jnp.concatenate(
                [_gated_norm(o[s, h], nw_ref[...], go_ref[rows[s], h * GLA_DV:(h + 1) * GLA_DV])
                 for h in range(n_heads)], axis=1)
        return carry

    _for_chunks(n_chunks // cpi, phase1)
    _for_chunks(n_chunks, phase2)

    @pl.when(tb == n_tb - 1)
    def _():
        sout_ref[...] = st[...]


def _gla(proj, n_seq, nb, t_len, tb_rows, chunk, valid, s0, wgk, bgk, nw):
    n_tb = t_len // tb_rows
    assert nb == 1 or n_tb == 1
    rows = nb * tb_rows

    def rowblk(b, t):
        return b * n_tb + t

    body = functools.partial(_gla_body, nb=nb, tb_rows=tb_rows, chunk=chunk, valid=valid, n_tb=n_tb)
    return pl.pallas_call(
        body,
        grid=(n_seq // nb, n_tb),
        in_specs=[
            pl.BlockSpec((rows, GLA_QK_W), lambda b, t: (rowblk(b, t), COL_GQ // GLA_QK_W)),
            pl.BlockSpec((rows, GLA_QK_W), lambda b, t: (rowblk(b, t), COL_GK // GLA_QK_W)),
            pl.BlockSpec((rows, GLA_V_W), lambda b, t: (rowblk(b, t), COL_GV // GLA_V_W)),
            pl.BlockSpec((rows, GLA_V_W), lambda b, t: (rowblk(b, t), COL_GG // GLA_V_W)),
            pl.BlockSpec((rows, SM_W), lambda b, t: (rowblk(b, t), COL_SM // SM_W)),
            pl.BlockSpec((nb, GLA_HEADS, GLA_DK, GLA_DV), lambda b, t: (b, 0, 0, 0)),
            pl.BlockSpec((SM_W, GLA_QK_W), lambda b, t: (0, 0)),
            pl.BlockSpec((1, GLA_QK_W), lambda b, t: (0, 0)),
            pl.BlockSpec((1, GLA_DV), lambda b, t: (0, 0)),
        ],
        out_specs=[
            pl.BlockSpec((rows, GLA_V_W), lambda b, t: (rowblk(b, t), 0)),
            pl.BlockSpec((nb, GLA_HEADS, GLA_DK, GLA_DV), lambda b, t: (b, 0, 0, 0)),
        ],
        out_shape=[
            jax.ShapeDtypeStruct((n_seq * t_len, GLA_V_W), F32),
            jax.ShapeDtypeStruct((n_seq, GLA_HEADS, GLA_DK, GLA_DV), F32),
        ],
        scratch_shapes=[
            pltpu.VMEM((nb, GLA_HEADS, GLA_DK, GLA_DV), F32),
            pltpu.VMEM((GLA_HEADS, rows, GLA_DK), F32),
            pltpu.VMEM((rows, GLA_V_W), F32),
            pltpu.VMEM((rows // chunk, GLA_HEADS, GLA_DK, GLA_DV), F32),
            pltpu.VMEM((rows // chunk, GLA_HEADS, GLA_DK, GLA_DV), F32),
        ],
        compiler_params=pltpu.CompilerParams(dimension_semantics=("arbitrary", "arbitrary"),
                                             vmem_limit_bytes=VMEM_LIMIT),
        name="gla_mixer",
    )(proj, proj, proj, proj, proj, s0, wgk, bgk, nw)


def _outproj_body(ogp_ref, ogs_ref, olp_ref, ols_ref, xp_ref, xs_ref, wo_ref, g_ref, wr_ref, br_ref,
                  x1_ref, h2_ref, rt_ref, rtt_ref, cnt_ref, base, *, n_p_blocks):
    i = pl.program_id(0)

    @pl.when(i == 0)
    def _():
        base[...] = jnp.zeros_like(base)

    o = jnp.concatenate([_group_pick(i, n_p_blocks, ogp_ref, ogs_ref),
                         _group_pick(i, n_p_blocks, olp_ref, ols_ref)], axis=1)
    x1 = _group_pick(i, n_p_blocks, xp_ref, xs_ref) + jnp.dot(o.astype(BF16), wo_ref[...],
                                                               preferred_element_type=F32)
    x1_ref[...] = x1
    h = _rms(x1, g_ref[...])
    _store_token_tiles(h2_ref, h)
    logits = _dot_3pass(h, wr_ref[...]) + br_ref[...]

    tm = logits.shape[0]
    lt = logits.T[:N_EXPERTS]
    eid = lax.broadcasted_iota(jnp.int32, (N_EXPERTS, tm), 0)
    work = lt
    sel = jnp.zeros((N_EXPERTS, tm), F32)
    hits, ids, vals = [], [], []
    for _ in range(TOP_K):
        m = jnp.max(work, axis=0, keepdims=True)
        idx = jnp.min(jnp.where(work == m, eid, N_EXPERTS), axis=0, keepdims=True)
        hit = eid == idx
        hits.append(hit)
        ids.append(idx)
        vals.append(m)
        work = jnp.where(hit, -jnp.inf, work)
        sel = sel + hit.astype(F32)
    exps = [jnp.exp(v - vals[0]) for v in vals]
    den = exps[0]
    for e in exps[1:]:
        den = den + e
    gates = [e / den for e in exps]

    ri = lax.broadcasted_iota(jnp.int32, (tm, tm), 0)
    ci = lax.broadcasted_iota(jnp.int32, (tm, tm), 1)
    before = _dot(sel, (ri < ci).astype(F32)) + base[...]
    ranks = [jnp.sum(jnp.where(hit, before, 0.0), axis=0, keepdims=True) for hit in hits]
    base[...] = base[...] + jnp.sum(sel, axis=1, keepdims=True)
    cnt_ref[...] = base[...]

    row = lax.broadcasted_iota(jnp.int32, (LANE, tm), 0)
    rec = jnp.zeros((LANE, tm), F32)
    for k in range(TOP_K):
        rec = jnp.where(row == k, ids[k].astype(F32), rec)
        rec = jnp.where(row == TOP_K + k, ranks[k], rec)
        rec = jnp.where(row == 2 * TOP_K + k, gates[k], rec)
    rt_ref[...] = rec.T
    rtt_ref[...] = rec[:2 * TOP_K]


def _outproj(og_p, og_s, ol_p, ol_s, x_p, x_s, wo, g, wr, br):
    n_p_blocks, n_s_blocks = x_p.shape[0] // ROW_TILE, x_s.shape[0] // ROW_TILE
    n = x_p.shape[0] + x_s.shape[0]
    return pl.pallas_call(
        functools.partial(_outproj_body, n_p_blocks=n_p_blocks),
        grid=(n_p_blocks + n_s_blocks,),
        in_specs=_group_specs(ROW_TILE, GDN_V_W, n_p_blocks) + _group_specs(ROW_TILE, GLA_V_W, n_p_blocks)
        + _group_specs(ROW_TILE, D_MODEL, n_p_blocks) + [
            pl.BlockSpec((D_MODEL, D_MODEL), lambda i: (0, 0)),
            pl.BlockSpec((1, D_MODEL), lambda i: (0, 0)),
            pl.BlockSpec((D_MODEL, LANE), lambda i: (0, 0)),
            pl.BlockSpec((1, LANE), lambda i: (0, 0)),
        ],
        out_specs=[
            pl.BlockSpec((ROW_TILE, D_MODEL), lambda i: (i, 0)),
            pl.BlockSpec((ROW_TILE * TOK_TILES, LANE), lambda i: (i, 0)),
            pl.BlockSpec((ROW_TILE, LANE), lambda i: (i, 0)),
            pl.BlockSpec((2 * TOP_K, ROW_TILE), lambda i: (0, i)),
            pl.BlockSpec((N_EXPERTS, 1), lambda i: (0, 0)),
        ],
        out_shape=[
            jax.ShapeDtypeStruct((n, D_MODEL), F32),
            jax.ShapeDtypeStruct((n * TOK_TILES, LANE), F32),
            jax.ShapeDtypeStruct((n, LANE), F32),
            jax.ShapeDtypeStruct((2 * TOP_K, n), F32),
            jax.ShapeDtypeStruct((N_EXPERTS, 1), F32),
        ],
        scratch_shapes=[pltpu.VMEM((N_EXPERTS, 1), F32)],
        compiler_params=pltpu.CompilerParams(dimension_semantics=("arbitrary",),
                                             vmem_limit_bytes=VMEM_LIMIT),
        name="out_proj",
    )(og_p, og_s, ol_p, ol_s, x_p, x_s, wo, g, wr, br)


def _store_token_tiles(ref2d, val):
    rows = val.shape[0]
    for c in range(TOK_TILES):
        ref2d[pl.ds(c, rows, stride=TOK_TILES), :] = val[:, c * LANE:(c + 1) * LANE]


def _load_token_tiles(ref2d, first_row, rows):
    return jnp.concatenate(
        [ref2d[pl.ds(first_row * TOK_TILES + c, rows, stride=TOK_TILES), :] for c in range(TOK_TILES)], axis=1)


def _expert_weight_copies(e, ws, wup_hbm, wdn_hbm, wup_buf, wdn_buf, wsems):
    return (pltpu.make_async_copy(wup_hbm.at[e], wup_buf.at[ws], wsems.at[ws]),
            pltpu.make_async_copy(wdn_hbm.at[e], wdn_buf.at[ws], wsems.at[ws]))


def _expert_body(be_ref, nu_ref, first_ref, wslot_ref, next_ref, x_ref,
                 wup_hbm, bup_ref, wdn_hbm, bdn_ref, y_ref, wup_buf, wdn_buf, wsems):
    i = pl.program_id(0)
    n_used = nu_ref[0]
    ws = wslot_ref[i]
    weight_copies = functools.partial(_expert_weight_copies, wup_hbm=wup_hbm, wdn_hbm=wdn_hbm, wup_buf=wup_buf,
                                      wdn_buf=wdn_buf, wsems=wsems)

    @pl.when((i == 0) & (n_used > 0))
    def _():
        for cp in weight_copies(be_ref[0], ws):
            cp.start(priority=EXPERT_WEIGHT_QUEUE)

    @pl.when(i < n_used)
    def _():
        @pl.when(first_ref[i] == 1)
        def _():
            for cp in weight_copies(be_ref[i], ws):
                cp.wait()

            @pl.when(next_ref[i] >= 0)
            def _():
                for cp in weight_copies(next_ref[i], 1 - ws):
                    cp.start(priority=EXPERT_WEIGHT_QUEUE)

        gu = _dot(_load_token_tiles(x_ref, 0, EXPERT_ROWS), wup_buf[ws]) + bup_ref[...]
        gate = jnp.minimum(gu[:, :D_FF], SWIGLU_LIMIT)
        up = jnp.clip(gu[:, D_FF:], -SWIGLU_LIMIT, SWIGLU_LIMIT)
        a = (up + 1.0) * gate * jax.nn.sigmoid(SWIGLU_ALPHA * gate)
        _store_token_tiles(y_ref, _dot(a, wdn_buf[ws]) + bdn_ref[...])

    @pl.when(i >= n_used)
    def _():
        y_ref[...] = jnp.zeros_like(y_ref)


def _experts(block_meta, xs_2d, w_up, b_up, w_down, b_down):
    n_blocks = block_meta[0].shape[0]
    grid_spec = pltpu.PrefetchScalarGridSpec(
        num_scalar_prefetch=len(block_meta),
        grid=(n_blocks,),
        in_specs=[
            pl.BlockSpec((EXPERT_ROWS * TOK_TILES, LANE), lambda i, *_: (i, 0)),
            pl.BlockSpec(memory_space=pl.ANY),
            pl.BlockSpec((None, 1, 2 * D_FF), lambda i, be, *_: (be[i], 0, 0)),
            pl.BlockSpec(memory_space=pl.ANY),
            pl.BlockSpec((None, 1, D_MODEL), lambda i, be, *_: (be[i], 0, 0)),
        ],
        out_specs=pl.BlockSpec((EXPERT_ROWS * TOK_TILES, LANE), lambda i, *_: (i, 0)),
        scratch_shapes=[
            pltpu.VMEM((2, D_MODEL, 2 * D_FF), F32),
            pltpu.VMEM((2, D_FF, D_MODEL), F32),
            pltpu.SemaphoreType.DMA((2,)),
        ],
    )
    return pl.pallas_call(
        _expert_body,
        grid_spec=grid_spec,
        out_shape=jax.ShapeDtypeStruct((n_blocks * EXPERT_ROWS * TOK_TILES, LANE), F32),
        compiler_params=pltpu.CompilerParams(dimension_semantics=("arbitrary",),
                                             vmem_limit_bytes=VMEM_LIMIT),
        name="experts",
    )(*block_meta, xs_2d, w_up, b_up.reshape(N_EXPERTS, 1, 2 * D_FF), w_down,
      b_down.reshape(N_EXPERTS, 1, D_MODEL))


def _dispatch(h_tiles, dest_kmajor, n_rows):
    n_tok = h_tiles.shape[0]
    info = plsc.get_sparse_core_info()
    n_workers = info.num_cores * info.num_subcores
    per_worker = n_tok // n_workers
    chunk = next(c for c in (64, 48, 32, 16, 8) if per_worker % c == 0)
    assert n_tok % n_workers == 0 and per_worker % SUBLANE == 0
    mesh = plsc.VectorSubcoreMesh(core_axis_name="c", subcore_axis_name="s")

    @functools.partial(
        pl.kernel, mesh=mesh,
        out_type=jax.ShapeDtypeStruct((n_rows, TOK_TILES, LANE), F32),
        scratch_types=[pltpu.VMEM((TOP_K, chunk), jnp.int32), pltpu.VMEM((chunk, TOK_TILES, LANE), F32),
                       pltpu.SemaphoreType.DMA],
    )
    def dispatch(h_hbm, dest_hbm, out_hbm, idx_v, rows_v, sem):
        wid = lax.axis_index("s") * info.num_cores + lax.axis_index("c")

        def step(j, carry):
            t0 = pl.multiple_of(wid * per_worker + j * chunk, SUBLANE)
            loads = [pltpu.async_copy(h_hbm.at[pl.ds(t0, chunk)], rows_v, sem)]
            for k in range(TOP_K):
                loads.append(pltpu.async_copy(
                    dest_hbm.at[pl.ds(pl.multiple_of(k * n_tok + t0, SUBLANE), chunk)], idx_v.at[k], sem))
            for cp in loads:
                cp.wait()
            stores = [pltpu.async_copy(rows_v, out_hbm.at[idx_v.at[k]], sem) for k in range(TOP_K)]
            for cp in stores:
                cp.wait()
            return carry

        lax.fori_loop(0, per_worker // chunk, step, 0)

    return dispatch(h_tiles, dest_kmajor)


def _gather_rows(src_tiles, idx_ref, n_rows, dst2d, sem, priorities):
    def issue(j, carry):
        for u in range(DMA_ISSUE_UNROLL):
            r = j * DMA_ISSUE_UNROLL + u
            dst = dst2d.at[pl.ds(pl.multiple_of(r * TOK_TILES, TOK_TILES), TOK_TILES), :]
            pltpu.make_async_copy(src_tiles.at[idx_ref[0, r]], dst, sem).start(
                priority=priorities[u % len(priorities)])
        return carry

    lax.fori_loop(0, n_rows // DMA_ISSUE_UNROLL, issue, 0)


def _wait_rows(src2d, n_rows, dst2d, sem):
    pltpu.make_async_copy(src2d.at[pl.ds(0, n_rows * TOK_TILES), :], dst2d, sem).wait()


def _combine_body(dest_ref, dest_next_ref, rt_ref, y_tiles, y_2d, x1_ref, g_ref, op_ref, os_ref, ybuf, sems,
                  *, n_p_blocks):
    i = pl.program_id(0)
    slot = i % 2
    n_rows = TOP_K * COMBINE_ROWS

    @pl.when(i == 0)
    def _():
        _gather_rows(y_tiles, dest_ref, n_rows, ybuf.at[0], sems.at[0], COMBINE_GATHER_QUEUES)

    _wait_rows(y_2d, n_rows, ybuf.at[slot], sems.at[slot])

    @pl.when(i + 1 < pl.num_programs(0))
    def _():
        _gather_rows(y_tiles, dest_next_ref, n_rows, ybuf.at[1 - slot], sems.at[1 - slot], COMBINE_GATHER_QUEUES)

    buf = ybuf.at[slot]
    moe = _load_token_tiles(buf, 0, COMBINE_ROWS) * rt_ref[:, 2 * TOP_K:2 * TOP_K + 1]
    for k in range(1, TOP_K):
        moe = moe + _load_token_tiles(buf, k * COMBINE_ROWS, COMBINE_ROWS) * rt_ref[:, 2 * TOP_K + k:2 * TOP_K + k + 1]
    res = _rms(x1_ref[...] + moe, g_ref[...])

    @pl.when(i < n_p_blocks)
    def _():
        op_ref[...] = res

    @pl.when(i >= n_p_blocks)
    def _():
        os_ref[...] = res


def _combine(dest_b, rt, y_2d, x1, g, n_p):
    n = x1.shape[0]
    n_blk = n // COMBINE_ROWS
    n_p_blocks = n_p // COMBINE_ROWS
    dest_blocks = dest_b.reshape(n_blk, 1, TOP_K * COMBINE_ROWS)
    return pl.pallas_call(
        functools.partial(_combine_body, n_p_blocks=n_p_blocks),
        grid=(n_blk,),
        in_specs=[
            pl.BlockSpec((None, 1, COMBINE_ROWS * TOP_K), lambda i: (i, 0, 0), memory_space=pltpu.SMEM),
            pl.BlockSpec((None, 1, COMBINE_ROWS * TOP_K), lambda i: (jnp.minimum(i + 1, n_blk - 1), 0, 0),
                         memory_space=pltpu.SMEM),
            pl.BlockSpec((COMBINE_ROWS, LANE), lambda i: (i, 0)),
            pl.BlockSpec(memory_space=pl.ANY),
            pl.BlockSpec(memory_space=pl.ANY),
            pl.BlockSpec((COMBINE_ROWS, D_MODEL), lambda i: (i, 0)),
            pl.BlockSpec((1, D_MODEL), lambda i: (0, 0)),
        ],
        out_specs=_group_specs(COMBINE_ROWS, D_MODEL, n_p_blocks),
        out_shape=[jax.ShapeDtypeStruct((n_p, D_MODEL), F32), jax.ShapeDtypeStruct((n - n_p, D_MODEL), F32)],
        scratch_shapes=[pltpu.VMEM((2, TOP_K * COMBINE_ROWS * TOK_TILES, LANE), F32),
                        pltpu.SemaphoreType.DMA((2,))],
        compiler_params=pltpu.CompilerParams(dimension_semantics=("arbitrary",),
                                             vmem_limit_bytes=VMEM_LIMIT),
        name="combine",
    )(dest_blocks, dest_blocks, rt, y_2d.reshape(-1, TOK_TILES, LANE), y_2d, x1, g)


def _plan_body(rtt_ref, cnt_ref, dk_ref, db_ref, meta_ref, pst):
    i = pl.program_id(0)
    sh = _log2(EXPERT_ROWS)
    n_e = N_EXPERTS

    @pl.when(i == 0)
    def _():
        cnt = cnt_ref[...].astype(jnp.int32)
        padded = (((cnt + (EXPERT_ROWS - 1)) >> sh) << sh).astype(F32)
        e_r = lax.broadcasted_iota(jnp.int32, (n_e, n_e), 0)
        e_c = lax.broadcasted_iota(jnp.int32, (n_e, n_e), 1)
        p_t = jnp.broadcast_to(padded, (n_e, n_e)).T
        pend = jnp.sum(jnp.where(e_c <= e_r, p_t, 0.0), axis=1, keepdims=True)
        pst[...] = pend - padded
        has_rows = p_t > 0.0
        group = jnp.sum(jnp.where((e_c <= e_r) & has_rows, 1.0, 0.0), axis=1, keepdims=True) - 1.0
        nxt = jnp.min(jnp.where((e_c > e_r) & has_rows, e_c, n_e), axis=1, keepdims=True)
        nxt = jnp.where(nxt >= n_e, -1, nxt)

        mb = meta_ref.shape[1]
        blk = lax.broadcasted_iota(jnp.int32, (n_e, mb), 1)
        eb = lax.broadcasted_iota(jnp.int32, (n_e, mb), 0)
        first_row = (blk * EXPERT_ROWS).astype(F32)

        def expert_of(row0):
            return jnp.minimum(jnp.sum(jnp.where(pend <= row0, 1, 0), axis=0, keepdims=True), n_e - 1)

        be = expert_of(first_row)
        be_prev = expert_of(first_row - EXPERT_ROWS)
        hit = eb == be
        wslot = jnp.sum(jnp.where(hit, group, 0.0), axis=0, keepdims=True).astype(jnp.int32) & 1
        nx = jnp.sum(jnp.where(hit, nxt, 0), axis=0, keepdims=True)
        n_used = pend[n_e - 1:n_e, :].astype(jnp.int32) >> sh
        lane = lax.broadcasted_iota(jnp.int32, (1, mb), 1)
        first = (((be != be_prev) | (lane == 0)) & (lane < n_used)).astype(jnp.int32)
        row8 = lax.broadcasted_iota(jnp.int32, (SUBLANE, mb), 0)
        meta = jnp.where(row8 == 0, be, jnp.where(row8 == 1, first, jnp.where(row8 == 2, wslot,
                         jnp.where(row8 == 3, nx, n_used))))
        meta_ref[...] = meta

    tm = rtt_ref.shape[1]
    eid = lax.broadcasted_iota(jnp.int32, (n_e, tm), 0).astype(F32)
    row8 = lax.broadcasted_iota(jnp.int32, (SUBLANE, tm), 0)
    d8 = jnp.zeros((SUBLANE, tm), jnp.int32)
    for k in range(TOP_K):
        start = jnp.sum(jnp.where(eid == rtt_ref[k:k + 1, :], pst[...], 0.0), axis=0, keepdims=True)
        d8 = jnp.where(row8 == k, (start + rtt_ref[TOP_K + k:TOP_K + k + 1, :]).astype(jnp.int32), d8)
    dk_ref[...] = d8[:TOP_K]
    for b in range(tm // COMBINE_ROWS):
        db_ref[b] = d8[:TOP_K, b * COMBINE_ROWS:(b + 1) * COMBINE_ROWS]


def _plan(rtt, cnt):
    n = rtt.shape[1]
    n_rows = n * TOP_K + N_EXPERTS * EXPERT_ROWS
    n_blocks = n_rows // EXPERT_ROWS
    mb = -(-n_blocks // LANE) * LANE
    dk, db, meta = pl.pallas_call(
        _plan_body,
        grid=(n // ROW_TILE,),
        in_specs=[pl.BlockSpec((2 * TOP_K, ROW_TILE), lambda i: (0, i)),
                  pl.BlockSpec((N_EXPERTS, 1), lambda i: (0, 0))],
        out_specs=[pl.BlockSpec((TOP_K, ROW_TILE), lambda i: (0, i)),
                   pl.BlockSpec((ROW_TILE // COMBINE_ROWS, TOP_K, COMBINE_ROWS), lambda i: (i, 0, 0)),
                   pl.BlockSpec((SUBLANE, mb), lambda i: (0, 0))],
        out_shape=[jax.ShapeDtypeStruct((TOP_K, n), jnp.int32),
                   jax.ShapeDtypeStruct((n // COMBINE_ROWS, TOP_K, COMBINE_ROWS), jnp.int32),
                   jax.ShapeDtypeStruct((SUBLANE, mb), jnp.int32)],
        scratch_shapes=[pltpu.VMEM((N_EXPERTS, 1), F32)],
        compiler_params=pltpu.CompilerParams(dimension_semantics=("arbitrary",)),
        name="plan",
    )(rtt, cnt)
    block_meta = (meta[0, :n_blocks], meta[4, 0:1], meta[1, :n_blocks], meta[2, :n_blocks], meta[3, :n_blocks])
    return dk, db, n_rows, block_meta


def _pad_lanes(v, width):
    return jnp.zeros((1, width), F32).at[0, :v.shape[0]].set(v.astype(F32))


def kernel(x_prompt, x_sample, state_gdn_conv, state_gdn, state_gla, rms_mix_w, w_in, conv_w, gdn_a_log,
           gdn_dt_bias, gdn_norm_w, gla_gk_w, gla_gk_b, gla_norm_w, w_out, rms_ffn_w, w_router, b_router,
           w_up, b_up, w_down, b_down, rms_final_w):
    bp, tp, d = x_prompt.shape
    bs, ts, _ = x_sample.shape
    n_p, n_s = bp * tp, bs * ts
    assert d == D_MODEL and state_gdn.shape[0] == 1, "single-layer kernel"
    assert tp >= CONV_WIDTH - 1 and ts >= CONV_WIDTH - 1, "new conv state is taken from the new tokens only"
    l = 0

    wi = w_in[l]
    a0 = GDN_CONV_CH + GDN_V_W
    g0 = a0 + 2 * GDN_HEADS
    lr0 = g0 + 2 * GLA_QK_W + 2 * GLA_V_W
    small = jnp.concatenate([wi[:, a0:a0 + 2 * GDN_HEADS], wi[:, lr0:lr0 + GLA_GATE_RANK],
                             jnp.zeros((d, SM_W - 2 * GDN_HEADS - GLA_GATE_RANK), F32)], axis=1)
    w_big = jnp.concatenate([wi[:, :a0], wi[:, g0:lr0], small], axis=1).astype(BF16)
    alog = _pad_lanes(gdn_a_log[l], SM_W)
    dtb = _pad_lanes(gdn_dt_bias[l], SM_W)
    wgk = jnp.zeros((SM_W, GLA_QK_W), F32).at[SM_LR:SM_LR + GLA_GATE_RANK].set(gla_gk_w[l])
    wr = jnp.zeros((d, LANE), F32).at[:, :N_EXPERTS].set(w_router[l])
    br = jnp.full((1, LANE), -1e30, F32).at[0, :N_EXPERTS].set(b_router[l])

    assert n_p % ROW_TILE == 0 and n_s % ROW_TILE == 0
    x_p, x_s = x_prompt.reshape(n_p, d), x_sample.reshape(n_s, d)
    proj = _inproj(x_p, x_s, rms_mix_w[l][None, :], w_big)

    tb_p = PROMPT_TIME_BLOCK
    zeros_conv = jnp.zeros((bp, CONV_WIDTH - 1, GDN_CONV_CH), F32)
    og_p, gdn_p, conv_p = _gdn(proj, bp, 1, tp, tb_p, CHUNK, CHUNK, zeros_conv,
                               jnp.zeros((bp, GDN_HEADS, GDN_DK, GDN_DV), F32), conv_w[l], alog, dtb,
                               gdn_norm_w[l][None, :])
    ol_p, gla_p = _gla(proj, bp, 1, tp, tb_p, CHUNK, CHUNK, jnp.zeros((bp, GLA_HEADS, GLA_DK, GLA_DV), F32),
                       wgk, gla_gk_b[l][None, :], gla_norm_w[l][None, :])

    ts_pad = SUBLANE
    nb_s = SAMPLE_SEQS_PER_STEP
    proj_s = proj[n_p:].reshape(bs, ts, PROJ_W)
    proj_sp = jnp.pad(proj_s, ((0, 0), (0, ts_pad - ts), (0, 0))).reshape(bs * ts_pad, PROJ_W)
    og_s, gdn_s, conv_s = _gdn(proj_sp, bs, nb_s, ts_pad, ts_pad, ts_pad, ts, state_gdn_conv[l], state_gdn[l],
                               conv_w[l], alog, dtb, gdn_norm_w[l][None, :])
    ol_s, gla_s = _gla(proj_sp, bs, nb_s, ts_pad, ts_pad, ts_pad, ts, state_gla[l], wgk, gla_gk_b[l][None, :],
                       gla_norm_w[l][None, :])
    og_s = og_s.reshape(bs, ts_pad, GDN_V_W)[:, :ts].reshape(n_s, GDN_V_W)
    ol_s = ol_s.reshape(bs, ts_pad, GLA_V_W)[:, :ts].reshape(n_s, GLA_V_W)

    x1, h2, rt, rtt, cnt = _outproj(og_p, og_s, ol_p, ol_s, x_p, x_s, w_out[l].astype(BF16),
                                    rms_ffn_w[l][None, :], wr, br)

    dest_k, dest_b, n_rows, block_meta = _plan(rtt, cnt)
    xs = _dispatch(h2.reshape(-1, TOK_TILES, LANE), dest_k.reshape(-1), n_rows)
    y_rows = _experts(block_meta, xs.reshape(-1, LANE), w_up[l], b_up[l], w_down[l], b_down[l])
    y_p, y_s = _combine(dest_b, rt, y_rows, x1, rms_final_w[None, :], n_p)
    y_prompt = y_p.reshape(bp, tp, d)
    y_sample = y_s.reshape(bs, ts, d)
    return (y_prompt, y_sample, conv_p[None], gdn_p[None], gla_p[None], conv_s[None], gdn_s[None], gla_s[None])
```

```python
import functools

import jax
import jax.numpy as jnp
from jax import lax
from jax.experimental import pallas as pl
from jax.experimental.pallas import tpu as pltpu
from jax.experimental.pallas import tpu_sc as plsc

F32 = jnp.float32
BF16 = jnp.bfloat16
HI = lax.Precision.HIGHEST

D_MODEL = 1024
GDN_HEADS = 4
GDN_DK = 128
GDN_DV = 128
GLA_HEADS = 4
GLA_DK = 64
GLA_DV = 128
GLA_GATE_RANK = 16
GLA_GATE_NORMALIZER = 16.0
CONV_WIDTH = 4
CHUNK = 64
N_EXPERTS = 32
TOP_K = 4
D_FF = 1024
SWIGLU_LIMIT = 7.0
SWIGLU_ALPHA = 1.702
RMS_EPS = 1e-6
L2_EPS = 1e-6

GDN_QK_W = GDN_HEADS * GDN_DK
GDN_V_W = GDN_HEADS * GDN_DV
GDN_CONV_CH = 2 * GDN_QK_W + GDN_V_W
GLA_QK_W = GLA_HEADS * GLA_DK
GLA_V_W = GLA_HEADS * GLA_DV

COL_QKV = 0
COL_Z = 1536
COL_GQ = 2048
COL_GK = 2304
COL_GV = 2560
COL_GG = 3072
COL_SM = 3584
SM_W = 128
PROJ_W = COL_SM + SM_W
SM_A, SM_B, SM_LR = 0, 4, 8

LANE = 128
SUBLANE = 8
TOK_TILES = D_MODEL // LANE
PACK_TILES = TOK_TILES // 2
ROW_TILE = 256
EXPERT_ROWS = 512
EXPERT_WEIGHT_QUEUE = 1
COMBINE_ROWS = 128
DMA_ISSUE_UNROLL = 8
COMBINE_GATHER_QUEUES = (0, 1)
GDN_CHUNKS_PER_TRIP = 4
GLA_CHUNKS_PER_TRIP = 4
PROMPT_TIME_BLOCK = 512
SAMPLE_SEQS_PER_STEP = 8
VMEM_LIMIT = 56 * 1024 * 1024


def _dot(a, b):
    return jnp.dot(a.astype(BF16), b.astype(BF16), preferred_element_type=F32)


def _dot_nt(a, b):
    return lax.dot_general(a.astype(BF16), b.astype(BF16), (((1,), (1,)), ((), ())),
                           preferred_element_type=F32)


def _dot_tn(a, b):
    return lax.dot_general(a.astype(BF16), b.astype(BF16), (((0,), (0,)), ((), ())),
                           preferred_element_type=F32)


def _dot_hi(a, b):
    return jnp.dot(a, b, precision=HI, preferred_element_type=F32)


def _dot_3pass(a, b):
    a_hi = a.astype(BF16)
    b_hi = b.astype(BF16)
    a_lo = (a - a_hi.astype(F32)).astype(BF16)
    b_lo = (b - b_hi.astype(F32)).astype(BF16)

    def mm(x, y):
        return jnp.dot(x, y, preferred_element_type=F32)

    return (mm(a_lo, b_hi) + mm(a_hi, b_lo)) + mm(a_hi, b_hi)


def _rms(x, w):
    return x * lax.rsqrt(jnp.mean(x * x, axis=-1, keepdims=True) + RMS_EPS) * w


def _silu(x):
    return x * jax.nn.sigmoid(x)


def _group_specs(rows, width, n_p_blocks):
    return [pl.BlockSpec((rows, width), lambda i: (jnp.minimum(i, n_p_blocks - 1), 0)),
            pl.BlockSpec((rows, width), lambda i: (jnp.maximum(i - n_p_blocks, 0), 0))]


def _group_pick(i, n_p_blocks, p_ref, s_ref):
    return jnp.where(i < n_p_blocks, p_ref[...], s_ref[...])


def _inproj_body(xp_ref, xs_ref, g_ref, w_ref, o_ref, *, n_p_blocks):
    x = _group_pick(pl.program_id(0), n_p_blocks, xp_ref, xs_ref)
    h = _rms(x, g_ref[...])
    o_ref[...] = jnp.dot(h.astype(BF16), w_ref[...], preferred_element_type=F32)


def _inproj(x_p, x_s, g, w):
    n_p_blocks, n_s_blocks = x_p.shape[0] // ROW_TILE, x_s.shape[0] // ROW_TILE
    n = x_p.shape[0] + x_s.shape[0]
    return pl.pallas_call(
        functools.partial(_inproj_body, n_p_blocks=n_p_blocks),
        grid=(n_p_blocks + n_s_blocks,),
        in_specs=_group_specs(ROW_TILE, D_MODEL, n_p_blocks) + [
            pl.BlockSpec((1, D_MODEL), lambda i: (0, 0)),
            pl.BlockSpec((D_MODEL, PROJ_W), lambda i: (0, 0)),
        ],
        out_specs=pl.BlockSpec((ROW_TILE, PROJ_W), lambda i: (i, 0)),
        out_shape=jax.ShapeDtypeStruct((n, PROJ_W), F32),
        compiler_params=pltpu.CompilerParams(dimension_semantics=("arbitrary",),
                                             vmem_limit_bytes=VMEM_LIMIT),
        name="in_proj",
    )(x_p, x_s, g, w)


def _log2(n):
    assert n & (n - 1) == 0
    return n.bit_length() - 1


def _tri_inv_all(ms, c, ii, jj):
    eye = (ii == jj).astype(F32)
    base = min(c, 8)
    sh = _log2(base)
    blk = (ii >> sh) == (jj >> sh)
    ns = [jnp.where(blk, m, 0.0) for m in ms]
    xs = [eye - n for n in ns]
    ps = [_dot(n, n) for n in ns]
    ts = [_dot(jnp.concatenate([x, p], axis=0), p) for x, p in zip(xs, ps)]
    xs = [x + t[:c] for x, t in zip(xs, ts)]
    ps = [t[c:] for t in ts]
    xs = [x + _dot(x, p) for x, p in zip(xs, ps)]
    s = base
    while s < c:
        sh_s, sh_b = _log2(s), _log2(2 * s)
        off = ((ii >> sh_b) == (jj >> sh_b)) & ((ii >> sh_s) != (jj >> sh_s))
        ys = [_dot(x, jnp.where(off, m, 0.0)) for x, m in zip(xs, ms)]
        xs = [x - _dot(y, x) for x, y in zip(xs, ys)]
        s *= 2
    return xs


def _gated_norm(o, w, z):
    return o * lax.rsqrt(jnp.mean(o * o, axis=-1, keepdims=True) + RMS_EPS) * w * _silu(z)


def _chunk_rows(s, tb_rows, ci, c):
    r = s * tb_rows + ci * c
    if not isinstance(r, int):
        r = pl.multiple_of(r, c)
    return r


def _for_chunks(n_chunks, step):
    if n_chunks == 1:
        step(0, 0)
    else:
        lax.fori_loop(0, n_chunks, step, 0)


def _gdn_body(qkv_ref, z_ref, sm_ref, cbuf_ref, s0_ref, cw_ref, alog_ref, dtb_ref, nw_ref,
              o_ref, sout_ref, cout_ref, st, xc, act, gcs, us, wss, qgs, kds, aqs,
              *, nb, tb_rows, chunk, valid, n_tb):
    tb = pl.program_id(1)
    c = chunk
    n_heads = GDN_HEADS
    tail = CONV_WIDTH - 1
    pad = SUBLANE
    units = [(s, h) for s in range(nb) for h in range(n_heads)]

    @pl.when(tb == 0)
    def _():
        st[...] = s0_ref[...]
        for s in range(nb):
            xc[s, pad - tail:pad, :] = cbuf_ref[s]

    if n_tb > 1:
        @pl.when(tb > 0)
        def _():
            for s in range(nb):
                xc[s, pad - tail:pad, :] = xc[s, tb_rows + pad - tail:tb_rows + pad, :]

    for s in range(nb):
        xc[s, pad:pad + tb_rows, :] = qkv_ref[s * tb_rows:(s + 1) * tb_rows, :]

    slab = min(tb_rows, 64)
    for s in range(nb):
        for sl in range(tb_rows // slab):
            for cb in range(GDN_CONV_CH // 512):
                cs = slice(cb * 512, (cb + 1) * 512)
                lo = pad - tail + sl * slab
                acc = xc[s, lo:lo + slab, cs] * cw_ref[0:1, cs]
                for i in range(1, CONV_WIDTH):
                    acc = acc + xc[s, lo + i:lo + i + slab, cs] * cw_ref[i:i + 1, cs]
                act[s * tb_rows + sl * slab:s * tb_rows + (sl + 1) * slab, cs] = _silu(acc)

    ii = lax.broadcasted_iota(jnp.int32, (c, c), 0)
    jj = lax.broadcasted_iota(jnp.int32, (c, c), 1)
    lower = (ii >= jj)
    lower_f = lower.astype(F32)
    strict = (ii > jj)
    rowmask = None
    if valid < c:
        rowmask = lax.broadcasted_iota(jnp.int32, (c, 1), 0) < valid

    def hs(h, w):
        return slice(h * w, (h + 1) * w)

    n_chunks = tb_rows // c
    cpi = next(k for k in (GDN_CHUNKS_PER_TRIP, 2, 1) if n_chunks % k == 0)
    p1_units = [(g, h) for g in range(nb * cpi) for h in range(n_heads)]

    def phase1(ci, carry):
        rows, b_ts, gc_ts, gc_tts = [], [], [], []
        for g in range(nb * cpi):
            rr = pl.ds(_chunk_rows(g // cpi, tb_rows, ci * cpi + g % cpi, c), c)
            sm = sm_ref[rr, :]
            g_t = -jnp.exp(alog_ref[...]) * jax.nn.softplus(sm + dtb_ref[...])
            b_t = jax.nn.sigmoid(sm)
            if rowmask is not None:
                g_t = jnp.where(rowmask, g_t, 0.0)
                b_t = jnp.where(rowmask, b_t, 0.0)
            gc_t = _dot_hi(lower_f, g_t)
            gcs[rr, :] = gc_t
            rows.append(rr)
            b_ts.append(b_t)
            gc_ts.append(gc_t)
            gc_tts.append(gc_t.T)
        qn, kn, kb, vb = {}, {}, {}, {}
        for (s, h) in p1_units:
            q = act[rows[s], hs(h, GDN_DK)]
            k = act[rows[s], slice(GDN_QK_W + h * GDN_DK, GDN_QK_W + (h + 1) * GDN_DK)]
            v = act[rows[s], slice(2 * GDN_QK_W + h * GDN_DV, 2 * GDN_QK_W + (h + 1) * GDN_DV)]
            if rowmask is not None:
                q = jnp.where(rowmask, q, 0.0)
                k = jnp.where(rowmask, k, 0.0)
                v = jnp.where(rowmask, v, 0.0)
            qn[s, h] = q * lax.rsqrt(jnp.sum(q * q, axis=-1, keepdims=True) + L2_EPS) * (GDN_DK ** -0.5)
            kn[s, h] = k * lax.rsqrt(jnp.sum(k * k, axis=-1, keepdims=True) + L2_EPS)
            beta = b_ts[s][:, SM_B + h:SM_B + h + 1]
            kb[s, h] = kn[s, h] * beta
            vb[s, h] = v * beta
        s1 = {u: _dot_nt(jnp.concatenate([kb[u], qn[u]], axis=0), kn[u]) for u in p1_units}
        mm = []
        for (s, h) in p1_units:
            gcol = gc_ts[s][:, SM_A + h:SM_A + h + 1]
            grow = gc_tts[s][SM_A + h:SM_A + h + 1, :]
            dec = jnp.exp(jnp.where(lower, gcol - grow, -jnp.inf))
            mm.append(jnp.where(strict, s1[s, h][:c] * dec, 0.0))
            aqs[h, rows[s], :] = s1[s, h][c:] * dec
        tms = _tri_inv_all(mm, c, ii, jj)
        for (s, h), tm in zip(p1_units, tms):
            gcol = gc_ts[s][:, SM_A + h:SM_A + h + 1]
            eg = jnp.exp(gcol)
            uw = _dot(tm, jnp.concatenate([vb[s, h], kb[s, h] * eg], axis=1))
            us[rows[s], hs(h, GDN_DV)] = uw[:, :GDN_DV]
            wss[rows[s], hs(h, GDN_DV)] = uw[:, GDN_DV:]
            qgs[rows[s], hs(h, GDN_DK)] = qn[s, h] * eg
            kds[rows[s], hs(h, GDN_DK)] = kn[s, h] * jnp.exp(gcol[c - 1:c, :] - gcol)
        return carry

    def phase2(ci, carry):
        r0 = [_chunk_rows(s, tb_rows, ci, c) for s in range(nb)]
        rows = [pl.ds(r, c) for r in r0]
        ws = {(s, h): _dot(jnp.concatenate([wss[rows[s], hs(h, GDN_DV)], qgs[rows[s], hs(h, GDN_DK)]], axis=0),
                           st[s, h]) for (s, h) in units}
        v_new = {(s, h): us[rows[s], hs(h, GDN_DV)] - ws[s, h][:c] for (s, h) in units}
        o = {(s, h): ws[s, h][c:] + _dot(aqs[h, rows[s], :], v_new[s, h]) for (s, h) in units}
        upd = {(s, h): _dot_tn(kds[rows[s], hs(h, GDN_DK)], v_new[s, h]) for (s, h) in units}
        for (s, h) in units:
            g_last = gcs[pl.ds(r0[s] + c - 1, 1), SM_A + h:SM_A + h + 1]
            st[s, h] = st[s, h] * jnp.exp(g_last) + upd[s, h]
        for s in range(nb):
            o_ref[rows[s], :] = jnp.concatenate(
                [_gated_norm(o[s, h], nw_ref[...], z_ref[rows[s], hs(h, GDN_DV)]) for h in range(n_heads)], axis=1)
        return carry

    _for_chunks(n_chunks // cpi, phase1)
    _for_chunks(n_chunks, phase2)

    @pl.when(tb == n_tb - 1)
    def _():
        sout_ref[...] = st[...]
        last = tb_rows if valid == c else valid
        for s in range(nb):
            cout_ref[s] = xc[s, pad + last - tail:pad + last, :]


def _gdn(proj, n_seq, nb, t_len, tb_rows, chunk, valid, conv_buf, s0, conv_w, alog, dtb, nw):
    n_tb = t_len // tb_rows
    assert nb == 1 or n_tb == 1
    rows = nb * tb_rows

    def rowblk(b, t):
        return b * n_tb + t

    body = functools.partial(_gdn_body, nb=nb, tb_rows=tb_rows, chunk=chunk, valid=valid, n_tb=n_tb)
    return pl.pallas_call(
        body,
        grid=(n_seq // nb, n_tb),
        in_specs=[
            pl.BlockSpec((rows, GDN_CONV_CH), lambda b, t: (rowblk(b, t), COL_QKV // GDN_CONV_CH)),
            pl.BlockSpec((rows, GDN_V_W), lambda b, t: (rowblk(b, t), COL_Z // GDN_V_W)),
            pl.BlockSpec((rows, SM_W), lambda b, t: (rowblk(b, t), COL_SM // SM_W)),
            pl.BlockSpec((nb, CONV_WIDTH - 1, GDN_CONV_CH), lambda b, t: (b, 0, 0)),
            pl.BlockSpec((nb, GDN_HEADS, GDN_DK, GDN_DV), lambda b, t: (b, 0, 0, 0)),
            pl.BlockSpec((CONV_WIDTH, GDN_CONV_CH), lambda b, t: (0, 0)),
            pl.BlockSpec((1, SM_W), lambda b, t: (0, 0)),
            pl.BlockSpec((1, SM_W), lambda b, t: (0, 0)),
            pl.BlockSpec((1, GDN_DV), lambda b, t: (0, 0)),
        ],
        out_specs=[
            pl.BlockSpec((rows, GDN_V_W), lambda b, t: (rowblk(b, t), 0)),
            pl.BlockSpec((nb, GDN_HEADS, GDN_DK, GDN_DV), lambda b, t: (b, 0, 0, 0)),
            pl.BlockSpec((nb, CONV_WIDTH - 1, GDN_CONV_CH), lambda b, t: (b, 0, 0)),
        ],
        out_shape=[
            jax.ShapeDtypeStruct((n_seq * t_len, GDN_V_W), F32),
            jax.ShapeDtypeStruct((n_seq, GDN_HEADS, GDN_DK, GDN_DV), F32),
            jax.ShapeDtypeStruct((n_seq, CONV_WIDTH - 1, GDN_CONV_CH), F32),
        ],
        scratch_shapes=[
            pltpu.VMEM((nb, GDN_HEADS, GDN_DK, GDN_DV), F32),
            pltpu.VMEM((nb, tb_rows + SUBLANE, GDN_CONV_CH), F32),
            pltpu.VMEM((rows, GDN_CONV_CH), F32),
            pltpu.VMEM((rows, SM_W), F32),
            pltpu.VMEM((rows, GDN_V_W), F32),
            pltpu.VMEM((rows, GDN_V_W), F32),
            pltpu.VMEM((rows, GDN_QK_W), F32),
            pltpu.VMEM((rows, GDN_QK_W), F32),
            pltpu.VMEM((GDN_HEADS, rows, chunk), F32),
        ],
        compiler_params=pltpu.CompilerParams(dimension_semantics=("arbitrary", "arbitrary"),
                                             vmem_limit_bytes=VMEM_LIMIT),
        name="gdn_mixer",
    )(proj, proj, proj, conv_buf, s0, conv_w, alog, dtb, nw)


def _gla_body(q_ref, k_ref, v_ref, go_ref, sm_ref, s0_ref, wgk_ref, bgk_ref, nw_ref,
              o_ref, sout_ref, st, qes, ois, upds, decs, *, nb, tb_rows, chunk, valid, n_tb):
    tb = pl.program_id(1)
    c = chunk
    n_heads = GLA_HEADS
    units = [(s, h) for s in range(nb) for h in range(n_heads)]

    @pl.when(tb == 0)
    def _():
        st[...] = s0_ref[...]

    ii = lax.broadcasted_iota(jnp.int32, (c, c), 0)
    jj = lax.broadcasted_iota(jnp.int32, (c, c), 1)
    lower = (ii >= jj)
    lower_f = lower.astype(F32)
    rid = lax.broadcasted_iota(jnp.int32, (c, 1), 0)
    rowmask = (rid < valid) if valid < c else None
    n_sub = max(c // 16, 1)
    sub = c // n_sub

    n_chunks = tb_rows // c
    cpi = next(k for k in (GLA_CHUNKS_PER_TRIP, 2, 1) if n_chunks % k == 0)
    p1_units = [(g, h) for g in range(nb * cpi) for h in range(n_heads)]

    def phase1(ci, carry):
        rows, slots, bcs, bc_ts = [], [], [], []
        for g in range(nb * cpi):
            chunk_idx = ci * cpi + g % cpi
            rr = pl.ds(_chunk_rows(g // cpi, tb_rows, chunk_idx, c), c)
            slots.append((g // cpi) * n_chunks + chunk_idx)
            gk = jax.nn.log_sigmoid(_dot(sm_ref[rr, :], wgk_ref[...]) + bgk_ref[...]) / GLA_GATE_NORMALIZER
            if rowmask is not None:
                gk = jnp.where(rowmask, gk, 0.0)
            bc = _dot_hi(lower_f, gk)
            rows.append(rr)
            bcs.append(bc)
            bc_ts.append(bc.T)
        q, k, v, bch = {}, {}, {}, {}
        for (s, h) in p1_units:
            ks = slice(h * GLA_DK, (h + 1) * GLA_DK)
            vs = slice(h * GLA_DV, (h + 1) * GLA_DV)
            q[s, h] = q_ref[rows[s], ks] * (GLA_DK ** -0.5)
            kk = k_ref[rows[s], ks]
            vv = v_ref[rows[s], vs]
            if rowmask is not None:
                kk = jnp.where(rowmask, kk, 0.0)
                vv = jnp.where(rowmask, vv, 0.0)
            k[s, h], v[s, h] = kk, vv
            bch[s, h] = bcs[s][:, ks]
        for (g, h) in p1_units:
            qes[h, rows[g], :] = q[g, h] * jnp.exp(bch[g, h])
        a = {}
        for u in p1_units:
            q_parts, k_parts = [], []
            for sb in range(n_sub):
                ref_row = bch[u][sb * sub:sb * sub + 1, :]
                in_blk = (rid >= sb * sub) & (rid < (sb + 1) * sub)
                q_parts.append(jnp.where(in_blk, q[u] * jnp.exp(jnp.where(in_blk, bch[u] - ref_row, 0.0)), 0.0))
                k_parts.append(k[u] * jnp.exp(jnp.where(rid < (sb + 1) * sub, ref_row - bch[u], 0.0)))
            q_hat = jnp.concatenate(q_parts, axis=1) if n_sub > 1 else q_parts[0]
            k_hat = jnp.concatenate(k_parts, axis=1) if n_sub > 1 else k_parts[0]
            a[u] = jnp.where(lower, _dot_nt(q_hat, k_hat), 0.0)
        upd = {u: _dot_tn(k[u] * jnp.exp(bch[u][c - 1:c, :] - bch[u]), v[u]) for u in p1_units}
        o_intra = {u: _dot(a[u], v[u]) for u in p1_units}
        for (g, h) in p1_units:
            dec_col = bc_ts[g][h * GLA_DK:(h + 1) * GLA_DK, c - 1:c]
            decs[slots[g], h] = jnp.broadcast_to(jnp.exp(dec_col), (GLA_DK, GLA_DV))
            upds[slots[g], h] = upd[g, h]
            ois[rows[g], h * GLA_DV:(h + 1) * GLA_DV] = o_intra[g, h]
        return carry

    def phase2(ci, carry):
        rows = [pl.ds(_chunk_rows(s, tb_rows, ci, c), c) for s in range(nb)]
        o = {(s, h): ois[rows[s], h * GLA_DV:(h + 1) * GLA_DV] + _dot(qes[h, rows[s], :], st[s, h])
             for (s, h) in units}
        for (s, h) in units:
            st[s, h] = decs[s * n_chunks + ci, h] * st[s, h] + upds[s * n_chunks + ci, h]
        for s in range(nb):
            o_ref[rows[s], :] = jnp.concatenate(
                [_gated_norm(o[s, h], nw_ref[...], go_ref[rows[s], h * GLA_DV:(h + 1) * GLA_DV])
                 for h in range(n_heads)], axis=1)
        return carry

    _for_chunks(n_chunks // cpi, phase1)
    _for_chunks(n_chunks, phase2)

    @pl.when(tb == n_tb - 1)
    def _():
        sout_ref[...] = st[...]


def _gla(proj, n_seq, nb, t_len, tb_rows, chunk, valid, s0, wgk, bgk, nw):
    n_tb = t_len // tb_rows
    assert nb == 1 or n_tb == 1
    rows = nb * tb_rows

    def rowblk(b, t):
        return b * n_tb + t

    body = functools.partial(_gla_body, nb=nb, tb_rows=tb_rows, chunk=chunk, valid=valid, n_tb=n_tb)
    return pl.pallas_call(
        body,
        grid=(n_seq // nb, n_tb),
        in_specs=[
            pl.BlockSpec((rows, GLA_QK_W), lambda b, t: (rowblk(b, t), COL_GQ // GLA_QK_W)),
            pl.BlockSpec((rows, GLA_QK_W), lambda b, t: (rowblk(b, t), COL_GK // GLA_QK_W)),
            pl.BlockSpec((rows, GLA_V_W), lambda b, t: (rowblk(b, t), COL_GV // GLA_V_W)),
            pl.BlockSpec((rows, GLA_V_W), lambda b, t: (rowblk(b, t), COL_GG // GLA_V_W)),
            pl.BlockSpec((rows, SM_W), lambda b, t: (rowblk(b, t), COL_SM // SM_W)),
            pl.BlockSpec((nb, GLA_HEADS, GLA_DK, GLA_DV), lambda b, t: (b, 0, 0, 0)),
            pl.BlockSpec((SM_W, GLA_QK_W), lambda b, t: (0, 0)),
            pl.BlockSpec((1, GLA_QK_W), lambda b, t: (0, 0)),
            pl.BlockSpec((1, GLA_DV), lambda b, t: (0, 0)),
        ],
        out_specs=[
            pl.BlockSpec((rows, GLA_V_W), lambda b, t: (rowblk(b, t), 0)),
            pl.BlockSpec((nb, GLA_HEADS, GLA_DK, GLA_DV), lambda b, t: (b, 0, 0, 0)),
        ],
        out_shape=[
            jax.ShapeDtypeStruct((n_seq * t_len, GLA_V_W), F32),
            jax.ShapeDtypeStruct((n_seq, GLA_HEADS, GLA_DK, GLA_DV), F32),
        ],
        scratch_shapes=[
            pltpu.VMEM((nb, GLA_HEADS, GLA_DK, GLA_DV), F32),
            pltpu.VMEM((GLA_HEADS, rows, GLA_DK), F32),
            pltpu.VMEM((rows, GLA_V_W), F32),
            pltpu.VMEM((rows // chunk, GLA_HEADS, GLA_DK, GLA_DV), F32),
            pltpu.VMEM((rows // chunk, GLA_HEADS, GLA_DK, GLA_DV), F32),
        ],
        compiler_params=pltpu.CompilerParams(dimension_semantics=("arbitrary", "arbitrary"),
                                             vmem_limit_bytes=VMEM_LIMIT),
        name="gla_mixer",
    )(proj, proj, proj, proj, proj, s0, wgk, bgk, nw)


def _outproj_body(ogp_ref, ogs_ref, olp_ref, ols_ref, xp_ref, xs_ref, wo_ref, g_ref, wr_ref, br_ref,
                  x1_ref, h2_ref, rt_ref, rtt_ref, cnt_ref, base, *, n_p_blocks):
    i = pl.program_id(0)

    @pl.when(i == 0)
    def _():
        base[...] = jnp.zeros_like(base)

    o = jnp.concatenate([_group_pick(i, n_p_blocks, ogp_ref, ogs_ref),
                         _group_pick(i, n_p_blocks, olp_ref, ols_ref)], axis=1)
    x1 = _group_pick(i, n_p_blocks, xp_ref, xs_ref) + jnp.dot(o.astype(BF16), wo_ref[...],
                                                               preferred_element_type=F32)
    x1_ref[...] = x1
    h = _rms(x1, g_ref[...])
    _store_token_tiles(h2_ref, _pack_bf16_pairs(h))
    logits = _dot_3pass(h, wr_ref[...]) + br_ref[...]

    tm = logits.shape[0]
    lt = logits.T[:N_EXPERTS]
    eid = lax.broadcasted_iota(jnp.int32, (N_EXPERTS, tm), 0)
    work = lt
    sel = jnp.zeros((N_EXPERTS, tm), F32)
    hits, ids, vals = [], [], []
    for _ in range(TOP_K):
        m = jnp.max(work, axis=0, keepdims=True)
        idx = jnp.min(jnp.where(work == m, eid, N_EXPERTS), axis=0, keepdims=True)
        hit = eid == idx
        hits.append(hit)
        ids.append(idx)
        vals.append(m)
        work = jnp.where(hit, -jnp.inf, work)
        sel = sel + hit.astype(F32)
    exps = [jnp.exp(v - vals[0]) for v in vals]
    den = exps[0]
    for e in exps[1:]:
        den = den + e
    gates = [e / den for e in exps]

    ri = lax.broadcasted_iota(jnp.int32, (tm, tm), 0)
    ci = lax.broadcasted_iota(jnp.int32, (tm, tm), 1)
    before = _dot(sel, (ri < ci).astype(F32)) + base[...]
    ranks = [jnp.sum(jnp.where(hit, before, 0.0), axis=0, keepdims=True) for hit in hits]
    base[...] = base[...] + jnp.sum(sel, axis=1, keepdims=True)
    cnt_ref[...] = base[...]

    row = lax.broadcasted_iota(jnp.int32, (LANE, tm), 0)
    rec = jnp.zeros((LANE, tm), F32)
    for k in range(TOP_K):
        rec = jnp.where(row == k, ids[k].astype(F32), rec)
        rec = jnp.where(row == TOP_K + k, ranks[k], rec)
        rec = jnp.where(row == 2 * TOP_K + k, gates[k], rec)
    rt_ref[...] = rec.T
    rtt_ref[...] = rec[:2 * TOP_K]


def _outproj(og_p, og_s, ol_p, ol_s, x_p, x_s, wo, g, wr, br):
    n_p_blocks, n_s_blocks = x_p.shape[0] // ROW_TILE, x_s.shape[0] // ROW_TILE
    n = x_p.shape[0] + x_s.shape[0]
    return pl.pallas_call(
        functools.partial(_outproj_body, n_p_blocks=n_p_blocks),
        grid=(n_p_blocks + n_s_blocks,),
        in_specs=_group_specs(ROW_TILE, GDN_V_W, n_p_blocks) + _group_specs(ROW_TILE, GLA_V_W, n_p_blocks)
        + _group_specs(ROW_TILE, D_MODEL, n_p_blocks) + [
            pl.BlockSpec((D_MODEL, D_MODEL), lambda i: (0, 0)),
            pl.BlockSpec((1, D_MODEL), lambda i: (0, 0)),
            pl.BlockSpec((D_MODEL, LANE), lambda i: (0, 0)),
            pl.BlockSpec((1, LANE), lambda i: (0, 0)),
        ],
        out_specs=[
            pl.BlockSpec((ROW_TILE, D_MODEL), lambda i: (i, 0)),
            pl.BlockSpec((ROW_TILE * PACK_TILES, LANE), lambda i: (i, 0)),
            pl.BlockSpec((ROW_TILE, LANE), lambda i: (i, 0)),
            pl.BlockSpec((2 * TOP_K, ROW_TILE), lambda i: (0, i)),
            pl.BlockSpec((N_EXPERTS, 1), lambda i: (0, 0)),
        ],
        out_shape=[
            jax.ShapeDtypeStruct((n, D_MODEL), F32),
            jax.ShapeDtypeStruct((n * PACK_TILES, LANE), jnp.uint32),
            jax.ShapeDtypeStruct((n, LANE), F32),
            jax.ShapeDtypeStruct((2 * TOP_K, n), F32),
            jax.ShapeDtypeStruct((N_EXPERTS, 1), F32),
        ],
        scratch_shapes=[pltpu.VMEM((N_EXPERTS, 1), F32)],
        compiler_params=pltpu.CompilerParams(dimension_semantics=("arbitrary",),
                                             vmem_limit_bytes=VMEM_LIMIT),
        name="out_proj",
    )(og_p, og_s, ol_p, ol_s, x_p, x_s, wo, g, wr, br)


def _store_token_tiles(ref2d, val):
    rows, tiles = val.shape[0], val.shape[1] // LANE
    for c in range(tiles):
        ref2d[pl.ds(c, rows, stride=tiles), :] = val[:, c * LANE:(c + 1) * LANE]


def _load_token_tiles(ref2d, first_row, rows, tiles=TOK_TILES):
    return jnp.concatenate(
        [ref2d[pl.ds(first_row * tiles + c, rows, stride=tiles), :] for c in range(tiles)], axis=1)


def _pack_bf16_pairs(x):
    half = x.shape[1] // 2
    bits = lax.bitcast_convert_type(x.astype(BF16).astype(F32), jnp.uint32)
    return (bits[:, :half] >> 16) | (bits[:, half:] & jnp.uint32(0xFFFF0000))


def _unpack_bf16_pairs(w):
    lo = lax.bitcast_convert_type(w << 16, F32)
    hi = lax.bitcast_convert_type(w & jnp.uint32(0xFFFF0000), F32)
    return jnp.concatenate([lo, hi], axis=1).astype(BF16)


def _expert_weight_copies(e, ws, wup_hbm, wdn_hbm, wup_buf, wdn_buf, wsems):
    return (pltpu.make_async_copy(wup_hbm.at[e], wup_buf.at[ws], wsems.at[ws]),
            pltpu.make_async_copy(wdn_hbm.at[e], wdn_buf.at[ws], wsems.at[ws]))


def _expert_body(be_ref, nu_ref, first_ref, wslot_ref, next_ref, x_ref,
                 wup_hbm, bup_ref, wdn_hbm, bdn_ref, y_ref, wup_buf, wdn_buf, wsems):
    i = pl.program_id(0)
    n_used = nu_ref[0]
    ws = wslot_ref[i]
    weight_copies = functools.partial(_expert_weight_copies, wup_hbm=wup_hbm, wdn_hbm=wdn_hbm, wup_buf=wup_buf,
                                      wdn_buf=wdn_buf, wsems=wsems)

    @pl.when((i == 0) & (n_used > 0))
    def _():
        for cp in weight_copies(be_ref[0], ws):
            cp.start(priority=EXPERT_WEIGHT_QUEUE)

    @pl.when(i < n_used)
    def _():
        @pl.when(first_ref[i] == 1)
        def _():
            for cp in weight_copies(be_ref[i], ws):
                cp.wait()

            @pl.when(next_ref[i] >= 0)
            def _():
                for cp in weight_copies(next_ref[i], 1 - ws):
                    cp.start(priority=EXPERT_WEIGHT_QUEUE)

        x = _unpack_bf16_pairs(_load_token_tiles(x_ref, 0, EXPERT_ROWS, PACK_TILES))
        gu = _dot(x, wup_buf[ws]) + bup_ref[...]
        gate = jnp.minimum(gu[:, :D_FF], SWIGLU_LIMIT)
        up = jnp.clip(gu[:, D_FF:], -SWIGLU_LIMIT, SWIGLU_LIMIT)
        a = (up + 1.0) * gate * jax.nn.sigmoid(SWIGLU_ALPHA * gate)
        _store_token_tiles(y_ref, _dot(a, wdn_buf[ws]) + bdn_ref[...])

    @pl.when(i >= n_used)
    def _():
        y_ref[...] = jnp.zeros_like(y_ref)


def _experts(block_meta, xs_2d, w_up, b_up, w_down, b_down):
    n_blocks = block_meta[0].shape[0]
    grid_spec = pltpu.PrefetchScalarGridSpec(
        num_scalar_prefetch=len(block_meta),
        grid=(n_blocks,),
        in_specs=[
            pl.BlockSpec((EXPERT_ROWS * PACK_TILES, LANE), lambda i, *_: (i, 0)),
            pl.BlockSpec(memory_space=pl.ANY),
            pl.BlockSpec((None, 1, 2 * D_FF), lambda i, be, *_: (be[i], 0, 0)),
            pl.BlockSpec(memory_space=pl.ANY),
            pl.BlockSpec((None, 1, D_MODEL), lambda i, be, *_: (be[i], 0, 0)),
        ],
        out_specs=pl.BlockSpec((EXPERT_ROWS * TOK_TILES, LANE), lambda i, *_: (i, 0)),
        scratch_shapes=[
            pltpu.VMEM((2, D_MODEL, 2 * D_FF), F32),
            pltpu.VMEM((2, D_FF, D_MODEL), F32),
            pltpu.SemaphoreType.DMA((2,)),
        ],
    )
    return pl.pallas_call(
        _expert_body,
        grid_spec=grid_spec,
        out_shape=jax.ShapeDtypeStruct((n_blocks * EXPERT_ROWS * TOK_TILES, LANE), F32),
        compiler_params=pltpu.CompilerParams(dimension_semantics=("arbitrary",),
                                             vmem_limit_bytes=VMEM_LIMIT),
        name="experts",
    )(*block_meta, xs_2d, w_up, b_up.reshape(N_EXPERTS, 1, 2 * D_FF), w_down,
      b_down.reshape(N_EXPERTS, 1, D_MODEL))


def _dispatch(h_tiles, dest_kmajor, n_rows):
    n_tok = h_tiles.shape[0]
    info = plsc.get_sparse_core_info()
    n_workers = info.num_cores * info.num_subcores
    per_worker = n_tok // n_workers
    chunk = next(c for c in (128, 96, 88, 64, 48, 32, 16, 8) if per_worker % c == 0)
    assert n_tok % n_workers == 0 and per_worker % SUBLANE == 0
    mesh = plsc.VectorSubcoreMesh(core_axis_name="c", subcore_axis_name="s")

    @functools.partial(
        pl.kernel, mesh=mesh,
        out_type=jax.ShapeDtypeStruct((n_rows,) + h_tiles.shape[1:], h_tiles.dtype),
        scratch_types=[pltpu.VMEM((TOP_K, chunk), jnp.int32), pltpu.VMEM((chunk,) + h_tiles.shape[1:], h_tiles.dtype),
                       pltpu.SemaphoreType.DMA],
    )
    def dispatch(h_hbm, dest_hbm, out_hbm, idx_v, rows_v, sem):
        wid = lax.axis_index("s") * info.num_cores + lax.axis_index("c")

        def step(j, carry):
            t0 = pl.multiple_of(wid * per_worker + j * chunk, SUBLANE)
            loads = [pltpu.async_copy(h_hbm.at[pl.ds(t0, chunk)], rows_v, sem)]
            for k in range(TOP_K):
                loads.append(pltpu.async_copy(
                    dest_hbm.at[pl.ds(pl.multiple_of(k * n_tok + t0, SUBLANE), chunk)], idx_v.at[k], sem))
            for cp in loads:
                cp.wait()
            stores = [pltpu.async_copy(rows_v, out_hbm.at[idx_v.at[k]], sem) for k in range(TOP_K)]
            for cp in stores:
                cp.wait()
            return carry

        lax.fori_loop(0, per_worker // chunk, step, 0)

    return dispatch(h_tiles, dest_kmajor)


def _gather_rows(src_tiles, idx_ref, n_rows, dst2d, sem, priorities):
    def issue(j, carry):
        for u in range(DMA_ISSUE_UNROLL):
            r = j * DMA_ISSUE_UNROLL + u
            dst = dst2d.at[pl.ds(pl.multiple_of(r * TOK_TILES, TOK_TILES), TOK_TILES), :]
            pltpu.make_async_copy(src_tiles.at[idx_ref[0, r]], dst, sem).start(
                priority=priorities[u % len(priorities)])
        return carry

    lax.fori_loop(0, n_rows // DMA_ISSUE_UNROLL, issue, 0)


def _wait_rows(src2d, n_rows, dst2d, sem):
    pltpu.make_async_copy(src2d.at[pl.ds(0, n_rows * TOK_TILES), :], dst2d, sem).wait()


def _combine_body(dest_ref, dest_next_ref, rt_ref, y_tiles, y_2d, x1_ref, g_ref, op_ref, os_ref, ybuf, sems,
                  *, n_p_blocks):
    i = pl.program_id(0)
    slot = i % 2
    n_rows = TOP_K * COMBINE_ROWS

    @pl.when(i == 0)
    def _():
        _gather_rows(y_tiles, dest_ref, n_rows, ybuf.at[0], sems.at[0], COMBINE_GATHER_QUEUES)

    _wait_rows(y_2d, n_rows, ybuf.at[slot], sems.at[slot])

    @pl.when(i + 1 < pl.num_programs(0))
    def _():
        _gather_rows(y_tiles, dest_next_ref, n_rows, ybuf.at[1 - slot], sems.at[1 - slot], COMBINE_GATHER_QUEUES)

    buf = ybuf.at[slot]
    moe = _load_token_tiles(buf, 0, COMBINE_ROWS) * rt_ref[:, 2 * TOP_K:2 * TOP_K + 1]
    for k in range(1, TOP_K):
        moe = moe + _load_token_tiles(buf, k * COMBINE_ROWS, COMBINE_ROWS) * rt_ref[:, 2 * TOP_K + k:2 * TOP_K + k + 1]
    res = _rms(x1_ref[...] + moe, g_ref[...])

    @pl.when(i < n_p_blocks)
    def _():
        op_ref[...] = res

    @pl.when(i >= n_p_blocks)
    def _():
        os_ref[...] = res


def _combine(dest_b, rt, y_2d, x1, g, n_p):
    n = x1.shape[0]
    n_blk = n // COMBINE_ROWS
    n_p_blocks = n_p // COMBINE_ROWS
    dest_blocks = dest_b.reshape(n_blk, 1, TOP_K * COMBINE_ROWS)
    return pl.pallas_call(
        functools.partial(_combine_body, n_p_blocks=n_p_blocks),
        grid=(n_blk,),
        in_specs=[
            pl.BlockSpec((None, 1, COMBINE_ROWS * TOP_K), lambda i: (i, 0, 0), memory_space=pltpu.SMEM),
            pl.BlockSpec((None, 1, COMBINE_ROWS * TOP_K), lambda i: (jnp.minimum(i + 1, n_blk - 1), 0, 0),
                         memory_space=pltpu.SMEM),
            pl.BlockSpec((COMBINE_ROWS, LANE), lambda i: (i, 0)),
            pl.BlockSpec(memory_space=pl.ANY),
            pl.BlockSpec(memory_space=pl.ANY),
            pl.BlockSpec((COMBINE_ROWS, D_MODEL), lambda i: (i, 0)),
            pl.BlockSpec((1, D_MODEL), lambda i: (0, 0)),
        ],
        out_specs=_group_specs(COMBINE_ROWS, D_MODEL, n_p_blocks),
        out_shape=[jax.ShapeDtypeStruct((n_p, D_MODEL), F32), jax.ShapeDtypeStruct((n - n_p, D_MODEL), F32)],
        scratch_shapes=[pltpu.VMEM((2, TOP_K * COMBINE_ROWS * TOK_TILES, LANE), F32),
                        pltpu.SemaphoreType.DMA((2,))],
        compiler_params=pltpu.CompilerParams(dimension_semantics=("arbitrary",),
                                             vmem_limit_bytes=VMEM_LIMIT),
        name="combine",
    )(dest_blocks, dest_blocks, rt, y_2d.reshape(-1, TOK_TILES, LANE), y_2d, x1, g)


def _plan_body(rtt_ref, cnt_ref, dk_ref, db_ref, meta_ref, pst):
    i = pl.program_id(0)
    sh = _log2(EXPERT_ROWS)
    n_e = N_EXPERTS

    @pl.when(i == 0)
    def _():
        cnt = cnt_ref[...].astype(jnp.int32)
        padded = (((cnt + (EXPERT_ROWS - 1)) >> sh) << sh).astype(F32)
        e_r = lax.broadcasted_iota(jnp.int32, (n_e, n_e), 0)
        e_c = lax.broadcasted_iota(jnp.int32, (n_e, n_e), 1)
        p_t = jnp.broadcast_to(padded, (n_e, n_e)).T
        pend = jnp.sum(jnp.where(e_c <= e_r, p_t, 0.0), axis=1, keepdims=True)
        pst[...] = pend - padded
        has_rows = p_t > 0.0
        group = jnp.sum(jnp.where((e_c <= e_r) & has_rows, 1.0, 0.0), axis=1, keepdims=True) - 1.0
        nxt = jnp.min(jnp.where((e_c > e_r) & has_rows, e_c, n_e), axis=1, keepdims=True)
        nxt = jnp.where(nxt >= n_e, -1, nxt)

        mb = meta_ref.shape[1]
        blk = lax.broadcasted_iota(jnp.int32, (n_e, mb), 1)
        eb = lax.broadcasted_iota(jnp.int32, (n_e, mb), 0)
        first_row = (blk * EXPERT_ROWS).astype(F32)

        def expert_of(row0):
            return jnp.minimum(jnp.sum(jnp.where(pend <= row0, 1, 0), axis=0, keepdims=True), n_e - 1)

        be = expert_of(first_row)
        be_prev = expert_of(first_row - EXPERT_ROWS)
        hit = eb == be
        wslot = jnp.sum(jnp.where(hit, group, 0.0), axis=0, keepdims=True).astype(jnp.int32) & 1
        nx = jnp.sum(jnp.where(hit, nxt, 0), axis=0, keepdims=True)
        n_used = pend[n_e - 1:n_e, :].astype(jnp.int32) >> sh
        lane = lax.broadcasted_iota(jnp.int32, (1, mb), 1)
        first = (((be != be_prev) | (lane == 0)) & (lane < n_used)).astype(jnp.int32)
        row8 = lax.broadcasted_iota(jnp.int32, (SUBLANE, mb), 0)
        meta = jnp.where(row8 == 0, be, jnp.where(row8 == 1, first, jnp.where(row8 == 2, wslot,
                         jnp.where(row8 == 3, nx, n_used))))
        meta_ref[...] = meta

    tm = rtt_ref.shape[1]
    eid = lax.broadcasted_iota(jnp.int32, (n_e, tm), 0).astype(F32)
    row8 = lax.broadcasted_iota(jnp.int32, (SUBLANE, tm), 0)
    d8 = jnp.zeros((SUBLANE, tm), jnp.int32)
    for k in range(TOP_K):
        start = jnp.sum(jnp.where(eid == rtt_ref[k:k + 1, :], pst[...], 0.0), axis=0, keepdims=True)
        d8 = jnp.where(row8 == k, (start + rtt_ref[TOP_K + k:TOP_K + k + 1, :]).astype(jnp.int32), d8)
    dk_ref[...] = d8[:TOP_K]
    for b in range(tm // COMBINE_ROWS):
        db_ref[b] = d8[:TOP_K, b * COMBINE_ROWS:(b + 1) * COMBINE_ROWS]


def _plan(rtt, cnt):
    n = rtt.shape[1]
    n_rows = n * TOP_K + N_EXPERTS * EXPERT_ROWS
    n_blocks = n_rows // EXPERT_ROWS
    mb = -(-n_blocks // LANE) * LANE
    dk, db, meta = pl.pallas_call(
        _plan_body,
        grid=(n // ROW_TILE,),
        in_specs=[pl.BlockSpec((2 * TOP_K, ROW_TILE), lambda i: (0, i)),
                  pl.BlockSpec((N_EXPERTS, 1), lambda i: (0, 0))],
        out_specs=[pl.BlockSpec((TOP_K, ROW_TILE), lambda i: (0, i)),
                   pl.BlockSpec((ROW_TILE // COMBINE_ROWS, TOP_K, COMBINE_ROWS), lambda i: (i, 0, 0)),
                   pl.BlockSpec((SUBLANE, mb), lambda i: (0, 0))],
        out_shape=[jax.ShapeDtypeStruct((TOP_K, n), jnp.int32),
                   jax.ShapeDtypeStruct((n // COMBINE_ROWS, TOP_K, COMBINE_ROWS), jnp.int32),
                   jax.ShapeDtypeStruct((SUBLANE, mb), jnp.int32)],
        scratch_shapes=[pltpu.VMEM((N_EXPERTS, 1), F32)],
        compiler_params=pltpu.CompilerParams(dimension_semantics=("arbitrary",)),
        name="plan",
    )(rtt, cnt)
    block_meta = (meta[0, :n_blocks], meta[4, 0:1], meta[1, :n_blocks], meta[2, :n_blocks], meta[3, :n_blocks])
    return dk, db, n_rows, block_meta


def _pad_lanes(v, width):
    return jnp.zeros((1, width), F32).at[0, :v.shape[0]].set(v.astype(F32))


def kernel(x_prompt, x_sample, state_gdn_conv, state_gdn, state_gla, rms_mix_w, w_in, conv_w, gdn_a_log,
           gdn_dt_bias, gdn_norm_w, gla_gk_w, gla_gk_b, gla_norm_w, w_out, rms_ffn_w, w_router, b_router,
           w_up, b_up, w_down, b_down, rms_final_w):
    bp, tp, d = x_prompt.shape
    bs, ts, _ = x_sample.shape
    n_p, n_s = bp * tp, bs * ts
    assert d == D_MODEL and state_gdn.shape[0] == 1, "single-layer kernel"
    assert tp >= CONV_WIDTH - 1 and ts >= CONV_WIDTH - 1, "new conv state is taken from the new tokens only"
    l = 0

    wi = w_in[l]
    a0 = GDN_CONV_CH + GDN_V_W
    g0 = a0 + 2 * GDN_HEADS
    lr0 = g0 + 2 * GLA_QK_W + 2 * GLA_V_W
    small = jnp.concatenate([wi[:, a0:a0 + 2 * GDN_HEADS], wi[:, lr0:lr0 + GLA_GATE_RANK],
                             jnp.zeros((d, SM_W - 2 * GDN_HEADS - GLA_GATE_RANK), F32)], axis=1)
    w_big = jnp.concatenate([wi[:, :a0], wi[:, g0:lr0], small], axis=1).astype(BF16)
    alog = _pad_lanes(gdn_a_log[l], SM_W)
    dtb = _pad_lanes(gdn_dt_bias[l], SM_W)
    wgk = jnp.zeros((SM_W, GLA_QK_W), F32).at[SM_LR:SM_LR + GLA_GATE_RANK].set(gla_gk_w[l])
    wr = jnp.zeros((d, LANE), F32).at[:, :N_EXPERTS].set(w_router[l])
    br = jnp.full((1, LANE), -1e30, F32).at[0, :N_EXPERTS].set(b_router[l])

    assert n_p % ROW_TILE == 0 and n_s % ROW_TILE == 0
    x_p, x_s = x_prompt.reshape(n_p, d), x_sample.reshape(n_s, d)
    proj = _inproj(x_p, x_s, rms_mix_w[l][None, :], w_big)

    tb_p = PROMPT_TIME_BLOCK
    zeros_conv = jnp.zeros((bp, CONV_WIDTH - 1, GDN_CONV_CH), F32)
    og_p, gdn_p, conv_p = _gdn(proj, bp, 1, tp, tb_p, CHUNK, CHUNK, zeros_conv,
                               jnp.zeros((bp, GDN_HEADS, GDN_DK, GDN_DV), F32), conv_w[l], alog, dtb,
                               gdn_norm_w[l][None, :])
    ol_p, gla_p = _gla(proj, bp, 1, tp, tb_p, CHUNK, CHUNK, jnp.zeros((bp, GLA_HEADS, GLA_DK, GLA_DV), F32),
                       wgk, gla_gk_b[l][None, :], gla_norm_w[l][None, :])

    ts_pad = SUBLANE
    nb_s = SAMPLE_SEQS_PER_STEP
    proj_s = proj[n_p:].reshape(bs, ts, PROJ_W)
    proj_sp = jnp.pad(proj_s, ((0, 0), (0, ts_pad - ts), (0, 0))).reshape(bs * ts_pad, PROJ_W)
    og_s, gdn_s, conv_s = _gdn(proj_sp, bs, nb_s, ts_pad, ts_pad, ts_pad, ts, state_gdn_conv[l], state_gdn[l],
                               conv_w[l], alog, dtb, gdn_norm_w[l][None, :])
    ol_s, gla_s = _gla(proj_sp, bs, nb_s, ts_pad, ts_pad, ts_pad, ts, state_gla[l], wgk, gla_gk_b[l][None, :],
                       gla_norm_w[l][None, :])
    og_s = og_s.reshape(bs, ts_pad, GDN_V_W)[:, :ts].reshape(n_s, GDN_V_W)
    ol_s = ol_s.reshape(bs, ts_pad, GLA_V_W)[:, :ts].reshape(n_s, GLA_V_W)

    x1, h2, rt, rtt, cnt = _outproj(og_p, og_s, ol_p, ol_s, x_p, x_s, w_out[l].astype(BF16),
                                    rms_ffn_w[l][None, :], wr, br)

    dest_k, dest_b, n_rows, block_meta = _plan(rtt, cnt)
    xs = _dispatch(h2.reshape(-1, PACK_TILES, LANE), dest_k.reshape(-1), n_rows)
    y_rows = _experts(block_meta, xs.reshape(-1, LANE), w_up[l], b_up[l], w_down[l], b_down[l])
    y_p, y_s = _combine(dest_b, rt, y_rows, x1, rms_final_w[None, :], n_p)
    y_prompt = y_p.reshape(bp, tp, d)
    y_sample = y_s.reshape(bs, ts, d)
    return (y_prompt, y_sample, conv_p[None], gdn_p[None], gla_p[None], conv_s[None], gdn_s[None], gla_s[None])
```

```python
import functools

import jax
import jax.numpy as jnp
from jax import lax
from jax.experimental import pallas as pl
from jax.experimental.pallas import tpu as pltpu
from jax.experimental.pallas import tpu_sc as plsc

F32 = jnp.float32
BF16 = jnp.bfloat16
HI = lax.Precision.HIGHEST

D_MODEL = 1024
GDN_HEADS = 4
GDN_DK = 128
GDN_DV = 128
GLA_HEADS = 4
GLA_DK = 64
GLA_DV = 128
GLA_GATE_RANK = 16
GLA_GATE_NORMALIZER = 16.0
CONV_WIDTH = 4
CHUNK = 64
N_EXPERTS = 32
TOP_K = 4
D_FF = 1024
SWIGLU_LIMIT = 7.0
SWIGLU_ALPHA = 1.702
RMS_EPS = 1e-6
L2_EPS = 1e-6

GDN_QK_W = GDN_HEADS * GDN_DK
GDN_V_W = GDN_HEADS * GDN_DV
GDN_CONV_CH = 2 * GDN_QK_W + GDN_V_W
GLA_QK_W = GLA_HEADS * GLA_DK
GLA_V_W = GLA_HEADS * GLA_DV

COL_QKV = 0
COL_Z = 1536
COL_GQ = 2048
COL_GK = 2304
COL_GV = 2560
COL_GG = 3072
COL_SM = 3584
SM_W = 128
PROJ_W = COL_SM + SM_W
SM_A, SM_B, SM_LR = 0, 4, 8

LANE = 128
SUBLANE = 8
TOK_TILES = D_MODEL // LANE
PACK_TILES = TOK_TILES // 2
ROW_TILE = 256
EXPERT_ROWS = 512
EXPERT_WEIGHT_QUEUE = 1
WEIGHT_CAST_ROWS = 128
COMBINE_ROWS = 128
PLAN_TILE_MAX = 2048
DMA_ISSUE_UNROLL = 8
COMBINE_GATHER_QUEUES = (0, 1)
GDN_CHUNKS_PER_TRIP = 4
GLA_CHUNKS_PER_TRIP = 4
PROMPT_TIME_BLOCK = 512
SAMPLE_SEQS_PER_STEP = 8
VMEM_LIMIT = 56 * 1024 * 1024


def _dot(a, b):
    return jnp.dot(a.astype(BF16), b.astype(BF16), preferred_element_type=F32)


def _dot_nt(a, b):
    return lax.dot_general(a.astype(BF16), b.astype(BF16), (((1,), (1,)), ((), ())),
                           preferred_element_type=F32)


def _dot_tn(a, b):
    return lax.dot_general(a.astype(BF16), b.astype(BF16), (((0,), (0,)), ((), ())),
                           preferred_element_type=F32)


def _dot_hi(a, b):
    return jnp.dot(a, b, precision=HI, preferred_element_type=F32)


def _dot_3pass(a, b):
    a_hi = a.astype(BF16)
    b_hi = b.astype(BF16)
    a_lo = (a - a_hi.astype(F32)).astype(BF16)
    b_lo = (b - b_hi.astype(F32)).astype(BF16)

    def mm(x, y):
        return jnp.dot(x, y, preferred_element_type=F32)

    return (mm(a_lo, b_hi) + mm(a_hi, b_lo)) + mm(a_hi, b_hi)


def _rms(x, w):
    return x * lax.rsqrt(jnp.mean(x * x, axis=-1, keepdims=True) + RMS_EPS) * w


def _silu(x):
    return x * jax.nn.sigmoid(x)


def _group_specs(rows, width, n_p_blocks):
    return [pl.BlockSpec((rows, width), lambda i: (jnp.minimum(i, n_p_blocks - 1), 0)),
            pl.BlockSpec((rows, width), lambda i: (jnp.maximum(i - n_p_blocks, 0), 0))]


def _group_pick(i, n_p_blocks, p_ref, s_ref):
    return jnp.where(i < n_p_blocks, p_ref[...], s_ref[...])


def _inproj_body(xp_ref, xs_ref, g_ref, w_ref, o_ref, *, n_p_blocks):
    x = _group_pick(pl.program_id(0), n_p_blocks, xp_ref, xs_ref)
    h = _rms(x, g_ref[...])
    o_ref[...] = jnp.dot(h.astype(BF16), w_ref[...], preferred_element_type=F32)


def _inproj(x_p, x_s, g, w):
    n_p_blocks, n_s_blocks = x_p.shape[0] // ROW_TILE, x_s.shape[0] // ROW_TILE
    n = x_p.shape[0] + x_s.shape[0]
    return pl.pallas_call(
        functools.partial(_inproj_body, n_p_blocks=n_p_blocks),
        grid=(n_p_blocks + n_s_blocks,),
        in_specs=_group_specs(ROW_TILE, D_MODEL, n_p_blocks) + [
            pl.BlockSpec((1, D_MODEL), lambda i: (0, 0)),
            pl.BlockSpec((D_MODEL, PROJ_W), lambda i: (0, 0)),
        ],
        out_specs=pl.BlockSpec((ROW_TILE, PROJ_W), lambda i: (i, 0)),
        out_shape=jax.ShapeDtypeStruct((n, PROJ_W), F32),
        compiler_params=pltpu.CompilerParams(dimension_semantics=("arbitrary",),
                                             vmem_limit_bytes=VMEM_LIMIT),
        name="in_proj",
    )(x_p, x_s, g, w)


def _log2(n):
    assert n & (n - 1) == 0
    return n.bit_length() - 1


def _tri_inv_all(ms, c, ii, jj):
    eye = (ii == jj).astype(F32)
    base = min(c, 8)
    sh = _log2(base)
    blk = (ii >> sh) == (jj >> sh)
    ns = [jnp.where(blk, m, 0.0) for m in ms]
    xs = [eye - n for n in ns]
    ps = [_dot(n, n) for n in ns]
    ts = [_dot(jnp.concatenate([x, p], axis=0), p) for x, p in zip(xs, ps)]
    xs = [x + t[:c] for x, t in zip(xs, ts)]
    ps = [t[c:] for t in ts]
    xs = [x + _dot(x, p) for x, p in zip(xs, ps)]
    s = base
    while s < c:
        sh_s, sh_b = _log2(s), _log2(2 * s)
        off = ((ii >> sh_b) == (jj >> sh_b)) & ((ii >> sh_s) != (jj >> sh_s))
        ys = [_dot(x, jnp.where(off, m, 0.0)) for x, m in zip(xs, ms)]
        xs = [x - _dot(y, x) for x, y in zip(xs, ys)]
        s *= 2
    return xs


def _gated_norm(o, w, z):
    return o * lax.rsqrt(jnp.mean(o * o, axis=-1, keepdims=True) + RMS_EPS) * w * _silu(z)


def _chunk_rows(s, tb_rows, ci, c):
    r = s * tb_rows + ci * c
    if not isinstance(r, int):
        r = pl.multiple_of(r, c)
    return r


def _for_chunks(n_chunks, step):
    if n_chunks == 1:
        step(0, 0)
    else:
        lax.fori_loop(0, n_chunks, step, 0)


def _gdn_body(qkv_ref, z_ref, sm_ref, cbuf_ref, s0_ref, cw_ref, alog_ref, dtb_ref, nw_ref,
              o_ref, sout_ref, cout_ref, st, xc, act, gcs, us, wss, qgs, kds, aqs,
              *, nb, tb_rows, chunk, valid, n_tb):
    tb = pl.program_id(1)
    c = chunk
    n_heads = GDN_HEADS
    tail = CONV_WIDTH - 1
    pad = SUBLANE
    units = [(s, h) for s in range(nb) for h in range(n_heads)]

    @pl.when(tb == 0)
    def _():
        st[...] = s0_ref[...]
        for s in range(nb):
            xc[s, pad - tail:pad, :] = cbuf_ref[s]

    if n_tb > 1:
        @pl.when(tb > 0)
        def _():
            for s in range(nb):
                xc[s, pad - tail:pad, :] = xc[s, tb_rows + pad - tail:tb_rows + pad, :]

    for s in range(nb):
        xc[s, pad:pad + tb_rows, :] = qkv_ref[s * tb_rows:(s + 1) * tb_rows, :]

    slab = min(tb_rows, 64)
    for s in range(nb):
        for sl in range(tb_rows // slab):
            for cb in range(GDN_CONV_CH // 512):
                cs = slice(cb * 512, (cb + 1) * 512)
                lo = pad - tail + sl * slab
                acc = xc[s, lo:lo + slab, cs] * cw_ref[0:1, cs]
                for i in range(1, CONV_WIDTH):
                    acc = acc + xc[s, lo + i:lo + i + slab, cs] * cw_ref[i:i + 1, cs]
                act[s * tb_rows + sl * slab:s * tb_rows + (sl + 1) * slab, cs] = _silu(acc)

    ii = lax.broadcasted_iota(jnp.int32, (c, c), 0)
    jj = lax.broadcasted_iota(jnp.int32, (c, c), 1)
    lower = (ii >= jj)
    lower_f = lower.astype(F32)
    strict = (ii > jj)
    rowmask = None
    if valid < c:
        rowmask = lax.broadcasted_iota(jnp.int32, (c, 1), 0) < valid

    def hs(h, w):
        return slice(h * w, (h + 1) * w)

    n_chunks = tb_rows // c
    cpi = next(k for k in (GDN_CHUNKS_PER_TRIP, 2, 1) if n_chunks % k == 0)
    p1_units = [(g, h) for g in range(nb * cpi) for h in range(n_heads)]

    def phase1(ci, carry):
        rows, b_ts, gc_ts, gc_tts = [], [], [], []
        for g in range(nb * cpi):
            rr = pl.ds(_chunk_rows(g // cpi, tb_rows, ci * cpi + g % cpi, c), c)
            sm = sm_ref[rr, :]
            g_t = -jnp.exp(alog_ref[...]) * jax.nn.softplus(sm + dtb_ref[...])
            b_t = jax.nn.sigmoid(sm)
            if rowmask is not None:
                g_t = jnp.where(rowmask, g_t, 0.0)
                b_t = jnp.where(rowmask, b_t, 0.0)
            gc_t = _dot_hi(lower_f, g_t)
            gcs[rr, :] = gc_t
            rows.append(rr)
            b_ts.append(b_t)
            gc_ts.append(gc_t)
            gc_tts.append(gc_t.T)
        qn, kn, kb, vb = {}, {}, {}, {}
        for (s, h) in p1_units:
            q = act[rows[s], hs(h, GDN_DK)]
            k = act[rows[s], slice(GDN_QK_W + h * GDN_DK, GDN_QK_W + (h + 1) * GDN_DK)]
            v = act[rows[s], slice(2 * GDN_QK_W + h * GDN_DV, 2 * GDN_QK_W + (h + 1) * GDN_DV)]
            if rowmask is not None:
                q = jnp.where(rowmask, q, 0.0)
                k = jnp.where(rowmask, k, 0.0)
                v = jnp.where(rowmask, v, 0.0)
            qn[s, h] = q * lax.rsqrt(jnp.sum(q * q, axis=-1, keepdims=True) + L2_EPS) * (GDN_DK ** -0.5)
            kn[s, h] = k * lax.rsqrt(jnp.sum(k * k, axis=-1, keepdims=True) + L2_EPS)
            beta = b_ts[s][:, SM_B + h:SM_B + h + 1]
            kb[s, h] = kn[s, h] * beta
            vb[s, h] = v * beta
        s1 = {u: _dot_nt(jnp.concatenate([kb[u], qn[u]], axis=0), kn[u]) for u in p1_units}
        mm = []
        for (s, h) in p1_units:
            gcol = gc_ts[s][:, SM_A + h:SM_A + h + 1]
            grow = gc_tts[s][SM_A + h:SM_A + h + 1, :]
            dec = jnp.exp(jnp.where(lower, gcol - grow, -jnp.inf))
            mm.append(jnp.where(strict, s1[s, h][:c] * dec, 0.0))
            aqs[h, rows[s], :] = s1[s, h][c:] * dec
        tms = _tri_inv_all(mm, c, ii, jj)
        for (s, h), tm in zip(p1_units, tms):
            gcol = gc_ts[s][:, SM_A + h:SM_A + h + 1]
            eg = jnp.exp(gcol)
            uw = _dot(tm, jnp.concatenate([vb[s, h], kb[s, h] * eg], axis=1))
            us[rows[s], hs(h, GDN_DV)] = uw[:, :GDN_DV]
            wss[rows[s], hs(h, GDN_DV)] = uw[:, GDN_DV:]
            qgs[rows[s], hs(h, GDN_DK)] = qn[s, h] * eg
            kds[rows[s], hs(h, GDN_DK)] = kn[s, h] * jnp.exp(gcol[c - 1:c, :] - gcol)
        return carry

    def phase2(ci, carry):
        r0 = [_chunk_rows(s, tb_rows, ci, c) for s in range(nb)]
        rows = [pl.ds(r, c) for r in r0]
        ws = {(s, h): _dot(jnp.concatenate([wss[rows[s], hs(h, GDN_DV)], qgs[rows[s], hs(h, GDN_DK)]], axis=0),
                           st[s, h]) for (s, h) in units}
        v_new = {(s, h): us[rows[s], hs(h, GDN_DV)] - ws[s, h][:c] for (s, h) in units}
        o = {(s, h): ws[s, h][c:] + _dot(aqs[h, rows[s], :], v_new[s, h]) for (s, h) in units}
        upd = {(s, h): _dot_tn(kds[rows[s], hs(h, GDN_DK)], v_new[s, h]) for (s, h) in units}
        for (s, h) in units:
            g_last = gcs[pl.ds(r0[s] + c - 1, 1), SM_A + h:SM_A + h + 1]
            st[s, h] = st[s, h] * jnp.exp(g_last) + upd[s, h]
        for s in range(nb):
            o_ref[rows[s], :] = jnp.concatenate(
                [_gated_norm(o[s, h], nw_ref[...], z_ref[rows[s], hs(h, GDN_DV)]) for h in range(n_heads)], axis=1)
        return carry

    _for_chunks(n_chunks // cpi, phase1)
    _for_chunks(n_chunks, phase2)

    @pl.when(tb == n_tb - 1)
    def _():
        sout_ref[...] = st[...]
        last = tb_rows if valid == c else valid
        for s in range(nb):
            cout_ref[s] = xc[s, pad + last - tail:pad + last, :]


def _gdn(proj, n_seq, nb, t_len, tb_rows, chunk, valid, conv_buf, s0, conv_w, alog, dtb, nw):
    n_tb = t_len // tb_rows
    assert nb == 1 or n_tb == 1
    rows = nb * tb_rows

    def rowblk(b, t):
        return b * n_tb + t

    body = functools.partial(_gdn_body, nb=nb, tb_rows=tb_rows, chunk=chunk, valid=valid, n_tb=n_tb)
    return pl.pallas_call(
        body,
        grid=(n_seq // nb, n_tb),
        in_specs=[
            pl.BlockSpec((rows, GDN_CONV_CH), lambda b, t: (rowblk(b, t), COL_QKV // GDN_CONV_CH)),
            pl.BlockSpec((rows, GDN_V_W), lambda b, t: (rowblk(b, t), COL_Z // GDN_V_W)),
            pl.BlockSpec((rows, SM_W), lambda b, t: (rowblk(b, t), COL_SM // SM_W)),
            pl.BlockSpec((nb, CONV_WIDTH - 1, GDN_CONV_CH), lambda b, t: (b, 0, 0)),
            pl.BlockSpec((nb, GDN_HEADS, GDN_DK, GDN_DV), lambda b, t: (b, 0, 0, 0)),
            pl.BlockSpec((CONV_WIDTH, GDN_CONV_CH), lambda b, t: (0, 0)),
            pl.BlockSpec((1, SM_W), lambda b, t: (0, 0)),
            pl.BlockSpec((1, SM_W), lambda b, t: (0, 0)),
            pl.BlockSpec((1, GDN_DV), lambda b, t: (0, 0)),
        ],
        out_specs=[
            pl.BlockSpec((rows, GDN_V_W), lambda b, t: (rowblk(b, t), 0)),
            pl.BlockSpec((nb, GDN_HEADS, GDN_DK, GDN_DV), lambda b, t: (b, 0, 0, 0)),
            pl.BlockSpec((nb, CONV_WIDTH - 1, GDN_CONV_CH), lambda b, t: (b, 0, 0)),
        ],
        out_shape=[
            jax.ShapeDtypeStruct((n_seq * t_len, GDN_V_W), F32),
            jax.ShapeDtypeStruct((n_seq, GDN_HEADS, GDN_DK, GDN_DV), F32),
            jax.ShapeDtypeStruct((n_seq, CONV_WIDTH - 1, GDN_CONV_CH), F32),
        ],
        scratch_shapes=[
            pltpu.VMEM((nb, GDN_HEADS, GDN_DK, GDN_DV), F32),
            pltpu.VMEM((nb, tb_rows + SUBLANE, GDN_CONV_CH), F32),
            pltpu.VMEM((rows, GDN_CONV_CH), F32),
            pltpu.VMEM((rows, SM_W), F32),
            pltpu.VMEM((rows, GDN_V_W), F32),
            pltpu.VMEM((rows, GDN_V_W), F32),
            pltpu.VMEM((rows, GDN_QK_W), F32),
            pltpu.VMEM((rows, GDN_QK_W), F32),
            pltpu.VMEM((GDN_HEADS, rows, chunk), F32),
        ],
        compiler_params=pltpu.CompilerParams(dimension_semantics=("arbitrary", "arbitrary"),
                                             vmem_limit_bytes=VMEM_LIMIT),
        name="gdn_mixer",
    )(proj, proj, proj, conv_buf, s0, conv_w, alog, dtb, nw)


def _gla_body(q_ref, k_ref, v_ref, go_ref, sm_ref, s0_ref, wgk_ref, bgk_ref, nw_ref,
              o_ref, sout_ref, st, qes, ois, upds, decs, *, nb, tb_rows, chunk, valid, n_tb):
    tb = pl.program_id(1)
    c = chunk
    n_heads = GLA_HEADS
    units = [(s, h) for s in range(nb) for h in range(n_heads)]

    @pl.when(tb == 0)
    def _():
        st[...] = s0_ref[...]

    ii = lax.broadcasted_iota(jnp.int32, (c, c), 0)
    jj = lax.broadcasted_iota(jnp.int32, (c, c), 1)
    lower = (ii >= jj)
    lower_f = lower.astype(F32)
    rid = lax.broadcasted_iota(jnp.int32, (c, 1), 0)
    rowmask = (rid < valid) if valid < c else None
    n_sub = max(c // 16, 1)
    sub = c // n_sub

    n_chunks = tb_rows // c
    cpi = next(k for k in (GLA_CHUNKS_PER_TRIP, 2, 1) if n_chunks % k == 0)
    p1_units = [(g, h) for g in range(nb * cpi) for h in range(n_heads)]

    def phase1(ci, carry):
        rows, slots, bcs, bc_ts = [], [], [], []
        for g in range(nb * cpi):
            chunk_idx = ci * cpi + g % cpi
            rr = pl.ds(_chunk_rows(g // cpi, tb_rows, chunk_idx, c), c)
            slots.append((g // cpi) * n_chunks + chunk_idx)
            gk = jax.nn.log_sigmoid(_dot(sm_ref[rr, :], wgk_ref[...]) + bgk_ref[...]) / GLA_GATE_NORMALIZER
            if rowmask is not None:
                gk = jnp.where(rowmask, gk, 0.0)
            bc = _dot_hi(lower_f, gk)
            rows.append(rr)
            bcs.append(bc)
            bc_ts.append(bc.T)
        q, k, v, bch = {}, {}, {}, {}
        for (s, h) in p1_units:
            ks = slice(h * GLA_DK, (h + 1) * GLA_DK)
            vs = slice(h * GLA_DV, (h + 1) * GLA_DV)
            q[s, h] = q_ref[rows[s], ks] * (GLA_DK ** -0.5)
            kk = k_ref[rows[s], ks]
            vv = v_ref[rows[s], vs]
            if rowmask is not None:
                kk = jnp.where(rowmask, kk, 0.0)
                vv = jnp.where(rowmask, vv, 0.0)
            k[s, h], v[s, h] = kk, vv
            bch[s, h] = bcs[s][:, ks]
        for (g, h) in p1_units:
            qes[h, rows[g], :] = q[g, h] * jnp.exp(bch[g, h])
        a = {}
        for u in p1_units:
            q_parts, k_parts = [], []
            for sb in range(n_sub):
                ref_row = bch[u][sb * sub:sb * sub + 1, :]
                in_blk = (rid >= sb * sub) & (rid < (sb + 1) * sub)
                q_parts.append(jnp.where(in_blk, q[u] * jnp.exp(jnp.where(in_blk, bch[u] - ref_row, 0.0)), 0.0))
                k_parts.append(k[u] * jnp.exp(jnp.where(rid < (sb + 1) * sub, ref_row - bch[u], 0.0)))
            q_hat = jnp.concatenate(q_parts, axis=1) if n_sub > 1 else q_parts[0]
            k_hat = jnp.concatenate(k_parts, axis=1) if n_sub > 1 else k_parts[0]
            a[u] = jnp.where(lower, _dot_nt(q_hat, k_hat), 0.0)
        upd = {u: _dot_tn(k[u] * jnp.exp(bch[u][c - 1:c, :] - bch[u]), v[u]) for u in p1_units}
        o_intra = {u: _dot(a[u], v[u]) for u in p1_units}
        for (g, h) in p1_units:
            dec_col = bc_ts[g][h * GLA_DK:(h + 1) * GLA_DK, c - 1:c]
            decs[slots[g], h] = jnp.broadcast_to(jnp.exp(dec_col), (GLA_DK, GLA_DV))
            upds[slots[g], h] = upd[g, h]
            ois[rows[g], h * GLA_DV:(h + 1) * GLA_DV] = o_intra[g, h]
        return carry

    def phase2(ci, carry):
        rows = [pl.ds(_chunk_rows(s, tb_rows, ci, c), c) for s in range(nb)]
        o = {(s, h): ois[rows[s], h * GLA_DV:(h + 1) * GLA_DV] + _dot(qes[h, rows[s], :], st[s, h])
             for (s, h) in units}
        for (s, h) in units:
            st[s, h] = decs[s * n_chunks + ci, h] * st[s, h] + upds[s * n_chunks + ci, h]
        for s in range(nb):
            o_ref[rows[s], :] = jnp.concatenate(
                [_gated_norm(o[s, h], nw_ref[...], go_ref[rows[s], h * GLA_DV:(h + 1) * GLA_DV])
                 for h in range(n_heads)], axis=1)
        return carry

    _for_chunks(n_chunks // cpi, phase1)
    _for_chunks(n_chunks, phase2)

    @pl.when(tb == n_tb - 1)
    def _():
        sout_ref[...] = st[...]


def _gla(proj, n_seq, nb, t_len, tb_rows, chunk, valid, s0, wgk, bgk, nw):
    n_tb = t_len // tb_rows
    assert nb == 1 or n_tb == 1
    rows = nb * tb_rows

    def rowblk(b, t):
        return b * n_tb + t

    body = functools.partial(_gla_body, nb=nb, tb_rows=tb_rows, chunk=chunk, valid=valid, n_tb=n_tb)
    return pl.pallas_call(
        body,
        grid=(n_seq // nb, n_tb),
        in_specs=[
            pl.BlockSpec((rows, GLA_QK_W), lambda b, t: (rowblk(b, t), COL_GQ // GLA_QK_W)),
            pl.BlockSpec((rows, GLA_QK_W), lambda b, t: (rowblk(b, t), COL_GK // GLA_QK_W)),
            pl.BlockSpec((rows, GLA_V_W), lambda b, t: (rowblk(b, t), COL_GV // GLA_V_W)),
            pl.BlockSpec((rows, GLA_V_W), lambda b, t: (rowblk(b, t), COL_GG // GLA_V_W)),
            pl.BlockSpec((rows, SM_W), lambda b, t: (rowblk(b, t), COL_SM // SM_W)),
            pl.BlockSpec((nb, GLA_HEADS, GLA_DK, GLA_DV), lambda b, t: (b, 0, 0, 0)),
            pl.BlockSpec((SM_W, GLA_QK_W), lambda b, t: (0, 0)),
            pl.BlockSpec((1, GLA_QK_W), lambda b, t: (0, 0)),
            pl.BlockSpec((1, GLA_DV), lambda b, t: (0, 0)),
        ],
        out_specs=[
            pl.BlockSpec((rows, GLA_V_W), lambda b, t: (rowblk(b, t), 0)),
            pl.BlockSpec((nb, GLA_HEADS, GLA_DK, GLA_DV), lambda b, t: (b, 0, 0, 0)),
        ],
        out_shape=[
            jax.ShapeDtypeStruct((n_seq * t_len, GLA_V_W), F32),
            jax.ShapeDtypeStruct((n_seq, GLA_HEADS, GLA_DK, GLA_DV), F32),
        ],
        scratch_shapes=[
            pltpu.VMEM((nb, GLA_HEADS, GLA_DK, GLA_DV), F32),
            pltpu.VMEM((GLA_HEADS, rows, GLA_DK), F32),
            pltpu.VMEM((rows, GLA_V_W), F32),
            pltpu.VMEM((rows // chunk, GLA_HEADS, GLA_DK, GLA_DV), F32),
            pltpu.VMEM((rows // chunk, GLA_HEADS, GLA_DK, GLA_DV), F32),
        ],
        compiler_params=pltpu.CompilerParams(dimension_semantics=("arbitrary", "arbitrary"),
                                             vmem_limit_bytes=VMEM_LIMIT),
        name="gla_mixer",
    )(proj, proj, proj, proj, proj, s0, wgk, bgk, nw)


def _outproj_body(ogp_ref, ogs_ref, olp_ref, ols_ref, xp_ref, xs_ref, wo_ref, g_ref, wr_ref, br_ref,
                  x1_ref, h2_ref, rt_ref, rtt_ref, cnt_ref, base, *, n_p_blocks):
    i = pl.program_id(0)

    @pl.when(i == 0)
    def _():
        base[...] = jnp.zeros_like(base)

    o = jnp.concatenate([_group_pick(i, n_p_blocks, ogp_ref, ogs_ref),
                         _group_pick(i, n_p_blocks, olp_ref, ols_ref)], axis=1)
    x1 = _group_pick(i, n_p_blocks, xp_ref, xs_ref) + jnp.dot(o.astype(BF16), wo_ref[...],
                                                               preferred_element_type=F32)
    x1_ref[...] = x1
    h = _rms(x1, g_ref[...])
    _store_token_tiles(h2_ref, _pack_bf16_pairs(h))
    logits = _dot_3pass(h, wr_ref[...]) + br_ref[...]

    tm = logits.shape[0]
    lt = logits.T[:N_EXPERTS]
    eid = lax.broadcasted_iota(jnp.int32, (N_EXPERTS, tm), 0)
    work = lt
    sel = jnp.zeros((N_EXPERTS, tm), F32)
    hits, ids, vals = [], [], []
    for _ in range(TOP_K):
        m = jnp.max(work, axis=0, keepdims=True)
        idx = jnp.min(jnp.where(work == m, eid, N_EXPERTS), axis=0, keepdims=True)
        hit = eid == idx
        hits.append(hit)
        ids.append(idx)
        vals.append(m)
        work = jnp.where(hit, -jnp.inf, work)
        sel = sel + hit.astype(F32)
    exps = [jnp.exp(v - vals[0]) for v in vals]
    den = exps[0]
    for e in exps[1:]:
        den = den + e
    gates = [e / den for e in exps]

    ri = lax.broadcasted_iota(jnp.int32, (tm, tm), 0)
    ci = lax.broadcasted_iota(jnp.int32, (tm, tm), 1)
    before = _dot(sel, (ri < ci).astype(F32)) + base[...]
    ranks = [jnp.sum(jnp.where(hit, before, 0.0), axis=0, keepdims=True) for hit in hits]
    base[...] = base[...] + jnp.sum(sel, axis=1, keepdims=True)
    cnt_ref[...] = base[...]

    row = lax.broadcasted_iota(jnp.int32, (LANE, tm), 0)
    rec = jnp.zeros((LANE, tm), F32)
    for k in range(TOP_K):
        rec = jnp.where(row == k, ids[k].astype(F32), rec)
        rec = jnp.where(row == TOP_K + k, ranks[k], rec)
        rec = jnp.where(row == 2 * TOP_K + k, gates[k], rec)
    rt_ref[...] = rec.T
    rtt_ref[...] = rec[:2 * TOP_K]


def _outproj(og_p, og_s, ol_p, ol_s, x_p, x_s, wo, g, wr, br):
    n_p_blocks, n_s_blocks = x_p.shape[0] // ROW_TILE, x_s.shape[0] // ROW_TILE
    n = x_p.shape[0] + x_s.shape[0]
    return pl.pallas_call(
        functools.partial(_outproj_body, n_p_blocks=n_p_blocks),
        grid=(n_p_blocks + n_s_blocks,),
        in_specs=_group_specs(ROW_TILE, GDN_V_W, n_p_blocks) + _group_specs(ROW_TILE, GLA_V_W, n_p_blocks)
        + _group_specs(ROW_TILE, D_MODEL, n_p_blocks) + [
            pl.BlockSpec((D_MODEL, D_MODEL), lambda i: (0, 0)),
            pl.BlockSpec((1, D_MODEL), lambda i: (0, 0)),
            pl.BlockSpec((D_MODEL, LANE), lambda i: (0, 0)),
            pl.BlockSpec((1, LANE), lambda i: (0, 0)),
        ],
        out_specs=[
            pl.BlockSpec((ROW_TILE, D_MODEL), lambda i: (i, 0)),
            pl.BlockSpec((ROW_TILE * PACK_TILES, LANE), lambda i: (i, 0)),
            pl.BlockSpec((ROW_TILE, LANE), lambda i: (i, 0)),
            pl.BlockSpec((2 * TOP_K, ROW_TILE), lambda i: (0, i)),
            pl.BlockSpec((N_EXPERTS, 1), lambda i: (0, 0)),
        ],
        out_shape=[
            jax.ShapeDtypeStruct((n, D_MODEL), F32),
            jax.ShapeDtypeStruct((n * PACK_TILES, LANE), jnp.uint32),
            jax.ShapeDtypeStruct((n, LANE), F32),
            jax.ShapeDtypeStruct((2 * TOP_K, n), F32),
            jax.ShapeDtypeStruct((N_EXPERTS, 1), F32),
        ],
        scratch_shapes=[pltpu.VMEM((N_EXPERTS, 1), F32)],
        compiler_params=pltpu.CompilerParams(dimension_semantics=("arbitrary",),
                                             vmem_limit_bytes=VMEM_LIMIT),
        name="out_proj",
    )(og_p, og_s, ol_p, ol_s, x_p, x_s, wo, g, wr, br)


def _store_token_tiles(ref2d, val):
    rows, tiles = val.shape[0], val.shape[1] // LANE
    for c in range(tiles):
        ref2d[pl.ds(c, rows, stride=tiles), :] = val[:, c * LANE:(c + 1) * LANE]


def _load_token_tiles(ref2d, first_row, rows, tiles=TOK_TILES):
    return jnp.concatenate(
        [ref2d[pl.ds(first_row * tiles + c, rows, stride=tiles), :] for c in range(tiles)], axis=1)


def _pack_bf16_pairs(x):
    half = x.shape[1] // 2
    bits = lax.bitcast_convert_type(x.astype(BF16).astype(F32), jnp.uint32)
    return (bits[:, :half] >> 16) | (bits[:, half:] & jnp.uint32(0xFFFF0000))


def _unpack_bf16_pairs(w):
    lo = lax.bitcast_convert_type(w << 16, F32)
    hi = lax.bitcast_convert_type(w & jnp.uint32(0xFFFF0000), F32)
    return jnp.concatenate([lo, hi], axis=1).astype(BF16)


def _expert_weight_copies(e, ws, wup_hbm, wdn_hbm, wup_buf, wdn_buf, wsems):
    return (pltpu.make_async_copy(wup_hbm.at[e], wup_buf.at[ws], wsems.at[ws]),
            pltpu.make_async_copy(wdn_hbm.at[e], wdn_buf.at[ws], wsems.at[ws]))


def _expert_body(be_ref, nu_ref, first_ref, wslot_ref, next_ref, x_ref,
                 wup_hbm, bup_ref, wdn_hbm, bdn_ref, y_ref, wup_buf, wdn_buf, wsems, wup_bf, wdn_bf):
    i = pl.program_id(0)
    n_used = nu_ref[0]
    ws = wslot_ref[i]
    weight_copies = functools.partial(_expert_weight_copies, wup_hbm=wup_hbm, wdn_hbm=wdn_hbm, wup_buf=wup_buf,
                                      wdn_buf=wdn_buf, wsems=wsems)

    @pl.when((i == 0) & (n_used > 0))
    def _():
        for cp in weight_copies(be_ref[0], ws):
            cp.start(priority=EXPERT_WEIGHT_QUEUE)

    @pl.when(i < n_used)
    def _():
        @pl.when(first_ref[i] == 1)
        def _():
            for cp in weight_copies(be_ref[i], ws):
                cp.wait()

            @pl.when(next_ref[i] >= 0)
            def _():
                for cp in weight_copies(next_ref[i], 1 - ws):
                    cp.start(priority=EXPERT_WEIGHT_QUEUE)

            for r in range(0, D_MODEL, WEIGHT_CAST_ROWS):
                wup_bf[r:r + WEIGHT_CAST_ROWS, :] = wup_buf[ws, r:r + WEIGHT_CAST_ROWS, :].astype(BF16)
            for r in range(0, D_FF, WEIGHT_CAST_ROWS):
                wdn_bf[r:r + WEIGHT_CAST_ROWS, :] = wdn_buf[ws, r:r + WEIGHT_CAST_ROWS, :].astype(BF16)

        x = _unpack_bf16_pairs(_load_token_tiles(x_ref, 0, EXPERT_ROWS, PACK_TILES))
        gu = _dot(x, wup_bf[...]) + bup_ref[...]
        gate = jnp.minimum(gu[:, :D_FF], SWIGLU_LIMIT)
        up = jnp.clip(gu[:, D_FF:], -SWIGLU_LIMIT, SWIGLU_LIMIT)
        a = (up + 1.0) * gate * jax.nn.sigmoid(SWIGLU_ALPHA * gate)
        _store_token_tiles(y_ref, _dot(a, wdn_bf[...]) + bdn_ref[...])

    @pl.when(i >= n_used)
    def _():
        y_ref[...] = jnp.zeros_like(y_ref)


def _experts(block_meta, xs_2d, w_up, b_up, w_down, b_down):
    n_blocks = block_meta[0].shape[0]
    grid_spec = pltpu.PrefetchScalarGridSpec(
        num_scalar_prefetch=len(block_meta),
        grid=(n_blocks,),
        in_specs=[
            pl.BlockSpec((EXPERT_ROWS * PACK_TILES, LANE), lambda i, *_: (i, 0)),
            pl.BlockSpec(memory_space=pl.ANY),
            pl.BlockSpec((None, 1, 2 * D_FF), lambda i, be, *_: (be[i], 0, 0)),
            pl.BlockSpec(memory_space=pl.ANY),
            pl.BlockSpec((None, 1, D_MODEL), lambda i, be, *_: (be[i], 0, 0)),
        ],
        out_specs=pl.BlockSpec((EXPERT_ROWS * TOK_TILES, LANE), lambda i, *_: (i, 0)),
        scratch_shapes=[
            pltpu.VMEM((2, D_MODEL, 2 * D_FF), F32),
            pltpu.VMEM((2, D_FF, D_MODEL), F32),
            pltpu.SemaphoreType.DMA((2,)),
            pltpu.VMEM((D_MODEL, 2 * D_FF), BF16),
            pltpu.VMEM((D_FF, D_MODEL), BF16),
        ],
    )
    return pl.pallas_call(
        _expert_body,
        grid_spec=grid_spec,
        out_shape=jax.ShapeDtypeStruct((n_blocks * EXPERT_ROWS * TOK_TILES, LANE), F32),
        compiler_params=pltpu.CompilerParams(dimension_semantics=("arbitrary",),
                                             vmem_limit_bytes=VMEM_LIMIT),
        name="experts",
    )(*block_meta, xs_2d, w_up, b_up.reshape(N_EXPERTS, 1, 2 * D_FF), w_down,
      b_down.reshape(N_EXPERTS, 1, D_MODEL))


def _dispatch(h_tiles, dest_kmajor, n_rows):
    n_tok = h_tiles.shape[0]
    info = plsc.get_sparse_core_info()
    n_workers = info.num_cores * info.num_subcores
    per_worker = n_tok // n_workers
    chunk = next(c for c in (128, 96, 88, 64, 48, 32, 16, 8) if per_worker % c == 0)
    assert n_tok % n_workers == 0 and per_worker % SUBLANE == 0
    mesh = plsc.VectorSubcoreMesh(core_axis_name="c", subcore_axis_name="s")

    @functools.partial(
        pl.kernel, mesh=mesh,
        out_type=jax.ShapeDtypeStruct((n_rows,) + h_tiles.shape[1:], h_tiles.dtype),
        scratch_types=[pltpu.VMEM((TOP_K, chunk), jnp.int32), pltpu.VMEM((chunk,) + h_tiles.shape[1:], h_tiles.dtype),
                       pltpu.SemaphoreType.DMA],
    )
    def dispatch(h_hbm, dest_hbm, out_hbm, idx_v, rows_v, sem):
        wid = lax.axis_index("s") * info.num_cores + lax.axis_index("c")

        def step(j, carry):
            t0 = pl.multiple_of(wid * per_worker + j * chunk, SUBLANE)
            loads = [pltpu.async_copy(h_hbm.at[pl.ds(t0, chunk)], rows_v, sem)]
            for k in range(TOP_K):
                loads.append(pltpu.async_copy(
                    dest_hbm.at[pl.ds(pl.multiple_of(k * n_tok + t0, SUBLANE), chunk)], idx_v.at[k], sem))
            for cp in loads:
                cp.wait()
            stores = [pltpu.async_copy(rows_v, out_hbm.at[idx_v.at[k]], sem) for k in range(TOP_K)]
            for cp in stores:
                cp.wait()
            return carry

        lax.fori_loop(0, per_worker // chunk, step, 0)

    return dispatch(h_tiles, dest_kmajor)


def _gather_rows(src_tiles, idx_ref, n_rows, dst2d, sem, priorities):
    def issue(j, carry):
        for u in range(DMA_ISSUE_UNROLL):
            r = j * DMA_ISSUE_UNROLL + u
            dst = dst2d.at[pl.ds(pl.multiple_of(r * TOK_TILES, TOK_TILES), TOK_TILES), :]
            pltpu.make_async_copy(src_tiles.at[idx_ref[0, r]], dst, sem).start(
                priority=priorities[u % len(priorities)])
        return carry

    lax.fori_loop(0, n_rows // DMA_ISSUE_UNROLL, issue, 0)


def _wait_rows(src2d, n_rows, dst2d, sem):
    pltpu.make_async_copy(src2d.at[pl.ds(0, n_rows * TOK_TILES), :], dst2d, sem).wait()


def _combine_body(dest_ref, dest_next_ref, rt_ref, y_tiles, y_2d, x1_ref, g_ref, op_ref, os_ref, ybuf, sems,
                  *, n_p_blocks):
    i = pl.program_id(0)
    slot = i % 2
    n_rows = TOP_K * COMBINE_ROWS

    @pl.when(i == 0)
    def _():
        _gather_rows(y_tiles, dest_ref, n_rows, ybuf.at[0], sems.at[0], COMBINE_GATHER_QUEUES)

    _wait_rows(y_2d, n_rows, ybuf.at[slot], sems.at[slot])

    @pl.when(i + 1 < pl.num_programs(0))
    def _():
        _gather_rows(y_tiles, dest_next_ref, n_rows, ybuf.at[1 - slot], sems.at[1 - slot], COMBINE_GATHER_QUEUES)

    buf = ybuf.at[slot]
    moe = _load_token_tiles(buf, 0, COMBINE_ROWS) * rt_ref[:, 2 * TOP_K:2 * TOP_K + 1]
    for k in range(1, TOP_K):
        moe = moe + _load_token_tiles(buf, k * COMBINE_ROWS, COMBINE_ROWS) * rt_ref[:, 2 * TOP_K + k:2 * TOP_K + k + 1]
    res = _rms(x1_ref[...] + moe, g_ref[...])

    @pl.when(i < n_p_blocks)
    def _():
        op_ref[...] = res

    @pl.when(i >= n_p_blocks)
    def _():
        os_ref[...] = res


def _combine(dest_b, rt, y_2d, x1, g, n_p):
    n = x1.shape[0]
    n_blk = n // COMBINE_ROWS
    n_p_blocks = n_p // COMBINE_ROWS
    dest_blocks = dest_b.reshape(n_blk, 1, TOP_K * COMBINE_ROWS)
    return pl.pallas_call(
        functools.partial(_combine_body, n_p_blocks=n_p_blocks),
        grid=(n_blk,),
        in_specs=[
            pl.BlockSpec((None, 1, COMBINE_ROWS * TOP_K), lambda i: (i, 0, 0), memory_space=pltpu.SMEM),
            pl.BlockSpec((None, 1, COMBINE_ROWS * TOP_K), lambda i: (jnp.minimum(i + 1, n_blk - 1), 0, 0),
                         memory_space=pltpu.SMEM),
            pl.BlockSpec((COMBINE_ROWS, LANE), lambda i: (i, 0)),
            pl.BlockSpec(memory_space=pl.ANY),
            pl.BlockSpec(memory_space=pl.ANY),
            pl.BlockSpec((COMBINE_ROWS, D_MODEL), lambda i: (i, 0)),
            pl.BlockSpec((1, D_MODEL), lambda i: (0, 0)),
        ],
        out_specs=_group_specs(COMBINE_ROWS, D_MODEL, n_p_blocks),
        out_shape=[jax.ShapeDtypeStruct((n_p, D_MODEL), F32), jax.ShapeDtypeStruct((n - n_p, D_MODEL), F32)],
        scratch_shapes=[pltpu.VMEM((2, TOP_K * COMBINE_ROWS * TOK_TILES, LANE), F32),
                        pltpu.SemaphoreType.DMA((2,))],
        compiler_params=pltpu.CompilerParams(dimension_semantics=("arbitrary",),
                                             vmem_limit_bytes=VMEM_LIMIT),
        name="combine",
    )(dest_blocks, dest_blocks, rt, y_2d.reshape(-1, TOK_TILES, LANE), y_2d, x1, g)


def _plan_body(rtt_ref, cnt_ref, dk_ref, db_ref, meta_ref, pst):
    i = pl.program_id(0)
    sh = _log2(EXPERT_ROWS)
    n_e = N_EXPERTS

    @pl.when(i == 0)
    def _():
        cnt = cnt_ref[...].astype(jnp.int32)
        padded = (((cnt + (EXPERT_ROWS - 1)) >> sh) << sh).astype(F32)
        e_r = lax.broadcasted_iota(jnp.int32, (n_e, n_e), 0)
        e_c = lax.broadcasted_iota(jnp.int32, (n_e, n_e), 1)
        p_t = jnp.broadcast_to(padded, (n_e, n_e)).T
        pend = jnp.sum(jnp.where(e_c <= e_r, p_t, 0.0), axis=1, keepdims=True)
        pst[...] = pend - padded
        has_rows = p_t > 0.0
        group = jnp.sum(jnp.where((e_c <= e_r) & has_rows, 1.0, 0.0), axis=1, keepdims=True) - 1.0
        nxt = jnp.min(jnp.where((e_c > e_r) & has_rows, e_c, n_e), axis=1, keepdims=True)
        nxt = jnp.where(nxt >= n_e, -1, nxt)

        mb = meta_ref.shape[1]
        blk = lax.broadcasted_iota(jnp.int32, (n_e, mb), 1)
        eb = lax.broadcasted_iota(jnp.int32, (n_e, mb), 0)
        first_row = (blk * EXPERT_ROWS).astype(F32)

        def expert_of(row0):
            return jnp.minimum(jnp.sum(jnp.where(pend <= row0, 1, 0), axis=0, keepdims=True), n_e - 1)

        be = expert_of(first_row)
        be_prev = expert_of(first_row - EXPERT_ROWS)
        hit = eb == be
        wslot = jnp.sum(jnp.where(hit, group, 0.0), axis=0, keepdims=True).astype(jnp.int32) & 1
        nx = jnp.sum(jnp.where(hit, nxt, 0), axis=0, keepdims=True)
        n_used = pend[n_e - 1:n_e, :].astype(jnp.int32) >> sh
        lane = lax.broadcasted_iota(jnp.int32, (1, mb), 1)
        first = (((be != be_prev) | (lane == 0)) & (lane < n_used)).astype(jnp.int32)
        row8 = lax.broadcasted_iota(jnp.int32, (SUBLANE, mb), 0)
        meta = jnp.where(row8 == 0, be, jnp.where(row8 == 1, first, jnp.where(row8 == 2, wslot,
                         jnp.where(row8 == 3, nx, n_used))))
        meta_ref[...] = meta

    tm = rtt_ref.shape[1]
    eid = lax.broadcasted_iota(jnp.int32, (n_e, tm), 0).astype(F32)
    row8 = lax.broadcasted_iota(jnp.int32, (SUBLANE, tm), 0)
    d8 = jnp.zeros((SUBLANE, tm), jnp.int32)
    for k in range(TOP_K):
        start = jnp.sum(jnp.where(eid == rtt_ref[k:k + 1, :], pst[...], 0.0), axis=0, keepdims=True)
        d8 = jnp.where(row8 == k, (start + rtt_ref[TOP_K + k:TOP_K + k + 1, :]).astype(jnp.int32), d8)
    dk_ref[...] = d8[:TOP_K]
    for b in range(tm // COMBINE_ROWS):
        db_ref[b] = d8[:TOP_K, b * COMBINE_ROWS:(b + 1) * COMBINE_ROWS]


def _plan(rtt, cnt):
    n = rtt.shape[1]
    n_rows = n * TOP_K + N_EXPERTS * EXPERT_ROWS
    n_blocks = n_rows // EXPERT_ROWS
    mb = -(-n_blocks // LANE) * LANE
    tile = max(t for t in range(COMBINE_ROWS, PLAN_TILE_MAX + 1, COMBINE_ROWS) if n % t == 0)
    dk, db, meta = pl.pallas_call(
        _plan_body,
        grid=(n // tile,),
        in_specs=[pl.BlockSpec((2 * TOP_K, tile), lambda i: (0, i)),
                  pl.BlockSpec((N_EXPERTS, 1), lambda i: (0, 0))],
        out_specs=[pl.BlockSpec((TOP_K, tile), lambda i: (0, i)),
                   pl.BlockSpec((tile // COMBINE_ROWS, TOP_K, COMBINE_ROWS), lambda i: (i, 0, 0)),
                   pl.BlockSpec((SUBLANE, mb), lambda i: (0, 0))],
        out_shape=[jax.ShapeDtypeStruct((TOP_K, n), jnp.int32),
                   jax.ShapeDtypeStruct((n // COMBINE_ROWS, TOP_K, COMBINE_ROWS), jnp.int32),
                   jax.ShapeDtypeStruct((SUBLANE, mb), jnp.int32)],
        scratch_shapes=[pltpu.VMEM((N_EXPERTS, 1), F32)],
        compiler_params=pltpu.CompilerParams(dimension_semantics=("arbitrary",)),
        name="plan",
    )(rtt, cnt)
    block_meta = (meta[0, :n_blocks], meta[4, 0:1], meta[1, :n_blocks], meta[2, :n_blocks], meta[3, :n_blocks])
    return dk, db, n_rows, block_meta


def _pad_lanes(v, width):
    return jnp.zeros((1, width), F32).at[0, :v.shape[0]].set(v.astype(F32))


def kernel(x_prompt, x_sample, state_gdn_conv, state_gdn, state_gla, rms_mix_w, w_in, conv_w, gdn_a_log,
           gdn_dt_bias, gdn_norm_w, gla_gk_w, gla_gk_b, gla_norm_w, w_out, rms_ffn_w, w_router, b_router,
           w_up, b_up, w_down, b_down, rms_final_w):
    bp, tp, d = x_prompt.shape
    bs, ts, _ = x_sample.shape
    n_p, n_s = bp * tp, bs * ts
    assert d == D_MODEL and state_gdn.shape[0] == 1, "single-layer kernel"
    assert tp >= CONV_WIDTH - 1 and ts >= CONV_WIDTH - 1, "new conv state is taken from the new tokens only"
    l = 0

    wi = w_in[l]
    a0 = GDN_CONV_CH + GDN_V_W
    g0 = a0 + 2 * GDN_HEADS
    lr0 = g0 + 2 * GLA_QK_W + 2 * GLA_V_W
    small = jnp.concatenate([wi[:, a0:a0 + 2 * GDN_HEADS], wi[:, lr0:lr0 + GLA_GATE_RANK],
                             jnp.zeros((d, SM_W - 2 * GDN_HEADS - GLA_GATE_RANK), F32)], axis=1)
    w_big = jnp.concatenate([wi[:, :a0], wi[:, g0:lr0], small], axis=1).astype(BF16)
    alog = _pad_lanes(gdn_a_log[l], SM_W)
    dtb = _pad_lanes(gdn_dt_bias[l], SM_W)
    wgk = jnp.zeros((SM_W, GLA_QK_W), F32).at[SM_LR:SM_LR + GLA_GATE_RANK].set(gla_gk_w[l])
    wr = jnp.zeros((d, LANE), F32).at[:, :N_EXPERTS].set(w_router[l])
    br = jnp.full((1, LANE), -1e30, F32).at[0, :N_EXPERTS].set(b_router[l])

    assert n_p % ROW_TILE == 0 and n_s % ROW_TILE == 0
    x_p, x_s = x_prompt.reshape(n_p, d), x_sample.reshape(n_s, d)
    proj = _inproj(x_p, x_s, rms_mix_w[l][None, :], w_big)

    tb_p = PROMPT_TIME_BLOCK
    zeros_conv = jnp.zeros((bp, CONV_WIDTH - 1, GDN_CONV_CH), F32)
    og_p, gdn_p, conv_p = _gdn(proj, bp, 1, tp, tb_p, CHUNK, CHUNK, zeros_conv,
                               jnp.zeros((bp, GDN_HEADS, GDN_DK, GDN_DV), F32), conv_w[l], alog, dtb,
                               gdn_norm_w[l][None, :])
    ol_p, gla_p = _gla(proj, bp, 1, tp, tb_p, CHUNK, CHUNK, jnp.zeros((bp, GLA_HEADS, GLA_DK, GLA_DV), F32),
                       wgk, gla_gk_b[l][None, :], gla_norm_w[l][None, :])

    ts_pad = SUBLANE
    nb_s = SAMPLE_SEQS_PER_STEP
    proj_s = proj[n_p:].reshape(bs, ts, PROJ_W)
    proj_sp = jnp.pad(proj_s, ((0, 0), (0, ts_pad - ts), (0, 0))).reshape(bs * ts_pad, PROJ_W)
    og_s, gdn_s, conv_s = _gdn(proj_sp, bs, nb_s, ts_pad, ts_pad, ts_pad, ts, state_gdn_conv[l], state_gdn[l],
                               conv_w[l], alog, dtb, gdn_norm_w[l][None, :])
    ol_s, gla_s = _gla(proj_sp, bs, nb_s, ts_pad, ts_pad, ts_pad, ts, state_gla[l], wgk, gla_gk_b[l][None, :],
                       gla_norm_w[l][None, :])
    og_s = og_s.reshape(bs, ts_pad, GDN_V_W)[:, :ts].reshape(n_s, GDN_V_W)
    ol_s = ol_s.reshape(bs, ts_pad, GLA_V_W)[:, :ts].reshape(n_s, GLA_V_W)

    x1, h2, rt, rtt, cnt = _outproj(og_p, og_s, ol_p, ol_s, x_p, x_s, w_out[l].astype(BF16),
                                    rms_ffn_w[l][None, :], wr, br)

    dest_k, dest_b, n_rows, block_meta = _plan(rtt, cnt)
    xs = _dispatch(h2.reshape(-1, PACK_TILES, LANE), dest_k.reshape(-1), n_rows)
    y_rows = _experts(block_meta, xs.reshape(-1, LANE), w_up[l], b_up[l], w_down[l], b_down[l])
    y_p, y_s = _combine(dest_b, rt, y_rows, x1, rms_final_w[None, :], n_p)
    y_prompt = y_p.reshape(bp, tp, d)
    y_sample = y_s.reshape(bs, ts, d)
    return (y_prompt, y_sample, conv_p[None], gdn_p[None], gla_p[None], conv_s[None], gdn_s[None], gla_s[None])
```

```python
import functools

import jax
import jax.numpy as jnp
from jax import lax
from jax.experimental import pallas as pl
from jax.experimental.pallas import tpu as pltpu
from jax.experimental.pallas import tpu_sc as plsc

F32 = jnp.float32
BF16 = jnp.bfloat16
HI = lax.Precision.HIGHEST

D_MODEL = 1024
GDN_HEADS = 4
GDN_DK = 128
GDN_DV = 128
GLA_HEADS = 4
GLA_DK = 64
GLA_DV = 128
GLA_GATE_RANK = 16
GLA_GATE_NORMALIZER = 16.0
CONV_WIDTH = 4
CHUNK = 64
N_EXPERTS = 32
TOP_K = 4
D_FF = 1024
SWIGLU_LIMIT = 7.0
SWIGLU_ALPHA = 1.702
RMS_EPS = 1e-6
L2_EPS = 1e-6

GDN_QK_W = GDN_HEADS * GDN_DK
GDN_V_W = GDN_HEADS * GDN_DV
GDN_CONV_CH = 2 * GDN_QK_W + GDN_V_W
GLA_QK_W = GLA_HEADS * GLA_DK
GLA_V_W = GLA_HEADS * GLA_DV

COL_QKV = 0
COL_Z = 1536
COL_GQ = 2048
COL_GK = 2304
COL_GV = 2560
COL_GG = 3072
COL_SM = 3584
SM_W = 128
PROJ_W = COL_SM + SM_W
SM_A, SM_B, SM_LR = 0, 4, 8

LANE = 128
SUBLANE = 8
TOK_TILES = D_MODEL // LANE
PACK_TILES = TOK_TILES // 2
ROW_TILE = 512
EXPERT_ROWS = 512
EXPERT_WEIGHT_QUEUE = 1
WEIGHT_CAST_ROWS = 128
COMBINE_ROWS = 256
PLAN_TILE_MAX = 2048
DMA_ISSUE_UNROLL = 8
COMBINE_GATHER_QUEUES = (0, 1)
GDN_CHUNKS_PER_TRIP = 4
GLA_CHUNKS_PER_TRIP = 4
PROMPT_TIME_BLOCK = 512
SAMPLE_SEQS_PER_STEP = 8
VMEM_LIMIT = 56 * 1024 * 1024


def _dot(a, b):
    return jnp.dot(a.astype(BF16), b.astype(BF16), preferred_element_type=F32)


def _dot_nt(a, b):
    return lax.dot_general(a.astype(BF16), b.astype(BF16), (((1,), (1,)), ((), ())),
                           preferred_element_type=F32)


def _dot_tn(a, b):
    return lax.dot_general(a.astype(BF16), b.astype(BF16), (((0,), (0,)), ((), ())),
                           preferred_element_type=F32)


def _dot_hi(a, b):
    return jnp.dot(a, b, precision=HI, preferred_element_type=F32)


def _dot_3pass(a, b):
    a_hi = a.astype(BF16)
    b_hi = b.astype(BF16)
    a_lo = (a - a_hi.astype(F32)).astype(BF16)
    b_lo = (b - b_hi.astype(F32)).astype(BF16)

    def mm(x, y):
        return jnp.dot(x, y, preferred_element_type=F32)

    return (mm(a_lo, b_hi) + mm(a_hi, b_lo)) + mm(a_hi, b_hi)


def _rms(x, w):
    return x * lax.rsqrt(jnp.mean(x * x, axis=-1, keepdims=True) + RMS_EPS) * w


def _silu(x):
    return x * jax.nn.sigmoid(x)


def _group_specs(rows, width, n_p_blocks):
    return [pl.BlockSpec((rows, width), lambda i: (jnp.minimum(i, n_p_blocks - 1), 0)),
            pl.BlockSpec((rows, width), lambda i: (jnp.maximum(i - n_p_blocks, 0), 0))]


def _group_pick(i, n_p_blocks, p_ref, s_ref):
    return jnp.where(i < n_p_blocks, p_ref[...], s_ref[...])


def _inproj_body(xp_ref, xs_ref, g_ref, w_ref, o_ref, *, n_p_blocks):
    x = _group_pick(pl.program_id(0), n_p_blocks, xp_ref, xs_ref)
    h = _rms(x, g_ref[...])
    o_ref[...] = jnp.dot(h.astype(BF16), w_ref[...], preferred_element_type=F32)


def _inproj(x_p, x_s, g, w):
    n_p_blocks, n_s_blocks = x_p.shape[0] // ROW_TILE, x_s.shape[0] // ROW_TILE
    n = x_p.shape[0] + x_s.shape[0]
    return pl.pallas_call(
        functools.partial(_inproj_body, n_p_blocks=n_p_blocks),
        grid=(n_p_blocks + n_s_blocks,),
        in_specs=_group_specs(ROW_TILE, D_MODEL, n_p_blocks) + [
            pl.BlockSpec((1, D_MODEL), lambda i: (0, 0)),
            pl.BlockSpec((D_MODEL, PROJ_W), lambda i: (0, 0)),
        ],
        out_specs=pl.BlockSpec((ROW_TILE, PROJ_W), lambda i: (i, 0)),
        out_shape=jax.ShapeDtypeStruct((n, PROJ_W), F32),
        compiler_params=pltpu.CompilerParams(dimension_semantics=("arbitrary",),
                                             vmem_limit_bytes=VMEM_LIMIT),
        name="in_proj",
    )(x_p, x_s, g, w)


def _log2(n):
    assert n & (n - 1) == 0
    return n.bit_length() - 1


def _tri_inv_all(ms, c, ii, jj):
    eye = (ii == jj).astype(F32)
    base = min(c, 8)
    sh = _log2(base)
    blk = (ii >> sh) == (jj >> sh)
    ns = [jnp.where(blk, m, 0.0) for m in ms]
    xs = [eye - n for n in ns]
    ps = [_dot(n, n) for n in ns]
    ts = [_dot(jnp.concatenate([x, p], axis=0), p) for x, p in zip(xs, ps)]
    xs = [x + t[:c] for x, t in zip(xs, ts)]
    ps = [t[c:] for t in ts]
    xs = [x + _dot(x, p) for x, p in zip(xs, ps)]
    s = base
    while s < c:
        sh_s, sh_b = _log2(s), _log2(2 * s)
        off = ((ii >> sh_b) == (jj >> sh_b)) & ((ii >> sh_s) != (jj >> sh_s))
        ys = [_dot(x, jnp.where(off, m, 0.0)) for x, m in zip(xs, ms)]
        xs = [x - _dot(y, x) for x, y in zip(xs, ys)]
        s *= 2
    return xs


def _gated_norm(o, w, z):
    return o * lax.rsqrt(jnp.mean(o * o, axis=-1, keepdims=True) + RMS_EPS) * w * _silu(z)


def _chunk_rows(s, tb_rows, ci, c):
    r = s * tb_rows + ci * c
    if not isinstance(r, int):
        r = pl.multiple_of(r, c)
    return r


def _for_chunks(n_chunks, step):
    if n_chunks == 1:
        step(0, 0)
    else:
        lax.fori_loop(0, n_chunks, step, 0)


def _gdn_body(qkv_ref, z_ref, sm_ref, cbuf_ref, s0_ref, cw_ref, alog_ref, dtb_ref, nw_ref,
              o_ref, sout_ref, cout_ref, st, xc, act, gcs, us, wss, qgs, kds, aqs,
              *, nb, tb_rows, chunk, valid, n_tb):
    tb = pl.program_id(1)
    c = chunk
    n_heads = GDN_HEADS
    tail = CONV_WIDTH - 1
    pad = SUBLANE
    units = [(s, h) for s in range(nb) for h in range(n_heads)]

    @pl.when(tb == 0)
    def _():
        st[...] = s0_ref[...]
        for s in range(nb):
            xc[s, pad - tail:pad, :] = cbuf_ref[s]

    if n_tb > 1:
        @pl.when(tb > 0)
        def _():
            for s in range(nb):
                xc[s, pad - tail:pad, :] = xc[s, tb_rows + pad - tail:tb_rows + pad, :]

    for s in range(nb):
        xc[s, pad:pad + tb_rows, :] = qkv_ref[s * tb_rows:(s + 1) * tb_rows, :]

    slab = min(tb_rows, 64)
    for s in range(nb):
        for sl in range(tb_rows // slab):
            for cb in range(GDN_CONV_CH // 512):
                cs = slice(cb * 512, (cb + 1) * 512)
                lo = pad - tail + sl * slab
                acc = xc[s, lo:lo + slab, cs] * cw_ref[0:1, cs]
                for i in range(1, CONV_WIDTH):
                    acc = acc + xc[s, lo + i:lo + i + slab, cs] * cw_ref[i:i + 1, cs]
                act[s * tb_rows + sl * slab:s * tb_rows + (sl + 1) * slab, cs] = _silu(acc)

    ii = lax.broadcasted_iota(jnp.int32, (c, c), 0)
    jj = lax.broadcasted_iota(jnp.int32, (c, c), 1)
    lower = (ii >= jj)
    lower_f = lower.astype(F32)
    strict = (ii > jj)
    rowmask = None
    if valid < c:
        rowmask = lax.broadcasted_iota(jnp.int32, (c, 1), 0) < valid

    def hs(h, w):
        return slice(h * w, (h + 1) * w)

    n_chunks = tb_rows // c
    cpi = next(k for k in (GDN_CHUNKS_PER_TRIP, 2, 1) if n_chunks % k == 0)
    p1_units = [(g, h) for g in range(nb * cpi) for h in range(n_heads)]

    def phase1(ci, carry):
        rows, b_ts, gc_ts, gc_tts = [], [], [], []
        for g in range(nb * cpi):
            rr = pl.ds(_chunk_rows(g // cpi, tb_rows, ci * cpi + g % cpi, c), c)
            sm = sm_ref[rr, :]
            g_t = -jnp.exp(alog_ref[...]) * jax.nn.softplus(sm + dtb_ref[...])
            b_t = jax.nn.sigmoid(sm)
            if rowmask is not None:
                g_t = jnp.where(rowmask, g_t, 0.0)
                b_t = jnp.where(rowmask, b_t, 0.0)
            gc_t = _dot_hi(lower_f, g_t)
            gcs[rr, :] = gc_t
            rows.append(rr)
            b_ts.append(b_t)
            gc_ts.append(gc_t)
            gc_tts.append(gc_t.T)
        qn, kn, kb, vb = {}, {}, {}, {}
        for (s, h) in p1_units:
            q = act[rows[s], hs(h, GDN_DK)]
            k = act[rows[s], slice(GDN_QK_W + h * GDN_DK, GDN_QK_W + (h + 1) * GDN_DK)]
            v = act[rows[s], slice(2 * GDN_QK_W + h * GDN_DV, 2 * GDN_QK_W + (h + 1) * GDN_DV)]
            if rowmask is not None:
                q = jnp.where(rowmask, q, 0.0)
                k = jnp.where(rowmask, k, 0.0)
                v = jnp.where(rowmask, v, 0.0)
            qn[s, h] = q * lax.rsqrt(jnp.sum(q * q, axis=-1, keepdims=True) + L2_EPS) * (GDN_DK ** -0.5)
            kn[s, h] = k * lax.rsqrt(jnp.sum(k * k, axis=-1, keepdims=True) + L2_EPS)
            beta = b_ts[s][:, SM_B + h:SM_B + h + 1]
            kb[s, h] = kn[s, h] * beta
            vb[s, h] = v * beta
        s1 = {u: _dot_nt(jnp.concatenate([kb[u], qn[u]], axis=0), kn[u]) for u in p1_units}
        mm = []
        for (s, h) in p1_units:
            gcol = gc_ts[s][:, SM_A + h:SM_A + h + 1]
            grow = gc_tts[s][SM_A + h:SM_A + h + 1, :]
            dec = jnp.exp(jnp.where(lower, gcol - grow, -jnp.inf))
            mm.append(jnp.where(strict, s1[s, h][:c] * dec, 0.0))
            aqs[h, rows[s], :] = s1[s, h][c:] * dec
        tms = _tri_inv_all(mm, c, ii, jj)
        for (s, h), tm in zip(p1_units, tms):
            gcol = gc_ts[s][:, SM_A + h:SM_A + h + 1]
            eg = jnp.exp(gcol)
            uw = _dot(tm, jnp.concatenate([vb[s, h], kb[s, h] * eg], axis=1))
            us[rows[s], hs(h, GDN_DV)] = uw[:, :GDN_DV]
            wss[rows[s], hs(h, GDN_DV)] = uw[:, GDN_DV:]
            qgs[rows[s], hs(h, GDN_DK)] = qn[s, h] * eg
            kds[rows[s], hs(h, GDN_DK)] = kn[s, h] * jnp.exp(gcol[c - 1:c, :] - gcol)
        return carry

    def phase2(ci, carry):
        r0 = [_chunk_rows(s, tb_rows, ci, c) for s in range(nb)]
        rows = [pl.ds(r, c) for r in r0]
        ws = {(s, h): _dot(jnp.concatenate([wss[rows[s], hs(h, GDN_DV)], qgs[rows[s], hs(h, GDN_DK)]], axis=0),
                           st[s, h]) for (s, h) in units}
        v_new = {(s, h): us[rows[s], hs(h, GDN_DV)] - ws[s, h][:c] for (s, h) in units}
        o = {(s, h): ws[s, h][c:] + _dot(aqs[h, rows[s], :], v_new[s, h]) for (s, h) in units}
        upd = {(s, h): _dot_tn(kds[rows[s], hs(h, GDN_DK)], v_new[s, h]) for (s, h) in units}
        for (s, h) in units:
            g_last = gcs[pl.ds(r0[s] + c - 1, 1), SM_A + h:SM_A + h + 1]
            st[s, h] = st[s, h] * jnp.exp(g_last) + upd[s, h]
        for s in range(nb):
            o_ref[rows[s], :] = jnp.concatenate(
                [_gated_norm(o[s, h], nw_ref[...], z_ref[rows[s], hs(h, GDN_DV)]) for h in range(n_heads)], axis=1)
        return carry

    _for_chunks(n_chunks // cpi, phase1)
    _for_chunks(n_chunks, phase2)

    @pl.when(tb == n_tb - 1)
    def _():
        sout_ref[...] = st[...]
        last = tb_rows if valid == c else valid
        for s in range(nb):
            cout_ref[s] = xc[s, pad + last - tail:pad + last, :]


def _gdn(proj, n_seq, nb, t_len, tb_rows, chunk, valid, conv_buf, s0, conv_w, alog, dtb, nw):
    n_tb = t_len // tb_rows
    assert nb == 1 or n_tb == 1
    rows = nb * tb_rows

    def rowblk(b, t):
        return b * n_tb + t

    body = functools.partial(_gdn_body, nb=nb, tb_rows=tb_rows, chunk=chunk, valid=valid, n_tb=n_tb)
    return pl.pallas_call(
        body,
        grid=(n_seq // nb, n_tb),
        in_specs=[
            pl.BlockSpec((rows, GDN_CONV_CH), lambda b, t: (rowblk(b, t), COL_QKV // GDN_CONV_CH)),
            pl.BlockSpec((rows, GDN_V_W), lambda b, t: (rowblk(b, t), COL_Z // GDN_V_W)),
            pl.BlockSpec((rows, SM_W), lambda b, t: (rowblk(b, t), COL_SM // SM_W)),
            pl.BlockSpec((nb, CONV_WIDTH - 1, GDN_CONV_CH), lambda b, t: (b, 0, 0)),
            pl.BlockSpec((nb, GDN_HEADS, GDN_DK, GDN_DV), lambda b, t: (b, 0, 0, 0)),
            pl.BlockSpec((CONV_WIDTH, GDN_CONV_CH), lambda b, t: (0, 0)),
            pl.BlockSpec((1, SM_W), lambda b, t: (0, 0)),
            pl.BlockSpec((1, SM_W), lambda b, t: (0, 0)),
            pl.BlockSpec((1, GDN_DV), lambda b, t: (0, 0)),
        ],
        out_specs=[
            pl.BlockSpec((rows, GDN_V_W), lambda b, t: (rowblk(b, t), 0)),
            pl.BlockSpec((nb, GDN_HEADS, GDN_DK, GDN_DV), lambda b, t: (b, 0, 0, 0)),
            pl.BlockSpec((nb, CONV_WIDTH - 1, GDN_CONV_CH), lambda b, t: (b, 0, 0)),
        ],
        out_shape=[
            jax.ShapeDtypeStruct((n_seq * t_len, GDN_V_W), F32),
            jax.ShapeDtypeStruct((n_seq, GDN_HEADS, GDN_DK, GDN_DV), F32),
            jax.ShapeDtypeStruct((n_seq, CONV_WIDTH - 1, GDN_CONV_CH), F32),
        ],
        scratch_shapes=[
            pltpu.VMEM((nb, GDN_HEADS, GDN_DK, GDN_DV), F32),
            pltpu.VMEM((nb, tb_rows + SUBLANE, GDN_CONV_CH), F32),
            pltpu.VMEM((rows, GDN_CONV_CH), F32),
            pltpu.VMEM((rows, SM_W), F32),
            pltpu.VMEM((rows, GDN_V_W), F32),
            pltpu.VMEM((rows, GDN_V_W), F32),
            pltpu.VMEM((rows, GDN_QK_W), F32),
            pltpu.VMEM((rows, GDN_QK_W), F32),
            pltpu.VMEM((GDN_HEADS, rows, chunk), F32),
        ],
        compiler_params=pltpu.CompilerParams(dimension_semantics=("arbitrary", "arbitrary"),
                                             vmem_limit_bytes=VMEM_LIMIT),
        name="gdn_mixer",
    )(proj, proj, proj, conv_buf, s0, conv_w, alog, dtb, nw)


def _gla_body(q_ref, k_ref, v_ref, go_ref, sm_ref, s0_ref, wgk_ref, bgk_ref, nw_ref,
              o_ref, sout_ref, st, qes, ois, upds, decs, *, nb, tb_rows, chunk, valid, n_tb):
    tb = pl.program_id(1)
    c = chunk
    n_heads = GLA_HEADS
    units = [(s, h) for s in range(nb) for h in range(n_heads)]

    @pl.when(tb == 0)
    def _():
        st[...] = s0_ref[...]

    ii = lax.broadcasted_iota(jnp.int32, (c, c), 0)
    jj = lax.broadcasted_iota(jnp.int32, (c, c), 1)
    lower = (ii >= jj)
    lower_f = lower.astype(F32)
    rid = lax.broadcasted_iota(jnp.int32, (c, 1), 0)
    rowmask = (rid < valid) if valid < c else None
    n_sub = max(c // 16, 1)
    sub = c // n_sub

    n_chunks = tb_rows // c
    cpi = next(k for k in (GLA_CHUNKS_PER_TRIP, 2, 1) if n_chunks % k == 0)
    p1_units = [(g, h) for g in range(nb * cpi) for h in range(n_heads)]

    def phase1(ci, carry):
        rows, slots, bcs, bc_ts = [], [], [], []
        for g in range(nb * cpi):
            chunk_idx = ci * cpi + g % cpi
            rr = pl.ds(_chunk_rows(g // cpi, tb_rows, chunk_idx, c), c)
            slots.append((g // cpi) * n_chunks + chunk_idx)
            gk = jax.nn.log_sigmoid(_dot(sm_ref[rr, :], wgk_ref[...]) + bgk_ref[...]) / GLA_GATE_NORMALIZER
            if rowmask is not None:
                gk = jnp.where(rowmask, gk, 0.0)
            bc = _dot_hi(lower_f, gk)
            rows.append(rr)
            bcs.append(bc)
            bc_ts.append(bc.T)
        q, k, v, bch = {}, {}, {}, {}
        for (s, h) in p1_units:
            ks = slice(h * GLA_DK, (h + 1) * GLA_DK)
            vs = slice(h * GLA_DV, (h + 1) * GLA_DV)
            q[s, h] = q_ref[rows[s], ks] * (GLA_DK ** -0.5)
            kk = k_ref[rows[s], ks]
            vv = v_ref[rows[s], vs]
            if rowmask is not None:
                kk = jnp.where(rowmask, kk, 0.0)
                vv = jnp.where(rowmask, vv, 0.0)
            k[s, h], v[s, h] = kk, vv
            bch[s, h] = bcs[s][:, ks]
        for (g, h) in p1_units:
            qes[h, rows[g], :] = q[g, h] * jnp.exp(bch[g, h])
        a = {}
        for u in p1_units:
            q_parts, k_parts = [], []
            for sb in range(n_sub):
                ref_row = bch[u][sb * sub:sb * sub + 1, :]
                in_blk = (rid >= sb * sub) & (rid < (sb + 1) * sub)
                q_parts.append(jnp.where(in_blk, q[u] * jnp.exp(jnp.where(in_blk, bch[u] - ref_row, 0.0)), 0.0))
                k_parts.append(k[u] * jnp.exp(jnp.where(rid < (sb + 1) * sub, ref_row - bch[u], 0.0)))
            q_hat = jnp.concatenate(q_parts, axis=1) if n_sub > 1 else q_parts[0]
            k_hat = jnp.concatenate(k_parts, axis=1) if n_sub > 1 else k_parts[0]
            a[u] = jnp.where(lower, _dot_nt(q_hat, k_hat), 0.0)
        upd = {u: _dot_tn(k[u] * jnp.exp(bch[u][c - 1:c, :] - bch[u]), v[u]) for u in p1_units}
        o_intra = {u: _dot(a[u], v[u]) for u in p1_units}
        for (g, h) in p1_units:
            dec_col = bc_ts[g][h * GLA_DK:(h + 1) * GLA_DK, c - 1:c]
            decs[slots[g], h] = jnp.broadcast_to(jnp.exp(dec_col), (GLA_DK, GLA_DV))
            upds[slots[g], h] = upd[g, h]
            ois[rows[g], h * GLA_DV:(h + 1) * GLA_DV] = o_intra[g, h]
        return carry

    def phase2(ci, carry):
        rows = [pl.ds(_chunk_rows(s, tb_rows, ci, c), c) for s in range(nb)]
        o = {(s, h): ois[rows[s], h * GLA_DV:(h + 1) * GLA_DV] + _dot(qes[h, rows[s], :], st[s, h])
             for (s, h) in units}
        for (s, h) in units:
            st[s, h] = decs[s * n_chunks + ci, h] * st[s, h] + upds[s * n_chunks + ci, h]
        for s in range(nb):
            o_ref[rows[s], :] = jnp.concatenate(
                [_gated_norm(o[s, h], nw_ref[...], go_ref[rows[s], h * GLA_DV:(h + 1) * GLA_DV])
                 for h in range(n_heads)], axis=1)
        return carry

    _for_chunks(n_chunks // cpi, phase1)
    _for_chunks(n_chunks, phase2)

    @pl.when(tb == n_tb - 1)
    def _():
        sout_ref[...] = st[...]


def _gla(proj, n_seq, nb, t_len, tb_rows, chunk, valid, s0, wgk, bgk, nw):
    n_tb = t_len // tb_rows
    assert nb == 1 or n_tb == 1
    rows = nb * tb_rows

    def rowblk(b, t):
        return b * n_tb + t

    body = functools.partial(_gla_body, nb=nb, tb_rows=tb_rows, chunk=chunk, valid=valid, n_tb=n_tb)
    return pl.pallas_call(
        body,
        grid=(n_seq // nb, n_tb),
        in_specs=[
            pl.BlockSpec((rows, GLA_QK_W), lambda b, t: (rowblk(b, t), COL_GQ // GLA_QK_W)),
            pl.BlockSpec((rows, GLA_QK_W), lambda b, t: (rowblk(b, t), COL_GK // GLA_QK_W)),
            pl.BlockSpec((rows, GLA_V_W), lambda b, t: (rowblk(b, t), COL_GV // GLA_V_W)),
            pl.BlockSpec((rows, GLA_V_W), lambda b, t: (rowblk(b, t), COL_GG // GLA_V_W)),
            pl.BlockSpec((rows, SM_W), lambda b, t: (rowblk(b, t), COL_SM // SM_W)),
            pl.BlockSpec((nb, GLA_HEADS, GLA_DK, GLA_DV), lambda b, t: (b, 0, 0, 0)),
            pl.BlockSpec((SM_W, GLA_QK_W), lambda b, t: (0, 0)),
            pl.BlockSpec((1, GLA_QK_W), lambda b, t: (0, 0)),
            pl.BlockSpec((1, GLA_DV), lambda b, t: (0, 0)),
        ],
        out_specs=[
            pl.BlockSpec((rows, GLA_V_W), lambda b, t: (rowblk(b, t), 0)),
            pl.BlockSpec((nb, GLA_HEADS, GLA_DK, GLA_DV), lambda b, t: (b, 0, 0, 0)),
        ],
        out_shape=[
            jax.ShapeDtypeStruct((n_seq * t_len, GLA_V_W), F32),
            jax.ShapeDtypeStruct((n_seq, GLA_HEADS, GLA_DK, GLA_DV), F32),
        ],
        scratch_shapes=[
            pltpu.VMEM((nb, GLA_HEADS, GLA_DK, GLA_DV), F32),
            pltpu.VMEM((GLA_HEADS, rows, GLA_DK), F32),
            pltpu.VMEM((rows, GLA_V_W), F32),
            pltpu.VMEM((rows // chunk, GLA_HEADS, GLA_DK, GLA_DV), F32),
            pltpu.VMEM((rows // chunk, GLA_HEADS, GLA_DK, GLA_DV), F32),
        ],
        compiler_params=pltpu.CompilerParams(dimension_semantics=("arbitrary", "arbitrary"),
                                             vmem_limit_bytes=VMEM_LIMIT),
        name="gla_mixer",
    )(proj, proj, proj, proj, proj, s0, wgk, bgk, nw)


def _outproj_body(ogp_ref, ogs_ref, olp_ref, ols_ref, xp_ref, xs_ref, wo_ref, g_ref, wr_ref, br_ref,
                  x1_ref, h2_ref, rt_ref, rtt_ref, cnt_ref, base, *, n_p_blocks):
    i = pl.program_id(0)

    @pl.when(i == 0)
    def _():
        base[...] = jnp.zeros_like(base)

    o = jnp.concatenate([_group_pick(i, n_p_blocks, ogp_ref, ogs_ref),
                         _group_pick(i, n_p_blocks, olp_ref, ols_ref)], axis=1)
    x1 = _group_pick(i, n_p_blocks, xp_ref, xs_ref) + jnp.dot(o.astype(BF16), wo_ref[...],
                                                               preferred_element_type=F32)
    x1_ref[...] = x1
    h = _rms(x1, g_ref[...])
    _store_token_tiles(h2_ref, _pack_bf16_pairs(h))
    logits = _dot_3pass(h, wr_ref[...]) + br_ref[...]

    tm = logits.shape[0]
    lt = logits.T[:N_EXPERTS]
    eid = lax.broadcasted_iota(jnp.int32, (N_EXPERTS, tm), 0)
    work = lt
    sel = jnp.zeros((N_EXPERTS, tm), F32)
    hits, ids, vals = [], [], []
    for _ in range(TOP_K):
        m = jnp.max(work, axis=0, keepdims=True)
        idx = jnp.min(jnp.where(work == m, eid, N_EXPERTS), axis=0, keepdims=True)
        hit = eid == idx
        hits.append(hit)
        ids.append(idx)
        vals.append(m)
        work = jnp.where(hit, -jnp.inf, work)
        sel = sel + hit.astype(F32)
    exps = [jnp.exp(v - vals[0]) for v in vals]
    den = exps[0]
    for e in exps[1:]:
        den = den + e
    gates = [e / den for e in exps]

    ri = lax.broadcasted_iota(jnp.int32, (tm, tm), 0)
    ci = lax.broadcasted_iota(jnp.int32, (tm, tm), 1)
    before = _dot(sel, (ri < ci).astype(F32)) + base[...]
    ranks = [jnp.sum(jnp.where(hit, before, 0.0), axis=0, keepdims=True) for hit in hits]
    base[...] = base[...] + jnp.sum(sel, axis=1, keepdims=True)
    cnt_ref[...] = base[...]

    row = lax.broadcasted_iota(jnp.int32, (LANE, tm), 0)
    rec = jnp.zeros((LANE, tm), F32)
    for k in range(TOP_K):
        rec = jnp.where(row == k, ids[k].astype(F32), rec)
        rec = jnp.where(row == TOP_K + k, ranks[k], rec)
        rec = jnp.where(row == 2 * TOP_K + k, gates[k], rec)
    rt_ref[...] = rec.T
    rtt_ref[...] = rec[:2 * TOP_K]


def _outproj(og_p, og_s, ol_p, ol_s, x_p, x_s, wo, g, wr, br):
    n_p_blocks, n_s_blocks = x_p.shape[0] // ROW_TILE, x_s.shape[0] // ROW_TILE
    n = x_p.shape[0] + x_s.shape[0]
    return pl.pallas_call(
        functools.partial(_outproj_body, n_p_blocks=n_p_blocks),
        grid=(n_p_blocks + n_s_blocks,),
        in_specs=_group_specs(ROW_TILE, GDN_V_W, n_p_blocks) + _group_specs(ROW_TILE, GLA_V_W, n_p_blocks)
        + _group_specs(ROW_TILE, D_MODEL, n_p_blocks) + [
            pl.BlockSpec((D_MODEL, D_MODEL), lambda i: (0, 0)),
            pl.BlockSpec((1, D_MODEL), lambda i: (0, 0)),
            pl.BlockSpec((D_MODEL, LANE), lambda i: (0, 0)),
            pl.BlockSpec((1, LANE), lambda i: (0, 0)),
        ],
        out_specs=[
            pl.BlockSpec((ROW_TILE, D_MODEL), lambda i: (i, 0)),
            pl.BlockSpec((ROW_TILE * PACK_TILES, LANE), lambda i: (i, 0)),
            pl.BlockSpec((ROW_TILE, LANE), lambda i: (i, 0)),
            pl.BlockSpec((2 * TOP_K, ROW_TILE), lambda i: (0, i)),
            pl.BlockSpec((N_EXPERTS, 1), lambda i: (0, 0)),
        ],
        out_shape=[
            jax.ShapeDtypeStruct((n, D_MODEL), F32),
            jax.ShapeDtypeStruct((n * PACK_TILES, LANE), jnp.uint32),
            jax.ShapeDtypeStruct((n, LANE), F32),
            jax.ShapeDtypeStruct((2 * TOP_K, n), F32),
            jax.ShapeDtypeStruct((N_EXPERTS, 1), F32),
        ],
        scratch_shapes=[pltpu.VMEM((N_EXPERTS, 1), F32)],
        compiler_params=pltpu.CompilerParams(dimension_semantics=("arbitrary",),
                                             vmem_limit_bytes=VMEM_LIMIT),
        name="out_proj",
    )(og_p, og_s, ol_p, ol_s, x_p, x_s, wo, g, wr, br)


def _store_token_tiles(ref2d, val):
    rows, tiles = val.shape[0], val.shape[1] // LANE
    for c in range(tiles):
        ref2d[pl.ds(c, rows, stride=tiles), :] = val[:, c * LANE:(c + 1) * LANE]


def _load_token_tiles(ref2d, first_row, rows, tiles=TOK_TILES):
    return jnp.concatenate(
        [ref2d[pl.ds(first_row * tiles + c, rows, stride=tiles), :] for c in range(tiles)], axis=1)


def _pack_bf16_pairs(x):
    half = x.shape[1] // 2
    bits = lax.bitcast_convert_type(x.astype(BF16).astype(F32), jnp.uint32)
    return (bits[:, :half] >> 16) | (bits[:, half:] & jnp.uint32(0xFFFF0000))


def _unpack_bf16_pairs(w):
    lo = lax.bitcast_convert_type(w << 16, F32)
    hi = lax.bitcast_convert_type(w & jnp.uint32(0xFFFF0000), F32)
    return jnp.concatenate([lo, hi], axis=1).astype(BF16)


def _expert_weight_copies(e, ws, wup_hbm, wdn_hbm, wup_buf, wdn_buf, wsems):
    return (pltpu.make_async_copy(wup_hbm.at[e], wup_buf.at[ws], wsems.at[ws]),
            pltpu.make_async_copy(wdn_hbm.at[e], wdn_buf.at[ws], wsems.at[ws]))


def _expert_body(be_ref, nu_ref, first_ref, wslot_ref, next_ref, x_ref,
                 wup_hbm, bup_ref, wdn_hbm, bdn_ref, y_ref, wup_buf, wdn_buf, wsems, wup_bf, wdn_bf):
    i = pl.program_id(0)
    n_used = nu_ref[0]
    ws = wslot_ref[i]
    weight_copies = functools.partial(_expert_weight_copies, wup_hbm=wup_hbm, wdn_hbm=wdn_hbm, wup_buf=wup_buf,
                                      wdn_buf=wdn_buf, wsems=wsems)

    @pl.when((i == 0) & (n_used > 0))
    def _():
        for cp in weight_copies(be_ref[0], ws):
            cp.start(priority=EXPERT_WEIGHT_QUEUE)

    @pl.when(i < n_used)
    def _():
        @pl.when(first_ref[i] == 1)
        def _():
            for cp in weight_copies(be_ref[i], ws):
                cp.wait()

            @pl.when(next_ref[i] >= 0)
            def _():
                for cp in weight_copies(next_ref[i], 1 - ws):
                    cp.start(priority=EXPERT_WEIGHT_QUEUE)

            for r in range(0, D_MODEL, WEIGHT_CAST_ROWS):
                wup_bf[r:r + WEIGHT_CAST_ROWS, :] = wup_buf[ws, r:r + WEIGHT_CAST_ROWS, :].astype(BF16)
            for r in range(0, D_FF, WEIGHT_CAST_ROWS):
                wdn_bf[r:r + WEIGHT_CAST_ROWS, :] = wdn_buf[ws, r:r + WEIGHT_CAST_ROWS, :].astype(BF16)

        x = _unpack_bf16_pairs(_load_token_tiles(x_ref, 0, EXPERT_ROWS, PACK_TILES))
        gu = _dot(x, wup_bf[...]) + bup_ref[...]
        gate = jnp.minimum(gu[:, :D_FF], SWIGLU_LIMIT)
        up = jnp.clip(gu[:, D_FF:], -SWIGLU_LIMIT, SWIGLU_LIMIT)
        a = (up + 1.0) * gate * jax.nn.sigmoid(SWIGLU_ALPHA * gate)
        _store_token_tiles(y_ref, _dot(a, wdn_bf[...]) + bdn_ref[...])

    @pl.when(i >= n_used)
    def _():
        y_ref[...] = jnp.zeros_like(y_ref)


def _experts(block_meta, xs_2d, w_up, b_up, w_down, b_down):
    n_blocks = block_meta[0].shape[0]
    grid_spec = pltpu.PrefetchScalarGridSpec(
        num_scalar_prefetch=len(block_meta),
        grid=(n_blocks,),
        in_specs=[
            pl.BlockSpec((EXPERT_ROWS * PACK_TILES, LANE), lambda i, *_: (i, 0)),
            pl.BlockSpec(memory_space=pl.ANY),
            pl.BlockSpec((None, 1, 2 * D_FF), lambda i, be, *_: (be[i], 0, 0)),
            pl.BlockSpec(memory_space=pl.ANY),
            pl.BlockSpec((None, 1, D_MODEL), lambda i, be, *_: (be[i], 0, 0)),
        ],
        out_specs=pl.BlockSpec((EXPERT_ROWS * TOK_TILES, LANE), lambda i, *_: (i, 0)),
        scratch_shapes=[
            pltpu.VMEM((2, D_MODEL, 2 * D_FF), F32),
            pltpu.VMEM((2, D_FF, D_MODEL), F32),
            pltpu.SemaphoreType.DMA((2,)),
            pltpu.VMEM((D_MODEL, 2 * D_FF), BF16),
            pltpu.VMEM((D_FF, D_MODEL), BF16),
        ],
    )
    return pl.pallas_call(
        _expert_body,
        grid_spec=grid_spec,
        out_shape=jax.ShapeDtypeStruct((n_blocks * EXPERT_ROWS * TOK_TILES, LANE), F32),
        compiler_params=pltpu.CompilerParams(dimension_semantics=("arbitrary",),
                                             vmem_limit_bytes=VMEM_LIMIT),
        name="experts",
    )(*block_meta, xs_2d, w_up, b_up.reshape(N_EXPERTS, 1, 2 * D_FF), w_down,
      b_down.reshape(N_EXPERTS, 1, D_MODEL))


def _dispatch(h_tiles, dest_kmajor, n_rows):
    n_tok = h_tiles.shape[0]
    info = plsc.get_sparse_core_info()
    n_workers = info.num_cores * info.num_subcores
    per_worker = n_tok // n_workers
    chunk = next(c for c in (128, 96, 88, 64, 48, 32, 16, 8) if per_worker % c == 0)
    assert n_tok % n_workers == 0 and per_worker % SUBLANE == 0
    mesh = plsc.VectorSubcoreMesh(core_axis_name="c", subcore_axis_name="s")

    @functools.partial(
        pl.kernel, mesh=mesh,
        out_type=jax.ShapeDtypeStruct((n_rows,) + h_tiles.shape[1:], h_tiles.dtype),
        scratch_types=[pltpu.VMEM((TOP_K, chunk), jnp.int32), pltpu.VMEM((chunk,) + h_tiles.shape[1:], h_tiles.dtype),
                       pltpu.SemaphoreType.DMA],
    )
    def dispatch(h_hbm, dest_hbm, out_hbm, idx_v, rows_v, sem):
        wid = lax.axis_index("s") * info.num_cores + lax.axis_index("c")

        def step(j, carry):
            t0 = pl.multiple_of(wid * per_worker + j * chunk, SUBLANE)
            loads = [pltpu.async_copy(h_hbm.at[pl.ds(t0, chunk)], rows_v, sem)]
            for k in range(TOP_K):
                loads.append(pltpu.async_copy(
                    dest_hbm.at[pl.ds(pl.multiple_of(k * n_tok + t0, SUBLANE), chunk)], idx_v.at[k], sem))
            for cp in loads:
                cp.wait()
            stores = [pltpu.async_copy(rows_v, out_hbm.at[idx_v.at[k]], sem) for k in range(TOP_K)]
            for cp in stores:
                cp.wait()
            return carry

        lax.fori_loop(0, per_worker // chunk, step, 0)

    return dispatch(h_tiles, dest_kmajor)


def _gather_rows(src_tiles, idx_ref, n_rows, dst2d, sem, priorities):
    def issue(j, carry):
        for u in range(DMA_ISSUE_UNROLL):
            r = j * DMA_ISSUE_UNROLL + u
            dst = dst2d.at[pl.ds(pl.multiple_of(r * TOK_TILES, TOK_TILES), TOK_TILES), :]
            pltpu.make_async_copy(src_tiles.at[idx_ref[0, r]], dst, sem).start(
                priority=priorities[u % len(priorities)])
        return carry

    lax.fori_loop(0, n_rows // DMA_ISSUE_UNROLL, issue, 0)


def _wait_rows(src2d, n_rows, dst2d, sem):
    pltpu.make_async_copy(src2d.at[pl.ds(0, n_rows * TOK_TILES), :], dst2d, sem).wait()


def _combine_body(dest_ref, dest_next_ref, rt_ref, y_tiles, y_2d, x1_ref, g_ref, op_ref, os_ref, ybuf, sems,
                  *, n_p_blocks):
    i = pl.program_id(0)
    slot = i % 2
    n_rows = TOP_K * COMBINE_ROWS

    @pl.when(i == 0)
    def _():
        _gather_rows(y_tiles, dest_ref, n_rows, ybuf.at[0], sems.at[0], COMBINE_GATHER_QUEUES)

    _wait_rows(y_2d, n_rows, ybuf.at[slot], sems.at[slot])

    @pl.when(i + 1 < pl.num_programs(0))
    def _():
        _gather_rows(y_tiles, dest_next_ref, n_rows, ybuf.at[1 - slot], sems.at[1 - slot], COMBINE_GATHER_QUEUES)

    buf = ybuf.at[slot]
    moe = _load_token_tiles(buf, 0, COMBINE_ROWS) * rt_ref[:, 2 * TOP_K:2 * TOP_K + 1]
    for k in range(1, TOP_K):
        moe = moe + _load_token_tiles(buf, k * COMBINE_ROWS, COMBINE_ROWS) * rt_ref[:, 2 * TOP_K + k:2 * TOP_K + k + 1]
    res = _rms(x1_ref[...] + moe, g_ref[...])

    @pl.when(i < n_p_blocks)
    def _():
        op_ref[...] = res

    @pl.when(i >= n_p_blocks)
    def _():
        os_ref[...] = res


def _combine(dest_b, rt, y_2d, x1, g, n_p):
    n = x1.shape[0]
    n_blk = n // COMBINE_ROWS
    n_p_blocks = n_p // COMBINE_ROWS
    dest_blocks = dest_b.reshape(n_blk, 1, TOP_K * COMBINE_ROWS)
    return pl.pallas_call(
        functools.partial(_combine_body, n_p_blocks=n_p_blocks),
        grid=(n_blk,),
        in_specs=[
            pl.BlockSpec((None, 1, COMBINE_ROWS * TOP_K), lambda i: (i, 0, 0), memory_space=pltpu.SMEM),
            pl.BlockSpec((None, 1, COMBINE_ROWS * TOP_K), lambda i: (jnp.minimum(i + 1, n_blk - 1), 0, 0),
                         memory_space=pltpu.SMEM),
            pl.BlockSpec((COMBINE_ROWS, LANE), lambda i: (i, 0)),
            pl.BlockSpec(memory_space=pl.ANY),
            pl.BlockSpec(memory_space=pl.ANY),
            pl.BlockSpec((COMBINE_ROWS, D_MODEL), lambda i: (i, 0)),
            pl.BlockSpec((1, D_MODEL), lambda i: (0, 0)),
        ],
        out_specs=_group_specs(COMBINE_ROWS, D_MODEL, n_p_blocks),
        out_shape=[jax.ShapeDtypeStruct((n_p, D_MODEL), F32), jax.ShapeDtypeStruct((n - n_p, D_MODEL), F32)],
        scratch_shapes=[pltpu.VMEM((2, TOP_K * COMBINE_ROWS * TOK_TILES, LANE), F32),
                        pltpu.SemaphoreType.DMA((2,))],
        compiler_params=pltpu.CompilerParams(dimension_semantics=("arbitrary",),
                                             vmem_limit_bytes=VMEM_LIMIT),
        name="combine",
    )(dest_blocks, dest_blocks, rt, y_2d.reshape(-1, TOK_TILES, LANE), y_2d, x1, g)


def _plan_body(rtt_ref, cnt_ref, dk_ref, db_ref, meta_ref, pst):
    i = pl.program_id(0)
    sh = _log2(EXPERT_ROWS)
    n_e = N_EXPERTS

    @pl.when(i == 0)
    def _():
        cnt = cnt_ref[...].astype(jnp.int32)
        padded = (((cnt + (EXPERT_ROWS - 1)) >> sh) << sh).astype(F32)
        e_r = lax.broadcasted_iota(jnp.int32, (n_e, n_e), 0)
        e_c = lax.broadcasted_iota(jnp.int32, (n_e, n_e), 1)
        p_t = jnp.broadcast_to(padded, (n_e, n_e)).T
        pend = jnp.sum(jnp.where(e_c <= e_r, p_t, 0.0), axis=1, keepdims=True)
        pst[...] = pend - padded
        has_rows = p_t > 0.0
        group = jnp.sum(jnp.where((e_c <= e_r) & has_rows, 1.0, 0.0), axis=1, keepdims=True) - 1.0
        nxt = jnp.min(jnp.where((e_c > e_r) & has_rows, e_c, n_e), axis=1, keepdims=True)
        nxt = jnp.where(nxt >= n_e, -1, nxt)

        mb = meta_ref.shape[1]
        blk = lax.broadcasted_iota(jnp.int32, (n_e, mb), 1)
        eb = lax.broadcasted_iota(jnp.int32, (n_e, mb), 0)
        first_row = (blk * EXPERT_ROWS).astype(F32)

        def expert_of(row0):
            return jnp.minimum(jnp.sum(jnp.where(pend <= row0, 1, 0), axis=0, keepdims=True), n_e - 1)

        be = expert_of(first_row)
        be_prev = expert_of(first_row - EXPERT_ROWS)
        hit = eb == be
        wslot = jnp.sum(jnp.where(hit, group, 0.0), axis=0, keepdims=True).astype(jnp.int32) & 1
        nx = jnp.sum(jnp.where(hit, nxt, 0), axis=0, keepdims=True)
        n_used = pend[n_e - 1:n_e, :].astype(jnp.int32) >> sh
        lane = lax.broadcasted_iota(jnp.int32, (1, mb), 1)
        first = (((be != be_prev) | (lane == 0)) & (lane < n_used)).astype(jnp.int32)
        row8 = lax.broadcasted_iota(jnp.int32, (SUBLANE, mb), 0)
        meta = jnp.where(row8 == 0, be, jnp.where(row8 == 1, first, jnp.where(row8 == 2, wslot,
                         jnp.where(row8 == 3, nx, n_used))))
        meta_ref[...] = meta

    tm = rtt_ref.shape[1]
    eid = lax.broadcasted_iota(jnp.int32, (n_e, tm), 0).astype(F32)
    row8 = lax.broadcasted_iota(jnp.int32, (SUBLANE, tm), 0)
    d8 = jnp.zeros((SUBLANE, tm), jnp.int32)
    for k in range(TOP_K):
        start = jnp.sum(jnp.where(eid == rtt_ref[k:k + 1, :], pst[...], 0.0), axis=0, keepdims=True)
        d8 = jnp.where(row8 == k, (start + rtt_ref[TOP_K + k:TOP_K + k + 1, :]).astype(jnp.int32), d8)
    dk_ref[...] = d8[:TOP_K]
    for b in range(tm // COMBINE_ROWS):
        db_ref[b] = d8[:TOP_K, b * COMBINE_ROWS:(b + 1) * COMBINE_ROWS]


def _plan(rtt, cnt):
    n = rtt.shape[1]
    n_rows = n * TOP_K + N_EXPERTS * EXPERT_ROWS
    n_blocks = n_rows // EXPERT_ROWS
    mb = -(-n_blocks // LANE) * LANE
    tile = max(t for t in range(COMBINE_ROWS, PLAN_TILE_MAX + 1, COMBINE_ROWS) if n % t == 0)
    dk, db, meta = pl.pallas_call(
        _plan_body,
        grid=(n // tile,),
        in_specs=[pl.BlockSpec((2 * TOP_K, tile), lambda i: (0, i)),
                  pl.BlockSpec((N_EXPERTS, 1), lambda i: (0, 0))],
        out_specs=[pl.BlockSpec((TOP_K, tile), lambda i: (0, i)),
                   pl.BlockSpec((tile // COMBINE_ROWS, TOP_K, COMBINE_ROWS), lambda i: (i, 0, 0)),
                   pl.BlockSpec((SUBLANE, mb), lambda i: (0, 0))],
        out_shape=[jax.ShapeDtypeStruct((TOP_K, n), jnp.int32),
                   jax.ShapeDtypeStruct((n // COMBINE_ROWS, TOP_K, COMBINE_ROWS), jnp.int32),
                   jax.ShapeDtypeStruct((SUBLANE, mb), jnp.int32)],
        scratch_shapes=[pltpu.VMEM((N_EXPERTS, 1), F32)],
        compiler_params=pltpu.CompilerParams(dimension_semantics=("arbitrary",)),
        name="plan",
    )(rtt, cnt)
    block_meta = (meta[0, :n_blocks], meta[4, 0:1], meta[1, :n_blocks], meta[2, :n_blocks], meta[3, :n_blocks])
    return dk, db, n_rows, block_meta


def _pad_lanes(v, width):
    return jnp.zeros((1, width), F32).at[0, :v.shape[0]].set(v.astype(F32))


def kernel(x_prompt, x_sample, state_gdn_conv, state_gdn, state_gla, rms_mix_w, w_in, conv_w, gdn_a_log,
           gdn_dt_bias, gdn_norm_w, gla_gk_w, gla_gk_b, gla_norm_w, w_out, rms_ffn_w, w_router, b_router,
           w_up, b_up, w_down, b_down, rms_final_w):
    bp, tp, d = x_prompt.shape
    bs, ts, _ = x_sample.shape
    n_p, n_s = bp * tp, bs * ts
    assert d == D_MODEL and state_gdn.shape[0] == 1, "single-layer kernel"
    assert tp >= CONV_WIDTH - 1 and ts >= CONV_WIDTH - 1, "new conv state is taken from the new tokens only"
    l = 0

    wi = w_in[l]
    a0 = GDN_CONV_CH + GDN_V_W
    g0 = a0 + 2 * GDN_HEADS
    lr0 = g0 + 2 * GLA_QK_W + 2 * GLA_V_W
    small = jnp.concatenate([wi[:, a0:a0 + 2 * GDN_HEADS], wi[:, lr0:lr0 + GLA_GATE_RANK],
                             jnp.zeros((d, SM_W - 2 * GDN_HEADS - GLA_GATE_RANK), F32)], axis=1)
    w_big = jnp.concatenate([wi[:, :a0], wi[:, g0:lr0], small], axis=1).astype(BF16)
    alog = _pad_lanes(gdn_a_log[l], SM_W)
    dtb = _pad_lanes(gdn_dt_bias[l], SM_W)
    wgk = jnp.zeros((SM_W, GLA_QK_W), F32).at[SM_LR:SM_LR + GLA_GATE_RANK].set(gla_gk_w[l])
    wr = jnp.zeros((d, LANE), F32).at[:, :N_EXPERTS].set(w_router[l])
    br = jnp.full((1, LANE), -1e30, F32).at[0, :N_EXPERTS].set(b_router[l])

    assert n_p % ROW_TILE == 0 and n_s % ROW_TILE == 0
    x_p, x_s = x_prompt.reshape(n_p, d), x_sample.reshape(n_s, d)
    proj = _inproj(x_p, x_s, rms_mix_w[l][None, :], w_big)

    tb_p = PROMPT_TIME_BLOCK
    zeros_conv = jnp.zeros((bp, CONV_WIDTH - 1, GDN_CONV_CH), F32)
    og_p, gdn_p, conv_p = _gdn(proj, bp, 1, tp, tb_p, CHUNK, CHUNK, zeros_conv,
                               jnp.zeros((bp, GDN_HEADS, GDN_DK, GDN_DV), F32), conv_w[l], alog, dtb,
                               gdn_norm_w[l][None, :])
    ol_p, gla_p = _gla(proj, bp, 1, tp, tb_p, CHUNK, CHUNK, jnp.zeros((bp, GLA_HEADS, GLA_DK, GLA_DV), F32),
                       wgk, gla_gk_b[l][None, :], gla_norm_w[l][None, :])

    ts_pad = SUBLANE
    nb_s = SAMPLE_SEQS_PER_STEP
    proj_s = proj[n_p:].reshape(bs, ts, PROJ_W)
    proj_sp = jnp.pad(proj_s, ((0, 0), (0, ts_pad - ts), (0, 0))).reshape(bs * ts_pad, PROJ_W)
    og_s, gdn_s, conv_s = _gdn(proj_sp, bs, nb_s, ts_pad, ts_pad, ts_pad, ts, state_gdn_conv[l], state_gdn[l],
                               conv_w[l], alog, dtb, gdn_norm_w[l][None, :])
    ol_s, gla_s = _gla(proj_sp, bs, nb_s, ts_pad, ts_pad, ts_pad, ts, state_gla[l], wgk, gla_gk_b[l][None, :],
                       gla_norm_w[l][None, :])
    og_s = og_s.reshape(bs, ts_pad, GDN_V_W)[:, :ts].reshape(n_s, GDN_V_W)
    ol_s = ol_s.reshape(bs, ts_pad, GLA_V_W)[:, :ts].reshape(n_s, GLA_V_W)

    x1, h2, rt, rtt, cnt = _outproj(og_p, og_s, ol_p, ol_s, x_p, x_s, w_out[l].astype(BF16),
                                    rms_ffn_w[l][None, :], wr, br)

    dest_k, dest_b, n_rows, block_meta = _plan(rtt, cnt)
    xs = _dispatch(h2.reshape(-1, PACK_TILES, LANE), dest_k.reshape(-1), n_rows)
    y_rows = _experts(block_meta, xs.reshape(-1, LANE), w_up[l], b_up[l], w_down[l], b_down[l])
    y_p, y_s = _combine(dest_b, rt, y_rows, x1, rms_final_w[None, :], n_p)
    y_prompt = y_p.reshape(bp, tp, d)
    y_sample = y_s.reshape(bs, ts, d)
    return (y_prompt, y_sample, conv_p[None], gdn_p[None], gla_p[None], conv_s[None], gdn_s[None], gla_s[None])
```

```python
import functools

import jax
import jax.numpy as jnp
from jax import lax
from jax.experimental import pallas as pl
from jax.experimental.pallas import tpu as pltpu
from jax.experimental.pallas import tpu_sc as plsc

F32 = jnp.float32
BF16 = jnp.bfloat16
HI = lax.Precision.HIGHEST

D_MODEL = 1024
GDN_HEADS = 4
GDN_DK = 128
GDN_DV = 128
GLA_HEADS = 4
GLA_DK = 64
GLA_DV = 128
GLA_GATE_RANK = 16
GLA_GATE_NORMALIZER = 16.0
CONV_WIDTH = 4
CHUNK = 64
N_EXPERTS = 32
TOP_K = 4
D_FF = 1024
SWIGLU_LIMIT = 7.0
SWIGLU_ALPHA = 1.702
RMS_EPS = 1e-6
L2_EPS = 1e-6

GDN_QK_W = GDN_HEADS * GDN_DK
GDN_V_W = GDN_HEADS * GDN_DV
GDN_CONV_CH = 2 * GDN_QK_W + GDN_V_W
GLA_QK_W = GLA_HEADS * GLA_DK
GLA_V_W = GLA_HEADS * GLA_DV

COL_QKV = 0
COL_Z = 1536
COL_GQ = 2048
COL_GK = 2304
COL_GV = 2560
COL_GG = 3072
COL_SM = 3584
SM_W = 128
PROJ_W = COL_SM + SM_W
SM_A, SM_B, SM_LR = 0, 4, 8

LANE = 128
SUBLANE = 8
TOK_TILES = D_MODEL // LANE
PACK_TILES = TOK_TILES // 2
ROW_TILE = 512
EXPERT_ROWS = 512
EXPERT_WEIGHT_QUEUE = 1
WEIGHT_CAST_ROWS = 128
COMBINE_ROWS = 256
PLAN_TILE_MAX = 2048
DMA_ISSUE_UNROLL = 8
COMBINE_GATHER_QUEUES = (0, 1)
CONV_ROW_SLAB = 128
GDN_CHUNKS_PER_TRIP = 4
GLA_CHUNKS_PER_TRIP = 4
PROMPT_TIME_BLOCK = 512
SAMPLE_SEQS_PER_STEP = 8
VMEM_LIMIT = 56 * 1024 * 1024


def _dot(a, b):
    return jnp.dot(a.astype(BF16), b.astype(BF16), preferred_element_type=F32)


def _dot_nt(a, b):
    return lax.dot_general(a.astype(BF16), b.astype(BF16), (((1,), (1,)), ((), ())),
                           preferred_element_type=F32)


def _dot_tn(a, b):
    return lax.dot_general(a.astype(BF16), b.astype(BF16), (((0,), (0,)), ((), ())),
                           preferred_element_type=F32)


def _dot_hi(a, b):
    return jnp.dot(a, b, precision=HI, preferred_element_type=F32)


def _dot_3pass(a, b):
    a_hi = a.astype(BF16)
    b_hi = b.astype(BF16)
    a_lo = (a - a_hi.astype(F32)).astype(BF16)
    b_lo = (b - b_hi.astype(F32)).astype(BF16)

    def mm(x, y):
        return jnp.dot(x, y, preferred_element_type=F32)

    return (mm(a_lo, b_hi) + mm(a_hi, b_lo)) + mm(a_hi, b_hi)


def _rms(x, w):
    return x * lax.rsqrt(jnp.mean(x * x, axis=-1, keepdims=True) + RMS_EPS) * w


def _silu(x):
    return x * jax.nn.sigmoid(x)


def _group_specs(rows, width, n_p_blocks):
    return [pl.BlockSpec((rows, width), lambda i: (jnp.minimum(i, n_p_blocks - 1), 0)),
            pl.BlockSpec((rows, width), lambda i: (jnp.maximum(i - n_p_blocks, 0), 0))]


def _group_pick(i, n_p_blocks, p_ref, s_ref):
    return jnp.where(i < n_p_blocks, p_ref[...], s_ref[...])


def _inproj_body(xp_ref, xs_ref, g_ref, w_ref, o_ref, *, n_p_blocks):
    x = _group_pick(pl.program_id(0), n_p_blocks, xp_ref, xs_ref)
    h = _rms(x, g_ref[...])
    o_ref[...] = jnp.dot(h.astype(BF16), w_ref[...], preferred_element_type=F32)


def _inproj(x_p, x_s, g, w):
    n_p_blocks, n_s_blocks = x_p.shape[0] // ROW_TILE, x_s.shape[0] // ROW_TILE
    n = x_p.shape[0] + x_s.shape[0]
    return pl.pallas_call(
        functools.partial(_inproj_body, n_p_blocks=n_p_blocks),
        grid=(n_p_blocks + n_s_blocks,),
        in_specs=_group_specs(ROW_TILE, D_MODEL, n_p_blocks) + [
            pl.BlockSpec((1, D_MODEL), lambda i: (0, 0)),
            pl.BlockSpec((D_MODEL, PROJ_W), lambda i: (0, 0)),
        ],
        out_specs=pl.BlockSpec((ROW_TILE, PROJ_W), lambda i: (i, 0)),
        out_shape=jax.ShapeDtypeStruct((n, PROJ_W), F32),
        compiler_params=pltpu.CompilerParams(dimension_semantics=("arbitrary",),
                                             vmem_limit_bytes=VMEM_LIMIT),
        name="in_proj",
    )(x_p, x_s, g, w)


def _log2(n):
    assert n & (n - 1) == 0
    return n.bit_length() - 1


def _tri_inv_all(ms, c, ii, jj):
    eye = (ii == jj).astype(F32)
    base = min(c, 8)
    sh = _log2(base)
    blk = (ii >> sh) == (jj >> sh)
    ns = [jnp.where(blk, m, 0.0) for m in ms]
    xs = [eye - n for n in ns]
    ps = [_dot(n, n) for n in ns]
    ts = [_dot(jnp.concatenate([x, p], axis=0), p) for x, p in zip(xs, ps)]
    xs = [x + t[:c] for x, t in zip(xs, ts)]
    ps = [t[c:] for t in ts]
    xs = [x + _dot(x, p) for x, p in zip(xs, ps)]
    s = base
    while s < c:
        sh_s, sh_b = _log2(s), _log2(2 * s)
        off = ((ii >> sh_b) == (jj >> sh_b)) & ((ii >> sh_s) != (jj >> sh_s))
        ys = [_dot(x, jnp.where(off, m, 0.0)) for x, m in zip(xs, ms)]
        xs = [x - _dot(y, x) for x, y in zip(xs, ys)]
        s *= 2
    return xs


def _gated_norm(o, w, z):
    return o * lax.rsqrt(jnp.mean(o * o, axis=-1, keepdims=True) + RMS_EPS) * w * _silu(z)


def _chunk_rows(s, tb_rows, ci, c):
    r = s * tb_rows + ci * c
    if not isinstance(r, int):
        r = pl.multiple_of(r, c)
    return r


def _for_chunks(n_chunks, step):
    if n_chunks == 1:
        step(0, 0)
    else:
        lax.fori_loop(0, n_chunks, step, 0)


def _gdn_body(qkv_ref, z_ref, sm_ref, cbuf_ref, s0_ref, cw_ref, alog_ref, dtb_ref, nw_ref,
              o_ref, sout_ref, cout_ref, st, xc, act, gcs, us, wss, qgs, kds, aqs,
              *, nb, tb_rows, chunk, valid, n_tb):
    tb = pl.program_id(1)
    c = chunk
    n_heads = GDN_HEADS
    tail = CONV_WIDTH - 1
    pad = SUBLANE
    units = [(s, h) for s in range(nb) for h in range(n_heads)]

    n_slabs = GDN_CONV_CH // LANE

    def lanes(j):
        return slice(j * LANE, (j + 1) * LANE)

    @pl.when(tb == 0)
    def _():
        st[...] = s0_ref[...]
        for s in range(nb):
            for j in range(n_slabs):
                xc[s, j, pad - tail:pad, :] = cbuf_ref[s, :, lanes(j)]

    if n_tb > 1:
        @pl.when(tb > 0)
        def _():
            for s in range(nb):
                for j in range(n_slabs):
                    xc[s, j, pad - tail:pad, :] = xc[s, j, tb_rows + pad - tail:tb_rows + pad, :]

    for s in range(nb):
        for j in range(n_slabs):
            xc[s, j, pad:pad + tb_rows, :] = qkv_ref[s * tb_rows:(s + 1) * tb_rows, lanes(j)]

    row_slab = min(tb_rows, CONV_ROW_SLAB)
    parities = 2 if row_slab >= 2 * SUBLANE else 1
    for s in range(nb):
        for j in range(n_slabs):
            src, dst = xc.at[s, j], act.at[j]
            for sl in range(tb_rows // row_slab):
                for p in range(parities):
                    lo = pad - tail + sl * row_slab + p
                    out0 = s * tb_rows + sl * row_slab + p

                    def rows_from(start):
                        if parities == 1:
                            return pl.ds(start, row_slab)
                        return pl.ds(start, row_slab // 2, stride=2)

                    acc = src[rows_from(lo), :] * cw_ref[0:1, lanes(j)]
                    for i in range(1, CONV_WIDTH):
                        acc = acc + src[rows_from(lo + i), :] * cw_ref[i:i + 1, lanes(j)]
                    dst[rows_from(out0), :] = _silu(acc)

    ii = lax.broadcasted_iota(jnp.int32, (c, c), 0)
    jj = lax.broadcasted_iota(jnp.int32, (c, c), 1)
    lower = (ii >= jj)
    lower_f = lower.astype(F32)
    strict = (ii > jj)
    rowmask = None
    if valid < c:
        rowmask = lax.broadcasted_iota(jnp.int32, (c, 1), 0) < valid

    def hs(h, w):
        return slice(h * w, (h + 1) * w)

    n_chunks = tb_rows // c
    cpi = next(k for k in (GDN_CHUNKS_PER_TRIP, 2, 1) if n_chunks % k == 0)
    p1_units = [(g, h) for g in range(nb * cpi) for h in range(n_heads)]

    def phase1(ci, carry):
        rows, b_ts, gc_ts, gc_tts = [], [], [], []
        for g in range(nb * cpi):
            rr = pl.ds(_chunk_rows(g // cpi, tb_rows, ci * cpi + g % cpi, c), c)
            sm = sm_ref[rr, :]
            g_t = -jnp.exp(alog_ref[...]) * jax.nn.softplus(sm + dtb_ref[...])
            b_t = jax.nn.sigmoid(sm)
            if rowmask is not None:
                g_t = jnp.where(rowmask, g_t, 0.0)
                b_t = jnp.where(rowmask, b_t, 0.0)
            gc_t = _dot_hi(lower_f, g_t)
            gcs[rr, :] = gc_t
            rows.append(rr)
            b_ts.append(b_t)
            gc_ts.append(gc_t)
            gc_tts.append(gc_t.T)
        qn, kn, kb, vb = {}, {}, {}, {}
        for (s, h) in p1_units:
            q = act[h, rows[s], :]
            k = act[n_heads + h, rows[s], :]
            v = act[2 * n_heads + h, rows[s], :]
            if rowmask is not None:
                q = jnp.where(rowmask, q, 0.0)
                k = jnp.where(rowmask, k, 0.0)
                v = jnp.where(rowmask, v, 0.0)
            qn[s, h] = q * lax.rsqrt(jnp.sum(q * q, axis=-1, keepdims=True) + L2_EPS) * (GDN_DK ** -0.5)
            kn[s, h] = k * lax.rsqrt(jnp.sum(k * k, axis=-1, keepdims=True) + L2_EPS)
            beta = b_ts[s][:, SM_B + h:SM_B + h + 1]
            kb[s, h] = kn[s, h] * beta
            vb[s, h] = v * beta
        s1 = {u: _dot_nt(jnp.concatenate([kb[u], qn[u]], axis=0), kn[u]) for u in p1_units}
        mm = []
        for (s, h) in p1_units:
            gcol = gc_ts[s][:, SM_A + h:SM_A + h + 1]
            grow = gc_tts[s][SM_A + h:SM_A + h + 1, :]
            dec = jnp.exp(jnp.where(lower, gcol - grow, -jnp.inf))
            mm.append(jnp.where(strict, s1[s, h][:c] * dec, 0.0))
            aqs[h, rows[s], :] = s1[s, h][c:] * dec
        tms = _tri_inv_all(mm, c, ii, jj)
        for (s, h), tm in zip(p1_units, tms):
            gcol = gc_ts[s][:, SM_A + h:SM_A + h + 1]
            eg = jnp.exp(gcol)
            uw = _dot(tm, jnp.concatenate([vb[s, h], kb[s, h] * eg], axis=1))
            us[rows[s], hs(h, GDN_DV)] = uw[:, :GDN_DV]
            wss[rows[s], hs(h, GDN_DV)] = uw[:, GDN_DV:]
            qgs[rows[s], hs(h, GDN_DK)] = qn[s, h] * eg
            kds[rows[s], hs(h, GDN_DK)] = kn[s, h] * jnp.exp(gcol[c - 1:c, :] - gcol)
        return carry

    def phase2(ci, carry):
        r0 = [_chunk_rows(s, tb_rows, ci, c) for s in range(nb)]
        rows = [pl.ds(r, c) for r in r0]
        ws = {(s, h): _dot(jnp.concatenate([wss[rows[s], hs(h, GDN_DV)], qgs[rows[s], hs(h, GDN_DK)]], axis=0),
                           st[s, h]) for (s, h) in units}
        v_new = {(s, h): us[rows[s], hs(h, GDN_DV)] - ws[s, h][:c] for (s, h) in units}
        o = {(s, h): ws[s, h][c:] + _dot(aqs[h, rows[s], :], v_new[s, h]) for (s, h) in units}
        upd = {(s, h): _dot_tn(kds[rows[s], hs(h, GDN_DK)], v_new[s, h]) for (s, h) in units}
        for (s, h) in units:
            g_last = gcs[pl.ds(r0[s] + c - 1, 1), SM_A + h:SM_A + h + 1]
            st[s, h] = st[s, h] * jnp.exp(g_last) + upd[s, h]
        for s in range(nb):
            o_ref[rows[s], :] = jnp.concatenate(
                [_gated_norm(o[s, h], nw_ref[...], z_ref[rows[s], hs(h, GDN_DV)]) for h in range(n_heads)], axis=1)
        return carry

    _for_chunks(n_chunks // cpi, phase1)
    _for_chunks(n_chunks, phase2)

    @pl.when(tb == n_tb - 1)
    def _():
        sout_ref[...] = st[...]
        last = tb_rows if valid == c else valid
        for s in range(nb):
            cout_ref[s] = jnp.concatenate(
                [xc[s, j, pad + last - tail:pad + last, :] for j in range(n_slabs)], axis=1)


def _gdn(proj, n_seq, nb, t_len, tb_rows, chunk, valid, conv_buf, s0, conv_w, alog, dtb, nw):
    n_tb = t_len // tb_rows
    assert nb == 1 or n_tb == 1
    rows = nb * tb_rows

    def rowblk(b, t):
        return b * n_tb + t

    body = functools.partial(_gdn_body, nb=nb, tb_rows=tb_rows, chunk=chunk, valid=valid, n_tb=n_tb)
    return pl.pallas_call(
        body,
        grid=(n_seq // nb, n_tb),
        in_specs=[
            pl.BlockSpec((rows, GDN_CONV_CH), lambda b, t: (rowblk(b, t), COL_QKV // GDN_CONV_CH)),
            pl.BlockSpec((rows, GDN_V_W), lambda b, t: (rowblk(b, t), COL_Z // GDN_V_W)),
            pl.BlockSpec((rows, SM_W), lambda b, t: (rowblk(b, t), COL_SM // SM_W)),
            pl.BlockSpec((nb, CONV_WIDTH - 1, GDN_CONV_CH), lambda b, t: (b, 0, 0)),
            pl.BlockSpec((nb, GDN_HEADS, GDN_DK, GDN_DV), lambda b, t: (b, 0, 0, 0)),
            pl.BlockSpec((CONV_WIDTH, GDN_CONV_CH), lambda b, t: (0, 0)),
            pl.BlockSpec((1, SM_W), lambda b, t: (0, 0)),
            pl.BlockSpec((1, SM_W), lambda b, t: (0, 0)),
            pl.BlockSpec((1, GDN_DV), lambda b, t: (0, 0)),
        ],
        out_specs=[
            pl.BlockSpec((rows, GDN_V_W), lambda b, t: (rowblk(b, t), 0)),
            pl.BlockSpec((nb, GDN_HEADS, GDN_DK, GDN_DV), lambda b, t: (b, 0, 0, 0)),
            pl.BlockSpec((nb, CONV_WIDTH - 1, GDN_CONV_CH), lambda b, t: (b, 0, 0)),
        ],
        out_shape=[
            jax.ShapeDtypeStruct((n_seq * t_len, GDN_V_W), F32),
            jax.ShapeDtypeStruct((n_seq, GDN_HEADS, GDN_DK, GDN_DV), F32),
            jax.ShapeDtypeStruct((n_seq, CONV_WIDTH - 1, GDN_CONV_CH), F32),
        ],
        scratch_shapes=[
            pltpu.VMEM((nb, GDN_HEADS, GDN_DK, GDN_DV), F32),
            pltpu.VMEM((nb, GDN_CONV_CH // LANE, tb_rows + SUBLANE, LANE), F32),
            pltpu.VMEM((GDN_CONV_CH // LANE, rows, LANE), F32),
            pltpu.VMEM((rows, SM_W), F32),
            pltpu.VMEM((rows, GDN_V_W), F32),
            pltpu.VMEM((rows, GDN_V_W), F32),
            pltpu.VMEM((rows, GDN_QK_W), F32),
            pltpu.VMEM((rows, GDN_QK_W), F32),
            pltpu.VMEM((GDN_HEADS, rows, chunk), F32),
        ],
        compiler_params=pltpu.CompilerParams(dimension_semantics=("arbitrary", "arbitrary"),
                                             vmem_limit_bytes=VMEM_LIMIT),
        name="gdn_mixer",
    )(proj, proj, proj, conv_buf, s0, conv_w, alog, dtb, nw)


def _gla_body(q_ref, k_ref, v_ref, go_ref, sm_ref, s0_ref, wgk_ref, bgk_ref, nw_ref,
              o_ref, sout_ref, st, qes, ois, upds, decs, *, nb, tb_rows, chunk, valid, n_tb):
    tb = pl.program_id(1)
    c = chunk
    n_heads = GLA_HEADS
    units = [(s, h) for s in range(nb) for h in range(n_heads)]

    @pl.when(tb == 0)
    def _():
        st[...] = s0_ref[...]

    ii = lax.broadcasted_iota(jnp.int32, (c, c), 0)
    jj = lax.broadcasted_iota(jnp.int32, (c, c), 1)
    lower = (ii >= jj)
    lower_f = lower.astype(F32)
    rid = lax.broadcasted_iota(jnp.int32, (c, 1), 0)
    rowmask = (rid < valid) if valid < c else None
    n_sub = max(c // 16, 1)
    sub = c // n_sub

    n_chunks = tb_rows // c
    cpi = next(k for k in (GLA_CHUNKS_PER_TRIP, 2, 1) if n_chunks % k == 0)
    p1_units = [(g, h) for g in range(nb * cpi) for h in range(n_heads)]

    def phase1(ci, carry):
        rows, slots, bcs, bc_ts = [], [], [], []
        for g in range(nb * cpi):
            chunk_idx = ci * cpi + g % cpi
            rr = pl.ds(_chunk_rows(g // cpi, tb_rows, chunk_idx, c), c)
            slots.append((g // cpi) * n_chunks + chunk_idx)
            gk = jax.nn.log_sigmoid(_dot(sm_ref[rr, :], wgk_ref[...]) + bgk_ref[...]) / GLA_GATE_NORMALIZER
            if rowmask is not None:
                gk = jnp.where(rowmask, gk, 0.0)
            bc = _dot_hi(lower_f, gk)
            rows.append(rr)
            bcs.append(bc)
            bc_ts.append(bc.T)
        q, k, v, bch = {}, {}, {}, {}
        for (s, h) in p1_units:
            ks = slice(h * GLA_DK, (h + 1) * GLA_DK)
            vs = slice(h * GLA_DV, (h + 1) * GLA_DV)
            q[s, h] = q_ref[rows[s], ks] * (GLA_DK ** -0.5)
            kk = k_ref[rows[s], ks]
            vv = v_ref[rows[s], vs]
            if rowmask is not None:
                kk = jnp.where(rowmask, kk, 0.0)
                vv = jnp.where(rowmask, vv, 0.0)
            k[s, h], v[s, h] = kk, vv
            bch[s, h] = bcs[s][:, ks]
        for (g, h) in p1_units:
            qes[h, rows[g], :] = q[g, h] * jnp.exp(bch[g, h])
        a = {}
        for u in p1_units:
            q_parts, k_parts = [], []
            for sb in range(n_sub):
                ref_row = bch[u][sb * sub:sb * sub + 1, :]
                in_blk = (rid >= sb * sub) & (rid < (sb + 1) * sub)
                q_parts.append(jnp.where(in_blk, q[u] * jnp.exp(jnp.where(in_blk, bch[u] - ref_row, 0.0)), 0.0))
                k_parts.append(k[u] * jnp.exp(jnp.where(rid < (sb + 1) * sub, ref_row - bch[u], 0.0)))
            q_hat = jnp.concatenate(q_parts, axis=1) if n_sub > 1 else q_parts[0]
            k_hat = jnp.concatenate(k_parts, axis=1) if n_sub > 1 else k_parts[0]
            a[u] = jnp.where(lower, _dot_nt(q_hat, k_hat), 0.0)
        upd = {u: _dot_tn(k[u] * jnp.exp(bch[u][c - 1:c, :] - bch[u]), v[u]) for u in p1_units}
        o_intra = {u: _dot(a[u], v[u]) for u in p1_units}
        for (g, h) in p1_units:
            dec_col = bc_ts[g][h * GLA_DK:(h + 1) * GLA_DK, c - 1:c]
            decs[slots[g], h] = jnp.broadcast_to(jnp.exp(dec_col), (GLA_DK, GLA_DV))
            upds[slots[g], h] = upd[g, h]
            ois[rows[g], h * GLA_DV:(h + 1) * GLA_DV] = o_intra[g, h]
        return carry

    def phase2(ci, carry):
        rows = [pl.ds(_chunk_rows(s, tb_rows, ci, c), c) for s in range(nb)]
        o = {(s, h): ois[rows[s], h * GLA_DV:(h + 1) * GLA_DV] + _dot(qes[h, rows[s], :], st[s, h])
             for (s, h) in units}
        for (s, h) in units:
            st[s, h] = decs[s * n_chunks + ci, h] * st[s, h] + upds[s * n_chunks + ci, h]
        for s in range(nb):
            o_ref[rows[s], :] = jnp.concatenate(
                [_gated_norm(o[s, h], nw_ref[...], go_ref[rows[s], h * GLA_DV:(h + 1) * GLA_DV])
                 for h in range(n_heads)], axis=1)
        return carry

    _for_chunks(n_chunks // cpi, phase1)
    _for_chunks(n_chunks, phase2)

    @pl.when(tb == n_tb - 1)
    def _():
        sout_ref[...] = st[...]


def _gla(proj, n_seq, nb, t_len, tb_rows, chunk, valid, s0, wgk, bgk, nw):
    n_tb = t_len // tb_rows
    assert nb == 1 or n_tb == 1
    rows = nb * tb_rows

    def rowblk(b, t):
        return b * n_tb + t

    body = functools.partial(_gla_body, nb=nb, tb_rows=tb_rows, chunk=chunk, valid=valid, n_tb=n_tb)
    return pl.pallas_call(
        body,
        grid=(n_seq // nb, n_tb),
        in_specs=[
            pl.BlockSpec((rows, GLA_QK_W), lambda b, t: (rowblk(b, t), COL_GQ // GLA_QK_W)),
            pl.BlockSpec((rows, GLA_QK_W), lambda b, t: (rowblk(b, t), COL_GK // GLA_QK_W)),
            pl.BlockSpec((rows, GLA_V_W), lambda b, t: (rowblk(b, t), COL_GV // GLA_V_W)),
            pl.BlockSpec((rows, GLA_V_W), lambda b, t: (rowblk(b, t), COL_GG // GLA_V_W)),
            pl.BlockSpec((rows, SM_W), lambda b, t: (rowblk(b, t), COL_SM // SM_W)),
            pl.BlockSpec((nb, GLA_HEADS, GLA_DK, GLA_DV), lambda b, t: (b, 0, 0, 0)),
            pl.BlockSpec((SM_W, GLA_QK_W), lambda b, t: (0, 0)),
            pl.BlockSpec((1, GLA_QK_W), lambda b, t: (0, 0)),
            pl.BlockSpec((1, GLA_DV), lambda b, t: (0, 0)),
        ],
        out_specs=[
            pl.BlockSpec((rows, GLA_V_W), lambda b, t: (rowblk(b, t), 0)),
            pl.BlockSpec((nb, GLA_HEADS, GLA_DK, GLA_DV), lambda b, t: (b, 0, 0, 0)),
        ],
        out_shape=[
            jax.ShapeDtypeStruct((n_seq * t_len, GLA_V_W), F32),
            jax.ShapeDtypeStruct((n_seq, GLA_HEADS, GLA_DK, GLA_DV), F32),
        ],
        scratch_shapes=[
            pltpu.VMEM((nb, GLA_HEADS, GLA_DK, GLA_DV), F32),
            pltpu.VMEM((GLA_HEADS, rows, GLA_DK), F32),
            pltpu.VMEM((rows, GLA_V_W), F32),
            pltpu.VMEM((rows // chunk, GLA_HEADS, GLA_DK, GLA_DV), F32),
            pltpu.VMEM((rows // chunk, GLA_HEADS, GLA_DK, GLA_DV), F32),
        ],
        compiler_params=pltpu.CompilerParams(dimension_semantics=("arbitrary", "arbitrary"),
                                             vmem_limit_bytes=VMEM_LIMIT),
        name="gla_mixer",
    )(proj, proj, proj, proj, proj, s0, wgk, bgk, nw)


def _outproj_body(ogp_ref, ogs_ref, olp_ref, ols_ref, xp_ref, xs_ref, wo_ref, g_ref, wr_ref, br_ref,
                  x1_ref, h2_ref, rt_ref, rtt_ref, cnt_ref, base, *, n_p_blocks):
    i = pl.program_id(0)

    @pl.when(i == 0)
    def _():
        base[...] = jnp.zeros_like(base)

    o = jnp.concatenate([_group_pick(i, n_p_blocks, ogp_ref, ogs_ref),
                         _group_pick(i, n_p_blocks, olp_ref, ols_ref)], axis=1)
    x1 = _group_pick(i, n_p_blocks, xp_ref, xs_ref) + jnp.dot(o.astype(BF16), wo_ref[...],
                                                               preferred_element_type=F32)
    x1_ref[...] = x1
    h = _rms(x1, g_ref[...])
    _store_token_tiles(h2_ref, _pack_bf16_pairs(h))
    logits = _dot_3pass(h, wr_ref[...]) + br_ref[...]

    tm = logits.shape[0]
    lt = logits.T[:N_EXPERTS]
    eid = lax.broadcasted_iota(jnp.int32, (N_EXPERTS, tm), 0)
    work = lt
    sel = jnp.zeros((N_EXPERTS, tm), F32)
    hits, ids, vals = [], [], []
    for _ in range(TOP_K):
        m = jnp.max(work, axis=0, keepdims=True)
        idx = jnp.min(jnp.where(work == m, eid, N_EXPERTS), axis=0, keepdims=True)
        hit = eid == idx
        hits.append(hit)
        ids.append(idx)
        vals.append(m)
        work = jnp.where(hit, -jnp.inf, work)
        sel = sel + hit.astype(F32)
    exps = [jnp.exp(v - vals[0]) for v in vals]
    den = exps[0]
    for e in exps[1:]:
        den = den + e
    gates = [e / den for e in exps]

    ri = lax.broadcasted_iota(jnp.int32, (tm, tm), 0)
    ci = lax.broadcasted_iota(jnp.int32, (tm, tm), 1)
    before = _dot(sel, (ri < ci).astype(F32)) + base[...]
    ranks = [jnp.sum(jnp.where(hit, before, 0.0), axis=0, keepdims=True) for hit in hits]
    base[...] = base[...] + jnp.sum(sel, axis=1, keepdims=True)
    cnt_ref[...] = base[...]

    row = lax.broadcasted_iota(jnp.int32, (LANE, tm), 0)
    rec = jnp.zeros((LANE, tm), F32)
    for k in range(TOP_K):
        rec = jnp.where(row == k, ids[k].astype(F32), rec)
        rec = jnp.where(row == TOP_K + k, ranks[k], rec)
        rec = jnp.where(row == 2 * TOP_K + k, gates[k], rec)
    rt_ref[...] = rec.T
    rtt_ref[...] = rec[:2 * TOP_K]


def _outproj(og_p, og_s, ol_p, ol_s, x_p, x_s, wo, g, wr, br):
    n_p_blocks, n_s_blocks = x_p.shape[0] // ROW_TILE, x_s.shape[0] // ROW_TILE
    n = x_p.shape[0] + x_s.shape[0]
    return pl.pallas_call(
        functools.partial(_outproj_body, n_p_blocks=n_p_blocks),
        grid=(n_p_blocks + n_s_blocks,),
        in_specs=_group_specs(ROW_TILE, GDN_V_W, n_p_blocks) + _group_specs(ROW_TILE, GLA_V_W, n_p_blocks)
        + _group_specs(ROW_TILE, D_MODEL, n_p_blocks) + [
            pl.BlockSpec((D_MODEL, D_MODEL), lambda i: (0, 0)),
            pl.BlockSpec((1, D_MODEL), lambda i: (0, 0)),
            pl.BlockSpec((D_MODEL, LANE), lambda i: (0, 0)),
            pl.BlockSpec((1, LANE), lambda i: (0, 0)),
        ],
        out_specs=[
            pl.BlockSpec((ROW_TILE, D_MODEL), lambda i: (i, 0)),
            pl.BlockSpec((ROW_TILE * PACK_TILES, LANE), lambda i: (i, 0)),
            pl.BlockSpec((ROW_TILE, LANE), lambda i: (i, 0)),
            pl.BlockSpec((2 * TOP_K, ROW_TILE), lambda i: (0, i)),
            pl.BlockSpec((N_EXPERTS, 1), lambda i: (0, 0)),
        ],
        out_shape=[
            jax.ShapeDtypeStruct((n, D_MODEL), F32),
            jax.ShapeDtypeStruct((n * PACK_TILES, LANE), jnp.uint32),
            jax.ShapeDtypeStruct((n, LANE), F32),
            jax.ShapeDtypeStruct((2 * TOP_K, n), F32),
            jax.ShapeDtypeStruct((N_EXPERTS, 1), F32),
        ],
        scratch_shapes=[pltpu.VMEM((N_EXPERTS, 1), F32)],
        compiler_params=pltpu.CompilerParams(dimension_semantics=("arbitrary",),
                                             vmem_limit_bytes=VMEM_LIMIT),
        name="out_proj",
    )(og_p, og_s, ol_p, ol_s, x_p, x_s, wo, g, wr, br)


def _store_token_tiles(ref2d, val):
    rows, tiles = val.shape[0], val.shape[1] // LANE
    for c in range(tiles):
        ref2d[pl.ds(c, rows, stride=tiles), :] = val[:, c * LANE:(c + 1) * LANE]


def _load_token_tiles(ref2d, first_row, rows, tiles=TOK_TILES):
    return jnp.concatenate(
        [ref2d[pl.ds(first_row * tiles + c, rows, stride=tiles), :] for c in range(tiles)], axis=1)


def _pack_bf16_pairs(x):
    half = x.shape[1] // 2
    bits = lax.bitcast_convert_type(x.astype(BF16).astype(F32), jnp.uint32)
    return (bits[:, :half] >> 16) | (bits[:, half:] & jnp.uint32(0xFFFF0000))


def _unpack_bf16_pairs(w):
    lo = lax.bitcast_convert_type(w << 16, F32)
    hi = lax.bitcast_convert_type(w & jnp.uint32(0xFFFF0000), F32)
    return jnp.concatenate([lo, hi], axis=1).astype(BF16)


def _expert_weight_copies(e, ws, wup_hbm, wdn_hbm, wup_buf, wdn_buf, wsems):
    return (pltpu.make_async_copy(wup_hbm.at[e], wup_buf.at[ws], wsems.at[ws]),
            pltpu.make_async_copy(wdn_hbm.at[e], wdn_buf.at[ws], wsems.at[ws]))


def _expert_body(be_ref, nu_ref, first_ref, wslot_ref, next_ref, x_ref,
                 wup_hbm, bup_ref, wdn_hbm, bdn_ref, y_ref, wup_buf, wdn_buf, wsems, wup_bf, wdn_bf):
    i = pl.program_id(0)
    n_used = nu_ref[0]
    ws = wslot_ref[i]
    weight_copies = functools.partial(_expert_weight_copies, wup_hbm=wup_hbm, wdn_hbm=wdn_hbm, wup_buf=wup_buf,
                                      wdn_buf=wdn_buf, wsems=wsems)

    @pl.when((i == 0) & (n_used > 0))
    def _():
        for cp in weight_copies(be_ref[0], ws):
            cp.start(priority=EXPERT_WEIGHT_QUEUE)

    @pl.when(i < n_used)
    def _():
        @pl.when(first_ref[i] == 1)
        def _():
            for cp in weight_copies(be_ref[i], ws):
                cp.wait()

            @pl.when(next_ref[i] >= 0)
            def _():
                for cp in weight_copies(next_ref[i], 1 - ws):
                    cp.start(priority=EXPERT_WEIGHT_QUEUE)

            for r in range(0, D_MODEL, WEIGHT_CAST_ROWS):
                wup_bf[r:r + WEIGHT_CAST_ROWS, :] = wup_buf[ws, r:r + WEIGHT_CAST_ROWS, :].astype(BF16)
            for r in range(0, D_FF, WEIGHT_CAST_ROWS):
                wdn_bf[r:r + WEIGHT_CAST_ROWS, :] = wdn_buf[ws, r:r + WEIGHT_CAST_ROWS, :].astype(BF16)

        x = _unpack_bf16_pairs(_load_token_tiles(x_ref, 0, EXPERT_ROWS, PACK_TILES))
        gu = _dot(x, wup_bf[...]) + bup_ref[...]
        gate = jnp.minimum(gu[:, :D_FF], SWIGLU_LIMIT)
        up = jnp.clip(gu[:, D_FF:], -SWIGLU_LIMIT, SWIGLU_LIMIT)
        a = (up + 1.0) * gate * jax.nn.sigmoid(SWIGLU_ALPHA * gate)
        _store_token_tiles(y_ref, _dot(a, wdn_bf[...]) + bdn_ref[...])

    @pl.when(i >= n_used)
    def _():
        y_ref[...] = jnp.zeros_like(y_ref)


def _experts(block_meta, xs_2d, w_up, b_up, w_down, b_down):
    n_blocks = block_meta[0].shape[0]
    grid_spec = pltpu.PrefetchScalarGridSpec(
        num_scalar_prefetch=len(block_meta),
        grid=(n_blocks,),
        in_specs=[
            pl.BlockSpec((EXPERT_ROWS * PACK_TILES, LANE), lambda i, *_: (i, 0)),
            pl.BlockSpec(memory_space=pl.ANY),
            pl.BlockSpec((None, 1, 2 * D_FF), lambda i, be, *_: (be[i], 0, 0)),
            pl.BlockSpec(memory_space=pl.ANY),
            pl.BlockSpec((None, 1, D_MODEL), lambda i, be, *_: (be[i], 0, 0)),
        ],
        out_specs=pl.BlockSpec((EXPERT_ROWS * TOK_TILES, LANE), lambda i, *_: (i, 0)),
        scratch_shapes=[
            pltpu.VMEM((2, D_MODEL, 2 * D_FF), F32),
            pltpu.VMEM((2, D_FF, D_MODEL), F32),
            pltpu.SemaphoreType.DMA((2,)),
            pltpu.VMEM((D_MODEL, 2 * D_FF), BF16),
            pltpu.VMEM((D_FF, D_MODEL), BF16),
        ],
    )
    return pl.pallas_call(
        _expert_body,
        grid_spec=grid_spec,
        out_shape=jax.ShapeDtypeStruct((n_blocks * EXPERT_ROWS * TOK_TILES, LANE), F32),
        compiler_params=pltpu.CompilerParams(dimension_semantics=("arbitrary",),
                                             vmem_limit_bytes=VMEM_LIMIT),
        name="experts",
    )(*block_meta, xs_2d, w_up, b_up.reshape(N_EXPERTS, 1, 2 * D_FF), w_down,
      b_down.reshape(N_EXPERTS, 1, D_MODEL))


def _dispatch(h_tiles, dest_kmajor, n_rows):
    n_tok = h_tiles.shape[0]
    info = plsc.get_sparse_core_info()
    n_workers = info.num_cores * info.num_subcores
    per_worker = n_tok // n_workers
    chunk = next(c for c in (128, 96, 88, 64, 48, 32, 16, 8) if per_worker % c == 0)
    assert n_tok % n_workers == 0 and per_worker % SUBLANE == 0
    mesh = plsc.VectorSubcoreMesh(core_axis_name="c", subcore_axis_name="s")

    @functools.partial(
        pl.kernel, mesh=mesh,
        out_type=jax.ShapeDtypeStruct((n_rows,) + h_tiles.shape[1:], h_tiles.dtype),
        scratch_types=[pltpu.VMEM((TOP_K, chunk), jnp.int32), pltpu.VMEM((chunk,) + h_tiles.shape[1:], h_tiles.dtype),
                       pltpu.SemaphoreType.DMA],
    )
    def dispatch(h_hbm, dest_hbm, out_hbm, idx_v, rows_v, sem):
        wid = lax.axis_index("s") * info.num_cores + lax.axis_index("c")

        def step(j, carry):
            t0 = pl.multiple_of(wid * per_worker + j * chunk, SUBLANE)
            loads = [pltpu.async_copy(h_hbm.at[pl.ds(t0, chunk)], rows_v, sem)]
            for k in range(TOP_K):
                loads.append(pltpu.async_copy(
                    dest_hbm.at[pl.ds(pl.multiple_of(k * n_tok + t0, SUBLANE), chunk)], idx_v.at[k], sem))
            for cp in loads:
                cp.wait()
            stores = [pltpu.async_copy(rows_v, out_hbm.at[idx_v.at[k]], sem) for k in range(TOP_K)]
            for cp in stores:
                cp.wait()
            return carry

        lax.fori_loop(0, per_worker // chunk, step, 0)

    return dispatch(h_tiles, dest_kmajor)


def _gather_rows(src_tiles, idx_ref, n_rows, dst2d, sem, priorities):
    def issue(j, carry):
        for u in range(DMA_ISSUE_UNROLL):
            r = j * DMA_ISSUE_UNROLL + u
            dst = dst2d.at[pl.ds(pl.multiple_of(r * TOK_TILES, TOK_TILES), TOK_TILES), :]
            pltpu.make_async_copy(src_tiles.at[idx_ref[0, r]], dst, sem).start(
                priority=priorities[u % len(priorities)])
        return carry

    lax.fori_loop(0, n_rows // DMA_ISSUE_UNROLL, issue, 0)


def _wait_rows(src2d, n_rows, dst2d, sem):
    pltpu.make_async_copy(src2d.at[pl.ds(0, n_rows * TOK_TILES), :], dst2d, sem).wait()


def _combine_body(dest_ref, dest_next_ref, rt_ref, y_tiles, y_2d, x1_ref, g_ref, op_ref, os_ref, ybuf, sems,
                  *, n_p_blocks):
    i = pl.program_id(0)
    slot = i % 2
    n_rows = TOP_K * COMBINE_ROWS

    @pl.when(i == 0)
    def _():
        _gather_rows(y_tiles, dest_ref, n_rows, ybuf.at[0], sems.at[0], COMBINE_GATHER_QUEUES)

    _wait_rows(y_2d, n_rows, ybuf.at[slot], sems.at[slot])

    @pl.when(i + 1 < pl.num_programs(0))
    def _():
        _gather_rows(y_tiles, dest_next_ref, n_rows, ybuf.at[1 - slot], sems.at[1 - slot], COMBINE_GATHER_QUEUES)

    buf = ybuf.at[slot]
    moe = _load_token_tiles(buf, 0, COMBINE_ROWS) * rt_ref[:, 2 * TOP_K:2 * TOP_K + 1]
    for k in range(1, TOP_K):
        moe = moe + _load_token_tiles(buf, k * COMBINE_ROWS, COMBINE_ROWS) * rt_ref[:, 2 * TOP_K + k:2 * TOP_K + k + 1]
    res = _rms(x1_ref[...] + moe, g_ref[...])

    @pl.when(i < n_p_blocks)
    def _():
        op_ref[...] = res

    @pl.when(i >= n_p_blocks)
    def _():
        os_ref[...] = res


def _combine(dest_b, rt, y_2d, x1, g, n_p):
    n = x1.shape[0]
    n_blk = n // COMBINE_ROWS
    n_p_blocks = n_p // COMBINE_ROWS
    dest_blocks = dest_b.reshape(n_blk, 1, TOP_K * COMBINE_ROWS)
    return pl.pallas_call(
        functools.partial(_combine_body, n_p_blocks=n_p_blocks),
        grid=(n_blk,),
        in_specs=[
            pl.BlockSpec((None, 1, COMBINE_ROWS * TOP_K), lambda i: (i, 0, 0), memory_space=pltpu.SMEM),
            pl.BlockSpec((None, 1, COMBINE_ROWS * TOP_K), lambda i: (jnp.minimum(i + 1, n_blk - 1), 0, 0),
                         memory_space=pltpu.SMEM),
            pl.BlockSpec((COMBINE_ROWS, LANE), lambda i: (i, 0)),
            pl.BlockSpec(memory_space=pl.ANY),
            pl.BlockSpec(memory_space=pl.ANY),
            pl.BlockSpec((COMBINE_ROWS, D_MODEL), lambda i: (i, 0)),
            pl.BlockSpec((1, D_MODEL), lambda i: (0, 0)),
        ],
        out_specs=_group_specs(COMBINE_ROWS, D_MODEL, n_p_blocks),
        out_shape=[jax.ShapeDtypeStruct((n_p, D_MODEL), F32), jax.ShapeDtypeStruct((n - n_p, D_MODEL), F32)],
        scratch_shapes=[pltpu.VMEM((2, TOP_K * COMBINE_ROWS * TOK_TILES, LANE), F32),
                        pltpu.SemaphoreType.DMA((2,))],
        compiler_params=pltpu.CompilerParams(dimension_semantics=("arbitrary",),
                                             vmem_limit_bytes=VMEM_LIMIT),
        name="combine",
    )(dest_blocks, dest_blocks, rt, y_2d.reshape(-1, TOK_TILES, LANE), y_2d, x1, g)


def _plan_body(rtt_ref, cnt_ref, dk_ref, db_ref, meta_ref, pst):
    i = pl.program_id(0)
    sh = _log2(EXPERT_ROWS)
    n_e = N_EXPERTS

    @pl.when(i == 0)
    def _():
        cnt = cnt_ref[...].astype(jnp.int32)
        padded = (((cnt + (EXPERT_ROWS - 1)) >> sh) << sh).astype(F32)
        e_r = lax.broadcasted_iota(jnp.int32, (n_e, n_e), 0)
        e_c = lax.broadcasted_iota(jnp.int32, (n_e, n_e), 1)
        p_t = jnp.broadcast_to(padded, (n_e, n_e)).T
        pend = jnp.sum(jnp.where(e_c <= e_r, p_t, 0.0), axis=1, keepdims=True)
        pst[...] = pend - padded
        has_rows = p_t > 0.0
        group = jnp.sum(jnp.where((e_c <= e_r) & has_rows, 1.0, 0.0), axis=1, keepdims=True) - 1.0
        nxt = jnp.min(jnp.where((e_c > e_r) & has_rows, e_c, n_e), axis=1, keepdims=True)
        nxt = jnp.where(nxt >= n_e, -1, nxt)

        mb = meta_ref.shape[1]
        blk = lax.broadcasted_iota(jnp.int32, (n_e, mb), 1)
        eb = lax.broadcasted_iota(jnp.int32, (n_e, mb), 0)
        first_row = (blk * EXPERT_ROWS).astype(F32)

        def expert_of(row0):
            return jnp.minimum(jnp.sum(jnp.where(pend <= row0, 1, 0), axis=0, keepdims=True), n_e - 1)

        be = expert_of(first_row)
        be_prev = expert_of(first_row - EXPERT_ROWS)
        hit = eb == be
        wslot = jnp.sum(jnp.where(hit, group, 0.0), axis=0, keepdims=True).astype(jnp.int32) & 1
        nx = jnp.sum(jnp.where(hit, nxt, 0), axis=0, keepdims=True)
        n_used = pend[n_e - 1:n_e, :].astype(jnp.int32) >> sh
        lane = lax.broadcasted_iota(jnp.int32, (1, mb), 1)
        first = (((be != be_prev) | (lane == 0)) & (lane < n_used)).astype(jnp.int32)
        row8 = lax.broadcasted_iota(jnp.int32, (SUBLANE, mb), 0)
        meta = jnp.where(row8 == 0, be, jnp.where(row8 == 1, first, jnp.where(row8 == 2, wslot,
                         jnp.where(row8 == 3, nx, n_used))))
        meta_ref[...] = meta

    tm = rtt_ref.shape[1]
    eid = lax.broadcasted_iota(jnp.int32, (n_e, tm), 0).astype(F32)
    row8 = lax.broadcasted_iota(jnp.int32, (SUBLANE, tm), 0)
    d8 = jnp.zeros((SUBLANE, tm), jnp.int32)
    for k in range(TOP_K):
        start = jnp.sum(jnp.where(eid == rtt_ref[k:k + 1, :], pst[...], 0.0), axis=0, keepdims=True)
        d8 = jnp.where(row8 == k, (start + rtt_ref[TOP_K + k:TOP_K + k + 1, :]).astype(jnp.int32), d8)
    dk_ref[...] = d8[:TOP_K]
    for b in range(tm // COMBINE_ROWS):
        db_ref[b] = d8[:TOP_K, b * COMBINE_ROWS:(b + 1) * COMBINE_ROWS]


def _plan(rtt, cnt):
    n = rtt.shape[1]
    n_rows = n * TOP_K + N_EXPERTS * EXPERT_ROWS
    n_blocks = n_rows // EXPERT_ROWS
    mb = -(-n_blocks // LANE) * LANE
    tile = max(t for t in range(COMBINE_ROWS, PLAN_TILE_MAX + 1, COMBINE_ROWS) if n % t == 0)
    dk, db, meta = pl.pallas_call(
        _plan_body,
        grid=(n // tile,),
        in_specs=[pl.BlockSpec((2 * TOP_K, tile), lambda i: (0, i)),
                  pl.BlockSpec((N_EXPERTS, 1), lambda i: (0, 0))],
        out_specs=[pl.BlockSpec((TOP_K, tile), lambda i: (0, i)),
                   pl.BlockSpec((tile // COMBINE_ROWS, TOP_K, COMBINE_ROWS), lambda i: (i, 0, 0)),
                   pl.BlockSpec((SUBLANE, mb), lambda i: (0, 0))],
        out_shape=[jax.ShapeDtypeStruct((TOP_K, n), jnp.int32),
                   jax.ShapeDtypeStruct((n // COMBINE_ROWS, TOP_K, COMBINE_ROWS), jnp.int32),
                   jax.ShapeDtypeStruct((SUBLANE, mb), jnp.int32)],
        scratch_shapes=[pltpu.VMEM((N_EXPERTS, 1), F32)],
        compiler_params=pltpu.CompilerParams(dimension_semantics=("arbitrary",)),
        name="plan",
    )(rtt, cnt)
    block_meta = (meta[0, :n_blocks], meta[4, 0:1], meta[1, :n_blocks], meta[2, :n_blocks], meta[3, :n_blocks])
    return dk, db, n_rows, block_meta


def _pad_lanes(v, width):
    return jnp.zeros((1, width), F32).at[0, :v.shape[0]].set(v.astype(F32))


def kernel(x_prompt, x_sample, state_gdn_conv, state_gdn, state_gla, rms_mix_w, w_in, conv_w, gdn_a_log,
           gdn_dt_bias, gdn_norm_w, gla_gk_w, gla_gk_b, gla_norm_w, w_out, rms_ffn_w, w_router, b_router,
           w_up, b_up, w_down, b_down, rms_final_w):
    bp, tp, d = x_prompt.shape
    bs, ts, _ = x_sample.shape
    n_p, n_s = bp * tp, bs * ts
    assert d == D_MODEL and state_gdn.shape[0] == 1, "single-layer kernel"
    assert tp >= CONV_WIDTH - 1 and ts >= CONV_WIDTH - 1, "new conv state is taken from the new tokens only"
    l = 0

    wi = w_in[l]
    a0 = GDN_CONV_CH + GDN_V_W
    g0 = a0 + 2 * GDN_HEADS
    lr0 = g0 + 2 * GLA_QK_W + 2 * GLA_V_W
    small = jnp.concatenate([wi[:, a0:a0 + 2 * GDN_HEADS], wi[:, lr0:lr0 + GLA_GATE_RANK],
                             jnp.zeros((d, SM_W - 2 * GDN_HEADS - GLA_GATE_RANK), F32)], axis=1)
    w_big = jnp.concatenate([wi[:, :a0], wi[:, g0:lr0], small], axis=1).astype(BF16)
    alog = _pad_lanes(gdn_a_log[l], SM_W)
    dtb = _pad_lanes(gdn_dt_bias[l], SM_W)
    wgk = jnp.zeros((SM_W, GLA_QK_W), F32).at[SM_LR:SM_LR + GLA_GATE_RANK].set(gla_gk_w[l])
    wr = jnp.zeros((d, LANE), F32).at[:, :N_EXPERTS].set(w_router[l])
    br = jnp.full((1, LANE), -1e30, F32).at[0, :N_EXPERTS].set(b_router[l])

    assert n_p % ROW_TILE == 0 and n_s % ROW_TILE == 0
    x_p, x_s = x_prompt.reshape(n_p, d), x_sample.reshape(n_s, d)
    proj = _inproj(x_p, x_s, rms_mix_w[l][None, :], w_big)

    tb_p = PROMPT_TIME_BLOCK
    zeros_conv = jnp.zeros((bp, CONV_WIDTH - 1, GDN_CONV_CH), F32)
    og_p, gdn_p, conv_p = _gdn(proj, bp, 1, tp, tb_p, CHUNK, CHUNK, zeros_conv,
                               jnp.zeros((bp, GDN_HEADS, GDN_DK, GDN_DV), F32), conv_w[l], alog, dtb,
                               gdn_norm_w[l][None, :])
    ol_p, gla_p = _gla(proj, bp, 1, tp, tb_p, CHUNK, CHUNK, jnp.zeros((bp, GLA_HEADS, GLA_DK, GLA_DV), F32),
                       wgk, gla_gk_b[l][None, :], gla_norm_w[l][None, :])

    ts_pad = SUBLANE
    nb_s = SAMPLE_SEQS_PER_STEP
    proj_s = proj[n_p:].reshape(bs, ts, PROJ_W)
    proj_sp = jnp.pad(proj_s, ((0, 0), (0, ts_pad - ts), (0, 0))).reshape(bs * ts_pad, PROJ_W)
    og_s, gdn_s, conv_s = _gdn(proj_sp, bs, nb_s, ts_pad, ts_pad, ts_pad, ts, state_gdn_conv[l], state_gdn[l],
                               conv_w[l], alog, dtb, gdn_norm_w[l][None, :])
    ol_s, gla_s = _gla(proj_sp, bs, nb_s, ts_pad, ts_pad, ts_pad, ts, state_gla[l], wgk, gla_gk_b[l][None, :],
                       gla_norm_w[l][None, :])
    og_s = og_s.reshape(bs, ts_pad, GDN_V_W)[:, :ts].reshape(n_s, GDN_V_W)
    ol_s = ol_s.reshape(bs, ts_pad, GLA_V_W)[:, :ts].reshape(n_s, GLA_V_W)

    x1, h2, rt, rtt, cnt = _outproj(og_p, og_s, ol_p, ol_s, x_p, x_s, w_out[l].astype(BF16),
                                    rms_ffn_w[l][None, :], wr, br)

    dest_k, dest_b, n_rows, block_meta = _plan(rtt, cnt)
    xs = _dispatch(h2.reshape(-1, PACK_TILES, LANE), dest_k.reshape(-1), n_rows)
    y_rows = _experts(block_meta, xs.reshape(-1, LANE), w_up[l], b_up[l], w_down[l], b_down[l])
    y_p, y_s = _combine(dest_b, rt, y_rows, x1, rms_final_w[None, :], n_p)
    y_prompt = y_p.reshape(bp, tp, d)
    y_sample = y_s.reshape(bs, ts, d)
    return (y_prompt, y_sample, conv_p[None], gdn_p[None], gla_p[None], conv_s[None], gdn_s[None], gla_s[None])
```

```python
import functools

import jax
import jax.numpy as jnp
from jax import lax
from jax.experimental import pallas as pl
from jax.experimental.pallas import tpu as pltpu
from jax.experimental.pallas import tpu_sc as plsc

F32 = jnp.float32
BF16 = jnp.bfloat16
HI = lax.Precision.HIGHEST

D_MODEL = 1024
GDN_HEADS = 4
GDN_DK = 128
GDN_DV = 128
GLA_HEADS = 4
GLA_DK = 64
GLA_DV = 128
GLA_GATE_RANK = 16
GLA_GATE_NORMALIZER = 16.0
CONV_WIDTH = 4
CHUNK = 64
N_EXPERTS = 32
TOP_K = 4
D_FF = 1024
SWIGLU_LIMIT = 7.0
SWIGLU_ALPHA = 1.702
RMS_EPS = 1e-6
L2_EPS = 1e-6

GDN_QK_W = GDN_HEADS * GDN_DK
GDN_V_W = GDN_HEADS * GDN_DV
GDN_CONV_CH = 2 * GDN_QK_W + GDN_V_W
GLA_QK_W = GLA_HEADS * GLA_DK
GLA_V_W = GLA_HEADS * GLA_DV

COL_QKV = 0
COL_Z = 1536
COL_GQ = 2048
COL_GK = 2304
COL_GV = 2560
COL_GG = 3072
COL_SM = 3584
SM_W = 128
PROJ_W = COL_SM + SM_W
SM_A, SM_B, SM_LR = 0, 4, 8

LANE = 128
SUBLANE = 8
TOK_TILES = D_MODEL // LANE
PACK_TILES = TOK_TILES // 2
ROW_TILE = 512
EXPERT_ROWS = 512
EXPERT_WEIGHT_QUEUE = 1
WEIGHT_CAST_ROWS = 128
COMBINE_ROWS = 256
PLAN_TILE_MAX = 2048
DMA_ISSUE_UNROLL = 8
COMBINE_GATHER_QUEUES = (0, 1)
CONV_ROW_SLAB = 128
GDN_CHUNKS_PER_TRIP = 4
GLA_CHUNKS_PER_TRIP = 4
SCAN_CHUNKS_PER_TRIP = 4
PROMPT_TIME_BLOCK = 512
SAMPLE_SEQS_PER_STEP = 8
VMEM_LIMIT = 56 * 1024 * 1024


def _dot(a, b):
    return jnp.dot(a.astype(BF16), b.astype(BF16), preferred_element_type=F32)


def _dot_nt(a, b):
    return lax.dot_general(a.astype(BF16), b.astype(BF16), (((1,), (1,)), ((), ())),
                           preferred_element_type=F32)


def _dot_tn(a, b):
    return lax.dot_general(a.astype(BF16), b.astype(BF16), (((0,), (0,)), ((), ())),
                           preferred_element_type=F32)


def _dot_hi(a, b):
    return jnp.dot(a, b, precision=HI, preferred_element_type=F32)


def _dot_3pass(a, b):
    a_hi = a.astype(BF16)
    b_hi = b.astype(BF16)
    a_lo = (a - a_hi.astype(F32)).astype(BF16)
    b_lo = (b - b_hi.astype(F32)).astype(BF16)

    def mm(x, y):
        return jnp.dot(x, y, preferred_element_type=F32)

    return (mm(a_lo, b_hi) + mm(a_hi, b_lo)) + mm(a_hi, b_hi)


def _rms(x, w):
    return x * lax.rsqrt(jnp.mean(x * x, axis=-1, keepdims=True) + RMS_EPS) * w


def _silu(x):
    return x * jax.nn.sigmoid(x)


def _group_specs(rows, width, n_p_blocks):
    return [pl.BlockSpec((rows, width), lambda i: (jnp.minimum(i, n_p_blocks - 1), 0)),
            pl.BlockSpec((rows, width), lambda i: (jnp.maximum(i - n_p_blocks, 0), 0))]


def _group_pick(i, n_p_blocks, p_ref, s_ref):
    return jnp.where(i < n_p_blocks, p_ref[...], s_ref[...])


def _inproj_body(xp_ref, xs_ref, g_ref, w_ref, o_ref, *, n_p_blocks):
    x = _group_pick(pl.program_id(0), n_p_blocks, xp_ref, xs_ref)
    h = _rms(x, g_ref[...])
    o_ref[...] = jnp.dot(h.astype(BF16), w_ref[...], preferred_element_type=F32)


def _inproj(x_p, x_s, g, w):
    n_p_blocks, n_s_blocks = x_p.shape[0] // ROW_TILE, x_s.shape[0] // ROW_TILE
    n = x_p.shape[0] + x_s.shape[0]
    return pl.pallas_call(
        functools.partial(_inproj_body, n_p_blocks=n_p_blocks),
        grid=(n_p_blocks + n_s_blocks,),
        in_specs=_group_specs(ROW_TILE, D_MODEL, n_p_blocks) + [
            pl.BlockSpec((1, D_MODEL), lambda i: (0, 0)),
            pl.BlockSpec((D_MODEL, PROJ_W), lambda i: (0, 0)),
        ],
        out_specs=pl.BlockSpec((ROW_TILE, PROJ_W), lambda i: (i, 0)),
        out_shape=jax.ShapeDtypeStruct((n, PROJ_W), F32),
        compiler_params=pltpu.CompilerParams(dimension_semantics=("arbitrary",),
                                             vmem_limit_bytes=VMEM_LIMIT),
        name="in_proj",
    )(x_p, x_s, g, w)


def _log2(n):
    assert n & (n - 1) == 0
    return n.bit_length() - 1


def _tri_inv_all(ms, c, ii, jj):
    eye = (ii == jj).astype(F32)
    base = min(c, 8)
    sh = _log2(base)
    blk = (ii >> sh) == (jj >> sh)
    ns = [jnp.where(blk, m, 0.0) for m in ms]
    xs = [eye - n for n in ns]
    ps = [_dot(n, n) for n in ns]
    ts = [_dot(jnp.concatenate([x, p], axis=0), p) for x, p in zip(xs, ps)]
    xs = [x + t[:c] for x, t in zip(xs, ts)]
    ps = [t[c:] for t in ts]
    xs = [x + _dot(x, p) for x, p in zip(xs, ps)]
    s = base
    while s < c:
        sh_s, sh_b = _log2(s), _log2(2 * s)
        off = ((ii >> sh_b) == (jj >> sh_b)) & ((ii >> sh_s) != (jj >> sh_s))
        ys = [_dot(x, jnp.where(off, m, 0.0)) for x, m in zip(xs, ms)]
        xs = [x - _dot(y, x) for x, y in zip(xs, ys)]
        s *= 2
    return xs


def _gated_norm(o, w, z):
    return o * lax.rsqrt(jnp.mean(o * o, axis=-1, keepdims=True) + RMS_EPS) * w * _silu(z)


def _chunk_rows(s, tb_rows, ci, c):
    r = s * tb_rows + ci * c
    if not isinstance(r, int):
        r = pl.multiple_of(r, c)
    return r


def _for_chunks(n_chunks, step):
    if n_chunks == 1:
        step(0, 0)
    else:
        lax.fori_loop(0, n_chunks, step, 0)


def _gdn_body(qkv_ref, z_ref, sm_ref, cbuf_ref, s0_ref, cw_ref, alog_ref, dtb_ref, nw_ref,
              o_ref, sout_ref, cout_ref, st, xc, act, gcs, us, wss, qgs, kds, aqs,
              *, nb, tb_rows, chunk, valid, n_tb):
    tb = pl.program_id(1)
    c = chunk
    n_heads = GDN_HEADS
    tail = CONV_WIDTH - 1
    pad = SUBLANE
    units = [(s, h) for s in range(nb) for h in range(n_heads)]

    n_slabs = GDN_CONV_CH // LANE

    def lanes(j):
        return slice(j * LANE, (j + 1) * LANE)

    @pl.when(tb == 0)
    def _():
        st[...] = s0_ref[...]
        for s in range(nb):
            for j in range(n_slabs):
                xc[s, j, pad - tail:pad, :] = cbuf_ref[s, :, lanes(j)]

    if n_tb > 1:
        @pl.when(tb > 0)
        def _():
            for s in range(nb):
                for j in range(n_slabs):
                    xc[s, j, pad - tail:pad, :] = xc[s, j, tb_rows + pad - tail:tb_rows + pad, :]

    for s in range(nb):
        for j in range(n_slabs):
            xc[s, j, pad:pad + tb_rows, :] = qkv_ref[s * tb_rows:(s + 1) * tb_rows, lanes(j)]

    row_slab = min(tb_rows, CONV_ROW_SLAB)
    parities = 2 if row_slab >= 2 * SUBLANE else 1
    for s in range(nb):
        for j in range(n_slabs):
            src, dst = xc.at[s, j], act.at[j]
            for sl in range(tb_rows // row_slab):
                for p in range(parities):
                    lo = pad - tail + sl * row_slab + p
                    out0 = s * tb_rows + sl * row_slab + p

                    def rows_from(start):
                        if parities == 1:
                            return pl.ds(start, row_slab)
                        return pl.ds(start, row_slab // 2, stride=2)

                    acc = src[rows_from(lo), :] * cw_ref[0:1, lanes(j)]
                    for i in range(1, CONV_WIDTH):
                        acc = acc + src[rows_from(lo + i), :] * cw_ref[i:i + 1, lanes(j)]
                    dst[rows_from(out0), :] = _silu(acc)

    ii = lax.broadcasted_iota(jnp.int32, (c, c), 0)
    jj = lax.broadcasted_iota(jnp.int32, (c, c), 1)
    lower = (ii >= jj)
    lower_f = lower.astype(F32)
    strict = (ii > jj)
    rowmask = None
    if valid < c:
        rowmask = lax.broadcasted_iota(jnp.int32, (c, 1), 0) < valid

    def hs(h, w):
        return slice(h * w, (h + 1) * w)

    n_chunks = tb_rows // c
    cpi = next(k for k in (GDN_CHUNKS_PER_TRIP, 2, 1) if n_chunks % k == 0)
    p1_units = [(g, h) for g in range(nb * cpi) for h in range(n_heads)]

    def phase1(ci, carry):
        rows, b_ts, gc_ts, gc_tts = [], [], [], []
        for g in range(nb * cpi):
            rr = pl.ds(_chunk_rows(g // cpi, tb_rows, ci * cpi + g % cpi, c), c)
            sm = sm_ref[rr, :]
            g_t = -jnp.exp(alog_ref[...]) * jax.nn.softplus(sm + dtb_ref[...])
            b_t = jax.nn.sigmoid(sm)
            if rowmask is not None:
                g_t = jnp.where(rowmask, g_t, 0.0)
                b_t = jnp.where(rowmask, b_t, 0.0)
            gc_t = _dot_hi(lower_f, g_t)
            gcs[rr, :] = gc_t
            rows.append(rr)
            b_ts.append(b_t)
            gc_ts.append(gc_t)
            gc_tts.append(gc_t.T)
        qn, kn, kb, vb = {}, {}, {}, {}
        for (s, h) in p1_units:
            q = act[h, rows[s], :]
            k = act[n_heads + h, rows[s], :]
            v = act[2 * n_heads + h, rows[s], :]
            if rowmask is not None:
                q = jnp.where(rowmask, q, 0.0)
                k = jnp.where(rowmask, k, 0.0)
                v = jnp.where(rowmask, v, 0.0)
            qn[s, h] = q * lax.rsqrt(jnp.sum(q * q, axis=-1, keepdims=True) + L2_EPS) * (GDN_DK ** -0.5)
            kn[s, h] = k * lax.rsqrt(jnp.sum(k * k, axis=-1, keepdims=True) + L2_EPS)
            beta = b_ts[s][:, SM_B + h:SM_B + h + 1]
            kb[s, h] = kn[s, h] * beta
            vb[s, h] = v * beta
        s1 = {u: _dot_nt(jnp.concatenate([kb[u], qn[u]], axis=0), kn[u]) for u in p1_units}
        mm = []
        for (s, h) in p1_units:
            gcol = gc_ts[s][:, SM_A + h:SM_A + h + 1]
            grow = gc_tts[s][SM_A + h:SM_A + h + 1, :]
            dec = jnp.exp(jnp.where(lower, gcol - grow, -jnp.inf))
            mm.append(jnp.where(strict, s1[s, h][:c] * dec, 0.0))
            aqs[h, rows[s], :] = s1[s, h][c:] * dec
        tms = _tri_inv_all(mm, c, ii, jj)
        for (s, h), tm in zip(p1_units, tms):
            gcol = gc_ts[s][:, SM_A + h:SM_A + h + 1]
            eg = jnp.exp(gcol)
            uw = _dot(tm, jnp.concatenate([vb[s, h], kb[s, h] * eg], axis=1))
            us[rows[s], hs(h, GDN_DV)] = uw[:, :GDN_DV]
            wss[rows[s], hs(h, GDN_DV)] = uw[:, GDN_DV:]
            qgs[rows[s], hs(h, GDN_DK)] = qn[s, h] * eg
            kds[rows[s], hs(h, GDN_DK)] = kn[s, h] * jnp.exp(gcol[c - 1:c, :] - gcol)
        return carry

    cp2 = next(k for k in (SCAN_CHUNKS_PER_TRIP, 1) if n_chunks % k == 0)

    def phase2(ti, carry):
        for u in range(cp2):
            scan_chunk(ti * cp2 + u)
        return carry

    def scan_chunk(ci):
        r0 = [_chunk_rows(s, tb_rows, ci, c) for s in range(nb)]
        rows = [pl.ds(r, c) for r in r0]
        ws = {(s, h): _dot(jnp.concatenate([wss[rows[s], hs(h, GDN_DV)], qgs[rows[s], hs(h, GDN_DK)]], axis=0),
                           st[s, h]) for (s, h) in units}
        v_new = {(s, h): us[rows[s], hs(h, GDN_DV)] - ws[s, h][:c] for (s, h) in units}
        o = {(s, h): ws[s, h][c:] + _dot(aqs[h, rows[s], :], v_new[s, h]) for (s, h) in units}
        upd = {(s, h): _dot_tn(kds[rows[s], hs(h, GDN_DK)], v_new[s, h]) for (s, h) in units}
        for (s, h) in units:
            g_last = gcs[pl.ds(r0[s] + c - 1, 1), SM_A + h:SM_A + h + 1]
            st[s, h] = st[s, h] * jnp.exp(g_last) + upd[s, h]
        for s in range(nb):
            o_ref[rows[s], :] = jnp.concatenate(
                [_gated_norm(o[s, h], nw_ref[...], z_ref[rows[s], hs(h, GDN_DV)]) for h in range(n_heads)], axis=1)

    _for_chunks(n_chunks // cpi, phase1)
    _for_chunks(n_chunks // cp2, phase2)

    @pl.when(tb == n_tb - 1)
    def _():
        sout_ref[...] = st[...]
        last = tb_rows if valid == c else valid
        for s in range(nb):
            cout_ref[s] = jnp.concatenate(
                [xc[s, j, pad + last - tail:pad + last, :] for j in range(n_slabs)], axis=1)


def _gdn(proj, n_seq, nb, t_len, tb_rows, chunk, valid, conv_buf, s0, conv_w, alog, dtb, nw):
    n_tb = t_len // tb_rows
    assert nb == 1 or n_tb == 1
    rows = nb * tb_rows

    def rowblk(b, t):
        return b * n_tb + t

    body = functools.partial(_gdn_body, nb=nb, tb_rows=tb_rows, chunk=chunk, valid=valid, n_tb=n_tb)
    return pl.pallas_call(
        body,
        grid=(n_seq // nb, n_tb),
        in_specs=[
            pl.BlockSpec((rows, GDN_CONV_CH), lambda b, t: (rowblk(b, t), COL_QKV // GDN_CONV_CH)),
            pl.BlockSpec((rows, GDN_V_W), lambda b, t: (rowblk(b, t), COL_Z // GDN_V_W)),
            pl.BlockSpec((rows, SM_W), lambda b, t: (rowblk(b, t), COL_SM // SM_W)),
            pl.BlockSpec((nb, CONV_WIDTH - 1, GDN_CONV_CH), lambda b, t: (b, 0, 0)),
            pl.BlockSpec((nb, GDN_HEADS, GDN_DK, GDN_DV), lambda b, t: (b, 0, 0, 0)),
            pl.BlockSpec((CONV_WIDTH, GDN_CONV_CH), lambda b, t: (0, 0)),
            pl.BlockSpec((1, SM_W), lambda b, t: (0, 0)),
            pl.BlockSpec((1, SM_W), lambda b, t: (0, 0)),
            pl.BlockSpec((1, GDN_DV), lambda b, t: (0, 0)),
        ],
        out_specs=[
            pl.BlockSpec((rows, GDN_V_W), lambda b, t: (rowblk(b, t), 0)),
            pl.BlockSpec((nb, GDN_HEADS, GDN_DK, GDN_DV), lambda b, t: (b, 0, 0, 0)),
            pl.BlockSpec((nb, CONV_WIDTH - 1, GDN_CONV_CH), lambda b, t: (b, 0, 0)),
        ],
        out_shape=[
            jax.ShapeDtypeStruct((n_seq * t_len, GDN_V_W), F32),
            jax.ShapeDtypeStruct((n_seq, GDN_HEADS, GDN_DK, GDN_DV), F32),
            jax.ShapeDtypeStruct((n_seq, CONV_WIDTH - 1, GDN_CONV_CH), F32),
        ],
        scratch_shapes=[
            pltpu.VMEM((nb, GDN_HEADS, GDN_DK, GDN_DV), F32),
            pltpu.VMEM((nb, GDN_CONV_CH // LANE, tb_rows + SUBLANE, LANE), F32),
            pltpu.VMEM((GDN_CONV_CH // LANE, rows, LANE), F32),
            pltpu.VMEM((rows, SM_W), F32),
            pltpu.VMEM((rows, GDN_V_W), F32),
            pltpu.VMEM((rows, GDN_V_W), F32),
            pltpu.VMEM((rows, GDN_QK_W), F32),
            pltpu.VMEM((rows, GDN_QK_W), F32),
            pltpu.VMEM((GDN_HEADS, rows, chunk), F32),
        ],
        compiler_params=pltpu.CompilerParams(dimension_semantics=("arbitrary", "arbitrary"),
                                             vmem_limit_bytes=VMEM_LIMIT),
        name="gdn_mixer",
    )(proj, proj, proj, conv_buf, s0, conv_w, alog, dtb, nw)


def _gla_body(q_ref, k_ref, v_ref, go_ref, sm_ref, s0_ref, wgk_ref, bgk_ref, nw_ref,
              o_ref, sout_ref, st, qes, ois, upds, decs, *, nb, tb_rows, chunk, valid, n_tb):
    tb = pl.program_id(1)
    c = chunk
    n_heads = GLA_HEADS
    units = [(s, h) for s in range(nb) for h in range(n_heads)]

    @pl.when(tb == 0)
    def _():
        st[...] = s0_ref[...]

    ii = lax.broadcasted_iota(jnp.int32, (c, c), 0)
    jj = lax.broadcasted_iota(jnp.int32, (c, c), 1)
    lower = (ii >= jj)
    lower_f = lower.astype(F32)
    rid = lax.broadcasted_iota(jnp.int32, (c, 1), 0)
    rowmask = (rid < valid) if valid < c else None
    n_sub = max(c // 16, 1)
    sub = c // n_sub

    n_chunks = tb_rows // c
    cpi = next(k for k in (GLA_CHUNKS_PER_TRIP, 2, 1) if n_chunks % k == 0)
    p1_units = [(g, h) for g in range(nb * cpi) for h in range(n_heads)]

    def phase1(ci, carry):
        rows, slots, bcs, bc_ts = [], [], [], []
        for g in range(nb * cpi):
            chunk_idx = ci * cpi + g % cpi
            rr = pl.ds(_chunk_rows(g // cpi, tb_rows, chunk_idx, c), c)
            slots.append((g // cpi) * n_chunks + chunk_idx)
            gk = jax.nn.log_sigmoid(_dot(sm_ref[rr, :], wgk_ref[...]) + bgk_ref[...]) / GLA_GATE_NORMALIZER
            if rowmask is not None:
                gk = jnp.where(rowmask, gk, 0.0)
            bc = _dot_hi(lower_f, gk)
            rows.append(rr)
            bcs.append(bc)
            bc_ts.append(bc.T)
        q, k, v, bch = {}, {}, {}, {}
        for (s, h) in p1_units:
            ks = slice(h * GLA_DK, (h + 1) * GLA_DK)
            vs = slice(h * GLA_DV, (h + 1) * GLA_DV)
            q[s, h] = q_ref[rows[s], ks] * (GLA_DK ** -0.5)
            kk = k_ref[rows[s], ks]
            vv = v_ref[rows[s], vs]
            if rowmask is not None:
                kk = jnp.where(rowmask, kk, 0.0)
                vv = jnp.where(rowmask, vv, 0.0)
            k[s, h], v[s, h] = kk, vv
            bch[s, h] = bcs[s][:, ks]
        for (g, h) in p1_units:
            qes[h, rows[g], :] = q[g, h] * jnp.exp(bch[g, h])
        a = {}
        for u in p1_units:
            q_parts, k_parts = [], []
            for sb in range(n_sub):
                ref_row = bch[u][sb * sub:sb * sub + 1, :]
                in_blk = (rid >= sb * sub) & (rid < (sb + 1) * sub)
                q_parts.append(jnp.where(in_blk, q[u] * jnp.exp(jnp.where(in_blk, bch[u] - ref_row, 0.0)), 0.0))
                k_parts.append(k[u] * jnp.exp(jnp.where(rid < (sb + 1) * sub, ref_row - bch[u], 0.0)))
            q_hat = jnp.concatenate(q_parts, axis=1) if n_sub > 1 else q_parts[0]
            k_hat = jnp.concatenate(k_parts, axis=1) if n_sub > 1 else k_parts[0]
            a[u] = jnp.where(lower, _dot_nt(q_hat, k_hat), 0.0)
        upd = {u: _dot_tn(k[u] * jnp.exp(bch[u][c - 1:c, :] - bch[u]), v[u]) for u in p1_units}
        o_intra = {u: _dot(a[u], v[u]) for u in p1_units}
        for (g, h) in p1_units:
            dec_col = bc_ts[g][h * GLA_DK:(h + 1) * GLA_DK, c - 1:c]
            decs[slots[g], h] = jnp.broadcast_to(jnp.exp(dec_col), (GLA_DK, GLA_DV))
            upds[slots[g], h] = upd[g, h]
            ois[rows[g], h * GLA_DV:(h + 1) * GLA_DV] = o_intra[g, h]
        return carry

    cp2 = next(k for k in (SCAN_CHUNKS_PER_TRIP, 1) if n_chunks % k == 0)

    def phase2(ti, carry):
        for u in range(cp2):
            scan_chunk(ti * cp2 + u)
        return carry

    def scan_chunk(ci):
        rows = [pl.ds(_chunk_rows(s, tb_rows, ci, c), c) for s in range(nb)]
        o = {(s, h): ois[rows[s], h * GLA_DV:(h + 1) * GLA_DV] + _dot(qes[h, rows[s], :], st[s, h])
             for (s, h) in units}
        for (s, h) in units:
            st[s, h] = decs[s * n_chunks + ci, h] * st[s, h] + upds[s * n_chunks + ci, h]
        for s in range(nb):
            o_ref[rows[s], :] = jnp.concatenate(
                [_gated_norm(o[s, h], nw_ref[...], go_ref[rows[s], h * GLA_DV:(h + 1) * GLA_DV])
                 for h in range(n_heads)], axis=1)

    _for_chunks(n_chunks // cpi, phase1)
    _for_chunks(n_chunks // cp2, phase2)

    @pl.when(tb == n_tb - 1)
    def _():
        sout_ref[...] = st[...]


def _gla(proj, n_seq, nb, t_len, tb_rows, chunk, valid, s0, wgk, bgk, nw):
    n_tb = t_len // tb_rows
    assert nb == 1 or n_tb == 1
    rows = nb * tb_rows

    def rowblk(b, t):
        return b * n_tb + t

    body = functools.partial(_gla_body, nb=nb, tb_rows=tb_rows, chunk=chunk, valid=valid, n_tb=n_tb)
    return pl.pallas_call(
        body,
        grid=(n_seq // nb, n_tb),
        in_specs=[
            pl.BlockSpec((rows, GLA_QK_W), lambda b, t: (rowblk(b, t), COL_GQ // GLA_QK_W)),
            pl.BlockSpec((rows, GLA_QK_W), lambda b, t: (rowblk(b, t), COL_GK // GLA_QK_W)),
            pl.BlockSpec((rows, GLA_V_W), lambda b, t: (rowblk(b, t), COL_GV // GLA_V_W)),
            pl.BlockSpec((rows, GLA_V_W), lambda b, t: (rowblk(b, t), COL_GG // GLA_V_W)),
            pl.BlockSpec((rows, SM_W), lambda b, t: (rowblk(b, t), COL_SM // SM_W)),
            pl.BlockSpec((nb, GLA_HEADS, GLA_DK, GLA_DV), lambda b, t: (b, 0, 0, 0)),
            pl.BlockSpec((SM_W, GLA_QK_W), lambda b, t: (0, 0)),
            pl.BlockSpec((1, GLA_QK_W), lambda b, t: (0, 0)),
            pl.BlockSpec((1, GLA_DV), lambda b, t: (0, 0)),
        ],
        out_specs=[
            pl.BlockSpec((rows, GLA_V_W), lambda b, t: (rowblk(b, t), 0)),
            pl.BlockSpec((nb, GLA_HEADS, GLA_DK, GLA_DV), lambda b, t: (b, 0, 0, 0)),
        ],
        out_shape=[
            jax.ShapeDtypeStruct((n_seq * t_len, GLA_V_W), F32),
            jax.ShapeDtypeStruct((n_seq, GLA_HEADS, GLA_DK, GLA_DV), F32),
        ],
        scratch_shapes=[
            pltpu.VMEM((nb, GLA_HEADS, GLA_DK, GLA_DV), F32),
            pltpu.VMEM((GLA_HEADS, rows, GLA_DK), F32),
            pltpu.VMEM((rows, GLA_V_W), F32),
            pltpu.VMEM((rows // chunk, GLA_HEADS, GLA_DK, GLA_DV), F32),
            pltpu.VMEM((rows // chunk, GLA_HEADS, GLA_DK, GLA_DV), F32),
        ],
        compiler_params=pltpu.CompilerParams(dimension_semantics=("arbitrary", "arbitrary"),
                                             vmem_limit_bytes=VMEM_LIMIT),
        name="gla_mixer",
    )(proj, proj, proj, proj, proj, s0, wgk, bgk, nw)


def _outproj_body(ogp_ref, ogs_ref, olp_ref, ols_ref, xp_ref, xs_ref, wo_ref, g_ref, wr_ref, br_ref,
                  x1_ref, h2_ref, rt_ref, rtt_ref, cnt_ref, base, *, n_p_blocks):
    i = pl.program_id(0)

    @pl.when(i == 0)
    def _():
        base[...] = jnp.zeros_like(base)

    o = jnp.concatenate([_group_pick(i, n_p_blocks, ogp_ref, ogs_ref),
                         _group_pick(i, n_p_blocks, olp_ref, ols_ref)], axis=1)
    x1 = _group_pick(i, n_p_blocks, xp_ref, xs_ref) + jnp.dot(o.astype(BF16), wo_ref[...],
                                                               preferred_element_type=F32)
    x1_ref[...] = x1
    h = _rms(x1, g_ref[...])
    _store_token_tiles(h2_ref, _pack_bf16_pairs(h))
    logits = _dot_3pass(h, wr_ref[...]) + br_ref[...]

    tm = logits.shape[0]
    lt = logits.T[:N_EXPERTS]
    eid = lax.broadcasted_iota(jnp.int32, (N_EXPERTS, tm), 0)
    work = lt
    sel = jnp.zeros((N_EXPERTS, tm), F32)
    hits, ids, vals = [], [], []
    for _ in range(TOP_K):
        m = jnp.max(work, axis=0, keepdims=True)
        idx = jnp.min(jnp.where(work == m, eid, N_EXPERTS), axis=0, keepdims=True)
        hit = eid == idx
        hits.append(hit)
        ids.append(idx)
        vals.append(m)
        work = jnp.where(hit, -jnp.inf, work)
        sel = sel + hit.astype(F32)
    exps = [jnp.exp(v - vals[0]) for v in vals]
    den = exps[0]
    for e in exps[1:]:
        den = den + e
    gates = [e / den for e in exps]

    ri = lax.broadcasted_iota(jnp.int32, (tm, tm), 0)
    ci = lax.broadcasted_iota(jnp.int32, (tm, tm), 1)
    before = _dot(sel, (ri < ci).astype(F32)) + base[...]
    ranks = [jnp.sum(jnp.where(hit, before, 0.0), axis=0, keepdims=True) for hit in hits]
    base[...] = base[...] + jnp.sum(sel, axis=1, keepdims=True)
    cnt_ref[...] = base[...]

    row = lax.broadcasted_iota(jnp.int32, (LANE, tm), 0)
    rec = jnp.zeros((LANE, tm), F32)
    for k in range(TOP_K):
        rec = jnp.where(row == k, ids[k].astype(F32), rec)
        rec = jnp.where(row == TOP_K + k, ranks[k], rec)
        rec = jnp.where(row == 2 * TOP_K + k, gates[k], rec)
    rt_ref[...] = rec.T
    rtt_ref[...] = rec[:2 * TOP_K]


def _outproj(og_p, og_s, ol_p, ol_s, x_p, x_s, wo, g, wr, br):
    n_p_blocks, n_s_blocks = x_p.shape[0] // ROW_TILE, x_s.shape[0] // ROW_TILE
    n = x_p.shape[0] + x_s.shape[0]
    return pl.pallas_call(
        functools.partial(_outproj_body, n_p_blocks=n_p_blocks),
        grid=(n_p_blocks + n_s_blocks,),
        in_specs=_group_specs(ROW_TILE, GDN_V_W, n_p_blocks) + _group_specs(ROW_TILE, GLA_V_W, n_p_blocks)
        + _group_specs(ROW_TILE, D_MODEL, n_p_blocks) + [
            pl.BlockSpec((D_MODEL, D_MODEL), lambda i: (0, 0)),
            pl.BlockSpec((1, D_MODEL), lambda i: (0, 0)),
            pl.BlockSpec((D_MODEL, LANE), lambda i: (0, 0)),
            pl.BlockSpec((1, LANE), lambda i: (0, 0)),
        ],
        out_specs=[
            pl.BlockSpec((ROW_TILE, D_MODEL), lambda i: (i, 0)),
            pl.BlockSpec((ROW_TILE * PACK_TILES, LANE), lambda i: (i, 0)),
            pl.BlockSpec((ROW_TILE, LANE), lambda i: (i, 0)),
            pl.BlockSpec((2 * TOP_K, ROW_TILE), lambda i: (0, i)),
            pl.BlockSpec((N_EXPERTS, 1), lambda i: (0, 0)),
        ],
        out_shape=[
            jax.ShapeDtypeStruct((n, D_MODEL), F32),
            jax.ShapeDtypeStruct((n * PACK_TILES, LANE), jnp.uint32),
            jax.ShapeDtypeStruct((n, LANE), F32),
            jax.ShapeDtypeStruct((2 * TOP_K, n), F32),
            jax.ShapeDtypeStruct((N_EXPERTS, 1), F32),
        ],
        scratch_shapes=[pltpu.VMEM((N_EXPERTS, 1), F32)],
        compiler_params=pltpu.CompilerParams(dimension_semantics=("arbitrary",),
                                             vmem_limit_bytes=VMEM_LIMIT),
        name="out_proj",
    )(og_p, og_s, ol_p, ol_s, x_p, x_s, wo, g, wr, br)


def _store_token_tiles(ref2d, val):
    rows, tiles = val.shape[0], val.shape[1] // LANE
    for c in range(tiles):
        ref2d[pl.ds(c, rows, stride=tiles), :] = val[:, c * LANE:(c + 1) * LANE]


def _load_token_tiles(ref2d, first_row, rows, tiles=TOK_TILES):
    return jnp.concatenate(
        [ref2d[pl.ds(first_row * tiles + c, rows, stride=tiles), :] for c in range(tiles)], axis=1)


def _pack_bf16_pairs(x):
    half = x.shape[1] // 2
    bits = lax.bitcast_convert_type(x.astype(BF16).astype(F32), jnp.uint32)
    return (bits[:, :half] >> 16) | (bits[:, half:] & jnp.uint32(0xFFFF0000))


def _unpack_bf16_pairs(w):
    lo = lax.bitcast_convert_type(w << 16, F32)
    hi = lax.bitcast_convert_type(w & jnp.uint32(0xFFFF0000), F32)
    return jnp.concatenate([lo, hi], axis=1).astype(BF16)


def _expert_weight_copies(e, ws, wup_hbm, wdn_hbm, wup_buf, wdn_buf, wsems):
    return (pltpu.make_async_copy(wup_hbm.at[e], wup_buf.at[ws], wsems.at[ws]),
            pltpu.make_async_copy(wdn_hbm.at[e], wdn_buf.at[ws], wsems.at[ws]))


def _expert_body(be_ref, nu_ref, first_ref, wslot_ref, next_ref, x_ref,
                 wup_hbm, bup_ref, wdn_hbm, bdn_ref, y_ref, wup_buf, wdn_buf, wsems, wup_bf, wdn_bf):
    i = pl.program_id(0)
    n_used = nu_ref[0]
    ws = wslot_ref[i]
    weight_copies = functools.partial(_expert_weight_copies, wup_hbm=wup_hbm, wdn_hbm=wdn_hbm, wup_buf=wup_buf,
                                      wdn_buf=wdn_buf, wsems=wsems)

    @pl.when((i == 0) & (n_used > 0))
    def _():
        for cp in weight_copies(be_ref[0], ws):
            cp.start(priority=EXPERT_WEIGHT_QUEUE)

    @pl.when(i < n_used)
    def _():
        @pl.when(first_ref[i] == 1)
        def _():
            for cp in weight_copies(be_ref[i], ws):
                cp.wait()

            @pl.when(next_ref[i] >= 0)
            def _():
                for cp in weight_copies(next_ref[i], 1 - ws):
                    cp.start(priority=EXPERT_WEIGHT_QUEUE)

            for r in range(0, D_MODEL, WEIGHT_CAST_ROWS):
                wup_bf[r:r + WEIGHT_CAST_ROWS, :] = wup_buf[ws, r:r + WEIGHT_CAST_ROWS, :].astype(BF16)
            for r in range(0, D_FF, WEIGHT_CAST_ROWS):
                wdn_bf[r:r + WEIGHT_CAST_ROWS, :] = wdn_buf[ws, r:r + WEIGHT_CAST_ROWS, :].astype(BF16)

        x = _unpack_bf16_pairs(_load_token_tiles(x_ref, 0, EXPERT_ROWS, PACK_TILES))
        gu = _dot(x, wup_bf[...]) + bup_ref[...]
        gate = jnp.minimum(gu[:, :D_FF], SWIGLU_LIMIT)
        up = jnp.clip(gu[:, D_FF:], -SWIGLU_LIMIT, SWIGLU_LIMIT)
        a = (up + 1.0) * gate * jax.nn.sigmoid(SWIGLU_ALPHA * gate)
        _store_token_tiles(y_ref, _dot(a, wdn_bf[...]) + bdn_ref[...])

    @pl.when(i >= n_used)
    def _():
        y_ref[...] = jnp.zeros_like(y_ref)


def _experts(block_meta, xs_2d, w_up, b_up, w_down, b_down):
    n_blocks = block_meta[0].shape[0]
    grid_spec = pltpu.PrefetchScalarGridSpec(
        num_scalar_prefetch=len(block_meta),
        grid=(n_blocks,),
        in_specs=[
            pl.BlockSpec((EXPERT_ROWS * PACK_TILES, LANE), lambda i, *_: (i, 0)),
            pl.BlockSpec(memory_space=pl.ANY),
            pl.BlockSpec((None, 1, 2 * D_FF), lambda i, be, *_: (be[i], 0, 0)),
            pl.BlockSpec(memory_space=pl.ANY),
            pl.BlockSpec((None, 1, D_MODEL), lambda i, be, *_: (be[i], 0, 0)),
        ],
        out_specs=pl.BlockSpec((EXPERT_ROWS * TOK_TILES, LANE), lambda i, *_: (i, 0)),
        scratch_shapes=[
            pltpu.VMEM((2, D_MODEL, 2 * D_FF), F32),
            pltpu.VMEM((2, D_FF, D_MODEL), F32),
            pltpu.SemaphoreType.DMA((2,)),
            pltpu.VMEM((D_MODEL, 2 * D_FF), BF16),
            pltpu.VMEM((D_FF, D_MODEL), BF16),
        ],
    )
    return pl.pallas_call(
        _expert_body,
        grid_spec=grid_spec,
        out_shape=jax.ShapeDtypeStruct((n_blocks * EXPERT_ROWS * TOK_TILES, LANE), F32),
        compiler_params=pltpu.CompilerParams(dimension_semantics=("arbitrary",),
                                             vmem_limit_bytes=VMEM_LIMIT),
        name="experts",
    )(*block_meta, xs_2d, w_up, b_up.reshape(N_EXPERTS, 1, 2 * D_FF), w_down,
      b_down.reshape(N_EXPERTS, 1, D_MODEL))


def _dispatch(h_tiles, dest_kmajor, n_rows):
    n_tok = h_tiles.shape[0]
    info = plsc.get_sparse_core_info()
    n_workers = info.num_cores * info.num_subcores
    per_worker = n_tok // n_workers
    chunk = next(c for c in (128, 96, 88, 64, 48, 32, 16, 8) if per_worker % c == 0)
    assert n_tok % n_workers == 0 and per_worker % SUBLANE == 0
    mesh = plsc.VectorSubcoreMesh(core_axis_name="c", subcore_axis_name="s")

    @functools.partial(
        pl.kernel, mesh=mesh,
        out_type=jax.ShapeDtypeStruct((n_rows,) + h_tiles.shape[1:], h_tiles.dtype),
        scratch_types=[pltpu.VMEM((TOP_K, chunk), jnp.int32), pltpu.VMEM((chunk,) + h_tiles.shape[1:], h_tiles.dtype),
                       pltpu.SemaphoreType.DMA],
    )
    def dispatch(h_hbm, dest_hbm, out_hbm, idx_v, rows_v, sem):
        wid = lax.axis_index("s") * info.num_cores + lax.axis_index("c")

        def step(j, carry):
            t0 = pl.multiple_of(wid * per_worker + j * chunk, SUBLANE)
            loads = [pltpu.async_copy(h_hbm.at[pl.ds(t0, chunk)], rows_v, sem)]
            for k in range(TOP_K):
                loads.append(pltpu.async_copy(
                    dest_hbm.at[pl.ds(pl.multiple_of(k * n_tok + t0, SUBLANE), chunk)], idx_v.at[k], sem))
            for cp in loads:
                cp.wait()
            stores = [pltpu.async_copy(rows_v, out_hbm.at[idx_v.at[k]], sem) for k in range(TOP_K)]
            for cp in stores:
                cp.wait()
            return carry

        lax.fori_loop(0, per_worker // chunk, step, 0)

    return dispatch(h_tiles, dest_kmajor)


def _gather_rows(src_tiles, idx_ref, n_rows, dst2d, sem, priorities):
    def issue(j, carry):
        for u in range(DMA_ISSUE_UNROLL):
            r = j * DMA_ISSUE_UNROLL + u
            dst = dst2d.at[pl.ds(pl.multiple_of(r * TOK_TILES, TOK_TILES), TOK_TILES), :]
            pltpu.make_async_copy(src_tiles.at[idx_ref[0, r]], dst, sem).start(
                priority=priorities[u % len(priorities)])
        return carry

    lax.fori_loop(0, n_rows // DMA_ISSUE_UNROLL, issue, 0)


def _wait_rows(src2d, n_rows, dst2d, sem):
    pltpu.make_async_copy(src2d.at[pl.ds(0, n_rows * TOK_TILES), :], dst2d, sem).wait()


def _combine_body(dest_ref, dest_next_ref, rt_ref, y_tiles, y_2d, x1_ref, g_ref, op_ref, os_ref, ybuf, sems,
                  *, n_p_blocks):
    i = pl.program_id(0)
    slot = i % 2
    n_rows = TOP_K * COMBINE_ROWS

    @pl.when(i == 0)
    def _():
        _gather_rows(y_tiles, dest_ref, n_rows, ybuf.at[0], sems.at[0], COMBINE_GATHER_QUEUES)

    _wait_rows(y_2d, n_rows, ybuf.at[slot], sems.at[slot])

    @pl.when(i + 1 < pl.num_programs(0))
    def _():
        _gather_rows(y_tiles, dest_next_ref, n_rows, ybuf.at[1 - slot], sems.at[1 - slot], COMBINE_GATHER_QUEUES)

    buf = ybuf.at[slot]
    moe = _load_token_tiles(buf, 0, COMBINE_ROWS) * rt_ref[:, 2 * TOP_K:2 * TOP_K + 1]
    for k in range(1, TOP_K):
        moe = moe + _load_token_tiles(buf, k * COMBINE_ROWS, COMBINE_ROWS) * rt_ref[:, 2 * TOP_K + k:2 * TOP_K + k + 1]
    res = _rms(x1_ref[...] + moe, g_ref[...])

    @pl.when(i < n_p_blocks)
    def _():
        op_ref[...] = res

    @pl.when(i >= n_p_blocks)
    def _():
        os_ref[...] = res


def _combine(dest_b, rt, y_2d, x1, g, n_p):
    n = x1.shape[0]
    n_blk = n // COMBINE_ROWS
    n_p_blocks = n_p // COMBINE_ROWS
    dest_blocks = dest_b.reshape(n_blk, 1, TOP_K * COMBINE_ROWS)
    return pl.pallas_call(
        functools.partial(_combine_body, n_p_blocks=n_p_blocks),
        grid=(n_blk,),
        in_specs=[
            pl.BlockSpec((None, 1, COMBINE_ROWS * TOP_K), lambda i: (i, 0, 0), memory_space=pltpu.SMEM),
            pl.BlockSpec((None, 1, COMBINE_ROWS * TOP_K), lambda i: (jnp.minimum(i + 1, n_blk - 1), 0, 0),
                         memory_space=pltpu.SMEM),
            pl.BlockSpec((COMBINE_ROWS, LANE), lambda i: (i, 0)),
            pl.BlockSpec(memory_space=pl.ANY),
            pl.BlockSpec(memory_space=pl.ANY),
            pl.BlockSpec((COMBINE_ROWS, D_MODEL), lambda i: (i, 0)),
            pl.BlockSpec((1, D_MODEL), lambda i: (0, 0)),
        ],
        out_specs=_group_specs(COMBINE_ROWS, D_MODEL, n_p_blocks),
        out_shape=[jax.ShapeDtypeStruct((n_p, D_MODEL), F32), jax.ShapeDtypeStruct((n - n_p, D_MODEL), F32)],
        scratch_shapes=[pltpu.VMEM((2, TOP_K * COMBINE_ROWS * TOK_TILES, LANE), F32),
                        pltpu.SemaphoreType.DMA((2,))],
        compiler_params=pltpu.CompilerParams(dimension_semantics=("arbitrary",),
                                             vmem_limit_bytes=VMEM_LIMIT),
        name="combine",
    )(dest_blocks, dest_blocks, rt, y_2d.reshape(-1, TOK_TILES, LANE), y_2d, x1, g)


def _plan_body(rtt_ref, cnt_ref, dk_ref, db_ref, meta_ref, pst):
    i = pl.program_id(0)
    sh = _log2(EXPERT_ROWS)
    n_e = N_EXPERTS

    @pl.when(i == 0)
    def _():
        cnt = cnt_ref[...].astype(jnp.int32)
        padded = (((cnt + (EXPERT_ROWS - 1)) >> sh) << sh).astype(F32)
        e_r = lax.broadcasted_iota(jnp.int32, (n_e, n_e), 0)
        e_c = lax.broadcasted_iota(jnp.int32, (n_e, n_e), 1)
        p_t = jnp.broadcast_to(padded, (n_e, n_e)).T
        pend = jnp.sum(jnp.where(e_c <= e_r, p_t, 0.0), axis=1, keepdims=True)
        pst[...] = pend - padded
        has_rows = p_t > 0.0
        group = jnp.sum(jnp.where((e_c <= e_r) & has_rows, 1.0, 0.0), axis=1, keepdims=True) - 1.0
        nxt = jnp.min(jnp.where((e_c > e_r) & has_rows, e_c, n_e), axis=1, keepdims=True)
        nxt = jnp.where(nxt >= n_e, -1, nxt)

        mb = meta_ref.shape[1]
        blk = lax.broadcasted_iota(jnp.int32, (n_e, mb), 1)
        eb = lax.broadcasted_iota(jnp.int32, (n_e, mb), 0)
        first_row = (blk * EXPERT_ROWS).astype(F32)

        def expert_of(row0):
            return jnp.minimum(jnp.sum(jnp.where(pend <= row0, 1, 0), axis=0, keepdims=True), n_e - 1)

        be = expert_of(first_row)
        be_prev = expert_of(first_row - EXPERT_ROWS)
        hit = eb == be
        wslot = jnp.sum(jnp.where(hit, group, 0.0), axis=0, keepdims=True).astype(jnp.int32) & 1
        nx = jnp.sum(jnp.where(hit, nxt, 0), axis=0, keepdims=True)
        n_used = pend[n_e - 1:n_e, :].astype(jnp.int32) >> sh
        lane = lax.broadcasted_iota(jnp.int32, (1, mb), 1)
        first = (((be != be_prev) | (lane == 0)) & (lane < n_used)).astype(jnp.int32)
        row8 = lax.broadcasted_iota(jnp.int32, (SUBLANE, mb), 0)
        meta = jnp.where(row8 == 0, be, jnp.where(row8 == 1, first, jnp.where(row8 == 2, wslot,
                         jnp.where(row8 == 3, nx, n_used))))
        meta_ref[...] = meta

    tm = rtt_ref.shape[1]
    eid = lax.broadcasted_iota(jnp.int32, (n_e, tm), 0).astype(F32)
    row8 = lax.broadcasted_iota(jnp.int32, (SUBLANE, tm), 0)
    d8 = jnp.zeros((SUBLANE, tm), jnp.int32)
    for k in range(TOP_K):
        start = jnp.sum(jnp.where(eid == rtt_ref[k:k + 1, :], pst[...], 0.0), axis=0, keepdims=True)
        d8 = jnp.where(row8 == k, (start + rtt_ref[TOP_K + k:TOP_K + k + 1, :]).astype(jnp.int32), d8)
    dk_ref[...] = d8[:TOP_K]
    for b in range(tm // COMBINE_ROWS):
        db_ref[b] = d8[:TOP_K, b * COMBINE_ROWS:(b + 1) * COMBINE_ROWS]


def _plan(rtt, cnt):
    n = rtt.shape[1]
    n_rows = n * TOP_K + N_EXPERTS * EXPERT_ROWS
    n_blocks = n_rows // EXPERT_ROWS
    mb = -(-n_blocks // LANE) * LANE
    tile = max(t for t in range(COMBINE_ROWS, PLAN_TILE_MAX + 1, COMBINE_ROWS) if n % t == 0)
    dk, db, meta = pl.pallas_call(
        _plan_body,
        grid=(n // tile,),
        in_specs=[pl.BlockSpec((2 * TOP_K, tile), lambda i: (0, i)),
                  pl.BlockSpec((N_EXPERTS, 1), lambda i: (0, 0))],
        out_specs=[pl.BlockSpec((TOP_K, tile), lambda i: (0, i)),
                   pl.BlockSpec((tile // COMBINE_ROWS, TOP_K, COMBINE_ROWS), lambda i: (i, 0, 0)),
                   pl.BlockSpec((SUBLANE, mb), lambda i: (0, 0))],
        out_shape=[jax.ShapeDtypeStruct((TOP_K, n), jnp.int32),
                   jax.ShapeDtypeStruct((n // COMBINE_ROWS, TOP_K, COMBINE_ROWS), jnp.int32),
                   jax.ShapeDtypeStruct((SUBLANE, mb), jnp.int32)],
        scratch_shapes=[pltpu.VMEM((N_EXPERTS, 1), F32)],
        compiler_params=pltpu.CompilerParams(dimension_semantics=("arbitrary",)),
        name="plan",
    )(rtt, cnt)
    block_meta = (meta[0, :n_blocks], meta[4, 0:1], meta[1, :n_blocks], meta[2, :n_blocks], meta[3, :n_blocks])
    return dk, db, n_rows, block_meta


def _pad_lanes(v, width):
    return jnp.zeros((1, width), F32).at[0, :v.shape[0]].set(v.astype(F32))


def kernel(x_prompt, x_sample, state_gdn_conv, state_gdn, state_gla, rms_mix_w, w_in, conv_w, gdn_a_log,
           gdn_dt_bias, gdn_norm_w, gla_gk_w, gla_gk_b, gla_norm_w, w_out, rms_ffn_w, w_router, b_router,
           w_up, b_up, w_down, b_down, rms_final_w):
    bp, tp, d = x_prompt.shape
    bs, ts, _ = x_sample.shape
    n_p, n_s = bp * tp, bs * ts
    assert d == D_MODEL and state_gdn.shape[0] == 1, "single-layer kernel"
    assert tp >= CONV_WIDTH - 1 and ts >= CONV_WIDTH - 1, "new conv state is taken from the new tokens only"
    l = 0

    wi = w_in[l]
    a0 = GDN_CONV_CH + GDN_V_W
    g0 = a0 + 2 * GDN_HEADS
    lr0 = g0 + 2 * GLA_QK_W + 2 * GLA_V_W
    small = jnp.concatenate([wi[:, a0:a0 + 2 * GDN_HEADS], wi[:, lr0:lr0 + GLA_GATE_RANK],
                             jnp.zeros((d, SM_W - 2 * GDN_HEADS - GLA_GATE_RANK), F32)], axis=1)
    w_big = jnp.concatenate([wi[:, :a0], wi[:, g0:lr0], small], axis=1).astype(BF16)
    alog = _pad_lanes(gdn_a_log[l], SM_W)
    dtb = _pad_lanes(gdn_dt_bias[l], SM_W)
    wgk = jnp.zeros((SM_W, GLA_QK_W), F32).at[SM_LR:SM_LR + GLA_GATE_RANK].set(gla_gk_w[l])
    wr = jnp.zeros((d, LANE), F32).at[:, :N_EXPERTS].set(w_router[l])
    br = jnp.full((1, LANE), -1e30, F32).at[0, :N_EXPERTS].set(b_router[l])

    assert n_p % ROW_TILE == 0 and n_s % ROW_TILE == 0
    x_p, x_s = x_prompt.reshape(n_p, d), x_sample.reshape(n_s, d)
    proj = _inproj(x_p, x_s, rms_mix_w[l][None, :], w_big)

    tb_p = PROMPT_TIME_BLOCK
    zeros_conv = jnp.zeros((bp, CONV_WIDTH - 1, GDN_CONV_CH), F32)
    og_p, gdn_p, conv_p = _gdn(proj, bp, 1, tp, tb_p, CHUNK, CHUNK, zeros_conv,
                               jnp.zeros((bp, GDN_HEADS, GDN_DK, GDN_DV), F32), conv_w[l], alog, dtb,
                               gdn_norm_w[l][None, :])
    ol_p, gla_p = _gla(proj, bp, 1, tp, tb_p, CHUNK, CHUNK, jnp.zeros((bp, GLA_HEADS, GLA_DK, GLA_DV), F32),
                       wgk, gla_gk_b[l][None, :], gla_norm_w[l][None, :])

    ts_pad = SUBLANE
    nb_s = SAMPLE_SEQS_PER_STEP
    proj_s = proj[n_p:].reshape(bs, ts, PROJ_W)
    proj_sp = jnp.pad(proj_s, ((0, 0), (0, ts_pad - ts), (0, 0))).reshape(bs * ts_pad, PROJ_W)
    og_s, gdn_s, conv_s = _gdn(proj_sp, bs, nb_s, ts_pad, ts_pad, ts_pad, ts, state_gdn_conv[l], state_gdn[l],
                               conv_w[l], alog, dtb, gdn_norm_w[l][None, :])
    ol_s, gla_s = _gla(proj_sp, bs, nb_s, ts_pad, ts_pad, ts_pad, ts, state_gla[l], wgk, gla_gk_b[l][None, :],
                       gla_norm_w[l][None, :])
    og_s = og_s.reshape(bs, ts_pad, GDN_V_W)[:, :ts].reshape(n_s, GDN_V_W)
    ol_s = ol_s.reshape(bs, ts_pad, GLA_V_W)[:, :ts].reshape(n_s, GLA_V_W)

    x1, h2, rt, rtt, cnt = _outproj(og_p, og_s, ol_p, ol_s, x_p, x_s, w_out[l].astype(BF16),
                                    rms_ffn_w[l][None, :], wr, br)

    dest_k, dest_b, n_rows, block_meta = _plan(rtt, cnt)
    xs = _dispatch(h2.reshape(-1, PACK_TILES, LANE), dest_k.reshape(-1), n_rows)
    y_rows = _experts(block_meta, xs.reshape(-1, LANE), w_up[l], b_up[l], w_down[l], b_down[l])
    y_p, y_s = _combine(dest_b, rt, y_rows, x1, rms_final_w[None, :], n_p)
    y_prompt = y_p.reshape(bp, tp, d)
    y_sample = y_s.reshape(bs, ts, d)
    return (y_prompt, y_sample, conv_p[None], gdn_p[None], gla_p[None], conv_s[None], gdn_s[None], gla_s[None])
```

```python
import functools

import jax
import jax.numpy as jnp
from jax import lax
from jax.experimental import pallas as pl
from jax.experimental.pallas import tpu as pltpu
from jax.experimental.pallas import tpu_sc as plsc

F32 = jnp.float32
BF16 = jnp.bfloat16
HI = lax.Precision.HIGHEST

D_MODEL = 1024
GDN_HEADS = 4
GDN_DK = 128
GDN_DV = 128
GLA_HEADS = 4
GLA_DK = 64
GLA_DV = 128
GLA_GATE_RANK = 16
GLA_GATE_NORMALIZER = 16.0
CONV_WIDTH = 4
CHUNK = 64
N_EXPERTS = 32
TOP_K = 4
D_FF = 1024
SWIGLU_LIMIT = 7.0
SWIGLU_ALPHA = 1.702
RMS_EPS = 1e-6
L2_EPS = 1e-6

GDN_QK_W = GDN_HEADS * GDN_DK
GDN_V_W = GDN_HEADS * GDN_DV
GDN_CONV_CH = 2 * GDN_QK_W + GDN_V_W
GLA_QK_W = GLA_HEADS * GLA_DK
GLA_V_W = GLA_HEADS * GLA_DV

COL_QKV = 0
COL_Z = 1536
COL_GQ = 2048
COL_GK = 2304
COL_GV = 2560
COL_GG = 3072
COL_SM = 3584
SM_W = 128
PROJ_W = COL_SM + SM_W
SM_A, SM_B, SM_LR = 0, 4, 8

LANE = 128
SUBLANE = 8
TOK_TILES = D_MODEL // LANE
PACK_TILES = TOK_TILES // 2
ROW_TILE = 512
EXPERT_ROWS = 512
EXPERT_WEIGHT_QUEUE = 1
WEIGHT_CAST_ROWS = 128
COMBINE_ROWS = 256
PLAN_TILE_MAX = 2048
DMA_ISSUE_UNROLL = 8
COMBINE_GATHER_QUEUES = (0, 1)
CONV_ROW_SLAB = 128
GDN_CHUNKS_PER_TRIP = 4
GLA_CHUNKS_PER_TRIP = 4
SCAN_CHUNKS_PER_TRIP = 8
PROMPT_TIME_BLOCK = 512
SAMPLE_SEQS_PER_STEP = 8
VMEM_LIMIT = 56 * 1024 * 1024


def _dot(a, b):
    return jnp.dot(a.astype(BF16), b.astype(BF16), preferred_element_type=F32)


def _dot_nt(a, b):
    return lax.dot_general(a.astype(BF16), b.astype(BF16), (((1,), (1,)), ((), ())),
                           preferred_element_type=F32)


def _dot_tn(a, b):
    return lax.dot_general(a.astype(BF16), b.astype(BF16), (((0,), (0,)), ((), ())),
                           preferred_element_type=F32)


def _dot_hi(a, b):
    return jnp.dot(a, b, precision=HI, preferred_element_type=F32)


def _dot_3pass(a, b):
    a_hi = a.astype(BF16)
    b_hi = b.astype(BF16)
    a_lo = (a - a_hi.astype(F32)).astype(BF16)
    b_lo = (b - b_hi.astype(F32)).astype(BF16)

    def mm(x, y):
        return jnp.dot(x, y, preferred_element_type=F32)

    return (mm(a_lo, b_hi) + mm(a_hi, b_lo)) + mm(a_hi, b_hi)


def _rms(x, w):
    return x * lax.rsqrt(jnp.mean(x * x, axis=-1, keepdims=True) + RMS_EPS) * w


def _silu(x):
    return x * jax.nn.sigmoid(x)


def _group_specs(rows, width, n_p_blocks):
    return [pl.BlockSpec((rows, width), lambda i: (jnp.minimum(i, n_p_blocks - 1), 0)),
            pl.BlockSpec((rows, width), lambda i: (jnp.maximum(i - n_p_blocks, 0), 0))]


def _group_pick(i, n_p_blocks, p_ref, s_ref):
    return jnp.where(i < n_p_blocks, p_ref[...], s_ref[...])


def _inproj_body(xp_ref, xs_ref, g_ref, w_ref, o_ref, *, n_p_blocks):
    x = _group_pick(pl.program_id(0), n_p_blocks, xp_ref, xs_ref)
    h = _rms(x, g_ref[...])
    o_ref[...] = jnp.dot(h.astype(BF16), w_ref[...], preferred_element_type=F32)


def _inproj(x_p, x_s, g, w):
    n_p_blocks, n_s_blocks = x_p.shape[0] // ROW_TILE, x_s.shape[0] // ROW_TILE
    n = x_p.shape[0] + x_s.shape[0]
    return pl.pallas_call(
        functools.partial(_inproj_body, n_p_blocks=n_p_blocks),
        grid=(n_p_blocks + n_s_blocks,),
        in_specs=_group_specs(ROW_TILE, D_MODEL, n_p_blocks) + [
            pl.BlockSpec((1, D_MODEL), lambda i: (0, 0)),
            pl.BlockSpec((D_MODEL, PROJ_W), lambda i: (0, 0)),
        ],
        out_specs=pl.BlockSpec((ROW_TILE, PROJ_W), lambda i: (i, 0)),
        out_shape=jax.ShapeDtypeStruct((n, PROJ_W), F32),
        compiler_params=pltpu.CompilerParams(dimension_semantics=("arbitrary",),
                                             vmem_limit_bytes=VMEM_LIMIT),
        name="in_proj",
    )(x_p, x_s, g, w)


def _log2(n):
    assert n & (n - 1) == 0
    return n.bit_length() - 1


def _tri_inv_all(ms, c, ii, jj):
    eye = (ii == jj).astype(F32)
    base = min(c, 8)
    sh = _log2(base)
    blk = (ii >> sh) == (jj >> sh)
    ns = [jnp.where(blk, m, 0.0) for m in ms]
    xs = [eye - n for n in ns]
    ps = [_dot(n, n) for n in ns]
    ts = [_dot(jnp.concatenate([x, p], axis=0), p) for x, p in zip(xs, ps)]
    xs = [x + t[:c] for x, t in zip(xs, ts)]
    ps = [t[c:] for t in ts]
    xs = [x + _dot(x, p) for x, p in zip(xs, ps)]
    s = base
    while s < c:
        sh_s, sh_b = _log2(s), _log2(2 * s)
        off = ((ii >> sh_b) == (jj >> sh_b)) & ((ii >> sh_s) != (jj >> sh_s))
        ys = [_dot(x, jnp.where(off, m, 0.0)) for x, m in zip(xs, ms)]
        xs = [x - _dot(y, x) for x, y in zip(xs, ys)]
        s *= 2
    return xs


def _gated_norm(o, w, z):
    return o * lax.rsqrt(jnp.mean(o * o, axis=-1, keepdims=True) + RMS_EPS) * w * _silu(z)


def _chunk_rows(s, tb_rows, ci, c):
    r = s * tb_rows + ci * c
    if not isinstance(r, int):
        r = pl.multiple_of(r, c)
    return r


def _for_chunks(n_chunks, step):
    if n_chunks == 1:
        step(0, 0)
    else:
        lax.fori_loop(0, n_chunks, step, 0)


def _gdn_body(qkv_ref, z_ref, sm_ref, cbuf_ref, s0_ref, cw_ref, alog_ref, dtb_ref, nw_ref,
              o_ref, sout_ref, cout_ref, st, xc, act, gcs, us, wss, qgs, kds, aqs,
              *, nb, tb_rows, chunk, valid, n_tb):
    tb = pl.program_id(1)
    c = chunk
    n_heads = GDN_HEADS
    tail = CONV_WIDTH - 1
    pad = SUBLANE
    units = [(s, h) for s in range(nb) for h in range(n_heads)]

    n_slabs = GDN_CONV_CH // LANE

    def lanes(j):
        return slice(j * LANE, (j + 1) * LANE)

    @pl.when(tb == 0)
    def _():
        st[...] = s0_ref[...]
        for s in range(nb):
            for j in range(n_slabs):
                xc[s, j, pad - tail:pad, :] = cbuf_ref[s, :, lanes(j)]

    if n_tb > 1:
        @pl.when(tb > 0)
        def _():
            for s in range(nb):
                for j in range(n_slabs):
                    xc[s, j, pad - tail:pad, :] = xc[s, j, tb_rows + pad - tail:tb_rows + pad, :]

    for s in range(nb):
        for j in range(n_slabs):
            xc[s, j, pad:pad + tb_rows, :] = qkv_ref[s * tb_rows:(s + 1) * tb_rows, lanes(j)]

    row_slab = min(tb_rows, CONV_ROW_SLAB)
    parities = 2 if row_slab >= 2 * SUBLANE else 1
    for s in range(nb):
        for j in range(n_slabs):
            src, dst = xc.at[s, j], act.at[j]
            for sl in range(tb_rows // row_slab):
                for p in range(parities):
                    lo = pad - tail + sl * row_slab + p
                    out0 = s * tb_rows + sl * row_slab + p

                    def rows_from(start):
                        if parities == 1:
                            return pl.ds(start, row_slab)
                        return pl.ds(start, row_slab // 2, stride=2)

                    acc = src[rows_from(lo), :] * cw_ref[0:1, lanes(j)]
                    for i in range(1, CONV_WIDTH):
                        acc = acc + src[rows_from(lo + i), :] * cw_ref[i:i + 1, lanes(j)]
                    dst[rows_from(out0), :] = _silu(acc)

    ii = lax.broadcasted_iota(jnp.int32, (c, c), 0)
    jj = lax.broadcasted_iota(jnp.int32, (c, c), 1)
    lower = (ii >= jj)
    lower_f = lower.astype(F32)
    strict = (ii > jj)
    rowmask = None
    if valid < c:
        rowmask = lax.broadcasted_iota(jnp.int32, (c, 1), 0) < valid

    def hs(h, w):
        return slice(h * w, (h + 1) * w)

    n_chunks = tb_rows // c
    cpi = next(k for k in (GDN_CHUNKS_PER_TRIP, 2, 1) if n_chunks % k == 0)
    p1_units = [(g, h) for g in range(nb * cpi) for h in range(n_heads)]

    def phase1(ci, carry):
        rows, b_ts, gc_ts, gc_tts = [], [], [], []
        for g in range(nb * cpi):
            rr = pl.ds(_chunk_rows(g // cpi, tb_rows, ci * cpi + g % cpi, c), c)
            sm = sm_ref[rr, :]
            g_t = -jnp.exp(alog_ref[...]) * jax.nn.softplus(sm + dtb_ref[...])
            b_t = jax.nn.sigmoid(sm)
            if rowmask is not None:
                g_t = jnp.where(rowmask, g_t, 0.0)
                b_t = jnp.where(rowmask, b_t, 0.0)
            gc_t = _dot_hi(lower_f, g_t)
            gcs[rr, :] = gc_t
            rows.append(rr)
            b_ts.append(b_t)
            gc_ts.append(gc_t)
            gc_tts.append(gc_t.T)
        qn, kn, kb, vb = {}, {}, {}, {}
        for (s, h) in p1_units:
            q = act[h, rows[s], :]
            k = act[n_heads + h, rows[s], :]
            v = act[2 * n_heads + h, rows[s], :]
            if rowmask is not None:
                q = jnp.where(rowmask, q, 0.0)
                k = jnp.where(rowmask, k, 0.0)
                v = jnp.where(rowmask, v, 0.0)
            qn[s, h] = q * lax.rsqrt(jnp.sum(q * q, axis=-1, keepdims=True) + L2_EPS) * (GDN_DK ** -0.5)
            kn[s, h] = k * lax.rsqrt(jnp.sum(k * k, axis=-1, keepdims=True) + L2_EPS)
            beta = b_ts[s][:, SM_B + h:SM_B + h + 1]
            kb[s, h] = kn[s, h] * beta
            vb[s, h] = v * beta
        s1 = {u: _dot_nt(jnp.concatenate([kb[u], qn[u]], axis=0), kn[u]) for u in p1_units}
        mm = []
        for (s, h) in p1_units:
            gcol = gc_ts[s][:, SM_A + h:SM_A + h + 1]
            grow = gc_tts[s][SM_A + h:SM_A + h + 1, :]
            dec = jnp.exp(jnp.where(lower, gcol - grow, -jnp.inf))
            mm.append(jnp.where(strict, s1[s, h][:c] * dec, 0.0))
            aqs[h, rows[s], :] = s1[s, h][c:] * dec
        tms = _tri_inv_all(mm, c, ii, jj)
        for (s, h), tm in zip(p1_units, tms):
            gcol = gc_ts[s][:, SM_A + h:SM_A + h + 1]
            eg = jnp.exp(gcol)
            uw = _dot(tm, jnp.concatenate([vb[s, h], kb[s, h] * eg], axis=1))
            us[rows[s], hs(h, GDN_DV)] = uw[:, :GDN_DV]
            wss[rows[s], hs(h, GDN_DV)] = uw[:, GDN_DV:]
            qgs[rows[s], hs(h, GDN_DK)] = qn[s, h] * eg
            kds[rows[s], hs(h, GDN_DK)] = kn[s, h] * jnp.exp(gcol[c - 1:c, :] - gcol)
        return carry

    cp2 = next(k for k in (SCAN_CHUNKS_PER_TRIP, 1) if n_chunks % k == 0)

    def phase2(ti, carry):
        for u in range(cp2):
            scan_chunk(ti * cp2 + u)
        return carry

    def scan_chunk(ci):
        r0 = [_chunk_rows(s, tb_rows, ci, c) for s in range(nb)]
        rows = [pl.ds(r, c) for r in r0]
        ws = {(s, h): _dot(jnp.concatenate([wss[rows[s], hs(h, GDN_DV)], qgs[rows[s], hs(h, GDN_DK)]], axis=0),
                           st[s, h]) for (s, h) in units}
        v_new = {(s, h): us[rows[s], hs(h, GDN_DV)] - ws[s, h][:c] for (s, h) in units}
        o = {(s, h): ws[s, h][c:] + _dot(aqs[h, rows[s], :], v_new[s, h]) for (s, h) in units}
        upd = {(s, h): _dot_tn(kds[rows[s], hs(h, GDN_DK)], v_new[s, h]) for (s, h) in units}
        for (s, h) in units:
            g_last = gcs[pl.ds(r0[s] + c - 1, 1), SM_A + h:SM_A + h + 1]
            st[s, h] = st[s, h] * jnp.exp(g_last) + upd[s, h]
        for s in range(nb):
            o_ref[rows[s], :] = jnp.concatenate(
                [_gated_norm(o[s, h], nw_ref[...], z_ref[rows[s], hs(h, GDN_DV)]) for h in range(n_heads)], axis=1)

    _for_chunks(n_chunks // cpi, phase1)
    _for_chunks(n_chunks // cp2, phase2)

    @pl.when(tb == n_tb - 1)
    def _():
        sout_ref[...] = st[...]
        last = tb_rows if valid == c else valid
        for s in range(nb):
            cout_ref[s] = jnp.concatenate(
                [xc[s, j, pad + last - tail:pad + last, :] for j in range(n_slabs)], axis=1)


def _gdn(proj, n_seq, nb, t_len, tb_rows, chunk, valid, conv_buf, s0, conv_w, alog, dtb, nw):
    n_tb = t_len // tb_rows
    assert nb == 1 or n_tb == 1
    rows = nb * tb_rows

    def rowblk(b, t):
        return b * n_tb + t

    body = functools.partial(_gdn_body, nb=nb, tb_rows=tb_rows, chunk=chunk, valid=valid, n_tb=n_tb)
    return pl.pallas_call(
        body,
        grid=(n_seq // nb, n_tb),
        in_specs=[
            pl.BlockSpec((rows, GDN_CONV_CH), lambda b, t: (rowblk(b, t), COL_QKV // GDN_CONV_CH)),
            pl.BlockSpec((rows, GDN_V_W), lambda b, t: (rowblk(b, t), COL_Z // GDN_V_W)),
            pl.BlockSpec((rows, SM_W), lambda b, t: (rowblk(b, t), COL_SM // SM_W)),
            pl.BlockSpec((nb, CONV_WIDTH - 1, GDN_CONV_CH), lambda b, t: (b, 0, 0)),
            pl.BlockSpec((nb, GDN_HEADS, GDN_DK, GDN_DV), lambda b, t: (b, 0, 0, 0)),
            pl.BlockSpec((CONV_WIDTH, GDN_CONV_CH), lambda b, t: (0, 0)),
            pl.BlockSpec((1, SM_W), lambda b, t: (0, 0)),
            pl.BlockSpec((1, SM_W), lambda b, t: (0, 0)),
            pl.BlockSpec((1, GDN_DV), lambda b, t: (0, 0)),
        ],
        out_specs=[
            pl.BlockSpec((rows, GDN_V_W), lambda b, t: (rowblk(b, t), 0)),
            pl.BlockSpec((nb, GDN_HEADS, GDN_DK, GDN_DV), lambda b, t: (b, 0, 0, 0)),
            pl.BlockSpec((nb, CONV_WIDTH - 1, GDN_CONV_CH), lambda b, t: (b, 0, 0)),
        ],
        out_shape=[
            jax.ShapeDtypeStruct((n_seq * t_len, GDN_V_W), F32),
            jax.ShapeDtypeStruct((n_seq, GDN_HEADS, GDN_DK, GDN_DV), F32),
            jax.ShapeDtypeStruct((n_seq, CONV_WIDTH - 1, GDN_CONV_CH), F32),
        ],
        scratch_shapes=[
            pltpu.VMEM((nb, GDN_HEADS, GDN_DK, GDN_DV), F32),
            pltpu.VMEM((nb, GDN_CONV_CH // LANE, tb_rows + SUBLANE, LANE), F32),
            pltpu.VMEM((GDN_CONV_CH // LANE, rows, LANE), F32),
            pltpu.VMEM((rows, SM_W), F32),
            pltpu.VMEM((rows, GDN_V_W), F32),
            pltpu.VMEM((rows, GDN_V_W), F32),
            pltpu.VMEM((rows, GDN_QK_W), F32),
            pltpu.VMEM((rows, GDN_QK_W), F32),
            pltpu.VMEM((GDN_HEADS, rows, chunk), F32),
        ],
        compiler_params=pltpu.CompilerParams(dimension_semantics=("arbitrary", "arbitrary"),
                                             vmem_limit_bytes=VMEM_LIMIT),
        name="gdn_mixer",
    )(proj, proj, proj, conv_buf, s0, conv_w, alog, dtb, nw)


def _gla_body(q_ref, k_ref, v_ref, go_ref, sm_ref, s0_ref, wgk_ref, bgk_ref, nw_ref,
              o_ref, sout_ref, st, qes, ois, upds, decs, *, nb, tb_rows, chunk, valid, n_tb):
    tb = pl.program_id(1)
    c = chunk
    n_heads = GLA_HEADS
    units = [(s, h) for s in range(nb) for h in range(n_heads)]

    @pl.when(tb == 0)
    def _():
        st[...] = s0_ref[...]

    ii = lax.broadcasted_iota(jnp.int32, (c, c), 0)
    jj = lax.broadcasted_iota(jnp.int32, (c, c), 1)
    lower = (ii >= jj)
    lower_f = lower.astype(F32)
    rid = lax.broadcasted_iota(jnp.int32, (c, 1), 0)
    rowmask = (rid < valid) if valid < c else None
    n_sub = max(c // 16, 1)
    sub = c // n_sub

    n_chunks = tb_rows // c
    cpi = next(k for k in (GLA_CHUNKS_PER_TRIP, 2, 1) if n_chunks % k == 0)
    p1_units = [(g, h) for g in range(nb * cpi) for h in range(n_heads)]

    def phase1(ci, carry):
        rows, slots, bcs, bc_ts = [], [], [], []
        for g in range(nb * cpi):
            chunk_idx = ci * cpi + g % cpi
            rr = pl.ds(_chunk_rows(g // cpi, tb_rows, chunk_idx, c), c)
            slots.append((g // cpi) * n_chunks + chunk_idx)
            gk = jax.nn.log_sigmoid(_dot(sm_ref[rr, :], wgk_ref[...]) + bgk_ref[...]) / GLA_GATE_NORMALIZER
            if rowmask is not None:
                gk = jnp.where(rowmask, gk, 0.0)
            bc = _dot_hi(lower_f, gk)
            rows.append(rr)
            bcs.append(bc)
            bc_ts.append(bc.T)
        q, k, v, bch = {}, {}, {}, {}
        for (s, h) in p1_units:
            ks = slice(h * GLA_DK, (h + 1) * GLA_DK)
            vs = slice(h * GLA_DV, (h + 1) * GLA_DV)
            q[s, h] = q_ref[rows[s], ks] * (GLA_DK ** -0.5)
            kk = k_ref[rows[s], ks]
            vv = v_ref[rows[s], vs]
            if rowmask is not None:
                kk = jnp.where(rowmask, kk, 0.0)
                vv = jnp.where(rowmask, vv, 0.0)
            k[s, h], v[s, h] = kk, vv
            bch[s, h] = bcs[s][:, ks]
        for (g, h) in p1_units:
            qes[h, rows[g], :] = q[g, h] * jnp.exp(bch[g, h])
        a = {}
        for u in p1_units:
            q_parts, k_parts = [], []
            for sb in range(n_sub):
                ref_row = bch[u][sb * sub:sb * sub + 1, :]
                in_blk = (rid >= sb * sub) & (rid < (sb + 1) * sub)
                q_parts.append(jnp.where(in_blk, q[u] * jnp.exp(jnp.where(in_blk, bch[u] - ref_row, 0.0)), 0.0))
                k_parts.append(k[u] * jnp.exp(jnp.where(rid < (sb + 1) * sub, ref_row - bch[u], 0.0)))
            q_hat = jnp.concatenate(q_parts, axis=1) if n_sub > 1 else q_parts[0]
            k_hat = jnp.concatenate(k_parts, axis=1) if n_sub > 1 else k_parts[0]
            a[u] = jnp.where(lower, _dot_nt(q_hat, k_hat), 0.0)
        upd = {u: _dot_tn(k[u] * jnp.exp(bch[u][c - 1:c, :] - bch[u]), v[u]) for u in p1_units}
        o_intra = {u: _dot(a[u], v[u]) for u in p1_units}
        for (g, h) in p1_units:
            dec_col = bc_ts[g][h * GLA_DK:(h + 1) * GLA_DK, c - 1:c]
            decs[slots[g], h] = jnp.broadcast_to(jnp.exp(dec_col), (GLA_DK, GLA_DV))
            upds[slots[g], h] = upd[g, h]
            ois[rows[g], h * GLA_DV:(h + 1) * GLA_DV] = o_intra[g, h]
        return carry

    cp2 = next(k for k in (SCAN_CHUNKS_PER_TRIP, 1) if n_chunks % k == 0)

    def phase2(ti, carry):
        for u in range(cp2):
            scan_chunk(ti * cp2 + u)
        return carry

    def scan_chunk(ci):
        rows = [pl.ds(_chunk_rows(s, tb_rows, ci, c), c) for s in range(nb)]
        o = {(s, h): ois[rows[s], h * GLA_DV:(h + 1) * GLA_DV] + _dot(qes[h, rows[s], :], st[s, h])
             for (s, h) in units}
        for (s, h) in units:
            st[s, h] = decs[s * n_chunks + ci, h] * st[s, h] + upds[s * n_chunks + ci, h]
        for s in range(nb):
            o_ref[rows[s], :] = jnp.concatenate(
                [_gated_norm(o[s, h], nw_ref[...], go_ref[rows[s], h * GLA_DV:(h + 1) * GLA_DV])
                 for h in range(n_heads)], axis=1)

    _for_chunks(n_chunks // cpi, phase1)
    _for_chunks(n_chunks // cp2, phase2)

    @pl.when(tb == n_tb - 1)
    def _():
        sout_ref[...] = st[...]


def _gla(proj, n_seq, nb, t_len, tb_rows, chunk, valid, s0, wgk, bgk, nw):
    n_tb = t_len // tb_rows
    assert nb == 1 or n_tb == 1
    rows = nb * tb_rows

    def rowblk(b, t):
        return b * n_tb + t

    body = functools.partial(_gla_body, nb=nb, tb_rows=tb_rows, chunk=chunk, valid=valid, n_tb=n_tb)
    return pl.pallas_call(
        body,
        grid=(n_seq // nb, n_tb),
        in_specs=[
            pl.BlockSpec((rows, GLA_QK_W), lambda b, t: (rowblk(b, t), COL_GQ // GLA_QK_W)),
            pl.BlockSpec((rows, GLA_QK_W), lambda b, t: (rowblk(b, t), COL_GK // GLA_QK_W)),
            pl.BlockSpec((rows, GLA_V_W), lambda b, t: (rowblk(b, t), COL_GV // GLA_V_W)),
            pl.BlockSpec((rows, GLA_V_W), lambda b, t: (rowblk(b, t), COL_GG // GLA_V_W)),
            pl.BlockSpec((rows, SM_W), lambda b, t: (rowblk(b, t), COL_SM // SM_W)),
            pl.BlockSpec((nb, GLA_HEADS, GLA_DK, GLA_DV), lambda b, t: (b, 0, 0, 0)),
            pl.BlockSpec((SM_W, GLA_QK_W), lambda b, t: (0, 0)),
            pl.BlockSpec((1, GLA_QK_W), lambda b, t: (0, 0)),
            pl.BlockSpec((1, GLA_DV), lambda b, t: (0, 0)),
        ],
        out_specs=[
            pl.BlockSpec((rows, GLA_V_W), lambda b, t: (rowblk(b, t), 0)),
            pl.BlockSpec((nb, GLA_HEADS, GLA_DK, GLA_DV), lambda b, t: (b, 0, 0, 0)),
        ],
        out_shape=[
            jax.ShapeDtypeStruct((n_seq * t_len, GLA_V_W), F32),
            jax.ShapeDtypeStruct((n_seq, GLA_HEADS, GLA_DK, GLA_DV), F32),
        ],
        scratch_shapes=[
            pltpu.VMEM((nb, GLA_HEADS, GLA_DK, GLA_DV), F32),
            pltpu.VMEM((GLA_HEADS, rows, GLA_DK), F32),
            pltpu.VMEM((rows, GLA_V_W), F32),
            pltpu.VMEM((rows // chunk, GLA_HEADS, GLA_DK, GLA_DV), F32),
            pltpu.VMEM((rows // chunk, GLA_HEADS, GLA_DK, GLA_DV), F32),
        ],
        compiler_params=pltpu.CompilerParams(dimension_semantics=("arbitrary", "arbitrary"),
                                             vmem_limit_bytes=VMEM_LIMIT),
        name="gla_mixer",
    )(proj, proj, proj, proj, proj, s0, wgk, bgk, nw)


def _outproj_body(ogp_ref, ogs_ref, olp_ref, ols_ref, xp_ref, xs_ref, wo_ref, g_ref, wr_ref, br_ref,
                  x1_ref, h2_ref, rt_ref, rtt_ref, cnt_ref, base, *, n_p_blocks):
    i = pl.program_id(0)

    @pl.when(i == 0)
    def _():
        base[...] = jnp.zeros_like(base)

    o = jnp.concatenate([_group_pick(i, n_p_blocks, ogp_ref, ogs_ref),
                         _group_pick(i, n_p_blocks, olp_ref, ols_ref)], axis=1)
    x1 = _group_pick(i, n_p_blocks, xp_ref, xs_ref) + jnp.dot(o.astype(BF16), wo_ref[...],
                                                               preferred_element_type=F32)
    x1_ref[...] = x1
    h = _rms(x1, g_ref[...])
    _store_token_tiles(h2_ref, _pack_bf16_pairs(h))
    logits = _dot_3pass(h, wr_ref[...]) + br_ref[...]

    tm = logits.shape[0]
    lt = logits.T[:N_EXPERTS]
    eid = lax.broadcasted_iota(jnp.int32, (N_EXPERTS, tm), 0)
    work = lt
    sel = jnp.zeros((N_EXPERTS, tm), F32)
    hits, ids, vals = [], [], []
    for _ in range(TOP_K):
        m = jnp.max(work, axis=0, keepdims=True)
        idx = jnp.min(jnp.where(work == m, eid, N_EXPERTS), axis=0, keepdims=True)
        hit = eid == idx
        hits.append(hit)
        ids.append(idx)
        vals.append(m)
        work = jnp.where(hit, -jnp.inf, work)
        sel = sel + hit.astype(F32)
    exps = [jnp.exp(v - vals[0]) for v in vals]
    den = exps[0]
    for e in exps[1:]:
        den = den + e
    gates = [e / den for e in exps]

    ri = lax.broadcasted_iota(jnp.int32, (tm, tm), 0)
    ci = lax.broadcasted_iota(jnp.int32, (tm, tm), 1)
    before = _dot(sel, (ri < ci).astype(F32)) + base[...]
    ranks = [jnp.sum(jnp.where(hit, before, 0.0), axis=0, keepdims=True) for hit in hits]
    base[...] = base[...] + jnp.sum(sel, axis=1, keepdims=True)
    cnt_ref[...] = base[...]

    row = lax.broadcasted_iota(jnp.int32, (LANE, tm), 0)
    rec = jnp.zeros((LANE, tm), F32)
    for k in range(TOP_K):
        rec = jnp.where(row == k, ids[k].astype(F32), rec)
        rec = jnp.where(row == TOP_K + k, ranks[k], rec)
        rec = jnp.where(row == 2 * TOP_K + k, gates[k], rec)
    rt_ref[...] = rec.T
    rtt_ref[...] = rec[:2 * TOP_K]


def _outproj(og_p, og_s, ol_p, ol_s, x_p, x_s, wo, g, wr, br):
    n_p_blocks, n_s_blocks = x_p.shape[0] // ROW_TILE, x_s.shape[0] // ROW_TILE
    n = x_p.shape[0] + x_s.shape[0]
    return pl.pallas_call(
        functools.partial(_outproj_body, n_p_blocks=n_p_blocks),
        grid=(n_p_blocks + n_s_blocks,),
        in_specs=_group_specs(ROW_TILE, GDN_V_W, n_p_blocks) + _group_specs(ROW_TILE, GLA_V_W, n_p_blocks)
        + _group_specs(ROW_TILE, D_MODEL, n_p_blocks) + [
            pl.BlockSpec((D_MODEL, D_MODEL), lambda i: (0, 0)),
            pl.BlockSpec((1, D_MODEL), lambda i: (0, 0)),
            pl.BlockSpec((D_MODEL, LANE), lambda i: (0, 0)),
            pl.BlockSpec((1, LANE), lambda i: (0, 0)),
        ],
        out_specs=[
            pl.BlockSpec((ROW_TILE, D_MODEL), lambda i: (i, 0)),
            pl.BlockSpec((ROW_TILE * PACK_TILES, LANE), lambda i: (i, 0)),
            pl.BlockSpec((ROW_TILE, LANE), lambda i: (i, 0)),
            pl.BlockSpec((2 * TOP_K, ROW_TILE), lambda i: (0, i)),
            pl.BlockSpec((N_EXPERTS, 1), lambda i: (0, 0)),
        ],
        out_shape=[
            jax.ShapeDtypeStruct((n, D_MODEL), F32),
            jax.ShapeDtypeStruct((n * PACK_TILES, LANE), jnp.uint32),
            jax.ShapeDtypeStruct((n, LANE), F32),
            jax.ShapeDtypeStruct((2 * TOP_K, n), F32),
            jax.ShapeDtypeStruct((N_EXPERTS, 1), F32),
        ],
        scratch_shapes=[pltpu.VMEM((N_EXPERTS, 1), F32)],
        compiler_params=pltpu.CompilerParams(dimension_semantics=("arbitrary",),
                                             vmem_limit_bytes=VMEM_LIMIT),
        name="out_proj",
    )(og_p, og_s, ol_p, ol_s, x_p, x_s, wo, g, wr, br)


def _store_token_tiles(ref2d, val):
    rows, tiles = val.shape[0], val.shape[1] // LANE
    for c in range(tiles):
        ref2d[pl.ds(c, rows, stride=tiles), :] = val[:, c * LANE:(c + 1) * LANE]


def _load_token_tiles(ref2d, first_row, rows, tiles=TOK_TILES):
    return jnp.concatenate(
        [ref2d[pl.ds(first_row * tiles + c, rows, stride=tiles), :] for c in range(tiles)], axis=1)


def _pack_bf16_pairs(x):
    half = x.shape[1] // 2
    bits = lax.bitcast_convert_type(x.astype(BF16).astype(F32), jnp.uint32)
    return (bits[:, :half] >> 16) | (bits[:, half:] & jnp.uint32(0xFFFF0000))


def _unpack_bf16_pairs(w):
    lo = lax.bitcast_convert_type(w << 16, F32)
    hi = lax.bitcast_convert_type(w & jnp.uint32(0xFFFF0000), F32)
    return jnp.concatenate([lo, hi], axis=1).astype(BF16)


def _expert_weight_copies(e, ws, wup_hbm, wdn_hbm, wup_buf, wdn_buf, wsems):
    return (pltpu.make_async_copy(wup_hbm.at[e], wup_buf.at[ws], wsems.at[ws]),
            pltpu.make_async_copy(wdn_hbm.at[e], wdn_buf.at[ws], wsems.at[ws]))


def _expert_body(be_ref, nu_ref, first_ref, wslot_ref, next_ref, valid_ref, x_ref,
                 wup_hbm, bup_ref, wdn_hbm, bdn_ref, y_ref, wup_buf, wdn_buf, wsems, wup_bf, wdn_bf):
    i = pl.program_id(0)
    n_used = nu_ref[0]
    ws = wslot_ref[i]
    weight_copies = functools.partial(_expert_weight_copies, wup_hbm=wup_hbm, wdn_hbm=wdn_hbm, wup_buf=wup_buf,
                                      wdn_buf=wdn_buf, wsems=wsems)

    @pl.when((i == 0) & (n_used > 0))
    def _():
        for cp in weight_copies(be_ref[0], ws):
            cp.start(priority=EXPERT_WEIGHT_QUEUE)

    @pl.when(i < n_used)
    def _():
        @pl.when(first_ref[i] == 1)
        def _():
            for cp in weight_copies(be_ref[i], ws):
                cp.wait()

            @pl.when(next_ref[i] >= 0)
            def _():
                for cp in weight_copies(next_ref[i], 1 - ws):
                    cp.start(priority=EXPERT_WEIGHT_QUEUE)

            for r in range(0, D_MODEL, WEIGHT_CAST_ROWS):
                wup_bf[r:r + WEIGHT_CAST_ROWS, :] = wup_buf[ws, r:r + WEIGHT_CAST_ROWS, :].astype(BF16)
            for r in range(0, D_FF, WEIGHT_CAST_ROWS):
                wdn_bf[r:r + WEIGHT_CAST_ROWS, :] = wdn_buf[ws, r:r + WEIGHT_CAST_ROWS, :].astype(BF16)

        def expert_rows(n):
            x = _unpack_bf16_pairs(_load_token_tiles(x_ref, 0, n, PACK_TILES))
            gu = _dot(x, wup_bf[...]) + bup_ref[...]
            gate = jnp.minimum(gu[:, :D_FF], SWIGLU_LIMIT)
            up = jnp.clip(gu[:, D_FF:], -SWIGLU_LIMIT, SWIGLU_LIMIT)
            a = (up + 1.0) * gate * jax.nn.sigmoid(SWIGLU_ALPHA * gate)
            _store_token_tiles(y_ref, _dot(a, wdn_bf[...]) + bdn_ref[...])

        half = EXPERT_ROWS // 2

        @pl.when(valid_ref[i] > half)
        def _():
            expert_rows(EXPERT_ROWS)

        @pl.when(valid_ref[i] <= half)
        def _():
            expert_rows(half)
            y_ref[half * TOK_TILES:, :] = jnp.zeros((half * TOK_TILES, LANE), F32)

    @pl.when(i >= n_used)
    def _():
        y_ref[...] = jnp.zeros_like(y_ref)


def _experts(block_meta, xs_2d, w_up, b_up, w_down, b_down):
    n_blocks = block_meta[0].shape[0]
    grid_spec = pltpu.PrefetchScalarGridSpec(
        num_scalar_prefetch=len(block_meta),
        grid=(n_blocks,),
        in_specs=[
            pl.BlockSpec((EXPERT_ROWS * PACK_TILES, LANE), lambda i, *_: (i, 0)),
            pl.BlockSpec(memory_space=pl.ANY),
            pl.BlockSpec((None, 1, 2 * D_FF), lambda i, be, *_: (be[i], 0, 0)),
            pl.BlockSpec(memory_space=pl.ANY),
            pl.BlockSpec((None, 1, D_MODEL), lambda i, be, *_: (be[i], 0, 0)),
        ],
        out_specs=pl.BlockSpec((EXPERT_ROWS * TOK_TILES, LANE), lambda i, *_: (i, 0)),
        scratch_shapes=[
            pltpu.VMEM((2, D_MODEL, 2 * D_FF), F32),
            pltpu.VMEM((2, D_FF, D_MODEL), F32),
            pltpu.SemaphoreType.DMA((2,)),
            pltpu.VMEM((D_MODEL, 2 * D_FF), BF16),
            pltpu.VMEM((D_FF, D_MODEL), BF16),
        ],
    )
    return pl.pallas_call(
        _expert_body,
        grid_spec=grid_spec,
        out_shape=jax.ShapeDtypeStruct((n_blocks * EXPERT_ROWS * TOK_TILES, LANE), F32),
        compiler_params=pltpu.CompilerParams(dimension_semantics=("arbitrary",),
                                             vmem_limit_bytes=VMEM_LIMIT),
        name="experts",
    )(*block_meta, xs_2d, w_up, b_up.reshape(N_EXPERTS, 1, 2 * D_FF), w_down,
      b_down.reshape(N_EXPERTS, 1, D_MODEL))


def _dispatch(h_tiles, dest_kmajor, n_rows):
    n_tok = h_tiles.shape[0]
    info = plsc.get_sparse_core_info()
    n_workers = info.num_cores * info.num_subcores
    per_worker = n_tok // n_workers
    chunk = next(c for c in (128, 96, 88, 64, 48, 32, 16, 8) if per_worker % c == 0)
    assert n_tok % n_workers == 0 and per_worker % SUBLANE == 0
    mesh = plsc.VectorSubcoreMesh(core_axis_name="c", subcore_axis_name="s")

    @functools.partial(
        pl.kernel, mesh=mesh,
        out_type=jax.ShapeDtypeStruct((n_rows,) + h_tiles.shape[1:], h_tiles.dtype),
        scratch_types=[pltpu.VMEM((TOP_K, chunk), jnp.int32), pltpu.VMEM((chunk,) + h_tiles.shape[1:], h_tiles.dtype),
                       pltpu.SemaphoreType.DMA],
    )
    def dispatch(h_hbm, dest_hbm, out_hbm, idx_v, rows_v, sem):
        wid = lax.axis_index("s") * info.num_cores + lax.axis_index("c")

        def step(j, carry):
            t0 = pl.multiple_of(wid * per_worker + j * chunk, SUBLANE)
            loads = [pltpu.async_copy(h_hbm.at[pl.ds(t0, chunk)], rows_v, sem)]
            for k in range(TOP_K):
                loads.append(pltpu.async_copy(
                    dest_hbm.at[pl.ds(pl.multiple_of(k * n_tok + t0, SUBLANE), chunk)], idx_v.at[k], sem))
            for cp in loads:
                cp.wait()
            stores = [pltpu.async_copy(rows_v, out_hbm.at[idx_v.at[k]], sem) for k in range(TOP_K)]
            for cp in stores:
                cp.wait()
            return carry

        lax.fori_loop(0, per_worker // chunk, step, 0)

    return dispatch(h_tiles, dest_kmajor)


def _gather_rows(src_tiles, idx_ref, n_rows, dst2d, sem, priorities):
    def issue(j, carry):
        for u in range(DMA_ISSUE_UNROLL):
            r = j * DMA_ISSUE_UNROLL + u
            dst = dst2d.at[pl.ds(pl.multiple_of(r * TOK_TILES, TOK_TILES), TOK_TILES), :]
            pltpu.make_async_copy(src_tiles.at[idx_ref[0, r]], dst, sem).start(
                priority=priorities[u % len(priorities)])
        return carry

    lax.fori_loop(0, n_rows // DMA_ISSUE_UNROLL, issue, 0)


def _wait_rows(src2d, n_rows, dst2d, sem):
    pltpu.make_async_copy(src2d.at[pl.ds(0, n_rows * TOK_TILES), :], dst2d, sem).wait()


def _combine_body(dest_ref, dest_next_ref, rt_ref, y_tiles, y_2d, x1_ref, g_ref, op_ref, os_ref, ybuf, sems,
                  *, n_p_blocks):
    i = pl.program_id(0)
    slot = i % 2
    n_rows = TOP_K * COMBINE_ROWS

    @pl.when(i == 0)
    def _():
        _gather_rows(y_tiles, dest_ref, n_rows, ybuf.at[0], sems.at[0], COMBINE_GATHER_QUEUES)

    _wait_rows(y_2d, n_rows, ybuf.at[slot], sems.at[slot])

    @pl.when(i + 1 < pl.num_programs(0))
    def _():
        _gather_rows(y_tiles, dest_next_ref, n_rows, ybuf.at[1 - slot], sems.at[1 - slot], COMBINE_GATHER_QUEUES)

    buf = ybuf.at[slot]
    moe = _load_token_tiles(buf, 0, COMBINE_ROWS) * rt_ref[:, 2 * TOP_K:2 * TOP_K + 1]
    for k in range(1, TOP_K):
        moe = moe + _load_token_tiles(buf, k * COMBINE_ROWS, COMBINE_ROWS) * rt_ref[:, 2 * TOP_K + k:2 * TOP_K + k + 1]
    res = _rms(x1_ref[...] + moe, g_ref[...])

    @pl.when(i < n_p_blocks)
    def _():
        op_ref[...] = res

    @pl.when(i >= n_p_blocks)
    def _():
        os_ref[...] = res


def _combine(dest_b, rt, y_2d, x1, g, n_p):
    n = x1.shape[0]
    n_blk = n // COMBINE_ROWS
    n_p_blocks = n_p // COMBINE_ROWS
    dest_blocks = dest_b.reshape(n_blk, 1, TOP_K * COMBINE_ROWS)
    return pl.pallas_call(
        functools.partial(_combine_body, n_p_blocks=n_p_blocks),
        grid=(n_blk,),
        in_specs=[
            pl.BlockSpec((None, 1, COMBINE_ROWS * TOP_K), lambda i: (i, 0, 0), memory_space=pltpu.SMEM),
            pl.BlockSpec((None, 1, COMBINE_ROWS * TOP_K), lambda i: (jnp.minimum(i + 1, n_blk - 1), 0, 0),
                         memory_space=pltpu.SMEM),
            pl.BlockSpec((COMBINE_ROWS, LANE), lambda i: (i, 0)),
            pl.BlockSpec(memory_space=pl.ANY),
            pl.BlockSpec(memory_space=pl.ANY),
            pl.BlockSpec((COMBINE_ROWS, D_MODEL), lambda i: (i, 0)),
            pl.BlockSpec((1, D_MODEL), lambda i: (0, 0)),
        ],
        out_specs=_group_specs(COMBINE_ROWS, D_MODEL, n_p_blocks),
        out_shape=[jax.ShapeDtypeStruct((n_p, D_MODEL), F32), jax.ShapeDtypeStruct((n - n_p, D_MODEL), F32)],
        scratch_shapes=[pltpu.VMEM((2, TOP_K * COMBINE_ROWS * TOK_TILES, LANE), F32),
                        pltpu.SemaphoreType.DMA((2,))],
        compiler_params=pltpu.CompilerParams(dimension_semantics=("arbitrary",),
                                             vmem_limit_bytes=VMEM_LIMIT),
        name="combine",
    )(dest_blocks, dest_blocks, rt, y_2d.reshape(-1, TOK_TILES, LANE), y_2d, x1, g)


def _plan_body(rtt_ref, cnt_ref, dk_ref, db_ref, meta_ref, pst):
    i = pl.program_id(0)
    sh = _log2(EXPERT_ROWS)
    n_e = N_EXPERTS

    @pl.when(i == 0)
    def _():
        cnt = cnt_ref[...].astype(jnp.int32)
        padded = (((cnt + (EXPERT_ROWS - 1)) >> sh) << sh).astype(F32)
        e_r = lax.broadcasted_iota(jnp.int32, (n_e, n_e), 0)
        e_c = lax.broadcasted_iota(jnp.int32, (n_e, n_e), 1)
        p_t = jnp.broadcast_to(padded, (n_e, n_e)).T
        pend = jnp.sum(jnp.where(e_c <= e_r, p_t, 0.0), axis=1, keepdims=True)
        pst[...] = pend - padded
        has_rows = p_t > 0.0
        group = jnp.sum(jnp.where((e_c <= e_r) & has_rows, 1.0, 0.0), axis=1, keepdims=True) - 1.0
        nxt = jnp.min(jnp.where((e_c > e_r) & has_rows, e_c, n_e), axis=1, keepdims=True)
        nxt = jnp.where(nxt >= n_e, -1, nxt)

        mb = meta_ref.shape[1]
        blk = lax.broadcasted_iota(jnp.int32, (n_e, mb), 1)
        eb = lax.broadcasted_iota(jnp.int32, (n_e, mb), 0)
        first_row = (blk * EXPERT_ROWS).astype(F32)

        def expert_of(row0):
            return jnp.minimum(jnp.sum(jnp.where(pend <= row0, 1, 0), axis=0, keepdims=True), n_e - 1)

        be = expert_of(first_row)
        be_prev = expert_of(first_row - EXPERT_ROWS)
        hit = eb == be
        wslot = jnp.sum(jnp.where(hit, group, 0.0), axis=0, keepdims=True).astype(jnp.int32) & 1
        nx = jnp.sum(jnp.where(hit, nxt, 0), axis=0, keepdims=True)
        n_used = pend[n_e - 1:n_e, :].astype(jnp.int32) >> sh
        lane = lax.broadcasted_iota(jnp.int32, (1, mb), 1)
        first = (((be != be_prev) | (lane == 0)) & (lane < n_used)).astype(jnp.int32)
        cnt_b = jnp.sum(jnp.where(hit, cnt_ref[...], 0.0), axis=0, keepdims=True)
        pst_b = jnp.sum(jnp.where(hit, pend - padded, 0.0), axis=0, keepdims=True)
        valid = jnp.clip(cnt_b - (first_row[0:1, :] - pst_b), 0.0, float(EXPERT_ROWS)).astype(jnp.int32)
        row8 = lax.broadcasted_iota(jnp.int32, (SUBLANE, mb), 0)
        meta = jnp.where(row8 == 0, be, jnp.where(row8 == 1, first, jnp.where(row8 == 2, wslot,
                         jnp.where(row8 == 3, nx, jnp.where(row8 == 4, n_used, valid)))))
        meta_ref[...] = meta

    tm = rtt_ref.shape[1]
    eid = lax.broadcasted_iota(jnp.int32, (n_e, tm), 0).astype(F32)
    row8 = lax.broadcasted_iota(jnp.int32, (SUBLANE, tm), 0)
    d8 = jnp.zeros((SUBLANE, tm), jnp.int32)
    for k in range(TOP_K):
        start = jnp.sum(jnp.where(eid == rtt_ref[k:k + 1, :], pst[...], 0.0), axis=0, keepdims=True)
        d8 = jnp.where(row8 == k, (start + rtt_ref[TOP_K + k:TOP_K + k + 1, :]).astype(jnp.int32), d8)
    dk_ref[...] = d8[:TOP_K]
    for b in range(tm // COMBINE_ROWS):
        db_ref[b] = d8[:TOP_K, b * COMBINE_ROWS:(b + 1) * COMBINE_ROWS]


def _plan(rtt, cnt):
    n = rtt.shape[1]
    n_rows = n * TOP_K + N_EXPERTS * EXPERT_ROWS
    n_blocks = n_rows // EXPERT_ROWS
    mb = -(-n_blocks // LANE) * LANE
    tile = max(t for t in range(COMBINE_ROWS, PLAN_TILE_MAX + 1, COMBINE_ROWS) if n % t == 0)
    dk, db, meta = pl.pallas_call(
        _plan_body,
        grid=(n // tile,),
        in_specs=[pl.BlockSpec((2 * TOP_K, tile), lambda i: (0, i)),
                  pl.BlockSpec((N_EXPERTS, 1), lambda i: (0, 0))],
        out_specs=[pl.BlockSpec((TOP_K, tile), lambda i: (0, i)),
                   pl.BlockSpec((tile // COMBINE_ROWS, TOP_K, COMBINE_ROWS), lambda i: (i, 0, 0)),
                   pl.BlockSpec((SUBLANE, mb), lambda i: (0, 0))],
        out_shape=[jax.ShapeDtypeStruct((TOP_K, n), jnp.int32),
                   jax.ShapeDtypeStruct((n // COMBINE_ROWS, TOP_K, COMBINE_ROWS), jnp.int32),
                   jax.ShapeDtypeStruct((SUBLANE, mb), jnp.int32)],
        scratch_shapes=[pltpu.VMEM((N_EXPERTS, 1), F32)],
        compiler_params=pltpu.CompilerParams(dimension_semantics=("arbitrary",)),
        name="plan",
    )(rtt, cnt)
    block_meta = (meta[0, :n_blocks], meta[4, 0:1], meta[1, :n_blocks], meta[2, :n_blocks], meta[3, :n_blocks],
                  meta[5, :n_blocks])
    return dk, db, n_rows, block_meta


def _pad_lanes(v, width):
    return jnp.zeros((1, width), F32).at[0, :v.shape[0]].set(v.astype(F32))


def kernel(x_prompt, x_sample, state_gdn_conv, state_gdn, state_gla, rms_mix_w, w_in, conv_w, gdn_a_log,
           gdn_dt_bias, gdn_norm_w, gla_gk_w, gla_gk_b, gla_norm_w, w_out, rms_ffn_w, w_router, b_router,
           w_up, b_up, w_down, b_down, rms_final_w):
    bp, tp, d = x_prompt.shape
    bs, ts, _ = x_sample.shape
    n_p, n_s = bp * tp, bs * ts
    assert d == D_MODEL and state_gdn.shape[0] == 1, "single-layer kernel"
    assert tp >= CONV_WIDTH - 1 and ts >= CONV_WIDTH - 1, "new conv state is taken from the new tokens only"
    l = 0

    wi = w_in[l]
    a0 = GDN_CONV_CH + GDN_V_W
    g0 = a0 + 2 * GDN_HEADS
    lr0 = g0 + 2 * GLA_QK_W + 2 * GLA_V_W
    small = jnp.concatenate([wi[:, a0:a0 + 2 * GDN_HEADS], wi[:, lr0:lr0 + GLA_GATE_RANK],
                             jnp.zeros((d, SM_W - 2 * GDN_HEADS - GLA_GATE_RANK), F32)], axis=1)
    w_big = jnp.concatenate([wi[:, :a0], wi[:, g0:lr0], small], axis=1).astype(BF16)
    alog = _pad_lanes(gdn_a_log[l], SM_W)
    dtb = _pad_lanes(gdn_dt_bias[l], SM_W)
    wgk = jnp.zeros((SM_W, GLA_QK_W), F32).at[SM_LR:SM_LR + GLA_GATE_RANK].set(gla_gk_w[l])
    wr = jnp.zeros((d, LANE), F32).at[:, :N_EXPERTS].set(w_router[l])
    br = jnp.full((1, LANE), -1e30, F32).at[0, :N_EXPERTS].set(b_router[l])

    assert n_p % ROW_TILE == 0 and n_s % ROW_TILE == 0
    x_p, x_s = x_prompt.reshape(n_p, d), x_sample.reshape(n_s, d)
    proj = _inproj(x_p, x_s, rms_mix_w[l][None, :], w_big)

    tb_p = PROMPT_TIME_BLOCK
    zeros_conv = jnp.zeros((bp, CONV_WIDTH - 1, GDN_CONV_CH), F32)
    og_p, gdn_p, conv_p = _gdn(proj, bp, 1, tp, tb_p, CHUNK, CHUNK, zeros_conv,
                               jnp.zeros((bp, GDN_HEADS, GDN_DK, GDN_DV), F32), conv_w[l], alog, dtb,
                               gdn_norm_w[l][None, :])
    ol_p, gla_p = _gla(proj, bp, 1, tp, tb_p, CHUNK, CHUNK, jnp.zeros((bp, GLA_HEADS, GLA_DK, GLA_DV), F32),
                       wgk, gla_gk_b[l][None, :], gla_norm_w[l][None, :])

    ts_pad = SUBLANE
    nb_s = SAMPLE_SEQS_PER_STEP
    proj_s = proj[n_p:].reshape(bs, ts, PROJ_W)
    proj_sp = jnp.pad(proj_s, ((0, 0), (0, ts_pad - ts), (0, 0))).reshape(bs * ts_pad, PROJ_W)
    og_s, gdn_s, conv_s = _gdn(proj_sp, bs, nb_s, ts_pad, ts_pad, ts_pad, ts, state_gdn_conv[l], state_gdn[l],
                               conv_w[l], alog, dtb, gdn_norm_w[l][None, :])
    ol_s, gla_s = _gla(proj_sp, bs, nb_s, ts_pad, ts_pad, ts_pad, ts, state_gla[l], wgk, gla_gk_b[l][None, :],
                       gla_norm_w[l][None, :])
    og_s = og_s.reshape(bs, ts_pad, GDN_V_W)[:, :ts].reshape(n_s, GDN_V_W)
    ol_s = ol_s.reshape(bs, ts_pad, GLA_V_W)[:, :ts].reshape(n_s, GLA_V_W)

    x1, h2, rt, rtt, cnt = _outproj(og_p, og_s, ol_p, ol_s, x_p, x_s, w_out[l].astype(BF16),
                                    rms_ffn_w[l][None, :], wr, br)

    dest_k, dest_b, n_rows, block_meta = _plan(rtt, cnt)
    xs = _dispatch(h2.reshape(-1, PACK_TILES, LANE), dest_k.reshape(-1), n_rows)
    y_rows = _experts(block_meta, xs.reshape(-1, LANE), w_up[l], b_up[l], w_down[l], b_down[l])
    y_p, y_s = _combine(dest_b, rt, y_rows, x1, rms_final_w[None, :], n_p)
    y_prompt = y_p.reshape(bp, tp, d)
    y_sample = y_s.reshape(bs, ts, d)
    return (y_prompt, y_sample, conv_p[None], gdn_p[None], gla_p[None], conv_s[None], gdn_s[None], gla_s[None])
```

```python
import functools

import jax
import jax.numpy as jnp
from jax import lax
from jax.experimental import pallas as pl
from jax.experimental.pallas import tpu as pltpu
from jax.experimental.pallas import tpu_sc as plsc

F32 = jnp.float32
BF16 = jnp.bfloat16
HI = lax.Precision.HIGHEST

D_MODEL = 1024
GDN_HEADS = 4
GDN_DK = 128
GDN_DV = 128
GLA_HEADS = 4
GLA_DK = 64
GLA_DV = 128
GLA_GATE_RANK = 16
GLA_GATE_NORMALIZER = 16.0
CONV_WIDTH = 4
CHUNK = 64
N_EXPERTS = 32
TOP_K = 4
D_FF = 1024
SWIGLU_LIMIT = 7.0
SWIGLU_ALPHA = 1.702
RMS_EPS = 1e-6
L2_EPS = 1e-6

GDN_QK_W = GDN_HEADS * GDN_DK
GDN_V_W = GDN_HEADS * GDN_DV
GDN_CONV_CH = 2 * GDN_QK_W + GDN_V_W
GLA_QK_W = GLA_HEADS * GLA_DK
GLA_V_W = GLA_HEADS * GLA_DV

COL_QKV = 0
COL_Z = 1536
COL_GQ = 2048
COL_GK = 2304
COL_GV = 2560
COL_GG = 3072
COL_SM = 3584
SM_W = 128
PROJ_W = COL_SM + SM_W
SM_A, SM_B, SM_LR = 0, 4, 8

LANE = 128
SUBLANE = 8
TOK_TILES = D_MODEL // LANE
PACK_TILES = TOK_TILES // 2
ROW_TILE = 512
EXPERT_ROWS = 512
EXPERT_WEIGHT_QUEUE = 1
WEIGHT_CAST_ROWS = 128
COMBINE_ROWS = 256
PLAN_TILE_MAX = 2048
DMA_ISSUE_UNROLL = 8
COMBINE_GATHER_QUEUES = (0, 1)
CONV_ROW_SLAB = 128
GDN_CHUNKS_PER_TRIP = 8
GLA_CHUNKS_PER_TRIP = 4
SCAN_CHUNKS_PER_TRIP = 8
PROMPT_TIME_BLOCK = 512
SAMPLE_SEQS_PER_STEP = 8
VMEM_LIMIT = 56 * 1024 * 1024


def _dot(a, b):
    return jnp.dot(a.astype(BF16), b.astype(BF16), preferred_element_type=F32)


def _dot_nt(a, b):
    return lax.dot_general(a.astype(BF16), b.astype(BF16), (((1,), (1,)), ((), ())),
                           preferred_element_type=F32)


def _dot_tn(a, b):
    return lax.dot_general(a.astype(BF16), b.astype(BF16), (((0,), (0,)), ((), ())),
                           preferred_element_type=F32)


def _dot_hi(a, b):
    return jnp.dot(a, b, precision=HI, preferred_element_type=F32)


def _dot_3pass(a, b):
    a_hi = a.astype(BF16)
    b_hi = b.astype(BF16)
    a_lo = (a - a_hi.astype(F32)).astype(BF16)
    b_lo = (b - b_hi.astype(F32)).astype(BF16)

    def mm(x, y):
        return jnp.dot(x, y, preferred_element_type=F32)

    return (mm(a_lo, b_hi) + mm(a_hi, b_lo)) + mm(a_hi, b_hi)


def _rms(x, w):
    return x * lax.rsqrt(jnp.mean(x * x, axis=-1, keepdims=True) + RMS_EPS) * w


def _silu(x):
    return x * jax.nn.sigmoid(x)


def _group_specs(rows, width, n_p_blocks):
    return [pl.BlockSpec((rows, width), lambda i: (jnp.minimum(i, n_p_blocks - 1), 0)),
            pl.BlockSpec((rows, width), lambda i: (jnp.maximum(i - n_p_blocks, 0), 0))]


def _group_pick(i, n_p_blocks, p_ref, s_ref):
    return jnp.where(i < n_p_blocks, p_ref[...], s_ref[...])


def _inproj_body(xp_ref, xs_ref, g_ref, w_ref, o_ref, *, n_p_blocks):
    x = _group_pick(pl.program_id(0), n_p_blocks, xp_ref, xs_ref)
    h = _rms(x, g_ref[...])
    o_ref[...] = jnp.dot(h.astype(BF16), w_ref[...], preferred_element_type=F32)


def _inproj(x_p, x_s, g, w):
    n_p_blocks, n_s_blocks = x_p.shape[0] // ROW_TILE, x_s.shape[0] // ROW_TILE
    n = x_p.shape[0] + x_s.shape[0]
    return pl.pallas_call(
        functools.partial(_inproj_body, n_p_blocks=n_p_blocks),
        grid=(n_p_blocks + n_s_blocks,),
        in_specs=_group_specs(ROW_TILE, D_MODEL, n_p_blocks) + [
            pl.BlockSpec((1, D_MODEL), lambda i: (0, 0)),
            pl.BlockSpec((D_MODEL, PROJ_W), lambda i: (0, 0)),
        ],
        out_specs=pl.BlockSpec((ROW_TILE, PROJ_W), lambda i: (i, 0)),
        out_shape=jax.ShapeDtypeStruct((n, PROJ_W), F32),
        compiler_params=pltpu.CompilerParams(dimension_semantics=("arbitrary",),
                                             vmem_limit_bytes=VMEM_LIMIT),
        name="in_proj",
    )(x_p, x_s, g, w)


def _log2(n):
    assert n & (n - 1) == 0
    return n.bit_length() - 1


def _tri_inv_all(ms, c, ii, jj):
    eye = (ii == jj).astype(F32)
    base = min(c, 8)
    sh = _log2(base)
    blk = (ii >> sh) == (jj >> sh)
    ns = [jnp.where(blk, m, 0.0) for m in ms]
    xs = [eye - n for n in ns]
    ps = [_dot(n, n) for n in ns]
    ts = [_dot(jnp.concatenate([x, p], axis=0), p) for x, p in zip(xs, ps)]
    xs = [x + t[:c] for x, t in zip(xs, ts)]
    ps = [t[c:] for t in ts]
    xs = [x + _dot(x, p) for x, p in zip(xs, ps)]
    s = base
    while s < c:
        sh_s, sh_b = _log2(s), _log2(2 * s)
        off = ((ii >> sh_b) == (jj >> sh_b)) & ((ii >> sh_s) != (jj >> sh_s))
        ys = [_dot(x, jnp.where(off, m, 0.0)) for x, m in zip(xs, ms)]
        xs = [x - _dot(y, x) for x, y in zip(xs, ys)]
        s *= 2
    return xs


def _gated_norm(o, w, z):
    return o * lax.rsqrt(jnp.mean(o * o, axis=-1, keepdims=True) + RMS_EPS) * w * _silu(z)


def _chunk_rows(s, tb_rows, ci, c):
    r = s * tb_rows + ci * c
    if not isinstance(r, int):
        r = pl.multiple_of(r, c)
    return r


def _for_chunks(n_chunks, step):
    if n_chunks == 1:
        step(0, 0)
    else:
        lax.fori_loop(0, n_chunks, step, 0)


def _gdn_body(qkv_ref, z_ref, sm_ref, cbuf_ref, s0_ref, cw_ref, alog_ref, dtb_ref, nw_ref,
              o_ref, sout_ref, cout_ref, st, xc, act, gcs, us, wss, qgs, kds, aqs,
              *, nb, tb_rows, chunk, valid, n_tb):
    tb = pl.program_id(1)
    c = chunk
    n_heads = GDN_HEADS
    tail = CONV_WIDTH - 1
    pad = SUBLANE
    units = [(s, h) for s in range(nb) for h in range(n_heads)]

    n_slabs = GDN_CONV_CH // LANE

    def lanes(j):
        return slice(j * LANE, (j + 1) * LANE)

    @pl.when(tb == 0)
    def _():
        st[...] = s0_ref[...]
        for s in range(nb):
            for j in range(n_slabs):
                xc[s, j, pad - tail:pad, :] = cbuf_ref[s, :, lanes(j)]

    if n_tb > 1:
        @pl.when(tb > 0)
        def _():
            for s in range(nb):
                for j in range(n_slabs):
                    xc[s, j, pad - tail:pad, :] = xc[s, j, tb_rows + pad - tail:tb_rows + pad, :]

    for s in range(nb):
        for j in range(n_slabs):
            xc[s, j, pad:pad + tb_rows, :] = qkv_ref[s * tb_rows:(s + 1) * tb_rows, lanes(j)]

    row_slab = min(tb_rows, CONV_ROW_SLAB)
    parities = 2 if row_slab >= 2 * SUBLANE else 1
    for s in range(nb):
        for j in range(n_slabs):
            src, dst = xc.at[s, j], act.at[j]
            for sl in range(tb_rows // row_slab):
                for p in range(parities):
                    lo = pad - tail + sl * row_slab + p
                    out0 = s * tb_rows + sl * row_slab + p

                    def rows_from(start):
                        if parities == 1:
                            return pl.ds(start, row_slab)
                        return pl.ds(start, row_slab // 2, stride=2)

                    acc = src[rows_from(lo), :] * cw_ref[0:1, lanes(j)]
                    for i in range(1, CONV_WIDTH):
                        acc = acc + src[rows_from(lo + i), :] * cw_ref[i:i + 1, lanes(j)]
                    dst[rows_from(out0), :] = _silu(acc)

    ii = lax.broadcasted_iota(jnp.int32, (c, c), 0)
    jj = lax.broadcasted_iota(jnp.int32, (c, c), 1)
    lower = (ii >= jj)
    lower_f = lower.astype(F32)
    strict = (ii > jj)
    rowmask = None
    if valid < c:
        rowmask = lax.broadcasted_iota(jnp.int32, (c, 1), 0) < valid

    def hs(h, w):
        return slice(h * w, (h + 1) * w)

    n_chunks = tb_rows // c
    cpi = next(k for k in (GDN_CHUNKS_PER_TRIP, 2, 1) if n_chunks % k == 0)
    p1_units = [(g, h) for g in range(nb * cpi) for h in range(n_heads)]

    def phase1(ci, carry):
        rows, b_ts, gc_ts, gc_tts = [], [], [], []
        for g in range(nb * cpi):
            rr = pl.ds(_chunk_rows(g // cpi, tb_rows, ci * cpi + g % cpi, c), c)
            sm = sm_ref[rr, :]
            g_t = -jnp.exp(alog_ref[...]) * jax.nn.softplus(sm + dtb_ref[...])
            b_t = jax.nn.sigmoid(sm)
            if rowmask is not None:
                g_t = jnp.where(rowmask, g_t, 0.0)
                b_t = jnp.where(rowmask, b_t, 0.0)
            gc_t = _dot_hi(lower_f, g_t)
            gcs[rr, :] = gc_t
            rows.append(rr)
            b_ts.append(b_t)
            gc_ts.append(gc_t)
            gc_tts.append(gc_t.T)
        qn, kn, kb, vb = {}, {}, {}, {}
        for (s, h) in p1_units:
            q = act[h, rows[s], :]
            k = act[n_heads + h, rows[s], :]
            v = act[2 * n_heads + h, rows[s], :]
            if rowmask is not None:
                q = jnp.where(rowmask, q, 0.0)
                k = jnp.where(rowmask, k, 0.0)
                v = jnp.where(rowmask, v, 0.0)
            qn[s, h] = q * lax.rsqrt(jnp.sum(q * q, axis=-1, keepdims=True) + L2_EPS) * (GDN_DK ** -0.5)
            kn[s, h] = k * lax.rsqrt(jnp.sum(k * k, axis=-1, keepdims=True) + L2_EPS)
            beta = b_ts[s][:, SM_B + h:SM_B + h + 1]
            kb[s, h] = kn[s, h] * beta
            vb[s, h] = v * beta
        s1 = {u: _dot_nt(jnp.concatenate([kb[u], qn[u]], axis=0), kn[u]) for u in p1_units}
        mm = []
        for (s, h) in p1_units:
            gcol = gc_ts[s][:, SM_A + h:SM_A + h + 1]
            grow = gc_tts[s][SM_A + h:SM_A + h + 1, :]
            dec = jnp.exp(jnp.where(lower, gcol - grow, -jnp.inf))
            mm.append(jnp.where(strict, s1[s, h][:c] * dec, 0.0))
            aqs[h, rows[s], :] = s1[s, h][c:] * dec
        tms = _tri_inv_all(mm, c, ii, jj)
        for (s, h), tm in zip(p1_units, tms):
            gcol = gc_ts[s][:, SM_A + h:SM_A + h + 1]
            eg = jnp.exp(gcol)
            uw = _dot(tm, jnp.concatenate([vb[s, h], kb[s, h] * eg], axis=1))
            us[rows[s], hs(h, GDN_DV)] = uw[:, :GDN_DV]
            wss[rows[s], hs(h, GDN_DV)] = uw[:, GDN_DV:]
            qgs[rows[s], hs(h, GDN_DK)] = qn[s, h] * eg
            kds[rows[s], hs(h, GDN_DK)] = kn[s, h] * jnp.exp(gcol[c - 1:c, :] - gcol)
        return carry

    cp2 = next(k for k in (SCAN_CHUNKS_PER_TRIP, 1) if n_chunks % k == 0)

    def phase2(ti, carry):
        for u in range(cp2):
            scan_chunk(ti * cp2 + u)
        return carry

    def scan_chunk(ci):
        r0 = [_chunk_rows(s, tb_rows, ci, c) for s in range(nb)]
        rows = [pl.ds(r, c) for r in r0]
        ws = {(s, h): _dot(jnp.concatenate([wss[rows[s], hs(h, GDN_DV)], qgs[rows[s], hs(h, GDN_DK)]], axis=0),
                           st[s, h]) for (s, h) in units}
        v_new = {(s, h): us[rows[s], hs(h, GDN_DV)] - ws[s, h][:c] for (s, h) in units}
        o = {(s, h): ws[s, h][c:] + _dot(aqs[h, rows[s], :], v_new[s, h]) for (s, h) in units}
        upd = {(s, h): _dot_tn(kds[rows[s], hs(h, GDN_DK)], v_new[s, h]) for (s, h) in units}
        for (s, h) in units:
            g_last = gcs[pl.ds(r0[s] + c - 1, 1), SM_A + h:SM_A + h + 1]
            st[s, h] = st[s, h] * jnp.exp(g_last) + upd[s, h]
        for s in range(nb):
            o_ref[rows[s], :] = jnp.concatenate(
                [_gated_norm(o[s, h], nw_ref[...], z_ref[rows[s], hs(h, GDN_DV)]) for h in range(n_heads)], axis=1)

    _for_chunks(n_chunks // cpi, phase1)
    _for_chunks(n_chunks // cp2, phase2)

    @pl.when(tb == n_tb - 1)
    def _():
        sout_ref[...] = st[...]
        last = tb_rows if valid == c else valid
        for s in range(nb):
            cout_ref[s] = jnp.concatenate(
                [xc[s, j, pad + last - tail:pad + last, :] for j in range(n_slabs)], axis=1)


def _gdn(proj, n_seq, nb, t_len, tb_rows, chunk, valid, conv_buf, s0, conv_w, alog, dtb, nw):
    n_tb = t_len // tb_rows
    assert nb == 1 or n_tb == 1
    rows = nb * tb_rows

    def rowblk(b, t):
        return b * n_tb + t

    body = functools.partial(_gdn_body, nb=nb, tb_rows=tb_rows, chunk=chunk, valid=valid, n_tb=n_tb)
    return pl.pallas_call(
        body,
        grid=(n_seq // nb, n_tb),
        in_specs=[
            pl.BlockSpec((rows, GDN_CONV_CH), lambda b, t: (rowblk(b, t), COL_QKV // GDN_CONV_CH)),
            pl.BlockSpec((rows, GDN_V_W), lambda b, t: (rowblk(b, t), COL_Z // GDN_V_W)),
            pl.BlockSpec((rows, SM_W), lambda b, t: (rowblk(b, t), COL_SM // SM_W)),
            pl.BlockSpec((nb, CONV_WIDTH - 1, GDN_CONV_CH), lambda b, t: (b, 0, 0)),
            pl.BlockSpec((nb, GDN_HEADS, GDN_DK, GDN_DV), lambda b, t: (b, 0, 0, 0)),
            pl.BlockSpec((CONV_WIDTH, GDN_CONV_CH), lambda b, t: (0, 0)),
            pl.BlockSpec((1, SM_W), lambda b, t: (0, 0)),
            pl.BlockSpec((1, SM_W), lambda b, t: (0, 0)),
            pl.BlockSpec((1, GDN_DV), lambda b, t: (0, 0)),
        ],
        out_specs=[
            pl.BlockSpec((rows, GDN_V_W), lambda b, t: (rowblk(b, t), 0)),
            pl.BlockSpec((nb, GDN_HEADS, GDN_DK, GDN_DV), lambda b, t: (b, 0, 0, 0)),
            pl.BlockSpec((nb, CONV_WIDTH - 1, GDN_CONV_CH), lambda b, t: (b, 0, 0)),
        ],
        out_shape=[
            jax.ShapeDtypeStruct((n_seq * t_len, GDN_V_W), F32),
            jax.ShapeDtypeStruct((n_seq, GDN_HEADS, GDN_DK, GDN_DV), F32),
            jax.ShapeDtypeStruct((n_seq, CONV_WIDTH - 1, GDN_CONV_CH), F32),
        ],
        scratch_shapes=[
            pltpu.VMEM((nb, GDN_HEADS, GDN_DK, GDN_DV), F32),
            pltpu.VMEM((nb, GDN_CONV_CH // LANE, tb_rows + SUBLANE, LANE), F32),
            pltpu.VMEM((GDN_CONV_CH // LANE, rows, LANE), F32),
            pltpu.VMEM((rows, SM_W), F32),
            pltpu.VMEM((rows, GDN_V_W), F32),
            pltpu.VMEM((rows, GDN_V_W), F32),
            pltpu.VMEM((rows, GDN_QK_W), F32),
            pltpu.VMEM((rows, GDN_QK_W), F32),
            pltpu.VMEM((GDN_HEADS, rows, chunk), F32),
        ],
        compiler_params=pltpu.CompilerParams(dimension_semantics=("arbitrary", "arbitrary"),
                                             vmem_limit_bytes=VMEM_LIMIT),
        name="gdn_mixer",
    )(proj, proj, proj, conv_buf, s0, conv_w, alog, dtb, nw)


def _gla_body(q_ref, k_ref, v_ref, go_ref, sm_ref, s0_ref, wgk_ref, bgk_ref, nw_ref,
              o_ref, sout_ref, st, qes, ois, upds, decs, *, nb, tb_rows, chunk, valid, n_tb):
    tb = pl.program_id(1)
    c = chunk
    n_heads = GLA_HEADS
    units = [(s, h) for s in range(nb) for h in range(n_heads)]

    @pl.when(tb == 0)
    def _():
        st[...] = s0_ref[...]

    ii = lax.broadcasted_iota(jnp.int32, (c, c), 0)
    jj = lax.broadcasted_iota(jnp.int32, (c, c), 1)
    lower = (ii >= jj)
    lower_f = lower.astype(F32)
    rid = lax.broadcasted_iota(jnp.int32, (c, 1), 0)
    rowmask = (rid < valid) if valid < c else None
    n_sub = max(c // 16, 1)
    sub = c // n_sub

    n_chunks = tb_rows // c
    cpi = next(k for k in (GLA_CHUNKS_PER_TRIP, 2, 1) if n_chunks % k == 0)
    p1_units = [(g, h) for g in range(nb * cpi) for h in range(n_heads)]

    def phase1(ci, carry):
        rows, slots, bcs, bc_ts = [], [], [], []
        for g in range(nb * cpi):
            chunk_idx = ci * cpi + g % cpi
            rr = pl.ds(_chunk_rows(g // cpi, tb_rows, chunk_idx, c), c)
            slots.append((g // cpi) * n_chunks + chunk_idx)
            gk = jax.nn.log_sigmoid(_dot(sm_ref[rr, :], wgk_ref[...]) + bgk_ref[...]) / GLA_GATE_NORMALIZER
            if rowmask is not None:
                gk = jnp.where(rowmask, gk, 0.0)
            bc = _dot_hi(lower_f, gk)
            rows.append(rr)
            bcs.append(bc)
            bc_ts.append(bc.T)
        q, k, v, bch = {}, {}, {}, {}
        for (s, h) in p1_units:
            ks = slice(h * GLA_DK, (h + 1) * GLA_DK)
            vs = slice(h * GLA_DV, (h + 1) * GLA_DV)
            q[s, h] = q_ref[rows[s], ks] * (GLA_DK ** -0.5)
            kk = k_ref[rows[s], ks]
            vv = v_ref[rows[s], vs]
            if rowmask is not None:
                kk = jnp.where(rowmask, kk, 0.0)
                vv = jnp.where(rowmask, vv, 0.0)
            k[s, h], v[s, h] = kk, vv
            bch[s, h] = bcs[s][:, ks]
        for (g, h) in p1_units:
            qes[h, rows[g], :] = q[g, h] * jnp.exp(bch[g, h])
        a = {}
        for u in p1_units:
            q_parts, k_parts = [], []
            for sb in range(n_sub):
                ref_row = bch[u][sb * sub:sb * sub + 1, :]
                in_blk = (rid >= sb * sub) & (rid < (sb + 1) * sub)
                q_parts.append(jnp.where(in_blk, q[u] * jnp.exp(jnp.where(in_blk, bch[u] - ref_row, 0.0)), 0.0))
                k_parts.append(k[u] * jnp.exp(jnp.where(rid < (sb + 1) * sub, ref_row - bch[u], 0.0)))
            q_hat = jnp.concatenate(q_parts, axis=1) if n_sub > 1 else q_parts[0]
            k_hat = jnp.concatenate(k_parts, axis=1) if n_sub > 1 else k_parts[0]
            a[u] = jnp.where(lower, _dot_nt(q_hat, k_hat), 0.0)
        upd = {u: _dot_tn(k[u] * jnp.exp(bch[u][c - 1:c, :] - bch[u]), v[u]) for u in p1_units}
        o_intra = {u: _dot(a[u], v[u]) for u in p1_units}
        for (g, h) in p1_units:
            dec_col = bc_ts[g][h * GLA_DK:(h + 1) * GLA_DK, c - 1:c]
            decs[slots[g], h] = jnp.broadcast_to(jnp.exp(dec_col), (GLA_DK, GLA_DV))
            upds[slots[g], h] = upd[g, h]
            ois[rows[g], h * GLA_DV:(h + 1) * GLA_DV] = o_intra[g, h]
        return carry

    cp2 = next(k for k in (SCAN_CHUNKS_PER_TRIP, 1) if n_chunks % k == 0)

    def phase2(ti, carry):
        for u in range(cp2):
            scan_chunk(ti * cp2 + u)
        return carry

    def scan_chunk(ci):
        rows = [pl.ds(_chunk_rows(s, tb_rows, ci, c), c) for s in range(nb)]
        o = {(s, h): ois[rows[s], h * GLA_DV:(h + 1) * GLA_DV] + _dot(qes[h, rows[s], :], st[s, h])
             for (s, h) in units}
        for (s, h) in units:
            st[s, h] = decs[s * n_chunks + ci, h] * st[s, h] + upds[s * n_chunks + ci, h]
        for s in range(nb):
            o_ref[rows[s], :] = jnp.concatenate(
                [_gated_norm(o[s, h], nw_ref[...], go_ref[rows[s], h * GLA_DV:(h + 1) * GLA_DV])
                 for h in range(n_heads)], axis=1)

    _for_chunks(n_chunks // cpi, phase1)
    _for_chunks(n_chunks // cp2, phase2)

    @pl.when(tb == n_tb - 1)
    def _():
        sout_ref[...] = st[...]


def _gla(proj, n_seq, nb, t_len, tb_rows, chunk, valid, s0, wgk, bgk, nw):
    n_tb = t_len // tb_rows
    assert nb == 1 or n_tb == 1
    rows = nb * tb_rows

    def rowblk(b, t):
        return b * n_tb + t

    body = functools.partial(_gla_body, nb=nb, tb_rows=tb_rows, chunk=chunk, valid=valid, n_tb=n_tb)
    return pl.pallas_call(
        body,
        grid=(n_seq // nb, n_tb),
        in_specs=[
            pl.BlockSpec((rows, GLA_QK_W), lambda b, t: (rowblk(b, t), COL_GQ // GLA_QK_W)),
            pl.BlockSpec((rows, GLA_QK_W), lambda b, t: (rowblk(b, t), COL_GK // GLA_QK_W)),
            pl.BlockSpec((rows, GLA_V_W), lambda b, t: (rowblk(b, t), COL_GV // GLA_V_W)),
            pl.BlockSpec((rows, GLA_V_W), lambda b, t: (rowblk(b, t), COL_GG // GLA_V_W)),
            pl.BlockSpec((rows, SM_W), lambda b, t: (rowblk(b, t), COL_SM // SM_W)),
            pl.BlockSpec((nb, GLA_HEADS, GLA_DK, GLA_DV), lambda b, t: (b, 0, 0, 0)),
            pl.BlockSpec((SM_W, GLA_QK_W), lambda b, t: (0, 0)),
            pl.BlockSpec((1, GLA_QK_W), lambda b, t: (0, 0)),
            pl.BlockSpec((1, GLA_DV), lambda b, t: (0, 0)),
        ],
        out_specs=[
            pl.BlockSpec((rows, GLA_V_W), lambda b, t: (rowblk(b, t), 0)),
            pl.BlockSpec((nb, GLA_HEADS, GLA_DK, GLA_DV), lambda b, t: (b, 0, 0, 0)),
        ],
        out_shape=[
            jax.ShapeDtypeStruct((n_seq * t_len, GLA_V_W), F32),
            jax.ShapeDtypeStruct((n_seq, GLA_HEADS, GLA_DK, GLA_DV), F32),
        ],
        scratch_shapes=[
            pltpu.VMEM((nb, GLA_HEADS, GLA_DK, GLA_DV), F32),
            pltpu.VMEM((GLA_HEADS, rows, GLA_DK), F32),
            pltpu.VMEM((rows, GLA_V_W), F32),
            pltpu.VMEM((rows // chunk, GLA_HEADS, GLA_DK, GLA_DV), F32),
            pltpu.VMEM((rows // chunk, GLA_HEADS, GLA_DK, GLA_DV), F32),
        ],
        compiler_params=pltpu.CompilerParams(dimension_semantics=("arbitrary", "arbitrary"),
                                             vmem_limit_bytes=VMEM_LIMIT),
        name="gla_mixer",
    )(proj, proj, proj, proj, proj, s0, wgk, bgk, nw)


def _outproj_body(ogp_ref, ogs_ref, olp_ref, ols_ref, xp_ref, xs_ref, wo_ref, g_ref, wr_ref, br_ref,
                  x1_ref, h2_ref, rt_ref, rtt_ref, cnt_ref, base, *, n_p_blocks):
    i = pl.program_id(0)

    @pl.when(i == 0)
    def _():
        base[...] = jnp.zeros_like(base)

    o = jnp.concatenate([_group_pick(i, n_p_blocks, ogp_ref, ogs_ref),
                         _group_pick(i, n_p_blocks, olp_ref, ols_ref)], axis=1)
    x1 = _group_pick(i, n_p_blocks, xp_ref, xs_ref) + jnp.dot(o.astype(BF16), wo_ref[...],
                                                               preferred_element_type=F32)
    x1_ref[...] = x1
    h = _rms(x1, g_ref[...])
    _store_token_tiles(h2_ref, _pack_bf16_pairs(h))
    logits = _dot_3pass(h, wr_ref[...]) + br_ref[...]

    tm = logits.shape[0]
    lt = logits.T[:N_EXPERTS]
    eid = lax.broadcasted_iota(jnp.int32, (N_EXPERTS, tm), 0)
    work = lt
    sel = jnp.zeros((N_EXPERTS, tm), F32)
    hits, ids, vals = [], [], []
    for _ in range(TOP_K):
        m = jnp.max(work, axis=0, keepdims=True)
        idx = jnp.min(jnp.where(work == m, eid, N_EXPERTS), axis=0, keepdims=True)
        hit = eid == idx
        hits.append(hit)
        ids.append(idx)
        vals.append(m)
        work = jnp.where(hit, -jnp.inf, work)
        sel = sel + hit.astype(F32)
    exps = [jnp.exp(v - vals[0]) for v in vals]
    den = exps[0]
    for e in exps[1:]:
        den = den + e
    gates = [e / den for e in exps]

    ri = lax.broadcasted_iota(jnp.int32, (tm, tm), 0)
    ci = lax.broadcasted_iota(jnp.int32, (tm, tm), 1)
    before = _dot(sel, (ri < ci).astype(F32)) + base[...]
    ranks = [jnp.sum(jnp.where(hit, before, 0.0), axis=0, keepdims=True) for hit in hits]
    base[...] = base[...] + jnp.sum(sel, axis=1, keepdims=True)
    cnt_ref[...] = base[...]

    row = lax.broadcasted_iota(jnp.int32, (LANE, tm), 0)
    rec = jnp.zeros((LANE, tm), F32)
    for k in range(TOP_K):
        rec = jnp.where(row == k, ids[k].astype(F32), rec)
        rec = jnp.where(row == TOP_K + k, ranks[k], rec)
        rec = jnp.where(row == 2 * TOP_K + k, gates[k], rec)
    rt_ref[...] = rec.T
    rtt_ref[...] = rec[:2 * TOP_K]


def _outproj(og_p, og_s, ol_p, ol_s, x_p, x_s, wo, g, wr, br):
    n_p_blocks, n_s_blocks = x_p.shape[0] // ROW_TILE, x_s.shape[0] // ROW_TILE
    n = x_p.shape[0] + x_s.shape[0]
    return pl.pallas_call(
        functools.partial(_outproj_body, n_p_blocks=n_p_blocks),
        grid=(n_p_blocks + n_s_blocks,),
        in_specs=_group_specs(ROW_TILE, GDN_V_W, n_p_blocks) + _group_specs(ROW_TILE, GLA_V_W, n_p_blocks)
        + _group_specs(ROW_TILE, D_MODEL, n_p_blocks) + [
            pl.BlockSpec((D_MODEL, D_MODEL), lambda i: (0, 0)),
            pl.BlockSpec((1, D_MODEL), lambda i: (0, 0)),
            pl.BlockSpec((D_MODEL, LANE), lambda i: (0, 0)),
            pl.BlockSpec((1, LANE), lambda i: (0, 0)),
        ],
        out_specs=[
            pl.BlockSpec((ROW_TILE, D_MODEL), lambda i: (i, 0)),
            pl.BlockSpec((ROW_TILE * PACK_TILES, LANE), lambda i: (i, 0)),
            pl.BlockSpec((ROW_TILE, LANE), lambda i: (i, 0)),
            pl.BlockSpec((2 * TOP_K, ROW_TILE), lambda i: (0, i)),
            pl.BlockSpec((N_EXPERTS, 1), lambda i: (0, 0)),
        ],
        out_shape=[
            jax.ShapeDtypeStruct((n, D_MODEL), F32),
            jax.ShapeDtypeStruct((n * PACK_TILES, LANE), jnp.uint32),
            jax.ShapeDtypeStruct((n, LANE), F32),
            jax.ShapeDtypeStruct((2 * TOP_K, n), F32),
            jax.ShapeDtypeStruct((N_EXPERTS, 1), F32),
        ],
        scratch_shapes=[pltpu.VMEM((N_EXPERTS, 1), F32)],
        compiler_params=pltpu.CompilerParams(dimension_semantics=("arbitrary",),
                                             vmem_limit_bytes=VMEM_LIMIT),
        name="out_proj",
    )(og_p, og_s, ol_p, ol_s, x_p, x_s, wo, g, wr, br)


def _store_token_tiles(ref2d, val):
    rows, tiles = val.shape[0], val.shape[1] // LANE
    for c in range(tiles):
        ref2d[pl.ds(c, rows, stride=tiles), :] = val[:, c * LANE:(c + 1) * LANE]


def _load_token_tiles(ref2d, first_row, rows, tiles=TOK_TILES):
    return jnp.concatenate(
        [ref2d[pl.ds(first_row * tiles + c, rows, stride=tiles), :] for c in range(tiles)], axis=1)


def _pack_bf16_pairs(x):
    half = x.shape[1] // 2
    bits = lax.bitcast_convert_type(x.astype(BF16).astype(F32), jnp.uint32)
    return (bits[:, :half] >> 16) | (bits[:, half:] & jnp.uint32(0xFFFF0000))


def _unpack_bf16_pairs(w):
    lo = lax.bitcast_convert_type(w << 16, F32)
    hi = lax.bitcast_convert_type(w & jnp.uint32(0xFFFF0000), F32)
    return jnp.concatenate([lo, hi], axis=1).astype(BF16)


def _expert_weight_copies(e, ws, wup_hbm, wdn_hbm, wup_buf, wdn_buf, wsems):
    return (pltpu.make_async_copy(wup_hbm.at[e], wup_buf.at[ws], wsems.at[ws]),
            pltpu.make_async_copy(wdn_hbm.at[e], wdn_buf.at[ws], wsems.at[ws]))


def _expert_body(be_ref, nu_ref, first_ref, wslot_ref, next_ref, valid_ref, x_ref,
                 wup_hbm, bup_ref, wdn_hbm, bdn_ref, y_ref, wup_buf, wdn_buf, wsems, wup_bf, wdn_bf):
    i = pl.program_id(0)
    n_used = nu_ref[0]
    ws = wslot_ref[i]
    weight_copies = functools.partial(_expert_weight_copies, wup_hbm=wup_hbm, wdn_hbm=wdn_hbm, wup_buf=wup_buf,
                                      wdn_buf=wdn_buf, wsems=wsems)

    @pl.when((i == 0) & (n_used > 0))
    def _():
        for cp in weight_copies(be_ref[0], ws):
            cp.start(priority=EXPERT_WEIGHT_QUEUE)

    @pl.when(i < n_used)
    def _():
        @pl.when(first_ref[i] == 1)
        def _():
            for cp in weight_copies(be_ref[i], ws):
                cp.wait()

            @pl.when(next_ref[i] >= 0)
            def _():
                for cp in weight_copies(next_ref[i], 1 - ws):
                    cp.start(priority=EXPERT_WEIGHT_QUEUE)

            for r in range(0, D_MODEL, WEIGHT_CAST_ROWS):
                wup_bf[r:r + WEIGHT_CAST_ROWS, :] = wup_buf[ws, r:r + WEIGHT_CAST_ROWS, :].astype(BF16)
            for r in range(0, D_FF, WEIGHT_CAST_ROWS):
                wdn_bf[r:r + WEIGHT_CAST_ROWS, :] = wdn_buf[ws, r:r + WEIGHT_CAST_ROWS, :].astype(BF16)

        def expert_rows(n):
            x = _unpack_bf16_pairs(_load_token_tiles(x_ref, 0, n, PACK_TILES))
            gu = _dot(x, wup_bf[...]) + bup_ref[...]
            gate = jnp.minimum(gu[:, :D_FF], SWIGLU_LIMIT)
            up = jnp.clip(gu[:, D_FF:], -SWIGLU_LIMIT, SWIGLU_LIMIT)
            a = (up + 1.0) * gate * jax.nn.sigmoid(SWIGLU_ALPHA * gate)
            _store_token_tiles(y_ref, _dot(a, wdn_bf[...]) + bdn_ref[...])

        half = EXPERT_ROWS // 2

        @pl.when(valid_ref[i] > half)
        def _():
            expert_rows(EXPERT_ROWS)

        @pl.when(valid_ref[i] <= half)
        def _():
            expert_rows(half)
            y_ref[half * TOK_TILES:, :] = jnp.zeros((half * TOK_TILES, LANE), F32)

    @pl.when(i >= n_used)
    def _():
        y_ref[...] = jnp.zeros_like(y_ref)


def _experts(block_meta, xs_2d, w_up, b_up, w_down, b_down):
    n_blocks = block_meta[0].shape[0]
    grid_spec = pltpu.PrefetchScalarGridSpec(
        num_scalar_prefetch=len(block_meta),
        grid=(n_blocks,),
        in_specs=[
            pl.BlockSpec((EXPERT_ROWS * PACK_TILES, LANE), lambda i, *_: (i, 0)),
            pl.BlockSpec(memory_space=pl.ANY),
            pl.BlockSpec((None, 1, 2 * D_FF), lambda i, be, *_: (be[i], 0, 0)),
            pl.BlockSpec(memory_space=pl.ANY),
            pl.BlockSpec((None, 1, D_MODEL), lambda i, be, *_: (be[i], 0, 0)),
        ],
        out_specs=pl.BlockSpec((EXPERT_ROWS * TOK_TILES, LANE), lambda i, *_: (i, 0)),
        scratch_shapes=[
            pltpu.VMEM((2, D_MODEL, 2 * D_FF), F32),
            pltpu.VMEM((2, D_FF, D_MODEL), F32),
            pltpu.SemaphoreType.DMA((2,)),
            pltpu.VMEM((D_MODEL, 2 * D_FF), BF16),
            pltpu.VMEM((D_FF, D_MODEL), BF16),
        ],
    )
    return pl.pallas_call(
        _expert_body,
        grid_spec=grid_spec,
        out_shape=jax.ShapeDtypeStruct((n_blocks * EXPERT_ROWS * TOK_TILES, LANE), F32),
        compiler_params=pltpu.CompilerParams(dimension_semantics=("arbitrary",),
                                             vmem_limit_bytes=VMEM_LIMIT),
        name="experts",
    )(*block_meta, xs_2d, w_up, b_up.reshape(N_EXPERTS, 1, 2 * D_FF), w_down,
      b_down.reshape(N_EXPERTS, 1, D_MODEL))


def _dispatch(h_tiles, dest_kmajor, n_rows):
    n_tok = h_tiles.shape[0]
    info = plsc.get_sparse_core_info()
    n_workers = info.num_cores * info.num_subcores
    per_worker = n_tok // n_workers
    chunk = next(c for c in (128, 96, 88, 64, 48, 32, 16, 8) if per_worker % c == 0)
    assert n_tok % n_workers == 0 and per_worker % SUBLANE == 0
    mesh = plsc.VectorSubcoreMesh(core_axis_name="c", subcore_axis_name="s")

    @functools.partial(
        pl.kernel, mesh=mesh,
        out_type=jax.ShapeDtypeStruct((n_rows,) + h_tiles.shape[1:], h_tiles.dtype),
        scratch_types=[pltpu.VMEM((TOP_K, chunk), jnp.int32), pltpu.VMEM((chunk,) + h_tiles.shape[1:], h_tiles.dtype),
                       pltpu.SemaphoreType.DMA],
    )
    def dispatch(h_hbm, dest_hbm, out_hbm, idx_v, rows_v, sem):
        wid = lax.axis_index("s") * info.num_cores + lax.axis_index("c")

        def step(j, carry):
            t0 = pl.multiple_of(wid * per_worker + j * chunk, SUBLANE)
            loads = [pltpu.async_copy(h_hbm.at[pl.ds(t0, chunk)], rows_v, sem)]
            for k in range(TOP_K):
                loads.append(pltpu.async_copy(
                    dest_hbm.at[pl.ds(pl.multiple_of(k * n_tok + t0, SUBLANE), chunk)], idx_v.at[k], sem))
            for cp in loads:
                cp.wait()
            stores = [pltpu.async_copy(rows_v, out_hbm.at[idx_v.at[k]], sem) for k in range(TOP_K)]
            for cp in stores:
                cp.wait()
            return carry

        lax.fori_loop(0, per_worker // chunk, step, 0)

    return dispatch(h_tiles, dest_kmajor)


def _gather_rows(src_tiles, idx_ref, n_rows, dst2d, sem, priorities):
    def issue(j, carry):
        for u in range(DMA_ISSUE_UNROLL):
            r = j * DMA_ISSUE_UNROLL + u
            dst = dst2d.at[pl.ds(pl.multiple_of(r * TOK_TILES, TOK_TILES), TOK_TILES), :]
            pltpu.make_async_copy(src_tiles.at[idx_ref[0, r]], dst, sem).start(
                priority=priorities[u % len(priorities)])
        return carry

    lax.fori_loop(0, n_rows // DMA_ISSUE_UNROLL, issue, 0)


def _wait_rows(src2d, n_rows, dst2d, sem):
    pltpu.make_async_copy(src2d.at[pl.ds(0, n_rows * TOK_TILES), :], dst2d, sem).wait()


def _combine_body(dest_ref, dest_next_ref, rt_ref, y_tiles, y_2d, x1_ref, g_ref, op_ref, os_ref, ybuf, sems,
                  *, n_p_blocks):
    i = pl.program_id(0)
    slot = i % 2
    n_rows = TOP_K * COMBINE_ROWS

    @pl.when(i == 0)
    def _():
        _gather_rows(y_tiles, dest_ref, n_rows, ybuf.at[0], sems.at[0], COMBINE_GATHER_QUEUES)

    _wait_rows(y_2d, n_rows, ybuf.at[slot], sems.at[slot])

    @pl.when(i + 1 < pl.num_programs(0))
    def _():
        _gather_rows(y_tiles, dest_next_ref, n_rows, ybuf.at[1 - slot], sems.at[1 - slot], COMBINE_GATHER_QUEUES)

    buf = ybuf.at[slot]
    moe = _load_token_tiles(buf, 0, COMBINE_ROWS) * rt_ref[:, 2 * TOP_K:2 * TOP_K + 1]
    for k in range(1, TOP_K):
        moe = moe + _load_token_tiles(buf, k * COMBINE_ROWS, COMBINE_ROWS) * rt_ref[:, 2 * TOP_K + k:2 * TOP_K + k + 1]
    res = _rms(x1_ref[...] + moe, g_ref[...])

    @pl.when(i < n_p_blocks)
    def _():
        op_ref[...] = res

    @pl.when(i >= n_p_blocks)
    def _():
        os_ref[...] = res


def _combine(dest_b, rt, y_2d, x1, g, n_p):
    n = x1.shape[0]
    n_blk = n // COMBINE_ROWS
    n_p_blocks = n_p // COMBINE_ROWS
    dest_blocks = dest_b.reshape(n_blk, 1, TOP_K * COMBINE_ROWS)
    return pl.pallas_call(
        functools.partial(_combine_body, n_p_blocks=n_p_blocks),
        grid=(n_blk,),
        in_specs=[
            pl.BlockSpec((None, 1, COMBINE_ROWS * TOP_K), lambda i: (i, 0, 0), memory_space=pltpu.SMEM),
            pl.BlockSpec((None, 1, COMBINE_ROWS * TOP_K), lambda i: (jnp.minimum(i + 1, n_blk - 1), 0, 0),
                         memory_space=pltpu.SMEM),
            pl.BlockSpec((COMBINE_ROWS, LANE), lambda i: (i, 0)),
            pl.BlockSpec(memory_space=pl.ANY),
            pl.BlockSpec(memory_space=pl.ANY),
            pl.BlockSpec((COMBINE_ROWS, D_MODEL), lambda i: (i, 0)),
            pl.BlockSpec((1, D_MODEL), lambda i: (0, 0)),
        ],
        out_specs=_group_specs(COMBINE_ROWS, D_MODEL, n_p_blocks),
        out_shape=[jax.ShapeDtypeStruct((n_p, D_MODEL), F32), jax.ShapeDtypeStruct((n - n_p, D_MODEL), F32)],
        scratch_shapes=[pltpu.VMEM((2, TOP_K * COMBINE_ROWS * TOK_TILES, LANE), F32),
                        pltpu.SemaphoreType.DMA((2,))],
        compiler_params=pltpu.CompilerParams(dimension_semantics=("arbitrary",),
                                             vmem_limit_bytes=VMEM_LIMIT),
        name="combine",
    )(dest_blocks, dest_blocks, rt, y_2d.reshape(-1, TOK_TILES, LANE), y_2d, x1, g)


def _plan_body(rtt_ref, cnt_ref, dk_ref, db_ref, meta_ref, pst):
    i = pl.program_id(0)
    sh = _log2(EXPERT_ROWS)
    n_e = N_EXPERTS

    @pl.when(i == 0)
    def _():
        cnt = cnt_ref[...].astype(jnp.int32)
        padded = (((cnt + (EXPERT_ROWS - 1)) >> sh) << sh).astype(F32)
        e_r = lax.broadcasted_iota(jnp.int32, (n_e, n_e), 0)
        e_c = lax.broadcasted_iota(jnp.int32, (n_e, n_e), 1)
        p_t = jnp.broadcast_to(padded, (n_e, n_e)).T
        pend = jnp.sum(jnp.where(e_c <= e_r, p_t, 0.0), axis=1, keepdims=True)
        pst[...] = pend - padded
        has_rows = p_t > 0.0
        group = jnp.sum(jnp.where((e_c <= e_r) & has_rows, 1.0, 0.0), axis=1, keepdims=True) - 1.0
        nxt = jnp.min(jnp.where((e_c > e_r) & has_rows, e_c, n_e), axis=1, keepdims=True)
        nxt = jnp.where(nxt >= n_e, -1, nxt)

        mb = meta_ref.shape[1]
        blk = lax.broadcasted_iota(jnp.int32, (n_e, mb), 1)
        eb = lax.broadcasted_iota(jnp.int32, (n_e, mb), 0)
        first_row = (blk * EXPERT_ROWS).astype(F32)

        def expert_of(row0):
            return jnp.minimum(jnp.sum(jnp.where(pend <= row0, 1, 0), axis=0, keepdims=True), n_e - 1)

        be = expert_of(first_row)
        be_prev = expert_of(first_row - EXPERT_ROWS)
        hit = eb == be
        wslot = jnp.sum(jnp.where(hit, group, 0.0), axis=0, keepdims=True).astype(jnp.int32) & 1
        nx = jnp.sum(jnp.where(hit, nxt, 0), axis=0, keepdims=True)
        n_used = pend[n_e - 1:n_e, :].astype(jnp.int32) >> sh
        lane = lax.broadcasted_iota(jnp.int32, (1, mb), 1)
        first = (((be != be_prev) | (lane == 0)) & (lane < n_used)).astype(jnp.int32)
        cnt_b = jnp.sum(jnp.where(hit, cnt_ref[...], 0.0), axis=0, keepdims=True)
        pst_b = jnp.sum(jnp.where(hit, pend - padded, 0.0), axis=0, keepdims=True)
        valid = jnp.clip(cnt_b - (first_row[0:1, :] - pst_b), 0.0, float(EXPERT_ROWS)).astype(jnp.int32)
        row8 = lax.broadcasted_iota(jnp.int32, (SUBLANE, mb), 0)
        meta = jnp.where(row8 == 0, be, jnp.where(row8 == 1, first, jnp.where(row8 == 2, wslot,
                         jnp.where(row8 == 3, nx, jnp.where(row8 == 4, n_used, valid)))))
        meta_ref[...] = meta

    tm = rtt_ref.shape[1]
    eid = lax.broadcasted_iota(jnp.int32, (n_e, tm), 0).astype(F32)
    row8 = lax.broadcasted_iota(jnp.int32, (SUBLANE, tm), 0)
    d8 = jnp.zeros((SUBLANE, tm), jnp.int32)
    for k in range(TOP_K):
        start = jnp.sum(jnp.where(eid == rtt_ref[k:k + 1, :], pst[...], 0.0), axis=0, keepdims=True)
        d8 = jnp.where(row8 == k, (start + rtt_ref[TOP_K + k:TOP_K + k + 1, :]).astype(jnp.int32), d8)
    dk_ref[...] = d8[:TOP_K]
    for b in range(tm // COMBINE_ROWS):
        db_ref[b] = d8[:TOP_K, b * COMBINE_ROWS:(b + 1) * COMBINE_ROWS]


def _plan(rtt, cnt):
    n = rtt.shape[1]
    n_rows = n * TOP_K + N_EXPERTS * EXPERT_ROWS
    n_blocks = n_rows // EXPERT_ROWS
    mb = -(-n_blocks // LANE) * LANE
    tile = max(t for t in range(COMBINE_ROWS, PLAN_TILE_MAX + 1, COMBINE_ROWS) if n % t == 0)
    dk, db, meta = pl.pallas_call(
        _plan_body,
        grid=(n // tile,),
        in_specs=[pl.BlockSpec((2 * TOP_K, tile), lambda i: (0, i)),
                  pl.BlockSpec((N_EXPERTS, 1), lambda i: (0, 0))],
        out_specs=[pl.BlockSpec((TOP_K, tile), lambda i: (0, i)),
                   pl.BlockSpec((tile // COMBINE_ROWS, TOP_K, COMBINE_ROWS), lambda i: (i, 0, 0)),
                   pl.BlockSpec((SUBLANE, mb), lambda i: (0, 0))],
        out_shape=[jax.ShapeDtypeStruct((TOP_K, n), jnp.int32),
                   jax.ShapeDtypeStruct((n // COMBINE_ROWS, TOP_K, COMBINE_ROWS), jnp.int32),
                   jax.ShapeDtypeStruct((SUBLANE, mb), jnp.int32)],
        scratch_shapes=[pltpu.VMEM((N_EXPERTS, 1), F32)],
        compiler_params=pltpu.CompilerParams(dimension_semantics=("arbitrary",)),
        name="plan",
    )(rtt, cnt)
    block_meta = (meta[0, :n_blocks], meta[4, 0:1], meta[1, :n_blocks], meta[2, :n_blocks], meta[3, :n_blocks],
                  meta[5, :n_blocks])
    return dk, db, n_rows, block_meta


def _pad_lanes(v, width):
    return jnp.zeros((1, width), F32).at[0, :v.shape[0]].set(v.astype(F32))


def kernel(x_prompt, x_sample, state_gdn_conv, state_gdn, state_gla, rms_mix_w, w_in, conv_w, gdn_a_log,
           gdn_dt_bias, gdn_norm_w, gla_gk_w, gla_gk_b, gla_norm_w, w_out, rms_ffn_w, w_router, b_router,
           w_up, b_up, w_down, b_down, rms_final_w):
    bp, tp, d = x_prompt.shape
    bs, ts, _ = x_sample.shape
    n_p, n_s = bp * tp, bs * ts
    assert d == D_MODEL and state_gdn.shape[0] == 1, "single-layer kernel"
    assert tp >= CONV_WIDTH - 1 and ts >= CONV_WIDTH - 1, "new conv state is taken from the new tokens only"
    l = 0

    wi = w_in[l]
    a0 = GDN_CONV_CH + GDN_V_W
    g0 = a0 + 2 * GDN_HEADS
    lr0 = g0 + 2 * GLA_QK_W + 2 * GLA_V_W
    small = jnp.concatenate([wi[:, a0:a0 + 2 * GDN_HEADS], wi[:, lr0:lr0 + GLA_GATE_RANK],
                             jnp.zeros((d, SM_W - 2 * GDN_HEADS - GLA_GATE_RANK), F32)], axis=1)
    w_big = jnp.concatenate([wi[:, :a0], wi[:, g0:lr0], small], axis=1).astype(BF16)
    alog = _pad_lanes(gdn_a_log[l], SM_W)
    dtb = _pad_lanes(gdn_dt_bias[l], SM_W)
    wgk = jnp.zeros((SM_W, GLA_QK_W), F32).at[SM_LR:SM_LR + GLA_GATE_RANK].set(gla_gk_w[l])
    wr = jnp.zeros((d, LANE), F32).at[:, :N_EXPERTS].set(w_router[l])
    br = jnp.full((1, LANE), -1e30, F32).at[0, :N_EXPERTS].set(b_router[l])

    assert n_p % ROW_TILE == 0 and n_s % ROW_TILE == 0
    x_p, x_s = x_prompt.reshape(n_p, d), x_sample.reshape(n_s, d)
    proj = _inproj(x_p, x_s, rms_mix_w[l][None, :], w_big)

    tb_p = PROMPT_TIME_BLOCK
    zeros_conv = jnp.zeros((bp, CONV_WIDTH - 1, GDN_CONV_CH), F32)
    og_p, gdn_p, conv_p = _gdn(proj, bp, 1, tp, tb_p, CHUNK, CHUNK, zeros_conv,
                               jnp.zeros((bp, GDN_HEADS, GDN_DK, GDN_DV), F32), conv_w[l], alog, dtb,
                               gdn_norm_w[l][None, :])
    ol_p, gla_p = _gla(proj, bp, 1, tp, tb_p, CHUNK, CHUNK, jnp.zeros((bp, GLA_HEADS, GLA_DK, GLA_DV), F32),
                       wgk, gla_gk_b[l][None, :], gla_norm_w[l][None, :])

    ts_pad = SUBLANE
    nb_s = SAMPLE_SEQS_PER_STEP
    proj_s = proj[n_p:].reshape(bs, ts, PROJ_W)
    proj_sp = jnp.pad(proj_s, ((0, 0), (0, ts_pad - ts), (0, 0))).reshape(bs * ts_pad, PROJ_W)
    og_s, gdn_s, conv_s = _gdn(proj_sp, bs, nb_s, ts_pad, ts_pad, ts_pad, ts, state_gdn_conv[l], state_gdn[l],
                               conv_w[l], alog, dtb, gdn_norm_w[l][None, :])
    ol_s, gla_s = _gla(proj_sp, bs, nb_s, ts_pad, ts_pad, ts_pad, ts, state_gla[l], wgk, gla_gk_b[l][None, :],
                       gla_norm_w[l][None, :])
    og_s = og_s.reshape(bs, ts_pad, GDN_V_W)[:, :ts].reshape(n_s, GDN_V_W)
    ol_s = ol_s.reshape(bs, ts_pad, GLA_V_W)[:, :ts].reshape(n_s, GLA_V_W)

    x1, h2, rt, rtt, cnt = _outproj(og_p, og_s, ol_p, ol_s, x_p, x_s, w_out[l].astype(BF16),
                                    rms_ffn_w[l][None, :], wr, br)

    dest_k, dest_b, n_rows, block_meta = _plan(rtt, cnt)
    xs = _dispatch(h2.reshape(-1, PACK_TILES, LANE), dest_k.reshape(-1), n_rows)
    y_rows = _experts(block_meta, xs.reshape(-1, LANE), w_up[l], b_up[l], w_down[l], b_down[l])
    y_p, y_s = _combine(dest_b, rt, y_rows, x1, rms_final_w[None, :], n_p)
    y_prompt = y_p.reshape(bp, tp, d)
    y_sample = y_s.reshape(bs, ts, d)
    return (y_prompt, y_sample, conv_p[None], gdn_p[None], gla_p[None], conv_s[None], gdn_s[None], gla_s[None])
```

```python
import functools

import jax
import jax.numpy as jnp
from jax import lax
from jax.experimental import pallas as pl
from jax.experimental.pallas import tpu as pltpu
from jax.experimental.pallas import tpu_sc as plsc

F32 = jnp.float32
BF16 = jnp.bfloat16
HI = lax.Precision.HIGHEST

D_MODEL = 1024
GDN_HEADS = 4
GDN_DK = 128
GDN_DV = 128
GLA_HEADS = 4
GLA_DK = 64
GLA_DV = 128
GLA_GATE_RANK = 16
GLA_GATE_NORMALIZER = 16.0
CONV_WIDTH = 4
CHUNK = 64
N_EXPERTS = 32
TOP_K = 4
D_FF = 1024
SWIGLU_LIMIT = 7.0
SWIGLU_ALPHA = 1.702
RMS_EPS = 1e-6
L2_EPS = 1e-6

GDN_QK_W = GDN_HEADS * GDN_DK
GDN_V_W = GDN_HEADS * GDN_DV
GDN_CONV_CH = 2 * GDN_QK_W + GDN_V_W
GLA_QK_W = GLA_HEADS * GLA_DK
GLA_V_W = GLA_HEADS * GLA_DV

COL_QKV = 0
COL_Z = 1536
COL_GQ = 2048
COL_GK = 2304
COL_GV = 2560
COL_GG = 3072
COL_SM = 3584
SM_W = 128
PROJ_W = COL_SM + SM_W
SM_A, SM_B, SM_LR = 0, 4, 8

LANE = 128
SUBLANE = 8
TOK_TILES = D_MODEL // LANE
PACK_TILES = TOK_TILES // 2
ROW_TILE = 512
EXPERT_ROWS = 512
EXPERT_WEIGHT_QUEUE = 1
WEIGHT_CAST_ROWS = 128
COMBINE_ROWS = 256
COMBINE_GATHER_SHARE = 0.46
PLAN_TILE_MAX = 2048
DMA_ISSUE_UNROLL = 8
COMBINE_GATHER_QUEUES = (0, 1)
CONV_ROW_SLAB = 128
GDN_CHUNKS_PER_TRIP = 8
GLA_CHUNKS_PER_TRIP = 4
SCAN_CHUNKS_PER_TRIP = 8
PROMPT_TIME_BLOCK = 512
SAMPLE_SEQS_PER_STEP = 8
VMEM_LIMIT = 56 * 1024 * 1024


def _dot(a, b):
    return jnp.dot(a.astype(BF16), b.astype(BF16), preferred_element_type=F32)


def _dot_nt(a, b):
    return lax.dot_general(a.astype(BF16), b.astype(BF16), (((1,), (1,)), ((), ())),
                           preferred_element_type=F32)


def _dot_tn(a, b):
    return lax.dot_general(a.astype(BF16), b.astype(BF16), (((0,), (0,)), ((), ())),
                           preferred_element_type=F32)


def _dot_hi(a, b):
    return jnp.dot(a, b, precision=HI, preferred_element_type=F32)


def _dot_3pass(a, b):
    a_hi = a.astype(BF16)
    b_hi = b.astype(BF16)
    a_lo = (a - a_hi.astype(F32)).astype(BF16)
    b_lo = (b - b_hi.astype(F32)).astype(BF16)

    def mm(x, y):
        return jnp.dot(x, y, preferred_element_type=F32)

    return (mm(a_lo, b_hi) + mm(a_hi, b_lo)) + mm(a_hi, b_hi)


def _rms(x, w):
    return x * lax.rsqrt(jnp.mean(x * x, axis=-1, keepdims=True) + RMS_EPS) * w


def _silu(x):
    return x * jax.nn.sigmoid(x)


def _group_specs(rows, width, n_p_blocks):
    return [pl.BlockSpec((rows, width), lambda i: (jnp.minimum(i, n_p_blocks - 1), 0)),
            pl.BlockSpec((rows, width), lambda i: (jnp.maximum(i - n_p_blocks, 0), 0))]


def _group_pick(i, n_p_blocks, p_ref, s_ref):
    return jnp.where(i < n_p_blocks, p_ref[...], s_ref[...])


def _inproj_body(xp_ref, xs_ref, g_ref, w_ref, o_ref, *, n_p_blocks):
    x = _group_pick(pl.program_id(0), n_p_blocks, xp_ref, xs_ref)
    h = _rms(x, g_ref[...])
    o_ref[...] = jnp.dot(h.astype(BF16), w_ref[...], preferred_element_type=F32)


def _inproj(x_p, x_s, g, w):
    n_p_blocks, n_s_blocks = x_p.shape[0] // ROW_TILE, x_s.shape[0] // ROW_TILE
    n = x_p.shape[0] + x_s.shape[0]
    return pl.pallas_call(
        functools.partial(_inproj_body, n_p_blocks=n_p_blocks),
        grid=(n_p_blocks + n_s_blocks,),
        in_specs=_group_specs(ROW_TILE, D_MODEL, n_p_blocks) + [
            pl.BlockSpec((1, D_MODEL), lambda i: (0, 0)),
            pl.BlockSpec((D_MODEL, PROJ_W), lambda i: (0, 0)),
        ],
        out_specs=pl.BlockSpec((ROW_TILE, PROJ_W), lambda i: (i, 0)),
        out_shape=jax.ShapeDtypeStruct((n, PROJ_W), F32),
        compiler_params=pltpu.CompilerParams(dimension_semantics=("arbitrary",),
                                             vmem_limit_bytes=VMEM_LIMIT),
        name="in_proj",
    )(x_p, x_s, g, w)


def _log2(n):
    assert n & (n - 1) == 0
    return n.bit_length() - 1


def _tri_inv_all(ms, c, ii, jj):
    eye = (ii == jj).astype(F32)
    base = min(c, 8)
    sh = _log2(base)
    blk = (ii >> sh) == (jj >> sh)
    ns = [jnp.where(blk, m, 0.0) for m in ms]
    xs = [eye - n for n in ns]
    ps = [_dot(n, n) for n in ns]
    ts = [_dot(jnp.concatenate([x, p], axis=0), p) for x, p in zip(xs, ps)]
    xs = [x + t[:c] for x, t in zip(xs, ts)]
    ps = [t[c:] for t in ts]
    xs = [x + _dot(x, p) for x, p in zip(xs, ps)]
    s = base
    while s < c:
        sh_s, sh_b = _log2(s), _log2(2 * s)
        off = ((ii >> sh_b) == (jj >> sh_b)) & ((ii >> sh_s) != (jj >> sh_s))
        ys = [_dot(x, jnp.where(off, m, 0.0)) for x, m in zip(xs, ms)]
        xs = [x - _dot(y, x) for x, y in zip(xs, ys)]
        s *= 2
    return xs


def _gated_norm(o, w, z):
    return o * lax.rsqrt(jnp.mean(o * o, axis=-1, keepdims=True) + RMS_EPS) * w * _silu(z)


def _chunk_rows(s, tb_rows, ci, c):
    r = s * tb_rows + ci * c
    if not isinstance(r, int):
        r = pl.multiple_of(r, c)
    return r


def _for_chunks(n_chunks, step):
    if n_chunks == 1:
        step(0, 0)
    else:
        lax.fori_loop(0, n_chunks, step, 0)


def _gdn_body(qkv_ref, z_ref, sm_ref, cbuf_ref, s0_ref, cw_ref, alog_ref, dtb_ref, nw_ref,
              o_ref, sout_ref, cout_ref, st, xc, act, gcs, us, wss, qgs, kds, aqs,
              *, nb, tb_rows, chunk, valid, n_tb):
    tb = pl.program_id(1)
    c = chunk
    n_heads = GDN_HEADS
    tail = CONV_WIDTH - 1
    pad = SUBLANE
    units = [(s, h) for s in range(nb) for h in range(n_heads)]

    n_slabs = GDN_CONV_CH // LANE

    def lanes(j):
        return slice(j * LANE, (j + 1) * LANE)

    @pl.when(tb == 0)
    def _():
        st[...] = s0_ref[...]
        for s in range(nb):
            for j in range(n_slabs):
                xc[s, j, pad - tail:pad, :] = cbuf_ref[s, :, lanes(j)]

    if n_tb > 1:
        @pl.when(tb > 0)
        def _():
            for s in range(nb):
                for j in range(n_slabs):
                    xc[s, j, pad - tail:pad, :] = xc[s, j, tb_rows + pad - tail:tb_rows + pad, :]

    for s in range(nb):
        for j in range(n_slabs):
            xc[s, j, pad:pad + tb_rows, :] = qkv_ref[s * tb_rows:(s + 1) * tb_rows, lanes(j)]

    row_slab = min(tb_rows, CONV_ROW_SLAB)
    parities = 2 if row_slab >= 2 * SUBLANE else 1
    for s in range(nb):
        for j in range(n_slabs):
            src, dst = xc.at[s, j], act.at[j]
            for sl in range(tb_rows // row_slab):
                for p in range(parities):
                    lo = pad - tail + sl * row_slab + p
                    out0 = s * tb_rows + sl * row_slab + p

                    def rows_from(start):
                        if parities == 1:
                            return pl.ds(start, row_slab)
                        return pl.ds(start, row_slab // 2, stride=2)

                    acc = src[rows_from(lo), :] * cw_ref[0:1, lanes(j)]
                    for i in range(1, CONV_WIDTH):
                        acc = acc + src[rows_from(lo + i), :] * cw_ref[i:i + 1, lanes(j)]
                    dst[rows_from(out0), :] = _silu(acc)

    ii = lax.broadcasted_iota(jnp.int32, (c, c), 0)
    jj = lax.broadcasted_iota(jnp.int32, (c, c), 1)
    lower = (ii >= jj)
    lower_f = lower.astype(F32)
    strict = (ii > jj)
    rowmask = None
    if valid < c:
        rowmask = lax.broadcasted_iota(jnp.int32, (c, 1), 0) < valid

    def hs(h, w):
        return slice(h * w, (h + 1) * w)

    n_chunks = tb_rows // c
    cpi = next(k for k in (GDN_CHUNKS_PER_TRIP, 2, 1) if n_chunks % k == 0)
    p1_units = [(g, h) for g in range(nb * cpi) for h in range(n_heads)]

    def phase1(ci, carry):
        rows, b_ts, gc_ts, gc_tts = [], [], [], []
        for g in range(nb * cpi):
            rr = pl.ds(_chunk_rows(g // cpi, tb_rows, ci * cpi + g % cpi, c), c)
            sm = sm_ref[rr, :]
            g_t = -jnp.exp(alog_ref[...]) * jax.nn.softplus(sm + dtb_ref[...])
            b_t = jax.nn.sigmoid(sm)
            if rowmask is not None:
                g_t = jnp.where(rowmask, g_t, 0.0)
                b_t = jnp.where(rowmask, b_t, 0.0)
            gc_t = _dot_hi(lower_f, g_t)
            gcs[rr, :] = gc_t
            rows.append(rr)
            b_ts.append(b_t)
            gc_ts.append(gc_t)
            gc_tts.append(gc_t.T)
        qn, kn, kb, vb = {}, {}, {}, {}
        for (s, h) in p1_units:
            q = act[h, rows[s], :]
            k = act[n_heads + h, rows[s], :]
            v = act[2 * n_heads + h, rows[s], :]
            if rowmask is not None:
                q = jnp.where(rowmask, q, 0.0)
                k = jnp.where(rowmask, k, 0.0)
                v = jnp.where(rowmask, v, 0.0)
            qn[s, h] = q * lax.rsqrt(jnp.sum(q * q, axis=-1, keepdims=True) + L2_EPS) * (GDN_DK ** -0.5)
            kn[s, h] = k * lax.rsqrt(jnp.sum(k * k, axis=-1, keepdims=True) + L2_EPS)
            beta = b_ts[s][:, SM_B + h:SM_B + h + 1]
            kb[s, h] = kn[s, h] * beta
            vb[s, h] = v * beta
        s1 = {u: _dot_nt(jnp.concatenate([kb[u], qn[u]], axis=0), kn[u]) for u in p1_units}
        mm = []
        for (s, h) in p1_units:
            gcol = gc_ts[s][:, SM_A + h:SM_A + h + 1]
            grow = gc_tts[s][SM_A + h:SM_A + h + 1, :]
            dec = jnp.exp(jnp.where(lower, gcol - grow, -jnp.inf))
            mm.append(jnp.where(strict, s1[s, h][:c] * dec, 0.0))
            aqs[h, rows[s], :] = s1[s, h][c:] * dec
        tms = _tri_inv_all(mm, c, ii, jj)
        for (s, h), tm in zip(p1_units, tms):
            gcol = gc_ts[s][:, SM_A + h:SM_A + h + 1]
            eg = jnp.exp(gcol)
            uw = _dot(tm, jnp.concatenate([vb[s, h], kb[s, h] * eg], axis=1))
            us[rows[s], hs(h, GDN_DV)] = uw[:, :GDN_DV]
            wss[rows[s], hs(h, GDN_DV)] = uw[:, GDN_DV:]
            qgs[rows[s], hs(h, GDN_DK)] = qn[s, h] * eg
            kds[rows[s], hs(h, GDN_DK)] = kn[s, h] * jnp.exp(gcol[c - 1:c, :] - gcol)
        return carry

    cp2 = next(k for k in (SCAN_CHUNKS_PER_TRIP, 1) if n_chunks % k == 0)

    def phase2(ti, carry):
        for u in range(cp2):
            scan_chunk(ti * cp2 + u)
        return carry

    def scan_chunk(ci):
        r0 = [_chunk_rows(s, tb_rows, ci, c) for s in range(nb)]
        rows = [pl.ds(r, c) for r in r0]
        ws = {(s, h): _dot(jnp.concatenate([wss[rows[s], hs(h, GDN_DV)], qgs[rows[s], hs(h, GDN_DK)]], axis=0),
                           st[s, h]) for (s, h) in units}
        v_new = {(s, h): us[rows[s], hs(h, GDN_DV)] - ws[s, h][:c] for (s, h) in units}
        o = {(s, h): ws[s, h][c:] + _dot(aqs[h, rows[s], :], v_new[s, h]) for (s, h) in units}
        upd = {(s, h): _dot_tn(kds[rows[s], hs(h, GDN_DK)], v_new[s, h]) for (s, h) in units}
        for (s, h) in units:
            g_last = gcs[pl.ds(r0[s] + c - 1, 1), SM_A + h:SM_A + h + 1]
            st[s, h] = st[s, h] * jnp.exp(g_last) + upd[s, h]
        for s in range(nb):
            o_ref[rows[s], :] = jnp.concatenate(
                [_gated_norm(o[s, h], nw_ref[...], z_ref[rows[s], hs(h, GDN_DV)]) for h in range(n_heads)], axis=1)

    _for_chunks(n_chunks // cpi, phase1)
    _for_chunks(n_chunks // cp2, phase2)

    @pl.when(tb == n_tb - 1)
    def _():
        sout_ref[...] = st[...]
        last = tb_rows if valid == c else valid
        for s in range(nb):
            cout_ref[s] = jnp.concatenate(
                [xc[s, j, pad + last - tail:pad + last, :] for j in range(n_slabs)], axis=1)


def _gdn(proj, n_seq, nb, t_len, tb_rows, chunk, valid, conv_buf, s0, conv_w, alog, dtb, nw):
    n_tb = t_len // tb_rows
    assert nb == 1 or n_tb == 1
    rows = nb * tb_rows

    def rowblk(b, t):
        return b * n_tb + t

    body = functools.partial(_gdn_body, nb=nb, tb_rows=tb_rows, chunk=chunk, valid=valid, n_tb=n_tb)
    return pl.pallas_call(
        body,
        grid=(n_seq // nb, n_tb),
        in_specs=[
            pl.BlockSpec((rows, GDN_CONV_CH), lambda b, t: (rowblk(b, t), COL_QKV // GDN_CONV_CH)),
            pl.BlockSpec((rows, GDN_V_W), lambda b, t: (rowblk(b, t), COL_Z // GDN_V_W)),
            pl.BlockSpec((rows, SM_W), lambda b, t: (rowblk(b, t), COL_SM // SM_W)),
            pl.BlockSpec((nb, CONV_WIDTH - 1, GDN_CONV_CH), lambda b, t: (b, 0, 0)),
            pl.BlockSpec((nb, GDN_HEADS, GDN_DK, GDN_DV), lambda b, t: (b, 0, 0, 0)),
            pl.BlockSpec((CONV_WIDTH, GDN_CONV_CH), lambda b, t: (0, 0)),
            pl.BlockSpec((1, SM_W), lambda b, t: (0, 0)),
            pl.BlockSpec((1, SM_W), lambda b, t: (0, 0)),
            pl.BlockSpec((1, GDN_DV), lambda b, t: (0, 0)),
        ],
        out_specs=[
            pl.BlockSpec((rows, GDN_V_W), lambda b, t: (rowblk(b, t), 0)),
            pl.BlockSpec((nb, GDN_HEADS, GDN_DK, GDN_DV), lambda b, t: (b, 0, 0, 0)),
            pl.BlockSpec((nb, CONV_WIDTH - 1, GDN_CONV_CH), lambda b, t: (b, 0, 0)),
        ],
        out_shape=[
            jax.ShapeDtypeStruct((n_seq * t_len, GDN_V_W), F32),
            jax.ShapeDtypeStruct((n_seq, GDN_HEADS, GDN_DK, GDN_DV), F32),
            jax.ShapeDtypeStruct((n_seq, CONV_WIDTH - 1, GDN_CONV_CH), F32),
        ],
        scratch_shapes=[
            pltpu.VMEM((nb, GDN_HEADS, GDN_DK, GDN_DV), F32),
            pltpu.VMEM((nb, GDN_CONV_CH // LANE, tb_rows + SUBLANE, LANE), F32),
            pltpu.VMEM((GDN_CONV_CH // LANE, rows, LANE), F32),
            pltpu.VMEM((rows, SM_W), F32),
            pltpu.VMEM((rows, GDN_V_W), F32),
            pltpu.VMEM((rows, GDN_V_W), F32),
            pltpu.VMEM((rows, GDN_QK_W), F32),
            pltpu.VMEM((rows, GDN_QK_W), F32),
            pltpu.VMEM((GDN_HEADS, rows, chunk), F32),
        ],
        compiler_params=pltpu.CompilerParams(dimension_semantics=("arbitrary", "arbitrary"),
                                             vmem_limit_bytes=VMEM_LIMIT),
        name="gdn_mixer",
    )(proj, proj, proj, conv_buf, s0, conv_w, alog, dtb, nw)


def _gla_body(q_ref, k_ref, v_ref, go_ref, sm_ref, s0_ref, wgk_ref, bgk_ref, nw_ref,
              o_ref, sout_ref, st, qes, ois, upds, decs, *, nb, tb_rows, chunk, valid, n_tb):
    tb = pl.program_id(1)
    c = chunk
    n_heads = GLA_HEADS
    units = [(s, h) for s in range(nb) for h in range(n_heads)]

    @pl.when(tb == 0)
    def _():
        st[...] = s0_ref[...]

    ii = lax.broadcasted_iota(jnp.int32, (c, c), 0)
    jj = lax.broadcasted_iota(jnp.int32, (c, c), 1)
    lower = (ii >= jj)
    lower_f = lower.astype(F32)
    rid = lax.broadcasted_iota(jnp.int32, (c, 1), 0)
    rowmask = (rid < valid) if valid < c else None
    n_sub = max(c // 16, 1)
    sub = c // n_sub

    n_chunks = tb_rows // c
    cpi = next(k for k in (GLA_CHUNKS_PER_TRIP, 2, 1) if n_chunks % k == 0)
    p1_units = [(g, h) for g in range(nb * cpi) for h in range(n_heads)]

    def phase1(ci, carry):
        rows, slots, bcs, bc_ts = [], [], [], []
        for g in range(nb * cpi):
            chunk_idx = ci * cpi + g % cpi
            rr = pl.ds(_chunk_rows(g // cpi, tb_rows, chunk_idx, c), c)
            slots.append((g // cpi) * n_chunks + chunk_idx)
            gk = jax.nn.log_sigmoid(_dot(sm_ref[rr, :], wgk_ref[...]) + bgk_ref[...]) / GLA_GATE_NORMALIZER
            if rowmask is not None:
                gk = jnp.where(rowmask, gk, 0.0)
            bc = _dot_hi(lower_f, gk)
            rows.append(rr)
            bcs.append(bc)
            bc_ts.append(bc.T)
        q, k, v, bch = {}, {}, {}, {}
        for (s, h) in p1_units:
            ks = slice(h * GLA_DK, (h + 1) * GLA_DK)
            vs = slice(h * GLA_DV, (h + 1) * GLA_DV)
            q[s, h] = q_ref[rows[s], ks] * (GLA_DK ** -0.5)
            kk = k_ref[rows[s], ks]
            vv = v_ref[rows[s], vs]
            if rowmask is not None:
                kk = jnp.where(rowmask, kk, 0.0)
                vv = jnp.where(rowmask, vv, 0.0)
            k[s, h], v[s, h] = kk, vv
            bch[s, h] = bcs[s][:, ks]
        for (g, h) in p1_units:
            qes[h, rows[g], :] = q[g, h] * jnp.exp(bch[g, h])
        a = {}
        for u in p1_units:
            q_parts, k_parts = [], []
            for sb in range(n_sub):
                ref_row = bch[u][sb * sub:sb * sub + 1, :]
                in_blk = (rid >= sb * sub) & (rid < (sb + 1) * sub)
                q_parts.append(jnp.where(in_blk, q[u] * jnp.exp(jnp.where(in_blk, bch[u] - ref_row, 0.0)), 0.0))
                k_parts.append(k[u] * jnp.exp(jnp.where(rid < (sb + 1) * sub, ref_row - bch[u], 0.0)))
            q_hat = jnp.concatenate(q_parts, axis=1) if n_sub > 1 else q_parts[0]
            k_hat = jnp.concatenate(k_parts, axis=1) if n_sub > 1 else k_parts[0]
            a[u] = jnp.where(lower, _dot_nt(q_hat, k_hat), 0.0)
        upd = {u: _dot_tn(k[u] * jnp.exp(bch[u][c - 1:c, :] - bch[u]), v[u]) for u in p1_units}
        o_intra = {u: _dot(a[u], v[u]) for u in p1_units}
        for (g, h) in p1_units:
            dec_col = bc_ts[g][h * GLA_DK:(h + 1) * GLA_DK, c - 1:c]
            decs[slots[g], h] = jnp.broadcast_to(jnp.exp(dec_col), (GLA_DK, GLA_DV))
            upds[slots[g], h] = upd[g, h]
            ois[rows[g], h * GLA_DV:(h + 1) * GLA_DV] = o_intra[g, h]
        return carry

    cp2 = next(k for k in (SCAN_CHUNKS_PER_TRIP, 1) if n_chunks % k == 0)

    def phase2(ti, carry):
        for u in range(cp2):
            scan_chunk(ti * cp2 + u)
        return carry

    def scan_chunk(ci):
        rows = [pl.ds(_chunk_rows(s, tb_rows, ci, c), c) for s in range(nb)]
        o = {(s, h): ois[rows[s], h * GLA_DV:(h + 1) * GLA_DV] + _dot(qes[h, rows[s], :], st[s, h])
             for (s, h) in units}
        for (s, h) in units:
            st[s, h] = decs[s * n_chunks + ci, h] * st[s, h] + upds[s * n_chunks + ci, h]
        for s in range(nb):
            o_ref[rows[s], :] = jnp.concatenate(
                [_gated_norm(o[s, h], nw_ref[...], go_ref[rows[s], h * GLA_DV:(h + 1) * GLA_DV])
                 for h in range(n_heads)], axis=1)

    _for_chunks(n_chunks // cpi, phase1)
    _for_chunks(n_chunks // cp2, phase2)

    @pl.when(tb == n_tb - 1)
    def _():
        sout_ref[...] = st[...]


def _gla(proj, n_seq, nb, t_len, tb_rows, chunk, valid, s0, wgk, bgk, nw):
    n_tb = t_len // tb_rows
    assert nb == 1 or n_tb == 1
    rows = nb * tb_rows

    def rowblk(b, t):
        return b * n_tb + t

    body = functools.partial(_gla_body, nb=nb, tb_rows=tb_rows, chunk=chunk, valid=valid, n_tb=n_tb)
    return pl.pallas_call(
        body,
        grid=(n_seq // nb, n_tb),
        in_specs=[
            pl.BlockSpec((rows, GLA_QK_W), lambda b, t: (rowblk(b, t), COL_GQ // GLA_QK_W)),
            pl.BlockSpec((rows, GLA_QK_W), lambda b, t: (rowblk(b, t), COL_GK // GLA_QK_W)),
            pl.BlockSpec((rows, GLA_V_W), lambda b, t: (rowblk(b, t), COL_GV // GLA_V_W)),
            pl.BlockSpec((rows, GLA_V_W), lambda b, t: (rowblk(b, t), COL_GG // GLA_V_W)),
            pl.BlockSpec((rows, SM_W), lambda b, t: (rowblk(b, t), COL_SM // SM_W)),
            pl.BlockSpec((nb, GLA_HEADS, GLA_DK, GLA_DV), lambda b, t: (b, 0, 0, 0)),
            pl.BlockSpec((SM_W, GLA_QK_W), lambda b, t: (0, 0)),
            pl.BlockSpec((1, GLA_QK_W), lambda b, t: (0, 0)),
            pl.BlockSpec((1, GLA_DV), lambda b, t: (0, 0)),
        ],
        out_specs=[
            pl.BlockSpec((rows, GLA_V_W), lambda b, t: (rowblk(b, t), 0)),
            pl.BlockSpec((nb, GLA_HEADS, GLA_DK, GLA_DV), lambda b, t: (b, 0, 0, 0)),
        ],
        out_shape=[
            jax.ShapeDtypeStruct((n_seq * t_len, GLA_V_W), F32),
            jax.ShapeDtypeStruct((n_seq, GLA_HEADS, GLA_DK, GLA_DV), F32),
        ],
        scratch_shapes=[
            pltpu.VMEM((nb, GLA_HEADS, GLA_DK, GLA_DV), F32),
            pltpu.VMEM((GLA_HEADS, rows, GLA_DK), F32),
            pltpu.VMEM((rows, GLA_V_W), F32),
            pltpu.VMEM((rows // chunk, GLA_HEADS, GLA_DK, GLA_DV), F32),
            pltpu.VMEM((rows // chunk, GLA_HEADS, GLA_DK, GLA_DV), F32),
        ],
        compiler_params=pltpu.CompilerParams(dimension_semantics=("arbitrary", "arbitrary"),
                                             vmem_limit_bytes=VMEM_LIMIT),
        name="gla_mixer",
    )(proj, proj, proj, proj, proj, s0, wgk, bgk, nw)


def _outproj_body(ogp_ref, ogs_ref, olp_ref, ols_ref, xp_ref, xs_ref, wo_ref, g_ref, wr_ref, br_ref,
                  x1_ref, h2_ref, rt_ref, rtt_ref, cnt_ref, base, *, n_p_blocks):
    i = pl.program_id(0)

    @pl.when(i == 0)
    def _():
        base[...] = jnp.zeros_like(base)

    o = jnp.concatenate([_group_pick(i, n_p_blocks, ogp_ref, ogs_ref),
                         _group_pick(i, n_p_blocks, olp_ref, ols_ref)], axis=1)
    x1 = _group_pick(i, n_p_blocks, xp_ref, xs_ref) + jnp.dot(o.astype(BF16), wo_ref[...],
                                                               preferred_element_type=F32)
    x1_ref[...] = x1
    h = _rms(x1, g_ref[...])
    _store_token_tiles(h2_ref, _pack_bf16_pairs(h))
    logits = _dot_3pass(h, wr_ref[...]) + br_ref[...]

    tm = logits.shape[0]
    lt = logits.T[:N_EXPERTS]
    eid = lax.broadcasted_iota(jnp.int32, (N_EXPERTS, tm), 0)
    work = lt
    sel = jnp.zeros((N_EXPERTS, tm), F32)
    hits, ids, vals = [], [], []
    for _ in range(TOP_K):
        m = jnp.max(work, axis=0, keepdims=True)
        idx = jnp.min(jnp.where(work == m, eid, N_EXPERTS), axis=0, keepdims=True)
        hit = eid == idx
        hits.append(hit)
        ids.append(idx)
        vals.append(m)
        work = jnp.where(hit, -jnp.inf, work)
        sel = sel + hit.astype(F32)
    exps = [jnp.exp(v - vals[0]) for v in vals]
    den = exps[0]
    for e in exps[1:]:
        den = den + e
    gates = [e / den for e in exps]

    ri = lax.broadcasted_iota(jnp.int32, (tm, tm), 0)
    ci = lax.broadcasted_iota(jnp.int32, (tm, tm), 1)
    before = _dot(sel, (ri < ci).astype(F32)) + base[...]
    ranks = [jnp.sum(jnp.where(hit, before, 0.0), axis=0, keepdims=True) for hit in hits]
    base[...] = base[...] + jnp.sum(sel, axis=1, keepdims=True)
    cnt_ref[...] = base[...]

    row = lax.broadcasted_iota(jnp.int32, (LANE, tm), 0)
    rec = jnp.zeros((LANE, tm), F32)
    for k in range(TOP_K):
        rec = jnp.where(row == k, ids[k].astype(F32), rec)
        rec = jnp.where(row == TOP_K + k, ranks[k], rec)
        rec = jnp.where(row == 2 * TOP_K + k, gates[k], rec)
    rt_ref[...] = rec.T
    rtt_ref[...] = rec[:2 * TOP_K]


def _outproj(og_p, og_s, ol_p, ol_s, x_p, x_s, wo, g, wr, br):
    n_p_blocks, n_s_blocks = x_p.shape[0] // ROW_TILE, x_s.shape[0] // ROW_TILE
    n = x_p.shape[0] + x_s.shape[0]
    return pl.pallas_call(
        functools.partial(_outproj_body, n_p_blocks=n_p_blocks),
        grid=(n_p_blocks + n_s_blocks,),
        in_specs=_group_specs(ROW_TILE, GDN_V_W, n_p_blocks) + _group_specs(ROW_TILE, GLA_V_W, n_p_blocks)
        + _group_specs(ROW_TILE, D_MODEL, n_p_blocks) + [
            pl.BlockSpec((D_MODEL, D_MODEL), lambda i: (0, 0)),
            pl.BlockSpec((1, D_MODEL), lambda i: (0, 0)),
            pl.BlockSpec((D_MODEL, LANE), lambda i: (0, 0)),
            pl.BlockSpec((1, LANE), lambda i: (0, 0)),
        ],
        out_specs=[
            pl.BlockSpec((ROW_TILE, D_MODEL), lambda i: (i, 0)),
            pl.BlockSpec((ROW_TILE * PACK_TILES, LANE), lambda i: (i, 0)),
            pl.BlockSpec((ROW_TILE, LANE), lambda i: (i, 0)),
            pl.BlockSpec((2 * TOP_K, ROW_TILE), lambda i: (0, i)),
            pl.BlockSpec((N_EXPERTS, 1), lambda i: (0, 0)),
        ],
        out_shape=[
            jax.ShapeDtypeStruct((n, D_MODEL), F32),
            jax.ShapeDtypeStruct((n * PACK_TILES, LANE), jnp.uint32),
            jax.ShapeDtypeStruct((n, LANE), F32),
            jax.ShapeDtypeStruct((2 * TOP_K, n), F32),
            jax.ShapeDtypeStruct((N_EXPERTS, 1), F32),
        ],
        scratch_shapes=[pltpu.VMEM((N_EXPERTS, 1), F32)],
        compiler_params=pltpu.CompilerParams(dimension_semantics=("arbitrary",),
                                             vmem_limit_bytes=VMEM_LIMIT),
        name="out_proj",
    )(og_p, og_s, ol_p, ol_s, x_p, x_s, wo, g, wr, br)


def _store_token_tiles(ref2d, val):
    rows, tiles = val.shape[0], val.shape[1] // LANE
    for c in range(tiles):
        ref2d[pl.ds(c, rows, stride=tiles), :] = val[:, c * LANE:(c + 1) * LANE]


def _load_token_tiles(ref2d, first_row, rows, tiles=TOK_TILES):
    return jnp.concatenate(
        [ref2d[pl.ds(first_row * tiles + c, rows, stride=tiles), :] for c in range(tiles)], axis=1)


def _pack_bf16_pairs(x):
    half = x.shape[1] // 2
    bits = lax.bitcast_convert_type(x.astype(BF16).astype(F32), jnp.uint32)
    return (bits[:, :half] >> 16) | (bits[:, half:] & jnp.uint32(0xFFFF0000))


def _unpack_bf16_pairs(w):
    lo = lax.bitcast_convert_type(w << 16, F32)
    hi = lax.bitcast_convert_type(w & jnp.uint32(0xFFFF0000), F32)
    return jnp.concatenate([lo, hi], axis=1).astype(BF16)


def _expert_weight_copies(e, ws, wup_hbm, wdn_hbm, wup_buf, wdn_buf, wsems):
    return (pltpu.make_async_copy(wup_hbm.at[e], wup_buf.at[ws], wsems.at[ws]),
            pltpu.make_async_copy(wdn_hbm.at[e], wdn_buf.at[ws], wsems.at[ws]))


def _expert_body(be_ref, nu_ref, first_ref, wslot_ref, next_ref, valid_ref, x_ref,
                 wup_hbm, bup_ref, wdn_hbm, bdn_ref, y_ref, wup_buf, wdn_buf, wsems, wup_bf, wdn_bf):
    i = pl.program_id(0)
    n_used = nu_ref[0]
    ws = wslot_ref[i]
    weight_copies = functools.partial(_expert_weight_copies, wup_hbm=wup_hbm, wdn_hbm=wdn_hbm, wup_buf=wup_buf,
                                      wdn_buf=wdn_buf, wsems=wsems)

    @pl.when((i == 0) & (n_used > 0))
    def _():
        for cp in weight_copies(be_ref[0], ws):
            cp.start(priority=EXPERT_WEIGHT_QUEUE)

    @pl.when(i < n_used)
    def _():
        @pl.when(first_ref[i] == 1)
        def _():
            for cp in weight_copies(be_ref[i], ws):
                cp.wait()

            @pl.when(next_ref[i] >= 0)
            def _():
                for cp in weight_copies(next_ref[i], 1 - ws):
                    cp.start(priority=EXPERT_WEIGHT_QUEUE)

            for r in range(0, D_MODEL, WEIGHT_CAST_ROWS):
                wup_bf[r:r + WEIGHT_CAST_ROWS, :] = wup_buf[ws, r:r + WEIGHT_CAST_ROWS, :].astype(BF16)
            for r in range(0, D_FF, WEIGHT_CAST_ROWS):
                wdn_bf[r:r + WEIGHT_CAST_ROWS, :] = wdn_buf[ws, r:r + WEIGHT_CAST_ROWS, :].astype(BF16)

        def expert_rows(n):
            x = _unpack_bf16_pairs(_load_token_tiles(x_ref, 0, n, PACK_TILES))
            gu = _dot(x, wup_bf[...]) + bup_ref[...]
            gate = jnp.minimum(gu[:, :D_FF], SWIGLU_LIMIT)
            up = jnp.clip(gu[:, D_FF:], -SWIGLU_LIMIT, SWIGLU_LIMIT)
            a = (up + 1.0) * gate * jax.nn.sigmoid(SWIGLU_ALPHA * gate)
            _store_token_tiles(y_ref, _dot(a, wdn_bf[...]) + bdn_ref[...])

        half = EXPERT_ROWS // 2

        @pl.when(valid_ref[i] > half)
        def _():
            expert_rows(EXPERT_ROWS)

        @pl.when(valid_ref[i] <= half)
        def _():
            expert_rows(half)
            y_ref[half * TOK_TILES:, :] = jnp.zeros((half * TOK_TILES, LANE), F32)

    @pl.when(i >= n_used)
    def _():
        y_ref[...] = jnp.zeros_like(y_ref)


def _experts(block_meta, xs_2d, w_up, b_up, w_down, b_down):
    n_blocks = block_meta[0].shape[0]
    grid_spec = pltpu.PrefetchScalarGridSpec(
        num_scalar_prefetch=len(block_meta),
        grid=(n_blocks,),
        in_specs=[
            pl.BlockSpec((EXPERT_ROWS * PACK_TILES, LANE), lambda i, *_: (i, 0)),
            pl.BlockSpec(memory_space=pl.ANY),
            pl.BlockSpec((None, 1, 2 * D_FF), lambda i, be, *_: (be[i], 0, 0)),
            pl.BlockSpec(memory_space=pl.ANY),
            pl.BlockSpec((None, 1, D_MODEL), lambda i, be, *_: (be[i], 0, 0)),
        ],
        out_specs=pl.BlockSpec((EXPERT_ROWS * TOK_TILES, LANE), lambda i, *_: (i, 0)),
        scratch_shapes=[
            pltpu.VMEM((2, D_MODEL, 2 * D_FF), F32),
            pltpu.VMEM((2, D_FF, D_MODEL), F32),
            pltpu.SemaphoreType.DMA((2,)),
            pltpu.VMEM((D_MODEL, 2 * D_FF), BF16),
            pltpu.VMEM((D_FF, D_MODEL), BF16),
        ],
    )
    return pl.pallas_call(
        _expert_body,
        grid_spec=grid_spec,
        out_shape=jax.ShapeDtypeStruct((n_blocks * EXPERT_ROWS * TOK_TILES, LANE), F32),
        compiler_params=pltpu.CompilerParams(dimension_semantics=("arbitrary",),
                                             vmem_limit_bytes=VMEM_LIMIT),
        name="experts",
    )(*block_meta, xs_2d, w_up, b_up.reshape(N_EXPERTS, 1, 2 * D_FF), w_down,
      b_down.reshape(N_EXPERTS, 1, D_MODEL))


def _dispatch(h_tiles, dest_kmajor, n_rows):
    n_tok = h_tiles.shape[0]
    info = plsc.get_sparse_core_info()
    n_workers = info.num_cores * info.num_subcores
    per_worker = n_tok // n_workers
    chunk = next(c for c in (128, 96, 88, 64, 48, 32, 16, 8) if per_worker % c == 0)
    assert n_tok % n_workers == 0 and per_worker % SUBLANE == 0
    mesh = plsc.VectorSubcoreMesh(core_axis_name="c", subcore_axis_name="s")

    @functools.partial(
        pl.kernel, mesh=mesh,
        out_type=jax.ShapeDtypeStruct((n_rows,) + h_tiles.shape[1:], h_tiles.dtype),
        scratch_types=[pltpu.VMEM((TOP_K, chunk), jnp.int32), pltpu.VMEM((chunk,) + h_tiles.shape[1:], h_tiles.dtype),
                       pltpu.SemaphoreType.DMA],
    )
    def dispatch(h_hbm, dest_hbm, out_hbm, idx_v, rows_v, sem):
        wid = lax.axis_index("s") * info.num_cores + lax.axis_index("c")

        def step(j, carry):
            t0 = pl.multiple_of(wid * per_worker + j * chunk, SUBLANE)
            loads = [pltpu.async_copy(h_hbm.at[pl.ds(t0, chunk)], rows_v, sem)]
            for k in range(TOP_K):
                loads.append(pltpu.async_copy(
                    dest_hbm.at[pl.ds(pl.multiple_of(k * n_tok + t0, SUBLANE), chunk)], idx_v.at[k], sem))
            for cp in loads:
                cp.wait()
            stores = [pltpu.async_copy(rows_v, out_hbm.at[idx_v.at[k]], sem) for k in range(TOP_K)]
            for cp in stores:
                cp.wait()
            return carry

        lax.fori_loop(0, per_worker // chunk, step, 0)

    return dispatch(h_tiles, dest_kmajor)


def _gather_rows(src_tiles, idx_ref, n_rows, dst2d, sem, priorities):
    def issue(j, carry):
        for u in range(DMA_ISSUE_UNROLL):
            r = j * DMA_ISSUE_UNROLL + u
            dst = dst2d.at[pl.ds(pl.multiple_of(r * TOK_TILES, TOK_TILES), TOK_TILES), :]
            pltpu.make_async_copy(src_tiles.at[idx_ref[0, r]], dst, sem).start(
                priority=priorities[u % len(priorities)])
        return carry

    lax.fori_loop(0, n_rows // DMA_ISSUE_UNROLL, issue, 0)


def _wait_rows(src2d, n_rows, dst2d, sem):
    pltpu.make_async_copy(src2d.at[pl.ds(0, n_rows * TOK_TILES), :], dst2d, sem).wait()


def _combine_body(dest_ref, dest_next_ref, rt_ref, y_tiles, y_2d, x1_ref, g_ref, o_ref, ybuf, sems):
    i = pl.program_id(0)
    slot = i % 2
    n_rows = TOP_K * COMBINE_ROWS

    @pl.when(i == 0)
    def _():
        _gather_rows(y_tiles, dest_ref, n_rows, ybuf.at[0], sems.at[0], COMBINE_GATHER_QUEUES)

    _wait_rows(y_2d, n_rows, ybuf.at[slot], sems.at[slot])

    @pl.when(i + 1 < pl.num_programs(0))
    def _():
        _gather_rows(y_tiles, dest_next_ref, n_rows, ybuf.at[1 - slot], sems.at[1 - slot], COMBINE_GATHER_QUEUES)

    buf = ybuf.at[slot]
    moe = _load_token_tiles(buf, 0, COMBINE_ROWS) * rt_ref[:, 2 * TOP_K:2 * TOP_K + 1]
    for k in range(1, TOP_K):
        moe = moe + _load_token_tiles(buf, k * COMBINE_ROWS, COMBINE_ROWS) * rt_ref[:, 2 * TOP_K + k:2 * TOP_K + k + 1]
    o_ref[...] = _rms(x1_ref[...] + moe, g_ref[...])


def _combine(dest_b, rt, y_2d, x1, g, n_p, n_a):
    n = x1.shape[0]
    n_blk = n // COMBINE_ROWS
    assert n_a <= n_p // COMBINE_ROWS
    dest_blocks = dest_b.reshape(n_blk, 1, TOP_K * COMBINE_ROWS)
    return pl.pallas_call(
        _combine_body,
        grid=(n_a,),
        in_specs=[
            pl.BlockSpec((None, 1, COMBINE_ROWS * TOP_K), lambda i: (i, 0, 0), memory_space=pltpu.SMEM),
            pl.BlockSpec((None, 1, COMBINE_ROWS * TOP_K), lambda i: (jnp.minimum(i + 1, n_blk - 1), 0, 0),
                         memory_space=pltpu.SMEM),
            pl.BlockSpec((COMBINE_ROWS, LANE), lambda i: (i, 0)),
            pl.BlockSpec(memory_space=pl.ANY),
            pl.BlockSpec(memory_space=pl.ANY),
            pl.BlockSpec((COMBINE_ROWS, D_MODEL), lambda i: (i, 0)),
            pl.BlockSpec((1, D_MODEL), lambda i: (0, 0)),
        ],
        out_specs=pl.BlockSpec((COMBINE_ROWS, D_MODEL), lambda i: (i, 0)),
        out_shape=jax.ShapeDtypeStruct((n_p, D_MODEL), F32),
        scratch_shapes=[pltpu.VMEM((2, TOP_K * COMBINE_ROWS * TOK_TILES, LANE), F32),
                        pltpu.SemaphoreType.DMA((2,))],
        compiler_params=pltpu.CompilerParams(dimension_semantics=("arbitrary",),
                                             vmem_limit_bytes=VMEM_LIMIT),
        name="combine",
    )(dest_blocks, dest_blocks, rt, y_2d.reshape(-1, TOK_TILES, LANE), y_2d, x1, g)


def _collect(y_tiles, dest_kmajor, n_tok, t_first):
    n_b = n_tok - t_first
    info = plsc.get_sparse_core_info()
    n_workers = info.num_cores * info.num_subcores
    per_worker = n_b // n_workers
    chunk = next(c for c in (24, 16, 8) if per_worker % c == 0)
    assert n_b % n_workers == 0 and per_worker % SUBLANE == 0 and t_first % SUBLANE == 0
    mesh = plsc.VectorSubcoreMesh(core_axis_name="c", subcore_axis_name="s")

    @functools.partial(
        pl.kernel, mesh=mesh,
        out_type=jax.ShapeDtypeStruct((TOP_K * n_b, TOK_TILES, LANE), F32),
        scratch_types=[pltpu.VMEM((TOP_K, chunk), jnp.int32), pltpu.VMEM((TOP_K, chunk, TOK_TILES, LANE), F32),
                       pltpu.SemaphoreType.DMA],
    )
    def collect(y_hbm, dest_hbm, out_hbm, idx_v, rows_v, sem):
        wid = lax.axis_index("s") * info.num_cores + lax.axis_index("c")

        def step(j, carry):
            r0 = pl.multiple_of(wid * per_worker + j * chunk, SUBLANE)

            def all_k(make):
                for cp in [make(k) for k in range(TOP_K)]:
                    cp.wait()

            all_k(lambda k: pltpu.async_copy(
                dest_hbm.at[pl.ds(pl.multiple_of(k * n_tok + t_first + r0, SUBLANE), chunk)], idx_v.at[k], sem))
            all_k(lambda k: pltpu.async_copy(y_hbm.at[idx_v.at[k]], rows_v.at[k], sem))
            all_k(lambda k: pltpu.async_copy(
                rows_v.at[k], out_hbm.at[pl.ds(pl.multiple_of(k * n_b + r0, SUBLANE), chunk)], sem))
            return carry

        lax.fori_loop(0, per_worker // chunk, step, 0)

    return collect(y_tiles, dest_kmajor)


def _combine_dense_body(rt_ref, *refs, first_block, n_p_blocks):
    y_refs, (x1_ref, g_ref, _, op_ref, os_ref) = refs[:TOP_K], refs[TOP_K:]
    blk = first_block + pl.program_id(0)
    moe = _load_token_tiles(y_refs[0], 0, COMBINE_ROWS) * rt_ref[:, 2 * TOP_K:2 * TOP_K + 1]
    for k in range(1, TOP_K):
        moe = moe + _load_token_tiles(y_refs[k], 0, COMBINE_ROWS) * rt_ref[:, 2 * TOP_K + k:2 * TOP_K + k + 1]
    res = _rms(x1_ref[...] + moe, g_ref[...])

    @pl.when(blk < n_p_blocks)
    def _():
        op_ref[...] = res

    @pl.when(blk >= n_p_blocks)
    def _():
        os_ref[...] = res


def _combine_dense(rt, y4_2d, x1, g, y_p_partial, n_p, n_a):
    n = x1.shape[0]
    n_blk = n // COMBINE_ROWS
    n_p_blocks = n_p // COMBINE_ROWS
    n_b = n_blk - n_a
    y_specs = [pl.BlockSpec((COMBINE_ROWS * TOK_TILES, LANE), lambda i, k=k: (k * n_b + i, 0))
               for k in range(TOP_K)]
    return pl.pallas_call(
        functools.partial(_combine_dense_body, first_block=n_a, n_p_blocks=n_p_blocks),
        grid=(n_b,),
        in_specs=[pl.BlockSpec((COMBINE_ROWS, LANE), lambda i: (n_a + i, 0))] + y_specs + [
            pl.BlockSpec((COMBINE_ROWS, D_MODEL), lambda i: (n_a + i, 0)),
            pl.BlockSpec((1, D_MODEL), lambda i: (0, 0)),
            pl.BlockSpec(memory_space=pl.ANY),
        ],
        out_specs=[pl.BlockSpec((COMBINE_ROWS, D_MODEL), lambda i: (jnp.minimum(n_a + i, n_p_blocks - 1), 0)),
                   pl.BlockSpec((COMBINE_ROWS, D_MODEL), lambda i: (jnp.maximum(n_a + i - n_p_blocks, 0), 0))],
        out_shape=[jax.ShapeDtypeStruct((n_p, D_MODEL), F32), jax.ShapeDtypeStruct((n - n_p, D_MODEL), F32)],
        input_output_aliases={TOP_K + 3: 0},
        compiler_params=pltpu.CompilerParams(dimension_semantics=("arbitrary",),
                                             vmem_limit_bytes=VMEM_LIMIT),
        name="combine_dense",
    )(rt, *([y4_2d] * TOP_K), x1, g, y_p_partial)


def _plan_body(rtt_ref, cnt_ref, dk_ref, db_ref, meta_ref, pst):
    i = pl.program_id(0)
    sh = _log2(EXPERT_ROWS)
    n_e = N_EXPERTS

    @pl.when(i == 0)
    def _():
        cnt = cnt_ref[...].astype(jnp.int32)
        padded = (((cnt + (EXPERT_ROWS - 1)) >> sh) << sh).astype(F32)
        e_r = lax.broadcasted_iota(jnp.int32, (n_e, n_e), 0)
        e_c = lax.broadcasted_iota(jnp.int32, (n_e, n_e), 1)
        p_t = jnp.broadcast_to(padded, (n_e, n_e)).T
        pend = jnp.sum(jnp.where(e_c <= e_r, p_t, 0.0), axis=1, keepdims=True)
        pst[...] = pend - padded
        has_rows = p_t > 0.0
        group = jnp.sum(jnp.where((e_c <= e_r) & has_rows, 1.0, 0.0), axis=1, keepdims=True) - 1.0
        nxt = jnp.min(jnp.where((e_c > e_r) & has_rows, e_c, n_e), axis=1, keepdims=True)
        nxt = jnp.where(nxt >= n_e, -1, nxt)

        mb = meta_ref.shape[1]
        blk = lax.broadcasted_iota(jnp.int32, (n_e, mb), 1)
        eb = lax.broadcasted_iota(jnp.int32, (n_e, mb), 0)
        first_row = (blk * EXPERT_ROWS).astype(F32)

        def expert_of(row0):
            return jnp.minimum(jnp.sum(jnp.where(pend <= row0, 1, 0), axis=0, keepdims=True), n_e - 1)

        be = expert_of(first_row)
        be_prev = expert_of(first_row - EXPERT_ROWS)
        hit = eb == be
        wslot = jnp.sum(jnp.where(hit, group, 0.0), axis=0, keepdims=True).astype(jnp.int32) & 1
        nx = jnp.sum(jnp.where(hit, nxt, 0), axis=0, keepdims=True)
        n_used = pend[n_e - 1:n_e, :].astype(jnp.int32) >> sh
        lane = lax.broadcasted_iota(jnp.int32, (1, mb), 1)
        first = (((be != be_prev) | (lane == 0)) & (lane < n_used)).astype(jnp.int32)
        cnt_b = jnp.sum(jnp.where(hit, cnt_ref[...], 0.0), axis=0, keepdims=True)
        pst_b = jnp.sum(jnp.where(hit, pend - padded, 0.0), axis=0, keepdims=True)
        valid = jnp.clip(cnt_b - (first_row[0:1, :] - pst_b), 0.0, float(EXPERT_ROWS)).astype(jnp.int32)
        row8 = lax.broadcasted_iota(jnp.int32, (SUBLANE, mb), 0)
        meta = jnp.where(row8 == 0, be, jnp.where(row8 == 1, first, jnp.where(row8 == 2, wslot,
                         jnp.where(row8 == 3, nx, jnp.where(row8 == 4, n_used, valid)))))
        meta_ref[...] = meta

    tm = rtt_ref.shape[1]
    eid = lax.broadcasted_iota(jnp.int32, (n_e, tm), 0).astype(F32)
    row8 = lax.broadcasted_iota(jnp.int32, (SUBLANE, tm), 0)
    d8 = jnp.zeros((SUBLANE, tm), jnp.int32)
    for k in range(TOP_K):
        start = jnp.sum(jnp.where(eid == rtt_ref[k:k + 1, :], pst[...], 0.0), axis=0, keepdims=True)
        d8 = jnp.where(row8 == k, (start + rtt_ref[TOP_K + k:TOP_K + k + 1, :]).astype(jnp.int32), d8)
    dk_ref[...] = d8[:TOP_K]
    for b in range(tm // COMBINE_ROWS):
        db_ref[b] = d8[:TOP_K, b * COMBINE_ROWS:(b + 1) * COMBINE_ROWS]


def _plan(rtt, cnt):
    n = rtt.shape[1]
    n_rows = n * TOP_K + N_EXPERTS * EXPERT_ROWS
    n_blocks = n_rows // EXPERT_ROWS
    mb = -(-n_blocks // LANE) * LANE
    tile = max(t for t in range(COMBINE_ROWS, PLAN_TILE_MAX + 1, COMBINE_ROWS) if n % t == 0)
    dk, db, meta = pl.pallas_call(
        _plan_body,
        grid=(n // tile,),
        in_specs=[pl.BlockSpec((2 * TOP_K, tile), lambda i: (0, i)),
                  pl.BlockSpec((N_EXPERTS, 1), lambda i: (0, 0))],
        out_specs=[pl.BlockSpec((TOP_K, tile), lambda i: (0, i)),
                   pl.BlockSpec((tile // COMBINE_ROWS, TOP_K, COMBINE_ROWS), lambda i: (i, 0, 0)),
                   pl.BlockSpec((SUBLANE, mb), lambda i: (0, 0))],
        out_shape=[jax.ShapeDtypeStruct((TOP_K, n), jnp.int32),
                   jax.ShapeDtypeStruct((n // COMBINE_ROWS, TOP_K, COMBINE_ROWS), jnp.int32),
                   jax.ShapeDtypeStruct((SUBLANE, mb), jnp.int32)],
        scratch_shapes=[pltpu.VMEM((N_EXPERTS, 1), F32)],
        compiler_params=pltpu.CompilerParams(dimension_semantics=("arbitrary",)),
        name="plan",
    )(rtt, cnt)
    block_meta = (meta[0, :n_blocks], meta[4, 0:1], meta[1, :n_blocks], meta[2, :n_blocks], meta[3, :n_blocks],
                  meta[5, :n_blocks])
    return dk, db, n_rows, block_meta


def _pad_lanes(v, width):
    return jnp.zeros((1, width), F32).at[0, :v.shape[0]].set(v.astype(F32))


def kernel(x_prompt, x_sample, state_gdn_conv, state_gdn, state_gla, rms_mix_w, w_in, conv_w, gdn_a_log,
           gdn_dt_bias, gdn_norm_w, gla_gk_w, gla_gk_b, gla_norm_w, w_out, rms_ffn_w, w_router, b_router,
           w_up, b_up, w_down, b_down, rms_final_w):
    bp, tp, d = x_prompt.shape
    bs, ts, _ = x_sample.shape
    n_p, n_s = bp * tp, bs * ts
    assert d == D_MODEL and state_gdn.shape[0] == 1, "single-layer kernel"
    assert tp >= CONV_WIDTH - 1 and ts >= CONV_WIDTH - 1, "new conv state is taken from the new tokens only"
    l = 0

    wi = w_in[l]
    a0 = GDN_CONV_CH + GDN_V_W
    g0 = a0 + 2 * GDN_HEADS
    lr0 = g0 + 2 * GLA_QK_W + 2 * GLA_V_W
    small = jnp.concatenate([wi[:, a0:a0 + 2 * GDN_HEADS], wi[:, lr0:lr0 + GLA_GATE_RANK],
                             jnp.zeros((d, SM_W - 2 * GDN_HEADS - GLA_GATE_RANK), F32)], axis=1)
    w_big = jnp.concatenate([wi[:, :a0], wi[:, g0:lr0], small], axis=1).astype(BF16)
    alog = _pad_lanes(gdn_a_log[l], SM_W)
    dtb = _pad_lanes(gdn_dt_bias[l], SM_W)
    wgk = jnp.zeros((SM_W, GLA_QK_W), F32).at[SM_LR:SM_LR + GLA_GATE_RANK].set(gla_gk_w[l])
    wr = jnp.zeros((d, LANE), F32).at[:, :N_EXPERTS].set(w_router[l])
    br = jnp.full((1, LANE), -1e30, F32).at[0, :N_EXPERTS].set(b_router[l])

    assert n_p % ROW_TILE == 0 and n_s % ROW_TILE == 0
    x_p, x_s = x_prompt.reshape(n_p, d), x_sample.reshape(n_s, d)
    proj = _inproj(x_p, x_s, rms_mix_w[l][None, :], w_big)

    tb_p = PROMPT_TIME_BLOCK
    zeros_conv = jnp.zeros((bp, CONV_WIDTH - 1, GDN_CONV_CH), F32)
    og_p, gdn_p, conv_p = _gdn(proj, bp, 1, tp, tb_p, CHUNK, CHUNK, zeros_conv,
                               jnp.zeros((bp, GDN_HEADS, GDN_DK, GDN_DV), F32), conv_w[l], alog, dtb,
                               gdn_norm_w[l][None, :])
    ol_p, gla_p = _gla(proj, bp, 1, tp, tb_p, CHUNK, CHUNK, jnp.zeros((bp, GLA_HEADS, GLA_DK, GLA_DV), F32),
                       wgk, gla_gk_b[l][None, :], gla_norm_w[l][None, :])

    ts_pad = SUBLANE
    nb_s = SAMPLE_SEQS_PER_STEP
    proj_s = proj[n_p:].reshape(bs, ts, PROJ_W)
    proj_sp = jnp.pad(proj_s, ((0, 0), (0, ts_pad - ts), (0, 0))).reshape(bs * ts_pad, PROJ_W)
    og_s, gdn_s, conv_s = _gdn(proj_sp, bs, nb_s, ts_pad, ts_pad, ts_pad, ts, state_gdn_conv[l], state_gdn[l],
                               conv_w[l], alog, dtb, gdn_norm_w[l][None, :])
    ol_s, gla_s = _gla(proj_sp, bs, nb_s, ts_pad, ts_pad, ts_pad, ts, state_gla[l], wgk, gla_gk_b[l][None, :],
                       gla_norm_w[l][None, :])
    og_s = og_s.reshape(bs, ts_pad, GDN_V_W)[:, :ts].reshape(n_s, GDN_V_W)
    ol_s = ol_s.reshape(bs, ts_pad, GLA_V_W)[:, :ts].reshape(n_s, GLA_V_W)

    x1, h2, rt, rtt, cnt = _outproj(og_p, og_s, ol_p, ol_s, x_p, x_s, w_out[l].astype(BF16),
                                    rms_ffn_w[l][None, :], wr, br)

    dest_k, dest_b, n_rows, block_meta = _plan(rtt, cnt)
    xs = _dispatch(h2.reshape(-1, PACK_TILES, LANE), dest_k.reshape(-1), n_rows)
    y_rows = _experts(block_meta, xs.reshape(-1, LANE), w_up[l], b_up[l], w_down[l], b_down[l])
    n_blk = (n_p + n_s) // COMBINE_ROWS
    n_a = max(a for a in range(1, n_p // COMBINE_ROWS + 1)
              if a <= COMBINE_GATHER_SHARE * n_blk and (n_blk - a) * COMBINE_ROWS % (32 * 24) == 0)
    y4 = _collect(y_rows.reshape(-1, TOK_TILES, LANE), dest_k.reshape(-1), n_p + n_s, n_a * COMBINE_ROWS)
    y_p_partial = _combine(dest_b, rt, y_rows, x1, rms_final_w[None, :], n_p, n_a)
    y_p, y_s = _combine_dense(rt, y4.reshape(-1, LANE), x1, rms_final_w[None, :], y_p_partial, n_p, n_a)
    y_prompt = y_p.reshape(bp, tp, d)
    y_sample = y_s.reshape(bs, ts, d)
    return (y_prompt, y_sample, conv_p[None], gdn_p[None], gla_p[None], conv_s[None], gdn_s[None], gla_s[None])
```

```python
import functools

import jax
import jax.numpy as jnp
from jax import lax
from jax.experimental import pallas as pl
from jax.experimental.pallas import tpu as pltpu
from jax.experimental.pallas import tpu_sc as plsc

F32 = jnp.float32
BF16 = jnp.bfloat16
HI = lax.Precision.HIGHEST

D_MODEL = 1024
GDN_HEADS = 4
GDN_DK = 128
GDN_DV = 128
GLA_HEADS = 4
GLA_DK = 64
GLA_DV = 128
GLA_GATE_RANK = 16
GLA_GATE_NORMALIZER = 16.0
CONV_WIDTH = 4
CHUNK = 64
N_EXPERTS = 32
TOP_K = 4
D_FF = 1024
SWIGLU_LIMIT = 7.0
SWIGLU_ALPHA = 1.702
RMS_EPS = 1e-6
L2_EPS = 1e-6

GDN_QK_W = GDN_HEADS * GDN_DK
GDN_V_W = GDN_HEADS * GDN_DV
GDN_CONV_CH = 2 * GDN_QK_W + GDN_V_W
GLA_QK_W = GLA_HEADS * GLA_DK
GLA_V_W = GLA_HEADS * GLA_DV

COL_QKV = 0
COL_Z = 1536
COL_GQ = 2048
COL_GK = 2304
COL_GV = 2560
COL_GG = 3072
COL_SM = 3584
SM_W = 128
PROJ_W = COL_SM + SM_W
SM_A, SM_B, SM_LR = 0, 4, 8

LANE = 128
SUBLANE = 8
TOK_TILES = D_MODEL // LANE
PACK_TILES = TOK_TILES // 2
ROW_TILE = 512
EXPERT_ROWS = 512
EXPERT_ROW_STEP = 128
EXPERT_WEIGHT_QUEUE = 1
WEIGHT_CAST_ROWS = 128
COMBINE_ROWS = 256
PLAN_TILE_MAX = 2048
DMA_ISSUE_UNROLL = 8
COMBINE_GATHER_QUEUES = (0, 1)
CONV_ROW_SLAB = 128
GDN_CHUNKS_PER_TRIP = 8
GLA_CHUNKS_PER_TRIP = 4
SCAN_CHUNKS_PER_TRIP = 8
PROMPT_TIME_BLOCK = 512
SAMPLE_SEQS_PER_STEP = 16
VMEM_LIMIT = 56 * 1024 * 1024


def _dot(a, b):
    return jnp.dot(a.astype(BF16), b.astype(BF16), preferred_element_type=F32)


def _dot_nt(a, b):
    return lax.dot_general(a.astype(BF16), b.astype(BF16), (((1,), (1,)), ((), ())),
                           preferred_element_type=F32)


def _dot_tn(a, b):
    return lax.dot_general(a.astype(BF16), b.astype(BF16), (((0,), (0,)), ((), ())),
                           preferred_element_type=F32)


def _dot_hi(a, b):
    return jnp.dot(a, b, precision=HI, preferred_element_type=F32)


def _dot_3pass(a, b):
    a_hi = a.astype(BF16)
    b_hi = b.astype(BF16)
    a_lo = (a - a_hi.astype(F32)).astype(BF16)
    b_lo = (b - b_hi.astype(F32)).astype(BF16)

    def mm(x, y):
        return jnp.dot(x, y, preferred_element_type=F32)

    return (mm(a_lo, b_hi) + mm(a_hi, b_lo)) + mm(a_hi, b_hi)


def _rms(x, w):
    return x * lax.rsqrt(jnp.mean(x * x, axis=-1, keepdims=True) + RMS_EPS) * w


def _silu(x):
    return x * jax.nn.sigmoid(x)


def _group_specs(rows, width, n_p_blocks):
    return [pl.BlockSpec((rows, width), lambda i: (jnp.minimum(i, n_p_blocks - 1), 0)),
            pl.BlockSpec((rows, width), lambda i: (jnp.maximum(i - n_p_blocks, 0), 0))]


def _group_pick(i, n_p_blocks, p_ref, s_ref):
    return jnp.where(i < n_p_blocks, p_ref[...], s_ref[...])


def _inproj_body(xp_ref, xs_ref, g_ref, w_ref, o_ref, *, n_p_blocks):
    x = _group_pick(pl.program_id(0), n_p_blocks, xp_ref, xs_ref)
    h = _rms(x, g_ref[...])
    o_ref[...] = jnp.dot(h.astype(BF16), w_ref[...], preferred_element_type=F32)


def _inproj(x_p, x_s, g, w):
    n_p_blocks, n_s_blocks = x_p.shape[0] // ROW_TILE, x_s.shape[0] // ROW_TILE
    n = x_p.shape[0] + x_s.shape[0]
    return pl.pallas_call(
        functools.partial(_inproj_body, n_p_blocks=n_p_blocks),
        grid=(n_p_blocks + n_s_blocks,),
        in_specs=_group_specs(ROW_TILE, D_MODEL, n_p_blocks) + [
            pl.BlockSpec((1, D_MODEL), lambda i: (0, 0)),
            pl.BlockSpec((D_MODEL, PROJ_W), lambda i: (0, 0)),
        ],
        out_specs=pl.BlockSpec((ROW_TILE, PROJ_W), lambda i: (i, 0)),
        out_shape=jax.ShapeDtypeStruct((n, PROJ_W), F32),
        compiler_params=pltpu.CompilerParams(dimension_semantics=("arbitrary",),
                                             vmem_limit_bytes=VMEM_LIMIT),
        name="in_proj",
    )(x_p, x_s, g, w)


def _log2(n):
    assert n & (n - 1) == 0
    return n.bit_length() - 1


def _tri_inv_all(ms, c, ii, jj):
    eye = (ii == jj).astype(F32)
    base = min(c, 8)
    sh = _log2(base)
    blk = (ii >> sh) == (jj >> sh)
    ns = [jnp.where(blk, m, 0.0) for m in ms]
    xs = [eye - n for n in ns]
    ps = [_dot(n, n) for n in ns]
    ts = [_dot(jnp.concatenate([x, p], axis=0), p) for x, p in zip(xs, ps)]
    xs = [x + t[:c] for x, t in zip(xs, ts)]
    ps = [t[c:] for t in ts]
    xs = [x + _dot(x, p) for x, p in zip(xs, ps)]
    s = base
    while s < c:
        sh_s, sh_b = _log2(s), _log2(2 * s)
        off = ((ii >> sh_b) == (jj >> sh_b)) & ((ii >> sh_s) != (jj >> sh_s))
        ys = [_dot(x, jnp.where(off, m, 0.0)) for x, m in zip(xs, ms)]
        xs = [x - _dot(y, x) for x, y in zip(xs, ys)]
        s *= 2
    return xs


def _gated_norm(o, w, z):
    return o * lax.rsqrt(jnp.mean(o * o, axis=-1, keepdims=True) + RMS_EPS) * w * _silu(z)


def _chunk_rows(s, tb_rows, ci, c):
    r = s * tb_rows + ci * c
    if not isinstance(r, int):
        r = pl.multiple_of(r, c)
    return r


def _for_chunks(n_chunks, step):
    if n_chunks == 1:
        step(0, 0)
    else:
        lax.fori_loop(0, n_chunks, step, 0)


def _gdn_body(qkv_ref, z_ref, sm_ref, cbuf_ref, s0_ref, cw_ref, alog_ref, dtb_ref, nw_ref,
              o_ref, sout_ref, cout_ref, st, xc, act, gcs, us, wss, qgs, kds, aqs,
              *, nb, tb_rows, chunk, valid, n_tb):
    tb = pl.program_id(1)
    c = chunk
    n_heads = GDN_HEADS
    tail = CONV_WIDTH - 1
    pad = SUBLANE
    units = [(s, h) for s in range(nb) for h in range(n_heads)]

    n_slabs = GDN_CONV_CH // LANE

    def lanes(j):
        return slice(j * LANE, (j + 1) * LANE)

    @pl.when(tb == 0)
    def _():
        st[...] = s0_ref[...]
        for s in range(nb):
            for j in range(n_slabs):
                xc[s, j, pad - tail:pad, :] = cbuf_ref[s, :, lanes(j)]

    if n_tb > 1:
        @pl.when(tb > 0)
        def _():
            for s in range(nb):
                for j in range(n_slabs):
                    xc[s, j, pad - tail:pad, :] = xc[s, j, tb_rows + pad - tail:tb_rows + pad, :]

    for s in range(nb):
        for j in range(n_slabs):
            xc[s, j, pad:pad + tb_rows, :] = qkv_ref[s * tb_rows:(s + 1) * tb_rows, lanes(j)]

    row_slab = min(tb_rows, CONV_ROW_SLAB)
    parities = 2 if row_slab >= 2 * SUBLANE else 1
    for s in range(nb):
        for j in range(n_slabs):
            src, dst = xc.at[s, j], act.at[j]
            for sl in range(tb_rows // row_slab):
                for p in range(parities):
                    lo = pad - tail + sl * row_slab + p
                    out0 = s * tb_rows + sl * row_slab + p

                    def rows_from(start):
                        if parities == 1:
                            return pl.ds(start, row_slab)
                        return pl.ds(start, row_slab // 2, stride=2)

                    acc = src[rows_from(lo), :] * cw_ref[0:1, lanes(j)]
                    for i in range(1, CONV_WIDTH):
                        acc = acc + src[rows_from(lo + i), :] * cw_ref[i:i + 1, lanes(j)]
                    dst[rows_from(out0), :] = _silu(acc)

    ii = lax.broadcasted_iota(jnp.int32, (c, c), 0)
    jj = lax.broadcasted_iota(jnp.int32, (c, c), 1)
    lower = (ii >= jj)
    lower_f = lower.astype(F32)
    strict = (ii > jj)
    rowmask = None
    if valid < c:
        rowmask = lax.broadcasted_iota(jnp.int32, (c, 1), 0) < valid

    def hs(h, w):
        return slice(h * w, (h + 1) * w)

    n_chunks = tb_rows // c
    cpi = next(k for k in (GDN_CHUNKS_PER_TRIP, 2, 1) if n_chunks % k == 0)
    p1_units = [(g, h) for g in range(nb * cpi) for h in range(n_heads)]

    def phase1(ci, carry):
        rows, b_ts, gc_ts, gc_tts = [], [], [], []
        for g in range(nb * cpi):
            rr = pl.ds(_chunk_rows(g // cpi, tb_rows, ci * cpi + g % cpi, c), c)
            sm = sm_ref[rr, :]
            g_t = -jnp.exp(alog_ref[...]) * jax.nn.softplus(sm + dtb_ref[...])
            b_t = jax.nn.sigmoid(sm)
            if rowmask is not None:
                g_t = jnp.where(rowmask, g_t, 0.0)
                b_t = jnp.where(rowmask, b_t, 0.0)
            gc_t = _dot_hi(lower_f, g_t)
            gcs[rr, :] = gc_t
            rows.append(rr)
            b_ts.append(b_t)
            gc_ts.append(gc_t)
            gc_tts.append(gc_t.T)
        qn, kn, kb, vb = {}, {}, {}, {}
        for (s, h) in p1_units:
            q = act[h, rows[s], :]
            k = act[n_heads + h, rows[s], :]
            v = act[2 * n_heads + h, rows[s], :]
            if rowmask is not None:
                q = jnp.where(rowmask, q, 0.0)
                k = jnp.where(rowmask, k, 0.0)
                v = jnp.where(rowmask, v, 0.0)
            qn[s, h] = q * lax.rsqrt(jnp.sum(q * q, axis=-1, keepdims=True) + L2_EPS) * (GDN_DK ** -0.5)
            kn[s, h] = k * lax.rsqrt(jnp.sum(k * k, axis=-1, keepdims=True) + L2_EPS)
            beta = b_ts[s][:, SM_B + h:SM_B + h + 1]
            kb[s, h] = kn[s, h] * beta
            vb[s, h] = v * beta
        s1 = {u: _dot_nt(jnp.concatenate([kb[u], qn[u]], axis=0), kn[u]) for u in p1_units}
        mm = []
        for (s, h) in p1_units:
            gcol = gc_ts[s][:, SM_A + h:SM_A + h + 1]
            grow = gc_tts[s][SM_A + h:SM_A + h + 1, :]
            dec = jnp.exp(jnp.where(lower, gcol - grow, -jnp.inf))
            mm.append(jnp.where(strict, s1[s, h][:c] * dec, 0.0))
            aqs[h, rows[s], :] = s1[s, h][c:] * dec
        tms = _tri_inv_all(mm, c, ii, jj)
        for (s, h), tm in zip(p1_units, tms):
            gcol = gc_ts[s][:, SM_A + h:SM_A + h + 1]
            eg = jnp.exp(gcol)
            uw = _dot(tm, jnp.concatenate([vb[s, h], kb[s, h] * eg], axis=1))
            us[rows[s], hs(h, GDN_DV)] = uw[:, :GDN_DV]
            wss[rows[s], hs(h, GDN_DV)] = uw[:, GDN_DV:]
            qgs[rows[s], hs(h, GDN_DK)] = qn[s, h] * eg
            kds[rows[s], hs(h, GDN_DK)] = kn[s, h] * jnp.exp(gcol[c - 1:c, :] - gcol)
        return carry

    cp2 = next(k for k in (SCAN_CHUNKS_PER_TRIP, 1) if n_chunks % k == 0)

    def phase2(ti, carry):
        for u in range(cp2):
            scan_chunk(ti * cp2 + u)
        return carry

    def scan_chunk(ci):
        r0 = [_chunk_rows(s, tb_rows, ci, c) for s in range(nb)]
        rows = [pl.ds(r, c) for r in r0]
        ws = {(s, h): _dot(jnp.concatenate([wss[rows[s], hs(h, GDN_DV)], qgs[rows[s], hs(h, GDN_DK)]], axis=0),
                           st[s, h]) for (s, h) in units}
        v_new = {(s, h): us[rows[s], hs(h, GDN_DV)] - ws[s, h][:c] for (s, h) in units}
        o = {(s, h): ws[s, h][c:] + _dot(aqs[h, rows[s], :], v_new[s, h]) for (s, h) in units}
        upd = {(s, h): _dot_tn(kds[rows[s], hs(h, GDN_DK)], v_new[s, h]) for (s, h) in units}
        for (s, h) in units:
            g_last = gcs[pl.ds(r0[s] + c - 1, 1), SM_A + h:SM_A + h + 1]
            st[s, h] = st[s, h] * jnp.exp(g_last) + upd[s, h]
        for s in range(nb):
            o_ref[rows[s], :] = jnp.concatenate(
                [_gated_norm(o[s, h], nw_ref[...], z_ref[rows[s], hs(h, GDN_DV)]) for h in range(n_heads)], axis=1)

    _for_chunks(n_chunks // cpi, phase1)
    _for_chunks(n_chunks // cp2, phase2)

    @pl.when(tb == n_tb - 1)
    def _():
        sout_ref[...] = st[...]
        last = tb_rows if valid == c else valid
        for s in range(nb):
            cout_ref[s] = jnp.concatenate(
                [xc[s, j, pad + last - tail:pad + last, :] for j in range(n_slabs)], axis=1)


def _gdn(proj, n_seq, nb, t_len, tb_rows, chunk, valid, conv_buf, s0, conv_w, alog, dtb, nw):
    n_tb = t_len // tb_rows
    assert nb == 1 or n_tb == 1
    rows = nb * tb_rows

    def rowblk(b, t):
        return b * n_tb + t

    body = functools.partial(_gdn_body, nb=nb, tb_rows=tb_rows, chunk=chunk, valid=valid, n_tb=n_tb)
    return pl.pallas_call(
        body,
        grid=(n_seq // nb, n_tb),
        in_specs=[
            pl.BlockSpec((rows, GDN_CONV_CH), lambda b, t: (rowblk(b, t), COL_QKV // GDN_CONV_CH)),
            pl.BlockSpec((rows, GDN_V_W), lambda b, t: (rowblk(b, t), COL_Z // GDN_V_W)),
            pl.BlockSpec((rows, SM_W), lambda b, t: (rowblk(b, t), COL_SM // SM_W)),
            pl.BlockSpec((nb, CONV_WIDTH - 1, GDN_CONV_CH), lambda b, t: (b, 0, 0)),
            pl.BlockSpec((nb, GDN_HEADS, GDN_DK, GDN_DV), lambda b, t: (b, 0, 0, 0)),
            pl.BlockSpec((CONV_WIDTH, GDN_CONV_CH), lambda b, t: (0, 0)),
            pl.BlockSpec((1, SM_W), lambda b, t: (0, 0)),
            pl.BlockSpec((1, SM_W), lambda b, t: (0, 0)),
            pl.BlockSpec((1, GDN_DV), lambda b, t: (0, 0)),
        ],
        out_specs=[
            pl.BlockSpec((rows, GDN_V_W), lambda b, t: (rowblk(b, t), 0)),
            pl.BlockSpec((nb, GDN_HEADS, GDN_DK, GDN_DV), lambda b, t: (b, 0, 0, 0)),
            pl.BlockSpec((nb, CONV_WIDTH - 1, GDN_CONV_CH), lambda b, t: (b, 0, 0)),
        ],
        out_shape=[
            jax.ShapeDtypeStruct((n_seq * t_len, GDN_V_W), F32),
            jax.ShapeDtypeStruct((n_seq, GDN_HEADS, GDN_DK, GDN_DV), F32),
            jax.ShapeDtypeStruct((n_seq, CONV_WIDTH - 1, GDN_CONV_CH), F32),
        ],
        scratch_shapes=[
            pltpu.VMEM((nb, GDN_HEADS, GDN_DK, GDN_DV), F32),
            pltpu.VMEM((nb, GDN_CONV_CH // LANE, tb_rows + SUBLANE, LANE), F32),
            pltpu.VMEM((GDN_CONV_CH // LANE, rows, LANE), F32),
            pltpu.VMEM((rows, SM_W), F32),
            pltpu.VMEM((rows, GDN_V_W), F32),
            pltpu.VMEM((rows, GDN_V_W), F32),
            pltpu.VMEM((rows, GDN_QK_W), F32),
            pltpu.VMEM((rows, GDN_QK_W), F32),
            pltpu.VMEM((GDN_HEADS, rows, chunk), F32),
        ],
        compiler_params=pltpu.CompilerParams(dimension_semantics=("arbitrary", "arbitrary"),
                                             vmem_limit_bytes=VMEM_LIMIT),
        name="gdn_mixer",
    )(proj, proj, proj, conv_buf, s0, conv_w, alog, dtb, nw)


def _gla_body(q_ref, k_ref, v_ref, go_ref, sm_ref, s0_ref, wgk_ref, bgk_ref, nw_ref,
              o_ref, sout_ref, st, qes, ois, upds, decs, *, nb, tb_rows, chunk, valid, n_tb):
    tb = pl.program_id(1)
    c = chunk
    n_heads = GLA_HEADS
    units = [(s, h) for s in range(nb) for h in range(n_heads)]

    @pl.when(tb == 0)
    def _():
        st[...] = s0_ref[...]

    ii = lax.broadcasted_iota(jnp.int32, (c, c), 0)
    jj = lax.broadcasted_iota(jnp.int32, (c, c), 1)
    lower = (ii >= jj)
    lower_f = lower.astype(F32)
    rid = lax.broadcasted_iota(jnp.int32, (c, 1), 0)
    rowmask = (rid < valid) if valid < c else None
    n_sub = max(c // 16, 1)
    sub = c // n_sub

    n_chunks = tb_rows // c
    cpi = next(k for k in (GLA_CHUNKS_PER_TRIP, 2, 1) if n_chunks % k == 0)
    p1_units = [(g, h) for g in range(nb * cpi) for h in range(n_heads)]

    def phase1(ci, carry):
        rows, slots, bcs, bc_ts = [], [], [], []
        for g in range(nb * cpi):
            chunk_idx = ci * cpi + g % cpi
            rr = pl.ds(_chunk_rows(g // cpi, tb_rows, chunk_idx, c), c)
            slots.append((g // cpi) * n_chunks + chunk_idx)
            gk = jax.nn.log_sigmoid(_dot(sm_ref[rr, :], wgk_ref[...]) + bgk_ref[...]) / GLA_GATE_NORMALIZER
            if rowmask is not None:
                gk = jnp.where(rowmask, gk, 0.0)
            bc = _dot_hi(lower_f, gk)
            rows.append(rr)
            bcs.append(bc)
            bc_ts.append(bc.T)
        q, k, v, bch = {}, {}, {}, {}
        for (s, h) in p1_units:
            ks = slice(h * GLA_DK, (h + 1) * GLA_DK)
            vs = slice(h * GLA_DV, (h + 1) * GLA_DV)
            q[s, h] = q_ref[rows[s], ks] * (GLA_DK ** -0.5)
            kk = k_ref[rows[s], ks]
            vv = v_ref[rows[s], vs]
            if rowmask is not None:
                kk = jnp.where(rowmask, kk, 0.0)
                vv = jnp.where(rowmask, vv, 0.0)
            k[s, h], v[s, h] = kk, vv
            bch[s, h] = bcs[s][:, ks]
        for (g, h) in p1_units:
            qes[h, rows[g], :] = q[g, h] * jnp.exp(bch[g, h])
        a = {}
        for u in p1_units:
            q_parts, k_parts = [], []
            for sb in range(n_sub):
                ref_row = bch[u][sb * sub:sb * sub + 1, :]
                in_blk = (rid >= sb * sub) & (rid < (sb + 1) * sub)
                q_parts.append(jnp.where(in_blk, q[u] * jnp.exp(jnp.where(in_blk, bch[u] - ref_row, 0.0)), 0.0))
                k_parts.append(k[u] * jnp.exp(jnp.where(rid < (sb + 1) * sub, ref_row - bch[u], 0.0)))
            q_hat = jnp.concatenate(q_parts, axis=1) if n_sub > 1 else q_parts[0]
            k_hat = jnp.concatenate(k_parts, axis=1) if n_sub > 1 else k_parts[0]
            a[u] = jnp.where(lower, _dot_nt(q_hat, k_hat), 0.0)
        upd = {u: _dot_tn(k[u] * jnp.exp(bch[u][c - 1:c, :] - bch[u]), v[u]) for u in p1_units}
        o_intra = {u: _dot(a[u], v[u]) for u in p1_units}
        for (g, h) in p1_units:
            dec_col = bc_ts[g][h * GLA_DK:(h + 1) * GLA_DK, c - 1:c]
            decs[slots[g], h] = jnp.broadcast_to(jnp.exp(dec_col), (GLA_DK, GLA_DV))
            upds[slots[g], h] = upd[g, h]
            ois[rows[g], h * GLA_DV:(h + 1) * GLA_DV] = o_intra[g, h]
        return carry

    cp2 = next(k for k in (SCAN_CHUNKS_PER_TRIP, 1) if n_chunks % k == 0)

    def phase2(ti, carry):
        for u in range(cp2):
            scan_chunk(ti * cp2 + u)
        return carry

    def scan_chunk(ci):
        rows = [pl.ds(_chunk_rows(s, tb_rows, ci, c), c) for s in range(nb)]
        o = {(s, h): ois[rows[s], h * GLA_DV:(h + 1) * GLA_DV] + _dot(qes[h, rows[s], :], st[s, h])
             for (s, h) in units}
        for (s, h) in units:
            st[s, h] = decs[s * n_chunks + ci, h] * st[s, h] + upds[s * n_chunks + ci, h]
        for s in range(nb):
            o_ref[rows[s], :] = jnp.concatenate(
                [_gated_norm(o[s, h], nw_ref[...], go_ref[rows[s], h * GLA_DV:(h + 1) * GLA_DV])
                 for h in range(n_heads)], axis=1)

    _for_chunks(n_chunks // cpi, phase1)
    _for_chunks(n_chunks // cp2, phase2)

    @pl.when(tb == n_tb - 1)
    def _():
        sout_ref[...] = st[...]


def _gla(proj, n_seq, nb, t_len, tb_rows, chunk, valid, s0, wgk, bgk, nw):
    n_tb = t_len // tb_rows
    assert nb == 1 or n_tb == 1
    rows = nb * tb_rows

    def rowblk(b, t):
        return b * n_tb + t

    body = functools.partial(_gla_body, nb=nb, tb_rows=tb_rows, chunk=chunk, valid=valid, n_tb=n_tb)
    return pl.pallas_call(
        body,
        grid=(n_seq // nb, n_tb),
        in_specs=[
            pl.BlockSpec((rows, GLA_QK_W), lambda b, t: (rowblk(b, t), COL_GQ // GLA_QK_W)),
            pl.BlockSpec((rows, GLA_QK_W), lambda b, t: (rowblk(b, t), COL_GK // GLA_QK_W)),
            pl.BlockSpec((rows, GLA_V_W), lambda b, t: (rowblk(b, t), COL_GV // GLA_V_W)),
            pl.BlockSpec((rows, GLA_V_W), lambda b, t: (rowblk(b, t), COL_GG // GLA_V_W)),
            pl.BlockSpec((rows, SM_W), lambda b, t: (rowblk(b, t), COL_SM // SM_W)),
            pl.BlockSpec((nb, GLA_HEADS, GLA_DK, GLA_DV), lambda b, t: (b, 0, 0, 0)),
            pl.BlockSpec((SM_W, GLA_QK_W), lambda b, t: (0, 0)),
            pl.BlockSpec((1, GLA_QK_W), lambda b, t: (0, 0)),
            pl.BlockSpec((1, GLA_DV), lambda b, t: (0, 0)),
        ],
        out_specs=[
            pl.BlockSpec((rows, GLA_V_W), lambda b, t: (rowblk(b, t), 0)),
            pl.BlockSpec((nb, GLA_HEADS, GLA_DK, GLA_DV), lambda b, t: (b, 0, 0, 0)),
        ],
        out_shape=[
            jax.ShapeDtypeStruct((n_seq * t_len, GLA_V_W), F32),
            jax.ShapeDtypeStruct((n_seq, GLA_HEADS, GLA_DK, GLA_DV), F32),
        ],
        scratch_shapes=[
            pltpu.VMEM((nb, GLA_HEADS, GLA_DK, GLA_DV), F32),
            pltpu.VMEM((GLA_HEADS, rows, GLA_DK), F32),
            pltpu.VMEM((rows, GLA_V_W), F32),
            pltpu.VMEM((rows // chunk, GLA_HEADS, GLA_DK, GLA_DV), F32),
            pltpu.VMEM((rows // chunk, GLA_HEADS, GLA_DK, GLA_DV), F32),
        ],
        compiler_params=pltpu.CompilerParams(dimension_semantics=("arbitrary", "arbitrary"),
                                             vmem_limit_bytes=VMEM_LIMIT),
        name="gla_mixer",
    )(proj, proj, proj, proj, proj, s0, wgk, bgk, nw)


def _outproj_body(ogp_ref, ogs_ref, olp_ref, ols_ref, xp_ref, xs_ref, wo_ref, g_ref, wr_ref, br_ref,
                  x1_ref, h2_ref, rt_ref, rtt_ref, cnt_ref, base, *, n_p_blocks):
    i = pl.program_id(0)

    @pl.when(i == 0)
    def _():
        base[...] = jnp.zeros_like(base)

    o = jnp.concatenate([_group_pick(i, n_p_blocks, ogp_ref, ogs_ref),
                         _group_pick(i, n_p_blocks, olp_ref, ols_ref)], axis=1)
    x1 = _group_pick(i, n_p_blocks, xp_ref, xs_ref) + jnp.dot(o.astype(BF16), wo_ref[...],
                                                               preferred_element_type=F32)
    x1_ref[...] = x1
    h = _rms(x1, g_ref[...])
    _store_token_tiles(h2_ref, _pack_bf16_pairs(h))
    logits = _dot_3pass(h, wr_ref[...]) + br_ref[...]

    tm = logits.shape[0]
    lt = logits.T[:N_EXPERTS]
    eid = lax.broadcasted_iota(jnp.int32, (N_EXPERTS, tm), 0)
    work = lt
    sel = jnp.zeros((N_EXPERTS, tm), F32)
    hits, ids, vals = [], [], []
    for _ in range(TOP_K):
        m = jnp.max(work, axis=0, keepdims=True)
        idx = jnp.min(jnp.where(work == m, eid, N_EXPERTS), axis=0, keepdims=True)
        hit = eid == idx
        hits.append(hit)
        ids.append(idx)
        vals.append(m)
        work = jnp.where(hit, -jnp.inf, work)
        sel = sel + hit.astype(F32)
    exps = [jnp.exp(v - vals[0]) for v in vals]
    den = exps[0]
    for e in exps[1:]:
        den = den + e
    gates = [e / den for e in exps]

    ri = lax.broadcasted_iota(jnp.int32, (tm, tm), 0)
    ci = lax.broadcasted_iota(jnp.int32, (tm, tm), 1)
    before = _dot(sel, (ri < ci).astype(F32)) + base[...]
    ranks = [jnp.sum(jnp.where(hit, before, 0.0), axis=0, keepdims=True) for hit in hits]
    base[...] = base[...] + jnp.sum(sel, axis=1, keepdims=True)
    cnt_ref[...] = base[...]

    row = lax.broadcasted_iota(jnp.int32, (LANE, tm), 0)
    rec = jnp.zeros((LANE, tm), F32)
    for k in range(TOP_K):
        rec = jnp.where(row == k, ids[k].astype(F32), rec)
        rec = jnp.where(row == TOP_K + k, ranks[k], rec)
        rec = jnp.where(row == 2 * TOP_K + k, gates[k], rec)
    rt_ref[...] = rec.T
    rtt_ref[...] = rec[:2 * TOP_K]


def _outproj(og_p, og_s, ol_p, ol_s, x_p, x_s, wo, g, wr, br):
    n_p_blocks, n_s_blocks = x_p.shape[0] // ROW_TILE, x_s.shape[0] // ROW_TILE
    n = x_p.shape[0] + x_s.shape[0]
    return pl.pallas_call(
        functools.partial(_outproj_body, n_p_blocks=n_p_blocks),
        grid=(n_p_blocks + n_s_blocks,),
        in_specs=_group_specs(ROW_TILE, GDN_V_W, n_p_blocks) + _group_specs(ROW_TILE, GLA_V_W, n_p_blocks)
        + _group_specs(ROW_TILE, D_MODEL, n_p_blocks) + [
            pl.BlockSpec((D_MODEL, D_MODEL), lambda i: (0, 0)),
            pl.BlockSpec((1, D_MODEL), lambda i: (0, 0)),
            pl.BlockSpec((D_MODEL, LANE), lambda i: (0, 0)),
            pl.BlockSpec((1, LANE), lambda i: (0, 0)),
        ],
        out_specs=[
            pl.BlockSpec((ROW_TILE, D_MODEL), lambda i: (i, 0)),
            pl.BlockSpec((ROW_TILE * PACK_TILES, LANE), lambda i: (i, 0)),
            pl.BlockSpec((ROW_TILE, LANE), lambda i: (i, 0)),
            pl.BlockSpec((2 * TOP_K, ROW_TILE), lambda i: (0, i)),
            pl.BlockSpec((N_EXPERTS, 1), lambda i: (0, 0)),
        ],
        out_shape=[
            jax.ShapeDtypeStruct((n, D_MODEL), F32),
            jax.ShapeDtypeStruct((n * PACK_TILES, LANE), jnp.uint32),
            jax.ShapeDtypeStruct((n, LANE), F32),
            jax.ShapeDtypeStruct((2 * TOP_K, n), F32),
            jax.ShapeDtypeStruct((N_EXPERTS, 1), F32),
        ],
        scratch_shapes=[pltpu.VMEM((N_EXPERTS, 1), F32)],
        compiler_params=pltpu.CompilerParams(dimension_semantics=("arbitrary",),
                                             vmem_limit_bytes=VMEM_LIMIT),
        name="out_proj",
    )(og_p, og_s, ol_p, ol_s, x_p, x_s, wo, g, wr, br)


def _store_token_tiles(ref2d, val):
    rows, tiles = val.shape[0], val.shape[1] // LANE
    for c in range(tiles):
        ref2d[pl.ds(c, rows, stride=tiles), :] = val[:, c * LANE:(c + 1) * LANE]


def _load_token_tiles(ref2d, first_row, rows, tiles=TOK_TILES):
    return jnp.concatenate(
        [ref2d[pl.ds(first_row * tiles + c, rows, stride=tiles), :] for c in range(tiles)], axis=1)


def _pack_bf16_pairs(x):
    half = x.shape[1] // 2
    bits = lax.bitcast_convert_type(x.astype(BF16).astype(F32), jnp.uint32)
    return (bits[:, :half] >> 16) | (bits[:, half:] & jnp.uint32(0xFFFF0000))


def _unpack_bf16_pairs(w):
    lo = lax.bitcast_convert_type(w << 16, F32)
    hi = lax.bitcast_convert_type(w & jnp.uint32(0xFFFF0000), F32)
    return jnp.concatenate([lo, hi], axis=1).astype(BF16)


def _expert_weight_copies(e, ws, wup_hbm, wdn_hbm, wup_buf, wdn_buf, wsems):
    return (pltpu.make_async_copy(wup_hbm.at[e], wup_buf.at[ws], wsems.at[ws]),
            pltpu.make_async_copy(wdn_hbm.at[e], wdn_buf.at[ws], wsems.at[ws]))


def _expert_body(be_ref, nu_ref, first_ref, wslot_ref, next_ref, valid_ref, x_ref,
                 wup_hbm, bup_ref, wdn_hbm, bdn_ref, y_ref, wup_buf, wdn_buf, wsems, wup_bf, wdn_bf):
    i = pl.program_id(0)
    n_used = nu_ref[0]
    ws = wslot_ref[i]
    weight_copies = functools.partial(_expert_weight_copies, wup_hbm=wup_hbm, wdn_hbm=wdn_hbm, wup_buf=wup_buf,
                                      wdn_buf=wdn_buf, wsems=wsems)

    @pl.when((i == 0) & (n_used > 0))
    def _():
        for cp in weight_copies(be_ref[0], ws):
            cp.start(priority=EXPERT_WEIGHT_QUEUE)

    @pl.when(i < n_used)
    def _():
        @pl.when(first_ref[i] == 1)
        def _():
            for cp in weight_copies(be_ref[i], ws):
                cp.wait()

            @pl.when(next_ref[i] >= 0)
            def _():
                for cp in weight_copies(next_ref[i], 1 - ws):
                    cp.start(priority=EXPERT_WEIGHT_QUEUE)

            for r in range(0, D_MODEL, WEIGHT_CAST_ROWS):
                wup_bf[r:r + WEIGHT_CAST_ROWS, :] = wup_buf[ws, r:r + WEIGHT_CAST_ROWS, :].astype(BF16)
            for r in range(0, D_FF, WEIGHT_CAST_ROWS):
                wdn_bf[r:r + WEIGHT_CAST_ROWS, :] = wdn_buf[ws, r:r + WEIGHT_CAST_ROWS, :].astype(BF16)

        def expert_rows(n):
            x = _unpack_bf16_pairs(_load_token_tiles(x_ref, 0, n, PACK_TILES))
            gu = _dot(x, wup_bf[...]) + bup_ref[...]
            gate = jnp.minimum(gu[:, :D_FF], SWIGLU_LIMIT)
            up = jnp.clip(gu[:, D_FF:], -SWIGLU_LIMIT, SWIGLU_LIMIT)
            a = (up + 1.0) * gate * jax.nn.sigmoid(SWIGLU_ALPHA * gate)
            _store_token_tiles(y_ref, _dot(a, wdn_bf[...]) + bdn_ref[...])

        valid = valid_ref[i]
        for n in range(EXPERT_ROW_STEP, EXPERT_ROWS + 1, EXPERT_ROW_STEP):
            @pl.when((valid <= n) if n == EXPERT_ROW_STEP else ((valid > n - EXPERT_ROW_STEP) & (valid <= n)))
            def _(n=n):
                expert_rows(n)
                if n < EXPERT_ROWS:
                    y_ref[n * TOK_TILES:, :] = jnp.zeros(((EXPERT_ROWS - n) * TOK_TILES, LANE), F32)

    @pl.when(i >= n_used)
    def _():
        y_ref[...] = jnp.zeros_like(y_ref)


def _experts(block_meta, xs_2d, w_up, b_up, w_down, b_down):
    n_blocks = block_meta[0].shape[0]
    grid_spec = pltpu.PrefetchScalarGridSpec(
        num_scalar_prefetch=len(block_meta),
        grid=(n_blocks,),
        in_specs=[
            pl.BlockSpec((EXPERT_ROWS * PACK_TILES, LANE), lambda i, *_: (i, 0)),
            pl.BlockSpec(memory_space=pl.ANY),
            pl.BlockSpec((None, 1, 2 * D_FF), lambda i, be, *_: (be[i], 0, 0)),
            pl.BlockSpec(memory_space=pl.ANY),
            pl.BlockSpec((None, 1, D_MODEL), lambda i, be, *_: (be[i], 0, 0)),
        ],
        out_specs=pl.BlockSpec((EXPERT_ROWS * TOK_TILES, LANE), lambda i, *_: (i, 0)),
        scratch_shapes=[
            pltpu.VMEM((2, D_MODEL, 2 * D_FF), F32),
            pltpu.VMEM((2, D_FF, D_MODEL), F32),
            pltpu.SemaphoreType.DMA((2,)),
            pltpu.VMEM((D_MODEL, 2 * D_FF), BF16),
            pltpu.VMEM((D_FF, D_MODEL), BF16),
        ],
    )
    return pl.pallas_call(
        _expert_body,
        grid_spec=grid_spec,
        out_shape=jax.ShapeDtypeStruct((n_blocks * EXPERT_ROWS * TOK_TILES, LANE), F32),
        compiler_params=pltpu.CompilerParams(dimension_semantics=("arbitrary",),
                                             vmem_limit_bytes=VMEM_LIMIT),
        name="experts",
    )(*block_meta, xs_2d, w_up, b_up.reshape(N_EXPERTS, 1, 2 * D_FF), w_down,
      b_down.reshape(N_EXPERTS, 1, D_MODEL))


def _dispatch(h_tiles, dest_kmajor, n_rows):
    n_tok = h_tiles.shape[0]
    info = plsc.get_sparse_core_info()
    n_workers = info.num_cores * info.num_subcores
    per_worker = n_tok // n_workers
    chunk = next(c for c in (128, 96, 88, 64, 48, 32, 16, 8) if per_worker % c == 0)
    assert n_tok % n_workers == 0 and per_worker % SUBLANE == 0
    mesh = plsc.VectorSubcoreMesh(core_axis_name="c", subcore_axis_name="s")

    @functools.partial(
        pl.kernel, mesh=mesh,
        out_type=jax.ShapeDtypeStruct((n_rows,) + h_tiles.shape[1:], h_tiles.dtype),
        scratch_types=[pltpu.VMEM((TOP_K, chunk), jnp.int32), pltpu.VMEM((chunk,) + h_tiles.shape[1:], h_tiles.dtype),
                       pltpu.SemaphoreType.DMA],
    )
    def dispatch(h_hbm, dest_hbm, out_hbm, idx_v, rows_v, sem):
        wid = lax.axis_index("s") * info.num_cores + lax.axis_index("c")

        def step(j, carry):
            t0 = pl.multiple_of(wid * per_worker + j * chunk, SUBLANE)
            loads = [pltpu.async_copy(h_hbm.at[pl.ds(t0, chunk)], rows_v, sem)]
            for k in range(TOP_K):
                loads.append(pltpu.async_copy(
                    dest_hbm.at[pl.ds(pl.multiple_of(k * n_tok + t0, SUBLANE), chunk)], idx_v.at[k], sem))
            for cp in loads:
                cp.wait()
            stores = [pltpu.async_copy(rows_v, out_hbm.at[idx_v.at[k]], sem) for k in range(TOP_K)]
            for cp in stores:
                cp.wait()
            return carry

        lax.fori_loop(0, per_worker // chunk, step, 0)

    return dispatch(h_tiles, dest_kmajor)


def _gather_rows(src_tiles, idx_ref, n_rows, dst2d, sem, priorities):
    def issue(j, carry):
        for u in range(DMA_ISSUE_UNROLL):
            r = j * DMA_ISSUE_UNROLL + u
            dst = dst2d.at[pl.ds(pl.multiple_of(r * TOK_TILES, TOK_TILES), TOK_TILES), :]
            pltpu.make_async_copy(src_tiles.at[idx_ref[0, r]], dst, sem).start(
                priority=priorities[u % len(priorities)])
        return carry

    lax.fori_loop(0, n_rows // DMA_ISSUE_UNROLL, issue, 0)


def _wait_rows(src2d, n_rows, dst2d, sem):
    pltpu.make_async_copy(src2d.at[pl.ds(0, n_rows * TOK_TILES), :], dst2d, sem).wait()


def _combine_body(dest_ref, dest_next_ref, rt_ref, y_tiles, y_2d, x1_ref, g_ref, op_ref, os_ref, ybuf, sems,
                  *, n_p_blocks):
    i = pl.program_id(0)
    slot = i % 2
    n_rows = TOP_K * COMBINE_ROWS

    @pl.when(i == 0)
    def _():
        _gather_rows(y_tiles, dest_ref, n_rows, ybuf.at[0], sems.at[0], COMBINE_GATHER_QUEUES)

    _wait_rows(y_2d, n_rows, ybuf.at[slot], sems.at[slot])

    @pl.when(i + 1 < pl.num_programs(0))
    def _():
        _gather_rows(y_tiles, dest_next_ref, n_rows, ybuf.at[1 - slot], sems.at[1 - slot], COMBINE_GATHER_QUEUES)

    buf = ybuf.at[slot]
    moe = _load_token_tiles(buf, 0, COMBINE_ROWS) * rt_ref[:, 2 * TOP_K:2 * TOP_K + 1]
    for k in range(1, TOP_K):
        moe = moe + _load_token_tiles(buf, k * COMBINE_ROWS, COMBINE_ROWS) * rt_ref[:, 2 * TOP_K + k:2 * TOP_K + k + 1]
    res = _rms(x1_ref[...] + moe, g_ref[...])

    @pl.when(i < n_p_blocks)
    def _():
        op_ref[...] = res

    @pl.when(i >= n_p_blocks)
    def _():
        os_ref[...] = res


def _combine(dest_b, rt, y_2d, x1, g, n_p):
    n = x1.shape[0]
    n_blk = n // COMBINE_ROWS
    n_p_blocks = n_p // COMBINE_ROWS
    dest_blocks = dest_b.reshape(n_blk, 1, TOP_K * COMBINE_ROWS)
    return pl.pallas_call(
        functools.partial(_combine_body, n_p_blocks=n_p_blocks),
        grid=(n_blk,),
        in_specs=[
            pl.BlockSpec((None, 1, COMBINE_ROWS * TOP_K), lambda i: (i, 0, 0), memory_space=pltpu.SMEM),
            pl.BlockSpec((None, 1, COMBINE_ROWS * TOP_K), lambda i: (jnp.minimum(i + 1, n_blk - 1), 0, 0),
                         memory_space=pltpu.SMEM),
            pl.BlockSpec((COMBINE_ROWS, LANE), lambda i: (i, 0)),
            pl.BlockSpec(memory_space=pl.ANY),
            pl.BlockSpec(memory_space=pl.ANY),
            pl.BlockSpec((COMBINE_ROWS, D_MODEL), lambda i: (i, 0)),
            pl.BlockSpec((1, D_MODEL), lambda i: (0, 0)),
        ],
        out_specs=_group_specs(COMBINE_ROWS, D_MODEL, n_p_blocks),
        out_shape=[jax.ShapeDtypeStruct((n_p, D_MODEL), F32), jax.ShapeDtypeStruct((n - n_p, D_MODEL), F32)],
        scratch_shapes=[pltpu.VMEM((2, TOP_K * COMBINE_ROWS * TOK_TILES, LANE), F32),
                        pltpu.SemaphoreType.DMA((2,))],
        compiler_params=pltpu.CompilerParams(dimension_semantics=("arbitrary",),
                                             vmem_limit_bytes=VMEM_LIMIT),
        name="combine",
    )(dest_blocks, dest_blocks, rt, y_2d.reshape(-1, TOK_TILES, LANE), y_2d, x1, g)


def _plan_body(rtt_ref, cnt_ref, dk_ref, db_ref, meta_ref, pst):
    i = pl.program_id(0)
    sh = _log2(EXPERT_ROWS)
    n_e = N_EXPERTS

    @pl.when(i == 0)
    def _():
        cnt = cnt_ref[...].astype(jnp.int32)
        padded = (((cnt + (EXPERT_ROWS - 1)) >> sh) << sh).astype(F32)
        e_r = lax.broadcasted_iota(jnp.int32, (n_e, n_e), 0)
        e_c = lax.broadcasted_iota(jnp.int32, (n_e, n_e), 1)
        p_t = jnp.broadcast_to(padded, (n_e, n_e)).T
        pend = jnp.sum(jnp.where(e_c <= e_r, p_t, 0.0), axis=1, keepdims=True)
        pst[...] = pend - padded
        has_rows = p_t > 0.0
        group = jnp.sum(jnp.where((e_c <= e_r) & has_rows, 1.0, 0.0), axis=1, keepdims=True) - 1.0
        nxt = jnp.min(jnp.where((e_c > e_r) & has_rows, e_c, n_e), axis=1, keepdims=True)
        nxt = jnp.where(nxt >= n_e, -1, nxt)

        mb = meta_ref.shape[1]
        blk = lax.broadcasted_iota(jnp.int32, (n_e, mb), 1)
        eb = lax.broadcasted_iota(jnp.int32, (n_e, mb), 0)
        first_row = (blk * EXPERT_ROWS).astype(F32)

        def expert_of(row0):
            return jnp.minimum(jnp.sum(jnp.where(pend <= row0, 1, 0), axis=0, keepdims=True), n_e - 1)

        be = expert_of(first_row)
        be_prev = expert_of(first_row - EXPERT_ROWS)
        hit = eb == be
        wslot = jnp.sum(jnp.where(hit, group, 0.0), axis=0, keepdims=True).astype(jnp.int32) & 1
        nx = jnp.sum(jnp.where(hit, nxt, 0), axis=0, keepdims=True)
        n_used = pend[n_e - 1:n_e, :].astype(jnp.int32) >> sh
        lane = lax.broadcasted_iota(jnp.int32, (1, mb), 1)
        first = (((be != be_prev) | (lane == 0)) & (lane < n_used)).astype(jnp.int32)
        cnt_b = jnp.sum(jnp.where(hit, cnt_ref[...], 0.0), axis=0, keepdims=True)
        pst_b = jnp.sum(jnp.where(hit, pend - padded, 0.0), axis=0, keepdims=True)
        valid = jnp.clip(cnt_b - (first_row[0:1, :] - pst_b), 0.0, float(EXPERT_ROWS)).astype(jnp.int32)
        row8 = lax.broadcasted_iota(jnp.int32, (SUBLANE, mb), 0)
        meta = jnp.where(row8 == 0, be, jnp.where(row8 == 1, first, jnp.where(row8 == 2, wslot,
                         jnp.where(row8 == 3, nx, jnp.where(row8 == 4, n_used, valid)))))
        meta_ref[...] = meta

    tm = rtt_ref.shape[1]
    eid = lax.broadcasted_iota(jnp.int32, (n_e, tm), 0).astype(F32)
    row8 = lax.broadcasted_iota(jnp.int32, (SUBLANE, tm), 0)
    d8 = jnp.zeros((SUBLANE, tm), jnp.int32)
    for k in range(TOP_K):
        start = jnp.sum(jnp.where(eid == rtt_ref[k:k + 1, :], pst[...], 0.0), axis=0, keepdims=True)
        d8 = jnp.where(row8 == k, (start + rtt_ref[TOP_K + k:TOP_K + k + 1, :]).astype(jnp.int32), d8)
    dk_ref[...] = d8[:TOP_K]
    for b in range(tm // COMBINE_ROWS):
        db_ref[b] = d8[:TOP_K, b * COMBINE_ROWS:(b + 1) * COMBINE_ROWS]


def _plan(rtt, cnt):
    n = rtt.shape[1]
    n_rows = n * TOP_K + N_EXPERTS * EXPERT_ROWS
    n_blocks = n_rows // EXPERT_ROWS
    mb = -(-n_blocks // LANE) * LANE
    tile = max(t for t in range(COMBINE_ROWS, PLAN_TILE_MAX + 1, COMBINE_ROWS) if n % t == 0)
    dk, db, meta = pl.pallas_call(
        _plan_body,
        grid=(n // tile,),
        in_specs=[pl.BlockSpec((2 * TOP_K, tile), lambda i: (0, i)),
                  pl.BlockSpec((N_EXPERTS, 1), lambda i: (0, 0))],
        out_specs=[pl.BlockSpec((TOP_K, tile), lambda i: (0, i)),
                   pl.BlockSpec((tile // COMBINE_ROWS, TOP_K, COMBINE_ROWS), lambda i: (i, 0, 0)),
                   pl.BlockSpec((SUBLANE, mb), lambda i: (0, 0))],
        out_shape=[jax.ShapeDtypeStruct((TOP_K, n), jnp.int32),
                   jax.ShapeDtypeStruct((n // COMBINE_ROWS, TOP_K, COMBINE_ROWS), jnp.int32),
                   jax.ShapeDtypeStruct((SUBLANE, mb), jnp.int32)],
        scratch_shapes=[pltpu.VMEM((N_EXPERTS, 1), F32)],
        compiler_params=pltpu.CompilerParams(dimension_semantics=("arbitrary",)),
        name="plan",
    )(rtt, cnt)
    block_meta = (meta[0, :n_blocks], meta[4, 0:1], meta[1, :n_blocks], meta[2, :n_blocks], meta[3, :n_blocks],
                  meta[5, :n_blocks])
    return dk, db, n_rows, block_meta


def _pad_lanes(v, width):
    return jnp.zeros((1, width), F32).at[0, :v.shape[0]].set(v.astype(F32))


def kernel(x_prompt, x_sample, state_gdn_conv, state_gdn, state_gla, rms_mix_w, w_in, conv_w, gdn_a_log,
           gdn_dt_bias, gdn_norm_w, gla_gk_w, gla_gk_b, gla_norm_w, w_out, rms_ffn_w, w_router, b_router,
           w_up, b_up, w_down, b_down, rms_final_w):
    bp, tp, d = x_prompt.shape
    bs, ts, _ = x_sample.shape
    n_p, n_s = bp * tp, bs * ts
    assert d == D_MODEL and state_gdn.shape[0] == 1, "single-layer kernel"
    assert tp >= CONV_WIDTH - 1 and ts >= CONV_WIDTH - 1, "new conv state is taken from the new tokens only"
    l = 0

    wi = w_in[l]
    a0 = GDN_CONV_CH + GDN_V_W
    g0 = a0 + 2 * GDN_HEADS
    lr0 = g0 + 2 * GLA_QK_W + 2 * GLA_V_W
    small = jnp.concatenate([wi[:, a0:a0 + 2 * GDN_HEADS], wi[:, lr0:lr0 + GLA_GATE_RANK],
                             jnp.zeros((d, SM_W - 2 * GDN_HEADS - GLA_GATE_RANK), F32)], axis=1)
    w_big = jnp.concatenate([wi[:, :a0], wi[:, g0:lr0], small], axis=1).astype(BF16)
    alog = _pad_lanes(gdn_a_log[l], SM_W)
    dtb = _pad_lanes(gdn_dt_bias[l], SM_W)
    wgk = jnp.zeros((SM_W, GLA_QK_W), F32).at[SM_LR:SM_LR + GLA_GATE_RANK].set(gla_gk_w[l])
    wr = jnp.zeros((d, LANE), F32).at[:, :N_EXPERTS].set(w_router[l])
    br = jnp.full((1, LANE), -1e30, F32).at[0, :N_EXPERTS].set(b_router[l])

    assert n_p % ROW_TILE == 0 and n_s % ROW_TILE == 0
    x_p, x_s = x_prompt.reshape(n_p, d), x_sample.reshape(n_s, d)
    proj = _inproj(x_p, x_s, rms_mix_w[l][None, :], w_big)

    tb_p = PROMPT_TIME_BLOCK
    zeros_conv = jnp.zeros((bp, CONV_WIDTH - 1, GDN_CONV_CH), F32)
    og_p, gdn_p, conv_p = _gdn(proj, bp, 1, tp, tb_p, CHUNK, CHUNK, zeros_conv,
                               jnp.zeros((bp, GDN_HEADS, GDN_DK, GDN_DV), F32), conv_w[l], alog, dtb,
                               gdn_norm_w[l][None, :])
    ol_p, gla_p = _gla(proj, bp, 1, tp, tb_p, CHUNK, CHUNK, jnp.zeros((bp, GLA_HEADS, GLA_DK, GLA_DV), F32),
                       wgk, gla_gk_b[l][None, :], gla_norm_w[l][None, :])

    ts_pad = SUBLANE
    nb_s = SAMPLE_SEQS_PER_STEP
    proj_s = proj[n_p:].reshape(bs, ts, PROJ_W)
    proj_sp = jnp.pad(proj_s, ((0, 0), (0, ts_pad - ts), (0, 0))).reshape(bs * ts_pad, PROJ_W)
    og_s, gdn_s, conv_s = _gdn(proj_sp, bs, nb_s, ts_pad, ts_pad, ts_pad, ts, state_gdn_conv[l], state_gdn[l],
                               conv_w[l], alog, dtb, gdn_norm_w[l][None, :])
    ol_s, gla_s = _gla(proj_sp, bs, nb_s, ts_pad, ts_pad, ts_pad, ts, state_gla[l], wgk, gla_gk_b[l][None, :],
                       gla_norm_w[l][None, :])
    og_s = og_s.reshape(bs, ts_pad, GDN_V_W)[:, :ts].reshape(n_s, GDN_V_W)
    ol_s = ol_s.reshape(bs, ts_pad, GLA_V_W)[:, :ts].reshape(n_s, GLA_V_W)

    x1, h2, rt, rtt, cnt = _outproj(og_p, og_s, ol_p, ol_s, x_p, x_s, w_out[l].astype(BF16),
                                    rms_ffn_w[l][None, :], wr, br)

    dest_k, dest_b, n_rows, block_meta = _plan(rtt, cnt)
    xs = _dispatch(h2.reshape(-1, PACK_TILES, LANE), dest_k.reshape(-1), n_rows)
    y_rows = _experts(block_meta, xs.reshape(-1, LANE), w_up[l], b_up[l], w_down[l], b_down[l])
    y_p, y_s = _combine(dest_b, rt, y_rows, x1, rms_final_w[None, :], n_p)
    y_prompt = y_p.reshape(bp, tp, d)
    y_sample = y_s.reshape(bs, ts, d)
    return (y_prompt, y_sample, conv_p[None], gdn_p[None], gla_p[None], conv_s[None], gdn_s[None], gla_s[None])
```

```python
import functools

import jax
import jax.numpy as jnp
from jax import lax
from jax.experimental import pallas as pl
from jax.experimental.pallas import tpu as pltpu
from jax.experimental.pallas import tpu_sc as plsc

F32 = jnp.float32
BF16 = jnp.bfloat16
HI = lax.Precision.HIGHEST

D_MODEL = 1024
GDN_HEADS = 4
GDN_DK = 128
GDN_DV = 128
GLA_HEADS = 4
GLA_DK = 64
GLA_DV = 128
GLA_GATE_RANK = 16
GLA_GATE_NORMALIZER = 16.0
CONV_WIDTH = 4
CHUNK = 64
N_EXPERTS = 32
TOP_K = 4
D_FF = 1024
SWIGLU_LIMIT = 7.0
SWIGLU_ALPHA = 1.702
RMS_EPS = 1e-6
L2_EPS = 1e-6

GDN_QK_W = GDN_HEADS * GDN_DK
GDN_V_W = GDN_HEADS * GDN_DV
GDN_CONV_CH = 2 * GDN_QK_W + GDN_V_W
GLA_QK_W = GLA_HEADS * GLA_DK
GLA_V_W = GLA_HEADS * GLA_DV

COL_QKV = 0
COL_Z = 1536
COL_GQ = 2048
COL_GK = 2304
COL_GV = 2560
COL_GG = 3072
COL_SM = 3584
SM_W = 128
PROJ_W = COL_SM + SM_W
SM_A, SM_B, SM_LR = 0, 4, 8

LANE = 128
SUBLANE = 8
TOK_TILES = D_MODEL // LANE
PACK_TILES = TOK_TILES // 2
ROW_TILE = 512
EXPERT_ROWS = 1024
EXPERT_ROW_STEP = 256
EXPERT_WEIGHT_QUEUE = 1
WEIGHT_CAST_ROWS = 128
COMBINE_ROWS = 256
PLAN_TILE_MAX = 2048
DMA_ISSUE_UNROLL = 8
COMBINE_GATHER_QUEUES = (0, 1)
CONV_ROW_SLAB = 128
GDN_CHUNKS_PER_TRIP = 8
GLA_CHUNKS_PER_TRIP = 4
SCAN_CHUNKS_PER_TRIP = 8
PROMPT_TIME_BLOCK = 512
SAMPLE_SEQS_PER_STEP = 16
VMEM_LIMIT = 56 * 1024 * 1024


def _dot(a, b):
    return jnp.dot(a.astype(BF16), b.astype(BF16), preferred_element_type=F32)


def _dot_nt(a, b):
    return lax.dot_general(a.astype(BF16), b.astype(BF16), (((1,), (1,)), ((), ())),
                           preferred_element_type=F32)


def _dot_tn(a, b):
    return lax.dot_general(a.astype(BF16), b.astype(BF16), (((0,), (0,)), ((), ())),
                           preferred_element_type=F32)


def _dot_hi(a, b):
    return jnp.dot(a, b, precision=HI, preferred_element_type=F32)


def _dot_3pass(a, b):
    a_hi = a.astype(BF16)
    b_hi = b.astype(BF16)
    a_lo = (a - a_hi.astype(F32)).astype(BF16)
    b_lo = (b - b_hi.astype(F32)).astype(BF16)

    def mm(x, y):
        return jnp.dot(x, y, preferred_element_type=F32)

    return (mm(a_lo, b_hi) + mm(a_hi, b_lo)) + mm(a_hi, b_hi)


def _rms(x, w):
    return x * lax.rsqrt(jnp.mean(x * x, axis=-1, keepdims=True) + RMS_EPS) * w


def _silu(x):
    return x * jax.nn.sigmoid(x)


def _group_specs(rows, width, n_p_blocks):
    return [pl.BlockSpec((rows, width), lambda i: (jnp.minimum(i, n_p_blocks - 1), 0)),
            pl.BlockSpec((rows, width), lambda i: (jnp.maximum(i - n_p_blocks, 0), 0))]


def _group_pick(i, n_p_blocks, p_ref, s_ref):
    return jnp.where(i < n_p_blocks, p_ref[...], s_ref[...])


def _inproj_body(xp_ref, xs_ref, g_ref, w_ref, o_ref, *, n_p_blocks):
    x = _group_pick(pl.program_id(0), n_p_blocks, xp_ref, xs_ref)
    h = _rms(x, g_ref[...])
    o_ref[...] = jnp.dot(h.astype(BF16), w_ref[...], preferred_element_type=F32)


def _inproj(x_p, x_s, g, w):
    n_p_blocks, n_s_blocks = x_p.shape[0] // ROW_TILE, x_s.shape[0] // ROW_TILE
    n = x_p.shape[0] + x_s.shape[0]
    return pl.pallas_call(
        functools.partial(_inproj_body, n_p_blocks=n_p_blocks),
        grid=(n_p_blocks + n_s_blocks,),
        in_specs=_group_specs(ROW_TILE, D_MODEL, n_p_blocks) + [
            pl.BlockSpec((1, D_MODEL), lambda i: (0, 0)),
            pl.BlockSpec((D_MODEL, PROJ_W), lambda i: (0, 0)),
        ],
        out_specs=pl.BlockSpec((ROW_TILE, PROJ_W), lambda i: (i, 0)),
        out_shape=jax.ShapeDtypeStruct((n, PROJ_W), F32),
        compiler_params=pltpu.CompilerParams(dimension_semantics=("arbitrary",),
                                             vmem_limit_bytes=VMEM_LIMIT),
        name="in_proj",
    )(x_p, x_s, g, w)


def _log2(n):
    assert n & (n - 1) == 0
    return n.bit_length() - 1


def _tri_inv_all(ms, c, ii, jj):
    eye = (ii == jj).astype(F32)
    base = min(c, 8)
    sh = _log2(base)
    blk = (ii >> sh) == (jj >> sh)
    ns = [jnp.where(blk, m, 0.0) for m in ms]
    xs = [eye - n for n in ns]
    ps = [_dot(n, n) for n in ns]
    ts = [_dot(jnp.concatenate([x, p], axis=0), p) for x, p in zip(xs, ps)]
    xs = [x + t[:c] for x, t in zip(xs, ts)]
    ps = [t[c:] for t in ts]
    xs = [x + _dot(x, p) for x, p in zip(xs, ps)]
    s = base
    while s < c:
        sh_s, sh_b = _log2(s), _log2(2 * s)
        off = ((ii >> sh_b) == (jj >> sh_b)) & ((ii >> sh_s) != (jj >> sh_s))
        ys = [_dot(x, jnp.where(off, m, 0.0)) for x, m in zip(xs, ms)]
        xs = [x - _dot(y, x) for x, y in zip(xs, ys)]
        s *= 2
    return xs


def _gated_norm(o, w, z):
    return o * lax.rsqrt(jnp.mean(o * o, axis=-1, keepdims=True) + RMS_EPS) * w * _silu(z)


def _chunk_rows(s, tb_rows, ci, c):
    r = s * tb_rows + ci * c
    if not isinstance(r, int):
        r = pl.multiple_of(r, c)
    return r


def _for_chunks(n_chunks, step):
    if n_chunks == 1:
        step(0, 0)
    else:
        lax.fori_loop(0, n_chunks, step, 0)


def _gdn_body(qkv_ref, z_ref, sm_ref, cbuf_ref, s0_ref, cw_ref, alog_ref, dtb_ref, nw_ref,
              o_ref, sout_ref, cout_ref, st, xc, act, gcs, us, wss, qgs, kds, aqs,
              *, nb, tb_rows, chunk, valid, n_tb):
    tb = pl.program_id(1)
    c = chunk
    n_heads = GDN_HEADS
    tail = CONV_WIDTH - 1
    pad = SUBLANE
    units = [(s, h) for s in range(nb) for h in range(n_heads)]

    n_slabs = GDN_CONV_CH // LANE

    def lanes(j):
        return slice(j * LANE, (j + 1) * LANE)

    @pl.when(tb == 0)
    def _():
        st[...] = s0_ref[...]
        for s in range(nb):
            for j in range(n_slabs):
                xc[s, j, pad - tail:pad, :] = cbuf_ref[s, :, lanes(j)]

    if n_tb > 1:
        @pl.when(tb > 0)
        def _():
            for s in range(nb):
                for j in range(n_slabs):
                    xc[s, j, pad - tail:pad, :] = xc[s, j, tb_rows + pad - tail:tb_rows + pad, :]

    for s in range(nb):
        for j in range(n_slabs):
            xc[s, j, pad:pad + tb_rows, :] = qkv_ref[s * tb_rows:(s + 1) * tb_rows, lanes(j)]

    row_slab = min(tb_rows, CONV_ROW_SLAB)
    parities = 2 if row_slab >= 2 * SUBLANE else 1
    for s in range(nb):
        for j in range(n_slabs):
            src, dst = xc.at[s, j], act.at[j]
            for sl in range(tb_rows // row_slab):
                for p in range(parities):
                    lo = pad - tail + sl * row_slab + p
                    out0 = s * tb_rows + sl * row_slab + p

                    def rows_from(start):
                        if parities == 1:
                            return pl.ds(start, row_slab)
                        return pl.ds(start, row_slab // 2, stride=2)

                    acc = src[rows_from(lo), :] * cw_ref[0:1, lanes(j)]
                    for i in range(1, CONV_WIDTH):
                        acc = acc + src[rows_from(lo + i), :] * cw_ref[i:i + 1, lanes(j)]
                    dst[rows_from(out0), :] = _silu(acc)

    ii = lax.broadcasted_iota(jnp.int32, (c, c), 0)
    jj = lax.broadcasted_iota(jnp.int32, (c, c), 1)
    lower = (ii >= jj)
    lower_f = lower.astype(F32)
    strict = (ii > jj)
    rowmask = None
    if valid < c:
        rowmask = lax.broadcasted_iota(jnp.int32, (c, 1), 0) < valid

    def hs(h, w):
        return slice(h * w, (h + 1) * w)

    n_chunks = tb_rows // c
    cpi = next(k for k in (GDN_CHUNKS_PER_TRIP, 2, 1) if n_chunks % k == 0)
    p1_units = [(g, h) for g in range(nb * cpi) for h in range(n_heads)]

    def phase1(ci, carry):
        rows, b_ts, gc_ts, gc_tts = [], [], [], []
        for g in range(nb * cpi):
            rr = pl.ds(_chunk_rows(g // cpi, tb_rows, ci * cpi + g % cpi, c), c)
            sm = sm_ref[rr, :]
            g_t = -jnp.exp(alog_ref[...]) * jax.nn.softplus(sm + dtb_ref[...])
            b_t = jax.nn.sigmoid(sm)
            if rowmask is not None:
                g_t = jnp.where(rowmask, g_t, 0.0)
                b_t = jnp.where(rowmask, b_t, 0.0)
            gc_t = _dot_hi(lower_f, g_t)
            gcs[rr, :] = gc_t
            rows.append(rr)
            b_ts.append(b_t)
            gc_ts.append(gc_t)
            gc_tts.append(gc_t.T)
        qn, kn, kb, vb = {}, {}, {}, {}
        for (s, h) in p1_units:
            q = act[h, rows[s], :]
            k = act[n_heads + h, rows[s], :]
            v = act[2 * n_heads + h, rows[s], :]
            if rowmask is not None:
                q = jnp.where(rowmask, q, 0.0)
                k = jnp.where(rowmask, k, 0.0)
                v = jnp.where(rowmask, v, 0.0)
            qn[s, h] = q * lax.rsqrt(jnp.sum(q * q, axis=-1, keepdims=True) + L2_EPS) * (GDN_DK ** -0.5)
            kn[s, h] = k * lax.rsqrt(jnp.sum(k * k, axis=-1, keepdims=True) + L2_EPS)
            beta = b_ts[s][:, SM_B + h:SM_B + h + 1]
            kb[s, h] = kn[s, h] * beta
            vb[s, h] = v * beta
        s1 = {u: _dot_nt(jnp.concatenate([kb[u], qn[u]], axis=0), kn[u]) for u in p1_units}
        mm = []
        for (s, h) in p1_units:
            gcol = gc_ts[s][:, SM_A + h:SM_A + h + 1]
            grow = gc_tts[s][SM_A + h:SM_A + h + 1, :]
            dec = jnp.exp(jnp.where(lower, gcol - grow, -jnp.inf))
            mm.append(jnp.where(strict, s1[s, h][:c] * dec, 0.0))
            aqs[h, rows[s], :] = s1[s, h][c:] * dec
        tms = _tri_inv_all(mm, c, ii, jj)
        for (s, h), tm in zip(p1_units, tms):
            gcol = gc_ts[s][:, SM_A + h:SM_A + h + 1]
            eg = jnp.exp(gcol)
            uw = _dot(tm, jnp.concatenate([vb[s, h], kb[s, h] * eg], axis=1))
            us[rows[s], hs(h, GDN_DV)] = uw[:, :GDN_DV]
            wss[rows[s], hs(h, GDN_DV)] = uw[:, GDN_DV:]
            qgs[rows[s], hs(h, GDN_DK)] = qn[s, h] * eg
            kds[rows[s], hs(h, GDN_DK)] = kn[s, h] * jnp.exp(gcol[c - 1:c, :] - gcol)
        return carry

    cp2 = next(k for k in (SCAN_CHUNKS_PER_TRIP, 1) if n_chunks % k == 0)

    def phase2(ti, carry):
        for u in range(cp2):
            scan_chunk(ti * cp2 + u)
        return carry

    def scan_chunk(ci):
        r0 = [_chunk_rows(s, tb_rows, ci, c) for s in range(nb)]
        rows = [pl.ds(r, c) for r in r0]
        ws = {(s, h): _dot(jnp.concatenate([wss[rows[s], hs(h, GDN_DV)], qgs[rows[s], hs(h, GDN_DK)]], axis=0),
                           st[s, h]) for (s, h) in units}
        v_new = {(s, h): us[rows[s], hs(h, GDN_DV)] - ws[s, h][:c] for (s, h) in units}
        o = {(s, h): ws[s, h][c:] + _dot(aqs[h, rows[s], :], v_new[s, h]) for (s, h) in units}
        upd = {(s, h): _dot_tn(kds[rows[s], hs(h, GDN_DK)], v_new[s, h]) for (s, h) in units}
        for (s, h) in units:
            g_last = gcs[pl.ds(r0[s] + c - 1, 1), SM_A + h:SM_A + h + 1]
            st[s, h] = st[s, h] * jnp.exp(g_last) + upd[s, h]
        for s in range(nb):
            o_ref[rows[s], :] = jnp.concatenate(
                [_gated_norm(o[s, h], nw_ref[...], z_ref[rows[s], hs(h, GDN_DV)]) for h in range(n_heads)], axis=1)

    _for_chunks(n_chunks // cpi, phase1)
    _for_chunks(n_chunks // cp2, phase2)

    @pl.when(tb == n_tb - 1)
    def _():
        sout_ref[...] = st[...]
        last = tb_rows if valid == c else valid
        for s in range(nb):
            cout_ref[s] = jnp.concatenate(
                [xc[s, j, pad + last - tail:pad + last, :] for j in range(n_slabs)], axis=1)


def _gdn(proj, n_seq, nb, t_len, tb_rows, chunk, valid, conv_buf, s0, conv_w, alog, dtb, nw):
    n_tb = t_len // tb_rows
    assert nb == 1 or n_tb == 1
    rows = nb * tb_rows

    def rowblk(b, t):
        return b * n_tb + t

    body = functools.partial(_gdn_body, nb=nb, tb_rows=tb_rows, chunk=chunk, valid=valid, n_tb=n_tb)
    return pl.pallas_call(
        body,
        grid=(n_seq // nb, n_tb),
        in_specs=[
            pl.BlockSpec((rows, GDN_CONV_CH), lambda b, t: (rowblk(b, t), COL_QKV // GDN_CONV_CH)),
            pl.BlockSpec((rows, GDN_V_W), lambda b, t: (rowblk(b, t), COL_Z // GDN_V_W)),
            pl.BlockSpec((rows, SM_W), lambda b, t: (rowblk(b, t), COL_SM // SM_W)),
            pl.BlockSpec((nb, CONV_WIDTH - 1, GDN_CONV_CH), lambda b, t: (b, 0, 0)),
            pl.BlockSpec((nb, GDN_HEADS, GDN_DK, GDN_DV), lambda b, t: (b, 0, 0, 0)),
            pl.BlockSpec((CONV_WIDTH, GDN_CONV_CH), lambda b, t: (0, 0)),
            pl.BlockSpec((1, SM_W), lambda b, t: (0, 0)),
            pl.BlockSpec((1, SM_W), lambda b, t: (0, 0)),
            pl.BlockSpec((1, GDN_DV), lambda b, t: (0, 0)),
        ],
        out_specs=[
            pl.BlockSpec((rows, GDN_V_W), lambda b, t: (rowblk(b, t), 0)),
            pl.BlockSpec((nb, GDN_HEADS, GDN_DK, GDN_DV), lambda b, t: (b, 0, 0, 0)),
            pl.BlockSpec((nb, CONV_WIDTH - 1, GDN_CONV_CH), lambda b, t: (b, 0, 0)),
        ],
        out_shape=[
            jax.ShapeDtypeStruct((n_seq * t_len, GDN_V_W), F32),
            jax.ShapeDtypeStruct((n_seq, GDN_HEADS, GDN_DK, GDN_DV), F32),
            jax.ShapeDtypeStruct((n_seq, CONV_WIDTH - 1, GDN_CONV_CH), F32),
        ],
        scratch_shapes=[
            pltpu.VMEM((nb, GDN_HEADS, GDN_DK, GDN_DV), F32),
            pltpu.VMEM((nb, GDN_CONV_CH // LANE, tb_rows + SUBLANE, LANE), F32),
            pltpu.VMEM((GDN_CONV_CH // LANE, rows, LANE), F32),
            pltpu.VMEM((rows, SM_W), F32),
            pltpu.VMEM((rows, GDN_V_W), F32),
            pltpu.VMEM((rows, GDN_V_W), F32),
            pltpu.VMEM((rows, GDN_QK_W), F32),
            pltpu.VMEM((rows, GDN_QK_W), F32),
            pltpu.VMEM((GDN_HEADS, rows, chunk), F32),
        ],
        compiler_params=pltpu.CompilerParams(dimension_semantics=("arbitrary", "arbitrary"),
                                             vmem_limit_bytes=VMEM_LIMIT),
        name="gdn_mixer",
    )(proj, proj, proj, conv_buf, s0, conv_w, alog, dtb, nw)


def _gla_body(q_ref, k_ref, v_ref, go_ref, sm_ref, s0_ref, wgk_ref, bgk_ref, nw_ref,
              o_ref, sout_ref, st, qes, ois, upds, decs, *, nb, tb_rows, chunk, valid, n_tb):
    tb = pl.program_id(1)
    c = chunk
    n_heads = GLA_HEADS
    units = [(s, h) for s in range(nb) for h in range(n_heads)]

    @pl.when(tb == 0)
    def _():
        st[...] = s0_ref[...]

    ii = lax.broadcasted_iota(jnp.int32, (c, c), 0)
    jj = lax.broadcasted_iota(jnp.int32, (c, c), 1)
    lower = (ii >= jj)
    lower_f = lower.astype(F32)
    rid = lax.broadcasted_iota(jnp.int32, (c, 1), 0)
    rowmask = (rid < valid) if valid < c else None
    n_sub = max(c // 16, 1)
    sub = c // n_sub

    n_chunks = tb_rows // c
    cpi = next(k for k in (GLA_CHUNKS_PER_TRIP, 2, 1) if n_chunks % k == 0)
    p1_units = [(g, h) for g in range(nb * cpi) for h in range(n_heads)]

    def phase1(ci, carry):
        rows, slots, bcs, bc_ts = [], [], [], []
        for g in range(nb * cpi):
            chunk_idx = ci * cpi + g % cpi
            rr = pl.ds(_chunk_rows(g // cpi, tb_rows, chunk_idx, c), c)
            slots.append((g // cpi) * n_chunks + chunk_idx)
            gk = jax.nn.log_sigmoid(_dot(sm_ref[rr, :], wgk_ref[...]) + bgk_ref[...]) / GLA_GATE_NORMALIZER
            if rowmask is not None:
                gk = jnp.where(rowmask, gk, 0.0)
            bc = _dot_hi(lower_f, gk)
            rows.append(rr)
            bcs.append(bc)
            bc_ts.append(bc.T)
        q, k, v, bch = {}, {}, {}, {}
        for (s, h) in p1_units:
            ks = slice(h * GLA_DK, (h + 1) * GLA_DK)
            vs = slice(h * GLA_DV, (h + 1) * GLA_DV)
            q[s, h] = q_ref[rows[s], ks] * (GLA_DK ** -0.5)
            kk = k_ref[rows[s], ks]
            vv = v_ref[rows[s], vs]
            if rowmask is not None:
                kk = jnp.where(rowmask, kk, 0.0)
                vv = jnp.where(rowmask, vv, 0.0)
            k[s, h], v[s, h] = kk, vv
            bch[s, h] = bcs[s][:, ks]
        for (g, h) in p1_units:
            qes[h, rows[g], :] = q[g, h] * jnp.exp(bch[g, h])
        a = {}
        for u in p1_units:
            q_parts, k_parts = [], []
            for sb in range(n_sub):
                ref_row = bch[u][sb * sub:sb * sub + 1, :]
                in_blk = (rid >= sb * sub) & (rid < (sb + 1) * sub)
                q_parts.append(jnp.where(in_blk, q[u] * jnp.exp(jnp.where(in_blk, bch[u] - ref_row, 0.0)), 0.0))
                k_parts.append(k[u] * jnp.exp(jnp.where(rid < (sb + 1) * sub, ref_row - bch[u], 0.0)))
            q_hat = jnp.concatenate(q_parts, axis=1) if n_sub > 1 else q_parts[0]
            k_hat = jnp.concatenate(k_parts, axis=1) if n_sub > 1 else k_parts[0]
            a[u] = jnp.where(lower, _dot_nt(q_hat, k_hat), 0.0)
        upd = {u: _dot_tn(k[u] * jnp.exp(bch[u][c - 1:c, :] - bch[u]), v[u]) for u in p1_units}
        o_intra = {u: _dot(a[u], v[u]) for u in p1_units}
        for (g, h) in p1_units:
            dec_col = bc_ts[g][h * GLA_DK:(h + 1) * GLA_DK, c - 1:c]
            decs[slots[g], h] = jnp.broadcast_to(jnp.exp(dec_col), (GLA_DK, GLA_DV))
            upds[slots[g], h] = upd[g, h]
            ois[rows[g], h * GLA_DV:(h + 1) * GLA_DV] = o_intra[g, h]
        return carry

    cp2 = next(k for k in (SCAN_CHUNKS_PER_TRIP, 1) if n_chunks % k == 0)

    def phase2(ti, carry):
        for u in range(cp2):
            scan_chunk(ti * cp2 + u)
        return carry

    def scan_chunk(ci):
        rows = [pl.ds(_chunk_rows(s, tb_rows, ci, c), c) for s in range(nb)]
        o = {(s, h): ois[rows[s], h * GLA_DV:(h + 1) * GLA_DV] + _dot(qes[h, rows[s], :], st[s, h])
             for (s, h) in units}
        for (s, h) in units:
            st[s, h] = decs[s * n_chunks + ci, h] * st[s, h] + upds[s * n_chunks + ci, h]
        for s in range(nb):
            o_ref[rows[s], :] = jnp.concatenate(
                [_gated_norm(o[s, h], nw_ref[...], go_ref[rows[s], h * GLA_DV:(h + 1) * GLA_DV])
                 for h in range(n_heads)], axis=1)

    _for_chunks(n_chunks // cpi, phase1)
    _for_chunks(n_chunks // cp2, phase2)

    @pl.when(tb == n_tb - 1)
    def _():
        sout_ref[...] = st[...]


def _gla(proj, n_seq, nb, t_len, tb_rows, chunk, valid, s0, wgk, bgk, nw):
    n_tb = t_len // tb_rows
    assert nb == 1 or n_tb == 1
    rows = nb * tb_rows

    def rowblk(b, t):
        return b * n_tb + t

    body = functools.partial(_gla_body, nb=nb, tb_rows=tb_rows, chunk=chunk, valid=valid, n_tb=n_tb)
    return pl.pallas_call(
        body,
        grid=(n_seq // nb, n_tb),
        in_specs=[
            pl.BlockSpec((rows, GLA_QK_W), lambda b, t: (rowblk(b, t), COL_GQ // GLA_QK_W)),
            pl.BlockSpec((rows, GLA_QK_W), lambda b, t: (rowblk(b, t), COL_GK // GLA_QK_W)),
            pl.BlockSpec((rows, GLA_V_W), lambda b, t: (rowblk(b, t), COL_GV // GLA_V_W)),
            pl.BlockSpec((rows, GLA_V_W), lambda b, t: (rowblk(b, t), COL_GG // GLA_V_W)),
            pl.BlockSpec((rows, SM_W), lambda b, t: (rowblk(b, t), COL_SM // SM_W)),
            pl.BlockSpec((nb, GLA_HEADS, GLA_DK, GLA_DV), lambda b, t: (b, 0, 0, 0)),
            pl.BlockSpec((SM_W, GLA_QK_W), lambda b, t: (0, 0)),
            pl.BlockSpec((1, GLA_QK_W), lambda b, t: (0, 0)),
            pl.BlockSpec((1, GLA_DV), lambda b, t: (0, 0)),
        ],
        out_specs=[
            pl.BlockSpec((rows, GLA_V_W), lambda b, t: (rowblk(b, t), 0)),
            pl.BlockSpec((nb, GLA_HEADS, GLA_DK, GLA_DV), lambda b, t: (b, 0, 0, 0)),
        ],
        out_shape=[
            jax.ShapeDtypeStruct((n_seq * t_len, GLA_V_W), F32),
            jax.ShapeDtypeStruct((n_seq, GLA_HEADS, GLA_DK, GLA_DV), F32),
        ],
        scratch_shapes=[
            pltpu.VMEM((nb, GLA_HEADS, GLA_DK, GLA_DV), F32),
            pltpu.VMEM((GLA_HEADS, rows, GLA_DK), F32),
            pltpu.VMEM((rows, GLA_V_W), F32),
            pltpu.VMEM((rows // chunk, GLA_HEADS, GLA_DK, GLA_DV), F32),
            pltpu.VMEM((rows // chunk, GLA_HEADS, GLA_DK, GLA_DV), F32),
        ],
        compiler_params=pltpu.CompilerParams(dimension_semantics=("arbitrary", "arbitrary"),
                                             vmem_limit_bytes=VMEM_LIMIT),
        name="gla_mixer",
    )(proj, proj, proj, proj, proj, s0, wgk, bgk, nw)


def _outproj_body(ogp_ref, ogs_ref, olp_ref, ols_ref, xp_ref, xs_ref, wo_ref, g_ref, wr_ref, br_ref,
                  x1_ref, h2_ref, rt_ref, rtt_ref, cnt_ref, base, *, n_p_blocks):
    i = pl.program_id(0)

    @pl.when(i == 0)
    def _():
        base[...] = jnp.zeros_like(base)

    o = jnp.concatenate([_group_pick(i, n_p_blocks, ogp_ref, ogs_ref),
                         _group_pick(i, n_p_blocks, olp_ref, ols_ref)], axis=1)
    x1 = _group_pick(i, n_p_blocks, xp_ref, xs_ref) + jnp.dot(o.astype(BF16), wo_ref[...],
                                                               preferred_element_type=F32)
    x1_ref[...] = x1
    h = _rms(x1, g_ref[...])
    _store_token_tiles(h2_ref, _pack_bf16_pairs(h))
    logits = _dot_3pass(h, wr_ref[...]) + br_ref[...]

    tm = logits.shape[0]
    lt = logits.T[:N_EXPERTS]
    eid = lax.broadcasted_iota(jnp.int32, (N_EXPERTS, tm), 0)
    work = lt
    sel = jnp.zeros((N_EXPERTS, tm), F32)
    hits, ids, vals = [], [], []
    for _ in range(TOP_K):
        m = jnp.max(work, axis=0, keepdims=True)
        idx = jnp.min(jnp.where(work == m, eid, N_EXPERTS), axis=0, keepdims=True)
        hit = eid == idx
        hits.append(hit)
        ids.append(idx)
        vals.append(m)
        work = jnp.where(hit, -jnp.inf, work)
        sel = sel + hit.astype(F32)
    exps = [jnp.exp(v - vals[0]) for v in vals]
    den = exps[0]
    for e in exps[1:]:
        den = den + e
    gates = [e / den for e in exps]

    ri = lax.broadcasted_iota(jnp.int32, (tm, tm), 0)
    ci = lax.broadcasted_iota(jnp.int32, (tm, tm), 1)
    before = _dot(sel, (ri < ci).astype(F32)) + base[...]
    ranks = [jnp.sum(jnp.where(hit, before, 0.0), axis=0, keepdims=True) for hit in hits]
    base[...] = base[...] + jnp.sum(sel, axis=1, keepdims=True)
    cnt_ref[...] = base[...]

    row = lax.broadcasted_iota(jnp.int32, (LANE, tm), 0)
    rec = jnp.zeros((LANE, tm), F32)
    for k in range(TOP_K):
        rec = jnp.where(row == k, ids[k].astype(F32), rec)
        rec = jnp.where(row == TOP_K + k, ranks[k], rec)
        rec = jnp.where(row == 2 * TOP_K + k, gates[k], rec)
    rt_ref[...] = rec.T
    rtt_ref[...] = rec[:2 * TOP_K]


def _outproj(og_p, og_s, ol_p, ol_s, x_p, x_s, wo, g, wr, br):
    n_p_blocks, n_s_blocks = x_p.shape[0] // ROW_TILE, x_s.shape[0] // ROW_TILE
    n = x_p.shape[0] + x_s.shape[0]
    return pl.pallas_call(
        functools.partial(_outproj_body, n_p_blocks=n_p_blocks),
        grid=(n_p_blocks + n_s_blocks,),
        in_specs=_group_specs(ROW_TILE, GDN_V_W, n_p_blocks) + _group_specs(ROW_TILE, GLA_V_W, n_p_blocks)
        + _group_specs(ROW_TILE, D_MODEL, n_p_blocks) + [
            pl.BlockSpec((D_MODEL, D_MODEL), lambda i: (0, 0)),
            pl.BlockSpec((1, D_MODEL), lambda i: (0, 0)),
            pl.BlockSpec((D_MODEL, LANE), lambda i: (0, 0)),
            pl.BlockSpec((1, LANE), lambda i: (0, 0)),
        ],
        out_specs=[
            pl.BlockSpec((ROW_TILE, D_MODEL), lambda i: (i, 0)),
            pl.BlockSpec((ROW_TILE * PACK_TILES, LANE), lambda i: (i, 0)),
            pl.BlockSpec((ROW_TILE, LANE), lambda i: (i, 0)),
            pl.BlockSpec((2 * TOP_K, ROW_TILE), lambda i: (0, i)),
            pl.BlockSpec((N_EXPERTS, 1), lambda i: (0, 0)),
        ],
        out_shape=[
            jax.ShapeDtypeStruct((n, D_MODEL), F32),
            jax.ShapeDtypeStruct((n * PACK_TILES, LANE), jnp.uint32),
            jax.ShapeDtypeStruct((n, LANE), F32),
            jax.ShapeDtypeStruct((2 * TOP_K, n), F32),
            jax.ShapeDtypeStruct((N_EXPERTS, 1), F32),
        ],
        scratch_shapes=[pltpu.VMEM((N_EXPERTS, 1), F32)],
        compiler_params=pltpu.CompilerParams(dimension_semantics=("arbitrary",),
                                             vmem_limit_bytes=VMEM_LIMIT),
        name="out_proj",
    )(og_p, og_s, ol_p, ol_s, x_p, x_s, wo, g, wr, br)


def _store_token_tiles(ref2d, val):
    rows, tiles = val.shape[0], val.shape[1] // LANE
    for c in range(tiles):
        ref2d[pl.ds(c, rows, stride=tiles), :] = val[:, c * LANE:(c + 1) * LANE]


def _load_token_tiles(ref2d, first_row, rows, tiles=TOK_TILES):
    return jnp.concatenate(
        [ref2d[pl.ds(first_row * tiles + c, rows, stride=tiles), :] for c in range(tiles)], axis=1)


def _pack_bf16_pairs(x):
    half = x.shape[1] // 2
    bits = lax.bitcast_convert_type(x.astype(BF16).astype(F32), jnp.uint32)
    return (bits[:, :half] >> 16) | (bits[:, half:] & jnp.uint32(0xFFFF0000))


def _unpack_bf16_pairs(w):
    lo = lax.bitcast_convert_type(w << 16, F32)
    hi = lax.bitcast_convert_type(w & jnp.uint32(0xFFFF0000), F32)
    return jnp.concatenate([lo, hi], axis=1).astype(BF16)


def _expert_weight_copies(e, ws, wup_hbm, wdn_hbm, wup_buf, wdn_buf, wsems):
    return (pltpu.make_async_copy(wup_hbm.at[e], wup_buf.at[ws], wsems.at[ws]),
            pltpu.make_async_copy(wdn_hbm.at[e], wdn_buf.at[ws], wsems.at[ws]))


def _expert_body(be_ref, nu_ref, first_ref, wslot_ref, next_ref, valid_ref, x_ref,
                 wup_hbm, bup_ref, wdn_hbm, bdn_ref, y_ref, wup_buf, wdn_buf, wsems, wup_bf, wdn_bf):
    i = pl.program_id(0)
    n_used = nu_ref[0]
    ws = wslot_ref[i]
    weight_copies = functools.partial(_expert_weight_copies, wup_hbm=wup_hbm, wdn_hbm=wdn_hbm, wup_buf=wup_buf,
                                      wdn_buf=wdn_buf, wsems=wsems)

    @pl.when((i == 0) & (n_used > 0))
    def _():
        for cp in weight_copies(be_ref[0], ws):
            cp.start(priority=EXPERT_WEIGHT_QUEUE)

    @pl.when(i < n_used)
    def _():
        @pl.when(first_ref[i] == 1)
        def _():
            for cp in weight_copies(be_ref[i], ws):
                cp.wait()

            @pl.when(next_ref[i] >= 0)
            def _():
                for cp in weight_copies(next_ref[i], 1 - ws):
                    cp.start(priority=EXPERT_WEIGHT_QUEUE)

            for r in range(0, D_MODEL, WEIGHT_CAST_ROWS):
                wup_bf[r:r + WEIGHT_CAST_ROWS, :] = wup_buf[ws, r:r + WEIGHT_CAST_ROWS, :].astype(BF16)
            for r in range(0, D_FF, WEIGHT_CAST_ROWS):
                wdn_bf[r:r + WEIGHT_CAST_ROWS, :] = wdn_buf[ws, r:r + WEIGHT_CAST_ROWS, :].astype(BF16)

        def expert_rows(n):
            x = _unpack_bf16_pairs(_load_token_tiles(x_ref, 0, n, PACK_TILES))
            gu = _dot(x, wup_bf[...]) + bup_ref[...]
            gate = jnp.minimum(gu[:, :D_FF], SWIGLU_LIMIT)
            up = jnp.clip(gu[:, D_FF:], -SWIGLU_LIMIT, SWIGLU_LIMIT)
            a = (up + 1.0) * gate * jax.nn.sigmoid(SWIGLU_ALPHA * gate)
            _store_token_tiles(y_ref, _dot(a, wdn_bf[...]) + bdn_ref[...])

        valid = valid_ref[i]
        for n in range(EXPERT_ROW_STEP, EXPERT_ROWS + 1, EXPERT_ROW_STEP):
            @pl.when((valid <= n) if n == EXPERT_ROW_STEP else ((valid > n - EXPERT_ROW_STEP) & (valid <= n)))
            def _(n=n):
                expert_rows(n)
                if n < EXPERT_ROWS:
                    y_ref[n * TOK_TILES:, :] = jnp.zeros(((EXPERT_ROWS - n) * TOK_TILES, LANE), F32)

    @pl.when(i >= n_used)
    def _():
        y_ref[...] = jnp.zeros_like(y_ref)


def _experts(block_meta, xs_2d, w_up, b_up, w_down, b_down):
    n_blocks = block_meta[0].shape[0]
    grid_spec = pltpu.PrefetchScalarGridSpec(
        num_scalar_prefetch=len(block_meta),
        grid=(n_blocks,),
        in_specs=[
            pl.BlockSpec((EXPERT_ROWS * PACK_TILES, LANE), lambda i, *_: (i, 0)),
            pl.BlockSpec(memory_space=pl.ANY),
            pl.BlockSpec((None, 1, 2 * D_FF), lambda i, be, *_: (be[i], 0, 0)),
            pl.BlockSpec(memory_space=pl.ANY),
            pl.BlockSpec((None, 1, D_MODEL), lambda i, be, *_: (be[i], 0, 0)),
        ],
        out_specs=pl.BlockSpec((EXPERT_ROWS * TOK_TILES, LANE), lambda i, *_: (i, 0)),
        scratch_shapes=[
            pltpu.VMEM((2, D_MODEL, 2 * D_FF), F32),
            pltpu.VMEM((2, D_FF, D_MODEL), F32),
            pltpu.SemaphoreType.DMA((2,)),
            pltpu.VMEM((D_MODEL, 2 * D_FF), BF16),
            pltpu.VMEM((D_FF, D_MODEL), BF16),
        ],
    )
    return pl.pallas_call(
        _expert_body,
        grid_spec=grid_spec,
        out_shape=jax.ShapeDtypeStruct((n_blocks * EXPERT_ROWS * TOK_TILES, LANE), F32),
        compiler_params=pltpu.CompilerParams(dimension_semantics=("arbitrary",),
                                             vmem_limit_bytes=VMEM_LIMIT),
        name="experts",
    )(*block_meta, xs_2d, w_up, b_up.reshape(N_EXPERTS, 1, 2 * D_FF), w_down,
      b_down.reshape(N_EXPERTS, 1, D_MODEL))


def _dispatch(h_tiles, dest_kmajor, n_rows):
    n_tok = h_tiles.shape[0]
    info = plsc.get_sparse_core_info()
    n_workers = info.num_cores * info.num_subcores
    per_worker = n_tok // n_workers
    chunk = next(c for c in (128, 96, 88, 64, 48, 32, 16, 8) if per_worker % c == 0)
    assert n_tok % n_workers == 0 and per_worker % SUBLANE == 0
    mesh = plsc.VectorSubcoreMesh(core_axis_name="c", subcore_axis_name="s")

    @functools.partial(
        pl.kernel, mesh=mesh,
        out_type=jax.ShapeDtypeStruct((n_rows,) + h_tiles.shape[1:], h_tiles.dtype),
        scratch_types=[pltpu.VMEM((TOP_K, chunk), jnp.int32), pltpu.VMEM((chunk,) + h_tiles.shape[1:], h_tiles.dtype),
                       pltpu.SemaphoreType.DMA],
    )
    def dispatch(h_hbm, dest_hbm, out_hbm, idx_v, rows_v, sem):
        wid = lax.axis_index("s") * info.num_cores + lax.axis_index("c")

        def step(j, carry):
            t0 = pl.multiple_of(wid * per_worker + j * chunk, SUBLANE)
            loads = [pltpu.async_copy(h_hbm.at[pl.ds(t0, chunk)], rows_v, sem)]
            for k in range(TOP_K):
                loads.append(pltpu.async_copy(
                    dest_hbm.at[pl.ds(pl.multiple_of(k * n_tok + t0, SUBLANE), chunk)], idx_v.at[k], sem))
            for cp in loads:
                cp.wait()
            stores = [pltpu.async_copy(rows_v, out_hbm.at[idx_v.at[k]], sem) for k in range(TOP_K)]
            for cp in stores:
                cp.wait()
            return carry

        lax.fori_loop(0, per_worker // chunk, step, 0)

    return dispatch(h_tiles, dest_kmajor)


def _gather_rows(src_tiles, idx_ref, n_rows, dst2d, sem, priorities):
    def issue(j, carry):
        for u in range(DMA_ISSUE_UNROLL):
            r = j * DMA_ISSUE_UNROLL + u
            dst = dst2d.at[pl.ds(pl.multiple_of(r * TOK_TILES, TOK_TILES), TOK_TILES), :]
            pltpu.make_async_copy(src_tiles.at[idx_ref[0, r]], dst, sem).start(
                priority=priorities[u % len(priorities)])
        return carry

    lax.fori_loop(0, n_rows // DMA_ISSUE_UNROLL, issue, 0)


def _wait_rows(src2d, n_rows, dst2d, sem):
    pltpu.make_async_copy(src2d.at[pl.ds(0, n_rows * TOK_TILES), :], dst2d, sem).wait()


def _combine_body(dest_ref, dest_next_ref, rt_ref, y_tiles, y_2d, x1_ref, g_ref, op_ref, os_ref, ybuf, sems,
                  *, n_p_blocks):
    i = pl.program_id(0)
    slot = i % 2
    n_rows = TOP_K * COMBINE_ROWS

    @pl.when(i == 0)
    def _():
        _gather_rows(y_tiles, dest_ref, n_rows, ybuf.at[0], sems.at[0], COMBINE_GATHER_QUEUES)

    _wait_rows(y_2d, n_rows, ybuf.at[slot], sems.at[slot])

    @pl.when(i + 1 < pl.num_programs(0))
    def _():
        _gather_rows(y_tiles, dest_next_ref, n_rows, ybuf.at[1 - slot], sems.at[1 - slot], COMBINE_GATHER_QUEUES)

    buf = ybuf.at[slot]
    moe = _load_token_tiles(buf, 0, COMBINE_ROWS) * rt_ref[:, 2 * TOP_K:2 * TOP_K + 1]
    for k in range(1, TOP_K):
        moe = moe + _load_token_tiles(buf, k * COMBINE_ROWS, COMBINE_ROWS) * rt_ref[:, 2 * TOP_K + k:2 * TOP_K + k + 1]
    res = _rms(x1_ref[...] + moe, g_ref[...])

    @pl.when(i < n_p_blocks)
    def _():
        op_ref[...] = res

    @pl.when(i >= n_p_blocks)
    def _():
        os_ref[...] = res


def _combine(dest_b, rt, y_2d, x1, g, n_p):
    n = x1.shape[0]
    n_blk = n // COMBINE_ROWS
    n_p_blocks = n_p // COMBINE_ROWS
    dest_blocks = dest_b.reshape(n_blk, 1, TOP_K * COMBINE_ROWS)
    return pl.pallas_call(
        functools.partial(_combine_body, n_p_blocks=n_p_blocks),
        grid=(n_blk,),
        in_specs=[
            pl.BlockSpec((None, 1, COMBINE_ROWS * TOP_K), lambda i: (i, 0, 0), memory_space=pltpu.SMEM),
            pl.BlockSpec((None, 1, COMBINE_ROWS * TOP_K), lambda i: (jnp.minimum(i + 1, n_blk - 1), 0, 0),
                         memory_space=pltpu.SMEM),
            pl.BlockSpec((COMBINE_ROWS, LANE), lambda i: (i, 0)),
            pl.BlockSpec(memory_space=pl.ANY),
            pl.BlockSpec(memory_space=pl.ANY),
            pl.BlockSpec((COMBINE_ROWS, D_MODEL), lambda i: (i, 0)),
            pl.BlockSpec((1, D_MODEL), lambda i: (0, 0)),
        ],
        out_specs=_group_specs(COMBINE_ROWS, D_MODEL, n_p_blocks),
        out_shape=[jax.ShapeDtypeStruct((n_p, D_MODEL), F32), jax.ShapeDtypeStruct((n - n_p, D_MODEL), F32)],
        scratch_shapes=[pltpu.VMEM((2, TOP_K * COMBINE_ROWS * TOK_TILES, LANE), F32),
                        pltpu.SemaphoreType.DMA((2,))],
        compiler_params=pltpu.CompilerParams(dimension_semantics=("arbitrary",),
                                             vmem_limit_bytes=VMEM_LIMIT),
        name="combine",
    )(dest_blocks, dest_blocks, rt, y_2d.reshape(-1, TOK_TILES, LANE), y_2d, x1, g)


def _plan_body(rtt_ref, cnt_ref, dk_ref, db_ref, meta_ref, pst):
    i = pl.program_id(0)
    sh = _log2(EXPERT_ROWS)
    n_e = N_EXPERTS

    @pl.when(i == 0)
    def _():
        cnt = cnt_ref[...].astype(jnp.int32)
        padded = (((cnt + (EXPERT_ROWS - 1)) >> sh) << sh).astype(F32)
        e_r = lax.broadcasted_iota(jnp.int32, (n_e, n_e), 0)
        e_c = lax.broadcasted_iota(jnp.int32, (n_e, n_e), 1)
        p_t = jnp.broadcast_to(padded, (n_e, n_e)).T
        pend = jnp.sum(jnp.where(e_c <= e_r, p_t, 0.0), axis=1, keepdims=True)
        pst[...] = pend - padded
        has_rows = p_t > 0.0
        group = jnp.sum(jnp.where((e_c <= e_r) & has_rows, 1.0, 0.0), axis=1, keepdims=True) - 1.0
        nxt = jnp.min(jnp.where((e_c > e_r) & has_rows, e_c, n_e), axis=1, keepdims=True)
        nxt = jnp.where(nxt >= n_e, -1, nxt)

        mb = meta_ref.shape[1]
        blk = lax.broadcasted_iota(jnp.int32, (n_e, mb), 1)
        eb = lax.broadcasted_iota(jnp.int32, (n_e, mb), 0)
        first_row = (blk * EXPERT_ROWS).astype(F32)

        def expert_of(row0):
            return jnp.minimum(jnp.sum(jnp.where(pend <= row0, 1, 0), axis=0, keepdims=True), n_e - 1)

        be = expert_of(first_row)
        be_prev = expert_of(first_row - EXPERT_ROWS)
        hit = eb == be
        wslot = jnp.sum(jnp.where(hit, group, 0.0), axis=0, keepdims=True).astype(jnp.int32) & 1
        nx = jnp.sum(jnp.where(hit, nxt, 0), axis=0, keepdims=True)
        n_used = pend[n_e - 1:n_e, :].astype(jnp.int32) >> sh
        lane = lax.broadcasted_iota(jnp.int32, (1, mb), 1)
        first = (((be != be_prev) | (lane == 0)) & (lane < n_used)).astype(jnp.int32)
        cnt_b = jnp.sum(jnp.where(hit, cnt_ref[...], 0.0), axis=0, keepdims=True)
        pst_b = jnp.sum(jnp.where(hit, pend - padded, 0.0), axis=0, keepdims=True)
        valid = jnp.clip(cnt_b - (first_row[0:1, :] - pst_b), 0.0, float(EXPERT_ROWS)).astype(jnp.int32)
        row8 = lax.broadcasted_iota(jnp.int32, (SUBLANE, mb), 0)
        meta = jnp.where(row8 == 0, be, jnp.where(row8 == 1, first, jnp.where(row8 == 2, wslot,
                         jnp.where(row8 == 3, nx, jnp.where(row8 == 4, n_used, valid)))))
        meta_ref[...] = meta

    tm = rtt_ref.shape[1]
    eid = lax.broadcasted_iota(jnp.int32, (n_e, tm), 0).astype(F32)
    row8 = lax.broadcasted_iota(jnp.int32, (SUBLANE, tm), 0)
    d8 = jnp.zeros((SUBLANE, tm), jnp.int32)
    for k in range(TOP_K):
        start = jnp.sum(jnp.where(eid == rtt_ref[k:k + 1, :], pst[...], 0.0), axis=0, keepdims=True)
        d8 = jnp.where(row8 == k, (start + rtt_ref[TOP_K + k:TOP_K + k + 1, :]).astype(jnp.int32), d8)
    dk_ref[...] = d8[:TOP_K]
    for b in range(tm // COMBINE_ROWS):
        db_ref[b] = d8[:TOP_K, b * COMBINE_ROWS:(b + 1) * COMBINE_ROWS]


def _plan(rtt, cnt):
    n = rtt.shape[1]
    n_rows = n * TOP_K + N_EXPERTS * EXPERT_ROWS
    n_blocks = n_rows // EXPERT_ROWS
    mb = -(-n_blocks // LANE) * LANE
    tile = max(t for t in range(COMBINE_ROWS, PLAN_TILE_MAX + 1, COMBINE_ROWS) if n % t == 0)
    dk, db, meta = pl.pallas_call(
        _plan_body,
        grid=(n // tile,),
        in_specs=[pl.BlockSpec((2 * TOP_K, tile), lambda i: (0, i)),
                  pl.BlockSpec((N_EXPERTS, 1), lambda i: (0, 0))],
        out_specs=[pl.BlockSpec((TOP_K, tile), lambda i: (0, i)),
                   pl.BlockSpec((tile // COMBINE_ROWS, TOP_K, COMBINE_ROWS), lambda i: (i, 0, 0)),
                   pl.BlockSpec((SUBLANE, mb), lambda i: (0, 0))],
        out_shape=[jax.ShapeDtypeStruct((TOP_K, n), jnp.int32),
                   jax.ShapeDtypeStruct((n // COMBINE_ROWS, TOP_K, COMBINE_ROWS), jnp.int32),
                   jax.ShapeDtypeStruct((SUBLANE, mb), jnp.int32)],
        scratch_shapes=[pltpu.VMEM((N_EXPERTS, 1), F32)],
        compiler_params=pltpu.CompilerParams(dimension_semantics=("arbitrary",)),
        name="plan",
    )(rtt, cnt)
    block_meta = (meta[0, :n_blocks], meta[4, 0:1], meta[1, :n_blocks], meta[2, :n_blocks], meta[3, :n_blocks],
                  meta[5, :n_blocks])
    return dk, db, n_rows, block_meta


def _pad_lanes(v, width):
    return jnp.zeros((1, width), F32).at[0, :v.shape[0]].set(v.astype(F32))


def kernel(x_prompt, x_sample, state_gdn_conv, state_gdn, state_gla, rms_mix_w, w_in, conv_w, gdn_a_log,
           gdn_dt_bias, gdn_norm_w, gla_gk_w, gla_gk_b, gla_norm_w, w_out, rms_ffn_w, w_router, b_router,
           w_up, b_up, w_down, b_down, rms_final_w):
    bp, tp, d = x_prompt.shape
    bs, ts, _ = x_sample.shape
    n_p, n_s = bp * tp, bs * ts
    assert d == D_MODEL and state_gdn.shape[0] == 1, "single-layer kernel"
    assert tp >= CONV_WIDTH - 1 and ts >= CONV_WIDTH - 1, "new conv state is taken from the new tokens only"
    l = 0

    wi = w_in[l]
    a0 = GDN_CONV_CH + GDN_V_W
    g0 = a0 + 2 * GDN_HEADS
    lr0 = g0 + 2 * GLA_QK_W + 2 * GLA_V_W
    small = jnp.concatenate([wi[:, a0:a0 + 2 * GDN_HEADS], wi[:, lr0:lr0 + GLA_GATE_RANK],
                             jnp.zeros((d, SM_W - 2 * GDN_HEADS - GLA_GATE_RANK), F32)], axis=1)
    w_big = jnp.concatenate([wi[:, :a0], wi[:, g0:lr0], small], axis=1).astype(BF16)
    alog = _pad_lanes(gdn_a_log[l], SM_W)
    dtb = _pad_lanes(gdn_dt_bias[l], SM_W)
    wgk = jnp.zeros((SM_W, GLA_QK_W), F32).at[SM_LR:SM_LR + GLA_GATE_RANK].set(gla_gk_w[l])
    wr = jnp.zeros((d, LANE), F32).at[:, :N_EXPERTS].set(w_router[l])
    br = jnp.full((1, LANE), -1e30, F32).at[0, :N_EXPERTS].set(b_router[l])

    assert n_p % ROW_TILE == 0 and n_s % ROW_TILE == 0
    x_p, x_s = x_prompt.reshape(n_p, d), x_sample.reshape(n_s, d)
    proj = _inproj(x_p, x_s, rms_mix_w[l][None, :], w_big)

    tb_p = PROMPT_TIME_BLOCK
    zeros_conv = jnp.zeros((bp, CONV_WIDTH - 1, GDN_CONV_CH), F32)
    og_p, gdn_p, conv_p = _gdn(proj, bp, 1, tp, tb_p, CHUNK, CHUNK, zeros_conv,
                               jnp.zeros((bp, GDN_HEADS, GDN_DK, GDN_DV), F32), conv_w[l], alog, dtb,
                               gdn_norm_w[l][None, :])
    ol_p, gla_p = _gla(proj, bp, 1, tp, tb_p, CHUNK, CHUNK, jnp.zeros((bp, GLA_HEADS, GLA_DK, GLA_DV), F32),
                       wgk, gla_gk_b[l][None, :], gla_norm_w[l][None, :])

    ts_pad = SUBLANE
    nb_s = SAMPLE_SEQS_PER_STEP
    proj_s = proj[n_p:].reshape(bs, ts, PROJ_W)
    proj_sp = jnp.pad(proj_s, ((0, 0), (0, ts_pad - ts), (0, 0))).reshape(bs * ts_pad, PROJ_W)
    og_s, gdn_s, conv_s = _gdn(proj_sp, bs, nb_s, ts_pad, ts_pad, ts_pad, ts, state_gdn_conv[l], state_gdn[l],
                               conv_w[l], alog, dtb, gdn_norm_w[l][None, :])
    ol_s, gla_s = _gla(proj_sp, bs, nb_s, ts_pad, ts_pad, ts_pad, ts, state_gla[l], wgk, gla_gk_b[l][None, :],
                       gla_norm_w[l][None, :])
    og_s = og_s.reshape(bs, ts_pad, GDN_V_W)[:, :ts].reshape(n_s, GDN_V_W)
    ol_s = ol_s.reshape(bs, ts_pad, GLA_V_W)[:, :ts].reshape(n_s, GLA_V_W)

    x1, h2, rt, rtt, cnt = _outproj(og_p, og_s, ol_p, ol_s, x_p, x_s, w_out[l].astype(BF16),
                                    rms_ffn_w[l][None, :], wr, br)

    dest_k, dest_b, n_rows, block_meta = _plan(rtt, cnt)
    xs = _dispatch(h2.reshape(-1, PACK_TILES, LANE), dest_k.reshape(-1), n_rows)
    y_rows = _experts(block_meta, xs.reshape(-1, LANE), w_up[l], b_up[l], w_down[l], b_down[l])
    y_p, y_s = _combine(dest_b, rt, y_rows, x1, rms_final_w[None, :], n_p)
    y_prompt = y_p.reshape(bp, tp, d)
    y_sample = y_s.reshape(bs, ts, d)
    return (y_prompt, y_sample, conv_p[None], gdn_p[None], gla_p[None], conv_s[None], gdn_s[None], gla_s[None])
```

```python
import functools

import jax
import jax.numpy as jnp
from jax import lax
from jax.experimental import pallas as pl
from jax.experimental.pallas import tpu as pltpu
from jax.experimental.pallas import tpu_sc as plsc

F32 = jnp.float32
BF16 = jnp.bfloat16
HI = lax.Precision.HIGHEST

D_MODEL = 1024
GDN_HEADS = 4
GDN_DK = 128
GDN_DV = 128
GLA_HEADS = 4
GLA_DK = 64
GLA_DV = 128
GLA_GATE_RANK = 16
GLA_GATE_NORMALIZER = 16.0
CONV_WIDTH = 4
CHUNK = 64
N_EXPERTS = 32
TOP_K = 4
D_FF = 1024
SWIGLU_LIMIT = 7.0
SWIGLU_ALPHA = 1.702
RMS_EPS = 1e-6
L2_EPS = 1e-6

GDN_QK_W = GDN_HEADS * GDN_DK
GDN_V_W = GDN_HEADS * GDN_DV
GDN_CONV_CH = 2 * GDN_QK_W + GDN_V_W
GLA_QK_W = GLA_HEADS * GLA_DK
GLA_V_W = GLA_HEADS * GLA_DV

COL_QKV = 0
COL_Z = 1536
COL_GQ = 2048
COL_GK = 2304
COL_GV = 2560
COL_GG = 3072
COL_SM = 3584
SM_W = 128
PROJ_W = COL_SM + SM_W
SM_A, SM_B, SM_LR = 0, 4, 8

LANE = 128
SUBLANE = 8
TOK_TILES = D_MODEL // LANE
PACK_TILES = TOK_TILES // 2
ROW_TILE = 512
EXPERT_ROWS = 512
EXPERT_ROW_STEP = 128
EXPERT_WEIGHT_QUEUE = 1
WEIGHT_CAST_ROWS = 128
COMBINE_ROWS = 256
PLAN_TILE_MAX = 2048
DMA_ISSUE_UNROLL = 8
COMBINE_GATHER_QUEUES = (0, 1)
CONV_ROW_SLAB = 128
GDN_CHUNKS_PER_TRIP = 8
GLA_CHUNKS_PER_TRIP = 4
SCAN_CHUNKS_PER_TRIP = 8
PROMPT_TIME_BLOCK = 512
SAMPLE_SEQS_PER_STEP = 16
VMEM_LIMIT = 56 * 1024 * 1024


def _dot(a, b):
    return jnp.dot(a.astype(BF16), b.astype(BF16), preferred_element_type=F32)


def _dot_nt(a, b):
    return lax.dot_general(a.astype(BF16), b.astype(BF16), (((1,), (1,)), ((), ())),
                           preferred_element_type=F32)


def _dot_tn(a, b):
    return lax.dot_general(a.astype(BF16), b.astype(BF16), (((0,), (0,)), ((), ())),
                           preferred_element_type=F32)


def _dot_hi(a, b):
    return jnp.dot(a, b, precision=HI, preferred_element_type=F32)


def _dot_3pass(a, b):
    a_hi = a.astype(BF16)
    b_hi = b.astype(BF16)
    a_lo = (a - a_hi.astype(F32)).astype(BF16)
    b_lo = (b - b_hi.astype(F32)).astype(BF16)

    def mm(x, y):
        return jnp.dot(x, y, preferred_element_type=F32)

    return (mm(a_lo, b_hi) + mm(a_hi, b_lo)) + mm(a_hi, b_hi)


def _rms(x, w):
    return x * lax.rsqrt(jnp.mean(x * x, axis=-1, keepdims=True) + RMS_EPS) * w


def _silu(x):
    return x * jax.nn.sigmoid(x)


def _group_specs(rows, width, n_p_blocks):
    return [pl.BlockSpec((rows, width), lambda i: (jnp.minimum(i, n_p_blocks - 1), 0)),
            pl.BlockSpec((rows, width), lambda i: (jnp.maximum(i - n_p_blocks, 0), 0))]


def _group_pick(i, n_p_blocks, p_ref, s_ref):
    return jnp.where(i < n_p_blocks, p_ref[...], s_ref[...])


def _inproj_body(xp_ref, xs_ref, g_ref, w_ref, o_ref, *, n_p_blocks):
    x = _group_pick(pl.program_id(0), n_p_blocks, xp_ref, xs_ref)
    h = _rms(x, g_ref[...])
    o_ref[...] = jnp.dot(h.astype(BF16), w_ref[...], preferred_element_type=F32)


def _inproj(x_p, x_s, g, w):
    n_p_blocks, n_s_blocks = x_p.shape[0] // ROW_TILE, x_s.shape[0] // ROW_TILE
    n = x_p.shape[0] + x_s.shape[0]
    return pl.pallas_call(
        functools.partial(_inproj_body, n_p_blocks=n_p_blocks),
        grid=(n_p_blocks + n_s_blocks,),
        in_specs=_group_specs(ROW_TILE, D_MODEL, n_p_blocks) + [
            pl.BlockSpec((1, D_MODEL), lambda i: (0, 0)),
            pl.BlockSpec((D_MODEL, PROJ_W), lambda i: (0, 0)),
        ],
        out_specs=pl.BlockSpec((ROW_TILE, PROJ_W), lambda i: (i, 0)),
        out_shape=jax.ShapeDtypeStruct((n, PROJ_W), F32),
        compiler_params=pltpu.CompilerParams(dimension_semantics=("arbitrary",),
                                             vmem_limit_bytes=VMEM_LIMIT),
        name="in_proj",
    )(x_p, x_s, g, w)


def _log2(n):
    assert n & (n - 1) == 0
    return n.bit_length() - 1


def _tri_inv_all(ms, c, ii, jj):
    eye = (ii == jj).astype(F32)
    base = min(c, 8)
    sh = _log2(base)
    blk = (ii >> sh) == (jj >> sh)
    ns = [jnp.where(blk, m, 0.0) for m in ms]
    xs = [eye - n for n in ns]
    ps = [_dot(n, n) for n in ns]
    ts = [_dot(jnp.concatenate([x, p], axis=0), p) for x, p in zip(xs, ps)]
    xs = [x + t[:c] for x, t in zip(xs, ts)]
    ps = [t[c:] for t in ts]
    xs = [x + _dot(x, p) for x, p in zip(xs, ps)]
    s = base
    while s < c:
        sh_s, sh_b = _log2(s), _log2(2 * s)
        off = ((ii >> sh_b) == (jj >> sh_b)) & ((ii >> sh_s) != (jj >> sh_s))
        ys = [_dot(x, jnp.where(off, m, 0.0)) for x, m in zip(xs, ms)]
        xs = [x - _dot(y, x) for x, y in zip(xs, ys)]
        s *= 2
    return xs


def _gated_norm(o, w, z):
    return o * lax.rsqrt(jnp.mean(o * o, axis=-1, keepdims=True) + RMS_EPS) * w * _silu(z)


def _chunk_rows(s, tb_rows, ci, c):
    r = s * tb_rows + ci * c
    if not isinstance(r, int):
        r = pl.multiple_of(r, c)
    return r


def _for_chunks(n_chunks, step):
    if n_chunks == 1:
        step(0, 0)
    else:
        lax.fori_loop(0, n_chunks, step, 0)


def _gdn_body(qkv_ref, z_ref, sm_ref, cbuf_ref, s0_ref, cw_ref, alog_ref, dtb_ref, nw_ref,
              o_ref, sout_ref, cout_ref, st, xc, act, gcs, us, wss, qgs, kds, aqs,
              *, nb, tb_rows, chunk, valid, n_tb):
    tb = pl.program_id(1)
    c = chunk
    n_heads = GDN_HEADS
    tail = CONV_WIDTH - 1
    pad = SUBLANE
    units = [(s, h) for s in range(nb) for h in range(n_heads)]

    n_slabs = GDN_CONV_CH // LANE

    def lanes(j):
        return slice(j * LANE, (j + 1) * LANE)

    @pl.when(tb == 0)
    def _():
        st[...] = s0_ref[...]
        for s in range(nb):
            for j in range(n_slabs):
                xc[s, j, pad - tail:pad, :] = cbuf_ref[s, :, lanes(j)]

    if n_tb > 1:
        @pl.when(tb > 0)
        def _():
            for s in range(nb):
                for j in range(n_slabs):
                    xc[s, j, pad - tail:pad, :] = xc[s, j, tb_rows + pad - tail:tb_rows + pad, :]

    for s in range(nb):
        for j in range(n_slabs):
            xc[s, j, pad:pad + tb_rows, :] = qkv_ref[s * tb_rows:(s + 1) * tb_rows, lanes(j)]

    row_slab = min(tb_rows, CONV_ROW_SLAB)
    parities = 2 if row_slab >= 2 * SUBLANE else 1
    for s in range(nb):
        for j in range(n_slabs):
            src, dst = xc.at[s, j], act.at[j]
            for sl in range(tb_rows // row_slab):
                for p in range(parities):
                    lo = pad - tail + sl * row_slab + p
                    out0 = s * tb_rows + sl * row_slab + p

                    def rows_from(start):
                        if parities == 1:
                            return pl.ds(start, row_slab)
                        return pl.ds(start, row_slab // 2, stride=2)

                    acc = src[rows_from(lo), :] * cw_ref[0:1, lanes(j)]
                    for i in range(1, CONV_WIDTH):
                        acc = acc + src[rows_from(lo + i), :] * cw_ref[i:i + 1, lanes(j)]
                    dst[rows_from(out0), :] = _silu(acc)

    ii = lax.broadcasted_iota(jnp.int32, (c, c), 0)
    jj = lax.broadcasted_iota(jnp.int32, (c, c), 1)
    lower = (ii >= jj)
    lower_f = lower.astype(F32)
    strict = (ii > jj)
    rowmask = None
    if valid < c:
        rowmask = lax.broadcasted_iota(jnp.int32, (c, 1), 0) < valid

    def hs(h, w):
        return slice(h * w, (h + 1) * w)

    n_chunks = tb_rows // c
    cpi = next(k for k in (GDN_CHUNKS_PER_TRIP, 2, 1) if n_chunks % k == 0)
    p1_units = [(g, h) for g in range(nb * cpi) for h in range(n_heads)]

    def phase1(ci, carry):
        rows, b_ts, gc_ts, gc_tts = [], [], [], []
        for g in range(nb * cpi):
            rr = pl.ds(_chunk_rows(g // cpi, tb_rows, ci * cpi + g % cpi, c), c)
            sm = sm_ref[rr, :]
            g_t = -jnp.exp(alog_ref[...]) * jax.nn.softplus(sm + dtb_ref[...])
            b_t = jax.nn.sigmoid(sm)
            if rowmask is not None:
                g_t = jnp.where(rowmask, g_t, 0.0)
                b_t = jnp.where(rowmask, b_t, 0.0)
            gc_t = _dot_hi(lower_f, g_t)
            gcs[rr, :] = gc_t
            rows.append(rr)
            b_ts.append(b_t)
            gc_ts.append(gc_t)
            gc_tts.append(gc_t.T)
        qn, kn, kb, vb = {}, {}, {}, {}
        for (s, h) in p1_units:
            q = act[h, rows[s], :]
            k = act[n_heads + h, rows[s], :]
            v = act[2 * n_heads + h, rows[s], :]
            if rowmask is not None:
                q = jnp.where(rowmask, q, 0.0)
                k = jnp.where(rowmask, k, 0.0)
                v = jnp.where(rowmask, v, 0.0)
            qn[s, h] = q * lax.rsqrt(jnp.sum(q * q, axis=-1, keepdims=True) + L2_EPS) * (GDN_DK ** -0.5)
            kn[s, h] = k * lax.rsqrt(jnp.sum(k * k, axis=-1, keepdims=True) + L2_EPS)
            beta = b_ts[s][:, SM_B + h:SM_B + h + 1]
            kb[s, h] = kn[s, h] * beta
            vb[s, h] = v * beta
        s1 = {u: _dot_nt(jnp.concatenate([kb[u], qn[u]], axis=0), kn[u]) for u in p1_units}
        mm = []
        for (s, h) in p1_units:
            gcol = gc_ts[s][:, SM_A + h:SM_A + h + 1]
            grow = gc_tts[s][SM_A + h:SM_A + h + 1, :]
            dec = jnp.exp(jnp.where(lower, gcol - grow, -jnp.inf))
            mm.append(jnp.where(strict, s1[s, h][:c] * dec, 0.0))
            aqs[h, rows[s], :] = s1[s, h][c:] * dec
        tms = _tri_inv_all(mm, c, ii, jj)
        for (s, h), tm in zip(p1_units, tms):
            gcol = gc_ts[s][:, SM_A + h:SM_A + h + 1]
            eg = jnp.exp(gcol)
            uw = _dot(tm, jnp.concatenate([vb[s, h], kb[s, h] * eg], axis=1))
            us[rows[s], hs(h, GDN_DV)] = uw[:, :GDN_DV]
            wss[rows[s], hs(h, GDN_DV)] = uw[:, GDN_DV:]
            qgs[rows[s], hs(h, GDN_DK)] = qn[s, h] * eg
            kds[rows[s], hs(h, GDN_DK)] = kn[s, h] * jnp.exp(gcol[c - 1:c, :] - gcol)
        return carry

    cp2 = next(k for k in (SCAN_CHUNKS_PER_TRIP, 1) if n_chunks % k == 0)

    def phase2(ti, carry):
        for u in range(cp2):
            scan_chunk(ti * cp2 + u)
        return carry

    def scan_chunk(ci):
        r0 = [_chunk_rows(s, tb_rows, ci, c) for s in range(nb)]
        rows = [pl.ds(r, c) for r in r0]
        ws = {(s, h): _dot(jnp.concatenate([wss[rows[s], hs(h, GDN_DV)], qgs[rows[s], hs(h, GDN_DK)]], axis=0),
                           st[s, h]) for (s, h) in units}
        v_new = {(s, h): us[rows[s], hs(h, GDN_DV)] - ws[s, h][:c] for (s, h) in units}
        o = {(s, h): ws[s, h][c:] + _dot(aqs[h, rows[s], :], v_new[s, h]) for (s, h) in units}
        upd = {(s, h): _dot_tn(kds[rows[s], hs(h, GDN_DK)], v_new[s, h]) for (s, h) in units}
        for (s, h) in units:
            g_last = gcs[pl.ds(r0[s] + c - 1, 1), SM_A + h:SM_A + h + 1]
            st[s, h] = st[s, h] * jnp.exp(g_last) + upd[s, h]
        for s in range(nb):
            o_ref[rows[s], :] = jnp.concatenate(
                [_gated_norm(o[s, h], nw_ref[...], z_ref[rows[s], hs(h, GDN_DV)]) for h in range(n_heads)], axis=1)

    _for_chunks(n_chunks // cpi, phase1)
    _for_chunks(n_chunks // cp2, phase2)

    @pl.when(tb == n_tb - 1)
    def _():
        sout_ref[...] = st[...]
        last = tb_rows if valid == c else valid
        for s in range(nb):
            cout_ref[s] = jnp.concatenate(
                [xc[s, j, pad + last - tail:pad + last, :] for j in range(n_slabs)], axis=1)


def _gdn(proj, n_seq, nb, t_len, tb_rows, chunk, valid, conv_buf, s0, conv_w, alog, dtb, nw):
    n_tb = t_len // tb_rows
    assert nb == 1 or n_tb == 1
    rows = nb * tb_rows

    def rowblk(b, t):
        return b * n_tb + t

    body = functools.partial(_gdn_body, nb=nb, tb_rows=tb_rows, chunk=chunk, valid=valid, n_tb=n_tb)
    return pl.pallas_call(
        body,
        grid=(n_seq // nb, n_tb),
        in_specs=[
            pl.BlockSpec((rows, GDN_CONV_CH), lambda b, t: (rowblk(b, t), COL_QKV // GDN_CONV_CH)),
            pl.BlockSpec((rows, GDN_V_W), lambda b, t: (rowblk(b, t), COL_Z // GDN_V_W)),
            pl.BlockSpec((rows, SM_W), lambda b, t: (rowblk(b, t), COL_SM // SM_W)),
            pl.BlockSpec((nb, CONV_WIDTH - 1, GDN_CONV_CH), lambda b, t: (b, 0, 0)),
            pl.BlockSpec((nb, GDN_HEADS, GDN_DK, GDN_DV), lambda b, t: (b, 0, 0, 0)),
            pl.BlockSpec((CONV_WIDTH, GDN_CONV_CH), lambda b, t: (0, 0)),
            pl.BlockSpec((1, SM_W), lambda b, t: (0, 0)),
            pl.BlockSpec((1, SM_W), lambda b, t: (0, 0)),
            pl.BlockSpec((1, GDN_DV), lambda b, t: (0, 0)),
        ],
        out_specs=[
            pl.BlockSpec((rows, GDN_V_W), lambda b, t: (rowblk(b, t), 0)),
            pl.BlockSpec((nb, GDN_HEADS, GDN_DK, GDN_DV), lambda b, t: (b, 0, 0, 0)),
            pl.BlockSpec((nb, CONV_WIDTH - 1, GDN_CONV_CH), lambda b, t: (b, 0, 0)),
        ],
        out_shape=[
            jax.ShapeDtypeStruct((n_seq * t_len, GDN_V_W), F32),
            jax.ShapeDtypeStruct((n_seq, GDN_HEADS, GDN_DK, GDN_DV), F32),
            jax.ShapeDtypeStruct((n_seq, CONV_WIDTH - 1, GDN_CONV_CH), F32),
        ],
        scratch_shapes=[
            pltpu.VMEM((nb, GDN_HEADS, GDN_DK, GDN_DV), F32),
            pltpu.VMEM((nb, GDN_CONV_CH // LANE, tb_rows + SUBLANE, LANE), F32),
            pltpu.VMEM((GDN_CONV_CH // LANE, rows, LANE), F32),
            pltpu.VMEM((rows, SM_W), F32),
            pltpu.VMEM((rows, GDN_V_W), F32),
            pltpu.VMEM((rows, GDN_V_W), F32),
            pltpu.VMEM((rows, GDN_QK_W), F32),
            pltpu.VMEM((rows, GDN_QK_W), F32),
            pltpu.VMEM((GDN_HEADS, rows, chunk), F32),
        ],
        compiler_params=pltpu.CompilerParams(dimension_semantics=("arbitrary", "arbitrary"),
                                             vmem_limit_bytes=VMEM_LIMIT),
        name="gdn_mixer",
    )(proj, proj, proj, conv_buf, s0, conv_w, alog, dtb, nw)


def _gla_body(q_ref, k_ref, v_ref, go_ref, sm_ref, s0_ref, wgk_ref, bgk_ref, nw_ref,
              o_ref, sout_ref, st, qes, ois, upds, decs, *, nb, tb_rows, chunk, valid, n_tb):
    tb = pl.program_id(1)
    c = chunk
    n_heads = GLA_HEADS
    units = [(s, h) for s in range(nb) for h in range(n_heads)]

    @pl.when(tb == 0)
    def _():
        st[...] = s0_ref[...]

    ii = lax.broadcasted_iota(jnp.int32, (c, c), 0)
    jj = lax.broadcasted_iota(jnp.int32, (c, c), 1)
    lower = (ii >= jj)
    lower_f = lower.astype(F32)
    rid = lax.broadcasted_iota(jnp.int32, (c, 1), 0)
    rowmask = (rid < valid) if valid < c else None
    n_sub = max(c // 16, 1)
    sub = c // n_sub

    n_chunks = tb_rows // c
    cpi = next(k for k in (GLA_CHUNKS_PER_TRIP, 2, 1) if n_chunks % k == 0)
    p1_units = [(g, h) for g in range(nb * cpi) for h in range(n_heads)]

    def phase1(ci, carry):
        rows, slots, bcs, bc_ts = [], [], [], []
        for g in range(nb * cpi):
            chunk_idx = ci * cpi + g % cpi
            rr = pl.ds(_chunk_rows(g // cpi, tb_rows, chunk_idx, c), c)
            slots.append((g // cpi) * n_chunks + chunk_idx)
            gk = jax.nn.log_sigmoid(_dot(sm_ref[rr, :], wgk_ref[...]) + bgk_ref[...]) / GLA_GATE_NORMALIZER
            if rowmask is not None:
                gk = jnp.where(rowmask, gk, 0.0)
            bc = _dot_hi(lower_f, gk)
            rows.append(rr)
            bcs.append(bc)
            bc_ts.append(bc.T)
        q, k, v, bch = {}, {}, {}, {}
        for (s, h) in p1_units:
            ks = slice(h * GLA_DK, (h + 1) * GLA_DK)
            vs = slice(h * GLA_DV, (h + 1) * GLA_DV)
            q[s, h] = q_ref[rows[s], ks] * (GLA_DK ** -0.5)
            kk = k_ref[rows[s], ks]
            vv = v_ref[rows[s], vs]
            if rowmask is not None:
                kk = jnp.where(rowmask, kk, 0.0)
                vv = jnp.where(rowmask, vv, 0.0)
            k[s, h], v[s, h] = kk, vv
            bch[s, h] = bcs[s][:, ks]
        for (g, h) in p1_units:
            qes[h, rows[g], :] = q[g, h] * jnp.exp(bch[g, h])
        a = {}
        for u in p1_units:
            q_parts, k_parts = [], []
            for sb in range(n_sub):
                ref_row = bch[u][sb * sub:sb * sub + 1, :]
                in_blk = (rid >= sb * sub) & (rid < (sb + 1) * sub)
                q_parts.append(jnp.where(in_blk, q[u] * jnp.exp(jnp.where(in_blk, bch[u] - ref_row, 0.0)), 0.0))
                k_parts.append(k[u] * jnp.exp(jnp.where(rid < (sb + 1) * sub, ref_row - bch[u], 0.0)))
            q_hat = jnp.concatenate(q_parts, axis=1) if n_sub > 1 else q_parts[0]
            k_hat = jnp.concatenate(k_parts, axis=1) if n_sub > 1 else k_parts[0]
            a[u] = jnp.where(lower, _dot_nt(q_hat, k_hat), 0.0)
        upd = {u: _dot_tn(k[u] * jnp.exp(bch[u][c - 1:c, :] - bch[u]), v[u]) for u in p1_units}
        o_intra = {u: _dot(a[u], v[u]) for u in p1_units}
        for (g, h) in p1_units:
            dec_col = bc_ts[g][h * GLA_DK:(h + 1) * GLA_DK, c - 1:c]
            decs[slots[g], h] = jnp.broadcast_to(jnp.exp(dec_col), (GLA_DK, GLA_DV))
            upds[slots[g], h] = upd[g, h]
            ois[rows[g], h * GLA_DV:(h + 1) * GLA_DV] = o_intra[g, h]
        return carry

    cp2 = next(k for k in (SCAN_CHUNKS_PER_TRIP, 1) if n_chunks % k == 0)

    def phase2(ti, carry):
        for u in range(cp2):
            scan_chunk(ti * cp2 + u)
        return carry

    def scan_chunk(ci):
        rows = [pl.ds(_chunk_rows(s, tb_rows, ci, c), c) for s in range(nb)]
        o = {(s, h): ois[rows[s], h * GLA_DV:(h + 1) * GLA_DV] + _dot(qes[h, rows[s], :], st[s, h])
             for (s, h) in units}
        for (s, h) in units:
            st[s, h] = decs[s * n_chunks + ci, h] * st[s, h] + upds[s * n_chunks + ci, h]
        for s in range(nb):
            o_ref[rows[s], :] = jnp.concatenate(
                [_gated_norm(o[s, h], nw_ref[...], go_ref[rows[s], h * GLA_DV:(h + 1) * GLA_DV])
                 for h in range(n_heads)], axis=1)

    _for_chunks(n_chunks // cpi, phase1)
    _for_chunks(n_chunks // cp2, phase2)

    @pl.when(tb == n_tb - 1)
    def _():
        sout_ref[...] = st[...]


def _gla(proj, n_seq, nb, t_len, tb_rows, chunk, valid, s0, wgk, bgk, nw):
    n_tb = t_len // tb_rows
    assert nb == 1 or n_tb == 1
    rows = nb * tb_rows

    def rowblk(b, t):
        return b * n_tb + t

    body = functools.partial(_gla_body, nb=nb, tb_rows=tb_rows, chunk=chunk, valid=valid, n_tb=n_tb)
    return pl.pallas_call(
        body,
        grid=(n_seq // nb, n_tb),
        in_specs=[
            pl.BlockSpec((rows, GLA_QK_W), lambda b, t: (rowblk(b, t), COL_GQ // GLA_QK_W)),
            pl.BlockSpec((rows, GLA_QK_W), lambda b, t: (rowblk(b, t), COL_GK // GLA_QK_W)),
            pl.BlockSpec((rows, GLA_V_W), lambda b, t: (rowblk(b, t), COL_GV // GLA_V_W)),
            pl.BlockSpec((rows, GLA_V_W), lambda b, t: (rowblk(b, t), COL_GG // GLA_V_W)),
            pl.BlockSpec((rows, SM_W), lambda b, t: (rowblk(b, t), COL_SM // SM_W)),
            pl.BlockSpec((nb, GLA_HEADS, GLA_DK, GLA_DV), lambda b, t: (b, 0, 0, 0)),
            pl.BlockSpec((SM_W, GLA_QK_W), lambda b, t: (0, 0)),
            pl.BlockSpec((1, GLA_QK_W), lambda b, t: (0, 0)),
            pl.BlockSpec((1, GLA_DV), lambda b, t: (0, 0)),
        ],
        out_specs=[
            pl.BlockSpec((rows, GLA_V_W), lambda b, t: (rowblk(b, t), 0)),
            pl.BlockSpec((nb, GLA_HEADS, GLA_DK, GLA_DV), lambda b, t: (b, 0, 0, 0)),
        ],
        out_shape=[
            jax.ShapeDtypeStruct((n_seq * t_len, GLA_V_W), F32),
            jax.ShapeDtypeStruct((n_seq, GLA_HEADS, GLA_DK, GLA_DV), F32),
        ],
        scratch_shapes=[
            pltpu.VMEM((nb, GLA_HEADS, GLA_DK, GLA_DV), F32),
            pltpu.VMEM((GLA_HEADS, rows, GLA_DK), F32),
            pltpu.VMEM((rows, GLA_V_W), F32),
            pltpu.VMEM((rows // chunk, GLA_HEADS, GLA_DK, GLA_DV), F32),
            pltpu.VMEM((rows // chunk, GLA_HEADS, GLA_DK, GLA_DV), F32),
        ],
        compiler_params=pltpu.CompilerParams(dimension_semantics=("arbitrary", "arbitrary"),
                                             vmem_limit_bytes=VMEM_LIMIT),
        name="gla_mixer",
    )(proj, proj, proj, proj, proj, s0, wgk, bgk, nw)


def _outproj_body(ogp_ref, ogs_ref, olp_ref, ols_ref, xp_ref, xs_ref, wo_ref, g_ref, wr_ref, br_ref,
                  x1_ref, h2_ref, rt_ref, rtt_ref, cnt_ref, base, *, n_p_blocks):
    i = pl.program_id(0)

    @pl.when(i == 0)
    def _():
        base[...] = jnp.zeros_like(base)

    o = jnp.concatenate([_group_pick(i, n_p_blocks, ogp_ref, ogs_ref),
                         _group_pick(i, n_p_blocks, olp_ref, ols_ref)], axis=1)
    x1 = _group_pick(i, n_p_blocks, xp_ref, xs_ref) + jnp.dot(o.astype(BF16), wo_ref[...],
                                                               preferred_element_type=F32)
    x1_ref[...] = x1
    h = _rms(x1, g_ref[...])
    _store_token_tiles(h2_ref, _pack_bf16_pairs(h))
    logits = _dot_3pass(h, wr_ref[...]) + br_ref[...]

    tm = logits.shape[0]
    lt = logits.T[:N_EXPERTS]
    eid = lax.broadcasted_iota(jnp.int32, (N_EXPERTS, tm), 0)
    work = lt
    sel = jnp.zeros((N_EXPERTS, tm), F32)
    hits, ids, vals = [], [], []
    for _ in range(TOP_K):
        m = jnp.max(work, axis=0, keepdims=True)
        idx = jnp.min(jnp.where(work == m, eid, N_EXPERTS), axis=0, keepdims=True)
        hit = eid == idx
        hits.append(hit)
        ids.append(idx)
        vals.append(m)
        work = jnp.where(hit, -jnp.inf, work)
        sel = sel + hit.astype(F32)
    exps = [jnp.exp(v - vals[0]) for v in vals]
    den = exps[0]
    for e in exps[1:]:
        den = den + e
    gates = [e / den for e in exps]

    ri = lax.broadcasted_iota(jnp.int32, (tm, tm), 0)
    ci = lax.broadcasted_iota(jnp.int32, (tm, tm), 1)
    before = _dot(sel, (ri < ci).astype(F32)) + base[...]
    ranks = [jnp.sum(jnp.where(hit, before, 0.0), axis=0, keepdims=True) for hit in hits]
    base[...] = base[...] + jnp.sum(sel, axis=1, keepdims=True)
    cnt_ref[...] = base[...]

    row = lax.broadcasted_iota(jnp.int32, (LANE, tm), 0)
    rec = jnp.zeros((LANE, tm), F32)
    for k in range(TOP_K):
        rec = jnp.where(row == k, ids[k].astype(F32), rec)
        rec = jnp.where(row == TOP_K + k, ranks[k], rec)
        rec = jnp.where(row == 2 * TOP_K + k, gates[k], rec)
    rt_ref[...] = rec.T
    rtt_ref[...] = rec[:2 * TOP_K]


def _outproj(og_p, og_s, ol_p, ol_s, x_p, x_s, wo, g, wr, br):
    n_p_blocks, n_s_blocks = x_p.shape[0] // ROW_TILE, x_s.shape[0] // ROW_TILE
    n = x_p.shape[0] + x_s.shape[0]
    return pl.pallas_call(
        functools.partial(_outproj_body, n_p_blocks=n_p_blocks),
        grid=(n_p_blocks + n_s_blocks,),
        in_specs=_group_specs(ROW_TILE, GDN_V_W, n_p_blocks) + _group_specs(ROW_TILE, GLA_V_W, n_p_blocks)
        + _group_specs(ROW_TILE, D_MODEL, n_p_blocks) + [
            pl.BlockSpec((D_MODEL, D_MODEL), lambda i: (0, 0)),
            pl.BlockSpec((1, D_MODEL), lambda i: (0, 0)),
            pl.BlockSpec((D_MODEL, LANE), lambda i: (0, 0)),
            pl.BlockSpec((1, LANE), lambda i: (0, 0)),
        ],
        out_specs=[
            pl.BlockSpec((ROW_TILE, D_MODEL), lambda i: (i, 0)),
            pl.BlockSpec((ROW_TILE * PACK_TILES, LANE), lambda i: (i, 0)),
            pl.BlockSpec((ROW_TILE, LANE), lambda i: (i, 0)),
            pl.BlockSpec((2 * TOP_K, ROW_TILE), lambda i: (0, i)),
            pl.BlockSpec((N_EXPERTS, 1), lambda i: (0, 0)),
        ],
        out_shape=[
            jax.ShapeDtypeStruct((n, D_MODEL), F32),
            jax.ShapeDtypeStruct((n * PACK_TILES, LANE), jnp.uint32),
            jax.ShapeDtypeStruct((n, LANE), F32),
            jax.ShapeDtypeStruct((2 * TOP_K, n), F32),
            jax.ShapeDtypeStruct((N_EXPERTS, 1), F32),
        ],
        scratch_shapes=[pltpu.VMEM((N_EXPERTS, 1), F32)],
        compiler_params=pltpu.CompilerParams(dimension_semantics=("arbitrary",),
                                             vmem_limit_bytes=VMEM_LIMIT),
        name="out_proj",
    )(og_p, og_s, ol_p, ol_s, x_p, x_s, wo, g, wr, br)


def _store_token_tiles(ref2d, val):
    rows, tiles = val.shape[0], val.shape[1] // LANE
    for c in range(tiles):
        ref2d[pl.ds(c, rows, stride=tiles), :] = val[:, c * LANE:(c + 1) * LANE]


def _load_token_tiles(ref2d, first_row, rows, tiles=TOK_TILES):
    return jnp.concatenate(
        [ref2d[pl.ds(first_row * tiles + c, rows, stride=tiles), :] for c in range(tiles)], axis=1)


def _pack_bf16_pairs(x):
    half = x.shape[1] // 2
    bits = lax.bitcast_convert_type(x.astype(BF16).astype(F32), jnp.uint32)
    return (bits[:, :half] >> 16) | (bits[:, half:] & jnp.uint32(0xFFFF0000))


def _unpack_bf16_pairs(w):
    lo = lax.bitcast_convert_type(w << 16, F32)
    hi = lax.bitcast_convert_type(w & jnp.uint32(0xFFFF0000), F32)
    return jnp.concatenate([lo, hi], axis=1).astype(BF16)


def _expert_weight_copies(e, ws, wup_hbm, wdn_hbm, wup_buf, wdn_buf, wsems):
    return (pltpu.make_async_copy(wup_hbm.at[e], wup_buf.at[ws], wsems.at[ws]),
            pltpu.make_async_copy(wdn_hbm.at[e], wdn_buf.at[ws], wsems.at[ws]))


def _expert_body(be_ref, nu_ref, first_ref, wslot_ref, next_ref, valid_ref, x_ref,
                 wup_hbm, bup_ref, wdn_hbm, bdn_ref, y_ref, wup_buf, wdn_buf, wsems, wup_bf, wdn_bf):
    i = pl.program_id(0)
    n_used = nu_ref[0]
    ws = wslot_ref[i]
    weight_copies = functools.partial(_expert_weight_copies, wup_hbm=wup_hbm, wdn_hbm=wdn_hbm, wup_buf=wup_buf,
                                      wdn_buf=wdn_buf, wsems=wsems)

    @pl.when((i == 0) & (n_used > 0))
    def _():
        for cp in weight_copies(be_ref[0], ws):
            cp.start(priority=EXPERT_WEIGHT_QUEUE)

    @pl.when(i < n_used)
    def _():
        @pl.when(first_ref[i] == 1)
        def _():
            for cp in weight_copies(be_ref[i], ws):
                cp.wait()

            @pl.when(next_ref[i] >= 0)
            def _():
                for cp in weight_copies(next_ref[i], 1 - ws):
                    cp.start(priority=EXPERT_WEIGHT_QUEUE)

            for r in range(0, D_MODEL, WEIGHT_CAST_ROWS):
                wup_bf[r:r + WEIGHT_CAST_ROWS, :] = wup_buf[ws, r:r + WEIGHT_CAST_ROWS, :].astype(BF16)
            for r in range(0, D_FF, WEIGHT_CAST_ROWS):
                wdn_bf[r:r + WEIGHT_CAST_ROWS, :] = wdn_buf[ws, r:r + WEIGHT_CAST_ROWS, :].astype(BF16)

        def expert_rows(n):
            x = _unpack_bf16_pairs(_load_token_tiles(x_ref, 0, n, PACK_TILES))
            gu = _dot(x, wup_bf[...]) + bup_ref[...]
            gate = jnp.minimum(gu[:, :D_FF], SWIGLU_LIMIT)
            up = jnp.clip(gu[:, D_FF:], -SWIGLU_LIMIT, SWIGLU_LIMIT)
            a = (up + 1.0) * gate * jax.nn.sigmoid(SWIGLU_ALPHA * gate)
            _store_token_tiles(y_ref, _dot(a, wdn_bf[...]) + bdn_ref[...])

        valid = valid_ref[i]
        for n in range(EXPERT_ROW_STEP, EXPERT_ROWS + 1, EXPERT_ROW_STEP):
            @pl.when((valid <= n) if n == EXPERT_ROW_STEP else ((valid > n - EXPERT_ROW_STEP) & (valid <= n)))
            def _(n=n):
                expert_rows(n)
                if n < EXPERT_ROWS:
                    y_ref[n * TOK_TILES:, :] = jnp.zeros(((EXPERT_ROWS - n) * TOK_TILES, LANE), F32)

    @pl.when(i >= n_used)
    def _():
        y_ref[...] = jnp.zeros_like(y_ref)


def _experts(block_meta, xs_2d, w_up, b_up, w_down, b_down):
    n_blocks = block_meta[0].shape[0]
    grid_spec = pltpu.PrefetchScalarGridSpec(
        num_scalar_prefetch=len(block_meta),
        grid=(n_blocks,),
        in_specs=[
            pl.BlockSpec((EXPERT_ROWS * PACK_TILES, LANE), lambda i, *_: (i, 0)),
            pl.BlockSpec(memory_space=pl.ANY),
            pl.BlockSpec((None, 1, 2 * D_FF), lambda i, be, *_: (be[i], 0, 0)),
            pl.BlockSpec(memory_space=pl.ANY),
            pl.BlockSpec((None, 1, D_MODEL), lambda i, be, *_: (be[i], 0, 0)),
        ],
        out_specs=pl.BlockSpec((EXPERT_ROWS * TOK_TILES, LANE), lambda i, *_: (i, 0)),
        scratch_shapes=[
            pltpu.VMEM((2, D_MODEL, 2 * D_FF), F32),
            pltpu.VMEM((2, D_FF, D_MODEL), F32),
            pltpu.SemaphoreType.DMA((2,)),
            pltpu.VMEM((D_MODEL, 2 * D_FF), BF16),
            pltpu.VMEM((D_FF, D_MODEL), BF16),
        ],
    )
    return pl.pallas_call(
        _expert_body,
        grid_spec=grid_spec,
        out_shape=jax.ShapeDtypeStruct((n_blocks * EXPERT_ROWS * TOK_TILES, LANE), F32),
        compiler_params=pltpu.CompilerParams(dimension_semantics=("arbitrary",),
                                             vmem_limit_bytes=VMEM_LIMIT),
        name="experts",
    )(*block_meta, xs_2d, w_up, b_up.reshape(N_EXPERTS, 1, 2 * D_FF), w_down,
      b_down.reshape(N_EXPERTS, 1, D_MODEL))


def _dispatch(h_tiles, dest_kmajor, n_rows):
    n_tok = h_tiles.shape[0]
    info = plsc.get_sparse_core_info()
    n_workers = info.num_cores * info.num_subcores
    per_worker = n_tok // n_workers
    chunk = next(c for c in (128, 96, 88, 64, 48, 32, 16, 8) if per_worker % c == 0)
    assert n_tok % n_workers == 0 and per_worker % SUBLANE == 0
    mesh = plsc.VectorSubcoreMesh(core_axis_name="c", subcore_axis_name="s")

    @functools.partial(
        pl.kernel, mesh=mesh,
        out_type=jax.ShapeDtypeStruct((n_rows,) + h_tiles.shape[1:], h_tiles.dtype),
        scratch_types=[pltpu.VMEM((TOP_K, chunk), jnp.int32), pltpu.VMEM((chunk,) + h_tiles.shape[1:], h_tiles.dtype),
                       pltpu.SemaphoreType.DMA],
    )
    def dispatch(h_hbm, dest_hbm, out_hbm, idx_v, rows_v, sem):
        wid = lax.axis_index("s") * info.num_cores + lax.axis_index("c")

        def step(j, carry):
            t0 = pl.multiple_of(wid * per_worker + j * chunk, SUBLANE)
            loads = [pltpu.async_copy(h_hbm.at[pl.ds(t0, chunk)], rows_v, sem)]
            for k in range(TOP_K):
                loads.append(pltpu.async_copy(
                    dest_hbm.at[pl.ds(pl.multiple_of(k * n_tok + t0, SUBLANE), chunk)], idx_v.at[k], sem))
            for cp in loads:
                cp.wait()
            stores = [pltpu.async_copy(rows_v, out_hbm.at[idx_v.at[k]], sem) for k in range(TOP_K)]
            for cp in stores:
                cp.wait()
            return carry

        lax.fori_loop(0, per_worker // chunk, step, 0)

    return dispatch(h_tiles, dest_kmajor)


def _gather_rows(src_tiles, idx_ref, n_rows, dst2d, sem, priorities):
    def issue(j, carry):
        for u in range(DMA_ISSUE_UNROLL):
            r = j * DMA_ISSUE_UNROLL + u
            dst = dst2d.at[pl.ds(pl.multiple_of(r * TOK_TILES, TOK_TILES), TOK_TILES), :]
            pltpu.make_async_copy(src_tiles.at[idx_ref[0, r]], dst, sem).start(
                priority=priorities[u % len(priorities)])
        return carry

    lax.fori_loop(0, n_rows // DMA_ISSUE_UNROLL, issue, 0)


def _wait_rows(src2d, n_rows, dst2d, sem):
    pltpu.make_async_copy(src2d.at[pl.ds(0, n_rows * TOK_TILES), :], dst2d, sem).wait()


def _combine_body(dest_ref, dest_next_ref, rt_ref, y_tiles, y_2d, x1_ref, g_ref, op_ref, os_ref, ybuf, sems,
                  *, n_p_blocks):
    i = pl.program_id(0)
    slot = i % 2
    n_rows = TOP_K * COMBINE_ROWS

    @pl.when(i == 0)
    def _():
        _gather_rows(y_tiles, dest_ref, n_rows, ybuf.at[0], sems.at[0], COMBINE_GATHER_QUEUES)

    _wait_rows(y_2d, n_rows, ybuf.at[slot], sems.at[slot])

    @pl.when(i + 1 < pl.num_programs(0))
    def _():
        _gather_rows(y_tiles, dest_next_ref, n_rows, ybuf.at[1 - slot], sems.at[1 - slot], COMBINE_GATHER_QUEUES)

    buf = ybuf.at[slot]
    moe = _load_token_tiles(buf, 0, COMBINE_ROWS) * rt_ref[:, 2 * TOP_K:2 * TOP_K + 1]
    for k in range(1, TOP_K):
        moe = moe + _load_token_tiles(buf, k * COMBINE_ROWS, COMBINE_ROWS) * rt_ref[:, 2 * TOP_K + k:2 * TOP_K + k + 1]
    res = _rms(x1_ref[...] + moe, g_ref[...])

    @pl.when(i < n_p_blocks)
    def _():
        op_ref[...] = res

    @pl.when(i >= n_p_blocks)
    def _():
        os_ref[...] = res


def _combine(dest_b, rt, y_2d, x1, g, n_p):
    n = x1.shape[0]
    n_blk = n // COMBINE_ROWS
    n_p_blocks = n_p // COMBINE_ROWS
    dest_blocks = dest_b.reshape(n_blk, 1, TOP_K * COMBINE_ROWS)
    return pl.pallas_call(
        functools.partial(_combine_body, n_p_blocks=n_p_blocks),
        grid=(n_blk,),
        in_specs=[
            pl.BlockSpec((None, 1, COMBINE_ROWS * TOP_K), lambda i: (i, 0, 0), memory_space=pltpu.SMEM),
            pl.BlockSpec((None, 1, COMBINE_ROWS * TOP_K), lambda i: (jnp.minimum(i + 1, n_blk - 1), 0, 0),
                         memory_space=pltpu.SMEM),
            pl.BlockSpec((COMBINE_ROWS, LANE), lambda i: (i, 0)),
            pl.BlockSpec(memory_space=pl.ANY),
            pl.BlockSpec(memory_space=pl.ANY),
            pl.BlockSpec((COMBINE_ROWS, D_MODEL), lambda i: (i, 0)),
            pl.BlockSpec((1, D_MODEL), lambda i: (0, 0)),
        ],
        out_specs=_group_specs(COMBINE_ROWS, D_MODEL, n_p_blocks),
        out_shape=[jax.ShapeDtypeStruct((n_p, D_MODEL), F32), jax.ShapeDtypeStruct((n - n_p, D_MODEL), F32)],
        scratch_shapes=[pltpu.VMEM((2, TOP_K * COMBINE_ROWS * TOK_TILES, LANE), F32),
                        pltpu.SemaphoreType.DMA((2,))],
        compiler_params=pltpu.CompilerParams(dimension_semantics=("arbitrary",),
                                             vmem_limit_bytes=VMEM_LIMIT),
        name="combine",
    )(dest_blocks, dest_blocks, rt, y_2d.reshape(-1, TOK_TILES, LANE), y_2d, x1, g)


def _plan_body(rtt_ref, cnt_ref, dk_ref, db_ref, meta_ref, pst):
    i = pl.program_id(0)
    sh = _log2(EXPERT_ROWS)
    n_e = N_EXPERTS

    @pl.when(i == 0)
    def _():
        cnt = cnt_ref[...].astype(jnp.int32)
        padded = (((cnt + (EXPERT_ROWS - 1)) >> sh) << sh).astype(F32)
        e_r = lax.broadcasted_iota(jnp.int32, (n_e, n_e), 0)
        e_c = lax.broadcasted_iota(jnp.int32, (n_e, n_e), 1)
        p_t = jnp.broadcast_to(padded, (n_e, n_e)).T
        pend = jnp.sum(jnp.where(e_c <= e_r, p_t, 0.0), axis=1, keepdims=True)
        pst[...] = pend - padded
        has_rows = p_t > 0.0
        group = jnp.sum(jnp.where((e_c <= e_r) & has_rows, 1.0, 0.0), axis=1, keepdims=True) - 1.0
        nxt = jnp.min(jnp.where((e_c > e_r) & has_rows, e_c, n_e), axis=1, keepdims=True)
        nxt = jnp.where(nxt >= n_e, -1, nxt)

        mb = meta_ref.shape[1]
        blk = lax.broadcasted_iota(jnp.int32, (n_e, mb), 1)
        eb = lax.broadcasted_iota(jnp.int32, (n_e, mb), 0)
        first_row = (blk * EXPERT_ROWS).astype(F32)

        def expert_of(row0):
            return jnp.minimum(jnp.sum(jnp.where(pend <= row0, 1, 0), axis=0, keepdims=True), n_e - 1)

        be = expert_of(first_row)
        be_prev = expert_of(first_row - EXPERT_ROWS)
        hit = eb == be
        wslot = jnp.sum(jnp.where(hit, group, 0.0), axis=0, keepdims=True).astype(jnp.int32) & 1
        nx = jnp.sum(jnp.where(hit, nxt, 0), axis=0, keepdims=True)
        n_used = pend[n_e - 1:n_e, :].astype(jnp.int32) >> sh
        lane = lax.broadcasted_iota(jnp.int32, (1, mb), 1)
        first = (((be != be_prev) | (lane == 0)) & (lane < n_used)).astype(jnp.int32)
        cnt_b = jnp.sum(jnp.where(hit, cnt_ref[...], 0.0), axis=0, keepdims=True)
        pst_b = jnp.sum(jnp.where(hit, pend - padded, 0.0), axis=0, keepdims=True)
        valid = jnp.clip(cnt_b - (first_row[0:1, :] - pst_b), 0.0, float(EXPERT_ROWS)).astype(jnp.int32)
        row8 = lax.broadcasted_iota(jnp.int32, (SUBLANE, mb), 0)
        meta = jnp.where(row8 == 0, be, jnp.where(row8 == 1, first, jnp.where(row8 == 2, wslot,
                         jnp.where(row8 == 3, nx, jnp.where(row8 == 4, n_used, valid)))))
        meta_ref[...] = meta

    tm = rtt_ref.shape[1]
    eid = lax.broadcasted_iota(jnp.int32, (n_e, tm), 0).astype(F32)
    row8 = lax.broadcasted_iota(jnp.int32, (SUBLANE, tm), 0)
    d8 = jnp.zeros((SUBLANE, tm), jnp.int32)
    for k in range(TOP_K):
        start = jnp.sum(jnp.where(eid == rtt_ref[k:k + 1, :], pst[...], 0.0), axis=0, keepdims=True)
        d8 = jnp.where(row8 == k, (start + rtt_ref[TOP_K + k:TOP_K + k + 1, :]).astype(jnp.int32), d8)
    dk_ref[...] = d8[:TOP_K]
    for b in range(tm // COMBINE_ROWS):
        db_ref[b] = d8[:TOP_K, b * COMBINE_ROWS:(b + 1) * COMBINE_ROWS]


def _plan(rtt, cnt):
    n = rtt.shape[1]
    n_rows = n * TOP_K + N_EXPERTS * EXPERT_ROWS
    n_blocks = n_rows // EXPERT_ROWS
    mb = -(-n_blocks // LANE) * LANE
    tile = max(t for t in range(COMBINE_ROWS, PLAN_TILE_MAX + 1, COMBINE_ROWS) if n % t == 0)
    dk, db, meta = pl.pallas_call(
        _plan_body,
        grid=(n // tile,),
        in_specs=[pl.BlockSpec((2 * TOP_K, tile), lambda i: (0, i)),
                  pl.BlockSpec((N_EXPERTS, 1), lambda i: (0, 0))],
        out_specs=[pl.BlockSpec((TOP_K, tile), lambda i: (0, i)),
                   pl.BlockSpec((tile // COMBINE_ROWS, TOP_K, COMBINE_ROWS), lambda i: (i, 0, 0)),
                   pl.BlockSpec((SUBLANE, mb), lambda i: (0, 0))],
        out_shape=[jax.ShapeDtypeStruct((TOP_K, n), jnp.int32),
                   jax.ShapeDtypeStruct((n // COMBINE_ROWS, TOP_K, COMBINE_ROWS), jnp.int32),
                   jax.ShapeDtypeStruct((SUBLANE, mb), jnp.int32)],
        scratch_shapes=[pltpu.VMEM((N_EXPERTS, 1), F32)],
        compiler_params=pltpu.CompilerParams(dimension_semantics=("arbitrary",)),
        name="plan",
    )(rtt, cnt)
    block_meta = (meta[0, :n_blocks], meta[4, 0:1], meta[1, :n_blocks], meta[2, :n_blocks], meta[3, :n_blocks],
                  meta[5, :n_blocks])
    return dk, db, n_rows, block_meta


def _pad_lanes(v, width):
    return jnp.zeros((1, width), F32).at[0, :v.shape[0]].set(v.astype(F32))


def kernel(x_prompt, x_sample, state_gdn_conv, state_gdn, state_gla, rms_mix_w, w_in, conv_w, gdn_a_log,
           gdn_dt_bias, gdn_norm_w, gla_gk_w, gla_gk_b, gla_norm_w, w_out, rms_ffn_w, w_router, b_router,
           w_up, b_up, w_down, b_down, rms_final_w):
    bp, tp, d = x_prompt.shape
    bs, ts, _ = x_sample.shape
    n_p, n_s = bp * tp, bs * ts
    assert d == D_MODEL and state_gdn.shape[0] == 1, "single-layer kernel"
    assert tp >= CONV_WIDTH - 1 and ts >= CONV_WIDTH - 1, "new conv state is taken from the new tokens only"
    l = 0

    wi = w_in[l]
    a0 = GDN_CONV_CH + GDN_V_W
    g0 = a0 + 2 * GDN_HEADS
    lr0 = g0 + 2 * GLA_QK_W + 2 * GLA_V_W
    small = jnp.concatenate([wi[:, a0:a0 + 2 * GDN_HEADS], wi[:, lr0:lr0 + GLA_GATE_RANK],
                             jnp.zeros((d, SM_W - 2 * GDN_HEADS - GLA_GATE_RANK), F32)], axis=1)
    w_big = jnp.concatenate([wi[:, :a0], wi[:, g0:lr0], small], axis=1).astype(BF16)
    alog = _pad_lanes(gdn_a_log[l], SM_W)
    dtb = _pad_lanes(gdn_dt_bias[l], SM_W)
    wgk = jnp.zeros((SM_W, GLA_QK_W), F32).at[SM_LR:SM_LR + GLA_GATE_RANK].set(gla_gk_w[l])
    wr = jnp.zeros((d, LANE), F32).at[:, :N_EXPERTS].set(w_router[l])
    br = jnp.full((1, LANE), -1e30, F32).at[0, :N_EXPERTS].set(b_router[l])

    assert n_p % ROW_TILE == 0 and n_s % ROW_TILE == 0
    x_p, x_s = x_prompt.reshape(n_p, d), x_sample.reshape(n_s, d)
    proj = _inproj(x_p, x_s, rms_mix_w[l][None, :], w_big)

    tb_p = PROMPT_TIME_BLOCK
    zeros_conv = jnp.zeros((bp, CONV_WIDTH - 1, GDN_CONV_CH), F32)
    og_p, gdn_p, conv_p = _gdn(proj, bp, 1, tp, tb_p, CHUNK, CHUNK, zeros_conv,
                               jnp.zeros((bp, GDN_HEADS, GDN_DK, GDN_DV), F32), conv_w[l], alog, dtb,
                               gdn_norm_w[l][None, :])
    ol_p, gla_p = _gla(proj, bp, 1, tp, tb_p, CHUNK, CHUNK, jnp.zeros((bp, GLA_HEADS, GLA_DK, GLA_DV), F32),
                       wgk, gla_gk_b[l][None, :], gla_norm_w[l][None, :])

    ts_pad = SUBLANE
    nb_s = SAMPLE_SEQS_PER_STEP
    proj_s = proj[n_p:].reshape(bs, ts, PROJ_W)
    proj_sp = jnp.pad(proj_s, ((0, 0), (0, ts_pad - ts), (0, 0))).reshape(bs * ts_pad, PROJ_W)
    og_s, gdn_s, conv_s = _gdn(proj_sp, bs, nb_s, ts_pad, ts_pad, ts_pad, ts, state_gdn_conv[l], state_gdn[l],
                               conv_w[l], alog, dtb, gdn_norm_w[l][None, :])
    ol_s, gla_s = _gla(proj_sp, bs, nb_s, ts_pad, ts_pad, ts_pad, ts, state_gla[l], wgk, gla_gk_b[l][None, :],
                       gla_norm_w[l][None, :])
    og_s = og_s.reshape(bs, ts_pad, GDN_V_W)[:, :ts].reshape(n_s, GDN_V_W)
    ol_s = ol_s.reshape(bs, ts_pad, GLA_V_W)[:, :ts].reshape(n_s, GLA_V_W)

    x1, h2, rt, rtt, cnt = _outproj(og_p, og_s, ol_p, ol_s, x_p, x_s, w_out[l].astype(BF16),
                                    rms_ffn_w[l][None, :], wr, br)

    dest_k, dest_b, n_rows, block_meta = _plan(rtt, cnt)
    xs = _dispatch(h2.reshape(-1, PACK_TILES, LANE), dest_k.reshape(-1), n_rows)
    y_rows = _experts(block_meta, xs.reshape(-1, LANE), w_up[l], b_up[l], w_down[l], b_down[l])
    y_p, y_s = _combine(dest_b, rt, y_rows, x1, rms_final_w[None, :], n_p)
    y_prompt = y_p.reshape(bp, tp, d)
    y_sample = y_s.reshape(bs, ts, d)
    return (y_prompt, y_sample, conv_p[None], gdn_p[None], gla_p[None], conv_s[None], gdn_s[None], gla_s[None])
```

```python
import functools

import jax
import jax.numpy as jnp
from jax import lax
from jax.experimental import pallas as pl
from jax.experimental.pallas import tpu as pltpu
from jax.experimental.pallas import tpu_sc as plsc

F32 = jnp.float32
BF16 = jnp.bfloat16
HI = lax.Precision.HIGHEST

D_MODEL = 1024
GDN_HEADS = 4
GDN_DK = 128
GDN_DV = 128
GLA_HEADS = 4
GLA_DK = 64
GLA_DV = 128
GLA_GATE_RANK = 16
GLA_GATE_NORMALIZER = 16.0
CONV_WIDTH = 4
CHUNK = 64
N_EXPERTS = 32
TOP_K = 4
D_FF = 1024
SWIGLU_LIMIT = 7.0
SWIGLU_ALPHA = 1.702
RMS_EPS = 1e-6
L2_EPS = 1e-6

GDN_QK_W = GDN_HEADS * GDN_DK
GDN_V_W = GDN_HEADS * GDN_DV
GDN_CONV_CH = 2 * GDN_QK_W + GDN_V_W
GLA_QK_W = GLA_HEADS * GLA_DK
GLA_V_W = GLA_HEADS * GLA_DV

COL_QKV = 0
COL_Z = 1536
COL_GQ = 2048
COL_GK = 2304
COL_GV = 2560
COL_GG = 3072
COL_SM = 3584
SM_W = 128
PROJ_W = COL_SM + SM_W
SM_A, SM_B, SM_LR = 0, 4, 8

LANE = 128
SUBLANE = 8
TOK_TILES = D_MODEL // LANE
PACK_TILES = TOK_TILES // 2
ROW_TILE = 512
EXPERT_ROWS = 512
EXPERT_ROW_STEP = 128
EXPERT_WEIGHT_QUEUE = 1
WEIGHT_CAST_ROWS = 128
COMBINE_ROWS = 512
PLAN_TILE_MAX = 2048
DMA_ISSUE_UNROLL = 16
COMBINE_GATHER_QUEUES = (0, 1)
CONV_ROW_SLAB = 128
GDN_CHUNKS_PER_TRIP = 8
GLA_CHUNKS_PER_TRIP = 4
SCAN_CHUNKS_PER_TRIP = 8
PROMPT_TIME_BLOCK = 512
SAMPLE_SEQS_PER_STEP = 16
VMEM_LIMIT = 56 * 1024 * 1024


def _dot(a, b):
    return jnp.dot(a.astype(BF16), b.astype(BF16), preferred_element_type=F32)


def _dot_nt(a, b):
    return lax.dot_general(a.astype(BF16), b.astype(BF16), (((1,), (1,)), ((), ())),
                           preferred_element_type=F32)


def _dot_tn(a, b):
    return lax.dot_general(a.astype(BF16), b.astype(BF16), (((0,), (0,)), ((), ())),
                           preferred_element_type=F32)


def _dot_hi(a, b):
    return jnp.dot(a, b, precision=HI, preferred_element_type=F32)


def _dot_3pass(a, b):
    a_hi = a.astype(BF16)
    b_hi = b.astype(BF16)
    a_lo = (a - a_hi.astype(F32)).astype(BF16)
    b_lo = (b - b_hi.astype(F32)).astype(BF16)

    def mm(x, y):
        return jnp.dot(x, y, preferred_element_type=F32)

    return (mm(a_lo, b_hi) + mm(a_hi, b_lo)) + mm(a_hi, b_hi)


def _rms(x, w):
    return x * lax.rsqrt(jnp.mean(x * x, axis=-1, keepdims=True) + RMS_EPS) * w


def _silu(x):
    return x * jax.nn.sigmoid(x)


def _group_specs(rows, width, n_p_blocks):
    return [pl.BlockSpec((rows, width), lambda i: (jnp.minimum(i, n_p_blocks - 1), 0)),
            pl.BlockSpec((rows, width), lambda i: (jnp.maximum(i - n_p_blocks, 0), 0))]


def _group_pick(i, n_p_blocks, p_ref, s_ref):
    return jnp.where(i < n_p_blocks, p_ref[...], s_ref[...])


def _inproj_body(xp_ref, xs_ref, g_ref, w_ref, o_ref, *, n_p_blocks):
    x = _group_pick(pl.program_id(0), n_p_blocks, xp_ref, xs_ref)
    h = _rms(x, g_ref[...])
    o_ref[...] = jnp.dot(h.astype(BF16), w_ref[...], preferred_element_type=F32)


def _inproj(x_p, x_s, g, w):
    n_p_blocks, n_s_blocks = x_p.shape[0] // ROW_TILE, x_s.shape[0] // ROW_TILE
    n = x_p.shape[0] + x_s.shape[0]
    return pl.pallas_call(
        functools.partial(_inproj_body, n_p_blocks=n_p_blocks),
        grid=(n_p_blocks + n_s_blocks,),
        in_specs=_group_specs(ROW_TILE, D_MODEL, n_p_blocks) + [
            pl.BlockSpec((1, D_MODEL), lambda i: (0, 0)),
            pl.BlockSpec((D_MODEL, PROJ_W), lambda i: (0, 0)),
        ],
        out_specs=pl.BlockSpec((ROW_TILE, PROJ_W), lambda i: (i, 0)),
        out_shape=jax.ShapeDtypeStruct((n, PROJ_W), F32),
        compiler_params=pltpu.CompilerParams(dimension_semantics=("arbitrary",),
                                             vmem_limit_bytes=VMEM_LIMIT),
        name="in_proj",
    )(x_p, x_s, g, w)


def _log2(n):
    assert n & (n - 1) == 0
    return n.bit_length() - 1


def _tri_inv_all(ms, c, ii, jj):
    eye = (ii == jj).astype(F32)
    base = min(c, 8)
    sh = _log2(base)
    blk = (ii >> sh) == (jj >> sh)
    ns = [jnp.where(blk, m, 0.0) for m in ms]
    xs = [eye - n for n in ns]
    ps = [_dot(n, n) for n in ns]
    ts = [_dot(jnp.concatenate([x, p], axis=0), p) for x, p in zip(xs, ps)]
    xs = [x + t[:c] for x, t in zip(xs, ts)]
    ps = [t[c:] for t in ts]
    xs = [x + _dot(x, p) for x, p in zip(xs, ps)]
    s = base
    while s < c:
        sh_s, sh_b = _log2(s), _log2(2 * s)
        off = ((ii >> sh_b) == (jj >> sh_b)) & ((ii >> sh_s) != (jj >> sh_s))
        ys = [_dot(x, jnp.where(off, m, 0.0)) for x, m in zip(xs, ms)]
        xs = [x - _dot(y, x) for x, y in zip(xs, ys)]
        s *= 2
    return xs


def _gated_norm(o, w, z):
    return o * lax.rsqrt(jnp.mean(o * o, axis=-1, keepdims=True) + RMS_EPS) * w * _silu(z)


def _chunk_rows(s, tb_rows, ci, c):
    r = s * tb_rows + ci * c
    if not isinstance(r, int):
        r = pl.multiple_of(r, c)
    return r


def _for_chunks(n_chunks, step):
    if n_chunks == 1:
        step(0, 0)
    else:
        lax.fori_loop(0, n_chunks, step, 0)


def _gdn_body(qkv_ref, z_ref, sm_ref, cbuf_ref, s0_ref, cw_ref, alog_ref, dtb_ref, nw_ref,
              o_ref, sout_ref, cout_ref, st, xc, act, gcs, us, wss, qgs, kds, aqs,
              *, nb, tb_rows, chunk, valid, n_tb):
    tb = pl.program_id(1)
    c = chunk
    n_heads = GDN_HEADS
    tail = CONV_WIDTH - 1
    pad = SUBLANE
    units = [(s, h) for s in range(nb) for h in range(n_heads)]

    n_slabs = GDN_CONV_CH // LANE

    def lanes(j):
        return slice(j * LANE, (j + 1) * LANE)

    @pl.when(tb == 0)
    def _():
        st[...] = s0_ref[...]
        for s in range(nb):
            for j in range(n_slabs):
                xc[s, j, pad - tail:pad, :] = cbuf_ref[s, :, lanes(j)]

    if n_tb > 1:
        @pl.when(tb > 0)
        def _():
            for s in range(nb):
                for j in range(n_slabs):
                    xc[s, j, pad - tail:pad, :] = xc[s, j, tb_rows + pad - tail:tb_rows + pad, :]

    for s in range(nb):
        for j in range(n_slabs):
            xc[s, j, pad:pad + tb_rows, :] = qkv_ref[s * tb_rows:(s + 1) * tb_rows, lanes(j)]

    row_slab = min(tb_rows, CONV_ROW_SLAB)
    parities = 2 if row_slab >= 2 * SUBLANE else 1
    for s in range(nb):
        for j in range(n_slabs):
            src, dst = xc.at[s, j], act.at[j]
            for sl in range(tb_rows // row_slab):
                for p in range(parities):
                    lo = pad - tail + sl * row_slab + p
                    out0 = s * tb_rows + sl * row_slab + p

                    def rows_from(start):
                        if parities == 1:
                            return pl.ds(start, row_slab)
                        return pl.ds(start, row_slab // 2, stride=2)

                    acc = src[rows_from(lo), :] * cw_ref[0:1, lanes(j)]
                    for i in range(1, CONV_WIDTH):
                        acc = acc + src[rows_from(lo + i), :] * cw_ref[i:i + 1, lanes(j)]
                    dst[rows_from(out0), :] = _silu(acc)

    ii = lax.broadcasted_iota(jnp.int32, (c, c), 0)
    jj = lax.broadcasted_iota(jnp.int32, (c, c), 1)
    lower = (ii >= jj)
    lower_f = lower.astype(F32)
    strict = (ii > jj)
    rowmask = None
    if valid < c:
        rowmask = lax.broadcasted_iota(jnp.int32, (c, 1), 0) < valid

    def hs(h, w):
        return slice(h * w, (h + 1) * w)

    n_chunks = tb_rows // c
    cpi = next(k for k in (GDN_CHUNKS_PER_TRIP, 2, 1) if n_chunks % k == 0)
    p1_units = [(g, h) for g in range(nb * cpi) for h in range(n_heads)]

    def phase1(ci, carry):
        rows, b_ts, gc_ts, gc_tts = [], [], [], []
        for g in range(nb * cpi):
            rr = pl.ds(_chunk_rows(g // cpi, tb_rows, ci * cpi + g % cpi, c), c)
            sm = sm_ref[rr, :]
            g_t = -jnp.exp(alog_ref[...]) * jax.nn.softplus(sm + dtb_ref[...])
            b_t = jax.nn.sigmoid(sm)
            if rowmask is not None:
                g_t = jnp.where(rowmask, g_t, 0.0)
                b_t = jnp.where(rowmask, b_t, 0.0)
            gc_t = _dot_hi(lower_f, g_t)
            gcs[rr, :] = gc_t
            rows.append(rr)
            b_ts.append(b_t)
            gc_ts.append(gc_t)
            gc_tts.append(gc_t.T)
        qn, kn, kb, vb = {}, {}, {}, {}
        for (s, h) in p1_units:
            q = act[h, rows[s], :]
            k = act[n_heads + h, rows[s], :]
            v = act[2 * n_heads + h, rows[s], :]
            if rowmask is not None:
                q = jnp.where(rowmask, q, 0.0)
                k = jnp.where(rowmask, k, 0.0)
                v = jnp.where(rowmask, v, 0.0)
            qn[s, h] = q * lax.rsqrt(jnp.sum(q * q, axis=-1, keepdims=True) + L2_EPS) * (GDN_DK ** -0.5)
            kn[s, h] = k * lax.rsqrt(jnp.sum(k * k, axis=-1, keepdims=True) + L2_EPS)
            beta = b_ts[s][:, SM_B + h:SM_B + h + 1]
            kb[s, h] = kn[s, h] * beta
            vb[s, h] = v * beta
        s1 = {u: _dot_nt(jnp.concatenate([kb[u], qn[u]], axis=0), kn[u]) for u in p1_units}
        mm = []
        for (s, h) in p1_units:
            gcol = gc_ts[s][:, SM_A + h:SM_A + h + 1]
            grow = gc_tts[s][SM_A + h:SM_A + h + 1, :]
            dec = jnp.exp(jnp.where(lower, gcol - grow, -jnp.inf))
            mm.append(jnp.where(strict, s1[s, h][:c] * dec, 0.0))
            aqs[h, rows[s], :] = s1[s, h][c:] * dec
        tms = _tri_inv_all(mm, c, ii, jj)
        for (s, h), tm in zip(p1_units, tms):
            gcol = gc_ts[s][:, SM_A + h:SM_A + h + 1]
            eg = jnp.exp(gcol)
            uw = _dot(tm, jnp.concatenate([vb[s, h], kb[s, h] * eg], axis=1))
            us[rows[s], hs(h, GDN_DV)] = uw[:, :GDN_DV]
            wss[rows[s], hs(h, GDN_DV)] = uw[:, GDN_DV:]
            qgs[rows[s], hs(h, GDN_DK)] = qn[s, h] * eg
            kds[rows[s], hs(h, GDN_DK)] = kn[s, h] * jnp.exp(gcol[c - 1:c, :] - gcol)
        return carry

    cp2 = next(k for k in (SCAN_CHUNKS_PER_TRIP, 1) if n_chunks % k == 0)

    def phase2(ti, carry):
        for u in range(cp2):
            scan_chunk(ti * cp2 + u)
        return carry

    def scan_chunk(ci):
        r0 = [_chunk_rows(s, tb_rows, ci, c) for s in range(nb)]
        rows = [pl.ds(r, c) for r in r0]
        ws = {(s, h): _dot(jnp.concatenate([wss[rows[s], hs(h, GDN_DV)], qgs[rows[s], hs(h, GDN_DK)]], axis=0),
                           st[s, h]) for (s, h) in units}
        v_new = {(s, h): us[rows[s], hs(h, GDN_DV)] - ws[s, h][:c] for (s, h) in units}
        o = {(s, h): ws[s, h][c:] + _dot(aqs[h, rows[s], :], v_new[s, h]) for (s, h) in units}
        upd = {(s, h): _dot_tn(kds[rows[s], hs(h, GDN_DK)], v_new[s, h]) for (s, h) in units}
        for (s, h) in units:
            g_last = gcs[pl.ds(r0[s] + c - 1, 1), SM_A + h:SM_A + h + 1]
            st[s, h] = st[s, h] * jnp.exp(g_last) + upd[s, h]
        for s in range(nb):
            o_ref[rows[s], :] = jnp.concatenate(
                [_gated_norm(o[s, h], nw_ref[...], z_ref[rows[s], hs(h, GDN_DV)]) for h in range(n_heads)], axis=1)

    _for_chunks(n_chunks // cpi, phase1)
    _for_chunks(n_chunks // cp2, phase2)

    @pl.when(tb == n_tb - 1)
    def _():
        sout_ref[...] = st[...]
        last = tb_rows if valid == c else valid
        for s in range(nb):
            cout_ref[s] = jnp.concatenate(
                [xc[s, j, pad + last - tail:pad + last, :] for j in range(n_slabs)], axis=1)


def _gdn(proj, n_seq, nb, t_len, tb_rows, chunk, valid, conv_buf, s0, conv_w, alog, dtb, nw):
    n_tb = t_len // tb_rows
    assert nb == 1 or n_tb == 1
    rows = nb * tb_rows

    def rowblk(b, t):
        return b * n_tb + t

    body = functools.partial(_gdn_body, nb=nb, tb_rows=tb_rows, chunk=chunk, valid=valid, n_tb=n_tb)
    return pl.pallas_call(
        body,
        grid=(n_seq // nb, n_tb),
        in_specs=[
            pl.BlockSpec((rows, GDN_CONV_CH), lambda b, t: (rowblk(b, t), COL_QKV // GDN_CONV_CH)),
            pl.BlockSpec((rows, GDN_V_W), lambda b, t: (rowblk(b, t), COL_Z // GDN_V_W)),
            pl.BlockSpec((rows, SM_W), lambda b, t: (rowblk(b, t), COL_SM // SM_W)),
            pl.BlockSpec((nb, CONV_WIDTH - 1, GDN_CONV_CH), lambda b, t: (b, 0, 0)),
            pl.BlockSpec((nb, GDN_HEADS, GDN_DK, GDN_DV), lambda b, t: (b, 0, 0, 0)),
            pl.BlockSpec((CONV_WIDTH, GDN_CONV_CH), lambda b, t: (0, 0)),
            pl.BlockSpec((1, SM_W), lambda b, t: (0, 0)),
            pl.BlockSpec((1, SM_W), lambda b, t: (0, 0)),
            pl.BlockSpec((1, GDN_DV), lambda b, t: (0, 0)),
        ],
        out_specs=[
            pl.BlockSpec((rows, GDN_V_W), lambda b, t: (rowblk(b, t), 0)),
            pl.BlockSpec((nb, GDN_HEADS, GDN_DK, GDN_DV), lambda b, t: (b, 0, 0, 0)),
            pl.BlockSpec((nb, CONV_WIDTH - 1, GDN_CONV_CH), lambda b, t: (b, 0, 0)),
        ],
        out_shape=[
            jax.ShapeDtypeStruct((n_seq * t_len, GDN_V_W), F32),
            jax.ShapeDtypeStruct((n_seq, GDN_HEADS, GDN_DK, GDN_DV), F32),
            jax.ShapeDtypeStruct((n_seq, CONV_WIDTH - 1, GDN_CONV_CH), F32),
        ],
        scratch_shapes=[
            pltpu.VMEM((nb, GDN_HEADS, GDN_DK, GDN_DV), F32),
            pltpu.VMEM((nb, GDN_CONV_CH // LANE, tb_rows + SUBLANE, LANE), F32),
            pltpu.VMEM((GDN_CONV_CH // LANE, rows, LANE), F32),
            pltpu.VMEM((rows, SM_W), F32),
            pltpu.VMEM((rows, GDN_V_W), F32),
            pltpu.VMEM((rows, GDN_V_W), F32),
            pltpu.VMEM((rows, GDN_QK_W), F32),
            pltpu.VMEM((rows, GDN_QK_W), F32),
            pltpu.VMEM((GDN_HEADS, rows, chunk), F32),
        ],
        compiler_params=pltpu.CompilerParams(dimension_semantics=("arbitrary", "arbitrary"),
                                             vmem_limit_bytes=VMEM_LIMIT),
        name="gdn_mixer",
    )(proj, proj, proj, conv_buf, s0, conv_w, alog, dtb, nw)


def _gla_body(q_ref, k_ref, v_ref, go_ref, sm_ref, s0_ref, wgk_ref, bgk_ref, nw_ref,
              o_ref, sout_ref, st, qes, ois, upds, decs, *, nb, tb_rows, chunk, valid, n_tb):
    tb = pl.program_id(1)
    c = chunk
    n_heads = GLA_HEADS
    units = [(s, h) for s in range(nb) for h in range(n_heads)]

    @pl.when(tb == 0)
    def _():
        st[...] = s0_ref[...]

    ii = lax.broadcasted_iota(jnp.int32, (c, c), 0)
    jj = lax.broadcasted_iota(jnp.int32, (c, c), 1)
    lower = (ii >= jj)
    lower_f = lower.astype(F32)
    rid = lax.broadcasted_iota(jnp.int32, (c, 1), 0)
    rowmask = (rid < valid) if valid < c else None
    n_sub = max(c // 16, 1)
    sub = c // n_sub

    n_chunks = tb_rows // c
    cpi = next(k for k in (GLA_CHUNKS_PER_TRIP, 2, 1) if n_chunks % k == 0)
    p1_units = [(g, h) for g in range(nb * cpi) for h in range(n_heads)]

    def phase1(ci, carry):
        rows, slots, bcs, bc_ts = [], [], [], []
        for g in range(nb * cpi):
            chunk_idx = ci * cpi + g % cpi
            rr = pl.ds(_chunk_rows(g // cpi, tb_rows, chunk_idx, c), c)
            slots.append((g // cpi) * n_chunks + chunk_idx)
            gk = jax.nn.log_sigmoid(_dot(sm_ref[rr, :], wgk_ref[...]) + bgk_ref[...]) / GLA_GATE_NORMALIZER
            if rowmask is not None:
                gk = jnp.where(rowmask, gk, 0.0)
            bc = _dot_hi(lower_f, gk)
            rows.append(rr)
            bcs.append(bc)
            bc_ts.append(bc.T)
        q, k, v, bch = {}, {}, {}, {}
        for (s, h) in p1_units:
            ks = slice(h * GLA_DK, (h + 1) * GLA_DK)
            vs = slice(h * GLA_DV, (h + 1) * GLA_DV)
            q[s, h] = q_ref[rows[s], ks] * (GLA_DK ** -0.5)
            kk = k_ref[rows[s], ks]
            vv = v_ref[rows[s], vs]
            if rowmask is not None:
                kk = jnp.where(rowmask, kk, 0.0)
                vv = jnp.where(rowmask, vv, 0.0)
            k[s, h], v[s, h] = kk, vv
            bch[s, h] = bcs[s][:, ks]
        for (g, h) in p1_units:
            qes[h, rows[g], :] = q[g, h] * jnp.exp(bch[g, h])
        a = {}
        for u in p1_units:
            q_parts, k_parts = [], []
            for sb in range(n_sub):
                ref_row = bch[u][sb * sub:sb * sub + 1, :]
                in_blk = (rid >= sb * sub) & (rid < (sb + 1) * sub)
                q_parts.append(jnp.where(in_blk, q[u] * jnp.exp(jnp.where(in_blk, bch[u] - ref_row, 0.0)), 0.0))
                k_parts.append(k[u] * jnp.exp(jnp.where(rid < (sb + 1) * sub, ref_row - bch[u], 0.0)))
            q_hat = jnp.concatenate(q_parts, axis=1) if n_sub > 1 else q_parts[0]
            k_hat = jnp.concatenate(k_parts, axis=1) if n_sub > 1 else k_parts[0]
            a[u] = jnp.where(lower, _dot_nt(q_hat, k_hat), 0.0)
        upd = {u: _dot_tn(k[u] * jnp.exp(bch[u][c - 1:c, :] - bch[u]), v[u]) for u in p1_units}
        o_intra = {u: _dot(a[u], v[u]) for u in p1_units}
        for (g, h) in p1_units:
            dec_col = bc_ts[g][h * GLA_DK:(h + 1) * GLA_DK, c - 1:c]
            decs[slots[g], h] = jnp.broadcast_to(jnp.exp(dec_col), (GLA_DK, GLA_DV))
            upds[slots[g], h] = upd[g, h]
            ois[rows[g], h * GLA_DV:(h + 1) * GLA_DV] = o_intra[g, h]
        return carry

    cp2 = next(k for k in (SCAN_CHUNKS_PER_TRIP, 1) if n_chunks % k == 0)

    def phase2(ti, carry):
        for u in range(cp2):
            scan_chunk(ti * cp2 + u)
        return carry

    def scan_chunk(ci):
        rows = [pl.ds(_chunk_rows(s, tb_rows, ci, c), c) for s in range(nb)]
        o = {(s, h): ois[rows[s], h * GLA_DV:(h + 1) * GLA_DV] + _dot(qes[h, rows[s], :], st[s, h])
             for (s, h) in units}
        for (s, h) in units:
            st[s, h] = decs[s * n_chunks + ci, h] * st[s, h] + upds[s * n_chunks + ci, h]
        for s in range(nb):
            o_ref[rows[s], :] = jnp.concatenate(
                [_gated_norm(o[s, h], nw_ref[...], go_ref[rows[s], h * GLA_DV:(h + 1) * GLA_DV])
                 for h in range(n_heads)], axis=1)

    _for_chunks(n_chunks // cpi, phase1)
    _for_chunks(n_chunks // cp2, phase2)

    @pl.when(tb == n_tb - 1)
    def _():
        sout_ref[...] = st[...]


def _gla(proj, n_seq, nb, t_len, tb_rows, chunk, valid, s0, wgk, bgk, nw):
    n_tb = t_len // tb_rows
    assert nb == 1 or n_tb == 1
    rows = nb * tb_rows

    def rowblk(b, t):
        return b * n_tb + t

    body = functools.partial(_gla_body, nb=nb, tb_rows=tb_rows, chunk=chunk, valid=valid, n_tb=n_tb)
    return pl.pallas_call(
        body,
        grid=(n_seq // nb, n_tb),
        in_specs=[
            pl.BlockSpec((rows, GLA_QK_W), lambda b, t: (rowblk(b, t), COL_GQ // GLA_QK_W)),
            pl.BlockSpec((rows, GLA_QK_W), lambda b, t: (rowblk(b, t), COL_GK // GLA_QK_W)),
            pl.BlockSpec((rows, GLA_V_W), lambda b, t: (rowblk(b, t), COL_GV // GLA_V_W)),
            pl.BlockSpec((rows, GLA_V_W), lambda b, t: (rowblk(b, t), COL_GG // GLA_V_W)),
            pl.BlockSpec((rows, SM_W), lambda b, t: (rowblk(b, t), COL_SM // SM_W)),
            pl.BlockSpec((nb, GLA_HEADS, GLA_DK, GLA_DV), lambda b, t: (b, 0, 0, 0)),
            pl.BlockSpec((SM_W, GLA_QK_W), lambda b, t: (0, 0)),
            pl.BlockSpec((1, GLA_QK_W), lambda b, t: (0, 0)),
            pl.BlockSpec((1, GLA_DV), lambda b, t: (0, 0)),
        ],
        out_specs=[
            pl.BlockSpec((rows, GLA_V_W), lambda b, t: (rowblk(b, t), 0)),
            pl.BlockSpec((nb, GLA_HEADS, GLA_DK, GLA_DV), lambda b, t: (b, 0, 0, 0)),
        ],
        out_shape=[
            jax.ShapeDtypeStruct((n_seq * t_len, GLA_V_W), F32),
            jax.ShapeDtypeStruct((n_seq, GLA_HEADS, GLA_DK, GLA_DV), F32),
        ],
        scratch_shapes=[
            pltpu.VMEM((nb, GLA_HEADS, GLA_DK, GLA_DV), F32),
            pltpu.VMEM((GLA_HEADS, rows, GLA_DK), F32),
            pltpu.VMEM((rows, GLA_V_W), F32),
            pltpu.VMEM((rows // chunk, GLA_HEADS, GLA_DK, GLA_DV), F32),
            pltpu.VMEM((rows // chunk, GLA_HEADS, GLA_DK, GLA_DV), F32),
        ],
        compiler_params=pltpu.CompilerParams(dimension_semantics=("arbitrary", "arbitrary"),
                                             vmem_limit_bytes=VMEM_LIMIT),
        name="gla_mixer",
    )(proj, proj, proj, proj, proj, s0, wgk, bgk, nw)


def _outproj_body(ogp_ref, ogs_ref, olp_ref, ols_ref, xp_ref, xs_ref, wo_ref, g_ref, wr_ref, br_ref,
                  x1_ref, h2_ref, rt_ref, rtt_ref, cnt_ref, base, *, n_p_blocks):
    i = pl.program_id(0)

    @pl.when(i == 0)
    def _():
        base[...] = jnp.zeros_like(base)

    o = jnp.concatenate([_group_pick(i, n_p_blocks, ogp_ref, ogs_ref),
                         _group_pick(i, n_p_blocks, olp_ref, ols_ref)], axis=1)
    x1 = _group_pick(i, n_p_blocks, xp_ref, xs_ref) + jnp.dot(o.astype(BF16), wo_ref[...],
                                                               preferred_element_type=F32)
    x1_ref[...] = x1
    h = _rms(x1, g_ref[...])
    _store_token_tiles(h2_ref, _pack_bf16_pairs(h))
    logits = _dot_3pass(h, wr_ref[...]) + br_ref[...]

    tm = logits.shape[0]
    lt = logits.T[:N_EXPERTS]
    eid = lax.broadcasted_iota(jnp.int32, (N_EXPERTS, tm), 0)
    work = lt
    sel = jnp.zeros((N_EXPERTS, tm), F32)
    hits, ids, vals = [], [], []
    for _ in range(TOP_K):
        m = jnp.max(work, axis=0, keepdims=True)
        idx = jnp.min(jnp.where(work == m, eid, N_EXPERTS), axis=0, keepdims=True)
        hit = eid == idx
        hits.append(hit)
        ids.append(idx)
        vals.append(m)
        work = jnp.where(hit, -jnp.inf, work)
        sel = sel + hit.astype(F32)
    exps = [jnp.exp(v - vals[0]) for v in vals]
    den = exps[0]
    for e in exps[1:]:
        den = den + e
    gates = [e / den for e in exps]

    ri = lax.broadcasted_iota(jnp.int32, (tm, tm), 0)
    ci = lax.broadcasted_iota(jnp.int32, (tm, tm), 1)
    before = _dot(sel, (ri < ci).astype(F32)) + base[...]
    ranks = [jnp.sum(jnp.where(hit, before, 0.0), axis=0, keepdims=True) for hit in hits]
    base[...] = base[...] + jnp.sum(sel, axis=1, keepdims=True)
    cnt_ref[...] = base[...]

    row = lax.broadcasted_iota(jnp.int32, (LANE, tm), 0)
    rec = jnp.zeros((LANE, tm), F32)
    for k in range(TOP_K):
        rec = jnp.where(row == k, ids[k].astype(F32), rec)
        rec = jnp.where(row == TOP_K + k, ranks[k], rec)
        rec = jnp.where(row == 2 * TOP_K + k, gates[k], rec)
    rt_ref[...] = rec.T
    rtt_ref[...] = rec[:2 * TOP_K]


def _outproj(og_p, og_s, ol_p, ol_s, x_p, x_s, wo, g, wr, br):
    n_p_blocks, n_s_blocks = x_p.shape[0] // ROW_TILE, x_s.shape[0] // ROW_TILE
    n = x_p.shape[0] + x_s.shape[0]
    return pl.pallas_call(
        functools.partial(_outproj_body, n_p_blocks=n_p_blocks),
        grid=(n_p_blocks + n_s_blocks,),
        in_specs=_group_specs(ROW_TILE, GDN_V_W, n_p_blocks) + _group_specs(ROW_TILE, GLA_V_W, n_p_blocks)
        + _group_specs(ROW_TILE, D_MODEL, n_p_blocks) + [
            pl.BlockSpec((D_MODEL, D_MODEL), lambda i: (0, 0)),
            pl.BlockSpec((1, D_MODEL), lambda i: (0, 0)),
            pl.BlockSpec((D_MODEL, LANE), lambda i: (0, 0)),
            pl.BlockSpec((1, LANE), lambda i: (0, 0)),
        ],
        out_specs=[
            pl.BlockSpec((ROW_TILE, D_MODEL), lambda i: (i, 0)),
            pl.BlockSpec((ROW_TILE * PACK_TILES, LANE), lambda i: (i, 0)),
            pl.BlockSpec((ROW_TILE, LANE), lambda i: (i, 0)),
            pl.BlockSpec((2 * TOP_K, ROW_TILE), lambda i: (0, i)),
            pl.BlockSpec((N_EXPERTS, 1), lambda i: (0, 0)),
        ],
        out_shape=[
            jax.ShapeDtypeStruct((n, D_MODEL), F32),
            jax.ShapeDtypeStruct((n * PACK_TILES, LANE), jnp.uint32),
            jax.ShapeDtypeStruct((n, LANE), F32),
            jax.ShapeDtypeStruct((2 * TOP_K, n), F32),
            jax.ShapeDtypeStruct((N_EXPERTS, 1), F32),
        ],
        scratch_shapes=[pltpu.VMEM((N_EXPERTS, 1), F32)],
        compiler_params=pltpu.CompilerParams(dimension_semantics=("arbitrary",),
                                             vmem_limit_bytes=VMEM_LIMIT),
        name="out_proj",
    )(og_p, og_s, ol_p, ol_s, x_p, x_s, wo, g, wr, br)


def _store_token_tiles(ref2d, val):
    rows, tiles = val.shape[0], val.shape[1] // LANE
    for c in range(tiles):
        ref2d[pl.ds(c, rows, stride=tiles), :] = val[:, c * LANE:(c + 1) * LANE]


def _load_token_tiles(ref2d, first_row, rows, tiles=TOK_TILES):
    return jnp.concatenate(
        [ref2d[pl.ds(first_row * tiles + c, rows, stride=tiles), :] for c in range(tiles)], axis=1)


def _pack_bf16_pairs(x):
    half = x.shape[1] // 2
    bits = lax.bitcast_convert_type(x.astype(BF16).astype(F32), jnp.uint32)
    return (bits[:, :half] >> 16) | (bits[:, half:] & jnp.uint32(0xFFFF0000))


def _unpack_bf16_pairs(w):
    lo = lax.bitcast_convert_type(w << 16, F32)
    hi = lax.bitcast_convert_type(w & jnp.uint32(0xFFFF0000), F32)
    return jnp.concatenate([lo, hi], axis=1).astype(BF16)


def _expert_weight_copies(e, ws, wup_hbm, wdn_hbm, wup_buf, wdn_buf, wsems):
    return (pltpu.make_async_copy(wup_hbm.at[e], wup_buf.at[ws], wsems.at[ws]),
            pltpu.make_async_copy(wdn_hbm.at[e], wdn_buf.at[ws], wsems.at[ws]))


def _expert_body(be_ref, nu_ref, first_ref, wslot_ref, next_ref, valid_ref, x_ref,
                 wup_hbm, bup_ref, wdn_hbm, bdn_ref, y_ref, wup_buf, wdn_buf, wsems, wup_bf, wdn_bf):
    i = pl.program_id(0)
    n_used = nu_ref[0]
    ws = wslot_ref[i]
    weight_copies = functools.partial(_expert_weight_copies, wup_hbm=wup_hbm, wdn_hbm=wdn_hbm, wup_buf=wup_buf,
                                      wdn_buf=wdn_buf, wsems=wsems)

    @pl.when((i == 0) & (n_used > 0))
    def _():
        for cp in weight_copies(be_ref[0], ws):
            cp.start(priority=EXPERT_WEIGHT_QUEUE)

    @pl.when(i < n_used)
    def _():
        @pl.when(first_ref[i] == 1)
        def _():
            for cp in weight_copies(be_ref[i], ws):
                cp.wait()

            @pl.when(next_ref[i] >= 0)
            def _():
                for cp in weight_copies(next_ref[i], 1 - ws):
                    cp.start(priority=EXPERT_WEIGHT_QUEUE)

            for r in range(0, D_MODEL, WEIGHT_CAST_ROWS):
                wup_bf[r:r + WEIGHT_CAST_ROWS, :] = wup_buf[ws, r:r + WEIGHT_CAST_ROWS, :].astype(BF16)
            for r in range(0, D_FF, WEIGHT_CAST_ROWS):
                wdn_bf[r:r + WEIGHT_CAST_ROWS, :] = wdn_buf[ws, r:r + WEIGHT_CAST_ROWS, :].astype(BF16)

        def expert_rows(n):
            x = _unpack_bf16_pairs(_load_token_tiles(x_ref, 0, n, PACK_TILES))
            gu = _dot(x, wup_bf[...]) + bup_ref[...]
            gate = jnp.minimum(gu[:, :D_FF], SWIGLU_LIMIT)
            up = jnp.clip(gu[:, D_FF:], -SWIGLU_LIMIT, SWIGLU_LIMIT)
            a = (up + 1.0) * gate * jax.nn.sigmoid(SWIGLU_ALPHA * gate)
            _store_token_tiles(y_ref, _dot(a, wdn_bf[...]) + bdn_ref[...])

        valid = valid_ref[i]
        for n in range(EXPERT_ROW_STEP, EXPERT_ROWS + 1, EXPERT_ROW_STEP):
            @pl.when((valid <= n) if n == EXPERT_ROW_STEP else ((valid > n - EXPERT_ROW_STEP) & (valid <= n)))
            def _(n=n):
                expert_rows(n)
                if n < EXPERT_ROWS:
                    y_ref[n * TOK_TILES:, :] = jnp.zeros(((EXPERT_ROWS - n) * TOK_TILES, LANE), F32)

    @pl.when(i >= n_used)
    def _():
        y_ref[...] = jnp.zeros_like(y_ref)


def _experts(block_meta, xs_2d, w_up, b_up, w_down, b_down):
    n_blocks = block_meta[0].shape[0]
    grid_spec = pltpu.PrefetchScalarGridSpec(
        num_scalar_prefetch=len(block_meta),
        grid=(n_blocks,),
        in_specs=[
            pl.BlockSpec((EXPERT_ROWS * PACK_TILES, LANE), lambda i, *_: (i, 0)),
            pl.BlockSpec(memory_space=pl.ANY),
            pl.BlockSpec((None, 1, 2 * D_FF), lambda i, be, *_: (be[i], 0, 0)),
            pl.BlockSpec(memory_space=pl.ANY),
            pl.BlockSpec((None, 1, D_MODEL), lambda i, be, *_: (be[i], 0, 0)),
        ],
        out_specs=pl.BlockSpec((EXPERT_ROWS * TOK_TILES, LANE), lambda i, *_: (i, 0)),
        scratch_shapes=[
            pltpu.VMEM((2, D_MODEL, 2 * D_FF), F32),
            pltpu.VMEM((2, D_FF, D_MODEL), F32),
            pltpu.SemaphoreType.DMA((2,)),
            pltpu.VMEM((D_MODEL, 2 * D_FF), BF16),
            pltpu.VMEM((D_FF, D_MODEL), BF16),
        ],
    )
    return pl.pallas_call(
        _expert_body,
        grid_spec=grid_spec,
        out_shape=jax.ShapeDtypeStruct((n_blocks * EXPERT_ROWS * TOK_TILES, LANE), F32),
        compiler_params=pltpu.CompilerParams(dimension_semantics=("arbitrary",),
                                             vmem_limit_bytes=VMEM_LIMIT),
        name="experts",
    )(*block_meta, xs_2d, w_up, b_up.reshape(N_EXPERTS, 1, 2 * D_FF), w_down,
      b_down.reshape(N_EXPERTS, 1, D_MODEL))


def _dispatch(h_tiles, dest_kmajor, n_rows):
    n_tok = h_tiles.shape[0]
    info = plsc.get_sparse_core_info()
    n_workers = info.num_cores * info.num_subcores
    per_worker = n_tok // n_workers
    chunk = next(c for c in (128, 96, 88, 64, 48, 32, 16, 8) if per_worker % c == 0)
    assert n_tok % n_workers == 0 and per_worker % SUBLANE == 0
    mesh = plsc.VectorSubcoreMesh(core_axis_name="c", subcore_axis_name="s")

    @functools.partial(
        pl.kernel, mesh=mesh,
        out_type=jax.ShapeDtypeStruct((n_rows,) + h_tiles.shape[1:], h_tiles.dtype),
        scratch_types=[pltpu.VMEM((TOP_K, chunk), jnp.int32), pltpu.VMEM((chunk,) + h_tiles.shape[1:], h_tiles.dtype),
                       pltpu.SemaphoreType.DMA],
    )
    def dispatch(h_hbm, dest_hbm, out_hbm, idx_v, rows_v, sem):
        wid = lax.axis_index("s") * info.num_cores + lax.axis_index("c")

        def step(j, carry):
            t0 = pl.multiple_of(wid * per_worker + j * chunk, SUBLANE)
            loads = [pltpu.async_copy(h_hbm.at[pl.ds(t0, chunk)], rows_v, sem)]
            for k in range(TOP_K):
                loads.append(pltpu.async_copy(
                    dest_hbm.at[pl.ds(pl.multiple_of(k * n_tok + t0, SUBLANE), chunk)], idx_v.at[k], sem))
            for cp in loads:
                cp.wait()
            stores = [pltpu.async_copy(rows_v, out_hbm.at[idx_v.at[k]], sem) for k in range(TOP_K)]
            for cp in stores:
                cp.wait()
            return carry

        lax.fori_loop(0, per_worker // chunk, step, 0)

    return dispatch(h_tiles, dest_kmajor)


def _gather_rows(src_tiles, idx_ref, n_rows, dst2d, sem, priorities):
    def issue(j, carry):
        for u in range(DMA_ISSUE_UNROLL):
            r = j * DMA_ISSUE_UNROLL + u
            dst = dst2d.at[pl.ds(pl.multiple_of(r * TOK_TILES, TOK_TILES), TOK_TILES), :]
            pltpu.make_async_copy(src_tiles.at[idx_ref[0, r]], dst, sem).start(
                priority=priorities[u % len(priorities)])
        return carry

    lax.fori_loop(0, n_rows // DMA_ISSUE_UNROLL, issue, 0)


def _wait_rows(src2d, n_rows, dst2d, sem):
    pltpu.make_async_copy(src2d.at[pl.ds(0, n_rows * TOK_TILES), :], dst2d, sem).wait()


def _combine_body(dest_ref, dest_next_ref, rt_ref, y_tiles, y_2d, x1_ref, g_ref, op_ref, os_ref, ybuf, sems,
                  *, n_p_blocks):
    i = pl.program_id(0)
    slot = i % 2
    n_rows = TOP_K * COMBINE_ROWS

    @pl.when(i == 0)
    def _():
        _gather_rows(y_tiles, dest_ref, n_rows, ybuf.at[0], sems.at[0], COMBINE_GATHER_QUEUES)

    _wait_rows(y_2d, n_rows, ybuf.at[slot], sems.at[slot])

    @pl.when(i + 1 < pl.num_programs(0))
    def _():
        _gather_rows(y_tiles, dest_next_ref, n_rows, ybuf.at[1 - slot], sems.at[1 - slot], COMBINE_GATHER_QUEUES)

    buf = ybuf.at[slot]
    moe = _load_token_tiles(buf, 0, COMBINE_ROWS) * rt_ref[:, 2 * TOP_K:2 * TOP_K + 1]
    for k in range(1, TOP_K):
        moe = moe + _load_token_tiles(buf, k * COMBINE_ROWS, COMBINE_ROWS) * rt_ref[:, 2 * TOP_K + k:2 * TOP_K + k + 1]
    res = _rms(x1_ref[...] + moe, g_ref[...])

    @pl.when(i < n_p_blocks)
    def _():
        op_ref[...] = res

    @pl.when(i >= n_p_blocks)
    def _():
        os_ref[...] = res


def _combine(dest_b, rt, y_2d, x1, g, n_p):
    n = x1.shape[0]
    n_blk = n // COMBINE_ROWS
    n_p_blocks = n_p // COMBINE_ROWS
    dest_blocks = dest_b.reshape(n_blk, 1, TOP_K * COMBINE_ROWS)
    return pl.pallas_call(
        functools.partial(_combine_body, n_p_blocks=n_p_blocks),
        grid=(n_blk,),
        in_specs=[
            pl.BlockSpec((None, 1, COMBINE_ROWS * TOP_K), lambda i: (i, 0, 0), memory_space=pltpu.SMEM),
            pl.BlockSpec((None, 1, COMBINE_ROWS * TOP_K), lambda i: (jnp.minimum(i + 1, n_blk - 1), 0, 0),
                         memory_space=pltpu.SMEM),
            pl.BlockSpec((COMBINE_ROWS, LANE), lambda i: (i, 0)),
            pl.BlockSpec(memory_space=pl.ANY),
            pl.BlockSpec(memory_space=pl.ANY),
            pl.BlockSpec((COMBINE_ROWS, D_MODEL), lambda i: (i, 0)),
            pl.BlockSpec((1, D_MODEL), lambda i: (0, 0)),
        ],
        out_specs=_group_specs(COMBINE_ROWS, D_MODEL, n_p_blocks),
        out_shape=[jax.ShapeDtypeStruct((n_p, D_MODEL), F32), jax.ShapeDtypeStruct((n - n_p, D_MODEL), F32)],
        scratch_shapes=[pltpu.VMEM((2, TOP_K * COMBINE_ROWS * TOK_TILES, LANE), F32),
                        pltpu.SemaphoreType.DMA((2,))],
        compiler_params=pltpu.CompilerParams(dimension_semantics=("arbitrary",),
                                             vmem_limit_bytes=VMEM_LIMIT),
        name="combine",
    )(dest_blocks, dest_blocks, rt, y_2d.reshape(-1, TOK_TILES, LANE), y_2d, x1, g)


def _plan_body(rtt_ref, cnt_ref, dk_ref, db_ref, meta_ref, pst):
    i = pl.program_id(0)
    sh = _log2(EXPERT_ROWS)
    n_e = N_EXPERTS

    @pl.when(i == 0)
    def _():
        cnt = cnt_ref[...].astype(jnp.int32)
        padded = (((cnt + (EXPERT_ROWS - 1)) >> sh) << sh).astype(F32)
        e_r = lax.broadcasted_iota(jnp.int32, (n_e, n_e), 0)
        e_c = lax.broadcasted_iota(jnp.int32, (n_e, n_e), 1)
        p_t = jnp.broadcast_to(padded, (n_e, n_e)).T
        pend = jnp.sum(jnp.where(e_c <= e_r, p_t, 0.0), axis=1, keepdims=True)
        pst[...] = pend - padded
        has_rows = p_t > 0.0
        group = jnp.sum(jnp.where((e_c <= e_r) & has_rows, 1.0, 0.0), axis=1, keepdims=True) - 1.0
        nxt = jnp.min(jnp.where((e_c > e_r) & has_rows, e_c, n_e), axis=1, keepdims=True)
        nxt = jnp.where(nxt >= n_e, -1, nxt)

        mb = meta_ref.shape[1]
        blk = lax.broadcasted_iota(jnp.int32, (n_e, mb), 1)
        eb = lax.broadcasted_iota(jnp.int32, (n_e, mb), 0)
        first_row = (blk * EXPERT_ROWS).astype(F32)

        def expert_of(row0):
            return jnp.minimum(jnp.sum(jnp.where(pend <= row0, 1, 0), axis=0, keepdims=True), n_e - 1)

        be = expert_of(first_row)
        be_prev = expert_of(first_row - EXPERT_ROWS)
        hit = eb == be
        wslot = jnp.sum(jnp.where(hit, group, 0.0), axis=0, keepdims=True).astype(jnp.int32) & 1
        nx = jnp.sum(jnp.where(hit, nxt, 0), axis=0, keepdims=True)
        n_used = pend[n_e - 1:n_e, :].astype(jnp.int32) >> sh
        lane = lax.broadcasted_iota(jnp.int32, (1, mb), 1)
        first = (((be != be_prev) | (lane == 0)) & (lane < n_used)).astype(jnp.int32)
        cnt_b = jnp.sum(jnp.where(hit, cnt_ref[...], 0.0), axis=0, keepdims=True)
        pst_b = jnp.sum(jnp.where(hit, pend - padded, 0.0), axis=0, keepdims=True)
        valid = jnp.clip(cnt_b - (first_row[0:1, :] - pst_b), 0.0, float(EXPERT_ROWS)).astype(jnp.int32)
        row8 = lax.broadcasted_iota(jnp.int32, (SUBLANE, mb), 0)
        meta = jnp.where(row8 == 0, be, jnp.where(row8 == 1, first, jnp.where(row8 == 2, wslot,
                         jnp.where(row8 == 3, nx, jnp.where(row8 == 4, n_used, valid)))))
        meta_ref[...] = meta

    tm = rtt_ref.shape[1]
    eid = lax.broadcasted_iota(jnp.int32, (n_e, tm), 0).astype(F32)
    row8 = lax.broadcasted_iota(jnp.int32, (SUBLANE, tm), 0)
    d8 = jnp.zeros((SUBLANE, tm), jnp.int32)
    for k in range(TOP_K):
        start = jnp.sum(jnp.where(eid == rtt_ref[k:k + 1, :], pst[...], 0.0), axis=0, keepdims=True)
        d8 = jnp.where(row8 == k, (start + rtt_ref[TOP_K + k:TOP_K + k + 1, :]).astype(jnp.int32), d8)
    dk_ref[...] = d8[:TOP_K]
    for b in range(tm // COMBINE_ROWS):
        db_ref[b] = d8[:TOP_K, b * COMBINE_ROWS:(b + 1) * COMBINE_ROWS]


def _plan(rtt, cnt):
    n = rtt.shape[1]
    n_rows = n * TOP_K + N_EXPERTS * EXPERT_ROWS
    n_blocks = n_rows // EXPERT_ROWS
    mb = -(-n_blocks // LANE) * LANE
    tile = max(t for t in range(COMBINE_ROWS, PLAN_TILE_MAX + 1, COMBINE_ROWS) if n % t == 0)
    dk, db, meta = pl.pallas_call(
        _plan_body,
        grid=(n // tile,),
        in_specs=[pl.BlockSpec((2 * TOP_K, tile), lambda i: (0, i)),
                  pl.BlockSpec((N_EXPERTS, 1), lambda i: (0, 0))],
        out_specs=[pl.BlockSpec((TOP_K, tile), lambda i: (0, i)),
                   pl.BlockSpec((tile // COMBINE_ROWS, TOP_K, COMBINE_ROWS), lambda i: (i, 0, 0)),
                   pl.BlockSpec((SUBLANE, mb), lambda i: (0, 0))],
        out_shape=[jax.ShapeDtypeStruct((TOP_K, n), jnp.int32),
                   jax.ShapeDtypeStruct((n // COMBINE_ROWS, TOP_K, COMBINE_ROWS), jnp.int32),
                   jax.ShapeDtypeStruct((SUBLANE, mb), jnp.int32)],
        scratch_shapes=[pltpu.VMEM((N_EXPERTS, 1), F32)],
        compiler_params=pltpu.CompilerParams(dimension_semantics=("arbitrary",)),
        name="plan",
    )(rtt, cnt)
    block_meta = (meta[0, :n_blocks], meta[4, 0:1], meta[1, :n_blocks], meta[2, :n_blocks], meta[3, :n_blocks],
                  meta[5, :n_blocks])
    return dk, db, n_rows, block_meta


def _pad_lanes(v, width):
    return jnp.zeros((1, width), F32).at[0, :v.shape[0]].set(v.astype(F32))


def kernel(x_prompt, x_sample, state_gdn_conv, state_gdn, state_gla, rms_mix_w, w_in, conv_w, gdn_a_log,
           gdn_dt_bias, gdn_norm_w, gla_gk_w, gla_gk_b, gla_norm_w, w_out, rms_ffn_w, w_router, b_router,
           w_up, b_up, w_down, b_down, rms_final_w):
    bp, tp, d = x_prompt.shape
    bs, ts, _ = x_sample.shape
    n_p, n_s = bp * tp, bs * ts
    assert d == D_MODEL and state_gdn.shape[0] == 1, "single-layer kernel"
    assert tp >= CONV_WIDTH - 1 and ts >= CONV_WIDTH - 1, "new conv state is taken from the new tokens only"
    l = 0

    wi = w_in[l]
    a0 = GDN_CONV_CH + GDN_V_W
    g0 = a0 + 2 * GDN_HEADS
    lr0 = g0 + 2 * GLA_QK_W + 2 * GLA_V_W
    small = jnp.concatenate([wi[:, a0:a0 + 2 * GDN_HEADS], wi[:, lr0:lr0 + GLA_GATE_RANK],
                             jnp.zeros((d, SM_W - 2 * GDN_HEADS - GLA_GATE_RANK), F32)], axis=1)
    w_big = jnp.concatenate([wi[:, :a0], wi[:, g0:lr0], small], axis=1).astype(BF16)
    alog = _pad_lanes(gdn_a_log[l], SM_W)
    dtb = _pad_lanes(gdn_dt_bias[l], SM_W)
    wgk = jnp.zeros((SM_W, GLA_QK_W), F32).at[SM_LR:SM_LR + GLA_GATE_RANK].set(gla_gk_w[l])
    wr = jnp.zeros((d, LANE), F32).at[:, :N_EXPERTS].set(w_router[l])
    br = jnp.full((1, LANE), -1e30, F32).at[0, :N_EXPERTS].set(b_router[l])

    assert n_p % ROW_TILE == 0 and n_s % ROW_TILE == 0
    x_p, x_s = x_prompt.reshape(n_p, d), x_sample.reshape(n_s, d)
    proj = _inproj(x_p, x_s, rms_mix_w[l][None, :], w_big)

    tb_p = PROMPT_TIME_BLOCK
    zeros_conv = jnp.zeros((bp, CONV_WIDTH - 1, GDN_CONV_CH), F32)
    og_p, gdn_p, conv_p = _gdn(proj, bp, 1, tp, tb_p, CHUNK, CHUNK, zeros_conv,
                               jnp.zeros((bp, GDN_HEADS, GDN_DK, GDN_DV), F32), conv_w[l], alog, dtb,
                               gdn_norm_w[l][None, :])
    ol_p, gla_p = _gla(proj, bp, 1, tp, tb_p, CHUNK, CHUNK, jnp.zeros((bp, GLA_HEADS, GLA_DK, GLA_DV), F32),
                       wgk, gla_gk_b[l][None, :], gla_norm_w[l][None, :])

    ts_pad = SUBLANE
    nb_s = SAMPLE_SEQS_PER_STEP
    proj_s = proj[n_p:].reshape(bs, ts, PROJ_W)
    proj_sp = jnp.pad(proj_s, ((0, 0), (0, ts_pad - ts), (0, 0))).reshape(bs * ts_pad, PROJ_W)
    og_s, gdn_s, conv_s = _gdn(proj_sp, bs, nb_s, ts_pad, ts_pad, ts_pad, ts, state_gdn_conv[l], state_gdn[l],
                               conv_w[l], alog, dtb, gdn_norm_w[l][None, :])
    ol_s, gla_s = _gla(proj_sp, bs, nb_s, ts_pad, ts_pad, ts_pad, ts, state_gla[l], wgk, gla_gk_b[l][None, :],
                       gla_norm_w[l][None, :])
    og_s = og_s.reshape(bs, ts_pad, GDN_V_W)[:, :ts].reshape(n_s, GDN_V_W)
    ol_s = ol_s.reshape(bs, ts_pad, GLA_V_W)[:, :ts].reshape(n_s, GLA_V_W)

    x1, h2, rt, rtt, cnt = _outproj(og_p, og_s, ol_p, ol_s, x_p, x_s, w_out[l].astype(BF16),
                                    rms_ffn_w[l][None, :], wr, br)

    dest_k, dest_b, n_rows, block_meta = _plan(rtt, cnt)
    xs = _dispatch(h2.reshape(-1, PACK_TILES, LANE), dest_k.reshape(-1), n_rows)
    y_rows = _experts(block_meta, xs.reshape(-1, LANE), w_up[l], b_up[l], w_down[l], b_down[l])
    y_p, y_s = _combine(dest_b, rt, y_rows, x1, rms_final_w[None, :], n_p)
    y_prompt = y_p.reshape(bp, tp, d)
    y_sample = y_s.reshape(bs, ts, d)
    return (y_prompt, y_sample, conv_p[None], gdn_p[None], gla_p[None], conv_s[None], gdn_s[None], gla_s[None])
```

```python
import functools

import jax
import jax.numpy as jnp
from jax import lax
from jax.experimental import pallas as pl
from jax.experimental.pallas import tpu as pltpu
from jax.experimental.pallas import tpu_sc as plsc

F32 = jnp.float32
BF16 = jnp.bfloat16
HI = lax.Precision.HIGHEST

D_MODEL = 1024
GDN_HEADS = 4
GDN_DK = 128
GDN_DV = 128
GLA_HEADS = 4
GLA_DK = 64
GLA_DV = 128
GLA_GATE_RANK = 16
GLA_GATE_NORMALIZER = 16.0
CONV_WIDTH = 4
CHUNK = 64
N_EXPERTS = 32
TOP_K = 4
D_FF = 1024
SWIGLU_LIMIT = 7.0
SWIGLU_ALPHA = 1.702
RMS_EPS = 1e-6
L2_EPS = 1e-6

GDN_QK_W = GDN_HEADS * GDN_DK
GDN_V_W = GDN_HEADS * GDN_DV
GDN_CONV_CH = 2 * GDN_QK_W + GDN_V_W
GLA_QK_W = GLA_HEADS * GLA_DK
GLA_V_W = GLA_HEADS * GLA_DV

COL_QKV = 0
COL_Z = 1536
COL_GQ = 2048
COL_GK = 2304
COL_GV = 2560
COL_GG = 3072
COL_SM = 3584
SM_W = 128
PROJ_W = COL_SM + SM_W
SM_A, SM_B, SM_LR = 0, 4, 8

LANE = 128
SUBLANE = 8
TOK_TILES = D_MODEL // LANE
PACK_TILES = TOK_TILES // 2
ROW_TILE = 512
EXPERT_ROWS = 512
EXPERT_ROW_STEP = 128
EXPERT_WEIGHT_QUEUE = 1
WEIGHT_CAST_ROWS = 128
COMBINE_ROWS = 512
PLAN_TILE_MAX = 2048
DMA_ISSUE_UNROLL = 64
COMBINE_GATHER_QUEUES = (0, 1)
CONV_ROW_SLAB = 128
GDN_CHUNKS_PER_TRIP = 8
GLA_CHUNKS_PER_TRIP = 4
SCAN_CHUNKS_PER_TRIP = 8
PROMPT_TIME_BLOCK = 512
SAMPLE_SEQS_PER_STEP = 16
VMEM_LIMIT = 56 * 1024 * 1024


def _dot(a, b):
    return jnp.dot(a.astype(BF16), b.astype(BF16), preferred_element_type=F32)


def _dot_nt(a, b):
    return lax.dot_general(a.astype(BF16), b.astype(BF16), (((1,), (1,)), ((), ())),
                           preferred_element_type=F32)


def _dot_tn(a, b):
    return lax.dot_general(a.astype(BF16), b.astype(BF16), (((0,), (0,)), ((), ())),
                           preferred_element_type=F32)


def _dot_hi(a, b):
    return jnp.dot(a, b, precision=HI, preferred_element_type=F32)


def _dot_3pass(a, b):
    a_hi = a.astype(BF16)
    b_hi = b.astype(BF16)
    a_lo = (a - a_hi.astype(F32)).astype(BF16)
    b_lo = (b - b_hi.astype(F32)).astype(BF16)

    def mm(x, y):
        return jnp.dot(x, y, preferred_element_type=F32)

    return (mm(a_lo, b_hi) + mm(a_hi, b_lo)) + mm(a_hi, b_hi)


def _rms(x, w):
    return x * lax.rsqrt(jnp.mean(x * x, axis=-1, keepdims=True) + RMS_EPS) * w


def _silu(x):
    return x * jax.nn.sigmoid(x)


def _group_specs(rows, width, n_p_blocks):
    return [pl.BlockSpec((rows, width), lambda i: (jnp.minimum(i, n_p_blocks - 1), 0)),
            pl.BlockSpec((rows, width), lambda i: (jnp.maximum(i - n_p_blocks, 0), 0))]


def _group_pick(i, n_p_blocks, p_ref, s_ref):
    return jnp.where(i < n_p_blocks, p_ref[...], s_ref[...])


def _inproj_body(xp_ref, xs_ref, g_ref, w_ref, o_ref, *, n_p_blocks):
    x = _group_pick(pl.program_id(0), n_p_blocks, xp_ref, xs_ref)
    h = _rms(x, g_ref[...])
    o_ref[...] = jnp.dot(h.astype(BF16), w_ref[...], preferred_element_type=F32)


def _inproj(x_p, x_s, g, w):
    n_p_blocks, n_s_blocks = x_p.shape[0] // ROW_TILE, x_s.shape[0] // ROW_TILE
    n = x_p.shape[0] + x_s.shape[0]
    return pl.pallas_call(
        functools.partial(_inproj_body, n_p_blocks=n_p_blocks),
        grid=(n_p_blocks + n_s_blocks,),
        in_specs=_group_specs(ROW_TILE, D_MODEL, n_p_blocks) + [
            pl.BlockSpec((1, D_MODEL), lambda i: (0, 0)),
            pl.BlockSpec((D_MODEL, PROJ_W), lambda i: (0, 0)),
        ],
        out_specs=pl.BlockSpec((ROW_TILE, PROJ_W), lambda i: (i, 0)),
        out_shape=jax.ShapeDtypeStruct((n, PROJ_W), F32),
        compiler_params=pltpu.CompilerParams(dimension_semantics=("arbitrary",),
                                             vmem_limit_bytes=VMEM_LIMIT),
        name="in_proj",
    )(x_p, x_s, g, w)


def _log2(n):
    assert n & (n - 1) == 0
    return n.bit_length() - 1


def _tri_inv_all(ms, c, ii, jj):
    eye = (ii == jj).astype(F32)
    base = min(c, 8)
    sh = _log2(base)
    blk = (ii >> sh) == (jj >> sh)
    ns = [jnp.where(blk, m, 0.0) for m in ms]
    xs = [eye - n for n in ns]
    ps = [_dot(n, n) for n in ns]
    ts = [_dot(jnp.concatenate([x, p], axis=0), p) for x, p in zip(xs, ps)]
    xs = [x + t[:c] for x, t in zip(xs, ts)]
    ps = [t[c:] for t in ts]
    xs = [x + _dot(x, p) for x, p in zip(xs, ps)]
    s = base
    while s < c:
        sh_s, sh_b = _log2(s), _log2(2 * s)
        off = ((ii >> sh_b) == (jj >> sh_b)) & ((ii >> sh_s) != (jj >> sh_s))
        ys = [_dot(x, jnp.where(off, m, 0.0)) for x, m in zip(xs, ms)]
        xs = [x - _dot(y, x) for x, y in zip(xs, ys)]
        s *= 2
    return xs


def _gated_norm(o, w, z):
    return o * lax.rsqrt(jnp.mean(o * o, axis=-1, keepdims=True) + RMS_EPS) * w * _silu(z)


def _chunk_rows(s, tb_rows, ci, c):
    r = s * tb_rows + ci * c
    if not isinstance(r, int):
        r = pl.multiple_of(r, c)
    return r


def _for_chunks(n_chunks, step):
    if n_chunks == 1:
        step(0, 0)
    else:
        lax.fori_loop(0, n_chunks, step, 0)


def _gdn_body(qkv_ref, z_ref, sm_ref, cbuf_ref, s0_ref, cw_ref, alog_ref, dtb_ref, nw_ref,
              o_ref, sout_ref, cout_ref, st, xc, act, gcs, us, wss, qgs, kds, aqs,
              *, nb, tb_rows, chunk, valid, n_tb):
    tb = pl.program_id(1)
    c = chunk
    n_heads = GDN_HEADS
    tail = CONV_WIDTH - 1
    pad = SUBLANE
    units = [(s, h) for s in range(nb) for h in range(n_heads)]

    n_slabs = GDN_CONV_CH // LANE

    def lanes(j):
        return slice(j * LANE, (j + 1) * LANE)

    @pl.when(tb == 0)
    def _():
        st[...] = s0_ref[...]
        for s in range(nb):
            for j in range(n_slabs):
                xc[s, j, pad - tail:pad, :] = cbuf_ref[s, :, lanes(j)]

    if n_tb > 1:
        @pl.when(tb > 0)
        def _():
            for s in range(nb):
                for j in range(n_slabs):
                    xc[s, j, pad - tail:pad, :] = xc[s, j, tb_rows + pad - tail:tb_rows + pad, :]

    for s in range(nb):
        for j in range(n_slabs):
            xc[s, j, pad:pad + tb_rows, :] = qkv_ref[s * tb_rows:(s + 1) * tb_rows, lanes(j)]

    row_slab = min(tb_rows, CONV_ROW_SLAB)
    parities = 2 if row_slab >= 2 * SUBLANE else 1
    for s in range(nb):
        for j in range(n_slabs):
            src, dst = xc.at[s, j], act.at[j]
            for sl in range(tb_rows // row_slab):
                for p in range(parities):
                    lo = pad - tail + sl * row_slab + p
                    out0 = s * tb_rows + sl * row_slab + p

                    def rows_from(start):
                        if parities == 1:
                            return pl.ds(start, row_slab)
                        return pl.ds(start, row_slab // 2, stride=2)

                    acc = src[rows_from(lo), :] * cw_ref[0:1, lanes(j)]
                    for i in range(1, CONV_WIDTH):
                        acc = acc + src[rows_from(lo + i), :] * cw_ref[i:i + 1, lanes(j)]
                    dst[rows_from(out0), :] = _silu(acc)

    ii = lax.broadcasted_iota(jnp.int32, (c, c), 0)
    jj = lax.broadcasted_iota(jnp.int32, (c, c), 1)
    lower = (ii >= jj)
    lower_f = lower.astype(F32)
    strict = (ii > jj)
    rowmask = None
    if valid < c:
        rowmask = lax.broadcasted_iota(jnp.int32, (c, 1), 0) < valid

    def hs(h, w):
        return slice(h * w, (h + 1) * w)

    n_chunks = tb_rows // c
    cpi = next(k for k in (GDN_CHUNKS_PER_TRIP, 2, 1) if n_chunks % k == 0)
    p1_units = [(g, h) for g in range(nb * cpi) for h in range(n_heads)]

    def phase1(ci, carry):
        rows, b_ts, gc_ts, gc_tts = [], [], [], []
        for g in range(nb * cpi):
            rr = pl.ds(_chunk_rows(g // cpi, tb_rows, ci * cpi + g % cpi, c), c)
            sm = sm_ref[rr, :]
            g_t = -jnp.exp(alog_ref[...]) * jax.nn.softplus(sm + dtb_ref[...])
            b_t = jax.nn.sigmoid(sm)
            if rowmask is not None:
                g_t = jnp.where(rowmask, g_t, 0.0)
                b_t = jnp.where(rowmask, b_t, 0.0)
            gc_t = _dot_hi(lower_f, g_t)
            gcs[rr, :] = gc_t
            rows.append(rr)
            b_ts.append(b_t)
            gc_ts.append(gc_t)
            gc_tts.append(gc_t.T)
        qn, kn, kb, vb = {}, {}, {}, {}
        for (s, h) in p1_units:
            q = act[h, rows[s], :]
            k = act[n_heads + h, rows[s], :]
            v = act[2 * n_heads + h, rows[s], :]
            if rowmask is not None:
                q = jnp.where(rowmask, q, 0.0)
                k = jnp.where(rowmask, k, 0.0)
                v = jnp.where(rowmask, v, 0.0)
            qn[s, h] = q * lax.rsqrt(jnp.sum(q * q, axis=-1, keepdims=True) + L2_EPS) * (GDN_DK ** -0.5)
            kn[s, h] = k * lax.rsqrt(jnp.sum(k * k, axis=-1, keepdims=True) + L2_EPS)
            beta = b_ts[s][:, SM_B + h:SM_B + h + 1]
            kb[s, h] = kn[s, h] * beta
            vb[s, h] = v * beta
        s1 = {u: _dot_nt(jnp.concatenate([kb[u], qn[u]], axis=0), kn[u]) for u in p1_units}
        mm = []
        for (s, h) in p1_units:
            gcol = gc_ts[s][:, SM_A + h:SM_A + h + 1]
            grow = gc_tts[s][SM_A + h:SM_A + h + 1, :]
            dec = jnp.exp(jnp.where(lower, gcol - grow, -jnp.inf))
            mm.append(jnp.where(strict, s1[s, h][:c] * dec, 0.0))
            aqs[h, rows[s], :] = s1[s, h][c:] * dec
        tms = _tri_inv_all(mm, c, ii, jj)
        for (s, h), tm in zip(p1_units, tms):
            gcol = gc_ts[s][:, SM_A + h:SM_A + h + 1]
            eg = jnp.exp(gcol)
            uw = _dot(tm, jnp.concatenate([vb[s, h], kb[s, h] * eg], axis=1))
            us[rows[s], hs(h, GDN_DV)] = uw[:, :GDN_DV]
            wss[rows[s], hs(h, GDN_DV)] = uw[:, GDN_DV:]
            qgs[rows[s], hs(h, GDN_DK)] = qn[s, h] * eg
            kds[rows[s], hs(h, GDN_DK)] = kn[s, h] * jnp.exp(gcol[c - 1:c, :] - gcol)
        return carry

    cp2 = next(k for k in (SCAN_CHUNKS_PER_TRIP, 1) if n_chunks % k == 0)

    def phase2(ti, carry):
        for u in range(cp2):
            scan_chunk(ti * cp2 + u)
        return carry

    def scan_chunk(ci):
        r0 = [_chunk_rows(s, tb_rows, ci, c) for s in range(nb)]
        rows = [pl.ds(r, c) for r in r0]
        ws = {(s, h): _dot(jnp.concatenate([wss[rows[s], hs(h, GDN_DV)], qgs[rows[s], hs(h, GDN_DK)]], axis=0),
                           st[s, h]) for (s, h) in units}
        v_new = {(s, h): us[rows[s], hs(h, GDN_DV)] - ws[s, h][:c] for (s, h) in units}
        o = {(s, h): ws[s, h][c:] + _dot(aqs[h, rows[s], :], v_new[s, h]) for (s, h) in units}
        upd = {(s, h): _dot_tn(kds[rows[s], hs(h, GDN_DK)], v_new[s, h]) for (s, h) in units}
        for (s, h) in units:
            g_last = gcs[pl.ds(r0[s] + c - 1, 1), SM_A + h:SM_A + h + 1]
            st[s, h] = st[s, h] * jnp.exp(g_last) + upd[s, h]
        for s in range(nb):
            o_ref[rows[s], :] = jnp.concatenate(
                [_gated_norm(o[s, h], nw_ref[...], z_ref[rows[s], hs(h, GDN_DV)]) for h in range(n_heads)], axis=1)

    _for_chunks(n_chunks // cpi, phase1)
    _for_chunks(n_chunks // cp2, phase2)

    @pl.when(tb == n_tb - 1)
    def _():
        sout_ref[...] = st[...]
        last = tb_rows if valid == c else valid
        for s in range(nb):
            cout_ref[s] = jnp.concatenate(
                [xc[s, j, pad + last - tail:pad + last, :] for j in range(n_slabs)], axis=1)


def _gdn(proj, n_seq, nb, t_len, tb_rows, chunk, valid, conv_buf, s0, conv_w, alog, dtb, nw):
    n_tb = t_len // tb_rows
    assert nb == 1 or n_tb == 1
    rows = nb * tb_rows

    def rowblk(b, t):
        return b * n_tb + t

    body = functools.partial(_gdn_body, nb=nb, tb_rows=tb_rows, chunk=chunk, valid=valid, n_tb=n_tb)
    return pl.pallas_call(
        body,
        grid=(n_seq // nb, n_tb),
        in_specs=[
            pl.BlockSpec((rows, GDN_CONV_CH), lambda b, t: (rowblk(b, t), COL_QKV // GDN_CONV_CH)),
            pl.BlockSpec((rows, GDN_V_W), lambda b, t: (rowblk(b, t), COL_Z // GDN_V_W)),
            pl.BlockSpec((rows, SM_W), lambda b, t: (rowblk(b, t), COL_SM // SM_W)),
            pl.BlockSpec((nb, CONV_WIDTH - 1, GDN_CONV_CH), lambda b, t: (b, 0, 0)),
            pl.BlockSpec((nb, GDN_HEADS, GDN_DK, GDN_DV), lambda b, t: (b, 0, 0, 0)),
            pl.BlockSpec((CONV_WIDTH, GDN_CONV_CH), lambda b, t: (0, 0)),
            pl.BlockSpec((1, SM_W), lambda b, t: (0, 0)),
            pl.BlockSpec((1, SM_W), lambda b, t: (0, 0)),
            pl.BlockSpec((1, GDN_DV), lambda b, t: (0, 0)),
        ],
        out_specs=[
            pl.BlockSpec((rows, GDN_V_W), lambda b, t: (rowblk(b, t), 0)),
            pl.BlockSpec((nb, GDN_HEADS, GDN_DK, GDN_DV), lambda b, t: (b, 0, 0, 0)),
            pl.BlockSpec((nb, CONV_WIDTH - 1, GDN_CONV_CH), lambda b, t: (b, 0, 0)),
        ],
        out_shape=[
            jax.ShapeDtypeStruct((n_seq * t_len, GDN_V_W), F32),
            jax.ShapeDtypeStruct((n_seq, GDN_HEADS, GDN_DK, GDN_DV), F32),
            jax.ShapeDtypeStruct((n_seq, CONV_WIDTH - 1, GDN_CONV_CH), F32),
        ],
        scratch_shapes=[
            pltpu.VMEM((nb, GDN_HEADS, GDN_DK, GDN_DV), F32),
            pltpu.VMEM((nb, GDN_CONV_CH // LANE, tb_rows + SUBLANE, LANE), F32),
            pltpu.VMEM((GDN_CONV_CH // LANE, rows, LANE), F32),
            pltpu.VMEM((rows, SM_W), F32),
            pltpu.VMEM((rows, GDN_V_W), F32),
            pltpu.VMEM((rows, GDN_V_W), F32),
            pltpu.VMEM((rows, GDN_QK_W), F32),
            pltpu.VMEM((rows, GDN_QK_W), F32),
            pltpu.VMEM((GDN_HEADS, rows, chunk), F32),
        ],
        compiler_params=pltpu.CompilerParams(dimension_semantics=("arbitrary", "arbitrary"),
                                             vmem_limit_bytes=VMEM_LIMIT),
        name="gdn_mixer",
    )(proj, proj, proj, conv_buf, s0, conv_w, alog, dtb, nw)


def _gla_body(q_ref, k_ref, v_ref, go_ref, sm_ref, s0_ref, wgk_ref, bgk_ref, nw_ref,
              o_ref, sout_ref, st, qes, ois, upds, decs, *, nb, tb_rows, chunk, valid, n_tb):
    tb = pl.program_id(1)
    c = chunk
    n_heads = GLA_HEADS
    units = [(s, h) for s in range(nb) for h in range(n_heads)]

    @pl.when(tb == 0)
    def _():
        st[...] = s0_ref[...]

    ii = lax.broadcasted_iota(jnp.int32, (c, c), 0)
    jj = lax.broadcasted_iota(jnp.int32, (c, c), 1)
    lower = (ii >= jj)
    lower_f = lower.astype(F32)
    rid = lax.broadcasted_iota(jnp.int32, (c, 1), 0)
    rowmask = (rid < valid) if valid < c else None
    n_sub = max(c // 16, 1)
    sub = c // n_sub

    n_chunks = tb_rows // c
    cpi = next(k for k in (GLA_CHUNKS_PER_TRIP, 2, 1) if n_chunks % k == 0)
    p1_units = [(g, h) for g in range(nb * cpi) for h in range(n_heads)]

    def phase1(ci, carry):
        rows, slots, bcs, bc_ts = [], [], [], []
        for g in range(nb * cpi):
            chunk_idx = ci * cpi + g % cpi
            rr = pl.ds(_chunk_rows(g // cpi, tb_rows, chunk_idx, c), c)
            slots.append((g // cpi) * n_chunks + chunk_idx)
            gk = jax.nn.log_sigmoid(_dot(sm_ref[rr, :], wgk_ref[...]) + bgk_ref[...]) / GLA_GATE_NORMALIZER
            if rowmask is not None:
                gk = jnp.where(rowmask, gk, 0.0)
            bc = _dot_hi(lower_f, gk)
            rows.append(rr)
            bcs.append(bc)
            bc_ts.append(bc.T)
        q, k, v, bch = {}, {}, {}, {}
        for (s, h) in p1_units:
            ks = slice(h * GLA_DK, (h + 1) * GLA_DK)
            vs = slice(h * GLA_DV, (h + 1) * GLA_DV)
            q[s, h] = q_ref[rows[s], ks] * (GLA_DK ** -0.5)
            kk = k_ref[rows[s], ks]
            vv = v_ref[rows[s], vs]
            if rowmask is not None:
                kk = jnp.where(rowmask, kk, 0.0)
                vv = jnp.where(rowmask, vv, 0.0)
            k[s, h], v[s, h] = kk, vv
            bch[s, h] = bcs[s][:, ks]
        for (g, h) in p1_units:
            qes[h, rows[g], :] = q[g, h] * jnp.exp(bch[g, h])
        a = {}
        for u in p1_units:
            q_parts, k_parts = [], []
            for sb in range(n_sub):
                ref_row = bch[u][sb * sub:sb * sub + 1, :]
                in_blk = (rid >= sb * sub) & (rid < (sb + 1) * sub)
                q_parts.append(jnp.where(in_blk, q[u] * jnp.exp(jnp.where(in_blk, bch[u] - ref_row, 0.0)), 0.0))
                k_parts.append(k[u] * jnp.exp(jnp.where(rid < (sb + 1) * sub, ref_row - bch[u], 0.0)))
            q_hat = jnp.concatenate(q_parts, axis=1) if n_sub > 1 else q_parts[0]
            k_hat = jnp.concatenate(k_parts, axis=1) if n_sub > 1 else k_parts[0]
            a[u] = jnp.where(lower, _dot_nt(q_hat, k_hat), 0.0)
        upd = {u: _dot_tn(k[u] * jnp.exp(bch[u][c - 1:c, :] - bch[u]), v[u]) for u in p1_units}
        o_intra = {u: _dot(a[u], v[u]) for u in p1_units}
        for (g, h) in p1_units:
            dec_col = bc_ts[g][h * GLA_DK:(h + 1) * GLA_DK, c - 1:c]
            decs[slots[g], h] = jnp.broadcast_to(jnp.exp(dec_col), (GLA_DK, GLA_DV))
            upds[slots[g], h] = upd[g, h]
            ois[rows[g], h * GLA_DV:(h + 1) * GLA_DV] = o_intra[g, h]
        return carry

    cp2 = next(k for k in (SCAN_CHUNKS_PER_TRIP, 1) if n_chunks % k == 0)

    def phase2(ti, carry):
        for u in range(cp2):
            scan_chunk(ti * cp2 + u)
        return carry

    def scan_chunk(ci):
        rows = [pl.ds(_chunk_rows(s, tb_rows, ci, c), c) for s in range(nb)]
        o = {(s, h): ois[rows[s], h * GLA_DV:(h + 1) * GLA_DV] + _dot(qes[h, rows[s], :], st[s, h])
             for (s, h) in units}
        for (s, h) in units:
            st[s, h] = decs[s * n_chunks + ci, h] * st[s, h] + upds[s * n_chunks + ci, h]
        for s in range(nb):
            o_ref[rows[s], :] = jnp.concatenate(
                [_gated_norm(o[s, h], nw_ref[...], go_ref[rows[s], h * GLA_DV:(h + 1) * GLA_DV])
                 for h in range(n_heads)], axis=1)

    _for_chunks(n_chunks // cpi, phase1)
    _for_chunks(n_chunks // cp2, phase2)

    @pl.when(tb == n_tb - 1)
    def _():
        sout_ref[...] = st[...]


def _gla(proj, n_seq, nb, t_len, tb_rows, chunk, valid, s0, wgk, bgk, nw):
    n_tb = t_len // tb_rows
    assert nb == 1 or n_tb == 1
    rows = nb * tb_rows

    def rowblk(b, t):
        return b * n_tb + t

    body = functools.partial(_gla_body, nb=nb, tb_rows=tb_rows, chunk=chunk, valid=valid, n_tb=n_tb)
    return pl.pallas_call(
        body,
        grid=(n_seq // nb, n_tb),
        in_specs=[
            pl.BlockSpec((rows, GLA_QK_W), lambda b, t: (rowblk(b, t), COL_GQ // GLA_QK_W)),
            pl.BlockSpec((rows, GLA_QK_W), lambda b, t: (rowblk(b, t), COL_GK // GLA_QK_W)),
            pl.BlockSpec((rows, GLA_V_W), lambda b, t: (rowblk(b, t), COL_GV // GLA_V_W)),
            pl.BlockSpec((rows, GLA_V_W), lambda b, t: (rowblk(b, t), COL_GG // GLA_V_W)),
            pl.BlockSpec((rows, SM_W), lambda b, t: (rowblk(b, t), COL_SM // SM_W)),
            pl.BlockSpec((nb, GLA_HEADS, GLA_DK, GLA_DV), lambda b, t: (b, 0, 0, 0)),
            pl.BlockSpec((SM_W, GLA_QK_W), lambda b, t: (0, 0)),
            pl.BlockSpec((1, GLA_QK_W), lambda b, t: (0, 0)),
            pl.BlockSpec((1, GLA_DV), lambda b, t: (0, 0)),
        ],
        out_specs=[
            pl.BlockSpec((rows, GLA_V_W), lambda b, t: (rowblk(b, t), 0)),
            pl.BlockSpec((nb, GLA_HEADS, GLA_DK, GLA_DV), lambda b, t: (b, 0, 0, 0)),
        ],
        out_shape=[
            jax.ShapeDtypeStruct((n_seq * t_len, GLA_V_W), F32),
            jax.ShapeDtypeStruct((n_seq, GLA_HEADS, GLA_DK, GLA_DV), F32),
        ],
        scratch_shapes=[
            pltpu.VMEM((nb, GLA_HEADS, GLA_DK, GLA_DV), F32),
            pltpu.VMEM((GLA_HEADS, rows, GLA_DK), F32),
            pltpu.VMEM((rows, GLA_V_W), F32),
            pltpu.VMEM((rows // chunk, GLA_HEADS, GLA_DK, GLA_DV), F32),
            pltpu.VMEM((rows // chunk, GLA_HEADS, GLA_DK, GLA_DV), F32),
        ],
        compiler_params=pltpu.CompilerParams(dimension_semantics=("arbitrary", "arbitrary"),
                                             vmem_limit_bytes=VMEM_LIMIT),
        name="gla_mixer",
    )(proj, proj, proj, proj, proj, s0, wgk, bgk, nw)


def _outproj_body(ogp_ref, ogs_ref, olp_ref, ols_ref, xp_ref, xs_ref, wo_ref, g_ref, wr_ref, br_ref,
                  x1_ref, h2_ref, rt_ref, rtt_ref, cnt_ref, base, *, n_p_blocks):
    i = pl.program_id(0)

    @pl.when(i == 0)
    def _():
        base[...] = jnp.zeros_like(base)

    o = jnp.concatenate([_group_pick(i, n_p_blocks, ogp_ref, ogs_ref),
                         _group_pick(i, n_p_blocks, olp_ref, ols_ref)], axis=1)
    x1 = _group_pick(i, n_p_blocks, xp_ref, xs_ref) + jnp.dot(o.astype(BF16), wo_ref[...],
                                                               preferred_element_type=F32)
    x1_ref[...] = x1
    h = _rms(x1, g_ref[...])
    _store_token_tiles(h2_ref, _pack_bf16_pairs(h))
    logits = _dot_3pass(h, wr_ref[...]) + br_ref[...]

    tm = logits.shape[0]
    lt = logits.T[:N_EXPERTS]
    eid = lax.broadcasted_iota(jnp.int32, (N_EXPERTS, tm), 0)
    work = lt
    sel = jnp.zeros((N_EXPERTS, tm), F32)
    hits, ids, vals = [], [], []
    for _ in range(TOP_K):
        m = jnp.max(work, axis=0, keepdims=True)
        idx = jnp.min(jnp.where(work == m, eid, N_EXPERTS), axis=0, keepdims=True)
        hit = eid == idx
        hits.append(hit)
        ids.append(idx)
        vals.append(m)
        work = jnp.where(hit, -jnp.inf, work)
        sel = sel + hit.astype(F32)
    exps = [jnp.exp(v - vals[0]) for v in vals]
    den = exps[0]
    for e in exps[1:]:
        den = den + e
    gates = [e / den for e in exps]

    ri = lax.broadcasted_iota(jnp.int32, (tm, tm), 0)
    ci = lax.broadcasted_iota(jnp.int32, (tm, tm), 1)
    before = _dot(sel, (ri < ci).astype(F32)) + base[...]
    ranks = [jnp.sum(jnp.where(hit, before, 0.0), axis=0, keepdims=True) for hit in hits]
    base[...] = base[...] + jnp.sum(sel, axis=1, keepdims=True)
    cnt_ref[...] = base[...]

    row = lax.broadcasted_iota(jnp.int32, (LANE, tm), 0)
    rec = jnp.zeros((LANE, tm), F32)
    for k in range(TOP_K):
        rec = jnp.where(row == k, ids[k].astype(F32), rec)
        rec = jnp.where(row == TOP_K + k, ranks[k], rec)
        rec = jnp.where(row == 2 * TOP_K + k, gates[k], rec)
    rt_ref[...] = rec.T
    rtt_ref[...] = rec[:2 * TOP_K]


def _outproj(og_p, og_s, ol_p, ol_s, x_p, x_s, wo, g, wr, br):
    n_p_blocks, n_s_blocks = x_p.shape[0] // ROW_TILE, x_s.shape[0] // ROW_TILE
    n = x_p.shape[0] + x_s.shape[0]
    return pl.pallas_call(
        functools.partial(_outproj_body, n_p_blocks=n_p_blocks),
        grid=(n_p_blocks + n_s_blocks,),
        in_specs=_group_specs(ROW_TILE, GDN_V_W, n_p_blocks) + _group_specs(ROW_TILE, GLA_V_W, n_p_blocks)
        + _group_specs(ROW_TILE, D_MODEL, n_p_blocks) + [
            pl.BlockSpec((D_MODEL, D_MODEL), lambda i: (0, 0)),
            pl.BlockSpec((1, D_MODEL), lambda i: (0, 0)),
            pl.BlockSpec((D_MODEL, LANE), lambda i: (0, 0)),
            pl.BlockSpec((1, LANE), lambda i: (0, 0)),
        ],
        out_specs=[
            pl.BlockSpec((ROW_TILE, D_MODEL), lambda i: (i, 0)),
            pl.BlockSpec((ROW_TILE * PACK_TILES, LANE), lambda i: (i, 0)),
            pl.BlockSpec((ROW_TILE, LANE), lambda i: (i, 0)),
            pl.BlockSpec((2 * TOP_K, ROW_TILE), lambda i: (0, i)),
            pl.BlockSpec((N_EXPERTS, 1), lambda i: (0, 0)),
        ],
        out_shape=[
            jax.ShapeDtypeStruct((n, D_MODEL), F32),
            jax.ShapeDtypeStruct((n * PACK_TILES, LANE), jnp.uint32),
            jax.ShapeDtypeStruct((n, LANE), F32),
            jax.ShapeDtypeStruct((2 * TOP_K, n), F32),
            jax.ShapeDtypeStruct((N_EXPERTS, 1), F32),
        ],
        scratch_shapes=[pltpu.VMEM((N_EXPERTS, 1), F32)],
        compiler_params=pltpu.CompilerParams(dimension_semantics=("arbitrary",),
                                             vmem_limit_bytes=VMEM_LIMIT),
        name="out_proj",
    )(og_p, og_s, ol_p, ol_s, x_p, x_s, wo, g, wr, br)


def _store_token_tiles(ref2d, val):
    rows, tiles = val.shape[0], val.shape[1] // LANE
    for c in range(tiles):
        ref2d[pl.ds(c, rows, stride=tiles), :] = val[:, c * LANE:(c + 1) * LANE]


def _load_token_tiles(ref2d, first_row, rows, tiles=TOK_TILES):
    return jnp.concatenate(
        [ref2d[pl.ds(first_row * tiles + c, rows, stride=tiles), :] for c in range(tiles)], axis=1)


def _pack_bf16_pairs(x):
    half = x.shape[1] // 2
    bits = lax.bitcast_convert_type(x.astype(BF16).astype(F32), jnp.uint32)
    return (bits[:, :half] >> 16) | (bits[:, half:] & jnp.uint32(0xFFFF0000))


def _unpack_bf16_pairs(w):
    lo = lax.bitcast_convert_type(w << 16, F32)
    hi = lax.bitcast_convert_type(w & jnp.uint32(0xFFFF0000), F32)
    return jnp.concatenate([lo, hi], axis=1).astype(BF16)


def _expert_weight_copies(e, ws, wup_hbm, wdn_hbm, wup_buf, wdn_buf, wsems):
    return (pltpu.make_async_copy(wup_hbm.at[e], wup_buf.at[ws], wsems.at[ws]),
            pltpu.make_async_copy(wdn_hbm.at[e], wdn_buf.at[ws], wsems.at[ws]))


def _expert_body(be_ref, nu_ref, first_ref, wslot_ref, next_ref, valid_ref, x_ref,
                 wup_hbm, bup_ref, wdn_hbm, bdn_ref, y_ref, wup_buf, wdn_buf, wsems, wup_bf, wdn_bf):
    i = pl.program_id(0)
    n_used = nu_ref[0]
    ws = wslot_ref[i]
    weight_copies = functools.partial(_expert_weight_copies, wup_hbm=wup_hbm, wdn_hbm=wdn_hbm, wup_buf=wup_buf,
                                      wdn_buf=wdn_buf, wsems=wsems)

    @pl.when((i == 0) & (n_used > 0))
    def _():
        for cp in weight_copies(be_ref[0], ws):
            cp.start(priority=EXPERT_WEIGHT_QUEUE)

    @pl.when(i < n_used)
    def _():
        @pl.when(first_ref[i] == 1)
        def _():
            for cp in weight_copies(be_ref[i], ws):
                cp.wait()

            @pl.when(next_ref[i] >= 0)
            def _():
                for cp in weight_copies(next_ref[i], 1 - ws):
                    cp.start(priority=EXPERT_WEIGHT_QUEUE)

            for r in range(0, D_MODEL, WEIGHT_CAST_ROWS):
                wup_bf[r:r + WEIGHT_CAST_ROWS, :] = wup_buf[ws, r:r + WEIGHT_CAST_ROWS, :].astype(BF16)
            for r in range(0, D_FF, WEIGHT_CAST_ROWS):
                wdn_bf[r:r + WEIGHT_CAST_ROWS, :] = wdn_buf[ws, r:r + WEIGHT_CAST_ROWS, :].astype(BF16)

        def expert_rows(n):
            x = _unpack_bf16_pairs(_load_token_tiles(x_ref, 0, n, PACK_TILES))
            gu = _dot(x, wup_bf[...]) + bup_ref[...]
            gate = jnp.minimum(gu[:, :D_FF], SWIGLU_LIMIT)
            up = jnp.clip(gu[:, D_FF:], -SWIGLU_LIMIT, SWIGLU_LIMIT)
            a = (up + 1.0) * gate * jax.nn.sigmoid(SWIGLU_ALPHA * gate)
            _store_token_tiles(y_ref, _dot(a, wdn_bf[...]) + bdn_ref[...])

        valid = valid_ref[i]
        for n in range(EXPERT_ROW_STEP, EXPERT_ROWS + 1, EXPERT_ROW_STEP):
            @pl.when((valid <= n) if n == EXPERT_ROW_STEP else ((valid > n - EXPERT_ROW_STEP) & (valid <= n)))
            def _(n=n):
                expert_rows(n)
                if n < EXPERT_ROWS:
                    y_ref[n * TOK_TILES:, :] = jnp.zeros(((EXPERT_ROWS - n) * TOK_TILES, LANE), F32)

    @pl.when(i >= n_used)
    def _():
        y_ref[...] = jnp.zeros_like(y_ref)


def _experts(block_meta, xs_2d, w_up, b_up, w_down, b_down):
    n_blocks = block_meta[0].shape[0]
    grid_spec = pltpu.PrefetchScalarGridSpec(
        num_scalar_prefetch=len(block_meta),
        grid=(n_blocks,),
        in_specs=[
            pl.BlockSpec((EXPERT_ROWS * PACK_TILES, LANE), lambda i, *_: (i, 0)),
            pl.BlockSpec(memory_space=pl.ANY),
            pl.BlockSpec((None, 1, 2 * D_FF), lambda i, be, *_: (be[i], 0, 0)),
            pl.BlockSpec(memory_space=pl.ANY),
            pl.BlockSpec((None, 1, D_MODEL), lambda i, be, *_: (be[i], 0, 0)),
        ],
        out_specs=pl.BlockSpec((EXPERT_ROWS * TOK_TILES, LANE), lambda i, *_: (i, 0)),
        scratch_shapes=[
            pltpu.VMEM((2, D_MODEL, 2 * D_FF), F32),
            pltpu.VMEM((2, D_FF, D_MODEL), F32),
            pltpu.SemaphoreType.DMA((2,)),
            pltpu.VMEM((D_MODEL, 2 * D_FF), BF16),
            pltpu.VMEM((D_FF, D_MODEL), BF16),
        ],
    )
    return pl.pallas_call(
        _expert_body,
        grid_spec=grid_spec,
        out_shape=jax.ShapeDtypeStruct((n_blocks * EXPERT_ROWS * TOK_TILES, LANE), F32),
        compiler_params=pltpu.CompilerParams(dimension_semantics=("arbitrary",),
                                             vmem_limit_bytes=VMEM_LIMIT),
        name="experts",
    )(*block_meta, xs_2d, w_up, b_up.reshape(N_EXPERTS, 1, 2 * D_FF), w_down,
      b_down.reshape(N_EXPERTS, 1, D_MODEL))


def _dispatch(h_tiles, dest_kmajor, n_rows):
    n_tok = h_tiles.shape[0]
    info = plsc.get_sparse_core_info()
    n_workers = info.num_cores * info.num_subcores
    per_worker = n_tok // n_workers
    chunk = next(c for c in (128, 96, 88, 64, 48, 32, 16, 8) if per_worker % c == 0)
    assert n_tok % n_workers == 0 and per_worker % SUBLANE == 0
    mesh = plsc.VectorSubcoreMesh(core_axis_name="c", subcore_axis_name="s")

    @functools.partial(
        pl.kernel, mesh=mesh,
        out_type=jax.ShapeDtypeStruct((n_rows,) + h_tiles.shape[1:], h_tiles.dtype),
        scratch_types=[pltpu.VMEM((TOP_K, chunk), jnp.int32), pltpu.VMEM((chunk,) + h_tiles.shape[1:], h_tiles.dtype),
                       pltpu.SemaphoreType.DMA],
    )
    def dispatch(h_hbm, dest_hbm, out_hbm, idx_v, rows_v, sem):
        wid = lax.axis_index("s") * info.num_cores + lax.axis_index("c")

        def step(j, carry):
            t0 = pl.multiple_of(wid * per_worker + j * chunk, SUBLANE)
            loads = [pltpu.async_copy(h_hbm.at[pl.ds(t0, chunk)], rows_v, sem)]
            for k in range(TOP_K):
                loads.append(pltpu.async_copy(
                    dest_hbm.at[pl.ds(pl.multiple_of(k * n_tok + t0, SUBLANE), chunk)], idx_v.at[k], sem))
            for cp in loads:
                cp.wait()
            stores = [pltpu.async_copy(rows_v, out_hbm.at[idx_v.at[k]], sem) for k in range(TOP_K)]
            for cp in stores:
                cp.wait()
            return carry

        lax.fori_loop(0, per_worker // chunk, step, 0)

    return dispatch(h_tiles, dest_kmajor)


def _gather_rows(src_tiles, idx_ref, n_rows, dst2d, sem, priorities):
    def issue(j, carry):
        for u in range(DMA_ISSUE_UNROLL):
            r = j * DMA_ISSUE_UNROLL + u
            dst = dst2d.at[pl.ds(pl.multiple_of(r * TOK_TILES, TOK_TILES), TOK_TILES), :]
            pltpu.make_async_copy(src_tiles.at[idx_ref[0, r]], dst, sem).start(
                priority=priorities[u % len(priorities)])
        return carry

    lax.fori_loop(0, n_rows // DMA_ISSUE_UNROLL, issue, 0)


def _wait_rows(src2d, n_rows, dst2d, sem):
    pltpu.make_async_copy(src2d.at[pl.ds(0, n_rows * TOK_TILES), :], dst2d, sem).wait()


def _combine_body(dest_ref, dest_next_ref, rt_ref, y_tiles, y_2d, x1_ref, g_ref, op_ref, os_ref, ybuf, sems,
                  *, n_p_blocks):
    i = pl.program_id(0)
    slot = i % 2
    n_rows = TOP_K * COMBINE_ROWS

    @pl.when(i == 0)
    def _():
        _gather_rows(y_tiles, dest_ref, n_rows, ybuf.at[0], sems.at[0], COMBINE_GATHER_QUEUES)

    _wait_rows(y_2d, n_rows, ybuf.at[slot], sems.at[slot])

    @pl.when(i + 1 < pl.num_programs(0))
    def _():
        _gather_rows(y_tiles, dest_next_ref, n_rows, ybuf.at[1 - slot], sems.at[1 - slot], COMBINE_GATHER_QUEUES)

    buf = ybuf.at[slot]
    moe = _load_token_tiles(buf, 0, COMBINE_ROWS) * rt_ref[:, 2 * TOP_K:2 * TOP_K + 1]
    for k in range(1, TOP_K):
        moe = moe + _load_token_tiles(buf, k * COMBINE_ROWS, COMBINE_ROWS) * rt_ref[:, 2 * TOP_K + k:2 * TOP_K + k + 1]
    res = _rms(x1_ref[...] + moe, g_ref[...])

    @pl.when(i < n_p_blocks)
    def _():
        op_ref[...] = res

    @pl.when(i >= n_p_blocks)
    def _():
        os_ref[...] = res


def _combine(dest_b, rt, y_2d, x1, g, n_p):
    n = x1.shape[0]
    n_blk = n // COMBINE_ROWS
    n_p_blocks = n_p // COMBINE_ROWS
    dest_blocks = dest_b.reshape(n_blk, 1, TOP_K * COMBINE_ROWS)
    return pl.pallas_call(
        functools.partial(_combine_body, n_p_blocks=n_p_blocks),
        grid=(n_blk,),
        in_specs=[
            pl.BlockSpec((None, 1, COMBINE_ROWS * TOP_K), lambda i: (i, 0, 0), memory_space=pltpu.SMEM),
            pl.BlockSpec((None, 1, COMBINE_ROWS * TOP_K), lambda i: (jnp.minimum(i + 1, n_blk - 1), 0, 0),
                         memory_space=pltpu.SMEM),
            pl.BlockSpec((COMBINE_ROWS, LANE), lambda i: (i, 0)),
            pl.BlockSpec(memory_space=pl.ANY),
            pl.BlockSpec(memory_space=pl.ANY),
            pl.BlockSpec((COMBINE_ROWS, D_MODEL), lambda i: (i, 0)),
            pl.BlockSpec((1, D_MODEL), lambda i: (0, 0)),
        ],
        out_specs=_group_specs(COMBINE_ROWS, D_MODEL, n_p_blocks),
        out_shape=[jax.ShapeDtypeStruct((n_p, D_MODEL), F32), jax.ShapeDtypeStruct((n - n_p, D_MODEL), F32)],
        scratch_shapes=[pltpu.VMEM((2, TOP_K * COMBINE_ROWS * TOK_TILES, LANE), F32),
                        pltpu.SemaphoreType.DMA((2,))],
        compiler_params=pltpu.CompilerParams(dimension_semantics=("arbitrary",),
                                             vmem_limit_bytes=VMEM_LIMIT),
        name="combine",
    )(dest_blocks, dest_blocks, rt, y_2d.reshape(-1, TOK_TILES, LANE), y_2d, x1, g)


def _plan_body(rtt_ref, cnt_ref, dk_ref, db_ref, meta_ref, pst):
    i = pl.program_id(0)
    sh = _log2(EXPERT_ROWS)
    n_e = N_EXPERTS

    @pl.when(i == 0)
    def _():
        cnt = cnt_ref[...].astype(jnp.int32)
        padded = (((cnt + (EXPERT_ROWS - 1)) >> sh) << sh).astype(F32)
        e_r = lax.broadcasted_iota(jnp.int32, (n_e, n_e), 0)
        e_c = lax.broadcasted_iota(jnp.int32, (n_e, n_e), 1)
        p_t = jnp.broadcast_to(padded, (n_e, n_e)).T
        pend = jnp.sum(jnp.where(e_c <= e_r, p_t, 0.0), axis=1, keepdims=True)
        pst[...] = pend - padded
        has_rows = p_t > 0.0
        group = jnp.sum(jnp.where((e_c <= e_r) & has_rows, 1.0, 0.0), axis=1, keepdims=True) - 1.0
        nxt = jnp.min(jnp.where((e_c > e_r) & has_rows, e_c, n_e), axis=1, keepdims=True)
        nxt = jnp.where(nxt >= n_e, -1, nxt)

        mb = meta_ref.shape[1]
        blk = lax.broadcasted_iota(jnp.int32, (n_e, mb), 1)
        eb = lax.broadcasted_iota(jnp.int32, (n_e, mb), 0)
        first_row = (blk * EXPERT_ROWS).astype(F32)

        def expert_of(row0):
            return jnp.minimum(jnp.sum(jnp.where(pend <= row0, 1, 0), axis=0, keepdims=True), n_e - 1)

        be = expert_of(first_row)
        be_prev = expert_of(first_row - EXPERT_ROWS)
        hit = eb == be
        wslot = jnp.sum(jnp.where(hit, group, 0.0), axis=0, keepdims=True).astype(jnp.int32) & 1
        nx = jnp.sum(jnp.where(hit, nxt, 0), axis=0, keepdims=True)
        n_used = pend[n_e - 1:n_e, :].astype(jnp.int32) >> sh
        lane = lax.broadcasted_iota(jnp.int32, (1, mb), 1)
        first = (((be != be_prev) | (lane == 0)) & (lane < n_used)).astype(jnp.int32)
        cnt_b = jnp.sum(jnp.where(hit, cnt_ref[...], 0.0), axis=0, keepdims=True)
        pst_b = jnp.sum(jnp.where(hit, pend - padded, 0.0), axis=0, keepdims=True)
        valid = jnp.clip(cnt_b - (first_row[0:1, :] - pst_b), 0.0, float(EXPERT_ROWS)).astype(jnp.int32)
        row8 = lax.broadcasted_iota(jnp.int32, (SUBLANE, mb), 0)
        meta = jnp.where(row8 == 0, be, jnp.where(row8 == 1, first, jnp.where(row8 == 2, wslot,
                         jnp.where(row8 == 3, nx, jnp.where(row8 == 4, n_used, valid)))))
        meta_ref[...] = meta

    tm = rtt_ref.shape[1]
    eid = lax.broadcasted_iota(jnp.int32, (n_e, tm), 0).astype(F32)
    row8 = lax.broadcasted_iota(jnp.int32, (SUBLANE, tm), 0)
    d8 = jnp.zeros((SUBLANE, tm), jnp.int32)
    for k in range(TOP_K):
        start = jnp.sum(jnp.where(eid == rtt_ref[k:k + 1, :], pst[...], 0.0), axis=0, keepdims=True)
        d8 = jnp.where(row8 == k, (start + rtt_ref[TOP_K + k:TOP_K + k + 1, :]).astype(jnp.int32), d8)
    dk_ref[...] = d8[:TOP_K]
    for b in range(tm // COMBINE_ROWS):
        db_ref[b] = d8[:TOP_K, b * COMBINE_ROWS:(b + 1) * COMBINE_ROWS]


def _plan(rtt, cnt):
    n = rtt.shape[1]
    n_rows = n * TOP_K + N_EXPERTS * EXPERT_ROWS
    n_blocks = n_rows // EXPERT_ROWS
    mb = -(-n_blocks // LANE) * LANE
    tile = max(t for t in range(COMBINE_ROWS, PLAN_TILE_MAX + 1, COMBINE_ROWS) if n % t == 0)
    dk, db, meta = pl.pallas_call(
        _plan_body,
        grid=(n // tile,),
        in_specs=[pl.BlockSpec((2 * TOP_K, tile), lambda i: (0, i)),
                  pl.BlockSpec((N_EXPERTS, 1), lambda i: (0, 0))],
        out_specs=[pl.BlockSpec((TOP_K, tile), lambda i: (0, i)),
                   pl.BlockSpec((tile // COMBINE_ROWS, TOP_K, COMBINE_ROWS), lambda i: (i, 0, 0)),
                   pl.BlockSpec((SUBLANE, mb), lambda i: (0, 0))],
        out_shape=[jax.ShapeDtypeStruct((TOP_K, n), jnp.int32),
                   jax.ShapeDtypeStruct((n // COMBINE_ROWS, TOP_K, COMBINE_ROWS), jnp.int32),
                   jax.ShapeDtypeStruct((SUBLANE, mb), jnp.int32)],
        scratch_shapes=[pltpu.VMEM((N_EXPERTS, 1), F32)],
        compiler_params=pltpu.CompilerParams(dimension_semantics=("arbitrary",)),
        name="plan",
    )(rtt, cnt)
    block_meta = (meta[0, :n_blocks], meta[4, 0:1], meta[1, :n_blocks], meta[2, :n_blocks], meta[3, :n_blocks],
                  meta[5, :n_blocks])
    return dk, db, n_rows, block_meta


def _pad_lanes(v, width):
    return jnp.zeros((1, width), F32).at[0, :v.shape[0]].set(v.astype(F32))


def kernel(x_prompt, x_sample, state_gdn_conv, state_gdn, state_gla, rms_mix_w, w_in, conv_w, gdn_a_log,
           gdn_dt_bias, gdn_norm_w, gla_gk_w, gla_gk_b, gla_norm_w, w_out, rms_ffn_w, w_router, b_router,
           w_up, b_up, w_down, b_down, rms_final_w):
    bp, tp, d = x_prompt.shape
    bs, ts, _ = x_sample.shape
    n_p, n_s = bp * tp, bs * ts
    assert d == D_MODEL and state_gdn.shape[0] == 1, "single-layer kernel"
    assert tp >= CONV_WIDTH - 1 and ts >= CONV_WIDTH - 1, "new conv state is taken from the new tokens only"
    l = 0

    wi = w_in[l]
    a0 = GDN_CONV_CH + GDN_V_W
    g0 = a0 + 2 * GDN_HEADS
    lr0 = g0 + 2 * GLA_QK_W + 2 * GLA_V_W
    small = jnp.concatenate([wi[:, a0:a0 + 2 * GDN_HEADS], wi[:, lr0:lr0 + GLA_GATE_RANK],
                             jnp.zeros((d, SM_W - 2 * GDN_HEADS - GLA_GATE_RANK), F32)], axis=1)
    w_big = jnp.concatenate([wi[:, :a0], wi[:, g0:lr0], small], axis=1).astype(BF16)
    alog = _pad_lanes(gdn_a_log[l], SM_W)
    dtb = _pad_lanes(gdn_dt_bias[l], SM_W)
    wgk = jnp.zeros((SM_W, GLA_QK_W), F32).at[SM_LR:SM_LR + GLA_GATE_RANK].set(gla_gk_w[l])
    wr = jnp.zeros((d, LANE), F32).at[:, :N_EXPERTS].set(w_router[l])
    br = jnp.full((1, LANE), -1e30, F32).at[0, :N_EXPERTS].set(b_router[l])

    assert n_p % ROW_TILE == 0 and n_s % ROW_TILE == 0
    x_p, x_s = x_prompt.reshape(n_p, d), x_sample.reshape(n_s, d)
    proj = _inproj(x_p, x_s, rms_mix_w[l][None, :], w_big)

    tb_p = PROMPT_TIME_BLOCK
    zeros_conv = jnp.zeros((bp, CONV_WIDTH - 1, GDN_CONV_CH), F32)
    og_p, gdn_p, conv_p = _gdn(proj, bp, 1, tp, tb_p, CHUNK, CHUNK, zeros_conv,
                               jnp.zeros((bp, GDN_HEADS, GDN_DK, GDN_DV), F32), conv_w[l], alog, dtb,
                               gdn_norm_w[l][None, :])
    ol_p, gla_p = _gla(proj, bp, 1, tp, tb_p, CHUNK, CHUNK, jnp.zeros((bp, GLA_HEADS, GLA_DK, GLA_DV), F32),
                       wgk, gla_gk_b[l][None, :], gla_norm_w[l][None, :])

    ts_pad = SUBLANE
    nb_s = SAMPLE_SEQS_PER_STEP
    proj_s = proj[n_p:].reshape(bs, ts, PROJ_W)
    proj_sp = jnp.pad(proj_s, ((0, 0), (0, ts_pad - ts), (0, 0))).reshape(bs * ts_pad, PROJ_W)
    og_s, gdn_s, conv_s = _gdn(proj_sp, bs, nb_s, ts_pad, ts_pad, ts_pad, ts, state_gdn_conv[l], state_gdn[l],
                               conv_w[l], alog, dtb, gdn_norm_w[l][None, :])
    ol_s, gla_s = _gla(proj_sp, bs, nb_s, ts_pad, ts_pad, ts_pad, ts, state_gla[l], wgk, gla_gk_b[l][None, :],
                       gla_norm_w[l][None, :])
    og_s = og_s.reshape(bs, ts_pad, GDN_V_W)[:, :ts].reshape(n_s, GDN_V_W)
    ol_s = ol_s.reshape(bs, ts_pad, GLA_V_W)[:, :ts].reshape(n_s, GLA_V_W)

    x1, h2, rt, rtt, cnt = _outproj(og_p, og_s, ol_p, ol_s, x_p, x_s, w_out[l].astype(BF16),
                                    rms_ffn_w[l][None, :], wr, br)

    dest_k, dest_b, n_rows, block_meta = _plan(rtt, cnt)
    xs = _dispatch(h2.reshape(-1, PACK_TILES, LANE), dest_k.reshape(-1), n_rows)
    y_rows = _experts(block_meta, xs.reshape(-1, LANE), w_up[l], b_up[l], w_down[l], b_down[l])
    y_p, y_s = _combine(dest_b, rt, y_rows, x1, rms_final_w[None, :], n_p)
    y_prompt = y_p.reshape(bp, tp, d)
    y_sample = y_s.reshape(bs, ts, d)
    return (y_prompt, y_sample, conv_p[None], gdn_p[None], gla_p[None], conv_s[None], gdn_s[None], gla_s[None])
```

```python
import functools

import jax
import jax.numpy as jnp
from jax import lax
from jax.experimental import pallas as pl
from jax.experimental.pallas import tpu as pltpu
from jax.experimental.pallas import tpu_sc as plsc

F32 = jnp.float32
BF16 = jnp.bfloat16
HI = lax.Precision.HIGHEST

D_MODEL = 1024
GDN_HEADS = 4
GDN_DK = 128
GDN_DV = 128
GLA_HEADS = 4
GLA_DK = 64
GLA_DV = 128
GLA_GATE_RANK = 16
GLA_GATE_NORMALIZER = 16.0
CONV_WIDTH = 4
CHUNK = 64
N_EXPERTS = 32
TOP_K = 4
D_FF = 1024
SWIGLU_LIMIT = 7.0
SWIGLU_ALPHA = 1.702
RMS_EPS = 1e-6
L2_EPS = 1e-6

GDN_QK_W = GDN_HEADS * GDN_DK
GDN_V_W = GDN_HEADS * GDN_DV
GDN_CONV_CH = 2 * GDN_QK_W + GDN_V_W
GLA_QK_W = GLA_HEADS * GLA_DK
GLA_V_W = GLA_HEADS * GLA_DV

COL_QKV = 0
COL_Z = 1536
COL_GQ = 2048
COL_GK = 2304
COL_GV = 2560
COL_GG = 3072
COL_SM = 3584
SM_W = 128
PROJ_W = COL_SM + SM_W
SM_A, SM_B, SM_LR = 0, 4, 8

LANE = 128
SUBLANE = 8
TOK_TILES = D_MODEL // LANE
PACK_TILES = TOK_TILES // 2
ROW_TILE = 512
EXPERT_ROWS = 512
EXPERT_ROW_STEP = 128
EXPERT_WEIGHT_QUEUE = 1
WEIGHT_CAST_ROWS = 128
COMBINE_ROWS = 512
PLAN_TILE_MAX = 2048
DMA_ISSUE_UNROLL = 64
COMBINE_GATHER_QUEUES = (0, 1)
CONV_ROW_SLAB = 128
GDN_CHUNKS_PER_TRIP = 8
GLA_CHUNKS_PER_TRIP = 4
SCAN_CHUNKS_PER_TRIP = 8
PROMPT_TIME_BLOCK = 512
SAMPLE_SEQS_PER_STEP = 16
VMEM_LIMIT = 56 * 1024 * 1024


def _dot(a, b):
    return jnp.dot(a.astype(BF16), b.astype(BF16), preferred_element_type=F32)


def _dot_nt(a, b):
    return lax.dot_general(a.astype(BF16), b.astype(BF16), (((1,), (1,)), ((), ())),
                           preferred_element_type=F32)


def _dot_tn(a, b):
    return lax.dot_general(a.astype(BF16), b.astype(BF16), (((0,), (0,)), ((), ())),
                           preferred_element_type=F32)


def _dot_hi(a, b):
    return jnp.dot(a, b, precision=HI, preferred_element_type=F32)


def _dot_3pass(a, b):
    a_hi = a.astype(BF16)
    b_hi = b.astype(BF16)
    a_lo = (a - a_hi.astype(F32)).astype(BF16)
    b_lo = (b - b_hi.astype(F32)).astype(BF16)

    def mm(x, y):
        return jnp.dot(x, y, preferred_element_type=F32)

    return (mm(a_lo, b_hi) + mm(a_hi, b_lo)) + mm(a_hi, b_hi)


def _rms(x, w):
    return x * lax.rsqrt(jnp.mean(x * x, axis=-1, keepdims=True) + RMS_EPS) * w


def _silu(x):
    return x * jax.nn.sigmoid(x)


def _group_specs(rows, width, n_p_blocks):
    return [pl.BlockSpec((rows, width), lambda i: (jnp.minimum(i, n_p_blocks - 1), 0)),
            pl.BlockSpec((rows, width), lambda i: (jnp.maximum(i - n_p_blocks, 0), 0))]


def _group_pick(i, n_p_blocks, p_ref, s_ref):
    return jnp.where(i < n_p_blocks, p_ref[...], s_ref[...])


def _inproj_body(xp_ref, xs_ref, g_ref, w_ref, o_ref, *, n_p_blocks):
    x = _group_pick(pl.program_id(0), n_p_blocks, xp_ref, xs_ref)
    h = _rms(x, g_ref[...])
    o_ref[...] = jnp.dot(h.astype(BF16), w_ref[...], preferred_element_type=F32)


def _inproj(x_p, x_s, g, w):
    n_p_blocks, n_s_blocks = x_p.shape[0] // ROW_TILE, x_s.shape[0] // ROW_TILE
    n = x_p.shape[0] + x_s.shape[0]
    return pl.pallas_call(
        functools.partial(_inproj_body, n_p_blocks=n_p_blocks),
        grid=(n_p_blocks + n_s_blocks,),
        in_specs=_group_specs(ROW_TILE, D_MODEL, n_p_blocks) + [
            pl.BlockSpec((1, D_MODEL), lambda i: (0, 0)),
            pl.BlockSpec((D_MODEL, PROJ_W), lambda i: (0, 0)),
        ],
        out_specs=pl.BlockSpec((ROW_TILE, PROJ_W), lambda i: (i, 0)),
        out_shape=jax.ShapeDtypeStruct((n, PROJ_W), F32),
        compiler_params=pltpu.CompilerParams(dimension_semantics=("arbitrary",),
                                             vmem_limit_bytes=VMEM_LIMIT),
        name="in_proj",
    )(x_p, x_s, g, w)


def _log2(n):
    assert n & (n - 1) == 0
    return n.bit_length() - 1


def _tri_inv_all(ms, c, ii, jj):
    eye = (ii == jj).astype(F32)
    base = min(c, 8)
    sh = _log2(base)
    blk = (ii >> sh) == (jj >> sh)
    ns = [jnp.where(blk, m, 0.0) for m in ms]
    xs = [eye - n for n in ns]
    ps = [_dot(n, n) for n in ns]
    ts = [_dot(jnp.concatenate([x, p], axis=0), p) for x, p in zip(xs, ps)]
    xs = [x + t[:c] for x, t in zip(xs, ts)]
    ps = [t[c:] for t in ts]
    xs = [x + _dot(x, p) for x, p in zip(xs, ps)]
    s = base
    while s < c:
        sh_s, sh_b = _log2(s), _log2(2 * s)
        off = ((ii >> sh_b) == (jj >> sh_b)) & ((ii >> sh_s) != (jj >> sh_s))
        ys = [_dot(x, jnp.where(off, m, 0.0)) for x, m in zip(xs, ms)]
        xs = [x - _dot(y, x) for x, y in zip(xs, ys)]
        s *= 2
    return xs


def _gated_norm(o, w, z):
    return o * lax.rsqrt(jnp.mean(o * o, axis=-1, keepdims=True) + RMS_EPS) * w * _silu(z)


def _chunk_rows(s, tb_rows, ci, c):
    r = s * tb_rows + ci * c
    if not isinstance(r, int):
        r = pl.multiple_of(r, c)
    return r


def _for_chunks(n_chunks, step):
    if n_chunks == 1:
        step(0, 0)
    else:
        lax.fori_loop(0, n_chunks, step, 0)


def _gdn_body(qkv_ref, z_ref, sm_ref, cbuf_ref, s0_ref, cw_ref, alog_ref, dtb_ref, nw_ref,
              o_ref, sout_ref, cout_ref, st, xc, act, gcs, us, wss, qgs, kds, aqs,
              *, nb, tb_rows, chunk, valid, n_tb):
    tb = pl.program_id(1)
    c = chunk
    n_heads = GDN_HEADS
    tail = CONV_WIDTH - 1
    pad = SUBLANE
    units = [(s, h) for s in range(nb) for h in range(n_heads)]

    n_slabs = GDN_CONV_CH // LANE

    def lanes(j):
        return slice(j * LANE, (j + 1) * LANE)

    @pl.when(tb == 0)
    def _():
        st[...] = s0_ref[...]
        for s in range(nb):
            for j in range(n_slabs):
                xc[s, j, pad - tail:pad, :] = cbuf_ref[s, :, lanes(j)]

    if n_tb > 1:
        @pl.when(tb > 0)
        def _():
            for s in range(nb):
                for j in range(n_slabs):
                    xc[s, j, pad - tail:pad, :] = xc[s, j, tb_rows + pad - tail:tb_rows + pad, :]

    for s in range(nb):
        for j in range(n_slabs):
            xc[s, j, pad:pad + tb_rows, :] = qkv_ref[s * tb_rows:(s + 1) * tb_rows, lanes(j)]

    row_slab = min(tb_rows, CONV_ROW_SLAB)
    parities = 2 if row_slab >= 2 * SUBLANE else 1
    for s in range(nb):
        for j in range(n_slabs):
            src, dst = xc.at[s, j], act.at[j]
            for sl in range(tb_rows // row_slab):
                for p in range(parities):
                    lo = pad - tail + sl * row_slab + p
                    out0 = s * tb_rows + sl * row_slab + p

                    def rows_from(start):
                        if parities == 1:
                            return pl.ds(start, row_slab)
                        return pl.ds(start, row_slab // 2, stride=2)

                    acc = src[rows_from(lo), :] * cw_ref[0:1, lanes(j)]
                    for i in range(1, CONV_WIDTH):
                        acc = acc + src[rows_from(lo + i), :] * cw_ref[i:i + 1, lanes(j)]
                    dst[rows_from(out0), :] = _silu(acc)

    ii = lax.broadcasted_iota(jnp.int32, (c, c), 0)
    jj = lax.broadcasted_iota(jnp.int32, (c, c), 1)
    lower = (ii >= jj)
    lower_f = lower.astype(F32)
    strict = (ii > jj)
    rowmask = None
    if valid < c:
        rowmask = lax.broadcasted_iota(jnp.int32, (c, 1), 0) < valid

    def hs(h, w):
        return slice(h * w, (h + 1) * w)

    n_chunks = tb_rows // c
    cpi = next(k for k in (GDN_CHUNKS_PER_TRIP, 2, 1) if n_chunks % k == 0)
    p1_units = [(g, h) for g in range(nb * cpi) for h in range(n_heads)]

    def phase1(ci, carry):
        rows, b_ts, gc_ts, gc_tts = [], [], [], []
        for g in range(nb * cpi):
            rr = pl.ds(_chunk_rows(g // cpi, tb_rows, ci * cpi + g % cpi, c), c)
            sm = sm_ref[rr, :]
            g_t = -jnp.exp(alog_ref[...]) * jax.nn.softplus(sm + dtb_ref[...])
            b_t = jax.nn.sigmoid(sm)
            if rowmask is not None:
                g_t = jnp.where(rowmask, g_t, 0.0)
                b_t = jnp.where(rowmask, b_t, 0.0)
            gc_t = _dot_hi(lower_f, g_t)
            gcs[rr, :] = gc_t
            rows.append(rr)
            b_ts.append(b_t)
            gc_ts.append(gc_t)
            gc_tts.append(gc_t.T)
        qn, kn, kb, vb = {}, {}, {}, {}
        for (s, h) in p1_units:
            q = act[h, rows[s], :]
            k = act[n_heads + h, rows[s], :]
            v = act[2 * n_heads + h, rows[s], :]
            if rowmask is not None:
                q = jnp.where(rowmask, q, 0.0)
                k = jnp.where(rowmask, k, 0.0)
                v = jnp.where(rowmask, v, 0.0)
            qn[s, h] = q * lax.rsqrt(jnp.sum(q * q, axis=-1, keepdims=True) + L2_EPS) * (GDN_DK ** -0.5)
            kn[s, h] = k * lax.rsqrt(jnp.sum(k * k, axis=-1, keepdims=True) + L2_EPS)
            beta = b_ts[s][:, SM_B + h:SM_B + h + 1]
            kb[s, h] = kn[s, h] * beta
            vb[s, h] = v * beta
        s1 = {u: _dot_nt(jnp.concatenate([kb[u], qn[u]], axis=0), kn[u]) for u in p1_units}
        mm = []
        for (s, h) in p1_units:
            gcol = gc_ts[s][:, SM_A + h:SM_A + h + 1]
            grow = gc_tts[s][SM_A + h:SM_A + h + 1, :]
            dec = jnp.exp(jnp.where(lower, gcol - grow, -jnp.inf))
            mm.append(jnp.where(strict, s1[s, h][:c] * dec, 0.0))
            aqs[h, rows[s], :] = s1[s, h][c:] * dec
        tms = _tri_inv_all(mm, c, ii, jj)
        for (s, h), tm in zip(p1_units, tms):
            gcol = gc_ts[s][:, SM_A + h:SM_A + h + 1]
            eg = jnp.exp(gcol)
            uw = _dot(tm, jnp.concatenate([vb[s, h], kb[s, h] * eg], axis=1))
            us[rows[s], hs(h, GDN_DV)] = uw[:, :GDN_DV]
            wss[rows[s], hs(h, GDN_DV)] = uw[:, GDN_DV:]
            qgs[rows[s], hs(h, GDN_DK)] = qn[s, h] * eg
            kds[rows[s], hs(h, GDN_DK)] = kn[s, h] * jnp.exp(gcol[c - 1:c, :] - gcol)
        return carry

    cp2 = next(k for k in (SCAN_CHUNKS_PER_TRIP, 1) if n_chunks % k == 0)

    def phase2(ti, carry):
        for u in range(cp2):
            scan_chunk(ti * cp2 + u)
        return carry

    def scan_chunk(ci):
        r0 = [_chunk_rows(s, tb_rows, ci, c) for s in range(nb)]
        rows = [pl.ds(r, c) for r in r0]
        ws = {(s, h): _dot(jnp.concatenate([wss[rows[s], hs(h, GDN_DV)], qgs[rows[s], hs(h, GDN_DK)]], axis=0),
                           st[s, h]) for (s, h) in units}
        v_new = {(s, h): us[rows[s], hs(h, GDN_DV)] - ws[s, h][:c] for (s, h) in units}
        o = {(s, h): ws[s, h][c:] + _dot(aqs[h, rows[s], :], v_new[s, h]) for (s, h) in units}
        upd = {(s, h): _dot_tn(kds[rows[s], hs(h, GDN_DK)], v_new[s, h]) for (s, h) in units}
        for (s, h) in units:
            g_last = gcs[pl.ds(r0[s] + c - 1, 1), SM_A + h:SM_A + h + 1]
            st[s, h] = st[s, h] * jnp.exp(g_last) + upd[s, h]
        for s in range(nb):
            o_ref[rows[s], :] = jnp.concatenate(
                [_gated_norm(o[s, h], nw_ref[...], z_ref[rows[s], hs(h, GDN_DV)]) for h in range(n_heads)], axis=1)

    _for_chunks(n_chunks // cpi, phase1)
    _for_chunks(n_chunks // cp2, phase2)

    @pl.when(tb == n_tb - 1)
    def _():
        sout_ref[...] = st[...]
        last = tb_rows if valid == c else valid
        for s in range(nb):
            cout_ref[s] = jnp.concatenate(
                [xc[s, j, pad + last - tail:pad + last, :] for j in range(n_slabs)], axis=1)


def _gdn(proj, n_seq, nb, t_len, tb_rows, chunk, valid, conv_buf, s0, conv_w, alog, dtb, nw):
    n_tb = t_len // tb_rows
    assert nb == 1 or n_tb == 1
    rows = nb * tb_rows

    def rowblk(b, t):
        return b * n_tb + t

    body = functools.partial(_gdn_body, nb=nb, tb_rows=tb_rows, chunk=chunk, valid=valid, n_tb=n_tb)
    return pl.pallas_call(
        body,
        grid=(n_seq // nb, n_tb),
        in_specs=[
            pl.BlockSpec((rows, GDN_CONV_CH), lambda b, t: (rowblk(b, t), COL_QKV // GDN_CONV_CH)),
            pl.BlockSpec((rows, GDN_V_W), lambda b, t: (rowblk(b, t), COL_Z // GDN_V_W)),
            pl.BlockSpec((rows, SM_W), lambda b, t: (rowblk(b, t), COL_SM // SM_W)),
            pl.BlockSpec((nb, CONV_WIDTH - 1, GDN_CONV_CH), lambda b, t: (b, 0, 0)),
            pl.BlockSpec((nb, GDN_HEADS, GDN_DK, GDN_DV), lambda b, t: (b, 0, 0, 0)),
            pl.BlockSpec((CONV_WIDTH, GDN_CONV_CH), lambda b, t: (0, 0)),
            pl.BlockSpec((1, SM_W), lambda b, t: (0, 0)),
            pl.BlockSpec((1, SM_W), lambda b, t: (0, 0)),
            pl.BlockSpec((1, GDN_DV), lambda b, t: (0, 0)),
        ],
        out_specs=[
            pl.BlockSpec((rows, GDN_V_W), lambda b, t: (rowblk(b, t), 0)),
            pl.BlockSpec((nb, GDN_HEADS, GDN_DK, GDN_DV), lambda b, t: (b, 0, 0, 0)),
            pl.BlockSpec((nb, CONV_WIDTH - 1, GDN_CONV_CH), lambda b, t: (b, 0, 0)),
        ],
        out_shape=[
            jax.ShapeDtypeStruct((n_seq * t_len, GDN_V_W), F32),
            jax.ShapeDtypeStruct((n_seq, GDN_HEADS, GDN_DK, GDN_DV), F32),
            jax.ShapeDtypeStruct((n_seq, CONV_WIDTH - 1, GDN_CONV_CH), F32),
        ],
        scratch_shapes=[
            pltpu.VMEM((nb, GDN_HEADS, GDN_DK, GDN_DV), F32),
            pltpu.VMEM((nb, GDN_CONV_CH // LANE, tb_rows + SUBLANE, LANE), F32),
            pltpu.VMEM((GDN_CONV_CH // LANE, rows, LANE), F32),
            pltpu.VMEM((rows, SM_W), F32),
            pltpu.VMEM((rows, GDN_V_W), F32),
            pltpu.VMEM((rows, GDN_V_W), F32),
            pltpu.VMEM((rows, GDN_QK_W), F32),
            pltpu.VMEM((rows, GDN_QK_W), F32),
            pltpu.VMEM((GDN_HEADS, rows, chunk), F32),
        ],
        compiler_params=pltpu.CompilerParams(dimension_semantics=("arbitrary", "arbitrary"),
                                             vmem_limit_bytes=VMEM_LIMIT),
        name="gdn_mixer",
    )(proj, proj, proj, conv_buf, s0, conv_w, alog, dtb, nw)


def _gla_body(q_ref, k_ref, v_ref, go_ref, sm_ref, s0_ref, wgk_ref, bgk_ref, nw_ref,
              o_ref, sout_ref, st, qes, ois, upds, decs, *, nb, tb_rows, chunk, valid, n_tb):
    tb = pl.program_id(1)
    c = chunk
    n_heads = GLA_HEADS
    units = [(s, h) for s in range(nb) for h in range(n_heads)]

    @pl.when(tb == 0)
    def _():
        st[...] = s0_ref[...]

    ii = lax.broadcasted_iota(jnp.int32, (c, c), 0)
    jj = lax.broadcasted_iota(jnp.int32, (c, c), 1)
    lower = (ii >= jj)
    lower_f = lower.astype(F32)
    rid = lax.broadcasted_iota(jnp.int32, (c, 1), 0)
    rowmask = (rid < valid) if valid < c else None
    n_sub = max(c // 16, 1)
    sub = c // n_sub

    n_chunks = tb_rows // c
    cpi = next(k for k in (GLA_CHUNKS_PER_TRIP, 2, 1) if n_chunks % k == 0)
    p1_units = [(g, h) for g in range(nb * cpi) for h in range(n_heads)]

    def phase1(ci, carry):
        rows, slots, bcs, bc_ts = [], [], [], []
        for g in range(nb * cpi):
            chunk_idx = ci * cpi + g % cpi
            rr = pl.ds(_chunk_rows(g // cpi, tb_rows, chunk_idx, c), c)
            slots.append((g // cpi) * n_chunks + chunk_idx)
            gk = jax.nn.log_sigmoid(_dot(sm_ref[rr, :], wgk_ref[...]) + bgk_ref[...]) / GLA_GATE_NORMALIZER
            if rowmask is not None:
                gk = jnp.where(rowmask, gk, 0.0)
            bc = _dot_hi(lower_f, gk)
            rows.append(rr)
            bcs.append(bc)
            bc_ts.append(bc.T)
        q, k, v, bch = {}, {}, {}, {}
        for (s, h) in p1_units:
            ks = slice(h * GLA_DK, (h + 1) * GLA_DK)
            vs = slice(h * GLA_DV, (h + 1) * GLA_DV)
            q[s, h] = q_ref[rows[s], ks] * (GLA_DK ** -0.5)
            kk = k_ref[rows[s], ks]
            vv = v_ref[rows[s], vs]
            if rowmask is not None:
                kk = jnp.where(rowmask, kk, 0.0)
                vv = jnp.where(rowmask, vv, 0.0)
            k[s, h], v[s, h] = kk, vv
            bch[s, h] = bcs[s][:, ks]
        for (g, h) in p1_units:
            qes[h, rows[g], :] = q[g, h] * jnp.exp(bch[g, h])
        a = {}
        for u in p1_units:
            q_parts, k_parts = [], []
            for sb in range(n_sub):
                ref_row = bch[u][sb * sub:sb * sub + 1, :]
                in_blk = (rid >= sb * sub) & (rid < (sb + 1) * sub)
                q_parts.append(jnp.where(in_blk, q[u] * jnp.exp(jnp.where(in_blk, bch[u] - ref_row, 0.0)), 0.0))
                k_parts.append(k[u] * jnp.exp(jnp.where(rid < (sb + 1) * sub, ref_row - bch[u], 0.0)))
            q_hat = jnp.concatenate(q_parts, axis=1) if n_sub > 1 else q_parts[0]
            k_hat = jnp.concatenate(k_parts, axis=1) if n_sub > 1 else k_parts[0]
            a[u] = jnp.where(lower, _dot_nt(q_hat, k_hat), 0.0)
        upd = {u: _dot_tn(k[u] * jnp.exp(bch[u][c - 1:c, :] - bch[u]), v[u]) for u in p1_units}
        o_intra = {u: _dot(a[u], v[u]) for u in p1_units}
        for (g, h) in p1_units:
            dec_col = bc_ts[g][h * GLA_DK:(h + 1) * GLA_DK, c - 1:c]
            decs[slots[g], h] = jnp.broadcast_to(jnp.exp(dec_col), (GLA_DK, GLA_DV))
            upds[slots[g], h] = upd[g, h]
            ois[rows[g], h * GLA_DV:(h + 1) * GLA_DV] = o_intra[g, h]
        return carry

    cp2 = next(k for k in (SCAN_CHUNKS_PER_TRIP, 1) if n_chunks % k == 0)

    def phase2(ti, carry):
        for u in range(cp2):
            scan_chunk(ti * cp2 + u)
        return carry

    def scan_chunk(ci):
        rows = [pl.ds(_chunk_rows(s, tb_rows, ci, c), c) for s in range(nb)]
        o = {(s, h): ois[rows[s], h * GLA_DV:(h + 1) * GLA_DV] + _dot(qes[h, rows[s], :], st[s, h])
             for (s, h) in units}
        for (s, h) in units:
            st[s, h] = decs[s * n_chunks + ci, h] * st[s, h] + upds[s * n_chunks + ci, h]
        for s in range(nb):
            o_ref[rows[s], :] = jnp.concatenate(
                [_gated_norm(o[s, h], nw_ref[...], go_ref[rows[s], h * GLA_DV:(h + 1) * GLA_DV])
                 for h in range(n_heads)], axis=1)

    _for_chunks(n_chunks // cpi, phase1)
    _for_chunks(n_chunks // cp2, phase2)

    @pl.when(tb == n_tb - 1)
    def _():
        sout_ref[...] = st[...]


def _gla(proj, n_seq, nb, t_len, tb_rows, chunk, valid, s0, wgk, bgk, nw):
    n_tb = t_len // tb_rows
    assert nb == 1 or n_tb == 1
    rows = nb * tb_rows

    def rowblk(b, t):
        return b * n_tb + t

    body = functools.partial(_gla_body, nb=nb, tb_rows=tb_rows, chunk=chunk, valid=valid, n_tb=n_tb)
    return pl.pallas_call(
        body,
        grid=(n_seq // nb, n_tb),
        in_specs=[
            pl.BlockSpec((rows, GLA_QK_W), lambda b, t: (rowblk(b, t), COL_GQ // GLA_QK_W)),
            pl.BlockSpec((rows, GLA_QK_W), lambda b, t: (rowblk(b, t), COL_GK // GLA_QK_W)),
            pl.BlockSpec((rows, GLA_V_W), lambda b, t: (rowblk(b, t), COL_GV // GLA_V_W)),
            pl.BlockSpec((rows, GLA_V_W), lambda b, t: (rowblk(b, t), COL_GG // GLA_V_W)),
            pl.BlockSpec((rows, SM_W), lambda b, t: (rowblk(b, t), COL_SM // SM_W)),
            pl.BlockSpec((nb, GLA_HEADS, GLA_DK, GLA_DV), lambda b, t: (b, 0, 0, 0)),
            pl.BlockSpec((SM_W, GLA_QK_W), lambda b, t: (0, 0)),
            pl.BlockSpec((1, GLA_QK_W), lambda b, t: (0, 0)),
            pl.BlockSpec((1, GLA_DV), lambda b, t: (0, 0)),
        ],
        out_specs=[
            pl.BlockSpec((rows, GLA_V_W), lambda b, t: (rowblk(b, t), 0)),
            pl.BlockSpec((nb, GLA_HEADS, GLA_DK, GLA_DV), lambda b, t: (b, 0, 0, 0)),
        ],
        out_shape=[
            jax.ShapeDtypeStruct((n_seq * t_len, GLA_V_W), F32),
            jax.ShapeDtypeStruct((n_seq, GLA_HEADS, GLA_DK, GLA_DV), F32),
        ],
        scratch_shapes=[
            pltpu.VMEM((nb, GLA_HEADS, GLA_DK, GLA_DV), F32),
            pltpu.VMEM((GLA_HEADS, rows, GLA_DK), F32),
            pltpu.VMEM((rows, GLA_V_W), F32),
            pltpu.VMEM((rows // chunk, GLA_HEADS, GLA_DK, GLA_DV), F32),
            pltpu.VMEM((rows // chunk, GLA_HEADS, GLA_DK, GLA_DV), F32),
        ],
        compiler_params=pltpu.CompilerParams(dimension_semantics=("arbitrary", "arbitrary"),
                                             vmem_limit_bytes=VMEM_LIMIT),
        name="gla_mixer",
    )(proj, proj, proj, proj, proj, s0, wgk, bgk, nw)


def _outproj_body(ogp_ref, ogs_ref, olp_ref, ols_ref, xp_ref, xs_ref, wo_ref, g_ref, wr_ref, br_ref,
                  x1_ref, h2_ref, rt_ref, rtt_ref, cnt_ref, base, *, n_p_blocks):
    i = pl.program_id(0)

    @pl.when(i == 0)
    def _():
        base[...] = jnp.zeros_like(base)

    o = jnp.concatenate([_group_pick(i, n_p_blocks, ogp_ref, ogs_ref),
                         _group_pick(i, n_p_blocks, olp_ref, ols_ref)], axis=1)
    x1 = _group_pick(i, n_p_blocks, xp_ref, xs_ref) + jnp.dot(o.astype(BF16), wo_ref[...],
                                                               preferred_element_type=F32)
    x1_ref[...] = x1
    h = _rms(x1, g_ref[...])
    _store_token_tiles(h2_ref, _pack_bf16_pairs(h))
    logits = _dot_3pass(h, wr_ref[...]) + br_ref[...]

    tm = logits.shape[0]
    lt = logits.T[:N_EXPERTS]
    eid = lax.broadcasted_iota(jnp.int32, (N_EXPERTS, tm), 0)
    work = lt
    sel = jnp.zeros((N_EXPERTS, tm), F32)
    hits, ids, vals = [], [], []
    for _ in range(TOP_K):
        m = jnp.max(work, axis=0, keepdims=True)
        idx = jnp.min(jnp.where(work == m, eid, N_EXPERTS), axis=0, keepdims=True)
        hit = eid == idx
        hits.append(hit)
        ids.append(idx)
        vals.append(m)
        work = jnp.where(hit, -jnp.inf, work)
        sel = sel + hit.astype(F32)
    exps = [jnp.exp(v - vals[0]) for v in vals]
    den = exps[0]
    for e in exps[1:]:
        den = den + e
    gates = [e / den for e in exps]

    ri = lax.broadcasted_iota(jnp.int32, (tm, tm), 0)
    ci = lax.broadcasted_iota(jnp.int32, (tm, tm), 1)
    before = _dot(sel, (ri < ci).astype(F32)) + base[...]
    ranks = [jnp.sum(jnp.where(hit, before, 0.0), axis=0, keepdims=True) for hit in hits]
    base[...] = base[...] + jnp.sum(sel, axis=1, keepdims=True)
    cnt_ref[...] = base[...]

    row = lax.broadcasted_iota(jnp.int32, (LANE, tm), 0)
    rec = jnp.zeros((LANE, tm), F32)
    for k in range(TOP_K):
        rec = jnp.where(row == k, ids[k].astype(F32), rec)
        rec = jnp.where(row == TOP_K + k, ranks[k], rec)
        rec = jnp.where(row == 2 * TOP_K + k, gates[k], rec)
    rt_ref[...] = rec.T
    rtt_ref[...] = rec[:2 * TOP_K]


def _outproj(og_p, og_s, ol_p, ol_s, x_p, x_s, wo, g, wr, br):
    n_p_blocks, n_s_blocks = x_p.shape[0] // ROW_TILE, x_s.shape[0] // ROW_TILE
    n = x_p.shape[0] + x_s.shape[0]
    return pl.pallas_call(
        functools.partial(_outproj_body, n_p_blocks=n_p_blocks),
        grid=(n_p_blocks + n_s_blocks,),
        in_specs=_group_specs(ROW_TILE, GDN_V_W, n_p_blocks) + _group_specs(ROW_TILE, GLA_V_W, n_p_blocks)
        + _group_specs(ROW_TILE, D_MODEL, n_p_blocks) + [
            pl.BlockSpec((D_MODEL, D_MODEL), lambda i: (0, 0)),
            pl.BlockSpec((1, D_MODEL), lambda i: (0, 0)),
            pl.BlockSpec((D_MODEL, LANE), lambda i: (0, 0)),
            pl.BlockSpec((1, LANE), lambda i: (0, 0)),
        ],
        out_specs=[
            pl.BlockSpec((ROW_TILE, D_MODEL), lambda i: (i, 0)),
            pl.BlockSpec((ROW_TILE * PACK_TILES, LANE), lambda i: (i, 0)),
            pl.BlockSpec((ROW_TILE, LANE), lambda i: (i, 0)),
            pl.BlockSpec((2 * TOP_K, ROW_TILE), lambda i: (0, i)),
            pl.BlockSpec((N_EXPERTS, 1), lambda i: (0, 0)),
        ],
        out_shape=[
            jax.ShapeDtypeStruct((n, D_MODEL), F32),
            jax.ShapeDtypeStruct((n * PACK_TILES, LANE), jnp.uint32),
            jax.ShapeDtypeStruct((n, LANE), F32),
            jax.ShapeDtypeStruct((2 * TOP_K, n), F32),
            jax.ShapeDtypeStruct((N_EXPERTS, 1), F32),
        ],
        scratch_shapes=[pltpu.VMEM((N_EXPERTS, 1), F32)],
        compiler_params=pltpu.CompilerParams(dimension_semantics=("arbitrary",),
                                             vmem_limit_bytes=VMEM_LIMIT),
        name="out_proj",
    )(og_p, og_s, ol_p, ol_s, x_p, x_s, wo, g, wr, br)


def _store_token_tiles(ref2d, val):
    rows, tiles = val.shape[0], val.shape[1] // LANE
    for c in range(tiles):
        ref2d[pl.ds(c, rows, stride=tiles), :] = val[:, c * LANE:(c + 1) * LANE]


def _load_token_tiles(ref2d, first_row, rows, tiles=TOK_TILES):
    return jnp.concatenate(
        [ref2d[pl.ds(first_row * tiles + c, rows, stride=tiles), :] for c in range(tiles)], axis=1)


def _pack_bf16_pairs(x):
    half = x.shape[1] // 2
    bits = lax.bitcast_convert_type(x.astype(BF16).astype(F32), jnp.uint32)
    return (bits[:, :half] >> 16) | (bits[:, half:] & jnp.uint32(0xFFFF0000))


def _unpack_bf16_pairs(w):
    lo = lax.bitcast_convert_type(w << 16, F32)
    hi = lax.bitcast_convert_type(w & jnp.uint32(0xFFFF0000), F32)
    return jnp.concatenate([lo, hi], axis=1).astype(BF16)


def _expert_weight_copies(e, ws, wup_hbm, wdn_hbm, wup_buf, wdn_buf, wsems):
    return (pltpu.make_async_copy(wup_hbm.at[e], wup_buf.at[ws], wsems.at[ws]),
            pltpu.make_async_copy(wdn_hbm.at[e], wdn_buf.at[ws], wsems.at[ws]))


def _expert_body(be_ref, nu_ref, first_ref, wslot_ref, next_ref, valid_ref, x_ref,
                 wup_hbm, bup_ref, wdn_hbm, bdn_ref, y_ref, wup_buf, wdn_buf, wsems, wup_bf, wdn_bf):
    i = pl.program_id(0)
    n_used = nu_ref[0]
    ws = wslot_ref[i]
    weight_copies = functools.partial(_expert_weight_copies, wup_hbm=wup_hbm, wdn_hbm=wdn_hbm, wup_buf=wup_buf,
                                      wdn_buf=wdn_buf, wsems=wsems)

    @pl.when((i == 0) & (n_used > 0))
    def _():
        for cp in weight_copies(be_ref[0], ws):
            cp.start(priority=EXPERT_WEIGHT_QUEUE)

    @pl.when(i < n_used)
    def _():
        @pl.when(first_ref[i] == 1)
        def _():
            for cp in weight_copies(be_ref[i], ws):
                cp.wait()

            @pl.when(next_ref[i] >= 0)
            def _():
                for cp in weight_copies(next_ref[i], 1 - ws):
                    cp.start(priority=EXPERT_WEIGHT_QUEUE)

            for r in range(0, D_MODEL, WEIGHT_CAST_ROWS):
                wup_bf[r:r + WEIGHT_CAST_ROWS, :] = wup_buf[ws, r:r + WEIGHT_CAST_ROWS, :].astype(BF16)
            for r in range(0, D_FF, WEIGHT_CAST_ROWS):
                wdn_bf[r:r + WEIGHT_CAST_ROWS, :] = wdn_buf[ws, r:r + WEIGHT_CAST_ROWS, :].astype(BF16)

        def expert_rows(n):
            x = _unpack_bf16_pairs(_load_token_tiles(x_ref, 0, n, PACK_TILES))
            gu = _dot(x, wup_bf[...]) + bup_ref[...]
            gate = jnp.minimum(gu[:, :D_FF], SWIGLU_LIMIT)
            up = jnp.clip(gu[:, D_FF:], -SWIGLU_LIMIT, SWIGLU_LIMIT)
            a = (up + 1.0) * gate * jax.nn.sigmoid(SWIGLU_ALPHA * gate)
            _store_token_tiles(y_ref, _dot(a, wdn_bf[...]) + bdn_ref[...])

        valid = valid_ref[i]
        for n in range(EXPERT_ROW_STEP, EXPERT_ROWS + 1, EXPERT_ROW_STEP):
            @pl.when((valid <= n) if n == EXPERT_ROW_STEP else ((valid > n - EXPERT_ROW_STEP) & (valid <= n)))
            def _(n=n):
                expert_rows(n)
                if n < EXPERT_ROWS:
                    y_ref[n * TOK_TILES:, :] = jnp.zeros(((EXPERT_ROWS - n) * TOK_TILES, LANE), F32)

    @pl.when(i >= n_used)
    def _():
        y_ref[...] = jnp.zeros_like(y_ref)


def _experts(block_meta, xs_2d, w_up, b_up, w_down, b_down):
    n_blocks = block_meta[0].shape[0]
    grid_spec = pltpu.PrefetchScalarGridSpec(
        num_scalar_prefetch=len(block_meta),
        grid=(n_blocks,),
        in_specs=[
            pl.BlockSpec((EXPERT_ROWS * PACK_TILES, LANE), lambda i, *_: (i, 0)),
            pl.BlockSpec(memory_space=pl.ANY),
            pl.BlockSpec((None, 1, 2 * D_FF), lambda i, be, *_: (be[i], 0, 0)),
            pl.BlockSpec(memory_space=pl.ANY),
            pl.BlockSpec((None, 1, D_MODEL), lambda i, be, *_: (be[i], 0, 0)),
        ],
        out_specs=pl.BlockSpec((EXPERT_ROWS * TOK_TILES, LANE), lambda i, *_: (i, 0)),
        scratch_shapes=[
            pltpu.VMEM((2, D_MODEL, 2 * D_FF), F32),
            pltpu.VMEM((2, D_FF, D_MODEL), F32),
            pltpu.SemaphoreType.DMA((2,)),
            pltpu.VMEM((D_MODEL, 2 * D_FF), BF16),
            pltpu.VMEM((D_FF, D_MODEL), BF16),
        ],
    )
    return pl.pallas_call(
        _expert_body,
        grid_spec=grid_spec,
        out_shape=jax.ShapeDtypeStruct((n_blocks * EXPERT_ROWS * TOK_TILES, LANE), F32),
        compiler_params=pltpu.CompilerParams(dimension_semantics=("arbitrary",),
                                             vmem_limit_bytes=VMEM_LIMIT),
        name="experts",
    )(*block_meta, xs_2d, w_up, b_up.reshape(N_EXPERTS, 1, 2 * D_FF), w_down,
      b_down.reshape(N_EXPERTS, 1, D_MODEL))


def _dispatch(h_tiles, dest_kmajor, n_rows):
    n_tok = h_tiles.shape[0]
    info = plsc.get_sparse_core_info()
    n_workers = info.num_cores * info.num_subcores
    per_worker = n_tok // n_workers
    chunk = next(c for c in (128, 96, 88, 64, 48, 32, 16, 8) if per_worker % c == 0)
    assert n_tok % n_workers == 0 and per_worker % SUBLANE == 0
    mesh = plsc.VectorSubcoreMesh(core_axis_name="c", subcore_axis_name="s")

    @functools.partial(
        pl.kernel, mesh=mesh,
        out_type=jax.ShapeDtypeStruct((n_rows,) + h_tiles.shape[1:], h_tiles.dtype),
        scratch_types=[pltpu.VMEM((2, TOP_K, chunk), jnp.int32),
                       pltpu.VMEM((2, chunk) + h_tiles.shape[1:], h_tiles.dtype),
                       pltpu.SemaphoreType.DMA, pltpu.SemaphoreType.DMA],
    )
    def dispatch(h_hbm, dest_hbm, out_hbm, idx_v, rows_v, load_sem, store_sem):
        wid = lax.axis_index("s") * info.num_cores + lax.axis_index("c")

        def start_loads(j, slot):
            t0 = pl.multiple_of(wid * per_worker + j * chunk, SUBLANE)
            cps = [pltpu.async_copy(h_hbm.at[pl.ds(t0, chunk)], rows_v.at[slot], load_sem)]
            for k in range(TOP_K):
                cps.append(pltpu.async_copy(
                    dest_hbm.at[pl.ds(pl.multiple_of(k * n_tok + t0, SUBLANE), chunk)], idx_v.at[slot, k], load_sem))
            return cps

        loads, stores = start_loads(0, 0), []
        for j in range(per_worker // chunk):
            slot = j % 2
            for cp in loads + stores:
                cp.wait()
            loads = start_loads(j + 1, 1 - slot) if j + 1 < per_worker // chunk else []
            stores = [pltpu.async_copy(rows_v.at[slot], out_hbm.at[idx_v.at[slot, k]], store_sem)
                      for k in range(TOP_K)]
        for cp in stores:
            cp.wait()

    return dispatch(h_tiles, dest_kmajor)


def _gather_rows(src_tiles, idx_ref, n_rows, dst2d, sem, priorities):
    def issue(j, carry):
        for u in range(DMA_ISSUE_UNROLL):
            r = j * DMA_ISSUE_UNROLL + u
            dst = dst2d.at[pl.ds(pl.multiple_of(r * TOK_TILES, TOK_TILES), TOK_TILES), :]
            pltpu.make_async_copy(src_tiles.at[idx_ref[0, r]], dst, sem).start(
                priority=priorities[u % len(priorities)])
        return carry

    lax.fori_loop(0, n_rows // DMA_ISSUE_UNROLL, issue, 0)


def _wait_rows(src2d, n_rows, dst2d, sem):
    pltpu.make_async_copy(src2d.at[pl.ds(0, n_rows * TOK_TILES), :], dst2d, sem).wait()


def _combine_body(dest_ref, dest_next_ref, rt_ref, y_tiles, y_2d, x1_ref, g_ref, op_ref, os_ref, ybuf, sems,
                  *, n_p_blocks):
    i = pl.program_id(0)
    slot = i % 2
    n_rows = TOP_K * COMBINE_ROWS

    @pl.when(i == 0)
    def _():
        _gather_rows(y_tiles, dest_ref, n_rows, ybuf.at[0], sems.at[0], COMBINE_GATHER_QUEUES)

    _wait_rows(y_2d, n_rows, ybuf.at[slot], sems.at[slot])

    @pl.when(i + 1 < pl.num_programs(0))
    def _():
        _gather_rows(y_tiles, dest_next_ref, n_rows, ybuf.at[1 - slot], sems.at[1 - slot], COMBINE_GATHER_QUEUES)

    buf = ybuf.at[slot]
    moe = _load_token_tiles(buf, 0, COMBINE_ROWS) * rt_ref[:, 2 * TOP_K:2 * TOP_K + 1]
    for k in range(1, TOP_K):
        moe = moe + _load_token_tiles(buf, k * COMBINE_ROWS, COMBINE_ROWS) * rt_ref[:, 2 * TOP_K + k:2 * TOP_K + k + 1]
    res = _rms(x1_ref[...] + moe, g_ref[...])

    @pl.when(i < n_p_blocks)
    def _():
        op_ref[...] = res

    @pl.when(i >= n_p_blocks)
    def _():
        os_ref[...] = res


def _combine(dest_b, rt, y_2d, x1, g, n_p):
    n = x1.shape[0]
    n_blk = n // COMBINE_ROWS
    n_p_blocks = n_p // COMBINE_ROWS
    dest_blocks = dest_b.reshape(n_blk, 1, TOP_K * COMBINE_ROWS)
    return pl.pallas_call(
        functools.partial(_combine_body, n_p_blocks=n_p_blocks),
        grid=(n_blk,),
        in_specs=[
            pl.BlockSpec((None, 1, COMBINE_ROWS * TOP_K), lambda i: (i, 0, 0), memory_space=pltpu.SMEM),
            pl.BlockSpec((None, 1, COMBINE_ROWS * TOP_K), lambda i: (jnp.minimum(i + 1, n_blk - 1), 0, 0),
                         memory_space=pltpu.SMEM),
            pl.BlockSpec((COMBINE_ROWS, LANE), lambda i: (i, 0)),
            pl.BlockSpec(memory_space=pl.ANY),
            pl.BlockSpec(memory_space=pl.ANY),
            pl.BlockSpec((COMBINE_ROWS, D_MODEL), lambda i: (i, 0)),
            pl.BlockSpec((1, D_MODEL), lambda i: (0, 0)),
        ],
        out_specs=_group_specs(COMBINE_ROWS, D_MODEL, n_p_blocks),
        out_shape=[jax.ShapeDtypeStruct((n_p, D_MODEL), F32), jax.ShapeDtypeStruct((n - n_p, D_MODEL), F32)],
        scratch_shapes=[pltpu.VMEM((2, TOP_K * COMBINE_ROWS * TOK_TILES, LANE), F32),
                        pltpu.SemaphoreType.DMA((2,))],
        compiler_params=pltpu.CompilerParams(dimension_semantics=("arbitrary",),
                                             vmem_limit_bytes=VMEM_LIMIT),
        name="combine",
    )(dest_blocks, dest_blocks, rt, y_2d.reshape(-1, TOK_TILES, LANE), y_2d, x1, g)


def _plan_body(rtt_ref, cnt_ref, dk_ref, db_ref, meta_ref, pst):
    i = pl.program_id(0)
    sh = _log2(EXPERT_ROWS)
    n_e = N_EXPERTS

    @pl.when(i == 0)
    def _():
        cnt = cnt_ref[...].astype(jnp.int32)
        padded = (((cnt + (EXPERT_ROWS - 1)) >> sh) << sh).astype(F32)
        e_r = lax.broadcasted_iota(jnp.int32, (n_e, n_e), 0)
        e_c = lax.broadcasted_iota(jnp.int32, (n_e, n_e), 1)
        p_t = jnp.broadcast_to(padded, (n_e, n_e)).T
        pend = jnp.sum(jnp.where(e_c <= e_r, p_t, 0.0), axis=1, keepdims=True)
        pst[...] = pend - padded
        has_rows = p_t > 0.0
        group = jnp.sum(jnp.where((e_c <= e_r) & has_rows, 1.0, 0.0), axis=1, keepdims=True) - 1.0
        nxt = jnp.min(jnp.where((e_c > e_r) & has_rows, e_c, n_e), axis=1, keepdims=True)
        nxt = jnp.where(nxt >= n_e, -1, nxt)

        mb = meta_ref.shape[1]
        blk = lax.broadcasted_iota(jnp.int32, (n_e, mb), 1)
        eb = lax.broadcasted_iota(jnp.int32, (n_e, mb), 0)
        first_row = (blk * EXPERT_ROWS).astype(F32)

        def expert_of(row0):
            return jnp.minimum(jnp.sum(jnp.where(pend <= row0, 1, 0), axis=0, keepdims=True), n_e - 1)

        be = expert_of(first_row)
        be_prev = expert_of(first_row - EXPERT_ROWS)
        hit = eb == be
        wslot = jnp.sum(jnp.where(hit, group, 0.0), axis=0, keepdims=True).astype(jnp.int32) & 1
        nx = jnp.sum(jnp.where(hit, nxt, 0), axis=0, keepdims=True)
        n_used = pend[n_e - 1:n_e, :].astype(jnp.int32) >> sh
        lane = lax.broadcasted_iota(jnp.int32, (1, mb), 1)
        first = (((be != be_prev) | (lane == 0)) & (lane < n_used)).astype(jnp.int32)
        cnt_b = jnp.sum(jnp.where(hit, cnt_ref[...], 0.0), axis=0, keepdims=True)
        pst_b = jnp.sum(jnp.where(hit, pend - padded, 0.0), axis=0, keepdims=True)
        valid = jnp.clip(cnt_b - (first_row[0:1, :] - pst_b), 0.0, float(EXPERT_ROWS)).astype(jnp.int32)
        row8 = lax.broadcasted_iota(jnp.int32, (SUBLANE, mb), 0)
        meta = jnp.where(row8 == 0, be, jnp.where(row8 == 1, first, jnp.where(row8 == 2, wslot,
                         jnp.where(row8 == 3, nx, jnp.where(row8 == 4, n_used, valid)))))
        meta_ref[...] = meta

    tm = rtt_ref.shape[1]
    eid = lax.broadcasted_iota(jnp.int32, (n_e, tm), 0).astype(F32)
    row8 = lax.broadcasted_iota(jnp.int32, (SUBLANE, tm), 0)
    d8 = jnp.zeros((SUBLANE, tm), jnp.int32)
    for k in range(TOP_K):
        start = jnp.sum(jnp.where(eid == rtt_ref[k:k + 1, :], pst[...], 0.0), axis=0, keepdims=True)
        d8 = jnp.where(row8 == k, (start + rtt_ref[TOP_K + k:TOP_K + k + 1, :]).astype(jnp.int32), d8)
    dk_ref[...] = d8[:TOP_K]
    for b in range(tm // COMBINE_ROWS):
        db_ref[b] = d8[:TOP_K, b * COMBINE_ROWS:(b + 1) * COMBINE_ROWS]


def _plan(rtt, cnt):
    n = rtt.shape[1]
    n_rows = n * TOP_K + N_EXPERTS * EXPERT_ROWS
    n_blocks = n_rows // EXPERT_ROWS
    mb = -(-n_blocks // LANE) * LANE
    tile = max(t for t in range(COMBINE_ROWS, PLAN_TILE_MAX + 1, COMBINE_ROWS) if n % t == 0)
    dk, db, meta = pl.pallas_call(
        _plan_body,
        grid=(n // tile,),
        in_specs=[pl.BlockSpec((2 * TOP_K, tile), lambda i: (0, i)),
                  pl.BlockSpec((N_EXPERTS, 1), lambda i: (0, 0))],
        out_specs=[pl.BlockSpec((TOP_K, tile), lambda i: (0, i)),
                   pl.BlockSpec((tile // COMBINE_ROWS, TOP_K, COMBINE_ROWS), lambda i: (i, 0, 0)),
                   pl.BlockSpec((SUBLANE, mb), lambda i: (0, 0))],
        out_shape=[jax.ShapeDtypeStruct((TOP_K, n), jnp.int32),
                   jax.ShapeDtypeStruct((n // COMBINE_ROWS, TOP_K, COMBINE_ROWS), jnp.int32),
                   jax.ShapeDtypeStruct((SUBLANE, mb), jnp.int32)],
        scratch_shapes=[pltpu.VMEM((N_EXPERTS, 1), F32)],
        compiler_params=pltpu.CompilerParams(dimension_semantics=("arbitrary",)),
        name="plan",
    )(rtt, cnt)
    block_meta = (meta[0, :n_blocks], meta[4, 0:1], meta[1, :n_blocks], meta[2, :n_blocks], meta[3, :n_blocks],
                  meta[5, :n_blocks])
    return dk, db, n_rows, block_meta


def _pad_lanes(v, width):
    return jnp.zeros((1, width), F32).at[0, :v.shape[0]].set(v.astype(F32))


def kernel(x_prompt, x_sample, state_gdn_conv, state_gdn, state_gla, rms_mix_w, w_in, conv_w, gdn_a_log,
           gdn_dt_bias, gdn_norm_w, gla_gk_w, gla_gk_b, gla_norm_w, w_out, rms_ffn_w, w_router, b_router,
           w_up, b_up, w_down, b_down, rms_final_w):
    bp, tp, d = x_prompt.shape
    bs, ts, _ = x_sample.shape
    n_p, n_s = bp * tp, bs * ts
    assert d == D_MODEL and state_gdn.shape[0] == 1, "single-layer kernel"
    assert tp >= CONV_WIDTH - 1 and ts >= CONV_WIDTH - 1, "new conv state is taken from the new tokens only"
    l = 0

    wi = w_in[l]
    a0 = GDN_CONV_CH + GDN_V_W
    g0 = a0 + 2 * GDN_HEADS
    lr0 = g0 + 2 * GLA_QK_W + 2 * GLA_V_W
    small = jnp.concatenate([wi[:, a0:a0 + 2 * GDN_HEADS], wi[:, lr0:lr0 + GLA_GATE_RANK],
                             jnp.zeros((d, SM_W - 2 * GDN_HEADS - GLA_GATE_RANK), F32)], axis=1)
    w_big = jnp.concatenate([wi[:, :a0], wi[:, g0:lr0], small], axis=1).astype(BF16)
    alog = _pad_lanes(gdn_a_log[l], SM_W)
    dtb = _pad_lanes(gdn_dt_bias[l], SM_W)
    wgk = jnp.zeros((SM_W, GLA_QK_W), F32).at[SM_LR:SM_LR + GLA_GATE_RANK].set(gla_gk_w[l])
    wr = jnp.zeros((d, LANE), F32).at[:, :N_EXPERTS].set(w_router[l])
    br = jnp.full((1, LANE), -1e30, F32).at[0, :N_EXPERTS].set(b_router[l])

    assert n_p % ROW_TILE == 0 and n_s % ROW_TILE == 0
    x_p, x_s = x_prompt.reshape(n_p, d), x_sample.reshape(n_s, d)
    proj = _inproj(x_p, x_s, rms_mix_w[l][None, :], w_big)

    tb_p = PROMPT_TIME_BLOCK
    zeros_conv = jnp.zeros((bp, CONV_WIDTH - 1, GDN_CONV_CH), F32)
    og_p, gdn_p, conv_p = _gdn(proj, bp, 1, tp, tb_p, CHUNK, CHUNK, zeros_conv,
                               jnp.zeros((bp, GDN_HEADS, GDN_DK, GDN_DV), F32), conv_w[l], alog, dtb,
                               gdn_norm_w[l][None, :])
    ol_p, gla_p = _gla(proj, bp, 1, tp, tb_p, CHUNK, CHUNK, jnp.zeros((bp, GLA_HEADS, GLA_DK, GLA_DV), F32),
                       wgk, gla_gk_b[l][None, :], gla_norm_w[l][None, :])

    ts_pad = SUBLANE
    nb_s = SAMPLE_SEQS_PER_STEP
    proj_s = proj[n_p:].reshape(bs, ts, PROJ_W)
    proj_sp = jnp.pad(proj_s, ((0, 0), (0, ts_pad - ts), (0, 0))).reshape(bs * ts_pad, PROJ_W)
    og_s, gdn_s, conv_s = _gdn(proj_sp, bs, nb_s, ts_pad, ts_pad, ts_pad, ts, state_gdn_conv[l], state_gdn[l],
                               conv_w[l], alog, dtb, gdn_norm_w[l][None, :])
    ol_s, gla_s = _gla(proj_sp, bs, nb_s, ts_pad, ts_pad, ts_pad, ts, state_gla[l], wgk, gla_gk_b[l][None, :],
                       gla_norm_w[l][None, :])
    og_s = og_s.reshape(bs, ts_pad, GDN_V_W)[:, :ts].reshape(n_s, GDN_V_W)
    ol_s = ol_s.reshape(bs, ts_pad, GLA_V_W)[:, :ts].reshape(n_s, GLA_V_W)

    x1, h2, rt, rtt, cnt = _outproj(og_p, og_s, ol_p, ol_s, x_p, x_s, w_out[l].astype(BF16),
                                    rms_ffn_w[l][None, :], wr, br)

    dest_k, dest_b, n_rows, block_meta = _plan(rtt, cnt)
    xs = _dispatch(h2.reshape(-1, PACK_TILES, LANE), dest_k.reshape(-1), n_rows)
    y_rows = _experts(block_meta, xs.reshape(-1, LANE), w_up[l], b_up[l], w_down[l], b_down[l])
    y_p, y_s = _combine(dest_b, rt, y_rows, x1, rms_final_w[None, :], n_p)
    y_prompt = y_p.reshape(bp, tp, d)
    y_sample = y_s.reshape(bs, ts, d)
    return (y_prompt, y_sample, conv_p[None], gdn_p[None], gla_p[None], conv_s[None], gdn_s[None], gla_s[None])
```

```python
import functools

import jax
import jax.numpy as jnp
from jax import lax
from jax.experimental import pallas as pl
from jax.experimental.pallas import tpu as pltpu
from jax.experimental.pallas import tpu_sc as plsc

F32 = jnp.float32
BF16 = jnp.bfloat16
HI = lax.Precision.HIGHEST

D_MODEL = 1024
GDN_HEADS = 4
GDN_DK = 128
GDN_DV = 128
GLA_HEADS = 4
GLA_DK = 64
GLA_DV = 128
GLA_GATE_RANK = 16
GLA_GATE_NORMALIZER = 16.0
CONV_WIDTH = 4
CHUNK = 64
N_EXPERTS = 32
TOP_K = 4
D_FF = 1024
SWIGLU_LIMIT = 7.0
SWIGLU_ALPHA = 1.702
RMS_EPS = 1e-6
L2_EPS = 1e-6

GDN_QK_W = GDN_HEADS * GDN_DK
GDN_V_W = GDN_HEADS * GDN_DV
GDN_CONV_CH = 2 * GDN_QK_W + GDN_V_W
GLA_QK_W = GLA_HEADS * GLA_DK
GLA_V_W = GLA_HEADS * GLA_DV

COL_QKV = 0
COL_Z = 1536
COL_GQ = 2048
COL_GK = 2304
COL_GV = 2560
COL_GG = 3072
COL_SM = 3584
SM_W = 128
PROJ_W = COL_SM + SM_W
SM_A, SM_B, SM_LR = 0, 4, 8

LANE = 128
SUBLANE = 8
TOK_TILES = D_MODEL // LANE
PACK_TILES = TOK_TILES // 2
ROW_TILE = 512
EXPERT_ROWS = 512
EXPERT_ROW_STEP = 64
EXPERT_WEIGHT_QUEUE = 1
WEIGHT_CAST_ROWS = 128
COMBINE_ROWS = 512
PLAN_TILE_MAX = 2048
DMA_ISSUE_UNROLL = 64
COMBINE_GATHER_QUEUES = (0, 1)
CONV_ROW_SLAB = 128
GDN_CHUNKS_PER_TRIP = 8
GLA_CHUNKS_PER_TRIP = 4
SCAN_CHUNKS_PER_TRIP = 8
PROMPT_TIME_BLOCK = 512
SAMPLE_SEQS_PER_STEP = 16
VMEM_LIMIT = 56 * 1024 * 1024


def _dot(a, b):
    return jnp.dot(a.astype(BF16), b.astype(BF16), preferred_element_type=F32)


def _dot_nt(a, b):
    return lax.dot_general(a.astype(BF16), b.astype(BF16), (((1,), (1,)), ((), ())),
                           preferred_element_type=F32)


def _dot_tn(a, b):
    return lax.dot_general(a.astype(BF16), b.astype(BF16), (((0,), (0,)), ((), ())),
                           preferred_element_type=F32)


def _dot_hi(a, b):
    return jnp.dot(a, b, precision=HI, preferred_element_type=F32)


def _dot_3pass(a, b):
    a_hi = a.astype(BF16)
    b_hi = b.astype(BF16)
    a_lo = (a - a_hi.astype(F32)).astype(BF16)
    b_lo = (b - b_hi.astype(F32)).astype(BF16)

    def mm(x, y):
        return jnp.dot(x, y, preferred_element_type=F32)

    return (mm(a_lo, b_hi) + mm(a_hi, b_lo)) + mm(a_hi, b_hi)


def _rms(x, w):
    return x * lax.rsqrt(jnp.mean(x * x, axis=-1, keepdims=True) + RMS_EPS) * w


def _silu(x):
    return x * jax.nn.sigmoid(x)


def _group_specs(rows, width, n_p_blocks):
    return [pl.BlockSpec((rows, width), lambda i: (jnp.minimum(i, n_p_blocks - 1), 0)),
            pl.BlockSpec((rows, width), lambda i: (jnp.maximum(i - n_p_blocks, 0), 0))]


def _group_pick(i, n_p_blocks, p_ref, s_ref):
    return jnp.where(i < n_p_blocks, p_ref[...], s_ref[...])


def _inproj_body(xp_ref, xs_ref, g_ref, w_ref, o_ref, *, n_p_blocks):
    x = _group_pick(pl.program_id(0), n_p_blocks, xp_ref, xs_ref)
    h = _rms(x, g_ref[...])
    o_ref[...] = jnp.dot(h.astype(BF16), w_ref[...], preferred_element_type=F32)


def _inproj(x_p, x_s, g, w):
    n_p_blocks, n_s_blocks = x_p.shape[0] // ROW_TILE, x_s.shape[0] // ROW_TILE
    n = x_p.shape[0] + x_s.shape[0]
    return pl.pallas_call(
        functools.partial(_inproj_body, n_p_blocks=n_p_blocks),
        grid=(n_p_blocks + n_s_blocks,),
        in_specs=_group_specs(ROW_TILE, D_MODEL, n_p_blocks) + [
            pl.BlockSpec((1, D_MODEL), lambda i: (0, 0)),
            pl.BlockSpec((D_MODEL, PROJ_W), lambda i: (0, 0)),
        ],
        out_specs=pl.BlockSpec((ROW_TILE, PROJ_W), lambda i: (i, 0)),
        out_shape=jax.ShapeDtypeStruct((n, PROJ_W), F32),
        compiler_params=pltpu.CompilerParams(dimension_semantics=("arbitrary",),
                                             vmem_limit_bytes=VMEM_LIMIT),
        name="in_proj",
    )(x_p, x_s, g, w)


def _log2(n):
    assert n & (n - 1) == 0
    return n.bit_length() - 1


def _tri_inv_all(ms, c, ii, jj):
    eye = (ii == jj).astype(F32)
    base = min(c, 8)
    sh = _log2(base)
    blk = (ii >> sh) == (jj >> sh)
    ns = [jnp.where(blk, m, 0.0) for m in ms]
    xs = [eye - n for n in ns]
    ps = [_dot(n, n) for n in ns]
    ts = [_dot(jnp.concatenate([x, p], axis=0), p) for x, p in zip(xs, ps)]
    xs = [x + t[:c] for x, t in zip(xs, ts)]
    ps = [t[c:] for t in ts]
    xs = [x + _dot(x, p) for x, p in zip(xs, ps)]
    s = base
    while s < c:
        sh_s, sh_b = _log2(s), _log2(2 * s)
        off = ((ii >> sh_b) == (jj >> sh_b)) & ((ii >> sh_s) != (jj >> sh_s))
        ys = [_dot(x, jnp.where(off, m, 0.0)) for x, m in zip(xs, ms)]
        xs = [x - _dot(y, x) for x, y in zip(xs, ys)]
        s *= 2
    return xs


def _gated_norm(o, w, z):
    return o * lax.rsqrt(jnp.mean(o * o, axis=-1, keepdims=True) + RMS_EPS) * w * _silu(z)


def _chunk_rows(s, tb_rows, ci, c):
    r = s * tb_rows + ci * c
    if not isinstance(r, int):
        r = pl.multiple_of(r, c)
    return r


def _for_chunks(n_chunks, step):
    if n_chunks == 1:
        step(0, 0)
    else:
        lax.fori_loop(0, n_chunks, step, 0)


def _gdn_body(qkv_ref, z_ref, sm_ref, cbuf_ref, s0_ref, cw_ref, alog_ref, dtb_ref, nw_ref,
              o_ref, sout_ref, cout_ref, st, xc, act, gcs, us, wss, qgs, kds, aqs,
              *, nb, tb_rows, chunk, valid, n_tb):
    tb = pl.program_id(1)
    c = chunk
    n_heads = GDN_HEADS
    tail = CONV_WIDTH - 1
    pad = SUBLANE
    units = [(s, h) for s in range(nb) for h in range(n_heads)]

    n_slabs = GDN_CONV_CH // LANE

    def lanes(j):
        return slice(j * LANE, (j + 1) * LANE)

    @pl.when(tb == 0)
    def _():
        st[...] = s0_ref[...]
        for s in range(nb):
            for j in range(n_slabs):
                xc[s, j, pad - tail:pad, :] = cbuf_ref[s, :, lanes(j)]

    if n_tb > 1:
        @pl.when(tb > 0)
        def _():
            for s in range(nb):
                for j in range(n_slabs):
                    xc[s, j, pad - tail:pad, :] = xc[s, j, tb_rows + pad - tail:tb_rows + pad, :]

    for s in range(nb):
        for j in range(n_slabs):
            xc[s, j, pad:pad + tb_rows, :] = qkv_ref[s * tb_rows:(s + 1) * tb_rows, lanes(j)]

    row_slab = min(tb_rows, CONV_ROW_SLAB)
    parities = 2 if row_slab >= 2 * SUBLANE else 1
    for s in range(nb):
        for j in range(n_slabs):
            src, dst = xc.at[s, j], act.at[j]
            for sl in range(tb_rows // row_slab):
                for p in range(parities):
                    lo = pad - tail + sl * row_slab + p
                    out0 = s * tb_rows + sl * row_slab + p

                    def rows_from(start):
                        if parities == 1:
                            return pl.ds(start, row_slab)
                        return pl.ds(start, row_slab // 2, stride=2)

                    acc = src[rows_from(lo), :] * cw_ref[0:1, lanes(j)]
                    for i in range(1, CONV_WIDTH):
                        acc = acc + src[rows_from(lo + i), :] * cw_ref[i:i + 1, lanes(j)]
                    dst[rows_from(out0), :] = _silu(acc)

    ii = lax.broadcasted_iota(jnp.int32, (c, c), 0)
    jj = lax.broadcasted_iota(jnp.int32, (c, c), 1)
    lower = (ii >= jj)
    lower_f = lower.astype(F32)
    strict = (ii > jj)
    rowmask = None
    if valid < c:
        rowmask = lax.broadcasted_iota(jnp.int32, (c, 1), 0) < valid

    def hs(h, w):
        return slice(h * w, (h + 1) * w)

    n_chunks = tb_rows // c
    cpi = next(k for k in (GDN_CHUNKS_PER_TRIP, 2, 1) if n_chunks % k == 0)
    p1_units = [(g, h) for g in range(nb * cpi) for h in range(n_heads)]

    def phase1(ci, carry):
        rows, b_ts, gc_ts, gc_tts = [], [], [], []
        for g in range(nb * cpi):
            rr = pl.ds(_chunk_rows(g // cpi, tb_rows, ci * cpi + g % cpi, c), c)
            sm = sm_ref[rr, :]
            g_t = -jnp.exp(alog_ref[...]) * jax.nn.softplus(sm + dtb_ref[...])
            b_t = jax.nn.sigmoid(sm)
            if rowmask is not None:
                g_t = jnp.where(rowmask, g_t, 0.0)
                b_t = jnp.where(rowmask, b_t, 0.0)
            gc_t = _dot_hi(lower_f, g_t)
            gcs[rr, :] = gc_t
            rows.append(rr)
            b_ts.append(b_t)
            gc_ts.append(gc_t)
            gc_tts.append(gc_t.T)
        qn, kn, kb, vb = {}, {}, {}, {}
        for (s, h) in p1_units:
            q = act[h, rows[s], :]
            k = act[n_heads + h, rows[s], :]
            v = act[2 * n_heads + h, rows[s], :]
            if rowmask is not None:
                q = jnp.where(rowmask, q, 0.0)
                k = jnp.where(rowmask, k, 0.0)
                v = jnp.where(rowmask, v, 0.0)
            qn[s, h] = q * lax.rsqrt(jnp.sum(q * q, axis=-1, keepdims=True) + L2_EPS) * (GDN_DK ** -0.5)
            kn[s, h] = k * lax.rsqrt(jnp.sum(k * k, axis=-1, keepdims=True) + L2_EPS)
            beta = b_ts[s][:, SM_B + h:SM_B + h + 1]
            kb[s, h] = kn[s, h] * beta
            vb[s, h] = v * beta
        s1 = {u: _dot_nt(jnp.concatenate([kb[u], qn[u]], axis=0), kn[u]) for u in p1_units}
        mm = []
        for (s, h) in p1_units:
            gcol = gc_ts[s][:, SM_A + h:SM_A + h + 1]
            grow = gc_tts[s][SM_A + h:SM_A + h + 1, :]
            dec = jnp.exp(jnp.where(lower, gcol - grow, -jnp.inf))
            mm.append(jnp.where(strict, s1[s, h][:c] * dec, 0.0))
            aqs[h, rows[s], :] = s1[s, h][c:] * dec
        tms = _tri_inv_all(mm, c, ii, jj)
        for (s, h), tm in zip(p1_units, tms):
            gcol = gc_ts[s][:, SM_A + h:SM_A + h + 1]
            eg = jnp.exp(gcol)
            uw = _dot(tm, jnp.concatenate([vb[s, h], kb[s, h] * eg], axis=1))
            us[rows[s], hs(h, GDN_DV)] = uw[:, :GDN_DV]
            wss[rows[s], hs(h, GDN_DV)] = uw[:, GDN_DV:]
            qgs[rows[s], hs(h, GDN_DK)] = qn[s, h] * eg
            kds[rows[s], hs(h, GDN_DK)] = kn[s, h] * jnp.exp(gcol[c - 1:c, :] - gcol)
        return carry

    cp2 = next(k for k in (SCAN_CHUNKS_PER_TRIP, 1) if n_chunks % k == 0)

    def phase2(ti, carry):
        for u in range(cp2):
            scan_chunk(ti * cp2 + u)
        return carry

    def scan_chunk(ci):
        r0 = [_chunk_rows(s, tb_rows, ci, c) for s in range(nb)]
        rows = [pl.ds(r, c) for r in r0]
        ws = {(s, h): _dot(jnp.concatenate([wss[rows[s], hs(h, GDN_DV)], qgs[rows[s], hs(h, GDN_DK)]], axis=0),
                           st[s, h]) for (s, h) in units}
        v_new = {(s, h): us[rows[s], hs(h, GDN_DV)] - ws[s, h][:c] for (s, h) in units}
        o = {(s, h): ws[s, h][c:] + _dot(aqs[h, rows[s], :], v_new[s, h]) for (s, h) in units}
        upd = {(s, h): _dot_tn(kds[rows[s], hs(h, GDN_DK)], v_new[s, h]) for (s, h) in units}
        for (s, h) in units:
            g_last = gcs[pl.ds(r0[s] + c - 1, 1), SM_A + h:SM_A + h + 1]
            st[s, h] = st[s, h] * jnp.exp(g_last) + upd[s, h]
        for s in range(nb):
            o_ref[rows[s], :] = jnp.concatenate(
                [_gated_norm(o[s, h], nw_ref[...], z_ref[rows[s], hs(h, GDN_DV)]) for h in range(n_heads)], axis=1)

    _for_chunks(n_chunks // cpi, phase1)
    _for_chunks(n_chunks // cp2, phase2)

    @pl.when(tb == n_tb - 1)
    def _():
        sout_ref[...] = st[...]
        last = tb_rows if valid == c else valid
        for s in range(nb):
            cout_ref[s] = jnp.concatenate(
                [xc[s, j, pad + last - tail:pad + last, :] for j in range(n_slabs)], axis=1)


def _gdn(proj, n_seq, nb, t_len, tb_rows, chunk, valid, conv_buf, s0, conv_w, alog, dtb, nw):
    n_tb = t_len // tb_rows
    assert nb == 1 or n_tb == 1
    rows = nb * tb_rows

    def rowblk(b, t):
        return b * n_tb + t

    body = functools.partial(_gdn_body, nb=nb, tb_rows=tb_rows, chunk=chunk, valid=valid, n_tb=n_tb)
    return pl.pallas_call(
        body,
        grid=(n_seq // nb, n_tb),
        in_specs=[
            pl.BlockSpec((rows, GDN_CONV_CH), lambda b, t: (rowblk(b, t), COL_QKV // GDN_CONV_CH)),
            pl.BlockSpec((rows, GDN_V_W), lambda b, t: (rowblk(b, t), COL_Z // GDN_V_W)),
            pl.BlockSpec((rows, SM_W), lambda b, t: (rowblk(b, t), COL_SM // SM_W)),
            pl.BlockSpec((nb, CONV_WIDTH - 1, GDN_CONV_CH), lambda b, t: (b, 0, 0)),
            pl.BlockSpec((nb, GDN_HEADS, GDN_DK, GDN_DV), lambda b, t: (b, 0, 0, 0)),
            pl.BlockSpec((CONV_WIDTH, GDN_CONV_CH), lambda b, t: (0, 0)),
            pl.BlockSpec((1, SM_W), lambda b, t: (0, 0)),
            pl.BlockSpec((1, SM_W), lambda b, t: (0, 0)),
            pl.BlockSpec((1, GDN_DV), lambda b, t: (0, 0)),
        ],
        out_specs=[
            pl.BlockSpec((rows, GDN_V_W), lambda b, t: (rowblk(b, t), 0)),
            pl.BlockSpec((nb, GDN_HEADS, GDN_DK, GDN_DV), lambda b, t: (b, 0, 0, 0)),
            pl.BlockSpec((nb, CONV_WIDTH - 1, GDN_CONV_CH), lambda b, t: (b, 0, 0)),
        ],
        out_shape=[
            jax.ShapeDtypeStruct((n_seq * t_len, GDN_V_W), F32),
            jax.ShapeDtypeStruct((n_seq, GDN_HEADS, GDN_DK, GDN_DV), F32),
            jax.ShapeDtypeStruct((n_seq, CONV_WIDTH - 1, GDN_CONV_CH), F32),
        ],
        scratch_shapes=[
            pltpu.VMEM((nb, GDN_HEADS, GDN_DK, GDN_DV), F32),
            pltpu.VMEM((nb, GDN_CONV_CH // LANE, tb_rows + SUBLANE, LANE), F32),
            pltpu.VMEM((GDN_CONV_CH // LANE, rows, LANE), F32),
            pltpu.VMEM((rows, SM_W), F32),
            pltpu.VMEM((rows, GDN_V_W), F32),
            pltpu.VMEM((rows, GDN_V_W), F32),
            pltpu.VMEM((rows, GDN_QK_W), F32),
            pltpu.VMEM((rows, GDN_QK_W), F32),
            pltpu.VMEM((GDN_HEADS, rows, chunk), F32),
        ],
        compiler_params=pltpu.CompilerParams(dimension_semantics=("arbitrary", "arbitrary"),
                                             vmem_limit_bytes=VMEM_LIMIT),
        name="gdn_mixer",
    )(proj, proj, proj, conv_buf, s0, conv_w, alog, dtb, nw)


def _gla_body(q_ref, k_ref, v_ref, go_ref, sm_ref, s0_ref, wgk_ref, bgk_ref, nw_ref,
              o_ref, sout_ref, st, qes, ois, upds, decs, *, nb, tb_rows, chunk, valid, n_tb):
    tb = pl.program_id(1)
    c = chunk
    n_heads = GLA_HEADS
    units = [(s, h) for s in range(nb) for h in range(n_heads)]

    @pl.when(tb == 0)
    def _():
        st[...] = s0_ref[...]

    ii = lax.broadcasted_iota(jnp.int32, (c, c), 0)
    jj = lax.broadcasted_iota(jnp.int32, (c, c), 1)
    lower = (ii >= jj)
    lower_f = lower.astype(F32)
    rid = lax.broadcasted_iota(jnp.int32, (c, 1), 0)
    rowmask = (rid < valid) if valid < c else None
    n_sub = max(c // 16, 1)
    sub = c // n_sub

    n_chunks = tb_rows // c
    cpi = next(k for k in (GLA_CHUNKS_PER_TRIP, 2, 1) if n_chunks % k == 0)
    p1_units = [(g, h) for g in range(nb * cpi) for h in range(n_heads)]

    def phase1(ci, carry):
        rows, slots, bcs, bc_ts = [], [], [], []
        for g in range(nb * cpi):
            chunk_idx = ci * cpi + g % cpi
            rr = pl.ds(_chunk_rows(g // cpi, tb_rows, chunk_idx, c), c)
            slots.append((g // cpi) * n_chunks + chunk_idx)
            gk = jax.nn.log_sigmoid(_dot(sm_ref[rr, :], wgk_ref[...]) + bgk_ref[...]) / GLA_GATE_NORMALIZER
            if rowmask is not None:
                gk = jnp.where(rowmask, gk, 0.0)
            bc = _dot_hi(lower_f, gk)
            rows.append(rr)
            bcs.append(bc)
            bc_ts.append(bc.T)
        q, k, v, bch = {}, {}, {}, {}
        for (s, h) in p1_units:
            ks = slice(h * GLA_DK, (h + 1) * GLA_DK)
            vs = slice(h * GLA_DV, (h + 1) * GLA_DV)
            q[s, h] = q_ref[rows[s], ks] * (GLA_DK ** -0.5)
            kk = k_ref[rows[s], ks]
            vv = v_ref[rows[s], vs]
            if rowmask is not None:
                kk = jnp.where(rowmask, kk, 0.0)
                vv = jnp.where(rowmask, vv, 0.0)
            k[s, h], v[s, h] = kk, vv
            bch[s, h] = bcs[s][:, ks]
        for (g, h) in p1_units:
            qes[h, rows[g], :] = q[g, h] * jnp.exp(bch[g, h])
        a = {}
        for u in p1_units:
            q_parts, k_parts = [], []
            for sb in range(n_sub):
                ref_row = bch[u][sb * sub:sb * sub + 1, :]
                in_blk = (rid >= sb * sub) & (rid < (sb + 1) * sub)
                q_parts.append(jnp.where(in_blk, q[u] * jnp.exp(jnp.where(in_blk, bch[u] - ref_row, 0.0)), 0.0))
                k_parts.append(k[u] * jnp.exp(jnp.where(rid < (sb + 1) * sub, ref_row - bch[u], 0.0)))
            q_hat = jnp.concatenate(q_parts, axis=1) if n_sub > 1 else q_parts[0]
            k_hat = jnp.concatenate(k_parts, axis=1) if n_sub > 1 else k_parts[0]
            a[u] = jnp.where(lower, _dot_nt(q_hat, k_hat), 0.0)
        upd = {u: _dot_tn(k[u] * jnp.exp(bch[u][c - 1:c, :] - bch[u]), v[u]) for u in p1_units}
        o_intra = {u: _dot(a[u], v[u]) for u in p1_units}
        for (g, h) in p1_units:
            dec_col = bc_ts[g][h * GLA_DK:(h + 1) * GLA_DK, c - 1:c]
            decs[slots[g], h] = jnp.broadcast_to(jnp.exp(dec_col), (GLA_DK, GLA_DV))
            upds[slots[g], h] = upd[g, h]
            ois[rows[g], h * GLA_DV:(h + 1) * GLA_DV] = o_intra[g, h]
        return carry

    cp2 = next(k for k in (SCAN_CHUNKS_PER_TRIP, 1) if n_chunks % k == 0)

    def phase2(ti, carry):
        for u in range(cp2):
            scan_chunk(ti * cp2 + u)
        return carry

    def scan_chunk(ci):
        rows = [pl.ds(_chunk_rows(s, tb_rows, ci, c), c) for s in range(nb)]
        o = {(s, h): ois[rows[s], h * GLA_DV:(h + 1) * GLA_DV] + _dot(qes[h, rows[s], :], st[s, h])
             for (s, h) in units}
        for (s, h) in units:
            st[s, h] = decs[s * n_chunks + ci, h] * st[s, h] + upds[s * n_chunks + ci, h]
        for s in range(nb):
            o_ref[rows[s], :] = jnp.concatenate(
                [_gated_norm(o[s, h], nw_ref[...], go_ref[rows[s], h * GLA_DV:(h + 1) * GLA_DV])
                 for h in range(n_heads)], axis=1)

    _for_chunks(n_chunks // cpi, phase1)
    _for_chunks(n_chunks // cp2, phase2)

    @pl.when(tb == n_tb - 1)
    def _():
        sout_ref[...] = st[...]


def _gla(proj, n_seq, nb, t_len, tb_rows, chunk, valid, s0, wgk, bgk, nw):
    n_tb = t_len // tb_rows
    assert nb == 1 or n_tb == 1
    rows = nb * tb_rows

    def rowblk(b, t):
        return b * n_tb + t

    body = functools.partial(_gla_body, nb=nb, tb_rows=tb_rows, chunk=chunk, valid=valid, n_tb=n_tb)
    return pl.pallas_call(
        body,
        grid=(n_seq // nb, n_tb),
        in_specs=[
            pl.BlockSpec((rows, GLA_QK_W), lambda b, t: (rowblk(b, t), COL_GQ // GLA_QK_W)),
            pl.BlockSpec((rows, GLA_QK_W), lambda b, t: (rowblk(b, t), COL_GK // GLA_QK_W)),
            pl.BlockSpec((rows, GLA_V_W), lambda b, t: (rowblk(b, t), COL_GV // GLA_V_W)),
            pl.BlockSpec((rows, GLA_V_W), lambda b, t: (rowblk(b, t), COL_GG // GLA_V_W)),
            pl.BlockSpec((rows, SM_W), lambda b, t: (rowblk(b, t), COL_SM // SM_W)),
            pl.BlockSpec((nb, GLA_HEADS, GLA_DK, GLA_DV), lambda b, t: (b, 0, 0, 0)),
            pl.BlockSpec((SM_W, GLA_QK_W), lambda b, t: (0, 0)),
            pl.BlockSpec((1, GLA_QK_W), lambda b, t: (0, 0)),
            pl.BlockSpec((1, GLA_DV), lambda b, t: (0, 0)),
        ],
        out_specs=[
            pl.BlockSpec((rows, GLA_V_W), lambda b, t: (rowblk(b, t), 0)),
            pl.BlockSpec((nb, GLA_HEADS, GLA_DK, GLA_DV), lambda b, t: (b, 0, 0, 0)),
        ],
        out_shape=[
            jax.ShapeDtypeStruct((n_seq * t_len, GLA_V_W), F32),
            jax.ShapeDtypeStruct((n_seq, GLA_HEADS, GLA_DK, GLA_DV), F32),
        ],
        scratch_shapes=[
            pltpu.VMEM((nb, GLA_HEADS, GLA_DK, GLA_DV), F32),
            pltpu.VMEM((GLA_HEADS, rows, GLA_DK), F32),
            pltpu.VMEM((rows, GLA_V_W), F32),
            pltpu.VMEM((rows // chunk, GLA_HEADS, GLA_DK, GLA_DV), F32),
            pltpu.VMEM((rows // chunk, GLA_HEADS, GLA_DK, GLA_DV), F32),
        ],
        compiler_params=pltpu.CompilerParams(dimension_semantics=("arbitrary", "arbitrary"),
                                             vmem_limit_bytes=VMEM_LIMIT),
        name="gla_mixer",
    )(proj, proj, proj, proj, proj, s0, wgk, bgk, nw)


def _outproj_body(ogp_ref, ogs_ref, olp_ref, ols_ref, xp_ref, xs_ref, wo_ref, g_ref, wr_ref, br_ref,
                  x1_ref, h2_ref, rt_ref, rtt_ref, cnt_ref, base, *, n_p_blocks):
    i = pl.program_id(0)

    @pl.when(i == 0)
    def _():
        base[...] = jnp.zeros_like(base)

    o = jnp.concatenate([_group_pick(i, n_p_blocks, ogp_ref, ogs_ref),
                         _group_pick(i, n_p_blocks, olp_ref, ols_ref)], axis=1)
    x1 = _group_pick(i, n_p_blocks, xp_ref, xs_ref) + jnp.dot(o.astype(BF16), wo_ref[...],
                                                               preferred_element_type=F32)
    x1_ref[...] = x1
    h = _rms(x1, g_ref[...])
    _store_token_tiles(h2_ref, _pack_bf16_pairs(h))
    logits = _dot_3pass(h, wr_ref[...]) + br_ref[...]

    tm = logits.shape[0]
    lt = logits.T[:N_EXPERTS]
    eid = lax.broadcasted_iota(jnp.int32, (N_EXPERTS, tm), 0)
    work = lt
    sel = jnp.zeros((N_EXPERTS, tm), F32)
    hits, ids, vals = [], [], []
    for _ in range(TOP_K):
        m = jnp.max(work, axis=0, keepdims=True)
        idx = jnp.min(jnp.where(work == m, eid, N_EXPERTS), axis=0, keepdims=True)
        hit = eid == idx
        hits.append(hit)
        ids.append(idx)
        vals.append(m)
        work = jnp.where(hit, -jnp.inf, work)
        sel = sel + hit.astype(F32)
    exps = [jnp.exp(v - vals[0]) for v in vals]
    den = exps[0]
    for e in exps[1:]:
        den = den + e
    gates = [e / den for e in exps]

    ri = lax.broadcasted_iota(jnp.int32, (tm, tm), 0)
    ci = lax.broadcasted_iota(jnp.int32, (tm, tm), 1)
    before = _dot(sel, (ri < ci).astype(F32)) + base[...]
    ranks = [jnp.sum(jnp.where(hit, before, 0.0), axis=0, keepdims=True) for hit in hits]
    base[...] = base[...] + jnp.sum(sel, axis=1, keepdims=True)
    cnt_ref[...] = base[...]

    row = lax.broadcasted_iota(jnp.int32, (LANE, tm), 0)
    rec = jnp.zeros((LANE, tm), F32)
    for k in range(TOP_K):
        rec = jnp.where(row == k, ids[k].astype(F32), rec)
        rec = jnp.where(row == TOP_K + k, ranks[k], rec)
        rec = jnp.where(row == 2 * TOP_K + k, gates[k], rec)
    rt_ref[...] = rec.T
    rtt_ref[...] = rec[:2 * TOP_K]


def _outproj(og_p, og_s, ol_p, ol_s, x_p, x_s, wo, g, wr, br):
    n_p_blocks, n_s_blocks = x_p.shape[0] // ROW_TILE, x_s.shape[0] // ROW_TILE
    n = x_p.shape[0] + x_s.shape[0]
    return pl.pallas_call(
        functools.partial(_outproj_body, n_p_blocks=n_p_blocks),
        grid=(n_p_blocks + n_s_blocks,),
        in_specs=_group_specs(ROW_TILE, GDN_V_W, n_p_blocks) + _group_specs(ROW_TILE, GLA_V_W, n_p_blocks)
        + _group_specs(ROW_TILE, D_MODEL, n_p_blocks) + [
            pl.BlockSpec((D_MODEL, D_MODEL), lambda i: (0, 0)),
            pl.BlockSpec((1, D_MODEL), lambda i: (0, 0)),
            pl.BlockSpec((D_MODEL, LANE), lambda i: (0, 0)),
            pl.BlockSpec((1, LANE), lambda i: (0, 0)),
        ],
        out_specs=[
            pl.BlockSpec((ROW_TILE, D_MODEL), lambda i: (i, 0)),
            pl.BlockSpec((ROW_TILE * PACK_TILES, LANE), lambda i: (i, 0)),
            pl.BlockSpec((ROW_TILE, LANE), lambda i: (i, 0)),
            pl.BlockSpec((2 * TOP_K, ROW_TILE), lambda i: (0, i)),
            pl.BlockSpec((N_EXPERTS, 1), lambda i: (0, 0)),
        ],
        out_shape=[
            jax.ShapeDtypeStruct((n, D_MODEL), F32),
            jax.ShapeDtypeStruct((n * PACK_TILES, LANE), jnp.uint32),
            jax.ShapeDtypeStruct((n, LANE), F32),
            jax.ShapeDtypeStruct((2 * TOP_K, n), F32),
            jax.ShapeDtypeStruct((N_EXPERTS, 1), F32),
        ],
        scratch_shapes=[pltpu.VMEM((N_EXPERTS, 1), F32)],
        compiler_params=pltpu.CompilerParams(dimension_semantics=("arbitrary",),
                                             vmem_limit_bytes=VMEM_LIMIT),
        name="out_proj",
    )(og_p, og_s, ol_p, ol_s, x_p, x_s, wo, g, wr, br)


def _store_token_tiles(ref2d, val):
    rows, tiles = val.shape[0], val.shape[1] // LANE
    for c in range(tiles):
        ref2d[pl.ds(c, rows, stride=tiles), :] = val[:, c * LANE:(c + 1) * LANE]


def _load_token_tiles(ref2d, first_row, rows, tiles=TOK_TILES):
    return jnp.concatenate(
        [ref2d[pl.ds(first_row * tiles + c, rows, stride=tiles), :] for c in range(tiles)], axis=1)


def _pack_bf16_pairs(x):
    half = x.shape[1] // 2
    bits = lax.bitcast_convert_type(x.astype(BF16).astype(F32), jnp.uint32)
    return (bits[:, :half] >> 16) | (bits[:, half:] & jnp.uint32(0xFFFF0000))


def _unpack_bf16_pairs(w):
    lo = lax.bitcast_convert_type(w << 16, F32)
    hi = lax.bitcast_convert_type(w & jnp.uint32(0xFFFF0000), F32)
    return jnp.concatenate([lo, hi], axis=1).astype(BF16)


def _expert_weight_copies(e, ws, wup_hbm, wdn_hbm, wup_buf, wdn_buf, wsems):
    return (pltpu.make_async_copy(wup_hbm.at[e], wup_buf.at[ws], wsems.at[ws]),
            pltpu.make_async_copy(wdn_hbm.at[e], wdn_buf.at[ws], wsems.at[ws]))


def _expert_body(be_ref, nu_ref, first_ref, wslot_ref, next_ref, valid_ref, x_ref,
                 wup_hbm, bup_ref, wdn_hbm, bdn_ref, y_ref, wup_buf, wdn_buf, wsems, wup_bf, wdn_bf):
    i = pl.program_id(0)
    n_used = nu_ref[0]
    ws = wslot_ref[i]
    weight_copies = functools.partial(_expert_weight_copies, wup_hbm=wup_hbm, wdn_hbm=wdn_hbm, wup_buf=wup_buf,
                                      wdn_buf=wdn_buf, wsems=wsems)

    @pl.when((i == 0) & (n_used > 0))
    def _():
        for cp in weight_copies(be_ref[0], ws):
            cp.start(priority=EXPERT_WEIGHT_QUEUE)

    @pl.when(i < n_used)
    def _():
        @pl.when(first_ref[i] == 1)
        def _():
            for cp in weight_copies(be_ref[i], ws):
                cp.wait()

            @pl.when(next_ref[i] >= 0)
            def _():
                for cp in weight_copies(next_ref[i], 1 - ws):
                    cp.start(priority=EXPERT_WEIGHT_QUEUE)

            for r in range(0, D_MODEL, WEIGHT_CAST_ROWS):
                wup_bf[r:r + WEIGHT_CAST_ROWS, :] = wup_buf[ws, r:r + WEIGHT_CAST_ROWS, :].astype(BF16)
            for r in range(0, D_FF, WEIGHT_CAST_ROWS):
                wdn_bf[r:r + WEIGHT_CAST_ROWS, :] = wdn_buf[ws, r:r + WEIGHT_CAST_ROWS, :].astype(BF16)

        def expert_rows(n):
            x = _unpack_bf16_pairs(_load_token_tiles(x_ref, 0, n, PACK_TILES))
            gu = _dot(x, wup_bf[...]) + bup_ref[...]
            gate = jnp.minimum(gu[:, :D_FF], SWIGLU_LIMIT)
            up = jnp.clip(gu[:, D_FF:], -SWIGLU_LIMIT, SWIGLU_LIMIT)
            a = (up + 1.0) * gate * jax.nn.sigmoid(SWIGLU_ALPHA * gate)
            _store_token_tiles(y_ref, _dot(a, wdn_bf[...]) + bdn_ref[...])

        valid = valid_ref[i]
        for n in range(EXPERT_ROW_STEP, EXPERT_ROWS + 1, EXPERT_ROW_STEP):
            @pl.when((valid <= n) if n == EXPERT_ROW_STEP else ((valid > n - EXPERT_ROW_STEP) & (valid <= n)))
            def _(n=n):
                expert_rows(n)
                if n < EXPERT_ROWS:
                    y_ref[n * TOK_TILES:, :] = jnp.zeros(((EXPERT_ROWS - n) * TOK_TILES, LANE), F32)

    @pl.when(i >= n_used)
    def _():
        y_ref[...] = jnp.zeros_like(y_ref)


def _experts(block_meta, xs_2d, w_up, b_up, w_down, b_down):
    n_blocks = block_meta[0].shape[0]
    grid_spec = pltpu.PrefetchScalarGridSpec(
        num_scalar_prefetch=len(block_meta),
        grid=(n_blocks,),
        in_specs=[
            pl.BlockSpec((EXPERT_ROWS * PACK_TILES, LANE), lambda i, *_: (i, 0)),
            pl.BlockSpec(memory_space=pl.ANY),
            pl.BlockSpec((None, 1, 2 * D_FF), lambda i, be, *_: (be[i], 0, 0)),
            pl.BlockSpec(memory_space=pl.ANY),
            pl.BlockSpec((None, 1, D_MODEL), lambda i, be, *_: (be[i], 0, 0)),
        ],
        out_specs=pl.BlockSpec((EXPERT_ROWS * TOK_TILES, LANE), lambda i, *_: (i, 0)),
        scratch_shapes=[
            pltpu.VMEM((2, D_MODEL, 2 * D_FF), F32),
            pltpu.VMEM((2, D_FF, D_MODEL), F32),
            pltpu.SemaphoreType.DMA((2,)),
            pltpu.VMEM((D_MODEL, 2 * D_FF), BF16),
            pltpu.VMEM((D_FF, D_MODEL), BF16),
        ],
    )
    return pl.pallas_call(
        _expert_body,
        grid_spec=grid_spec,
        out_shape=jax.ShapeDtypeStruct((n_blocks * EXPERT_ROWS * TOK_TILES, LANE), F32),
        compiler_params=pltpu.CompilerParams(dimension_semantics=("arbitrary",),
                                             vmem_limit_bytes=VMEM_LIMIT),
        name="experts",
    )(*block_meta, xs_2d, w_up, b_up.reshape(N_EXPERTS, 1, 2 * D_FF), w_down,
      b_down.reshape(N_EXPERTS, 1, D_MODEL))


def _dispatch(h_tiles, dest_kmajor, n_rows):
    n_tok = h_tiles.shape[0]
    info = plsc.get_sparse_core_info()
    n_workers = info.num_cores * info.num_subcores
    per_worker = n_tok // n_workers
    chunk = next(c for c in (128, 96, 88, 64, 48, 32, 16, 8) if per_worker % c == 0)
    assert n_tok % n_workers == 0 and per_worker % SUBLANE == 0
    mesh = plsc.VectorSubcoreMesh(core_axis_name="c", subcore_axis_name="s")

    @functools.partial(
        pl.kernel, mesh=mesh,
        out_type=jax.ShapeDtypeStruct((n_rows,) + h_tiles.shape[1:], h_tiles.dtype),
        scratch_types=[pltpu.VMEM((2, TOP_K, chunk), jnp.int32),
                       pltpu.VMEM((2, chunk) + h_tiles.shape[1:], h_tiles.dtype),
                       pltpu.SemaphoreType.DMA, pltpu.SemaphoreType.DMA],
    )
    def dispatch(h_hbm, dest_hbm, out_hbm, idx_v, rows_v, load_sem, store_sem):
        wid = lax.axis_index("s") * info.num_cores + lax.axis_index("c")

        def start_loads(j, slot):
            t0 = pl.multiple_of(wid * per_worker + j * chunk, SUBLANE)
            cps = [pltpu.async_copy(h_hbm.at[pl.ds(t0, chunk)], rows_v.at[slot], load_sem)]
            for k in range(TOP_K):
                cps.append(pltpu.async_copy(
                    dest_hbm.at[pl.ds(pl.multiple_of(k * n_tok + t0, SUBLANE), chunk)], idx_v.at[slot, k], load_sem))
            return cps

        loads, stores = start_loads(0, 0), []
        for j in range(per_worker // chunk):
            slot = j % 2
            for cp in loads + stores:
                cp.wait()
            loads = start_loads(j + 1, 1 - slot) if j + 1 < per_worker // chunk else []
            stores = [pltpu.async_copy(rows_v.at[slot], out_hbm.at[idx_v.at[slot, k]], store_sem)
                      for k in range(TOP_K)]
        for cp in stores:
            cp.wait()

    return dispatch(h_tiles, dest_kmajor)


def _gather_rows(src_tiles, idx_ref, n_rows, dst2d, sem, priorities):
    def issue(j, carry):
        for u in range(DMA_ISSUE_UNROLL):
            r = j * DMA_ISSUE_UNROLL + u
            dst = dst2d.at[pl.ds(pl.multiple_of(r * TOK_TILES, TOK_TILES), TOK_TILES), :]
            pltpu.make_async_copy(src_tiles.at[idx_ref[0, r]], dst, sem).start(
                priority=priorities[u % len(priorities)])
        return carry

    lax.fori_loop(0, n_rows // DMA_ISSUE_UNROLL, issue, 0)


def _wait_rows(src2d, n_rows, dst2d, sem):
    pltpu.make_async_copy(src2d.at[pl.ds(0, n_rows * TOK_TILES), :], dst2d, sem).wait()


def _combine_body(dest_ref, dest_next_ref, rt_ref, y_tiles, y_2d, x1_ref, g_ref, op_ref, os_ref, ybuf, sems,
                  *, n_p_blocks):
    i = pl.program_id(0)
    slot = i % 2
    n_rows = TOP_K * COMBINE_ROWS

    @pl.when(i == 0)
    def _():
        _gather_rows(y_tiles, dest_ref, n_rows, ybuf.at[0], sems.at[0], COMBINE_GATHER_QUEUES)

    _wait_rows(y_2d, n_rows, ybuf.at[slot], sems.at[slot])

    @pl.when(i + 1 < pl.num_programs(0))
    def _():
        _gather_rows(y_tiles, dest_next_ref, n_rows, ybuf.at[1 - slot], sems.at[1 - slot], COMBINE_GATHER_QUEUES)

    buf = ybuf.at[slot]
    moe = _load_token_tiles(buf, 0, COMBINE_ROWS) * rt_ref[:, 2 * TOP_K:2 * TOP_K + 1]
    for k in range(1, TOP_K):
        moe = moe + _load_token_tiles(buf, k * COMBINE_ROWS, COMBINE_ROWS) * rt_ref[:, 2 * TOP_K + k:2 * TOP_K + k + 1]
    res = _rms(x1_ref[...] + moe, g_ref[...])

    @pl.when(i < n_p_blocks)
    def _():
        op_ref[...] = res

    @pl.when(i >= n_p_blocks)
    def _():
        os_ref[...] = res


def _combine(dest_b, rt, y_2d, x1, g, n_p):
    n = x1.shape[0]
    n_blk = n // COMBINE_ROWS
    n_p_blocks = n_p // COMBINE_ROWS
    dest_blocks = dest_b.reshape(n_blk, 1, TOP_K * COMBINE_ROWS)
    return pl.pallas_call(
        functools.partial(_combine_body, n_p_blocks=n_p_blocks),
        grid=(n_blk,),
        in_specs=[
            pl.BlockSpec((None, 1, COMBINE_ROWS * TOP_K), lambda i: (i, 0, 0), memory_space=pltpu.SMEM),
            pl.BlockSpec((None, 1, COMBINE_ROWS * TOP_K), lambda i: (jnp.minimum(i + 1, n_blk - 1), 0, 0),
                         memory_space=pltpu.SMEM),
            pl.BlockSpec((COMBINE_ROWS, LANE), lambda i: (i, 0)),
            pl.BlockSpec(memory_space=pl.ANY),
            pl.BlockSpec(memory_space=pl.ANY),
            pl.BlockSpec((COMBINE_ROWS, D_MODEL), lambda i: (i, 0)),
            pl.BlockSpec((1, D_MODEL), lambda i: (0, 0)),
        ],
        out_specs=_group_specs(COMBINE_ROWS, D_MODEL, n_p_blocks),
        out_shape=[jax.ShapeDtypeStruct((n_p, D_MODEL), F32), jax.ShapeDtypeStruct((n - n_p, D_MODEL), F32)],
        scratch_shapes=[pltpu.VMEM((2, TOP_K * COMBINE_ROWS * TOK_TILES, LANE), F32),
                        pltpu.SemaphoreType.DMA((2,))],
        compiler_params=pltpu.CompilerParams(dimension_semantics=("arbitrary",),
                                             vmem_limit_bytes=VMEM_LIMIT),
        name="combine",
    )(dest_blocks, dest_blocks, rt, y_2d.reshape(-1, TOK_TILES, LANE), y_2d, x1, g)


def _plan_body(rtt_ref, cnt_ref, dk_ref, db_ref, meta_ref, pst):
    i = pl.program_id(0)
    sh = _log2(EXPERT_ROWS)
    n_e = N_EXPERTS

    @pl.when(i == 0)
    def _():
        cnt = cnt_ref[...].astype(jnp.int32)
        padded = (((cnt + (EXPERT_ROWS - 1)) >> sh) << sh).astype(F32)
        e_r = lax.broadcasted_iota(jnp.int32, (n_e, n_e), 0)
        e_c = lax.broadcasted_iota(jnp.int32, (n_e, n_e), 1)
        p_t = jnp.broadcast_to(padded, (n_e, n_e)).T
        pend = jnp.sum(jnp.where(e_c <= e_r, p_t, 0.0), axis=1, keepdims=True)
        pst[...] = pend - padded
        has_rows = p_t > 0.0
        group = jnp.sum(jnp.where((e_c <= e_r) & has_rows, 1.0, 0.0), axis=1, keepdims=True) - 1.0
        nxt = jnp.min(jnp.where((e_c > e_r) & has_rows, e_c, n_e), axis=1, keepdims=True)
        nxt = jnp.where(nxt >= n_e, -1, nxt)

        mb = meta_ref.shape[1]
        blk = lax.broadcasted_iota(jnp.int32, (n_e, mb), 1)
        eb = lax.broadcasted_iota(jnp.int32, (n_e, mb), 0)
        first_row = (blk * EXPERT_ROWS).astype(F32)

        def expert_of(row0):
            return jnp.minimum(jnp.sum(jnp.where(pend <= row0, 1, 0), axis=0, keepdims=True), n_e - 1)

        be = expert_of(first_row)
        be_prev = expert_of(first_row - EXPERT_ROWS)
        hit = eb == be
        wslot = jnp.sum(jnp.where(hit, group, 0.0), axis=0, keepdims=True).astype(jnp.int32) & 1
        nx = jnp.sum(jnp.where(hit, nxt, 0), axis=0, keepdims=True)
        n_used = pend[n_e - 1:n_e, :].astype(jnp.int32) >> sh
        lane = lax.broadcasted_iota(jnp.int32, (1, mb), 1)
        first = (((be != be_prev) | (lane == 0)) & (lane < n_used)).astype(jnp.int32)
        cnt_b = jnp.sum(jnp.where(hit, cnt_ref[...], 0.0), axis=0, keepdims=True)
        pst_b = jnp.sum(jnp.where(hit, pend - padded, 0.0), axis=0, keepdims=True)
        valid = jnp.clip(cnt_b - (first_row[0:1, :] - pst_b), 0.0, float(EXPERT_ROWS)).astype(jnp.int32)
        row8 = lax.broadcasted_iota(jnp.int32, (SUBLANE, mb), 0)
        meta = jnp.where(row8 == 0, be, jnp.where(row8 == 1, first, jnp.where(row8 == 2, wslot,
                         jnp.where(row8 == 3, nx, jnp.where(row8 == 4, n_used, valid)))))
        meta_ref[...] = meta

    tm = rtt_ref.shape[1]
    eid = lax.broadcasted_iota(jnp.int32, (n_e, tm), 0).astype(F32)
    row8 = lax.broadcasted_iota(jnp.int32, (SUBLANE, tm), 0)
    d8 = jnp.zeros((SUBLANE, tm), jnp.int32)
    for k in range(TOP_K):
        start = jnp.sum(jnp.where(eid == rtt_ref[k:k + 1, :], pst[...], 0.0), axis=0, keepdims=True)
        d8 = jnp.where(row8 == k, (start + rtt_ref[TOP_K + k:TOP_K + k + 1, :]).astype(jnp.int32), d8)
    dk_ref[...] = d8[:TOP_K]
    for b in range(tm // COMBINE_ROWS):
        db_ref[b] = d8[:TOP_K, b * COMBINE_ROWS:(b + 1) * COMBINE_ROWS]


def _plan(rtt, cnt):
    n = rtt.shape[1]
    n_rows = n * TOP_K + N_EXPERTS * EXPERT_ROWS
    n_blocks = n_rows // EXPERT_ROWS
    mb = -(-n_blocks // LANE) * LANE
    tile = max(t for t in range(COMBINE_ROWS, PLAN_TILE_MAX + 1, COMBINE_ROWS) if n % t == 0)
    dk, db, meta = pl.pallas_call(
        _plan_body,
        grid=(n // tile,),
        in_specs=[pl.BlockSpec((2 * TOP_K, tile), lambda i: (0, i)),
                  pl.BlockSpec((N_EXPERTS, 1), lambda i: (0, 0))],
        out_specs=[pl.BlockSpec((TOP_K, tile), lambda i: (0, i)),
                   pl.BlockSpec((tile // COMBINE_ROWS, TOP_K, COMBINE_ROWS), lambda i: (i, 0, 0)),
                   pl.BlockSpec((SUBLANE, mb), lambda i: (0, 0))],
        out_shape=[jax.ShapeDtypeStruct((TOP_K, n), jnp.int32),
                   jax.ShapeDtypeStruct((n // COMBINE_ROWS, TOP_K, COMBINE_ROWS), jnp.int32),
                   jax.ShapeDtypeStruct((SUBLANE, mb), jnp.int32)],
        scratch_shapes=[pltpu.VMEM((N_EXPERTS, 1), F32)],
        compiler_params=pltpu.CompilerParams(dimension_semantics=("arbitrary",)),
        name="plan",
    )(rtt, cnt)
    block_meta = (meta[0, :n_blocks], meta[4, 0:1], meta[1, :n_blocks], meta[2, :n_blocks], meta[3, :n_blocks],
                  meta[5, :n_blocks])
    return dk, db, n_rows, block_meta


def _pad_lanes(v, width):
    return jnp.zeros((1, width), F32).at[0, :v.shape[0]].set(v.astype(F32))


def kernel(x_prompt, x_sample, state_gdn_conv, state_gdn, state_gla, rms_mix_w, w_in, conv_w, gdn_a_log,
           gdn_dt_bias, gdn_norm_w, gla_gk_w, gla_gk_b, gla_norm_w, w_out, rms_ffn_w, w_router, b_router,
           w_up, b_up, w_down, b_down, rms_final_w):
    bp, tp, d = x_prompt.shape
    bs, ts, _ = x_sample.shape
    n_p, n_s = bp * tp, bs * ts
    assert d == D_MODEL and state_gdn.shape[0] == 1, "single-layer kernel"
    assert tp >= CONV_WIDTH - 1 and ts >= CONV_WIDTH - 1, "new conv state is taken from the new tokens only"
    l = 0

    wi = w_in[l]
    a0 = GDN_CONV_CH + GDN_V_W
    g0 = a0 + 2 * GDN_HEADS
    lr0 = g0 + 2 * GLA_QK_W + 2 * GLA_V_W
    small = jnp.concatenate([wi[:, a0:a0 + 2 * GDN_HEADS], wi[:, lr0:lr0 + GLA_GATE_RANK],
                             jnp.zeros((d, SM_W - 2 * GDN_HEADS - GLA_GATE_RANK), F32)], axis=1)
    w_big = jnp.concatenate([wi[:, :a0], wi[:, g0:lr0], small], axis=1).astype(BF16)
    alog = _pad_lanes(gdn_a_log[l], SM_W)
    dtb = _pad_lanes(gdn_dt_bias[l], SM_W)
    wgk = jnp.zeros((SM_W, GLA_QK_W), F32).at[SM_LR:SM_LR + GLA_GATE_RANK].set(gla_gk_w[l])
    wr = jnp.zeros((d, LANE), F32).at[:, :N_EXPERTS].set(w_router[l])
    br = jnp.full((1, LANE), -1e30, F32).at[0, :N_EXPERTS].set(b_router[l])

    assert n_p % ROW_TILE == 0 and n_s % ROW_TILE == 0
    x_p, x_s = x_prompt.reshape(n_p, d), x_sample.reshape(n_s, d)
    proj = _inproj(x_p, x_s, rms_mix_w[l][None, :], w_big)

    tb_p = PROMPT_TIME_BLOCK
    zeros_conv = jnp.zeros((bp, CONV_WIDTH - 1, GDN_CONV_CH), F32)
    og_p, gdn_p, conv_p = _gdn(proj, bp, 1, tp, tb_p, CHUNK, CHUNK, zeros_conv,
                               jnp.zeros((bp, GDN_HEADS, GDN_DK, GDN_DV), F32), conv_w[l], alog, dtb,
                               gdn_norm_w[l][None, :])
    ol_p, gla_p = _gla(proj, bp, 1, tp, tb_p, CHUNK, CHUNK, jnp.zeros((bp, GLA_HEADS, GLA_DK, GLA_DV), F32),
                       wgk, gla_gk_b[l][None, :], gla_norm_w[l][None, :])

    ts_pad = SUBLANE
    nb_s = SAMPLE_SEQS_PER_STEP
    proj_s = proj[n_p:].reshape(bs, ts, PROJ_W)
    proj_sp = jnp.pad(proj_s, ((0, 0), (0, ts_pad - ts), (0, 0))).reshape(bs * ts_pad, PROJ_W)
    og_s, gdn_s, conv_s = _gdn(proj_sp, bs, nb_s, ts_pad, ts_pad, ts_pad, ts, state_gdn_conv[l], state_gdn[l],
                               conv_w[l], alog, dtb, gdn_norm_w[l][None, :])
    ol_s, gla_s = _gla(proj_sp, bs, nb_s, ts_pad, ts_pad, ts_pad, ts, state_gla[l], wgk, gla_gk_b[l][None, :],
                       gla_norm_w[l][None, :])
    og_s = og_s.reshape(bs, ts_pad, GDN_V_W)[:, :ts].reshape(n_s, GDN_V_W)
    ol_s = ol_s.reshape(bs, ts_pad, GLA_V_W)[:, :ts].reshape(n_s, GLA_V_W)

    x1, h2, rt, rtt, cnt = _outproj(og_p, og_s, ol_p, ol_s, x_p, x_s, w_out[l].astype(BF16),
                                    rms_ffn_w[l][None, :], wr, br)

    dest_k, dest_b, n_rows, block_meta = _plan(rtt, cnt)
    xs = _dispatch(h2.reshape(-1, PACK_TILES, LANE), dest_k.reshape(-1), n_rows)
    y_rows = _experts(block_meta, xs.reshape(-1, LANE), w_up[l], b_up[l], w_down[l], b_down[l])
    y_p, y_s = _combine(dest_b, rt, y_rows, x1, rms_final_w[None, :], n_p)
    y_prompt = y_p.reshape(bp, tp, d)
    y_sample = y_s.reshape(bs, ts, d)
    return (y_prompt, y_sample, conv_p[None], gdn_p[None], gla_p[None], conv_s[None], gdn_s[None], gla_s[None])
```

```python
import functools

import jax
import jax.numpy as jnp
from jax import lax
from jax.experimental import pallas as pl
from jax.experimental.pallas import tpu as pltpu
from jax.experimental.pallas import tpu_sc as plsc

F32 = jnp.float32
BF16 = jnp.bfloat16
HI = lax.Precision.HIGHEST

D_MODEL = 1024
GDN_HEADS = 4
GDN_DK = 128
GDN_DV = 128
GLA_HEADS = 4
GLA_DK = 64
GLA_DV = 128
GLA_GATE_RANK = 16
GLA_GATE_NORMALIZER = 16.0
CONV_WIDTH = 4
CHUNK = 64
N_EXPERTS = 32
TOP_K = 4
D_FF = 1024
SWIGLU_LIMIT = 7.0
SWIGLU_ALPHA = 1.702
RMS_EPS = 1e-6
L2_EPS = 1e-6

GDN_QK_W = GDN_HEADS * GDN_DK
GDN_V_W = GDN_HEADS * GDN_DV
GDN_CONV_CH = 2 * GDN_QK_W + GDN_V_W
GLA_QK_W = GLA_HEADS * GLA_DK
GLA_V_W = GLA_HEADS * GLA_DV

COL_QKV = 0
COL_Z = 1536
COL_GQ = 2048
COL_GK = 2304
COL_GV = 2560
COL_GG = 3072
COL_SM = 3584
SM_W = 128
PROJ_W = COL_SM + SM_W
SM_A, SM_B, SM_LR = 0, 4, 8

LANE = 128
SUBLANE = 8
TOK_TILES = D_MODEL // LANE
PACK_TILES = TOK_TILES // 2
ROW_TILE = 512
EXPERT_ROWS = 512
EXPERT_ROW_STEP = 128
EXPERT_WEIGHT_QUEUE = 1
WEIGHT_CAST_ROWS = 128
COMBINE_ROWS = 512
PLAN_TILE_MAX = 2048
DMA_ISSUE_UNROLL = 64
COMBINE_GATHER_QUEUES = (0, 1)
CONV_ROW_SLAB = 128
GDN_CHUNKS_PER_TRIP = 8
GLA_CHUNKS_PER_TRIP = 4
SCAN_CHUNKS_PER_TRIP = 8
PROMPT_TIME_BLOCK = 512
SAMPLE_SEQS_PER_STEP = 16
VMEM_LIMIT = 56 * 1024 * 1024


def _dot(a, b):
    return jnp.dot(a.astype(BF16), b.astype(BF16), preferred_element_type=F32)


def _dot_nt(a, b):
    return lax.dot_general(a.astype(BF16), b.astype(BF16), (((1,), (1,)), ((), ())),
                           preferred_element_type=F32)


def _dot_tn(a, b):
    return lax.dot_general(a.astype(BF16), b.astype(BF16), (((0,), (0,)), ((), ())),
                           preferred_element_type=F32)


def _dot_hi(a, b):
    return jnp.dot(a, b, precision=HI, preferred_element_type=F32)


def _dot_3pass(a, b):
    a_hi = a.astype(BF16)
    b_hi = b.astype(BF16)
    a_lo = (a - a_hi.astype(F32)).astype(BF16)
    b_lo = (b - b_hi.astype(F32)).astype(BF16)

    def mm(x, y):
        return jnp.dot(x, y, preferred_element_type=F32)

    return (mm(a_lo, b_hi) + mm(a_hi, b_lo)) + mm(a_hi, b_hi)


def _rms(x, w):
    return x * lax.rsqrt(jnp.mean(x * x, axis=-1, keepdims=True) + RMS_EPS) * w


def _silu(x):
    return x * jax.nn.sigmoid(x)


def _group_specs(rows, width, n_p_blocks):
    return [pl.BlockSpec((rows, width), lambda i: (jnp.minimum(i, n_p_blocks - 1), 0)),
            pl.BlockSpec((rows, width), lambda i: (jnp.maximum(i - n_p_blocks, 0), 0))]


def _group_pick(i, n_p_blocks, p_ref, s_ref):
    return jnp.where(i < n_p_blocks, p_ref[...], s_ref[...])


def _inproj_body(xp_ref, xs_ref, g_ref, w_ref, o_ref, *, n_p_blocks):
    x = _group_pick(pl.program_id(0), n_p_blocks, xp_ref, xs_ref)
    h = _rms(x, g_ref[...])
    o_ref[...] = jnp.dot(h.astype(BF16), w_ref[...], preferred_element_type=F32)


def _inproj(x_p, x_s, g, w):
    n_p_blocks, n_s_blocks = x_p.shape[0] // ROW_TILE, x_s.shape[0] // ROW_TILE
    n = x_p.shape[0] + x_s.shape[0]
    return pl.pallas_call(
        functools.partial(_inproj_body, n_p_blocks=n_p_blocks),
        grid=(n_p_blocks + n_s_blocks,),
        in_specs=_group_specs(ROW_TILE, D_MODEL, n_p_blocks) + [
            pl.BlockSpec((1, D_MODEL), lambda i: (0, 0)),
            pl.BlockSpec((D_MODEL, PROJ_W), lambda i: (0, 0)),
        ],
        out_specs=pl.BlockSpec((ROW_TILE, PROJ_W), lambda i: (i, 0)),
        out_shape=jax.ShapeDtypeStruct((n, PROJ_W), F32),
        compiler_params=pltpu.CompilerParams(dimension_semantics=("arbitrary",),
                                             vmem_limit_bytes=VMEM_LIMIT),
        name="in_proj",
    )(x_p, x_s, g, w)


def _log2(n):
    assert n & (n - 1) == 0
    return n.bit_length() - 1


def _tri_inv_all(ms, c, ii, jj):
    eye = (ii == jj).astype(F32)
    base = min(c, 8)
    sh = _log2(base)
    blk = (ii >> sh) == (jj >> sh)
    ns = [jnp.where(blk, m, 0.0) for m in ms]
    xs = [eye - n for n in ns]
    ps = [_dot(n, n) for n in ns]
    ts = [_dot(jnp.concatenate([x, p], axis=0), p) for x, p in zip(xs, ps)]
    xs = [x + t[:c] for x, t in zip(xs, ts)]
    ps = [t[c:] for t in ts]
    xs = [x + _dot(x, p) for x, p in zip(xs, ps)]
    s = base
    while s < c:
        sh_s, sh_b = _log2(s), _log2(2 * s)
        off = ((ii >> sh_b) == (jj >> sh_b)) & ((ii >> sh_s) != (jj >> sh_s))
        ys = [_dot(x, jnp.where(off, m, 0.0)) for x, m in zip(xs, ms)]
        xs = [x - _dot(y, x) for x, y in zip(xs, ys)]
        s *= 2
    return xs


def _gated_norm(o, w, z):
    return o * lax.rsqrt(jnp.mean(o * o, axis=-1, keepdims=True) + RMS_EPS) * w * _silu(z)


def _chunk_rows(s, tb_rows, ci, c):
    r = s * tb_rows + ci * c
    if not isinstance(r, int):
        r = pl.multiple_of(r, c)
    return r


def _for_chunks(n_chunks, step):
    if n_chunks == 1:
        step(0, 0)
    else:
        lax.fori_loop(0, n_chunks, step, 0)


def _gdn_body(qkv_ref, z_ref, sm_ref, cbuf_ref, s0_ref, cw_ref, alog_ref, dtb_ref, nw_ref,
              o_ref, sout_ref, cout_ref, st, xc, act, gcs, us, wss, qgs, kds, aqs,
              *, nb, tb_rows, chunk, valid, n_tb):
    tb = pl.program_id(1)
    c = chunk
    n_heads = GDN_HEADS
    tail = CONV_WIDTH - 1
    pad = SUBLANE
    units = [(s, h) for s in range(nb) for h in range(n_heads)]

    n_slabs = GDN_CONV_CH // LANE

    def lanes(j):
        return slice(j * LANE, (j + 1) * LANE)

    @pl.when(tb == 0)
    def _():
        st[...] = s0_ref[...]
        for s in range(nb):
            for j in range(n_slabs):
                xc[s, j, pad - tail:pad, :] = cbuf_ref[s, :, lanes(j)]

    if n_tb > 1:
        @pl.when(tb > 0)
        def _():
            for s in range(nb):
                for j in range(n_slabs):
                    xc[s, j, pad - tail:pad, :] = xc[s, j, tb_rows + pad - tail:tb_rows + pad, :]

    for s in range(nb):
        for j in range(n_slabs):
            xc[s, j, pad:pad + tb_rows, :] = qkv_ref[s * tb_rows:(s + 1) * tb_rows, lanes(j)]

    row_slab = min(tb_rows, CONV_ROW_SLAB)
    parities = 2 if row_slab >= 2 * SUBLANE else 1
    for s in range(nb):
        for j in range(n_slabs):
            src, dst = xc.at[s, j], act.at[j]
            for sl in range(tb_rows // row_slab):
                for p in range(parities):
                    lo = pad - tail + sl * row_slab + p
                    out0 = s * tb_rows + sl * row_slab + p

                    def rows_from(start):
                        if parities == 1:
                            return pl.ds(start, row_slab)
                        return pl.ds(start, row_slab // 2, stride=2)

                    acc = src[rows_from(lo), :] * cw_ref[0:1, lanes(j)]
                    for i in range(1, CONV_WIDTH):
                        acc = acc + src[rows_from(lo + i), :] * cw_ref[i:i + 1, lanes(j)]
                    dst[rows_from(out0), :] = _silu(acc)

    ii = lax.broadcasted_iota(jnp.int32, (c, c), 0)
    jj = lax.broadcasted_iota(jnp.int32, (c, c), 1)
    lower = (ii >= jj)
    lower_f = lower.astype(F32)
    strict = (ii > jj)
    rowmask = None
    if valid < c:
        rowmask = lax.broadcasted_iota(jnp.int32, (c, 1), 0) < valid

    def hs(h, w):
        return slice(h * w, (h + 1) * w)

    n_chunks = tb_rows // c
    cpi = next(k for k in (GDN_CHUNKS_PER_TRIP, 2, 1) if n_chunks % k == 0)
    p1_units = [(g, h) for g in range(nb * cpi) for h in range(n_heads)]

    def phase1(ci, carry):
        rows, b_ts, gc_ts, gc_tts = [], [], [], []
        for g in range(nb * cpi):
            rr = pl.ds(_chunk_rows(g // cpi, tb_rows, ci * cpi + g % cpi, c), c)
            sm = sm_ref[rr, :]
            g_t = -jnp.exp(alog_ref[...]) * jax.nn.softplus(sm + dtb_ref[...])
            b_t = jax.nn.sigmoid(sm)
            if rowmask is not None:
                g_t = jnp.where(rowmask, g_t, 0.0)
                b_t = jnp.where(rowmask, b_t, 0.0)
            gc_t = _dot_hi(lower_f, g_t)
            gcs[rr, :] = gc_t
            rows.append(rr)
            b_ts.append(b_t)
            gc_ts.append(gc_t)
            gc_tts.append(gc_t.T)
        qn, kn, kb, vb = {}, {}, {}, {}
        for (s, h) in p1_units:
            q = act[h, rows[s], :]
            k = act[n_heads + h, rows[s], :]
            v = act[2 * n_heads + h, rows[s], :]
            if rowmask is not None:
                q = jnp.where(rowmask, q, 0.0)
                k = jnp.where(rowmask, k, 0.0)
                v = jnp.where(rowmask, v, 0.0)
            qn[s, h] = q * lax.rsqrt(jnp.sum(q * q, axis=-1, keepdims=True) + L2_EPS) * (GDN_DK ** -0.5)
            kn[s, h] = k * lax.rsqrt(jnp.sum(k * k, axis=-1, keepdims=True) + L2_EPS)
            beta = b_ts[s][:, SM_B + h:SM_B + h + 1]
            kb[s, h] = kn[s, h] * beta
            vb[s, h] = v * beta
        s1 = {u: _dot_nt(jnp.concatenate([kb[u], qn[u]], axis=0), kn[u]) for u in p1_units}
        mm = []
        for (s, h) in p1_units:
            gcol = gc_ts[s][:, SM_A + h:SM_A + h + 1]
            grow = gc_tts[s][SM_A + h:SM_A + h + 1, :]
            dec = jnp.exp(jnp.where(lower, gcol - grow, -jnp.inf))
            mm.append(jnp.where(strict, s1[s, h][:c] * dec, 0.0))
            aqs[h, rows[s], :] = (s1[s, h][c:] * dec).astype(aqs.dtype)
        tms = _tri_inv_all(mm, c, ii, jj)
        for (s, h), tm in zip(p1_units, tms):
            gcol = gc_ts[s][:, SM_A + h:SM_A + h + 1]
            eg = jnp.exp(gcol)
            uw = _dot(tm, jnp.concatenate([vb[s, h], kb[s, h] * eg], axis=1))
            us[rows[s], hs(h, GDN_DV)] = uw[:, :GDN_DV]
            wss[rows[s], hs(h, GDN_DV)] = uw[:, GDN_DV:].astype(wss.dtype)
            qgs[rows[s], hs(h, GDN_DK)] = (qn[s, h] * eg).astype(qgs.dtype)
            kds[rows[s], hs(h, GDN_DK)] = (kn[s, h] * jnp.exp(gcol[c - 1:c, :] - gcol)).astype(kds.dtype)
        return carry

    cp2 = next(k for k in (SCAN_CHUNKS_PER_TRIP, 1) if n_chunks % k == 0)

    def phase2(ti, carry):
        for u in range(cp2):
            scan_chunk(ti * cp2 + u)
        return carry

    def scan_chunk(ci):
        r0 = [_chunk_rows(s, tb_rows, ci, c) for s in range(nb)]
        rows = [pl.ds(r, c) for r in r0]
        ws = {(s, h): _dot(jnp.concatenate([wss[rows[s], hs(h, GDN_DV)], qgs[rows[s], hs(h, GDN_DK)]], axis=0),
                           st[s, h]) for (s, h) in units}
        v_new = {(s, h): us[rows[s], hs(h, GDN_DV)] - ws[s, h][:c] for (s, h) in units}
        o = {(s, h): ws[s, h][c:] + _dot(aqs[h, rows[s], :], v_new[s, h]) for (s, h) in units}
        upd = {(s, h): _dot_tn(kds[rows[s], hs(h, GDN_DK)], v_new[s, h]) for (s, h) in units}
        for (s, h) in units:
            g_last = gcs[pl.ds(r0[s] + c - 1, 1), SM_A + h:SM_A + h + 1]
            st[s, h] = st[s, h] * jnp.exp(g_last) + upd[s, h]
        for s in range(nb):
            o_ref[rows[s], :] = jnp.concatenate(
                [_gated_norm(o[s, h], nw_ref[...], z_ref[rows[s], hs(h, GDN_DV)]) for h in range(n_heads)], axis=1)

    _for_chunks(n_chunks // cpi, phase1)
    _for_chunks(n_chunks // cp2, phase2)

    @pl.when(tb == n_tb - 1)
    def _():
        sout_ref[...] = st[...]
        last = tb_rows if valid == c else valid
        for s in range(nb):
            cout_ref[s] = jnp.concatenate(
                [xc[s, j, pad + last - tail:pad + last, :] for j in range(n_slabs)], axis=1)


def _gdn(proj, n_seq, nb, t_len, tb_rows, chunk, valid, conv_buf, s0, conv_w, alog, dtb, nw):
    op_dt = BF16 if chunk % (2 * SUBLANE) == 0 else F32
    n_tb = t_len // tb_rows
    assert nb == 1 or n_tb == 1
    rows = nb * tb_rows

    def rowblk(b, t):
        return b * n_tb + t

    body = functools.partial(_gdn_body, nb=nb, tb_rows=tb_rows, chunk=chunk, valid=valid, n_tb=n_tb)
    return pl.pallas_call(
        body,
        grid=(n_seq // nb, n_tb),
        in_specs=[
            pl.BlockSpec((rows, GDN_CONV_CH), lambda b, t: (rowblk(b, t), COL_QKV // GDN_CONV_CH)),
            pl.BlockSpec((rows, GDN_V_W), lambda b, t: (rowblk(b, t), COL_Z // GDN_V_W)),
            pl.BlockSpec((rows, SM_W), lambda b, t: (rowblk(b, t), COL_SM // SM_W)),
            pl.BlockSpec((nb, CONV_WIDTH - 1, GDN_CONV_CH), lambda b, t: (b, 0, 0)),
            pl.BlockSpec((nb, GDN_HEADS, GDN_DK, GDN_DV), lambda b, t: (b, 0, 0, 0)),
            pl.BlockSpec((CONV_WIDTH, GDN_CONV_CH), lambda b, t: (0, 0)),
            pl.BlockSpec((1, SM_W), lambda b, t: (0, 0)),
            pl.BlockSpec((1, SM_W), lambda b, t: (0, 0)),
            pl.BlockSpec((1, GDN_DV), lambda b, t: (0, 0)),
        ],
        out_specs=[
            pl.BlockSpec((rows, GDN_V_W), lambda b, t: (rowblk(b, t), 0)),
            pl.BlockSpec((nb, GDN_HEADS, GDN_DK, GDN_DV), lambda b, t: (b, 0, 0, 0)),
            pl.BlockSpec((nb, CONV_WIDTH - 1, GDN_CONV_CH), lambda b, t: (b, 0, 0)),
        ],
        out_shape=[
            jax.ShapeDtypeStruct((n_seq * t_len, GDN_V_W), F32),
            jax.ShapeDtypeStruct((n_seq, GDN_HEADS, GDN_DK, GDN_DV), F32),
            jax.ShapeDtypeStruct((n_seq, CONV_WIDTH - 1, GDN_CONV_CH), F32),
        ],
        scratch_shapes=[
            pltpu.VMEM((nb, GDN_HEADS, GDN_DK, GDN_DV), F32),
            pltpu.VMEM((nb, GDN_CONV_CH // LANE, tb_rows + SUBLANE, LANE), F32),
            pltpu.VMEM((GDN_CONV_CH // LANE, rows, LANE), F32),
            pltpu.VMEM((rows, SM_W), F32),
            pltpu.VMEM((rows, GDN_V_W), F32),
            pltpu.VMEM((rows, GDN_V_W), op_dt),
            pltpu.VMEM((rows, GDN_QK_W), op_dt),
            pltpu.VMEM((rows, GDN_QK_W), op_dt),
            pltpu.VMEM((GDN_HEADS, rows, chunk), op_dt),
        ],
        compiler_params=pltpu.CompilerParams(dimension_semantics=("arbitrary", "arbitrary"),
                                             vmem_limit_bytes=VMEM_LIMIT),
        name="gdn_mixer",
    )(proj, proj, proj, conv_buf, s0, conv_w, alog, dtb, nw)


def _gla_body(q_ref, k_ref, v_ref, go_ref, sm_ref, s0_ref, wgk_ref, bgk_ref, nw_ref,
              o_ref, sout_ref, st, qes, ois, upds, decs, *, nb, tb_rows, chunk, valid, n_tb):
    tb = pl.program_id(1)
    c = chunk
    n_heads = GLA_HEADS
    units = [(s, h) for s in range(nb) for h in range(n_heads)]

    @pl.when(tb == 0)
    def _():
        st[...] = s0_ref[...]

    ii = lax.broadcasted_iota(jnp.int32, (c, c), 0)
    jj = lax.broadcasted_iota(jnp.int32, (c, c), 1)
    lower = (ii >= jj)
    lower_f = lower.astype(F32)
    rid = lax.broadcasted_iota(jnp.int32, (c, 1), 0)
    rowmask = (rid < valid) if valid < c else None
    n_sub = max(c // 16, 1)
    sub = c // n_sub

    n_chunks = tb_rows // c
    cpi = next(k for k in (GLA_CHUNKS_PER_TRIP, 2, 1) if n_chunks % k == 0)
    p1_units = [(g, h) for g in range(nb * cpi) for h in range(n_heads)]

    def phase1(ci, carry):
        rows, slots, bcs, bc_ts = [], [], [], []
        for g in range(nb * cpi):
            chunk_idx = ci * cpi + g % cpi
            rr = pl.ds(_chunk_rows(g // cpi, tb_rows, chunk_idx, c), c)
            slots.append((g // cpi) * n_chunks + chunk_idx)
            gk = jax.nn.log_sigmoid(_dot(sm_ref[rr, :], wgk_ref[...]) + bgk_ref[...]) / GLA_GATE_NORMALIZER
            if rowmask is not None:
                gk = jnp.where(rowmask, gk, 0.0)
            bc = _dot_hi(lower_f, gk)
            rows.append(rr)
            bcs.append(bc)
            bc_ts.append(bc.T)
        q, k, v, bch = {}, {}, {}, {}
        for (s, h) in p1_units:
            ks = slice(h * GLA_DK, (h + 1) * GLA_DK)
            vs = slice(h * GLA_DV, (h + 1) * GLA_DV)
            q[s, h] = q_ref[rows[s], ks] * (GLA_DK ** -0.5)
            kk = k_ref[rows[s], ks]
            vv = v_ref[rows[s], vs]
            if rowmask is not None:
                kk = jnp.where(rowmask, kk, 0.0)
                vv = jnp.where(rowmask, vv, 0.0)
            k[s, h], v[s, h] = kk, vv
            bch[s, h] = bcs[s][:, ks]
        for (g, h) in p1_units:
            qes[h, rows[g], :] = q[g, h] * jnp.exp(bch[g, h])
        a = {}
        for u in p1_units:
            q_parts, k_parts = [], []
            for sb in range(n_sub):
                ref_row = bch[u][sb * sub:sb * sub + 1, :]
                in_blk = (rid >= sb * sub) & (rid < (sb + 1) * sub)
                q_parts.append(jnp.where(in_blk, q[u] * jnp.exp(jnp.where(in_blk, bch[u] - ref_row, 0.0)), 0.0))
                k_parts.append(k[u] * jnp.exp(jnp.where(rid < (sb + 1) * sub, ref_row - bch[u], 0.0)))
            q_hat = jnp.concatenate(q_parts, axis=1) if n_sub > 1 else q_parts[0]
            k_hat = jnp.concatenate(k_parts, axis=1) if n_sub > 1 else k_parts[0]
            a[u] = jnp.where(lower, _dot_nt(q_hat, k_hat), 0.0)
        upd = {u: _dot_tn(k[u] * jnp.exp(bch[u][c - 1:c, :] - bch[u]), v[u]) for u in p1_units}
        o_intra = {u: _dot(a[u], v[u]) for u in p1_units}
        for (g, h) in p1_units:
            dec_col = bc_ts[g][h * GLA_DK:(h + 1) * GLA_DK, c - 1:c]
            decs[slots[g], h] = jnp.broadcast_to(jnp.exp(dec_col), (GLA_DK, GLA_DV))
            upds[slots[g], h] = upd[g, h]
            ois[rows[g], h * GLA_DV:(h + 1) * GLA_DV] = o_intra[g, h]
        return carry

    cp2 = next(k for k in (SCAN_CHUNKS_PER_TRIP, 1) if n_chunks % k == 0)

    def phase2(ti, carry):
        for u in range(cp2):
            scan_chunk(ti * cp2 + u)
        return carry

    def scan_chunk(ci):
        rows = [pl.ds(_chunk_rows(s, tb_rows, ci, c), c) for s in range(nb)]
        o = {(s, h): ois[rows[s], h * GLA_DV:(h + 1) * GLA_DV] + _dot(qes[h, rows[s], :], st[s, h])
             for (s, h) in units}
        for (s, h) in units:
            st[s, h] = decs[s * n_chunks + ci, h] * st[s, h] + upds[s * n_chunks + ci, h]
        for s in range(nb):
            o_ref[rows[s], :] = jnp.concatenate(
                [_gated_norm(o[s, h], nw_ref[...], go_ref[rows[s], h * GLA_DV:(h + 1) * GLA_DV])
                 for h in range(n_heads)], axis=1)

    _for_chunks(n_chunks // cpi, phase1)
    _for_chunks(n_chunks // cp2, phase2)

    @pl.when(tb == n_tb - 1)
    def _():
        sout_ref[...] = st[...]


def _gla(proj, n_seq, nb, t_len, tb_rows, chunk, valid, s0, wgk, bgk, nw):
    n_tb = t_len // tb_rows
    assert nb == 1 or n_tb == 1
    rows = nb * tb_rows

    def rowblk(b, t):
        return b * n_tb + t

    body = functools.partial(_gla_body, nb=nb, tb_rows=tb_rows, chunk=chunk, valid=valid, n_tb=n_tb)
    return pl.pallas_call(
        body,
        grid=(n_seq // nb, n_tb),
        in_specs=[
            pl.BlockSpec((rows, GLA_QK_W), lambda b, t: (rowblk(b, t), COL_GQ // GLA_QK_W)),
            pl.BlockSpec((rows, GLA_QK_W), lambda b, t: (rowblk(b, t), COL_GK // GLA_QK_W)),
            pl.BlockSpec((rows, GLA_V_W), lambda b, t: (rowblk(b, t), COL_GV // GLA_V_W)),
            pl.BlockSpec((rows, GLA_V_W), lambda b, t: (rowblk(b, t), COL_GG // GLA_V_W)),
            pl.BlockSpec((rows, SM_W), lambda b, t: (rowblk(b, t), COL_SM // SM_W)),
            pl.BlockSpec((nb, GLA_HEADS, GLA_DK, GLA_DV), lambda b, t: (b, 0, 0, 0)),
            pl.BlockSpec((SM_W, GLA_QK_W), lambda b, t: (0, 0)),
            pl.BlockSpec((1, GLA_QK_W), lambda b, t: (0, 0)),
            pl.BlockSpec((1, GLA_DV), lambda b, t: (0, 0)),
        ],
        out_specs=[
            pl.BlockSpec((rows, GLA_V_W), lambda b, t: (rowblk(b, t), 0)),
            pl.BlockSpec((nb, GLA_HEADS, GLA_DK, GLA_DV), lambda b, t: (b, 0, 0, 0)),
        ],
        out_shape=[
            jax.ShapeDtypeStruct((n_seq * t_len, GLA_V_W), F32),
            jax.ShapeDtypeStruct((n_seq, GLA_HEADS, GLA_DK, GLA_DV), F32),
        ],
        scratch_shapes=[
            pltpu.VMEM((nb, GLA_HEADS, GLA_DK, GLA_DV), F32),
            pltpu.VMEM((GLA_HEADS, rows, GLA_DK), F32),
            pltpu.VMEM((rows, GLA_V_W), F32),
            pltpu.VMEM((rows // chunk, GLA_HEADS, GLA_DK, GLA_DV), F32),
            pltpu.VMEM((rows // chunk, GLA_HEADS, GLA_DK, GLA_DV), F32),
        ],
        compiler_params=pltpu.CompilerParams(dimension_semantics=("arbitrary", "arbitrary"),
                                             vmem_limit_bytes=VMEM_LIMIT),
        name="gla_mixer",
    )(proj, proj, proj, proj, proj, s0, wgk, bgk, nw)


def _outproj_body(ogp_ref, ogs_ref, olp_ref, ols_ref, xp_ref, xs_ref, wo_ref, g_ref, wr_ref, br_ref,
                  x1_ref, h2_ref, rt_ref, rtt_ref, cnt_ref, base, *, n_p_blocks):
    i = pl.program_id(0)

    @pl.when(i == 0)
    def _():
        base[...] = jnp.zeros_like(base)

    o = jnp.concatenate([_group_pick(i, n_p_blocks, ogp_ref, ogs_ref),
                         _group_pick(i, n_p_blocks, olp_ref, ols_ref)], axis=1)
    x1 = _group_pick(i, n_p_blocks, xp_ref, xs_ref) + jnp.dot(o.astype(BF16), wo_ref[...],
                                                               preferred_element_type=F32)
    x1_ref[...] = x1
    h = _rms(x1, g_ref[...])
    _store_token_tiles(h2_ref, _pack_bf16_pairs(h))
    logits = _dot_3pass(h, wr_ref[...]) + br_ref[...]

    tm = logits.shape[0]
    lt = logits.T[:N_EXPERTS]
    eid = lax.broadcasted_iota(jnp.int32, (N_EXPERTS, tm), 0)
    work = lt
    sel = jnp.zeros((N_EXPERTS, tm), F32)
    hits, ids, vals = [], [], []
    for _ in range(TOP_K):
        m = jnp.max(work, axis=0, keepdims=True)
        idx = jnp.min(jnp.where(work == m, eid, N_EXPERTS), axis=0, keepdims=True)
        hit = eid == idx
        hits.append(hit)
        ids.append(idx)
        vals.append(m)
        work = jnp.where(hit, -jnp.inf, work)
        sel = sel + hit.astype(F32)
    exps = [jnp.exp(v - vals[0]) for v in vals]
    den = exps[0]
    for e in exps[1:]:
        den = den + e
    gates = [e / den for e in exps]

    ri = lax.broadcasted_iota(jnp.int32, (tm, tm), 0)
    ci = lax.broadcasted_iota(jnp.int32, (tm, tm), 1)
    before = _dot(sel, (ri < ci).astype(F32)) + base[...]
    ranks = [jnp.sum(jnp.where(hit, before, 0.0), axis=0, keepdims=True) for hit in hits]
    base[...] = base[...] + jnp.sum(sel, axis=1, keepdims=True)
    cnt_ref[...] = base[...]

    row = lax.broadcasted_iota(jnp.int32, (LANE, tm), 0)
    rec = jnp.zeros((LANE, tm), F32)
    for k in range(TOP_K):
        rec = jnp.where(row == k, ids[k].astype(F32), rec)
        rec = jnp.where(row == TOP_K + k, ranks[k], rec)
        rec = jnp.where(row == 2 * TOP_K + k, gates[k], rec)
    rt_ref[...] = rec.T
    rtt_ref[...] = rec[:2 * TOP_K]


def _outproj(og_p, og_s, ol_p, ol_s, x_p, x_s, wo, g, wr, br):
    n_p_blocks, n_s_blocks = x_p.shape[0] // ROW_TILE, x_s.shape[0] // ROW_TILE
    n = x_p.shape[0] + x_s.shape[0]
    return pl.pallas_call(
        functools.partial(_outproj_body, n_p_blocks=n_p_blocks),
        grid=(n_p_blocks + n_s_blocks,),
        in_specs=_group_specs(ROW_TILE, GDN_V_W, n_p_blocks) + _group_specs(ROW_TILE, GLA_V_W, n_p_blocks)
        + _group_specs(ROW_TILE, D_MODEL, n_p_blocks) + [
            pl.BlockSpec((D_MODEL, D_MODEL), lambda i: (0, 0)),
            pl.BlockSpec((1, D_MODEL), lambda i: (0, 0)),
            pl.BlockSpec((D_MODEL, LANE), lambda i: (0, 0)),
            pl.BlockSpec((1, LANE), lambda i: (0, 0)),
        ],
        out_specs=[
            pl.BlockSpec((ROW_TILE, D_MODEL), lambda i: (i, 0)),
            pl.BlockSpec((ROW_TILE * PACK_TILES, LANE), lambda i: (i, 0)),
            pl.BlockSpec((ROW_TILE, LANE), lambda i: (i, 0)),
            pl.BlockSpec((2 * TOP_K, ROW_TILE), lambda i: (0, i)),
            pl.BlockSpec((N_EXPERTS, 1), lambda i: (0, 0)),
        ],
        out_shape=[
            jax.ShapeDtypeStruct((n, D_MODEL), F32),
            jax.ShapeDtypeStruct((n * PACK_TILES, LANE), jnp.uint32),
            jax.ShapeDtypeStruct((n, LANE), F32),
            jax.ShapeDtypeStruct((2 * TOP_K, n), F32),
            jax.ShapeDtypeStruct((N_EXPERTS, 1), F32),
        ],
        scratch_shapes=[pltpu.VMEM((N_EXPERTS, 1), F32)],
        compiler_params=pltpu.CompilerParams(dimension_semantics=("arbitrary",),
                                             vmem_limit_bytes=VMEM_LIMIT),
        name="out_proj",
    )(og_p, og_s, ol_p, ol_s, x_p, x_s, wo, g, wr, br)


def _store_token_tiles(ref2d, val):
    rows, tiles = val.shape[0], val.shape[1] // LANE
    for c in range(tiles):
        ref2d[pl.ds(c, rows, stride=tiles), :] = val[:, c * LANE:(c + 1) * LANE]


def _load_token_tiles(ref2d, first_row, rows, tiles=TOK_TILES):
    return jnp.concatenate(
        [ref2d[pl.ds(first_row * tiles + c, rows, stride=tiles), :] for c in range(tiles)], axis=1)


def _pack_bf16_pairs(x):
    half = x.shape[1] // 2
    bits = lax.bitcast_convert_type(x.astype(BF16).astype(F32), jnp.uint32)
    return (bits[:, :half] >> 16) | (bits[:, half:] & jnp.uint32(0xFFFF0000))


def _unpack_bf16_pairs(w):
    lo = lax.bitcast_convert_type(w << 16, F32)
    hi = lax.bitcast_convert_type(w & jnp.uint32(0xFFFF0000), F32)
    return jnp.concatenate([lo, hi], axis=1).astype(BF16)


def _expert_weight_copies(e, ws, wup_hbm, wdn_hbm, wup_buf, wdn_buf, wsems):
    return (pltpu.make_async_copy(wup_hbm.at[e], wup_buf.at[ws], wsems.at[ws]),
            pltpu.make_async_copy(wdn_hbm.at[e], wdn_buf.at[ws], wsems.at[ws]))


def _expert_body(be_ref, nu_ref, first_ref, wslot_ref, next_ref, valid_ref, x_ref,
                 wup_hbm, bup_ref, wdn_hbm, bdn_ref, y_ref, wup_buf, wdn_buf, wsems, wup_bf, wdn_bf):
    i = pl.program_id(0)
    n_used = nu_ref[0]
    ws = wslot_ref[i]
    weight_copies = functools.partial(_expert_weight_copies, wup_hbm=wup_hbm, wdn_hbm=wdn_hbm, wup_buf=wup_buf,
                                      wdn_buf=wdn_buf, wsems=wsems)

    @pl.when((i == 0) & (n_used > 0))
    def _():
        for cp in weight_copies(be_ref[0], ws):
            cp.start(priority=EXPERT_WEIGHT_QUEUE)

    @pl.when(i < n_used)
    def _():
        @pl.when(first_ref[i] == 1)
        def _():
            for cp in weight_copies(be_ref[i], ws):
                cp.wait()

            @pl.when(next_ref[i] >= 0)
            def _():
                for cp in weight_copies(next_ref[i], 1 - ws):
                    cp.start(priority=EXPERT_WEIGHT_QUEUE)

            for r in range(0, D_MODEL, WEIGHT_CAST_ROWS):
                wup_bf[r:r + WEIGHT_CAST_ROWS, :] = wup_buf[ws, r:r + WEIGHT_CAST_ROWS, :].astype(BF16)
            for r in range(0, D_FF, WEIGHT_CAST_ROWS):
                wdn_bf[r:r + WEIGHT_CAST_ROWS, :] = wdn_buf[ws, r:r + WEIGHT_CAST_ROWS, :].astype(BF16)

        def expert_rows(n):
            x = _unpack_bf16_pairs(_load_token_tiles(x_ref, 0, n, PACK_TILES))
            gu = _dot(x, wup_bf[...]) + bup_ref[...]
            gate = jnp.minimum(gu[:, :D_FF], SWIGLU_LIMIT)
            up = jnp.clip(gu[:, D_FF:], -SWIGLU_LIMIT, SWIGLU_LIMIT)
            a = (up + 1.0) * gate * jax.nn.sigmoid(SWIGLU_ALPHA * gate)
            _store_token_tiles(y_ref, _dot(a, wdn_bf[...]) + bdn_ref[...])

        valid = valid_ref[i]
        for n in range(EXPERT_ROW_STEP, EXPERT_ROWS + 1, EXPERT_ROW_STEP):
            @pl.when((valid <= n) if n == EXPERT_ROW_STEP else ((valid > n - EXPERT_ROW_STEP) & (valid <= n)))
            def _(n=n):
                expert_rows(n)
                if n < EXPERT_ROWS:
                    y_ref[n * TOK_TILES:, :] = jnp.zeros(((EXPERT_ROWS - n) * TOK_TILES, LANE), F32)

    @pl.when(i >= n_used)
    def _():
        y_ref[...] = jnp.zeros_like(y_ref)


def _experts(block_meta, xs_2d, w_up, b_up, w_down, b_down):
    n_blocks = block_meta[0].shape[0]
    grid_spec = pltpu.PrefetchScalarGridSpec(
        num_scalar_prefetch=len(block_meta),
        grid=(n_blocks,),
        in_specs=[
            pl.BlockSpec((EXPERT_ROWS * PACK_TILES, LANE), lambda i, *_: (i, 0)),
            pl.BlockSpec(memory_space=pl.ANY),
            pl.BlockSpec((None, 1, 2 * D_FF), lambda i, be, *_: (be[i], 0, 0)),
            pl.BlockSpec(memory_space=pl.ANY),
            pl.BlockSpec((None, 1, D_MODEL), lambda i, be, *_: (be[i], 0, 0)),
        ],
        out_specs=pl.BlockSpec((EXPERT_ROWS * TOK_TILES, LANE), lambda i, *_: (i, 0)),
        scratch_shapes=[
            pltpu.VMEM((2, D_MODEL, 2 * D_FF), F32),
            pltpu.VMEM((2, D_FF, D_MODEL), F32),
            pltpu.SemaphoreType.DMA((2,)),
            pltpu.VMEM((D_MODEL, 2 * D_FF), BF16),
            pltpu.VMEM((D_FF, D_MODEL), BF16),
        ],
    )
    return pl.pallas_call(
        _expert_body,
        grid_spec=grid_spec,
        out_shape=jax.ShapeDtypeStruct((n_blocks * EXPERT_ROWS * TOK_TILES, LANE), F32),
        compiler_params=pltpu.CompilerParams(dimension_semantics=("arbitrary",),
                                             vmem_limit_bytes=VMEM_LIMIT),
        name="experts",
    )(*block_meta, xs_2d, w_up, b_up.reshape(N_EXPERTS, 1, 2 * D_FF), w_down,
      b_down.reshape(N_EXPERTS, 1, D_MODEL))


def _dispatch(h_tiles, dest_kmajor, n_rows):
    n_tok = h_tiles.shape[0]
    info = plsc.get_sparse_core_info()
    n_workers = info.num_cores * info.num_subcores
    per_worker = n_tok // n_workers
    chunk = next(c for c in (128, 96, 88, 64, 48, 32, 16, 8) if per_worker % c == 0)
    assert n_tok % n_workers == 0 and per_worker % SUBLANE == 0
    mesh = plsc.VectorSubcoreMesh(core_axis_name="c", subcore_axis_name="s")

    @functools.partial(
        pl.kernel, mesh=mesh,
        out_type=jax.ShapeDtypeStruct((n_rows,) + h_tiles.shape[1:], h_tiles.dtype),
        scratch_types=[pltpu.VMEM((2, TOP_K, chunk), jnp.int32),
                       pltpu.VMEM((2, chunk) + h_tiles.shape[1:], h_tiles.dtype),
                       pltpu.SemaphoreType.DMA, pltpu.SemaphoreType.DMA],
    )
    def dispatch(h_hbm, dest_hbm, out_hbm, idx_v, rows_v, load_sem, store_sem):
        wid = lax.axis_index("s") * info.num_cores + lax.axis_index("c")

        def start_loads(j, slot):
            t0 = pl.multiple_of(wid * per_worker + j * chunk, SUBLANE)
            cps = [pltpu.async_copy(h_hbm.at[pl.ds(t0, chunk)], rows_v.at[slot], load_sem)]
            for k in range(TOP_K):
                cps.append(pltpu.async_copy(
                    dest_hbm.at[pl.ds(pl.multiple_of(k * n_tok + t0, SUBLANE), chunk)], idx_v.at[slot, k], load_sem))
            return cps

        loads, stores = start_loads(0, 0), []
        for j in range(per_worker // chunk):
            slot = j % 2
            for cp in loads + stores:
                cp.wait()
            loads = start_loads(j + 1, 1 - slot) if j + 1 < per_worker // chunk else []
            stores = [pltpu.async_copy(rows_v.at[slot], out_hbm.at[idx_v.at[slot, k]], store_sem)
                      for k in range(TOP_K)]
        for cp in stores:
            cp.wait()

    return dispatch(h_tiles, dest_kmajor)


def _gather_rows(src_tiles, idx_ref, n_rows, dst2d, sem, priorities):
    def issue(j, carry):
        for u in range(DMA_ISSUE_UNROLL):
            r = j * DMA_ISSUE_UNROLL + u
            dst = dst2d.at[pl.ds(pl.multiple_of(r * TOK_TILES, TOK_TILES), TOK_TILES), :]
            pltpu.make_async_copy(src_tiles.at[idx_ref[0, r]], dst, sem).start(
                priority=priorities[u % len(priorities)])
        return carry

    lax.fori_loop(0, n_rows // DMA_ISSUE_UNROLL, issue, 0)


def _wait_rows(src2d, n_rows, dst2d, sem):
    pltpu.make_async_copy(src2d.at[pl.ds(0, n_rows * TOK_TILES), :], dst2d, sem).wait()


def _combine_body(dest_ref, dest_next_ref, rt_ref, y_tiles, y_2d, x1_ref, g_ref, op_ref, os_ref, ybuf, sems,
                  *, n_p_blocks):
    i = pl.program_id(0)
    slot = i % 2
    n_rows = TOP_K * COMBINE_ROWS

    @pl.when(i == 0)
    def _():
        _gather_rows(y_tiles, dest_ref, n_rows, ybuf.at[0], sems.at[0], COMBINE_GATHER_QUEUES)

    _wait_rows(y_2d, n_rows, ybuf.at[slot], sems.at[slot])

    @pl.when(i + 1 < pl.num_programs(0))
    def _():
        _gather_rows(y_tiles, dest_next_ref, n_rows, ybuf.at[1 - slot], sems.at[1 - slot], COMBINE_GATHER_QUEUES)

    buf = ybuf.at[slot]
    moe = _load_token_tiles(buf, 0, COMBINE_ROWS) * rt_ref[:, 2 * TOP_K:2 * TOP_K + 1]
    for k in range(1, TOP_K):
        moe = moe + _load_token_tiles(buf, k * COMBINE_ROWS, COMBINE_ROWS) * rt_ref[:, 2 * TOP_K + k:2 * TOP_K + k + 1]
    res = _rms(x1_ref[...] + moe, g_ref[...])

    @pl.when(i < n_p_blocks)
    def _():
        op_ref[...] = res

    @pl.when(i >= n_p_blocks)
    def _():
        os_ref[...] = res


def _combine(dest_b, rt, y_2d, x1, g, n_p):
    n = x1.shape[0]
    n_blk = n // COMBINE_ROWS
    n_p_blocks = n_p // COMBINE_ROWS
    dest_blocks = dest_b.reshape(n_blk, 1, TOP_K * COMBINE_ROWS)
    return pl.pallas_call(
        functools.partial(_combine_body, n_p_blocks=n_p_blocks),
        grid=(n_blk,),
        in_specs=[
            pl.BlockSpec((None, 1, COMBINE_ROWS * TOP_K), lambda i: (i, 0, 0), memory_space=pltpu.SMEM),
            pl.BlockSpec((None, 1, COMBINE_ROWS * TOP_K), lambda i: (jnp.minimum(i + 1, n_blk - 1), 0, 0),
                         memory_space=pltpu.SMEM),
            pl.BlockSpec((COMBINE_ROWS, LANE), lambda i: (i, 0)),
            pl.BlockSpec(memory_space=pl.ANY),
            pl.BlockSpec(memory_space=pl.ANY),
            pl.BlockSpec((COMBINE_ROWS, D_MODEL), lambda i: (i, 0)),
            pl.BlockSpec((1, D_MODEL), lambda i: (0, 0)),
        ],
        out_specs=_group_specs(COMBINE_ROWS, D_MODEL, n_p_blocks),
        out_shape=[jax.ShapeDtypeStruct((n_p, D_MODEL), F32), jax.ShapeDtypeStruct((n - n_p, D_MODEL), F32)],
        scratch_shapes=[pltpu.VMEM((2, TOP_K * COMBINE_ROWS * TOK_TILES, LANE), F32),
                        pltpu.SemaphoreType.DMA((2,))],
        compiler_params=pltpu.CompilerParams(dimension_semantics=("arbitrary",),
                                             vmem_limit_bytes=VMEM_LIMIT),
        name="combine",
    )(dest_blocks, dest_blocks, rt, y_2d.reshape(-1, TOK_TILES, LANE), y_2d, x1, g)


def _plan_body(rtt_ref, cnt_ref, dk_ref, db_ref, meta_ref, pst):
    i = pl.program_id(0)
    sh = _log2(EXPERT_ROWS)
    n_e = N_EXPERTS

    @pl.when(i == 0)
    def _():
        cnt = cnt_ref[...].astype(jnp.int32)
        padded = (((cnt + (EXPERT_ROWS - 1)) >> sh) << sh).astype(F32)
        e_r = lax.broadcasted_iota(jnp.int32, (n_e, n_e), 0)
        e_c = lax.broadcasted_iota(jnp.int32, (n_e, n_e), 1)
        p_t = jnp.broadcast_to(padded, (n_e, n_e)).T
        pend = jnp.sum(jnp.where(e_c <= e_r, p_t, 0.0), axis=1, keepdims=True)
        pst[...] = pend - padded
        has_rows = p_t > 0.0
        group = jnp.sum(jnp.where((e_c <= e_r) & has_rows, 1.0, 0.0), axis=1, keepdims=True) - 1.0
        nxt = jnp.min(jnp.where((e_c > e_r) & has_rows, e_c, n_e), axis=1, keepdims=True)
        nxt = jnp.where(nxt >= n_e, -1, nxt)

        mb = meta_ref.shape[1]
        blk = lax.broadcasted_iota(jnp.int32, (n_e, mb), 1)
        eb = lax.broadcasted_iota(jnp.int32, (n_e, mb), 0)
        first_row = (blk * EXPERT_ROWS).astype(F32)

        def expert_of(row0):
            return jnp.minimum(jnp.sum(jnp.where(pend <= row0, 1, 0), axis=0, keepdims=True), n_e - 1)

        be = expert_of(first_row)
        be_prev = expert_of(first_row - EXPERT_ROWS)
        hit = eb == be
        wslot = jnp.sum(jnp.where(hit, group, 0.0), axis=0, keepdims=True).astype(jnp.int32) & 1
        nx = jnp.sum(jnp.where(hit, nxt, 0), axis=0, keepdims=True)
        n_used = pend[n_e - 1:n_e, :].astype(jnp.int32) >> sh
        lane = lax.broadcasted_iota(jnp.int32, (1, mb), 1)
        first = (((be != be_prev) | (lane == 0)) & (lane < n_used)).astype(jnp.int32)
        cnt_b = jnp.sum(jnp.where(hit, cnt_ref[...], 0.0), axis=0, keepdims=True)
        pst_b = jnp.sum(jnp.where(hit, pend - padded, 0.0), axis=0, keepdims=True)
        valid = jnp.clip(cnt_b - (first_row[0:1, :] - pst_b), 0.0, float(EXPERT_ROWS)).astype(jnp.int32)
        row8 = lax.broadcasted_iota(jnp.int32, (SUBLANE, mb), 0)
        meta = jnp.where(row8 == 0, be, jnp.where(row8 == 1, first, jnp.where(row8 == 2, wslot,
                         jnp.where(row8 == 3, nx, jnp.where(row8 == 4, n_used, valid)))))
        meta_ref[...] = meta

    tm = rtt_ref.shape[1]
    eid = lax.broadcasted_iota(jnp.int32, (n_e, tm), 0).astype(F32)
    row8 = lax.broadcasted_iota(jnp.int32, (SUBLANE, tm), 0)
    d8 = jnp.zeros((SUBLANE, tm), jnp.int32)
    for k in range(TOP_K):
        start = jnp.sum(jnp.where(eid == rtt_ref[k:k + 1, :], pst[...], 0.0), axis=0, keepdims=True)
        d8 = jnp.where(row8 == k, (start + rtt_ref[TOP_K + k:TOP_K + k + 1, :]).astype(jnp.int32), d8)
    dk_ref[...] = d8[:TOP_K]
    for b in range(tm // COMBINE_ROWS):
        db_ref[b] = d8[:TOP_K, b * COMBINE_ROWS:(b + 1) * COMBINE_ROWS]


def _plan(rtt, cnt):
    n = rtt.shape[1]
    n_rows = n * TOP_K + N_EXPERTS * EXPERT_ROWS
    n_blocks = n_rows // EXPERT_ROWS
    mb = -(-n_blocks // LANE) * LANE
    tile = max(t for t in range(COMBINE_ROWS, PLAN_TILE_MAX + 1, COMBINE_ROWS) if n % t == 0)
    dk, db, meta = pl.pallas_call(
        _plan_body,
        grid=(n // tile,),
        in_specs=[pl.BlockSpec((2 * TOP_K, tile), lambda i: (0, i)),
                  pl.BlockSpec((N_EXPERTS, 1), lambda i: (0, 0))],
        out_specs=[pl.BlockSpec((TOP_K, tile), lambda i: (0, i)),
                   pl.BlockSpec((tile // COMBINE_ROWS, TOP_K, COMBINE_ROWS), lambda i: (i, 0, 0)),
                   pl.BlockSpec((SUBLANE, mb), lambda i: (0, 0))],
        out_shape=[jax.ShapeDtypeStruct((TOP_K, n), jnp.int32),
                   jax.ShapeDtypeStruct((n // COMBINE_ROWS, TOP_K, COMBINE_ROWS), jnp.int32),
                   jax.ShapeDtypeStruct((SUBLANE, mb), jnp.int32)],
        scratch_shapes=[pltpu.VMEM((N_EXPERTS, 1), F32)],
        compiler_params=pltpu.CompilerParams(dimension_semantics=("arbitrary",)),
        name="plan",
    )(rtt, cnt)
    block_meta = (meta[0, :n_blocks], meta[4, 0:1], meta[1, :n_blocks], meta[2, :n_blocks], meta[3, :n_blocks],
                  meta[5, :n_blocks])
    return dk, db, n_rows, block_meta


def _pad_lanes(v, width):
    return jnp.zeros((1, width), F32).at[0, :v.shape[0]].set(v.astype(F32))


def kernel(x_prompt, x_sample, state_gdn_conv, state_gdn, state_gla, rms_mix_w, w_in, conv_w, gdn_a_log,
           gdn_dt_bias, gdn_norm_w, gla_gk_w, gla_gk_b, gla_norm_w, w_out, rms_ffn_w, w_router, b_router,
           w_up, b_up, w_down, b_down, rms_final_w):
    bp, tp, d = x_prompt.shape
    bs, ts, _ = x_sample.shape
    n_p, n_s = bp * tp, bs * ts
    assert d == D_MODEL and state_gdn.shape[0] == 1, "single-layer kernel"
    assert tp >= CONV_WIDTH - 1 and ts >= CONV_WIDTH - 1, "new conv state is taken from the new tokens only"
    l = 0

    wi = w_in[l]
    a0 = GDN_CONV_CH + GDN_V_W
    g0 = a0 + 2 * GDN_HEADS
    lr0 = g0 + 2 * GLA_QK_W + 2 * GLA_V_W
    small = jnp.concatenate([wi[:, a0:a0 + 2 * GDN_HEADS], wi[:, lr0:lr0 + GLA_GATE_RANK],
                             jnp.zeros((d, SM_W - 2 * GDN_HEADS - GLA_GATE_RANK), F32)], axis=1)
    w_big = jnp.concatenate([wi[:, :a0], wi[:, g0:lr0], small], axis=1).astype(BF16)
    alog = _pad_lanes(gdn_a_log[l], SM_W)
    dtb = _pad_lanes(gdn_dt_bias[l], SM_W)
    wgk = jnp.zeros((SM_W, GLA_QK_W), F32).at[SM_LR:SM_LR + GLA_GATE_RANK].set(gla_gk_w[l])
    wr = jnp.zeros((d, LANE), F32).at[:, :N_EXPERTS].set(w_router[l])
    br = jnp.full((1, LANE), -1e30, F32).at[0, :N_EXPERTS].set(b_router[l])

    assert n_p % ROW_TILE == 0 and n_s % ROW_TILE == 0
    x_p, x_s = x_prompt.reshape(n_p, d), x_sample.reshape(n_s, d)
    proj = _inproj(x_p, x_s, rms_mix_w[l][None, :], w_big)

    tb_p = PROMPT_TIME_BLOCK
    zeros_conv = jnp.zeros((bp, CONV_WIDTH - 1, GDN_CONV_CH), F32)
    og_p, gdn_p, conv_p = _gdn(proj, bp, 1, tp, tb_p, CHUNK, CHUNK, zeros_conv,
                               jnp.zeros((bp, GDN_HEADS, GDN_DK, GDN_DV), F32), conv_w[l], alog, dtb,
                               gdn_norm_w[l][None, :])
    ol_p, gla_p = _gla(proj, bp, 1, tp, tb_p, CHUNK, CHUNK, jnp.zeros((bp, GLA_HEADS, GLA_DK, GLA_DV), F32),
                       wgk, gla_gk_b[l][None, :], gla_norm_w[l][None, :])

    ts_pad = SUBLANE
    nb_s = SAMPLE_SEQS_PER_STEP
    proj_s = proj[n_p:].reshape(bs, ts, PROJ_W)
    proj_sp = jnp.pad(proj_s, ((0, 0), (0, ts_pad - ts), (0, 0))).reshape(bs * ts_pad, PROJ_W)
    og_s, gdn_s, conv_s = _gdn(proj_sp, bs, nb_s, ts_pad, ts_pad, ts_pad, ts, state_gdn_conv[l], state_gdn[l],
                               conv_w[l], alog, dtb, gdn_norm_w[l][None, :])
    ol_s, gla_s = _gla(proj_sp, bs, nb_s, ts_pad, ts_pad, ts_pad, ts, state_gla[l], wgk, gla_gk_b[l][None, :],
                       gla_norm_w[l][None, :])
    og_s = og_s.reshape(bs, ts_pad, GDN_V_W)[:, :ts].reshape(n_s, GDN_V_W)
    ol_s = ol_s.reshape(bs, ts_pad, GLA_V_W)[:, :ts].reshape(n_s, GLA_V_W)

    x1, h2, rt, rtt, cnt = _outproj(og_p, og_s, ol_p, ol_s, x_p, x_s, w_out[l].astype(BF16),
                                    rms_ffn_w[l][None, :], wr, br)

    dest_k, dest_b, n_rows, block_meta = _plan(rtt, cnt)
    xs = _dispatch(h2.reshape(-1, PACK_TILES, LANE), dest_k.reshape(-1), n_rows)
    y_rows = _experts(block_meta, xs.reshape(-1, LANE), w_up[l], b_up[l], w_down[l], b_down[l])
    y_p, y_s = _combine(dest_b, rt, y_rows, x1, rms_final_w[None, :], n_p)
    y_prompt = y_p.reshape(bp, tp, d)
    y_sample = y_s.reshape(bs, ts, d)
    return (y_prompt, y_sample, conv_p[None], gdn_p[None], gla_p[None], conv_s[None], gdn_s[None], gla_s[None])
```
